```python
import jax, jax.numpy as jnp
from jax import lax
import numpy as np

D_MODEL = 2048
BATCH = 8
SEQ = 4096
DEPTH = 1

CHUNK = 64
A_HEADS = 16
A_HEAD_DIM = 64
A_WIDTH = A_HEADS * A_HEAD_DIM
A_PAST_CHUNKS = 8
A_BAND = (A_PAST_CHUNKS + 1) * CHUNK
REL_CLIP = 256
REL_SIZE = REL_CLIP + CHUNK
B_HEADS = 4
B_KEY_DIM = D_MODEL // 4
B_VAL_DIM = D_MODEL // 2
B_HK = B_KEY_DIM // B_HEADS
B_HV = B_VAL_DIM // B_HEADS
GATE_RANK = 16
GATE_TAU = 16.0
D_FF = 256 * ((8 * D_MODEL // 3 + 255) // 256)
N_MOD = 9
ALPHA = (2.0 * DEPTH) ** 0.25
BETA = (8.0 * DEPTH) ** -0.25
LN_EPS = 1e-5
RMS_EPS = 1e-6
SPLITS = (A_WIDTH, A_WIDTH, A_WIDTH,
          B_KEY_DIM, B_KEY_DIM, B_VAL_DIM,
          B_VAL_DIM, GATE_RANK,
          D_MODEL, D_MODEL)
SPLIT_POINTS = tuple(int(v) for v in np.cumsum(SPLITS)[:-1])
W_IN_COLS = sum(SPLITS)

kernel_name = "hybrid_chunk_attn_gla_macaron_deepnorm_adaln"


def layer_norm(x, g, b):
    xf = x.astype(jnp.float32)
    mu = jnp.mean(xf, axis=-1, keepdims=True)
    var = jnp.mean(jnp.square(xf - mu), axis=-1, keepdims=True)
    y = (xf - mu) * lax.rsqrt(var + LN_EPS)
    return (y * g.astype(jnp.float32) + b.astype(jnp.float32)).astype(x.dtype)


def modulate(x, shift, scale):
    return x * (1.0 + scale[:, None, :]) + shift[:, None, :]


def swiglu(u, w_in, w_out):
    a, b = jnp.split(u @ w_in, 2, axis=-1)
    return (jax.nn.silu(a) * b) @ w_out


def chunk_band_attention(q, k, v, rel_bias):
    bn, s, h, dh = q.shape
    nc = s // CHUNK
    pad = A_PAST_CHUNKS * CHUNK
    kp = jnp.pad(k, ((0, 0), (pad, 0), (0, 0), (0, 0)))
    vp = jnp.pad(v, ((0, 0), (pad, 0), (0, 0), (0, 0)))
    qi = jnp.arange(CHUNK)[:, None]
    ks = jnp.arange(A_BAND)[None, :]
    rel = ks - pad - qi
    idx = jnp.clip(rel, -REL_CLIP, CHUNK - 1) + REL_CLIP
    bias = rel_bias[:, idx].astype(jnp.float32)
    qc = q.reshape(bn, nc, CHUNK, h, dh).swapaxes(0, 1)
    scale = dh ** -0.5

    def one_chunk(args):
        n, qn = args
        start = n * CHUNK
        kb = lax.dynamic_slice_in_dim(kp, start, A_BAND, axis=1)
        vb = lax.dynamic_slice_in_dim(vp, start, A_BAND, axis=1)
        sc = jnp.einsum('bqhd,bkhd->bhqk', qn, kb).astype(jnp.float32) * scale + bias
        valid = (start - pad + jnp.arange(A_BAND)) >= 0
        sc = jnp.where(valid, sc, -jnp.inf)
        p = jax.nn.softmax(sc, axis=-1).astype(vb.dtype)
        return jnp.einsum('bhqk,bkhd->bqhd', p, vb)

    out = lax.map(one_chunk, (jnp.arange(nc), qc))
    return out.swapaxes(0, 1).reshape(bn, s, h * dh)


def gla_chunk_readout(q, k, v, log_a):
    bn, s, h, dk = q.shape
    dv = v.shape[-1]
    nc = s // CHUNK
    qc = q.reshape(bn, nc, CHUNK, h, dk).astype(jnp.float32)
    kc = k.reshape(bn, nc, CHUNK, h, dk).astype(jnp.float32)
    vc = v.reshape(bn, nc, CHUNK, h, dv).astype(jnp.float32)
    cum = jnp.cumsum(log_a.reshape(bn, nc, CHUNK, h, dk).astype(jnp.float32), axis=2)
    last = cum[:, :, -1:]
    kdec = kc * jnp.exp(last - cum)
    u = jnp.einsum('bnchk,bnchv->bnhkv', kdec, vc)
    chunk_decay = jnp.exp(last[:, :, 0])

    def step(state, xs):
        dec, un, qn = xs
        state = dec[..., None] * state + un
        return state, jnp.einsum('bchk,bhkv->bchv', qn, state)

    s0 = jnp.zeros((bn, h, dk, dv), jnp.float32)
    _, o = lax.scan(step, s0, (chunk_decay.swapaxes(0, 1), u.swapaxes(0, 1), qc.swapaxes(0, 1)))
    return o.swapaxes(0, 1).reshape(bn, s, h, dv)


def token_mix(u, w_mix_in, rel_bias, w_alpha2, b_alpha, gla_norm_g, w_proj_a, w_proj_b, w_mix_out):
    bn, s, _ = u.shape
    qa, ka, va, qb, kb, vb, rb, lr, ga, gb = jnp.split(u @ w_mix_in, SPLIT_POINTS, axis=-1)
    ya = chunk_band_attention(qa.reshape(bn, s, A_HEADS, A_HEAD_DIM),
                              ka.reshape(bn, s, A_HEADS, A_HEAD_DIM),
                              va.reshape(bn, s, A_HEADS, A_HEAD_DIM), rel_bias)
    log_a = jax.nn.log_sigmoid((lr @ w_alpha2 + b_alpha).astype(jnp.float32)) / GATE_TAU
    ob = gla_chunk_readout(qb.reshape(bn, s, B_HEADS, B_HK) * (B_HK ** -0.5),
                           kb.reshape(bn, s, B_HEADS, B_HK),
                           vb.reshape(bn, s, B_HEADS, B_HV),
                           log_a.reshape(bn, s, B_HEADS, B_HK))
    ob = ob * lax.rsqrt(jnp.mean(jnp.square(ob), axis=-1, keepdims=True) + RMS_EPS)
    ob = (ob * gla_norm_g.astype(jnp.float32)).astype(u.dtype).reshape(bn, s, B_VAL_DIM)
    yb = ob * jax.nn.silu(rb)
    merged = jax.nn.sigmoid(ga) * (ya @ w_proj_a) + jax.nn.sigmoid(gb) * (yb @ w_proj_b)
    return merged @ w_mix_out


def _fwd_setup_inputs(seed: int = 0) -> dict:
    key = jax.random.key(seed)
    ks = jax.random.split(key, 24)
    f32 = jnp.float32
    nrm = lambda k, shape, s: jax.random.normal(k, shape, f32) * s
    L, D = DEPTH, D_MODEL
    return {
        "x": nrm(ks[0], (BATCH, SEQ, D), 1.0),
        "c": nrm(ks[1], (BATCH, D), 1.0),
        "w_ada": nrm(ks[2], (L, D, N_MOD * D), 0.5 * D ** -0.5),
        "b_ada": nrm(ks[3], (L, N_MOD * D), 0.01),
        "ffn1_w_in": nrm(ks[4], (L, D, 2 * D_FF), D ** -0.5),
        "ffn1_w_out": nrm(ks[5], (L, D_FF, D), BETA * D_FF ** -0.5),
        "ln1_g": 1.0 + nrm(ks[6], (L, D), 0.02),
        "ln1_b": nrm(ks[7], (L, D), 0.02),
        "w_mix_in": nrm(ks[8], (L, D, W_IN_COLS), D ** -0.5),
        "rel_bias": nrm(ks[9], (L, A_HEADS, REL_SIZE), 0.5),
        "w_alpha2": nrm(ks[10], (L, GATE_RANK, B_KEY_DIM), GATE_RANK ** -0.5),
        "b_alpha": nrm(ks[11], (L, B_KEY_DIM), 0.1),
        "gla_norm_g": 1.0 + nrm(ks[12], (L, B_HV), 0.02),
        "w_proj_a": nrm(ks[13], (L, A_WIDTH, D), BETA * A_WIDTH ** -0.5),
        "w_proj_b": nrm(ks[14], (L, B_VAL_DIM, D), BETA * B_VAL_DIM ** -0.5),
        "w_mix_out": nrm(ks[15], (L, D, D), BETA * D ** -0.5),
        "ln2_g": 1.0 + nrm(ks[16], (L, D), 0.02),
        "ln2_b": nrm(ks[17], (L, D), 0.02),
        "ffn2_w_in": nrm(ks[18], (L, D, 2 * D_FF), D ** -0.5),
        "ffn2_w_out": nrm(ks[19], (L, D_FF, D), BETA * D_FF ** -0.5),
        "ln3_g": 1.0 + nrm(ks[20], (L, D), 0.02),
        "ln3_b": nrm(ks[21], (L, D), 0.02),
    }


def _fwd_reference(x, c, w_ada, b_ada, ffn1_w_in, ffn1_w_out, ln1_g, ln1_b, w_mix_in, rel_bias,
              w_alpha2, b_alpha, gla_norm_g, w_proj_a, w_proj_b, w_mix_out, ln2_g, ln2_b,
              ffn2_w_in, ffn2_w_out, ln3_g, ln3_b):
    h = x
    for l in range(DEPTH):
        mod = jax.nn.silu(c) @ w_ada[l] + b_ada[l]
        sh1, sc1, g1, sh2, sc2, g2, sh3, sc3, g3 = jnp.split(mod, N_MOD, axis=-1)
        f1 = swiglu(modulate(h, sh1, sc1), ffn1_w_in[l], ffn1_w_out[l])
        h = layer_norm(ALPHA * h + 0.5 * g1[:, None, :] * f1, ln1_g[l], ln1_b[l])
        m = token_mix(modulate(h, sh2, sc2), w_mix_in[l], rel_bias[l], w_alpha2[l], b_alpha[l],
                      gla_norm_g[l], w_proj_a[l], w_proj_b[l], w_mix_out[l])
        h = layer_norm(ALPHA * h + g2[:, None, :] * m, ln2_g[l], ln2_b[l])
        f2 = swiglu(modulate(h, sh3, sc3), ffn2_w_in[l], ffn2_w_out[l])
        h = layer_norm(ALPHA * h + 0.5 * g3[:, None, :] * f2, ln3_g[l], ln3_b[l])
    return h


import jax as _jax
import jax.numpy as _jnp

TWIN_FORMAT = 'train_step'
FWD_PARAMS = ['x', 'c', 'w_ada', 'b_ada', 'ffn1_w_in', 'ffn1_w_out', 'ln1_g', 'ln1_b', 'w_mix_in', 'rel_bias', 'w_alpha2', 'b_alpha', 'gla_norm_g', 'w_proj_a', 'w_proj_b', 'w_mix_out', 'ln2_g', 'ln2_b', 'ffn2_w_in', 'ffn2_w_out', 'ln3_g', 'ln3_b']
TWIN_WEIGHTS = ['w_ada', 'b_ada', 'ffn1_w_in', 'ffn1_w_out', 'ln1_g', 'ln1_b', 'w_mix_in', 'rel_bias', 'w_alpha2', 'b_alpha', 'gla_norm_g', 'w_proj_a', 'w_proj_b', 'w_mix_out', 'ln2_g', 'ln2_b', 'ffn2_w_in', 'ffn2_w_out', 'ln3_g', 'ln3_b']
TWIN_DIFF_INPUT = 'x'
TWIN_INPUTS = ['x', 'c', 'w_ada', 'b_ada', 'ffn1_w_in', 'ffn1_w_out', 'ln1_g', 'ln1_b', 'w_mix_in', 'rel_bias', 'w_alpha2', 'b_alpha', 'gla_norm_g', 'w_proj_a', 'w_proj_b', 'w_mix_out', 'ln2_g', 'ln2_b', 'ffn2_w_in', 'ffn2_w_out', 'ln3_g', 'ln3_b', 'loss_target', 'm_w_ada', 'm_b_ada', 'm_ffn1_w_in', 'm_ffn1_w_out', 'm_ln1_g', 'm_ln1_b', 'm_w_mix_in', 'm_rel_bias', 'm_w_alpha2', 'm_b_alpha', 'm_gla_norm_g', 'm_w_proj_a', 'm_w_proj_b', 'm_w_mix_out', 'm_ln2_g', 'm_ln2_b', 'm_ffn2_w_in', 'm_ffn2_w_out', 'm_ln3_g', 'm_ln3_b', 'v_w_ada', 'v_b_ada', 'v_ffn1_w_in', 'v_ffn1_w_out', 'v_ln1_g', 'v_ln1_b', 'v_w_mix_in', 'v_rel_bias', 'v_w_alpha2', 'v_b_alpha', 'v_gla_norm_g', 'v_w_proj_a', 'v_w_proj_b', 'v_w_mix_out', 'v_ln2_g', 'v_ln2_b', 'v_ffn2_w_in', 'v_ffn2_w_out', 'v_ln3_g', 'v_ln3_b']
TWIN_OUTPUTS = ['loss', 'grad_x', 'grad_w_ada', 'grad_b_ada', 'grad_ffn1_w_in', 'grad_ffn1_w_out', 'grad_ln1_g', 'grad_ln1_b', 'grad_w_mix_in', 'grad_rel_bias', 'grad_w_alpha2', 'grad_b_alpha', 'grad_gla_norm_g', 'grad_w_proj_a', 'grad_w_proj_b', 'grad_w_mix_out', 'grad_ln2_g', 'grad_ln2_b', 'grad_ffn2_w_in', 'grad_ffn2_w_out', 'grad_ln3_g', 'grad_ln3_b', 'delta_w_ada', 'delta_b_ada', 'delta_ffn1_w_in', 'delta_ffn1_w_out', 'delta_ln1_g', 'delta_ln1_b', 'delta_w_mix_in', 'delta_rel_bias', 'delta_w_alpha2', 'delta_b_alpha', 'delta_gla_norm_g', 'delta_w_proj_a', 'delta_w_proj_b', 'delta_w_mix_out', 'delta_ln2_g', 'delta_ln2_b', 'delta_ffn2_w_in', 'delta_ffn2_w_out', 'delta_ln3_g', 'delta_ln3_b', 'new_m_w_ada', 'new_m_b_ada', 'new_m_ffn1_w_in', 'new_m_ffn1_w_out', 'new_m_ln1_g', 'new_m_ln1_b', 'new_m_w_mix_in', 'new_m_rel_bias', 'new_m_w_alpha2', 'new_m_b_alpha', 'new_m_gla_norm_g', 'new_m_w_proj_a', 'new_m_w_proj_b', 'new_m_w_mix_out', 'new_m_ln2_g', 'new_m_ln2_b', 'new_m_ffn2_w_in', 'new_m_ffn2_w_out', 'new_m_ln3_g', 'new_m_ln3_b', 'new_v_w_ada', 'new_v_b_ada', 'new_v_ffn1_w_in', 'new_v_ffn1_w_out', 'new_v_ln1_g', 'new_v_ln1_b', 'new_v_w_mix_in', 'new_v_rel_bias', 'new_v_w_alpha2', 'new_v_b_alpha', 'new_v_gla_norm_g', 'new_v_w_proj_a', 'new_v_w_proj_b', 'new_v_w_mix_out', 'new_v_ln2_g', 'new_v_ln2_b', 'new_v_ffn2_w_in', 'new_v_ffn2_w_out', 'new_v_ln3_g', 'new_v_ln3_b']
TWIN_LEAF_KINDS = {'loss': 'loss', 'grad_x': 'grad_x', 'grad_w_ada': 'grad_w', 'grad_b_ada': 'grad_w', 'grad_ffn1_w_in': 'grad_w', 'grad_ffn1_w_out': 'grad_w', 'grad_ln1_g': 'grad_w', 'grad_ln1_b': 'grad_w', 'grad_w_mix_in': 'grad_w', 'grad_rel_bias': 'grad_w', 'grad_w_alpha2': 'grad_w', 'grad_b_alpha': 'grad_w', 'grad_gla_norm_g': 'grad_w', 'grad_w_proj_a': 'grad_w', 'grad_w_proj_b': 'grad_w', 'grad_w_mix_out': 'grad_w', 'grad_ln2_g': 'grad_w', 'grad_ln2_b': 'grad_w', 'grad_ffn2_w_in': 'grad_w', 'grad_ffn2_w_out': 'grad_w', 'grad_ln3_g': 'grad_w', 'grad_ln3_b': 'grad_w', 'delta_w_ada': 'delta_w', 'delta_b_ada': 'delta_w', 'delta_ffn1_w_in': 'delta_w', 'delta_ffn1_w_out': 'delta_w', 'delta_ln1_g': 'delta_w', 'delta_ln1_b': 'delta_w', 'delta_w_mix_in': 'delta_w', 'delta_rel_bias': 'delta_w', 'delta_w_alpha2': 'delta_w', 'delta_b_alpha': 'delta_w', 'delta_gla_norm_g': 'delta_w', 'delta_w_proj_a': 'delta_w', 'delta_w_proj_b': 'delta_w', 'delta_w_mix_out': 'delta_w', 'delta_ln2_g': 'delta_w', 'delta_ln2_b': 'delta_w', 'delta_ffn2_w_in': 'delta_w', 'delta_ffn2_w_out': 'delta_w', 'delta_ln3_g': 'delta_w', 'delta_ln3_b': 'delta_w', 'new_m_w_ada': 'new_m', 'new_m_b_ada': 'new_m', 'new_m_ffn1_w_in': 'new_m', 'new_m_ffn1_w_out': 'new_m', 'new_m_ln1_g': 'new_m', 'new_m_ln1_b': 'new_m', 'new_m_w_mix_in': 'new_m', 'new_m_rel_bias': 'new_m', 'new_m_w_alpha2': 'new_m', 'new_m_b_alpha': 'new_m', 'new_m_gla_norm_g': 'new_m', 'new_m_w_proj_a': 'new_m', 'new_m_w_proj_b': 'new_m', 'new_m_w_mix_out': 'new_m', 'new_m_ln2_g': 'new_m', 'new_m_ln2_b': 'new_m', 'new_m_ffn2_w_in': 'new_m', 'new_m_ffn2_w_out': 'new_m', 'new_m_ln3_g': 'new_m', 'new_m_ln3_b': 'new_m', 'new_v_w_ada': 'new_v', 'new_v_b_ada': 'new_v', 'new_v_ffn1_w_in': 'new_v', 'new_v_ffn1_w_out': 'new_v', 'new_v_ln1_g': 'new_v', 'new_v_ln1_b': 'new_v', 'new_v_w_mix_in': 'new_v', 'new_v_rel_bias': 'new_v', 'new_v_w_alpha2': 'new_v', 'new_v_b_alpha': 'new_v', 'new_v_gla_norm_g': 'new_v', 'new_v_w_proj_a': 'new_v', 'new_v_w_proj_b': 'new_v', 'new_v_w_mix_out': 'new_v', 'new_v_ln2_g': 'new_v', 'new_v_ln2_b': 'new_v', 'new_v_ffn2_w_in': 'new_v', 'new_v_ffn2_w_out': 'new_v', 'new_v_ln3_g': 'new_v', 'new_v_ln3_b': 'new_v'}


def _forward(args):
    return _fwd_reference(*[args[k] for k in FWD_PARAMS])


def _output_shape():
    def fwd():
        inp = _fwd_setup_inputs(0)
        return _fwd_reference(*[inp[k] for k in FWD_PARAMS])
    out = _jax.eval_shape(fwd)
    return out.shape, out.dtype

N_MICROBATCH = 1
ADAM_LR = 0.001
ADAM_B1 = 0.9
ADAM_B2 = 0.999
ADAM_EPS = 1e-08
ADAM_WD = 0.01
ADAM_STEP = 10
PER_EXAMPLE_BATCH_AXIS = {'x': 0, 'c': 0, 'loss_target': 0}
SHARED_INPUTS = []
_WEIGHT_DTYPES = {'w_ada': _jnp.float32, 'b_ada': _jnp.float32, 'ffn1_w_in': _jnp.float32, 'ffn1_w_out': _jnp.float32, 'ln1_g': _jnp.float32, 'ln1_b': _jnp.float32, 'w_mix_in': _jnp.float32, 'rel_bias': _jnp.float32, 'w_alpha2': _jnp.float32, 'b_alpha': _jnp.float32, 'gla_norm_g': _jnp.float32, 'w_proj_a': _jnp.float32, 'w_proj_b': _jnp.float32, 'w_mix_out': _jnp.float32, 'ln2_g': _jnp.float32, 'ln2_b': _jnp.float32, 'ffn2_w_in': _jnp.float32, 'ffn2_w_out': _jnp.float32, 'ln3_g': _jnp.float32, 'ln3_b': _jnp.float32}
MOMENT_SCALE = {'w_ada': 5.981393e-03, 'b_ada': 1.022265e-02, 'ffn1_w_in': 2.925728e-03, 'ffn1_w_out': 8.041312e-03, 'ln1_g': 5.882180e-01, 'ln1_b': 2.807624e-01, 'w_mix_in': 2.868568e-03, 'rel_bias': 4.658886e-04, 'w_alpha2': 1.096884e-03, 'b_alpha': 3.447813e-03, 'gla_norm_g': 9.612466e-03, 'w_proj_a': 2.801372e-03, 'w_proj_b': 4.813061e-03, 'w_mix_out': 5.568988e-03, 'ln2_g': 5.889148e-01, 'ln2_b': 2.821737e-01, 'ffn2_w_in': 2.925654e-03, 'ffn2_w_out': 8.038342e-03, 'ln3_g': 1.602918e+01, 'ln3_b': 4.852971e-01}


def _to_microbatches(a, axis):
    t = _jnp.moveaxis(a, axis, 0)
    t = t.reshape((N_MICROBATCH, t.shape[0] // N_MICROBATCH) + t.shape[1:])
    return _jnp.moveaxis(t, 1, axis + 1)


def setup_inputs(seed: int = 0) -> dict:
    inp = _fwd_setup_inputs(seed)
    key = _jax.random.fold_in(_jax.random.key(seed), 7919)
    shape, _ = _output_shape()
    out = dict(inp)
    out["loss_target"] = _jax.random.normal(_jax.random.fold_in(key, 0), shape, _jnp.float32)
    for i, name in enumerate(TWIN_WEIGHTS):
        w = inp[name].astype(_jnp.float32)
        if MOMENT_SCALE is None:
            s = _jnp.sqrt(_jnp.mean(_jnp.square(w)) + 1e-30)
        else:
            s = MOMENT_SCALE[name]
        km, kv = _jax.random.split(_jax.random.fold_in(key, i + 1))
        out[name] = w
        out["m_" + name] = s * _jax.random.normal(km, w.shape, _jnp.float32)
        out["v_" + name] = (s * s) * _jax.random.uniform(kv, w.shape, _jnp.float32, 0.5, 1.5)
    if N_MICROBATCH > 1:
        for name, axis in PER_EXAMPLE_BATCH_AXIS.items():
            out[name] = _to_microbatches(out[name], axis)
    return {'x': out['x'], 'c': out['c'], 'w_ada': out['w_ada'], 'b_ada': out['b_ada'], 'ffn1_w_in': out['ffn1_w_in'], 'ffn1_w_out': out['ffn1_w_out'], 'ln1_g': out['ln1_g'], 'ln1_b': out['ln1_b'], 'w_mix_in': out['w_mix_in'], 'rel_bias': out['rel_bias'], 'w_alpha2': out['w_alpha2'], 'b_alpha': out['b_alpha'], 'gla_norm_g': out['gla_norm_g'], 'w_proj_a': out['w_proj_a'], 'w_proj_b': out['w_proj_b'], 'w_mix_out': out['w_mix_out'], 'ln2_g': out['ln2_g'], 'ln2_b': out['ln2_b'], 'ffn2_w_in': out['ffn2_w_in'], 'ffn2_w_out': out['ffn2_w_out'], 'ln3_g': out['ln3_g'], 'ln3_b': out['ln3_b'], 'loss_target': out['loss_target'], 'm_w_ada': out['m_w_ada'], 'm_b_ada': out['m_b_ada'], 'm_ffn1_w_in': out['m_ffn1_w_in'], 'm_ffn1_w_out': out['m_ffn1_w_out'], 'm_ln1_g': out['m_ln1_g'], 'm_ln1_b': out['m_ln1_b'], 'm_w_mix_in': out['m_w_mix_in'], 'm_rel_bias': out['m_rel_bias'], 'm_w_alpha2': out['m_w_alpha2'], 'm_b_alpha': out['m_b_alpha'], 'm_gla_norm_g': out['m_gla_norm_g'], 'm_w_proj_a': out['m_w_proj_a'], 'm_w_proj_b': out['m_w_proj_b'], 'm_w_mix_out': out['m_w_mix_out'], 'm_ln2_g': out['m_ln2_g'], 'm_ln2_b': out['m_ln2_b'], 'm_ffn2_w_in': out['m_ffn2_w_in'], 'm_ffn2_w_out': out['m_ffn2_w_out'], 'm_ln3_g': out['m_ln3_g'], 'm_ln3_b': out['m_ln3_b'], 'v_w_ada': out['v_w_ada'], 'v_b_ada': out['v_b_ada'], 'v_ffn1_w_in': out['v_ffn1_w_in'], 'v_ffn1_w_out': out['v_ffn1_w_out'], 'v_ln1_g': out['v_ln1_g'], 'v_ln1_b': out['v_ln1_b'], 'v_w_mix_in': out['v_w_mix_in'], 'v_rel_bias': out['v_rel_bias'], 'v_w_alpha2': out['v_w_alpha2'], 'v_b_alpha': out['v_b_alpha'], 'v_gla_norm_g': out['v_gla_norm_g'], 'v_w_proj_a': out['v_w_proj_a'], 'v_w_proj_b': out['v_w_proj_b'], 'v_w_mix_out': out['v_w_mix_out'], 'v_ln2_g': out['v_ln2_g'], 'v_ln2_b': out['v_ln2_b'], 'v_ffn2_w_in': out['v_ffn2_w_in'], 'v_ffn2_w_out': out['v_ffn2_w_out'], 'v_ln3_g': out['v_ln3_g'], 'v_ln3_b': out['v_ln3_b']}


def _loss(weights, diff, rest, loss_target):
    with _jax.named_scope("forward"):
        args = {**rest, TWIN_DIFF_INPUT: diff, **{k: w.astype(_WEIGHT_DTYPES[k]) for k, w in weights.items()}}
        y = _forward(args)
    with _jax.named_scope("loss_head"):
        err = _jnp.square(y.astype(_jnp.float32) - loss_target)
        return 0.5 * _jnp.sum(_jnp.mean(err, axis=-1)) if err.ndim else 0.5 * err


def _adamw(w, g, m, v):
    m = ADAM_B1 * m + (1.0 - ADAM_B1) * g
    v = ADAM_B2 * v + (1.0 - ADAM_B2) * _jnp.square(g)
    m_hat = m / (1.0 - ADAM_B1 ** ADAM_STEP)
    v_hat = v / (1.0 - ADAM_B2 ** ADAM_STEP)
    delta = -ADAM_LR * (m_hat / (_jnp.sqrt(v_hat) + ADAM_EPS) + ADAM_WD * w)
    return delta, m, v


def reference(x, c, w_ada, b_ada, ffn1_w_in, ffn1_w_out, ln1_g, ln1_b, w_mix_in, rel_bias, w_alpha2, b_alpha, gla_norm_g, w_proj_a, w_proj_b, w_mix_out, ln2_g, ln2_b, ffn2_w_in, ffn2_w_out, ln3_g, ln3_b, loss_target, m_w_ada, m_b_ada, m_ffn1_w_in, m_ffn1_w_out, m_ln1_g, m_ln1_b, m_w_mix_in, m_rel_bias, m_w_alpha2, m_b_alpha, m_gla_norm_g, m_w_proj_a, m_w_proj_b, m_w_mix_out, m_ln2_g, m_ln2_b, m_ffn2_w_in, m_ffn2_w_out, m_ln3_g, m_ln3_b, v_w_ada, v_b_ada, v_ffn1_w_in, v_ffn1_w_out, v_ln1_g, v_ln1_b, v_w_mix_in, v_rel_bias, v_w_alpha2, v_b_alpha, v_gla_norm_g, v_w_proj_a, v_w_proj_b, v_w_mix_out, v_ln2_g, v_ln2_b, v_ffn2_w_in, v_ffn2_w_out, v_ln3_g, v_ln3_b):
    given = dict(x=x, c=c, w_ada=w_ada, b_ada=b_ada, ffn1_w_in=ffn1_w_in, ffn1_w_out=ffn1_w_out, ln1_g=ln1_g, ln1_b=ln1_b, w_mix_in=w_mix_in, rel_bias=rel_bias, w_alpha2=w_alpha2, b_alpha=b_alpha, gla_norm_g=gla_norm_g, w_proj_a=w_proj_a, w_proj_b=w_proj_b, w_mix_out=w_mix_out, ln2_g=ln2_g, ln2_b=ln2_b, ffn2_w_in=ffn2_w_in, ffn2_w_out=ffn2_w_out, ln3_g=ln3_g, ln3_b=ln3_b, loss_target=loss_target, m_w_ada=m_w_ada, m_b_ada=m_b_ada, m_ffn1_w_in=m_ffn1_w_in, m_ffn1_w_out=m_ffn1_w_out, m_ln1_g=m_ln1_g, m_ln1_b=m_ln1_b, m_w_mix_in=m_w_mix_in, m_rel_bias=m_rel_bias, m_w_alpha2=m_w_alpha2, m_b_alpha=m_b_alpha, m_gla_norm_g=m_gla_norm_g, m_w_proj_a=m_w_proj_a, m_w_proj_b=m_w_proj_b, m_w_mix_out=m_w_mix_out, m_ln2_g=m_ln2_g, m_ln2_b=m_ln2_b, m_ffn2_w_in=m_ffn2_w_in, m_ffn2_w_out=m_ffn2_w_out, m_ln3_g=m_ln3_g, m_ln3_b=m_ln3_b, v_w_ada=v_w_ada, v_b_ada=v_b_ada, v_ffn1_w_in=v_ffn1_w_in, v_ffn1_w_out=v_ffn1_w_out, v_ln1_g=v_ln1_g, v_ln1_b=v_ln1_b, v_w_mix_in=v_w_mix_in, v_rel_bias=v_rel_bias, v_w_alpha2=v_w_alpha2, v_b_alpha=v_b_alpha, v_gla_norm_g=v_gla_norm_g, v_w_proj_a=v_w_proj_a, v_w_proj_b=v_w_proj_b, v_w_mix_out=v_w_mix_out, v_ln2_g=v_ln2_g, v_ln2_b=v_ln2_b, v_ffn2_w_in=v_ffn2_w_in, v_ffn2_w_out=v_ffn2_w_out, v_ln3_g=v_ln3_g, v_ln3_b=v_ln3_b)
    weights = {n: given[n] for n in TWIN_WEIGHTS}
    shared = {n: given[n] for n in SHARED_INPUTS}
    per_example = {n: given[n] for n in ['x', 'c']}
    grad_fn = _jax.value_and_grad(_loss, argnums=(0, 1))

    def one_microbatch(ex, loss_target):
        ex = dict(ex)
        diff = ex.pop(TWIN_DIFF_INPUT)
        return grad_fn(weights, diff, {**shared, **ex}, loss_target)

    if N_MICROBATCH == 1:
        loss, (grad_w, grad_x) = one_microbatch(per_example, given["loss_target"])
    else:
        def body(carry, xs):
            loss_sum, grad_sum = carry
            l_k, (gw_k, gx_k) = one_microbatch(xs[0], xs[1])
            with _jax.named_scope("update"):
                return (loss_sum + l_k, _jax.tree.map(_jnp.add, grad_sum, gw_k)), gx_k

        init = (_jnp.zeros((), _jnp.float32), _jax.tree.map(_jnp.zeros_like, weights))
        (loss, grad_w), grad_x = _jax.lax.scan(body, init, (per_example, given["loss_target"]))
    with _jax.named_scope("update"):
        delta_w, new_m, new_v = {}, {}, {}
        for n in TWIN_WEIGHTS:
            delta_w[n], new_m[n], new_v[n] = _adamw(weights[n], grad_w[n], given["m_" + n], given["v_" + n])
    return (loss, grad_x, *[grad_w[n] for n in TWIN_WEIGHTS], *[delta_w[n] for n in TWIN_WEIGHTS],
            *[new_m[n] for n in TWIN_WEIGHTS], *[new_v[n] for n in TWIN_WEIGHTS])
```

```python
import functools

import jax
import jax.numpy as jnp
from jax import lax
from jax.experimental import pallas as pl
from jax.experimental.pallas import tpu as pltpu

F32 = jnp.float32
BF16 = jnp.bfloat16
MESH = pl.DeviceIdType.MESH
HIGHEST = lax.Precision.HIGHEST

VMEM_LIMIT_BYTES = 56 * 1024 * 1024
LANES = 128
SUBLANES = 8

CHUNK = 64
A_HEADS = 16
A_HEAD_DIM = 64
A_PAST_CHUNKS = 8
A_BAND = (A_PAST_CHUNKS + 1) * CHUNK
A_PAD = A_PAST_CHUNKS * CHUNK
REL_CLIP = 256
REL_SIZE = REL_CLIP + CHUNK
B_HEADS = 4
GATE_RANK = 16
GATE_TAU = 16.0
N_MOD = 9
DEPTH = 1
ALPHA = (2.0 * DEPTH) ** 0.25
LN_EPS = 1e-5
RMS_EPS = 1e-6
ADAM_LR = 0.001
ADAM_B1 = 0.9
ADAM_B2 = 0.999
ADAM_EPS = 1e-08
ADAM_WD = 0.01
ADAM_STEP = 10
NEG_BIG = -1e30

N_CHIPS = 4
N_DEV = 8


def _cp(*sem):
    return pltpu.CompilerParams(dimension_semantics=sem, vmem_limit_bytes=VMEM_LIMIT_BYTES)


def _tile(n, prefs):
    for t in prefs:
        if t <= n and n % t == 0:
            return t
    return n


_DIMS = {"nn": (((1,), (0,)), ((), ())), "nt": (((1,), (1,)), ((), ())), "tn": (((0,), (0,)), ((), ()))}


def _dot(a, b, mode="nn", precision=None):
    return lax.dot_general(a, b, _DIMS[mode], precision=precision, preferred_element_type=F32)


def _sigmoid(x):
    return 1.0 / (1.0 + jnp.exp(-x))


def _mm(name, mode, a, b, mnk, *, tm, tn, tk, out_dtype=F32, precision=None, a_spec=None, b_spec=None,
        out_shape=None, o_spec=None, add=None, a_fn=None):
    m, n, k = mnk
    assert m % tm == 0 and n % tn == 0 and k % tk == 0, (name, mnk, tm, tn, tk)
    nk = k // tk
    if a_spec is None:
        a_spec = {"nn": pl.BlockSpec((tm, tk), lambda i, j, kk: (i, kk)),
                  "nt": pl.BlockSpec((tm, tk), lambda i, j, kk: (i, kk)),
                  "tn": pl.BlockSpec((tk, tm), lambda i, j, kk: (kk, i))}[mode]
    if b_spec is None:
        b_spec = {"nn": pl.BlockSpec((tk, tn), lambda i, j, kk: (kk, j)),
                  "nt": pl.BlockSpec((tn, tk), lambda i, j, kk: (j, kk)),
                  "tn": pl.BlockSpec((tk, tn), lambda i, j, kk: (kk, j))}[mode]
    if o_spec is None:
        o_spec = pl.BlockSpec((tm, tn), lambda i, j, kk: (i, j))
    if out_shape is None:
        out_shape = (m, n)
    has_add = add is not None

    def body(*refs):
        a_ref, b_ref = refs[0], refs[1]
        add_ref = refs[2] if has_add else None
        o_ref = refs[3] if has_add else refs[2]
        av = a_ref[...]
        if a_fn is not None:
            av = a_fn(av)
        part = _dot(av, b_ref[...], mode, precision)

        def finish(total):
            if has_add:
                total = total + add_ref[...]
            o_ref[...] = total.astype(out_dtype)

        if nk == 1:
            finish(part)
        else:
            acc_ref = refs[-1]
            kk = pl.program_id(2)

            @pl.when(kk == 0)
            def _():
                acc_ref[...] = part

            @pl.when(kk > 0)
            def _():
                acc_ref[...] += part

            @pl.when(kk == nk - 1)
            def _():
                finish(acc_ref[...])

    in_specs = [a_spec, b_spec]
    operands = [a, b]
    if has_add:
        in_specs.append(pl.BlockSpec((tm, tn), lambda i, j, kk: (i, j)))
        operands.append(add)
    return pl.pallas_call(
        body, name=name, out_shape=jax.ShapeDtypeStruct(out_shape, out_dtype), grid=(m // tm, n // tn, nk),
        in_specs=in_specs, out_specs=o_spec,
        scratch_shapes=[pltpu.VMEM((tm, tn), F32)] if nk > 1 else [],
        compiler_params=_cp("parallel", "parallel", "arbitrary"),
    )(*operands)


def _row_spec(tr, d):
    return pl.BlockSpec((tr, d), lambda i: (i, 0))


def _vec_spec(d, rows=1):
    return pl.BlockSpec((rows, d), lambda i: (0, 0))


def _modulate(name, x, sh, sc):
    s, d = x.shape
    tr = _tile(s, (512, 256))

    def body(x_ref, sh_ref, sc_ref, o_ref):
        o_ref[...] = (x_ref[...] * (1.0 + sc_ref[...]) + sh_ref[...]).astype(BF16)

    return pl.pallas_call(
        body, name=name, out_shape=jax.ShapeDtypeStruct((s, d), BF16), grid=(s // tr,),
        in_specs=[_row_spec(tr, d), _vec_spec(d), _vec_spec(d)], out_specs=_row_spec(tr, d),
        compiler_params=_cp("parallel"),
    )(x, sh, sc)


def _ln_stats(r):
    mu = jnp.mean(r, axis=-1, keepdims=True)
    xc = r - mu
    var = jnp.mean(xc * xc, axis=-1, keepdims=True)
    rstd = lax.rsqrt(var + LN_EPS)
    return xc * rstd, rstd


def _resid_ln_fwd(name, x, f, gate, ln_g, ln_b, sh_n, sc_n, coef):
    s, d = x.shape
    tr = _tile(s, (256,))

    def body(x_ref, f_ref, gate_ref, g_ref, b_ref, sh_ref, sc_ref, h_ref, u_ref):
        r = ALPHA * x_ref[...] + (coef * gate_ref[...]) * f_ref[...]
        xhat, _ = _ln_stats(r)
        h = xhat * g_ref[...] + b_ref[...]
        h_ref[...] = h
        u_ref[...] = (h * (1.0 + sc_ref[...]) + sh_ref[...]).astype(BF16)

    return pl.pallas_call(
        body, name=name, out_shape=(jax.ShapeDtypeStruct((s, d), F32), jax.ShapeDtypeStruct((s, d), BF16)),
        grid=(s // tr,), in_specs=[_row_spec(tr, d), _row_spec(tr, d)] + [_vec_spec(d)] * 5,
        out_specs=(_row_spec(tr, d), _row_spec(tr, d)), compiler_params=_cp("parallel"),
    )(x, f, gate, ln_g, ln_b, sh_n, sc_n)


ROW_DSC, ROW_DSH, ROW_DLN_G, ROW_DLN_B, ROW_DGATE, ROW_LOSS = 0, 1, 2, 3, 4, 5


def _ln_bwd_core(dy, xhat, rstd, ln_g):
    dxhat = dy * ln_g
    m1 = jnp.mean(dxhat, axis=-1, keepdims=True)
    m2 = jnp.mean(dxhat * xhat, axis=-1, keepdims=True)
    return rstd * (dxhat - m1 - xhat * m2)


def _colsum(v):
    return jnp.sum(v, axis=0, keepdims=True)


def _final_ln_loss_bwd(name, x, f, target, gate, ln_g, ln_b, coef):
    s, d = x.shape
    tr = _tile(s, (256,))
    inv_d = 1.0 / d

    def body(x_ref, f_ref, t_ref, gate_ref, g_ref, b_ref, dr_ref, df_ref, acc_ref):
        @pl.when(pl.program_id(0) == 0)
        def _():
            acc_ref[...] = jnp.zeros_like(acc_ref)

        fv = f_ref[...]
        r = ALPHA * x_ref[...] + (coef * gate_ref[...]) * fv
        xhat, rstd = _ln_stats(r)
        h = xhat * g_ref[...] + b_ref[...]
        err = h - t_ref[...]
        dy = err * inv_d
        dr = _ln_bwd_core(dy, xhat, rstd, g_ref[...])
        dr_ref[...] = dr
        df_ref[...] = ((coef * gate_ref[...]) * dr).astype(BF16)
        acc_ref[ROW_DLN_G:ROW_DLN_G + 1, :] += _colsum(dy * xhat)
        acc_ref[ROW_DLN_B:ROW_DLN_B + 1, :] += _colsum(dy)
        acc_ref[ROW_DGATE:ROW_DGATE + 1, :] += _colsum((coef * dr) * fv)
        acc_ref[ROW_LOSS:ROW_LOSS + 1, :] += _colsum(err * err) * (0.5 * inv_d)

    return pl.pallas_call(
        body, name=name,
        out_shape=(jax.ShapeDtypeStruct((s, d), F32), jax.ShapeDtypeStruct((s, d), BF16),
                   jax.ShapeDtypeStruct((SUBLANES, d), F32)),
        grid=(s // tr,), in_specs=[_row_spec(tr, d)] * 3 + [_vec_spec(d)] * 3,
        out_specs=(_row_spec(tr, d), _row_spec(tr, d), _vec_spec(d, SUBLANES)),
        compiler_params=_cp("arbitrary"),
    )(x, f, target, gate, ln_g, ln_b)


def _resid_ln_bwd(name, du_n, dr_n, x, f, sc_n, gate, ln_g, ln_b, coef):
    s, d = x.shape
    tr = _tile(s, (256,))

    def body(du_ref, drn_ref, x_ref, f_ref, sc_ref, gate_ref, g_ref, b_ref, dr_ref, df_ref, acc_ref):
        @pl.when(pl.program_id(0) == 0)
        def _():
            acc_ref[...] = jnp.zeros_like(acc_ref)

        fv = f_ref[...]
        du = du_ref[...]
        r = ALPHA * x_ref[...] + (coef * gate_ref[...]) * fv
        xhat, rstd = _ln_stats(r)
        h = xhat * g_ref[...] + b_ref[...]
        dy = du * (1.0 + sc_ref[...]) + ALPHA * drn_ref[...]
        dr = _ln_bwd_core(dy, xhat, rstd, g_ref[...])
        dr_ref[...] = dr
        df_ref[...] = ((coef * gate_ref[...]) * dr).astype(BF16)
        acc_ref[ROW_DSC:ROW_DSC + 1, :] += _colsum(du * h)
        acc_ref[ROW_DSH:ROW_DSH + 1, :] += _colsum(du)
        acc_ref[ROW_DLN_G:ROW_DLN_G + 1, :] += _colsum(dy * xhat)
        acc_ref[ROW_DLN_B:ROW_DLN_B + 1, :] += _colsum(dy)
        acc_ref[ROW_DGATE:ROW_DGATE + 1, :] += _colsum((coef * dr) * fv)

    return pl.pallas_call(
        body, name=name,
        out_shape=(jax.ShapeDtypeStruct((s, d), F32), jax.ShapeDtypeStruct((s, d), BF16),
                   jax.ShapeDtypeStruct((SUBLANES, d), F32)),
        grid=(s // tr,), in_specs=[_row_spec(tr, d)] * 4 + [_vec_spec(d)] * 4,
        out_specs=(_row_spec(tr, d), _row_spec(tr, d), _vec_spec(d, SUBLANES)),
        compiler_params=_cp("arbitrary"),
    )(du_n, dr_n, x, f, sc_n, gate, ln_g, ln_b)


def _input_grad(name, du, dr, x, sc):
    s, d = x.shape
    tr = _tile(s, (256,))

    def body(du_ref, dr_ref, x_ref, sc_ref, gx_ref, acc_ref):
        @pl.when(pl.program_id(0) == 0)
        def _():
            acc_ref[...] = jnp.zeros_like(acc_ref)

        du = du_ref[...]
        gx_ref[...] = du * (1.0 + sc_ref[...]) + ALPHA * dr_ref[...]
        acc_ref[ROW_DSC:ROW_DSC + 1, :] += _colsum(du * x_ref[...])
        acc_ref[ROW_DSH:ROW_DSH + 1, :] += _colsum(du)

    return pl.pallas_call(
        body, name=name,
        out_shape=(jax.ShapeDtypeStruct((s, d), F32), jax.ShapeDtypeStruct((SUBLANES, d), F32)),
        grid=(s // tr,), in_specs=[_row_spec(tr, d)] * 3 + [_vec_spec(d)],
        out_specs=(_row_spec(tr, d), _vec_spec(d, SUBLANES)), compiler_params=_cp("arbitrary"),
    )(du, dr, x, sc)


def _ffn_in_fwd(name, u, w_in):
    s, d = u.shape
    f = w_in.shape[1] // 2
    tm, tn = _tile(s, (1024, 512)), _tile(f, (512, 256, 128))
    nb = f // tn

    def body(u_ref, wa_ref, wb_ref, ab_ref, act_ref):
        uv = u_ref[...]
        a = _dot(uv, wa_ref[...])
        b = _dot(uv, wb_ref[...])
        ab_ref[0] = a.astype(BF16)
        ab_ref[1] = b.astype(BF16)
        act_ref[...] = (a * _sigmoid(a) * b).astype(BF16)

    return pl.pallas_call(
        body, name=name,
        out_shape=(jax.ShapeDtypeStruct((2, s, f), BF16), jax.ShapeDtypeStruct((s, f), BF16)),
        grid=(s // tm, nb),
        in_specs=[pl.BlockSpec((tm, d), lambda i, j: (i, 0)), pl.BlockSpec((d, tn), lambda i, j: (0, j)),
                  pl.BlockSpec((d, tn), lambda i, j: (0, j + nb))],
        out_specs=(pl.BlockSpec((2, tm, tn), lambda i, j: (0, i, j)), pl.BlockSpec((tm, tn), lambda i, j: (i, j))),
        compiler_params=_cp("parallel", "parallel"),
    )(u, w_in, w_in)


def _ffn_out_bwd(name, df, w_out, ab):
    s, d = df.shape
    f = w_out.shape[0]
    tm, tn = _tile(s, (1024, 512)), _tile(f, (512, 256, 128))

    def body(df_ref, w_ref, ab_ref, dab_ref):
        dact = _dot(df_ref[...], w_ref[...], "nt")
        a = ab_ref[0].astype(F32)
        b = ab_ref[1].astype(F32)
        sg = _sigmoid(a)
        dab_ref[0] = (dact * b * (sg * (1.0 + a * (1.0 - sg)))).astype(BF16)
        dab_ref[1] = (dact * (a * sg)).astype(BF16)

    return pl.pallas_call(
        body, name=name, out_shape=jax.ShapeDtypeStruct((2, s, f), BF16), grid=(s // tm, f // tn),
        in_specs=[pl.BlockSpec((tm, d), lambda i, j: (i, 0)), pl.BlockSpec((tn, d), lambda i, j: (j, 0)),
                  pl.BlockSpec((2, tm, tn), lambda i, j: (0, i, j))],
        out_specs=pl.BlockSpec((2, tm, tn), lambda i, j: (0, i, j)),
        compiler_params=_cp("parallel", "parallel"),
    )(df, w_out, ab)


def _ffn_forward(tag, u, w_in, w_out):
    s, d = u.shape
    f = w_out.shape[0]
    ab, act = _ffn_in_fwd(f"{tag}_in_fwd", u, w_in)
    out = _mm(f"{tag}_out_fwd", "nn", act, w_out, (s, d, f), tm=_tile(s, (1024,)), tn=_tile(d, (1024,)),
              tk=_tile(f, (1408, 512, 128)))
    return out, (ab, act)


def _ffn_backward(tag, df, u, saved, w_in, w_out):
    ab, act = saved
    s, d = u.shape
    f = w_out.shape[0]
    dab = _ffn_out_bwd(f"{tag}_out_bwd", df, w_out, ab)
    tmf = _tile(f, (1408, 512, 128))
    dw_out = _mm(f"{tag}_dw_out", "tn", act, df, (f, d, s), tm=tmf, tn=_tile(d, (1024,)), tk=_tile(s, (1024,)))
    tk = _tile(f, (1408, 512, 128))
    nkh = f // tk
    du = _mm(f"{tag}_du", "nt", dab, w_in, (s, d, 2 * f), tm=_tile(s, (1024,)), tn=_tile(d, (1024,)), tk=tk,
             a_spec=pl.BlockSpec((None, _tile(s, (1024,)), tk), lambda i, j, kk: (kk // nkh, i, kk % nkh)))
    tn = _tile(f, (1408, 512, 128))
    nbh = f // tn
    tks = _tile(s, (1024,))
    dw_in = _mm(f"{tag}_dw_in", "tn", u, dab, (d, 2 * f, s), tm=_tile(d, (1024,)), tn=tn, tk=tks,
                b_spec=pl.BlockSpec((None, tks, tn), lambda i, j, kk: (j // nbh, kk, j % nbh)))
    return du, dw_in, dw_out


def _attn_scores(qc, kw, bias, key0):
    sc = _dot(qc, kw, "nt") * (A_HEAD_DIM ** -0.5) + bias
    ks = lax.broadcasted_iota(jnp.int32, sc.shape, 1)
    sc = jnp.where(key0 + ks >= 0, sc, NEG_BIG)
    m = jnp.max(sc, axis=-1, keepdims=True)
    p = jnp.exp(sc - m)
    return p, jnp.sum(p, axis=-1, keepdims=True)


def _attn_fwd(q, kp, vp, bias):
    h, s, dh = q.shape
    tq = _tile(s, (256, 128, 64))
    ncq = tq // CHUNK

    def body(q_ref, k_ref, v_ref, b_ref, o_ref):
        base = pl.program_id(1) * tq
        bias_v = b_ref[...]
        for c in range(ncq):
            start = pl.multiple_of(base + c * CHUNK, CHUNK)
            kw = k_ref[pl.ds(start, A_BAND), :]
            vw = v_ref[pl.ds(start, A_BAND), :]
            p, l = _attn_scores(q_ref[c * CHUNK:(c + 1) * CHUNK, :], kw, bias_v, start - A_PAD)
            o = _dot((p / l).astype(BF16), vw)
            o_ref[c * CHUNK:(c + 1) * CHUNK, :] = o.astype(BF16)

    return pl.pallas_call(
        body, name="attn_fwd", out_shape=jax.ShapeDtypeStruct((h, s, dh), BF16), grid=(h, s // tq),
        in_specs=[pl.BlockSpec((None, tq, dh), lambda hh, i: (hh, i, 0)),
                  pl.BlockSpec((None, s + A_PAD, dh), lambda hh, i: (hh, 0, 0)),
                  pl.BlockSpec((None, s + A_PAD, dh), lambda hh, i: (hh, 0, 0)),
                  pl.BlockSpec((None, CHUNK, A_BAND), lambda hh, i: (hh, 0, 0))],
        out_specs=pl.BlockSpec((None, tq, dh), lambda hh, i: (hh, i, 0)),
        compiler_params=_cp("parallel", "arbitrary"),
    )(q, kp, vp, bias)


def _attn_bwd(q, kp, vp, bias, do):
    h, s, dh = q.shape
    tq = _tile(s, (256, 128, 64))
    ncq = tq // CHUNK
    scale = A_HEAD_DIM ** -0.5

    def body(q_ref, k_ref, v_ref, b_ref, do_ref, dq_ref, dk_ref, dv_ref, db_ref):
        @pl.when(pl.program_id(1) == 0)
        def _():
            dk_ref[...] = jnp.zeros_like(dk_ref)
            dv_ref[...] = jnp.zeros_like(dv_ref)
            db_ref[...] = jnp.zeros_like(db_ref)

        base = pl.program_id(1) * tq
        bias_v = b_ref[...]
        for c in range(ncq):
            start = pl.multiple_of(base + c * CHUNK, CHUNK)
            rows = slice(c * CHUNK, (c + 1) * CHUNK)
            kw = k_ref[pl.ds(start, A_BAND), :]
            vw = v_ref[pl.ds(start, A_BAND), :]
            qc = q_ref[rows, :]
            doc = do_ref[rows, :]
            p, l = _attn_scores(qc, kw, bias_v, start - A_PAD)
            p = p / l
            dp = _dot(doc, vw, "nt")
            delta = jnp.sum(p * dp, axis=-1, keepdims=True)
            ds = p * (dp - delta)
            db_ref[...] += ds
            dsb = (ds * scale).astype(BF16)
            dq_ref[rows, :] = _dot(dsb, kw).astype(BF16)
            dk_ref[pl.ds(start, A_BAND), :] += _dot(dsb, qc, "tn")
            dv_ref[pl.ds(start, A_BAND), :] += _dot(p.astype(BF16), doc, "tn")

    kv_spec = pl.BlockSpec((None, s + A_PAD, dh), lambda hh, i: (hh, 0, 0))
    q_spec = pl.BlockSpec((None, tq, dh), lambda hh, i: (hh, i, 0))
    b_spec = pl.BlockSpec((None, CHUNK, A_BAND), lambda hh, i: (hh, 0, 0))
    return pl.pallas_call(
        body, name="attn_bwd",
        out_shape=(jax.ShapeDtypeStruct((h, s, dh), BF16), jax.ShapeDtypeStruct((h, s + A_PAD, dh), F32),
                   jax.ShapeDtypeStruct((h, s + A_PAD, dh), F32), jax.ShapeDtypeStruct((h, CHUNK, A_BAND), F32)),
        grid=(h, s // tq), in_specs=[q_spec, kv_spec, kv_spec, b_spec, q_spec],
        out_specs=(q_spec, kv_spec, kv_spec, b_spec), compiler_params=_cp("parallel", "arbitrary"),
    )(q, kp, vp, bias, do)


def _rel_onehot():
    qi = jnp.arange(CHUNK)[:, None]
    ks = jnp.arange(A_BAND)[None, :]
    idx = (jnp.clip(ks - A_PAD - qi, -REL_CLIP, CHUNK - 1) + REL_CLIP).reshape(1, CHUNK * A_BAND)
    return (jnp.arange(REL_SIZE)[:, None] == idx).astype(F32)


def _gla_gate(lr, wa2, balpha):
    z = _dot(lr, wa2) + balpha
    la = (jnp.minimum(z, 0.0) - jnp.log(1.0 + jnp.exp(-jnp.abs(z)))) * (1.0 / GATE_TAU)
    row = lax.broadcasted_iota(jnp.int32, (CHUNK, CHUNK), 0)
    col = lax.broadcasted_iota(jnp.int32, (CHUNK, CHUNK), 1)
    cum = _dot((row >= col).astype(F32), la, precision=HIGHEST)
    return z, la, cum


def _gla_dims(p2):
    kd = p2.shape[1] // 6
    hk = kd // B_HEADS
    hv = 2 * hk
    return kd, hk, hv


def _gla_fwd(p2, lrp, wa2p, balpha, gnorm):
    s = p2.shape[0]
    kd, hk, hv = _gla_dims(p2)
    nc = s // CHUNK
    qscale = hk ** -0.5

    def body(p_ref, lr_ref, wa_ref, ba_ref, gn_ref, yb_ref, st_ref, state):
        @pl.when(pl.program_id(0) == 0)
        def _():
            state[...] = jnp.zeros_like(state)

        _, _, cum = _gla_gate(lr_ref[...], wa_ref[...], ba_ref[...])
        last = cum[CHUNK - 1:CHUNK, :]
        e = jnp.exp(last - cum)
        dch = jnp.exp(last)
        gn = gn_ref[...]
        for hh in range(B_HEADS):
            ks = slice(hh * hk, (hh + 1) * hk)
            q = p_ref[:, hh * hk:(hh + 1) * hk].astype(F32)
            k = p_ref[:, kd + hh * hk:kd + (hh + 1) * hk].astype(F32)
            v = p_ref[:, 2 * kd + hh * hv:2 * kd + (hh + 1) * hv]
            rg = p_ref[:, 4 * kd + hh * hv:4 * kd + (hh + 1) * hv].astype(F32)
            kdec = (k * e[:, ks]).astype(BF16)
            st = state[hh] * dch[:, ks] + _dot(v, kdec, "tn")
            state[hh] = st
            st_ref[hh] = st
            o = _dot((q * qscale).astype(BF16), st.astype(BF16), "nt")
            rinv = lax.rsqrt(jnp.mean(o * o, axis=-1, keepdims=True) + RMS_EPS)
            yb_ref[:, hh * hv:(hh + 1) * hv] = ((o * rinv * gn) * (rg * _sigmoid(rg))).astype(BF16)

    return pl.pallas_call(
        body, name="gla_fwd",
        out_shape=(jax.ShapeDtypeStruct((s, 2 * kd), BF16), jax.ShapeDtypeStruct((nc, B_HEADS, hv, hk), F32)),
        grid=(nc,),
        in_specs=[pl.BlockSpec((CHUNK, 6 * kd), lambda i: (i, 0)), pl.BlockSpec((CHUNK, LANES), lambda i: (i, 0)),
                  pl.BlockSpec((LANES, kd), lambda i: (0, 0)), pl.BlockSpec((1, kd), lambda i: (0, 0)),
                  pl.BlockSpec((1, hv), lambda i: (0, 0))],
        out_specs=(pl.BlockSpec((CHUNK, 2 * kd), lambda i: (i, 0)),
                   pl.BlockSpec((None, B_HEADS, hv, hk), lambda i: (i, 0, 0, 0))),
        scratch_shapes=[pltpu.VMEM((B_HEADS, hv, hk), F32)], compiler_params=_cp("arbitrary"),
    )(p2, lrp, wa2p, balpha, gnorm)


GLA_ROW_DBALPHA, GLA_ROW_DGNORM = 0, 1


def _gla_bwd(p2, lrp, wa2p, balpha, gnorm, states, dyb):
    s = p2.shape[0]
    kd, hk, hv = _gla_dims(p2)
    nc = s // CHUNK
    qscale = hk ** -0.5

    def body(p_ref, lr_ref, wa_ref, ba_ref, gn_ref, st_ref, sp_ref, dy_ref, dp_ref, dz_ref, sm_ref, gcar):
        i = pl.program_id(0)

        @pl.when(i == 0)
        def _():
            gcar[...] = jnp.zeros_like(gcar)
            sm_ref[...] = jnp.zeros_like(sm_ref)

        has_prev = (i < nc - 1).astype(F32)
        z, _, cum = _gla_gate(lr_ref[...], wa_ref[...], ba_ref[...])
        last = cum[CHUNK - 1:CHUNK, :]
        e = jnp.exp(last - cum)
        dch = jnp.exp(last)
        sgn = _sigmoid(-z) * (1.0 / GATE_TAU)
        gn = gn_ref[...]
        row = lax.broadcasted_iota(jnp.int32, (CHUNK, CHUNK), 0)
        col = lax.broadcasted_iota(jnp.int32, (CHUNK, CHUNK), 1)
        tri_strict = (row > col).astype(F32)
        for hh in range(B_HEADS):
            ks = slice(hh * hk, (hh + 1) * hk)
            q = p_ref[:, hh * hk:(hh + 1) * hk].astype(F32)
            k = p_ref[:, kd + hh * hk:kd + (hh + 1) * hk].astype(F32)
            v = p_ref[:, 2 * kd + hh * hv:2 * kd + (hh + 1) * hv]
            rg = p_ref[:, 4 * kd + hh * hv:4 * kd + (hh + 1) * hv].astype(F32)
            kdecf = k * e[:, ks]
            kdec = kdecf.astype(BF16)
            st16 = st_ref[hh].astype(BF16)
            qs = (q * qscale).astype(BF16)
            o = _dot(qs, st16, "nt")
            rinv = lax.rsqrt(jnp.mean(o * o, axis=-1, keepdims=True) + RMS_EPS)
            dy = dy_ref[:, hh * hv:(hh + 1) * hv].astype(F32)
            sg = _sigmoid(rg)
            onorm = o * rinv
            drg = dy * (onorm * gn) * (sg * (1.0 + rg * (1.0 - sg)))
            dob = dy * (rg * sg)
            sm_ref[GLA_ROW_DGNORM:GLA_ROW_DGNORM + 1, 0:hv] += _colsum(dob * onorm)
            t = dob * gn
            do = rinv * (t - onorm * jnp.mean(t * onorm, axis=-1, keepdims=True))
            do16 = do.astype(BF16)
            dq = _dot(do16, st16) * qscale
            gt = _dot(do16, qs, "tn") + gcar[hh]
            gcar[hh] = gt * dch[:, ks]
            dd = _colsum(gt * sp_ref[hh]) * has_prev
            gt16 = gt.astype(BF16)
            dkdec = _dot(v, gt16)
            dv = _dot(kdec, gt16, "nt")
            dla = dd * dch[:, ks] + _dot(tri_strict, dkdec * kdecf, precision=HIGHEST)
            dzh = dla * sgn[:, ks]
            sm_ref[GLA_ROW_DBALPHA:GLA_ROW_DBALPHA + 1, hh * hk:(hh + 1) * hk] += _colsum(dzh)
            dz_ref[:, hh * hk:(hh + 1) * hk] = dzh.astype(BF16)
            dp_ref[:, hh * hk:(hh + 1) * hk] = dq.astype(BF16)
            dp_ref[:, kd + hh * hk:kd + (hh + 1) * hk] = (dkdec * e[:, ks]).astype(BF16)
            dp_ref[:, 2 * kd + hh * hv:2 * kd + (hh + 1) * hv] = dv.astype(BF16)
            dp_ref[:, 4 * kd + hh * hv:4 * kd + (hh + 1) * hv] = drg.astype(BF16)

    rev = lambda i: (nc - 1 - i, 0)
    return pl.pallas_call(
        body, name="gla_bwd",
        out_shape=(jax.ShapeDtypeStruct((s, 6 * kd), BF16), jax.ShapeDtypeStruct((s, kd), BF16),
                   jax.ShapeDtypeStruct((SUBLANES, kd), F32)),
        grid=(nc,),
        in_specs=[pl.BlockSpec((CHUNK, 6 * kd), rev), pl.BlockSpec((CHUNK, LANES), rev),
                  pl.BlockSpec((LANES, kd), lambda i: (0, 0)), pl.BlockSpec((1, kd), lambda i: (0, 0)),
                  pl.BlockSpec((1, hv), lambda i: (0, 0)),
                  pl.BlockSpec((None, B_HEADS, hv, hk), lambda i: (nc - 1 - i, 0, 0, 0)),
                  pl.BlockSpec((None, B_HEADS, hv, hk), lambda i: (jnp.maximum(nc - 2 - i, 0), 0, 0, 0)),
                  pl.BlockSpec((CHUNK, 2 * kd), rev)],
        out_specs=(pl.BlockSpec((CHUNK, 6 * kd), rev), pl.BlockSpec((CHUNK, kd), rev),
                   pl.BlockSpec((SUBLANES, kd), lambda i: (0, 0))),
        scratch_shapes=[pltpu.VMEM((B_HEADS, hv, hk), F32)], compiler_params=_cp("arbitrary"),
    )(p2, lrp, wa2p, balpha, gnorm, states, states, dyb)


def _merge_fwd(ya, yb, wpa, wpb, g):
    s, ka = ya.shape
    kb = yb.shape[1]
    d = wpa.shape[1]
    tm, tn = _tile(s, (1024, 512)), _tile(d, (512,))

    def body(ya_ref, yb_ref, wa_ref, wb_ref, g_ref, m_ref, pab_ref):
        pa = _dot(ya_ref[...], wa_ref[...])
        pb = _dot(yb_ref[...], wb_ref[...])
        m_ref[...] = (_sigmoid(g_ref[0].astype(F32)) * pa + _sigmoid(g_ref[1].astype(F32)) * pb).astype(BF16)
        pab_ref[0] = pa.astype(BF16)
        pab_ref[1] = pb.astype(BF16)

    st = pl.BlockSpec((2, tm, tn), lambda i, j: (0, i, j))
    return pl.pallas_call(
        body, name="merge_fwd",
        out_shape=(jax.ShapeDtypeStruct((s, d), BF16), jax.ShapeDtypeStruct((2, s, d), BF16)),
        grid=(s // tm, d // tn),
        in_specs=[pl.BlockSpec((tm, ka), lambda i, j: (i, 0)), pl.BlockSpec((tm, kb), lambda i, j: (i, 0)),
                  pl.BlockSpec((ka, tn), lambda i, j: (0, j)), pl.BlockSpec((kb, tn), lambda i, j: (0, j)), st],
        out_specs=(pl.BlockSpec((tm, tn), lambda i, j: (i, j)), st),
        compiler_params=_cp("parallel", "parallel"),
    )(ya, yb, wpa, wpb, g)


def _merge_bwd(dm, wmo, g, pab):
    s, d = dm.shape
    tm, tn = _tile(s, (1024, 512)), _tile(d, (512,))

    def body(dm_ref, w_ref, g_ref, pab_ref, dpab_ref, dg_ref):
        dmg = _dot(dm_ref[...], w_ref[...], "nt")
        for j in range(2):
            sg = _sigmoid(g_ref[j].astype(F32))
            dpab_ref[j] = (dmg * sg).astype(BF16)
            dg_ref[j] = (dmg * pab_ref[j].astype(F32) * (sg * (1.0 - sg))).astype(BF16)

    st = pl.BlockSpec((2, tm, tn), lambda i, j: (0, i, j))
    return pl.pallas_call(
        body, name="merge_bwd",
        out_shape=(jax.ShapeDtypeStruct((2, s, d), BF16), jax.ShapeDtypeStruct((2, s, d), BF16)),
        grid=(s // tm, d // tn),
        in_specs=[pl.BlockSpec((tm, d), lambda i, j: (i, 0)), pl.BlockSpec((tn, d), lambda i, j: (j, 0)), st, st],
        out_specs=(st, st), compiler_params=_cp("parallel", "parallel"),
    )(dm, wmo, g, pab)


def _split_mix_in(w_mix_in):
    aw = A_HEADS * A_HEAD_DIM
    d = w_mix_in.shape[0]
    kd = d // 4
    o1 = 3 * aw
    o2 = o1 + 6 * kd
    o3 = o2 + GATE_RANK
    w_lr = jnp.pad(w_mix_in[:, o2:o3], ((0, 0), (0, LANES - GATE_RANK)))
    return w_mix_in[:, :o1], w_mix_in[:, o1:o2], w_lr, w_mix_in[:, o3:]


def _heads_major(t, s):
    return t.reshape(s, A_HEADS, A_HEAD_DIM).transpose(1, 0, 2)


def _mix_forward(u2, wts, small):
    s, d = u2.shape
    w1, w2, w_lr, w_g, wpa, wpb, wmo = wts
    bias, wa2p, balpha, gnorm = small
    aw = A_HEADS * A_HEAD_DIM
    tm = _tile(s, (1024,))
    p1 = _mm("mix_in_a", "nn", u2, w1, (s, 3 * aw, d), tm=tm, tn=1024, tk=d, out_dtype=BF16)
    p2 = _mm("mix_in_b", "nn", u2, w2, (s, w2.shape[1], d), tm=tm, tn=1024, tk=d, out_dtype=BF16)
    lrp = _mm("mix_in_lr", "nn", u2, w_lr, (s, LANES, d), tm=tm, tn=LANES, tk=d, out_dtype=BF16)
    nbg = d // 1024
    g = _mm("mix_in_g", "nn", u2, w_g, (s, 2 * d, d), tm=tm, tn=1024, tk=d, out_dtype=BF16, out_shape=(2, s, d),
            o_spec=pl.BlockSpec((None, tm, 1024), lambda i, j, kk: (j // nbg, i, j % nbg)))
    q = _heads_major(p1[:, :aw], s)
    kp = jnp.pad(_heads_major(p1[:, aw:2 * aw], s), ((0, 0), (A_PAD, 0), (0, 0)))
    vp = jnp.pad(_heads_major(p1[:, 2 * aw:], s), ((0, 0), (A_PAD, 0), (0, 0)))
    ya = _attn_fwd(q, kp, vp, bias).transpose(1, 0, 2).reshape(s, aw)
    yb, states = _gla_fwd(p2, lrp, wa2p, balpha, gnorm)
    merged, pab = _merge_fwd(ya, yb, wpa, wpb, g)
    m = _mm("mix_out", "nn", merged, wmo, (s, d, d), tm=tm, tn=1024, tk=d)
    return m, (q, kp, vp, p2, lrp, states, ya, yb, g, pab, merged)


def _mix_backward(dm, u2, saved, wts, small):
    s, d = u2.shape
    w1, w2, w_lr, w_g, wpa, wpb, wmo = wts
    bias, wa2p, balpha, gnorm = small
    q, kp, vp, p2, lrp, states, ya, yb, g, pab, merged = saved
    aw = A_HEADS * A_HEAD_DIM
    kd = d // 4
    tm = _tile(s, (1024,))
    tks = _tile(s, (1024,))

    dwmo = _mm("mix_dw_out", "tn", merged, dm, (d, d, s), tm=1024, tn=1024, tk=tks)
    dpab, dg = _merge_bwd(dm, wmo, g, pab)
    sel = lambda j: pl.BlockSpec((None, tm, d), lambda i, jj, kk: (j, i, 0))
    dya = _mm("mix_dya", "nt", dpab, wpa, (s, aw, d), tm=tm, tn=1024, tk=d, out_dtype=BF16, a_spec=sel(0))
    dyb = _mm("mix_dyb", "nt", dpab, wpb, (s, 2 * kd, d), tm=tm, tn=1024, tk=d, out_dtype=BF16, a_spec=sel(1))
    selk = lambda j: pl.BlockSpec((None, tks, 1024), lambda i, jj, kk: (j, kk, jj))
    dwpa = _mm("mix_dwpa", "tn", ya, dpab, (aw, d, s), tm=1024, tn=1024, tk=tks, b_spec=selk(0))
    dwpb = _mm("mix_dwpb", "tn", yb, dpab, (2 * kd, d, s), tm=1024, tn=1024, tk=tks, b_spec=selk(1))

    do = _heads_major(dya, s)
    dq, dkp, dvp, dbias = _attn_bwd(q, kp, vp, bias, do)
    unheads = lambda t: t.transpose(1, 0, 2).reshape(s, aw)
    dp1 = jnp.concatenate([unheads(dq), unheads(dkp[:, A_PAD:].astype(BF16)), unheads(dvp[:, A_PAD:].astype(BF16))],
                          axis=1)
    dp2, dz, gsm = _gla_bwd(p2, lrp, wa2p, balpha, gnorm, states, dyb)
    dlrp = _mm("gla_dlr", "nt", dz, wa2p, (s, LANES, kd), tm=tm, tn=LANES, tk=kd, out_dtype=BF16)
    dwa2p = _mm("gla_dwa2", "tn", lrp, dz, (LANES, kd, s), tm=LANES, tn=kd, tk=tks)

    du = _mm("mix_du_a", "nt", dp1, w1, (s, d, 3 * aw), tm=tm, tn=1024, tk=1024)
    du = _mm("mix_du_b", "nt", dp2, w2, (s, d, 6 * kd), tm=tm, tn=1024, tk=1024, add=du)
    du = _mm("mix_du_lr", "nt", dlrp, w_lr, (s, d, LANES), tm=tm, tn=1024, tk=LANES, add=du)
    nkg = d // 1024
    du = _mm("mix_du_g", "nt", dg, w_g, (s, d, 2 * d), tm=tm, tn=1024, tk=1024, add=du,
             a_spec=pl.BlockSpec((None, tm, 1024), lambda i, j, kk: (kk // nkg, i, kk % nkg)))
    dw1 = _mm("mix_dw_a", "tn", u2, dp1, (d, 3 * aw, s), tm=1024, tn=1024, tk=tks)
    dw2 = _mm("mix_dw_b", "tn", u2, dp2, (d, 6 * kd, s), tm=1024, tn=1024, tk=tks)
    dwlr = _mm("mix_dw_lr", "tn", u2, dlrp, (d, LANES, s), tm=1024, tn=LANES, tk=tks)
    dwg = _mm("mix_dw_g", "tn", u2, dg, (d, 2 * d, s), tm=1024, tn=1024, tk=tks,
              b_spec=pl.BlockSpec((None, tks, 1024), lambda i, j, kk: (j // nkg, kk, j % nkg)))
    dw_mix_in = jnp.concatenate([dw1, dw2, dwlr[:, :GATE_RANK], dwg], axis=1)
    return du, (dw_mix_in, dwpa, dwpb, dwmo), (dbias, dwa2p[:GATE_RANK], gsm)


def _device_step(x, target, mod, big, small):
    s, d = x.shape
    row = lambda i: mod[i:i + 1]
    sh1, sc1, g1, sh2, sc2, g2, sh3, sc3, g3 = (row(i) for i in range(N_MOD))

    onehot = _rel_onehot()
    bias = _mm("rel_bias_expand", "nn", small["rel_bias"], onehot, (A_HEADS, CHUNK * A_BAND, REL_SIZE),
               tm=A_HEADS, tn=4608, tk=REL_SIZE, precision=HIGHEST).reshape(A_HEADS, CHUNK, A_BAND)
    wa2p = jnp.pad(small["w_alpha2"], ((0, LANES - GATE_RANK), (0, 0))).astype(BF16)
    mix_small = (bias, wa2p, small["b_alpha"], small["gla_norm_g"])
    mix_w = _split_mix_in(big["w_mix_in"]) + (big["w_proj_a"], big["w_proj_b"], big["w_mix_out"])

    u1 = _modulate("mod1", x, sh1, sc1)
    f1, sv1 = _ffn_forward("ffn1", u1, big["ffn1_w_in"], big["ffn1_w_out"])
    h1, u2 = _resid_ln_fwd("ln1_fwd", x, f1, g1, small["ln1_g"], small["ln1_b"], sh2, sc2, 0.5)
    m, svm = _mix_forward(u2, mix_w, mix_small)
    h2, u3 = _resid_ln_fwd("ln2_fwd", h1, m, g2, small["ln2_g"], small["ln2_b"], sh3, sc3, 1.0)
    f2, sv2 = _ffn_forward("ffn2", u3, big["ffn2_w_in"], big["ffn2_w_out"])

    dr3, df2, acc3 = _final_ln_loss_bwd("ln3_loss_bwd", h2, f2, target, g3, small["ln3_g"], small["ln3_b"], 0.5)
    du3, dw_in2, dw_out2 = _ffn_backward("ffn2", df2, u3, sv2, big["ffn2_w_in"], big["ffn2_w_out"])
    dr2, dmx, acc2 = _resid_ln_bwd("ln2_bwd", du3, dr3, h1, m, sc3, g2, small["ln2_g"], small["ln2_b"], 1.0)
    du2, (dw_mix_in, dwpa, dwpb, dwmo), (dbias, dwa2, gsm) = _mix_backward(dmx, u2, svm, mix_w, mix_small)
    dr1, df1, acc1 = _resid_ln_bwd("ln1_bwd", du2, dr2, x, f1, sc2, g1, small["ln1_g"], small["ln1_b"], 0.5)
    du1, dw_in1, dw_out1 = _ffn_backward("ffn1", df1, u1, sv1, big["ffn1_w_in"], big["ffn1_w_out"])
    grad_x, acc0 = _input_grad("input_grad", du1, dr1, x, sc1)

    drel = _mm("rel_bias_grad", "nt", dbias.reshape(A_HEADS, CHUNK * A_BAND), onehot,
               (A_HEADS, REL_SIZE, CHUNK * A_BAND), tm=A_HEADS, tn=REL_SIZE, tk=4608, precision=HIGHEST)
    loss = jnp.sum(acc3[ROW_LOSS])
    dmod = jnp.stack([acc0[ROW_DSH], acc0[ROW_DSC], acc1[ROW_DGATE], acc1[ROW_DSH], acc1[ROW_DSC], acc2[ROW_DGATE],
                      acc2[ROW_DSH], acc2[ROW_DSC], acc3[ROW_DGATE]])
    kd = d // 4
    big_grads = dict(ffn1_w_in=dw_in1, ffn1_w_out=dw_out1, w_mix_in=dw_mix_in, w_proj_a=dwpa, w_proj_b=dwpb,
                     w_mix_out=dwmo, ffn2_w_in=dw_in2, ffn2_w_out=dw_out2)
    small_grads = dict(ln1_g=acc1[ROW_DLN_G], ln1_b=acc1[ROW_DLN_B], ln2_g=acc2[ROW_DLN_G], ln2_b=acc2[ROW_DLN_B],
                       ln3_g=acc3[ROW_DLN_G], ln3_b=acc3[ROW_DLN_B], b_alpha=gsm[GLA_ROW_DBALPHA],
                       gla_norm_g=gsm[GLA_ROW_DGNORM, :kd // B_HEADS * 2], rel_bias=drel, w_alpha2=dwa2)
    return loss, grad_x, big_grads, small_grads, dmod


HBM_SPEC = pl.BlockSpec(memory_space=pl.ANY)


def _mesh_pos():
    return lax.axis_index("x"), lax.axis_index("y"), lax.axis_index("c")


def _other_chips(x, y):
    return [(1 - x, y), (x, 1 - y), (1 - x, 1 - y)]


def _remote(src, dst, send_sem, recv_sem, to):
    return pltpu.make_async_remote_copy(src_ref=src, dst_ref=dst, send_sem=send_sem, recv_sem=recv_sem,
                                        device_id=to, device_id_type=MESH)


def _allgather_rows(name, v):
    m_per, n = v.shape

    def body(x_ref, out_ref, send_sems, recv_sems, local_sem):
        x, y, c = _mesh_pos()
        me, sibling = (x, y, c), (x, y, 1 - c)
        chips = _other_chips(x, y)

        def rows(px, py, pc):
            return out_ref.at[pl.ds((4 * px + 2 * py + pc) * m_per, m_per), :]

        def copy(k, block, to, src=None):
            return _remote(rows(*block) if src is None else src, rows(*block), send_sems.at[k], recv_sems.at[k], to)

        mine = pltpu.make_async_copy(x_ref, rows(*me), local_sem)
        mine.start()
        first = [copy(0, me, sibling, src=x_ref)]
        first += [copy(1 + j, me, (*chip, c), src=x_ref) for j, chip in enumerate(chips)]
        for cp in first:
            cp.start()
        passed = [copy(4 + j, (*chip, c), sibling) for j, chip in enumerate(chips)]
        for j, chip in enumerate(chips):
            copy(1 + j, (*chip, c), me).wait_recv()
            passed[j].start()
        copy(0, sibling, me).wait_recv()
        for j, chip in enumerate(chips):
            copy(4 + j, (*chip, 1 - c), me).wait_recv()
        for cp in first + passed:
            cp.wait_send()
        mine.wait()

    return pl.pallas_call(
        body, name=name, out_shape=jax.ShapeDtypeStruct((N_DEV * m_per, n), v.dtype),
        in_specs=[pl.BlockSpec(memory_space=pltpu.VMEM)], out_specs=pl.BlockSpec(memory_space=pltpu.VMEM),
        scratch_shapes=[pltpu.SemaphoreType.DMA((7,)), pltpu.SemaphoreType.DMA((7,)), pltpu.SemaphoreType.DMA],
    )(v)


def _allgather_weights(shards):
    n = len(shards)

    def body(*refs):
        ins, outs = refs[:n], refs[n:2 * n]
        send_sems, recv_sems, local_sems = refs[2 * n:]
        x, y, c = _mesh_pos()
        sibling = (x, y, 1 - c)
        chips = _other_chips(x, y)
        j0 = 2 * x + y

        def half(ref, w, j, hc):
            hr = shards[w].shape[0] // 2
            return ref.at[j, pl.ds(hc * hr, hr), :]

        locals_, sends = [], []
        for w in range(n):
            hr = shards[w].shape[0] // 2
            lc = pltpu.make_async_copy(ins[w], outs[w].at[j0], local_sems.at[w])
            lc.start()
            locals_.append(lc)
            for r, chip in enumerate(chips):
                cp = _remote(ins[w].at[pl.ds(c * hr, hr), :], half(outs[w], w, j0, c), send_sems.at[w, r],
                             recv_sems.at[w, r], (*chip, c))
                cp.start()
                sends.append(cp)
        for w in range(n):
            for r, chip in enumerate(chips):
                jr = 2 * chip[0] + chip[1]
                landed = half(outs[w], w, jr, c)
                _remote(landed, landed, send_sems.at[w, r], recv_sems.at[w, r], (*chip, c)).wait_recv()
                fw = _remote(landed, landed, send_sems.at[w, 3 + r], recv_sems.at[w, 3 + r], sibling)
                fw.start()
                sends.append(fw)
        for w in range(n):
            for r, chip in enumerate(chips):
                jr = 2 * chip[0] + chip[1]
                got = half(outs[w], w, jr, 1 - c)
                _remote(got, got, send_sems.at[w, 3 + r], recv_sems.at[w, 3 + r], sibling).wait_recv()
        for cp in sends:
            cp.wait_send()
        for lc in locals_:
            lc.wait()

    return pl.pallas_call(
        body, name="allgather_weights",
        out_shape=[jax.ShapeDtypeStruct((N_CHIPS,) + sh.shape, sh.dtype) for sh in shards],
        in_specs=[HBM_SPEC] * n, out_specs=[HBM_SPEC] * n,
        scratch_shapes=[pltpu.SemaphoreType.DMA((n, 6)), pltpu.SemaphoreType.DMA((n, 6)),
                        pltpu.SemaphoreType.DMA((n,))],
    )(*shards)


def _exchange_halves(grads):
    n = len(grads)

    def body(*refs):
        ins, outs = refs[:n], refs[n:2 * n]
        send_sems, recv_sems = refs[2 * n:]
        x, y, c = _mesh_pos()
        sibling = (x, y, 1 - c)
        cps = []
        for w in range(n):
            hr = grads[w].shape[1] // 2
            cp = _remote(ins[w].at[:, pl.ds((1 - c) * hr, hr), :], outs[w], send_sems.at[w], recv_sems.at[w], sibling)
            cp.start()
            cps.append(cp)
        for cp in cps:
            cp.wait_recv()
        for cp in cps:
            cp.wait_send()

    return pl.pallas_call(
        body, name="grad_exchange_halves",
        out_shape=[jax.ShapeDtypeStruct((N_CHIPS, g.shape[1] // 2, g.shape[2]), g.dtype) for g in grads],
        in_specs=[HBM_SPEC] * n, out_specs=[HBM_SPEC] * n,
        scratch_shapes=[pltpu.SemaphoreType.DMA((n,)), pltpu.SemaphoreType.DMA((n,))],
    )(*grads)


def _scatter_to_owners(parts):
    n = len(parts)

    def body(*refs):
        ins, outs = refs[:n], refs[n:2 * n]
        send_sems, recv_sems, local_sems = refs[2 * n:]
        x, y, c = _mesh_pos()
        chips = _other_chips(x, y)
        j0 = 2 * x + y
        locals_, sends = [], []
        for w in range(n):
            lc = pltpu.make_async_copy(ins[w].at[j0], outs[w].at[j0], local_sems.at[w])
            lc.start()
            locals_.append(lc)
            for r, chip in enumerate(chips):
                jr = 2 * chip[0] + chip[1]
                cp = _remote(ins[w].at[jr], outs[w].at[j0], send_sems.at[w, r], recv_sems.at[w, r], (*chip, c))
                cp.start()
                sends.append(cp)
        for w in range(n):
            for r, chip in enumerate(chips):
                jr = 2 * chip[0] + chip[1]
                _remote(ins[w].at[jr], outs[w].at[jr], send_sems.at[w, r], recv_sems.at[w, r], (*chip, c)).wait_recv()
        for cp in sends:
            cp.wait_send()
        for lc in locals_:
            lc.wait()

    return pl.pallas_call(
        body, name="grad_scatter_to_owners",
        out_shape=[jax.ShapeDtypeStruct(p.shape, p.dtype) for p in parts],
        in_specs=[HBM_SPEC] * n, out_specs=[HBM_SPEC] * n,
        scratch_shapes=[pltpu.SemaphoreType.DMA((n, 3)), pltpu.SemaphoreType.DMA((n, 3)),
                        pltpu.SemaphoreType.DMA((n,))],
    )(*parts)


def _share_reduced_halves(halves):
    n = len(halves)

    def body(*refs):
        ins, outs = refs[:n], refs[n:2 * n]
        send_sems, recv_sems, local_sems = refs[2 * n:]
        x, y, c = _mesh_pos()
        sibling = (x, y, 1 - c)
        locals_, sends = [], []
        for w in range(n):
            hr = halves[w].shape[0]
            mine = outs[w].at[pl.ds(c * hr, hr), :]
            lc = pltpu.make_async_copy(ins[w], mine, local_sems.at[w])
            lc.start()
            locals_.append(lc)
            cp = _remote(ins[w], mine, send_sems.at[w], recv_sems.at[w], sibling)
            cp.start()
            sends.append(cp)
        for w in range(n):
            hr = halves[w].shape[0]
            theirs = outs[w].at[pl.ds((1 - c) * hr, hr), :]
            _remote(ins[w], theirs, send_sems.at[w], recv_sems.at[w], sibling).wait_recv()
        for cp in sends:
            cp.wait_send()
        for lc in locals_:
            lc.wait()

    return pl.pallas_call(
        body, name="grad_share_reduced",
        out_shape=[jax.ShapeDtypeStruct((2 * h.shape[0], h.shape[1]), h.dtype) for h in halves],
        in_specs=[HBM_SPEC] * n, out_specs=[HBM_SPEC] * n,
        scratch_shapes=[pltpu.SemaphoreType.DMA((n,)), pltpu.SemaphoreType.DMA((n,)), pltpu.SemaphoreType.DMA((n,))],
    )(*halves)


TILE_BYTES = 2 * 1024 * 1024


def _row_tile(rows, cols, itemsize=4):
    for t in (1024, 512, 256, 128, 64, 32, 16, 8):
        if rows % t == 0 and t * cols * itemsize <= TILE_BYTES:
            return t
    return rows


def _pair_sum(name, g, recv, core):
    _, hr, cols = recv.shape
    tr = _row_tile(hr, cols)
    nb = hr // tr

    def body(c_ref, g_ref, r_ref, o_ref):
        o_ref[...] = (g_ref[...] + r_ref[...]).astype(BF16)

    grid_spec = pltpu.PrefetchScalarGridSpec(
        num_scalar_prefetch=1, grid=(N_CHIPS, nb),
        in_specs=[pl.BlockSpec((None, tr, cols), lambda j, i, cr: (j, cr[0] * nb + i, 0)),
                  pl.BlockSpec((None, tr, cols), lambda j, i, cr: (j, i, 0))],
        out_specs=pl.BlockSpec((None, tr, cols), lambda j, i, cr: (j, i, 0)))
    return pl.pallas_call(body, name=name, out_shape=jax.ShapeDtypeStruct(recv.shape, BF16), grid_spec=grid_spec,
                          compiler_params=_cp("parallel", "parallel"))(core, g, recv)


def _quad_sum(name, parts):
    _, hr, cols = parts.shape
    tr = _row_tile(hr, cols)

    def body(p_ref, o_ref):
        o_ref[...] = ((p_ref[0].astype(F32) + p_ref[1].astype(F32)) + p_ref[2].astype(F32)) + p_ref[3].astype(F32)

    return pl.pallas_call(
        body, name=name, out_shape=jax.ShapeDtypeStruct((hr, cols), F32), grid=(hr // tr,),
        in_specs=[pl.BlockSpec((N_CHIPS, tr, cols), lambda i: (0, i, 0))],
        out_specs=pl.BlockSpec((tr, cols), lambda i: (i, 0)), compiler_params=_cp("parallel"),
    )(parts)


def _device_sum(name, gathered):
    def body(g_ref, o_ref):
        total = g_ref[0]
        for k in range(1, N_DEV):
            total = total + g_ref[k]
        o_ref[...] = total

    return pl.pallas_call(body, name=name, out_shape=jax.ShapeDtypeStruct(gathered.shape[1:], F32))(gathered)


def _adamw(name, w, g, m, v):
    rows, cols = w.shape
    tr = _row_tile(rows, cols)
    bc1 = 1.0 - ADAM_B1 ** ADAM_STEP
    bc2 = 1.0 - ADAM_B2 ** ADAM_STEP

    def body(w_ref, g_ref, m_ref, v_ref, d_ref, mo_ref, vo_ref):
        gv = g_ref[...]
        mn = ADAM_B1 * m_ref[...] + (1.0 - ADAM_B1) * gv
        vn = ADAM_B2 * v_ref[...] + (1.0 - ADAM_B2) * (gv * gv)
        mo_ref[...] = mn
        vo_ref[...] = vn
        d_ref[...] = -ADAM_LR * ((mn / bc1) / (jnp.sqrt(vn / bc2) + ADAM_EPS) + ADAM_WD * w_ref[...])

    spec = pl.BlockSpec((tr, cols), lambda i: (i, 0))
    return pl.pallas_call(
        body, name=name, out_shape=[jax.ShapeDtypeStruct((rows, cols), F32)] * 3, grid=(rows // tr,),
        in_specs=[spec] * 4, out_specs=[spec] * 3, compiler_params=_cp("parallel"),
    )(w, g, m, v)


WEIGHTS = ["w_ada", "b_ada", "ffn1_w_in", "ffn1_w_out", "ln1_g", "ln1_b", "w_mix_in", "rel_bias", "w_alpha2",
           "b_alpha", "gla_norm_g", "w_proj_a", "w_proj_b", "w_mix_out", "ln2_g", "ln2_b", "ffn2_w_in", "ffn2_w_out",
           "ln3_g", "ln3_b"]
BIG = {"ffn1_w_in": True, "ffn1_w_out": False, "w_mix_in": True, "w_proj_a": True, "w_proj_b": True,
       "w_mix_out": False, "ffn2_w_in": True, "ffn2_w_out": False}
SMALL = ["ln1_g", "ln1_b", "ln2_g", "ln2_b", "ln3_g", "ln3_b", "b_alpha", "gla_norm_g", "rel_bias", "w_alpha2"]


def _pad_rows(vec, rows=SUBLANES):
    per = -(-vec.shape[0] // (rows * LANES)) * LANES
    return jnp.pad(vec, (0, rows * per - vec.shape[0])).reshape(rows, per)


def _silu(v):
    return v * _sigmoid(v)


def _step(args):
    x_pos, y_pos, c_pos = _mesh_pos()
    chip = 2 * x_pos + y_pos
    dev = 4 * x_pos + 2 * y_pos + c_pos
    w = {k: args[k][0] for k in WEIGHTS}
    mom = {k: args["m_" + k][0] for k in WEIGHTS}
    vel = {k: args["v_" + k][0] for k in WEIGHTS}
    x = args["x"][0]
    target = args["loss_target"][0]
    s, d = x.shape
    kd = d // 4
    rel_sh = w["rel_bias"].shape[1]
    wa2_sh = w["w_alpha2"].shape[1]
    ada_sh = w["w_ada"].shape[1]

    n_rel, n_wa2 = A_HEADS * rel_sh, GATE_RANK * wa2_sh
    packed = _pad_rows(jnp.concatenate([args["c"].reshape(-1), w["rel_bias"].reshape(-1), w["w_alpha2"].reshape(-1)]))
    got = _allgather_rows("gather_small_inputs", packed).reshape(N_DEV, -1)
    c_all = got[:, :d]
    per_chip = got[0::2]
    rel_bias = per_chip[:, d:d + n_rel].reshape(N_CHIPS, A_HEADS, rel_sh).transpose(1, 0, 2).reshape(A_HEADS, -1)
    w_alpha2 = per_chip[:, d + n_rel:d + n_rel + n_wa2].reshape(N_CHIPS, GATE_RANK, wa2_sh).transpose(1, 0, 2)
    w_alpha2 = w_alpha2.reshape(GATE_RANK, -1)

    b_shard = lax.dynamic_slice(w["b_ada"], (chip * ada_sh,), (ada_sh,))
    mod_shard = _mm("ada_fwd", "nn", c_all, w["w_ada"], (N_DEV, ada_sh, d), tm=N_DEV, tn=_tile(ada_sh, (512, 128)),
                    tk=d, precision=HIGHEST, a_fn=_silu, add=jnp.broadcast_to(b_shard[None], (N_DEV, ada_sh)))
    mod_all = _allgather_rows("gather_mod", mod_shard).reshape(N_DEV, N_DEV, ada_sh)[0::2]
    mod_all = mod_all.transpose(1, 0, 2).reshape(N_DEV, N_MOD * d)
    mod = lax.dynamic_index_in_dim(mod_all, dev, 0, keepdims=False).reshape(N_MOD, d)

    names = list(BIG)
    gathered = _allgather_weights([w[k].astype(BF16) for k in names])
    big = {}
    for k, g in zip(names, gathered):
        _, r, cc = g.shape
        big[k] = g.transpose(1, 0, 2).reshape(r, N_CHIPS * cc) if BIG[k] else g.reshape(N_CHIPS * r, cc)

    small = dict(rel_bias=rel_bias, w_alpha2=w_alpha2, b_alpha=w["b_alpha"][None], gla_norm_g=w["gla_norm_g"][None])
    for k in ("ln1_g", "ln1_b", "ln2_g", "ln2_b", "ln3_g", "ln3_b"):
        small[k] = w[k][None]
    loss_local, grad_x, big_grads, small_grads, dmod = _device_step(x, target, mod, big, small)
    loss = lax.psum(loss_local, ("x", "y", "c"))

    flat = jnp.concatenate([small_grads[k].reshape(-1) for k in SMALL] + [dmod.reshape(-1)])
    n_small = flat.shape[0] - N_MOD * d
    packed = _pad_rows(flat)
    all_small = _allgather_rows("gather_small_grads", packed).reshape(N_DEV, SUBLANES, -1)
    summed = _device_sum("small_grad_sum", all_small).reshape(-1)
    dmod_all = all_small.reshape(N_DEV, -1)[:, n_small:n_small + N_MOD * d]
    dmod_shard = lax.dynamic_slice(dmod_all, (0, chip * ada_sh), (N_DEV, ada_sh))
    grads = {"b_ada": summed[n_small:n_small + N_MOD * d]}
    off = 0
    for k in SMALL:
        size = small_grads[k].size
        grads[k] = summed[off:off + size].reshape(small_grads[k].shape)
        off += size
    grads["rel_bias"] = lax.dynamic_slice(grads["rel_bias"], (0, chip * rel_sh), (A_HEADS, rel_sh))
    grads["w_alpha2"] = lax.dynamic_slice(grads["w_alpha2"], (0, chip * wa2_sh), (GATE_RANK, wa2_sh))
    grads["w_ada"] = _mm("ada_bwd", "nn", jnp.pad(c_all.T, ((0, 0), (0, LANES - N_DEV))),
                         jnp.pad(dmod_shard, ((0, LANES - N_DEV), (0, 0))), (d, ada_sh, LANES), tm=_tile(d, (1024,)),
                         tn=_tile(ada_sh, (512, 128)), tk=LANES, precision=HIGHEST, a_fn=_silu)

    stacked = []
    for k in names:
        g = big_grads[k]
        cc = w[k].shape[1]
        r = w[k].shape[0]
        stacked.append(g.reshape(r, N_CHIPS, cc).transpose(1, 0, 2) if BIG[k] else g.reshape(N_CHIPS, r, cc))
    core = c_pos.astype(jnp.int32).reshape(1)
    from_sibling = _exchange_halves(stacked)
    chip_sums = [_pair_sum(f"pair_sum_{k}", g, r, core) for k, g, r in zip(names, stacked, from_sibling)]
    landed = _scatter_to_owners(chip_sums)
    halves = [_quad_sum(f"quad_sum_{k}", p) for k, p in zip(names, landed)]
    for k, g in zip(names, _share_reduced_halves(halves)):
        grads[k] = g

    delta, new_m, new_v = {}, {}, {}
    for k in ["w_ada"] + names:
        delta[k], new_m[k], new_v[k] = _adamw(f"adamw_{k}", w[k], grads[k], mom[k], vel[k])
    tiny = ["b_ada"] + SMALL
    pack = lambda src: _pad_rows(jnp.concatenate([src[k].reshape(-1) for k in tiny]), rows=1).reshape(-1, LANES)
    outs = _adamw("adamw_small", pack(w), pack(grads), pack(mom), pack(vel))
    off = 0
    for k in tiny:
        size = w[k].size
        for dst, src in zip((delta, new_m, new_v), outs):
            dst[k] = src.reshape(-1)[off:off + size].reshape(w[k].shape)
        off += size

    lead = lambda t: t[None]
    return (loss, lead(grad_x), *[lead(grads[k]) for k in WEIGHTS], *[lead(delta[k]) for k in WEIGHTS],
            *[lead(new_m[k]) for k in WEIGHTS], *[lead(new_v[k]) for k in WEIGHTS])


def kernel(x, c, w_ada, b_ada, ffn1_w_in, ffn1_w_out, ln1_g, ln1_b, w_mix_in, rel_bias, w_alpha2, b_alpha, gla_norm_g, w_proj_a, w_proj_b, w_mix_out, ln2_g, ln2_b, ffn2_w_in, ffn2_w_out, ln3_g, ln3_b, loss_target, m_w_ada, m_b_ada, m_ffn1_w_in, m_ffn1_w_out, m_ln1_g, m_ln1_b, m_w_mix_in, m_rel_bias, m_w_alpha2, m_b_alpha, m_gla_norm_g, m_w_proj_a, m_w_proj_b, m_w_mix_out, m_ln2_g, m_ln2_b, m_ffn2_w_in, m_ffn2_w_out, m_ln3_g, m_ln3_b, v_w_ada, v_b_ada, v_ffn1_w_in, v_ffn1_w_out, v_ln1_g, v_ln1_b, v_w_mix_in, v_rel_bias, v_w_alpha2, v_b_alpha, v_gla_norm_g, v_w_proj_a, v_w_proj_b, v_w_mix_out, v_ln2_g, v_ln2_b, v_ffn2_w_in, v_ffn2_w_out, v_ln3_g, v_ln3_b):
    return _step(dict(locals()))
```

```python
import functools

import jax
import jax.numpy as jnp
from jax import lax
from jax.experimental import pallas as pl
from jax.experimental.pallas import tpu as pltpu

F32 = jnp.float32
BF16 = jnp.bfloat16
MESH = pl.DeviceIdType.MESH
HIGHEST = lax.Precision.HIGHEST

VMEM_LIMIT_BYTES = 56 * 1024 * 1024
LANES = 128
SUBLANES = 8

CHUNK = 64
A_HEADS = 16
A_HEAD_DIM = 64
A_PAST_CHUNKS = 8
A_BAND = (A_PAST_CHUNKS + 1) * CHUNK
A_PAD = A_PAST_CHUNKS * CHUNK
REL_CLIP = 256
REL_SIZE = REL_CLIP + CHUNK
B_HEADS = 4
GATE_RANK = 16
GATE_TAU = 16.0
N_MOD = 9
DEPTH = 1
ALPHA = (2.0 * DEPTH) ** 0.25
LN_EPS = 1e-5
RMS_EPS = 1e-6
ADAM_LR = 0.001
ADAM_B1 = 0.9
ADAM_B2 = 0.999
ADAM_EPS = 1e-08
ADAM_WD = 0.01
ADAM_STEP = 10
NEG_BIG = -1e30

N_CHIPS = 4
N_DEV = 8


def _cp(*sem):
    return pltpu.CompilerParams(dimension_semantics=sem, vmem_limit_bytes=VMEM_LIMIT_BYTES)


def _tile(n, prefs):
    for t in prefs:
        if t <= n and n % t == 0:
            return t
    return n


_DIMS = {"nn": (((1,), (0,)), ((), ())), "nt": (((1,), (1,)), ((), ())), "tn": (((0,), (0,)), ((), ()))}


def _dot(a, b, mode="nn", precision=None):
    return lax.dot_general(a, b, _DIMS[mode], precision=precision, preferred_element_type=F32)


def _sigmoid(x):
    return 1.0 / (1.0 + jnp.exp(-x))


def _mm(name, mode, a, b, mnk, *, tm, tn, tk, out_dtype=F32, precision=None, a_spec=None, b_spec=None,
        out_shape=None, o_spec=None, add=None, a_fn=None):
    m, n, k = mnk
    assert m % tm == 0 and n % tn == 0 and k % tk == 0, (name, mnk, tm, tn, tk)
    nk = k // tk
    if a_spec is None:
        a_spec = {"nn": pl.BlockSpec((tm, tk), lambda i, j, kk: (i, kk)),
                  "nt": pl.BlockSpec((tm, tk), lambda i, j, kk: (i, kk)),
                  "tn": pl.BlockSpec((tk, tm), lambda i, j, kk: (kk, i))}[mode]
    if b_spec is None:
        b_spec = {"nn": pl.BlockSpec((tk, tn), lambda i, j, kk: (kk, j)),
                  "nt": pl.BlockSpec((tn, tk), lambda i, j, kk: (j, kk)),
                  "tn": pl.BlockSpec((tk, tn), lambda i, j, kk: (kk, j))}[mode]
    if o_spec is None:
        o_spec = pl.BlockSpec((tm, tn), lambda i, j, kk: (i, j))
    if out_shape is None:
        out_shape = (m, n)
    has_add = add is not None

    def body(*refs):
        a_ref, b_ref = refs[0], refs[1]
        add_ref = refs[2] if has_add else None
        o_ref = refs[3] if has_add else refs[2]
        av = a_ref[...]
        if a_fn is not None:
            av = a_fn(av)
        part = _dot(av, b_ref[...], mode, precision)

        def finish(total):
            if has_add:
                total = total + add_ref[...]
            o_ref[...] = total.astype(out_dtype)

        if nk == 1:
            finish(part)
        else:
            acc_ref = refs[-1]
            kk = pl.program_id(2)

            @pl.when(kk == 0)
            def _():
                acc_ref[...] = part

            @pl.when(kk > 0)
            def _():
                acc_ref[...] += part

            @pl.when(kk == nk - 1)
            def _():
                finish(acc_ref[...])

    in_specs = [a_spec, b_spec]
    operands = [a, b]
    if has_add:
        in_specs.append(pl.BlockSpec((tm, tn), lambda i, j, kk: (i, j)))
        operands.append(add)
    return pl.pallas_call(
        body, name=name, out_shape=jax.ShapeDtypeStruct(out_shape, out_dtype), grid=(m // tm, n // tn, nk),
        in_specs=in_specs, out_specs=o_spec,
        scratch_shapes=[pltpu.VMEM((tm, tn), F32)] if nk > 1 else [],
        compiler_params=_cp("parallel", "parallel", "arbitrary"),
    )(*operands)


def _row_spec(tr, d):
    return pl.BlockSpec((tr, d), lambda i: (i, 0))


def _vec_spec(d, rows=1):
    return pl.BlockSpec((rows, d), lambda i: (0, 0))


def _modulate(name, x, sh, sc):
    s, d = x.shape
    tr = _tile(s, (512, 256))

    def body(x_ref, sh_ref, sc_ref, o_ref):
        o_ref[...] = (x_ref[...] * (1.0 + sc_ref[...]) + sh_ref[...]).astype(BF16)

    return pl.pallas_call(
        body, name=name, out_shape=jax.ShapeDtypeStruct((s, d), BF16), grid=(s // tr,),
        in_specs=[_row_spec(tr, d), _vec_spec(d), _vec_spec(d)], out_specs=_row_spec(tr, d),
        compiler_params=_cp("parallel"),
    )(x, sh, sc)


def _ln_stats(r):
    mu = jnp.mean(r, axis=-1, keepdims=True)
    xc = r - mu
    var = jnp.mean(xc * xc, axis=-1, keepdims=True)
    rstd = lax.rsqrt(var + LN_EPS)
    return xc * rstd, rstd


def _resid_ln_fwd(name, x, f, gate, ln_g, ln_b, sh_n, sc_n, coef):
    s, d = x.shape
    tr = _tile(s, (256,))

    def body(x_ref, f_ref, gate_ref, g_ref, b_ref, sh_ref, sc_ref, h_ref, u_ref):
        r = ALPHA * x_ref[...] + (coef * gate_ref[...]) * f_ref[...]
        xhat, _ = _ln_stats(r)
        h = xhat * g_ref[...] + b_ref[...]
        h_ref[...] = h
        u_ref[...] = (h * (1.0 + sc_ref[...]) + sh_ref[...]).astype(BF16)

    return pl.pallas_call(
        body, name=name, out_shape=(jax.ShapeDtypeStruct((s, d), F32), jax.ShapeDtypeStruct((s, d), BF16)),
        grid=(s // tr,), in_specs=[_row_spec(tr, d), _row_spec(tr, d)] + [_vec_spec(d)] * 5,
        out_specs=(_row_spec(tr, d), _row_spec(tr, d)), compiler_params=_cp("parallel"),
    )(x, f, gate, ln_g, ln_b, sh_n, sc_n)


ROW_DSC, ROW_DSH, ROW_DLN_G, ROW_DLN_B, ROW_DGATE, ROW_LOSS = 0, 1, 2, 3, 4, 5


def _ln_bwd_core(dy, xhat, rstd, ln_g):
    dxhat = dy * ln_g
    m1 = jnp.mean(dxhat, axis=-1, keepdims=True)
    m2 = jnp.mean(dxhat * xhat, axis=-1, keepdims=True)
    return rstd * (dxhat - m1 - xhat * m2)


def _colsum(v):
    return jnp.sum(v, axis=0, keepdims=True)


def _final_ln_loss_bwd(name, x, f, target, gate, ln_g, ln_b, coef):
    s, d = x.shape
    tr = _tile(s, (256,))
    inv_d = 1.0 / d

    def body(x_ref, f_ref, t_ref, gate_ref, g_ref, b_ref, dr_ref, df_ref, acc_ref):
        @pl.when(pl.program_id(0) == 0)
        def _():
            acc_ref[...] = jnp.zeros_like(acc_ref)

        fv = f_ref[...]
        r = ALPHA * x_ref[...] + (coef * gate_ref[...]) * fv
        xhat, rstd = _ln_stats(r)
        h = xhat * g_ref[...] + b_ref[...]
        err = h - t_ref[...]
        dy = err * inv_d
        dr = _ln_bwd_core(dy, xhat, rstd, g_ref[...])
        dr_ref[...] = dr
        df_ref[...] = ((coef * gate_ref[...]) * dr).astype(BF16)
        acc_ref[ROW_DLN_G:ROW_DLN_G + 1, :] += _colsum(dy * xhat)
        acc_ref[ROW_DLN_B:ROW_DLN_B + 1, :] += _colsum(dy)
        acc_ref[ROW_DGATE:ROW_DGATE + 1, :] += _colsum((coef * dr) * fv)
        acc_ref[ROW_LOSS:ROW_LOSS + 1, :] += _colsum(err * err) * (0.5 * inv_d)

    return pl.pallas_call(
        body, name=name,
        out_shape=(jax.ShapeDtypeStruct((s, d), F32), jax.ShapeDtypeStruct((s, d), BF16),
                   jax.ShapeDtypeStruct((SUBLANES, d), F32)),
        grid=(s // tr,), in_specs=[_row_spec(tr, d)] * 3 + [_vec_spec(d)] * 3,
        out_specs=(_row_spec(tr, d), _row_spec(tr, d), _vec_spec(d, SUBLANES)),
        compiler_params=_cp("arbitrary"),
    )(x, f, target, gate, ln_g, ln_b)


def _resid_ln_bwd(name, du_n, dr_n, x, f, sc_n, gate, ln_g, ln_b, coef):
    s, d = x.shape
    tr = _tile(s, (256,))

    def body(du_ref, drn_ref, x_ref, f_ref, sc_ref, gate_ref, g_ref, b_ref, dr_ref, df_ref, acc_ref):
        @pl.when(pl.program_id(0) == 0)
        def _():
            acc_ref[...] = jnp.zeros_like(acc_ref)

        fv = f_ref[...]
        du = du_ref[...]
        r = ALPHA * x_ref[...] + (coef * gate_ref[...]) * fv
        xhat, rstd = _ln_stats(r)
        h = xhat * g_ref[...] + b_ref[...]
        dy = du * (1.0 + sc_ref[...]) + ALPHA * drn_ref[...]
        dr = _ln_bwd_core(dy, xhat, rstd, g_ref[...])
        dr_ref[...] = dr
        df_ref[...] = ((coef * gate_ref[...]) * dr).astype(BF16)
        acc_ref[ROW_DSC:ROW_DSC + 1, :] += _colsum(du * h)
        acc_ref[ROW_DSH:ROW_DSH + 1, :] += _colsum(du)
        acc_ref[ROW_DLN_G:ROW_DLN_G + 1, :] += _colsum(dy * xhat)
        acc_ref[ROW_DLN_B:ROW_DLN_B + 1, :] += _colsum(dy)
        acc_ref[ROW_DGATE:ROW_DGATE + 1, :] += _colsum((coef * dr) * fv)

    return pl.pallas_call(
        body, name=name,
        out_shape=(jax.ShapeDtypeStruct((s, d), F32), jax.ShapeDtypeStruct((s, d), BF16),
                   jax.ShapeDtypeStruct((SUBLANES, d), F32)),
        grid=(s // tr,), in_specs=[_row_spec(tr, d)] * 4 + [_vec_spec(d)] * 4,
        out_specs=(_row_spec(tr, d), _row_spec(tr, d), _vec_spec(d, SUBLANES)),
        compiler_params=_cp("arbitrary"),
    )(du_n, dr_n, x, f, sc_n, gate, ln_g, ln_b)


def _input_grad(name, du, dr, x, sc):
    s, d = x.shape
    tr = _tile(s, (256,))

    def body(du_ref, dr_ref, x_ref, sc_ref, gx_ref, acc_ref):
        @pl.when(pl.program_id(0) == 0)
        def _():
            acc_ref[...] = jnp.zeros_like(acc_ref)

        du = du_ref[...]
        gx_ref[...] = du * (1.0 + sc_ref[...]) + ALPHA * dr_ref[...]
        acc_ref[ROW_DSC:ROW_DSC + 1, :] += _colsum(du * x_ref[...])
        acc_ref[ROW_DSH:ROW_DSH + 1, :] += _colsum(du)

    return pl.pallas_call(
        body, name=name,
        out_shape=(jax.ShapeDtypeStruct((s, d), F32), jax.ShapeDtypeStruct((SUBLANES, d), F32)),
        grid=(s // tr,), in_specs=[_row_spec(tr, d)] * 3 + [_vec_spec(d)],
        out_specs=(_row_spec(tr, d), _vec_spec(d, SUBLANES)), compiler_params=_cp("arbitrary"),
    )(du, dr, x, sc)


def _ffn_in_fwd(name, u, w_in):
    s, d = u.shape
    cs = w_in.shape[2]
    f = 2 * cs
    tm, tn = _tile(s, (2048, 1024, 512)), _tile(cs, (256, 128))
    nb = f // tn
    nbs = cs // tn

    def body(u_ref, wa_ref, wb_ref, ab_ref, act_ref):
        uv = u_ref[...]
        a = _dot(uv, wa_ref[...])
        b = _dot(uv, wb_ref[...])
        ab_ref[0] = a.astype(BF16)
        ab_ref[1] = b.astype(BF16)
        act_ref[...] = (a * _sigmoid(a) * b).astype(BF16)

    return pl.pallas_call(
        body, name=name,
        out_shape=(jax.ShapeDtypeStruct((2, s, f), BF16), jax.ShapeDtypeStruct((s, f), BF16)),
        grid=(s // tm, nb),
        in_specs=[pl.BlockSpec((tm, d), lambda i, j: (i, 0)),
                  pl.BlockSpec((None, d, tn), lambda i, j: (j // nbs, 0, j % nbs)),
                  pl.BlockSpec((None, d, tn), lambda i, j: (2 + j // nbs, 0, j % nbs))],
        out_specs=(pl.BlockSpec((2, tm, tn), lambda i, j: (0, i, j)), pl.BlockSpec((tm, tn), lambda i, j: (i, j))),
        compiler_params=_cp("parallel", "parallel"),
    )(u, w_in, w_in)


def _ffn_out_bwd(name, df, w_out, ab):
    s, d = df.shape
    f = w_out.shape[0]
    tm, tn = _tile(s, (1024, 512)), _tile(f, (512, 256, 128))

    def body(df_ref, w_ref, ab_ref, dab_ref):
        dact = _dot(df_ref[...], w_ref[...], "nt")
        a = ab_ref[0].astype(F32)
        b = ab_ref[1].astype(F32)
        sg = _sigmoid(a)
        dab_ref[0] = (dact * b * (sg * (1.0 + a * (1.0 - sg)))).astype(BF16)
        dab_ref[1] = (dact * (a * sg)).astype(BF16)

    return pl.pallas_call(
        body, name=name, out_shape=jax.ShapeDtypeStruct((2, s, f), BF16), grid=(s // tm, f // tn),
        in_specs=[pl.BlockSpec((tm, d), lambda i, j: (i, 0)), pl.BlockSpec((tn, d), lambda i, j: (j, 0)),
                  pl.BlockSpec((2, tm, tn), lambda i, j: (0, i, j))],
        out_specs=pl.BlockSpec((2, tm, tn), lambda i, j: (0, i, j)),
        compiler_params=_cp("parallel", "parallel"),
    )(df, w_out, ab)


def _ffn_forward(tag, u, w_in, w_out):
    s, d = u.shape
    f = w_out.shape[0]
    ab, act = _ffn_in_fwd(f"{tag}_in_fwd", u, w_in)
    out = _mm(f"{tag}_out_fwd", "nn", act, w_out, (s, d, f), tm=_tile(s, (1024,)), tn=_tile(d, (1024,)),
              tk=_tile(f, (1408, 512, 128)))
    return out, (ab, act)


def _ffn_backward(tag, df, u, saved, w_in, w_out):
    ab, act = saved
    s, d = u.shape
    f = w_out.shape[0]
    dab = _ffn_out_bwd(f"{tag}_out_bwd", df, w_out, ab)
    tmf = _tile(f, (1408, 512, 128))
    dw_out = _mm(f"{tag}_dw_out", "tn", act, df, (f, d, s), tm=tmf, tn=_tile(d, (1024,)), tk=_tile(s, (1024,)))
    cs = w_in.shape[2]
    tk = _tile(cs, (1408, 256, 128))
    nkh, nks = f // tk, cs // tk
    tmd = _tile(d, (1024,))
    du = _mm(f"{tag}_du", "nt", dab, w_in, (s, d, 2 * f), tm=_tile(s, (1024,)), tn=tmd, tk=tk,
             a_spec=pl.BlockSpec((None, _tile(s, (1024,)), tk), lambda i, j, kk: (kk // nkh, i, kk % nkh)),
             b_spec=pl.BlockSpec((None, tmd, tk), lambda i, j, kk: (kk // nks, j, kk % nks)))
    tn = tk
    nbh, nbs = f // tn, cs // tn
    tks = _tile(s, (1024,))
    dw_in = _mm(f"{tag}_dw_in", "tn", u, dab, (d, 2 * f, s), tm=tmd, tn=tn, tk=tks,
                b_spec=pl.BlockSpec((None, tks, tn), lambda i, j, kk: (j // nbh, kk, j % nbh)),
                out_shape=(N_CHIPS, d, cs), o_spec=pl.BlockSpec((None, tmd, tn), lambda i, j, kk: (j // nbs, i, j % nbs)))
    return du, dw_in, dw_out


def _attn_scores(qc, kw, bias, key0):
    sc = _dot(qc, kw, "nt") * (A_HEAD_DIM ** -0.5) + bias
    ks = lax.broadcasted_iota(jnp.int32, sc.shape, 1)
    sc = jnp.where(key0 + ks >= 0, sc, NEG_BIG)
    m = jnp.max(sc, axis=-1, keepdims=True)
    p = jnp.exp(sc - m)
    return p, jnp.sum(p, axis=-1, keepdims=True)


def _attn_fwd(q, kp, vp, bias):
    h, s, dh = q.shape
    tq = _tile(s, (256, 128, 64))
    ncq = tq // CHUNK

    def body(q_ref, k_ref, v_ref, b_ref, o_ref):
        base = pl.program_id(1) * tq
        bias_v = b_ref[...]
        for c in range(ncq):
            start = pl.multiple_of(base + c * CHUNK, CHUNK)
            kw = k_ref[pl.ds(start, A_BAND), :]
            vw = v_ref[pl.ds(start, A_BAND), :]
            p, l = _attn_scores(q_ref[c * CHUNK:(c + 1) * CHUNK, :], kw, bias_v, start - A_PAD)
            o = _dot((p / l).astype(BF16), vw)
            o_ref[c * CHUNK:(c + 1) * CHUNK, :] = o.astype(BF16)

    return pl.pallas_call(
        body, name="attn_fwd", out_shape=jax.ShapeDtypeStruct((h, s, dh), BF16), grid=(h, s // tq),
        in_specs=[pl.BlockSpec((None, tq, dh), lambda hh, i: (hh, i, 0)),
                  pl.BlockSpec((None, s + A_PAD, dh), lambda hh, i: (hh, 0, 0)),
                  pl.BlockSpec((None, s + A_PAD, dh), lambda hh, i: (hh, 0, 0)),
                  pl.BlockSpec((None, CHUNK, A_BAND), lambda hh, i: (hh, 0, 0))],
        out_specs=pl.BlockSpec((None, tq, dh), lambda hh, i: (hh, i, 0)),
        compiler_params=_cp("parallel", "arbitrary"),
    )(q, kp, vp, bias)


def _attn_bwd(q, kp, vp, bias, do):
    h, s, dh = q.shape
    tq = _tile(s, (256, 128, 64))
    ncq = tq // CHUNK
    scale = A_HEAD_DIM ** -0.5

    def body(q_ref, k_ref, v_ref, b_ref, do_ref, dq_ref, dk_ref, dv_ref, db_ref):
        @pl.when(pl.program_id(1) == 0)
        def _():
            dk_ref[...] = jnp.zeros_like(dk_ref)
            dv_ref[...] = jnp.zeros_like(dv_ref)
            db_ref[...] = jnp.zeros_like(db_ref)

        base = pl.program_id(1) * tq
        bias_v = b_ref[...]
        for c in range(ncq):
            start = pl.multiple_of(base + c * CHUNK, CHUNK)
            rows = slice(c * CHUNK, (c + 1) * CHUNK)
            kw = k_ref[pl.ds(start, A_BAND), :]
            vw = v_ref[pl.ds(start, A_BAND), :]
            qc = q_ref[rows, :]
            doc = do_ref[rows, :]
            p, l = _attn_scores(qc, kw, bias_v, start - A_PAD)
            p = p / l
            dp = _dot(doc, vw, "nt")
            delta = jnp.sum(p * dp, axis=-1, keepdims=True)
            ds = p * (dp - delta)
            db_ref[...] += ds
            dsb = (ds * scale).astype(BF16)
            dq_ref[rows, :] = _dot(dsb, kw).astype(BF16)
            dk_ref[pl.ds(start, A_BAND), :] += _dot(dsb, qc, "tn")
            dv_ref[pl.ds(start, A_BAND), :] += _dot(p.astype(BF16), doc, "tn")

    kv_spec = pl.BlockSpec((None, s + A_PAD, dh), lambda hh, i: (hh, 0, 0))
    q_spec = pl.BlockSpec((None, tq, dh), lambda hh, i: (hh, i, 0))
    b_spec = pl.BlockSpec((None, CHUNK, A_BAND), lambda hh, i: (hh, 0, 0))
    return pl.pallas_call(
        body, name="attn_bwd",
        out_shape=(jax.ShapeDtypeStruct((h, s, dh), BF16), jax.ShapeDtypeStruct((h, s + A_PAD, dh), F32),
                   jax.ShapeDtypeStruct((h, s + A_PAD, dh), F32), jax.ShapeDtypeStruct((h, CHUNK, A_BAND), F32)),
        grid=(h, s // tq), in_specs=[q_spec, kv_spec, kv_spec, b_spec, q_spec],
        out_specs=(q_spec, kv_spec, kv_spec, b_spec), compiler_params=_cp("parallel", "arbitrary"),
    )(q, kp, vp, bias, do)


def _rel_onehot():
    qi = jnp.arange(CHUNK)[:, None]
    ks = jnp.arange(A_BAND)[None, :]
    idx = (jnp.clip(ks - A_PAD - qi, -REL_CLIP, CHUNK - 1) + REL_CLIP).reshape(1, CHUNK * A_BAND)
    return (jnp.arange(REL_SIZE)[:, None] == idx).astype(F32)


def _gla_gate(lr, wa2, balpha):
    z = _dot(lr, wa2) + balpha
    la = (jnp.minimum(z, 0.0) - jnp.log(1.0 + jnp.exp(-jnp.abs(z)))) * (1.0 / GATE_TAU)
    row = lax.broadcasted_iota(jnp.int32, (CHUNK, CHUNK), 0)
    col = lax.broadcasted_iota(jnp.int32, (CHUNK, CHUNK), 1)
    cum = _dot((row >= col).astype(F32), la, precision=HIGHEST)
    return z, la, cum


def _gla_dims(p2):
    kd = p2.shape[1] // 6
    hk = kd // B_HEADS
    hv = 2 * hk
    return kd, hk, hv


def _gla_fwd(p2, lrp, wa2p, balpha, gnorm):
    s = p2.shape[0]
    kd, hk, hv = _gla_dims(p2)
    nc = s // CHUNK
    qscale = hk ** -0.5

    def body(p_ref, lr_ref, wa_ref, ba_ref, gn_ref, yb_ref, st_ref, state):
        @pl.when(pl.program_id(0) == 0)
        def _():
            state[...] = jnp.zeros_like(state)

        _, _, cum = _gla_gate(lr_ref[...], wa_ref[...], ba_ref[...])
        last = cum[CHUNK - 1:CHUNK, :]
        e = jnp.exp(last - cum)
        dch = jnp.exp(last)
        gn = gn_ref[...]
        for hh in range(B_HEADS):
            ks = slice(hh * hk, (hh + 1) * hk)
            q = p_ref[:, hh * hk:(hh + 1) * hk].astype(F32)
            k = p_ref[:, kd + hh * hk:kd + (hh + 1) * hk].astype(F32)
            v = p_ref[:, 2 * kd + hh * hv:2 * kd + (hh + 1) * hv]
            rg = p_ref[:, 4 * kd + hh * hv:4 * kd + (hh + 1) * hv].astype(F32)
            kdec = (k * e[:, ks]).astype(BF16)
            st = state[hh] * dch[:, ks] + _dot(v, kdec, "tn")
            state[hh] = st
            st_ref[hh] = st
            o = _dot((q * qscale).astype(BF16), st.astype(BF16), "nt")
            rinv = lax.rsqrt(jnp.mean(o * o, axis=-1, keepdims=True) + RMS_EPS)
            yb_ref[:, hh * hv:(hh + 1) * hv] = ((o * rinv * gn) * (rg * _sigmoid(rg))).astype(BF16)

    return pl.pallas_call(
        body, name="gla_fwd",
        out_shape=(jax.ShapeDtypeStruct((s, 2 * kd), BF16), jax.ShapeDtypeStruct((nc, B_HEADS, hv, hk), F32)),
        grid=(nc,),
        in_specs=[pl.BlockSpec((CHUNK, 6 * kd), lambda i: (i, 0)), pl.BlockSpec((CHUNK, LANES), lambda i: (i, 0)),
                  pl.BlockSpec((LANES, kd), lambda i: (0, 0)), pl.BlockSpec((1, kd), lambda i: (0, 0)),
                  pl.BlockSpec((1, hv), lambda i: (0, 0))],
        out_specs=(pl.BlockSpec((CHUNK, 2 * kd), lambda i: (i, 0)),
                   pl.BlockSpec((None, B_HEADS, hv, hk), lambda i: (i, 0, 0, 0))),
        scratch_shapes=[pltpu.VMEM((B_HEADS, hv, hk), F32)], compiler_params=_cp("arbitrary"),
    )(p2, lrp, wa2p, balpha, gnorm)


GLA_ROW_DBALPHA, GLA_ROW_DGNORM = 0, 1


def _gla_bwd(p2, lrp, wa2p, balpha, gnorm, states, dyb):
    s = p2.shape[0]
    kd, hk, hv = _gla_dims(p2)
    nc = s // CHUNK
    qscale = hk ** -0.5

    def body(p_ref, lr_ref, wa_ref, ba_ref, gn_ref, st_ref, sp_ref, dy_ref, dp_ref, dz_ref, sm_ref, gcar):
        i = pl.program_id(0)

        @pl.when(i == 0)
        def _():
            gcar[...] = jnp.zeros_like(gcar)
            sm_ref[...] = jnp.zeros_like(sm_ref)

        has_prev = (i < nc - 1).astype(F32)
        z, _, cum = _gla_gate(lr_ref[...], wa_ref[...], ba_ref[...])
        last = cum[CHUNK - 1:CHUNK, :]
        e = jnp.exp(last - cum)
        dch = jnp.exp(last)
        sgn = _sigmoid(-z) * (1.0 / GATE_TAU)
        gn = gn_ref[...]
        row = lax.broadcasted_iota(jnp.int32, (CHUNK, CHUNK), 0)
        col = lax.broadcasted_iota(jnp.int32, (CHUNK, CHUNK), 1)
        tri_strict = (row > col).astype(F32)
        for hh in range(B_HEADS):
            ks = slice(hh * hk, (hh + 1) * hk)
            q = p_ref[:, hh * hk:(hh + 1) * hk].astype(F32)
            k = p_ref[:, kd + hh * hk:kd + (hh + 1) * hk].astype(F32)
            v = p_ref[:, 2 * kd + hh * hv:2 * kd + (hh + 1) * hv]
            rg = p_ref[:, 4 * kd + hh * hv:4 * kd + (hh + 1) * hv].astype(F32)
            kdecf = k * e[:, ks]
            kdec = kdecf.astype(BF16)
            st16 = st_ref[hh].astype(BF16)
            qs = (q * qscale).astype(BF16)
            o = _dot(qs, st16, "nt")
            rinv = lax.rsqrt(jnp.mean(o * o, axis=-1, keepdims=True) + RMS_EPS)
            dy = dy_ref[:, hh * hv:(hh + 1) * hv].astype(F32)
            sg = _sigmoid(rg)
            onorm = o * rinv
            drg = dy * (onorm * gn) * (sg * (1.0 + rg * (1.0 - sg)))
            dob = dy * (rg * sg)
            sm_ref[GLA_ROW_DGNORM:GLA_ROW_DGNORM + 1, 0:hv] += _colsum(dob * onorm)
            t = dob * gn
            do = rinv * (t - onorm * jnp.mean(t * onorm, axis=-1, keepdims=True))
            do16 = do.astype(BF16)
            dq = _dot(do16, st16) * qscale
            gt = _dot(do16, qs, "tn") + gcar[hh]
            gcar[hh] = gt * dch[:, ks]
            dd = _colsum(gt * sp_ref[hh]) * has_prev
            gt16 = gt.astype(BF16)
            dkdec = _dot(v, gt16)
            dv = _dot(kdec, gt16, "nt")
            dla = dd * dch[:, ks] + _dot(tri_strict, dkdec * kdecf, precision=HIGHEST)
            dzh = dla * sgn[:, ks]
            sm_ref[GLA_ROW_DBALPHA:GLA_ROW_DBALPHA + 1, hh * hk:(hh + 1) * hk] += _colsum(dzh)
            dz_ref[:, hh * hk:(hh + 1) * hk] = dzh.astype(BF16)
            dp_ref[:, hh * hk:(hh + 1) * hk] = dq.astype(BF16)
            dp_ref[:, kd + hh * hk:kd + (hh + 1) * hk] = (dkdec * e[:, ks]).astype(BF16)
            dp_ref[:, 2 * kd + hh * hv:2 * kd + (hh + 1) * hv] = dv.astype(BF16)
            dp_ref[:, 4 * kd + hh * hv:4 * kd + (hh + 1) * hv] = drg.astype(BF16)

    rev = lambda i: (nc - 1 - i, 0)
    return pl.pallas_call(
        body, name="gla_bwd",
        out_shape=(jax.ShapeDtypeStruct((s, 6 * kd), BF16), jax.ShapeDtypeStruct((s, kd), BF16),
                   jax.ShapeDtypeStruct((SUBLANES, kd), F32)),
        grid=(nc,),
        in_specs=[pl.BlockSpec((CHUNK, 6 * kd), rev), pl.BlockSpec((CHUNK, LANES), rev),
                  pl.BlockSpec((LANES, kd), lambda i: (0, 0)), pl.BlockSpec((1, kd), lambda i: (0, 0)),
                  pl.BlockSpec((1, hv), lambda i: (0, 0)),
                  pl.BlockSpec((None, B_HEADS, hv, hk), lambda i: (nc - 1 - i, 0, 0, 0)),
                  pl.BlockSpec((None, B_HEADS, hv, hk), lambda i: (jnp.maximum(nc - 2 - i, 0), 0, 0, 0)),
                  pl.BlockSpec((CHUNK, 2 * kd), rev)],
        out_specs=(pl.BlockSpec((CHUNK, 6 * kd), rev), pl.BlockSpec((CHUNK, kd), rev),
                   pl.BlockSpec((SUBLANES, kd), lambda i: (0, 0))),
        scratch_shapes=[pltpu.VMEM((B_HEADS, hv, hk), F32)], compiler_params=_cp("arbitrary"),
    )(p2, lrp, wa2p, balpha, gnorm, states, states, dyb)


def _merge_fwd(ya, yb, wpa, wpb, g):
    s, ka = ya.shape
    kb = yb.shape[1]
    d = wpa.shape[1]
    tm, tn = _tile(s, (1024, 512)), _tile(d, (512,))

    def body(ya_ref, yb_ref, wa_ref, wb_ref, g_ref, m_ref, pab_ref):
        pa = _dot(ya_ref[...], wa_ref[...])
        pb = _dot(yb_ref[...], wb_ref[...])
        m_ref[...] = (_sigmoid(g_ref[0].astype(F32)) * pa + _sigmoid(g_ref[1].astype(F32)) * pb).astype(BF16)
        pab_ref[0] = pa.astype(BF16)
        pab_ref[1] = pb.astype(BF16)

    st = pl.BlockSpec((2, tm, tn), lambda i, j: (0, i, j))
    return pl.pallas_call(
        body, name="merge_fwd",
        out_shape=(jax.ShapeDtypeStruct((s, d), BF16), jax.ShapeDtypeStruct((2, s, d), BF16)),
        grid=(s // tm, d // tn),
        in_specs=[pl.BlockSpec((tm, ka), lambda i, j: (i, 0)), pl.BlockSpec((tm, kb), lambda i, j: (i, 0)),
                  pl.BlockSpec((ka, tn), lambda i, j: (0, j)), pl.BlockSpec((kb, tn), lambda i, j: (0, j)), st],
        out_specs=(pl.BlockSpec((tm, tn), lambda i, j: (i, j)), st),
        compiler_params=_cp("parallel", "parallel"),
    )(ya, yb, wpa, wpb, g)


def _merge_bwd(dm, wmo, g, pab):
    s, d = dm.shape
    tm, tn = _tile(s, (1024, 512)), _tile(d, (512,))

    def body(dm_ref, w_ref, g_ref, pab_ref, dpab_ref, dg_ref):
        dmg = _dot(dm_ref[...], w_ref[...], "nt")
        for j in range(2):
            sg = _sigmoid(g_ref[j].astype(F32))
            dpab_ref[j] = (dmg * sg).astype(BF16)
            dg_ref[j] = (dmg * pab_ref[j].astype(F32) * (sg * (1.0 - sg))).astype(BF16)

    st = pl.BlockSpec((2, tm, tn), lambda i, j: (0, i, j))
    return pl.pallas_call(
        body, name="merge_bwd",
        out_shape=(jax.ShapeDtypeStruct((2, s, d), BF16), jax.ShapeDtypeStruct((2, s, d), BF16)),
        grid=(s // tm, d // tn),
        in_specs=[pl.BlockSpec((tm, d), lambda i, j: (i, 0)), pl.BlockSpec((tn, d), lambda i, j: (j, 0)), st, st],
        out_specs=(st, st), compiler_params=_cp("parallel", "parallel"),
    )(dm, wmo, g, pab)


def _split_mix_in(w_mix_in):
    aw = A_HEADS * A_HEAD_DIM
    d = w_mix_in.shape[0]
    kd = d // 4
    o1 = 3 * aw
    o2 = o1 + 6 * kd
    o3 = o2 + GATE_RANK
    w_lr = jnp.pad(w_mix_in[:, o2:o3], ((0, 0), (0, LANES - GATE_RANK)))
    return w_mix_in[:, :o1], w_mix_in[:, o1:o2], w_lr, w_mix_in[:, o3:]


def _heads_major(t, s):
    return t.reshape(s, A_HEADS, A_HEAD_DIM).transpose(1, 0, 2)


def _mix_forward(u2, wts, small):
    s, d = u2.shape
    w1, w2, w_lr, w_g, wpa, wpb, wmo = wts
    bias, wa2p, balpha, gnorm = small
    aw = A_HEADS * A_HEAD_DIM
    tm = _tile(s, (1024,))
    p1 = _mm("mix_in_a", "nn", u2, w1, (s, 3 * aw, d), tm=tm, tn=1024, tk=d, out_dtype=BF16)
    p2 = _mm("mix_in_b", "nn", u2, w2, (s, w2.shape[1], d), tm=tm, tn=1024, tk=d, out_dtype=BF16)
    lrp = _mm("mix_in_lr", "nn", u2, w_lr, (s, LANES, d), tm=tm, tn=LANES, tk=d, out_dtype=BF16)
    nbg = d // 1024
    g = _mm("mix_in_g", "nn", u2, w_g, (s, 2 * d, d), tm=tm, tn=1024, tk=d, out_dtype=BF16, out_shape=(2, s, d),
            o_spec=pl.BlockSpec((None, tm, 1024), lambda i, j, kk: (j // nbg, i, j % nbg)))
    q = _heads_major(p1[:, :aw], s)
    kp = jnp.pad(_heads_major(p1[:, aw:2 * aw], s), ((0, 0), (A_PAD, 0), (0, 0)))
    vp = jnp.pad(_heads_major(p1[:, 2 * aw:], s), ((0, 0), (A_PAD, 0), (0, 0)))
    ya = _attn_fwd(q, kp, vp, bias).transpose(1, 0, 2).reshape(s, aw)
    yb, states = _gla_fwd(p2, lrp, wa2p, balpha, gnorm)
    merged, pab = _merge_fwd(ya, yb, wpa, wpb, g)
    m = _mm("mix_out", "nn", merged, wmo, (s, d, d), tm=tm, tn=1024, tk=d)
    return m, (q, kp, vp, p2, lrp, states, ya, yb, g, pab, merged)


def _mix_backward(dm, u2, saved, wts, small):
    s, d = u2.shape
    w1, w2, w_lr, w_g, wpa, wpb, wmo = wts
    bias, wa2p, balpha, gnorm = small
    q, kp, vp, p2, lrp, states, ya, yb, g, pab, merged = saved
    aw = A_HEADS * A_HEAD_DIM
    kd = d // 4
    tm = _tile(s, (1024,))
    tks = _tile(s, (1024,))

    dwmo = _mm("mix_dw_out", "tn", merged, dm, (d, d, s), tm=1024, tn=1024, tk=tks)
    dpab, dg = _merge_bwd(dm, wmo, g, pab)
    sel = lambda j: pl.BlockSpec((None, tm, d), lambda i, jj, kk: (j, i, 0))
    dya = _mm("mix_dya", "nt", dpab, wpa, (s, aw, d), tm=tm, tn=1024, tk=d, out_dtype=BF16, a_spec=sel(0))
    dyb = _mm("mix_dyb", "nt", dpab, wpb, (s, 2 * kd, d), tm=tm, tn=1024, tk=d, out_dtype=BF16, a_spec=sel(1))
    selk = lambda j: pl.BlockSpec((None, tks, 1024), lambda i, jj, kk: (j, kk, jj))
    dwpa = _mm("mix_dwpa", "tn", ya, dpab, (aw, d, s), tm=1024, tn=1024, tk=tks, b_spec=selk(0))
    dwpb = _mm("mix_dwpb", "tn", yb, dpab, (2 * kd, d, s), tm=1024, tn=1024, tk=tks, b_spec=selk(1))

    do = _heads_major(dya, s)
    dq, dkp, dvp, dbias = _attn_bwd(q, kp, vp, bias, do)
    unheads = lambda t: t.transpose(1, 0, 2).reshape(s, aw)
    dp1 = jnp.concatenate([unheads(dq), unheads(dkp[:, A_PAD:].astype(BF16)), unheads(dvp[:, A_PAD:].astype(BF16))],
                          axis=1)
    dp2, dz, gsm = _gla_bwd(p2, lrp, wa2p, balpha, gnorm, states, dyb)
    dlrp = _mm("gla_dlr", "nt", dz, wa2p, (s, LANES, kd), tm=tm, tn=LANES, tk=kd, out_dtype=BF16)
    dwa2p = _mm("gla_dwa2", "tn", lrp, dz, (LANES, kd, s), tm=LANES, tn=kd, tk=tks)

    du = _mm("mix_du_a", "nt", dp1, w1, (s, d, 3 * aw), tm=tm, tn=1024, tk=1024)
    du = _mm("mix_du_b", "nt", dp2, w2, (s, d, 6 * kd), tm=tm, tn=1024, tk=1024, add=du)
    du = _mm("mix_du_lr", "nt", dlrp, w_lr, (s, d, LANES), tm=tm, tn=1024, tk=LANES, add=du)
    nkg = d // 1024
    du = _mm("mix_du_g", "nt", dg, w_g, (s, d, 2 * d), tm=tm, tn=1024, tk=1024, add=du,
             a_spec=pl.BlockSpec((None, tm, 1024), lambda i, j, kk: (kk // nkg, i, kk % nkg)))
    dw1 = _mm("mix_dw_a", "tn", u2, dp1, (d, 3 * aw, s), tm=1024, tn=1024, tk=tks)
    dw2 = _mm("mix_dw_b", "tn", u2, dp2, (d, 6 * kd, s), tm=1024, tn=1024, tk=tks)
    dwlr = _mm("mix_dw_lr", "tn", u2, dlrp, (d, LANES, s), tm=1024, tn=LANES, tk=tks)
    dwg = _mm("mix_dw_g", "tn", u2, dg, (d, 2 * d, s), tm=1024, tn=1024, tk=tks,
              b_spec=pl.BlockSpec((None, tks, 1024), lambda i, j, kk: (j // nkg, kk, j % nkg)))
    dw_mix_in = jnp.concatenate([dw1, dw2, dwlr[:, :GATE_RANK], dwg], axis=1)
    return du, (dw_mix_in, dwpa, dwpb, dwmo), (dbias, dwa2p[:GATE_RANK], gsm)


def _device_step(x, target, mod, big, small):
    s, d = x.shape
    row = lambda i: mod[i:i + 1]
    sh1, sc1, g1, sh2, sc2, g2, sh3, sc3, g3 = (row(i) for i in range(N_MOD))

    onehot = _rel_onehot()
    bias = _mm("rel_bias_expand", "nn", small["rel_bias"], onehot, (A_HEADS, CHUNK * A_BAND, REL_SIZE),
               tm=A_HEADS, tn=4608, tk=REL_SIZE, precision=HIGHEST).reshape(A_HEADS, CHUNK, A_BAND)
    wa2p = jnp.pad(small["w_alpha2"], ((0, LANES - GATE_RANK), (0, 0))).astype(BF16)
    mix_small = (bias, wa2p, small["b_alpha"], small["gla_norm_g"])
    mix_w = _split_mix_in(big["w_mix_in"]) + (big["w_proj_a"], big["w_proj_b"], big["w_mix_out"])

    u1 = _modulate("mod1", x, sh1, sc1)
    f1, sv1 = _ffn_forward("ffn1", u1, big["ffn1_w_in"], big["ffn1_w_out"])
    h1, u2 = _resid_ln_fwd("ln1_fwd", x, f1, g1, small["ln1_g"], small["ln1_b"], sh2, sc2, 0.5)
    m, svm = _mix_forward(u2, mix_w, mix_small)
    h2, u3 = _resid_ln_fwd("ln2_fwd", h1, m, g2, small["ln2_g"], small["ln2_b"], sh3, sc3, 1.0)
    f2, sv2 = _ffn_forward("ffn2", u3, big["ffn2_w_in"], big["ffn2_w_out"])

    dr3, df2, acc3 = _final_ln_loss_bwd("ln3_loss_bwd", h2, f2, target, g3, small["ln3_g"], small["ln3_b"], 0.5)
    du3, dw_in2, dw_out2 = _ffn_backward("ffn2", df2, u3, sv2, big["ffn2_w_in"], big["ffn2_w_out"])
    dr2, dmx, acc2 = _resid_ln_bwd("ln2_bwd", du3, dr3, h1, m, sc3, g2, small["ln2_g"], small["ln2_b"], 1.0)
    du2, (dw_mix_in, dwpa, dwpb, dwmo), (dbias, dwa2, gsm) = _mix_backward(dmx, u2, svm, mix_w, mix_small)
    dr1, df1, acc1 = _resid_ln_bwd("ln1_bwd", du2, dr2, x, f1, sc2, g1, small["ln1_g"], small["ln1_b"], 0.5)
    du1, dw_in1, dw_out1 = _ffn_backward("ffn1", df1, u1, sv1, big["ffn1_w_in"], big["ffn1_w_out"])
    grad_x, acc0 = _input_grad("input_grad", du1, dr1, x, sc1)

    drel = _mm("rel_bias_grad", "nt", dbias.reshape(A_HEADS, CHUNK * A_BAND), onehot,
               (A_HEADS, REL_SIZE, CHUNK * A_BAND), tm=A_HEADS, tn=REL_SIZE, tk=4608, precision=HIGHEST)
    loss = jnp.sum(acc3[ROW_LOSS])
    dmod = jnp.stack([acc0[ROW_DSH], acc0[ROW_DSC], acc1[ROW_DGATE], acc1[ROW_DSH], acc1[ROW_DSC], acc2[ROW_DGATE],
                      acc2[ROW_DSH], acc2[ROW_DSC], acc3[ROW_DGATE]])
    kd = d // 4
    big_grads = dict(ffn1_w_in=dw_in1, ffn1_w_out=dw_out1, w_mix_in=dw_mix_in, w_proj_a=dwpa, w_proj_b=dwpb,
                     w_mix_out=dwmo, ffn2_w_in=dw_in2, ffn2_w_out=dw_out2)
    small_grads = dict(ln1_g=acc1[ROW_DLN_G], ln1_b=acc1[ROW_DLN_B], ln2_g=acc2[ROW_DLN_G], ln2_b=acc2[ROW_DLN_B],
                       ln3_g=acc3[ROW_DLN_G], ln3_b=acc3[ROW_DLN_B], b_alpha=gsm[GLA_ROW_DBALPHA],
                       gla_norm_g=gsm[GLA_ROW_DGNORM, :kd // B_HEADS * 2], rel_bias=drel, w_alpha2=dwa2)
    return loss, grad_x, big_grads, small_grads, dmod


HBM_SPEC = pl.BlockSpec(memory_space=pl.ANY)


def _mesh_pos():
    return lax.axis_index("x"), lax.axis_index("y"), lax.axis_index("c")


def _other_chips(x, y):
    return [(1 - x, y), (x, 1 - y), (1 - x, 1 - y)]


def _remote(src, dst, send_sem, recv_sem, to):
    return pltpu.make_async_remote_copy(src_ref=src, dst_ref=dst, send_sem=send_sem, recv_sem=recv_sem,
                                        device_id=to, device_id_type=MESH)


def _allgather_rows(name, v):
    m_per, n = v.shape

    def body(x_ref, out_ref, send_sems, recv_sems, local_sem):
        x, y, c = _mesh_pos()
        me, sibling = (x, y, c), (x, y, 1 - c)
        chips = _other_chips(x, y)

        def rows(px, py, pc):
            return out_ref.at[pl.ds((4 * px + 2 * py + pc) * m_per, m_per), :]

        def copy(k, block, to, src=None):
            return _remote(rows(*block) if src is None else src, rows(*block), send_sems.at[k], recv_sems.at[k], to)

        mine = pltpu.make_async_copy(x_ref, rows(*me), local_sem)
        mine.start()
        first = [copy(0, me, sibling, src=x_ref)]
        first += [copy(1 + j, me, (*chip, c), src=x_ref) for j, chip in enumerate(chips)]
        for cp in first:
            cp.start()
        passed = [copy(4 + j, (*chip, c), sibling) for j, chip in enumerate(chips)]
        for j, chip in enumerate(chips):
            copy(1 + j, (*chip, c), me).wait_recv()
            passed[j].start()
        copy(0, sibling, me).wait_recv()
        for j, chip in enumerate(chips):
            copy(4 + j, (*chip, 1 - c), me).wait_recv()
        for cp in first + passed:
            cp.wait_send()
        mine.wait()

    return pl.pallas_call(
        body, name=name, out_shape=jax.ShapeDtypeStruct((N_DEV * m_per, n), v.dtype),
        in_specs=[pl.BlockSpec(memory_space=pltpu.VMEM)], out_specs=pl.BlockSpec(memory_space=pltpu.VMEM),
        scratch_shapes=[pltpu.SemaphoreType.DMA((7,)), pltpu.SemaphoreType.DMA((7,)), pltpu.SemaphoreType.DMA],
    )(v)


def _allgather_weights(shards):
    n = len(shards)

    def body(*refs):
        ins, outs = refs[:n], refs[n:2 * n]
        send_sems, recv_sems = refs[2 * n:]
        x, y, c = _mesh_pos()
        sibling = (x, y, 1 - c)
        chips = _other_chips(x, y)
        j0 = 2 * x + y

        def half(ref, w, j, hc):
            hr = shards[w].shape[0] // 2
            return ref.at[j, pl.ds(hc * hr, hr), :]

        sends = []
        for w in range(n):
            hr = shards[w].shape[0] // 2
            for r, chip in enumerate(chips):
                cp = _remote(ins[w].at[pl.ds(c * hr, hr), :], half(outs[w], w, j0, c), send_sems.at[w, r],
                             recv_sems.at[w, r], (*chip, c))
                cp.start()
                sends.append(cp)
        for w in range(n):
            for r, chip in enumerate(chips):
                jr = 2 * chip[0] + chip[1]
                landed = half(outs[w], w, jr, c)
                _remote(landed, landed, send_sems.at[w, r], recv_sems.at[w, r], (*chip, c)).wait_recv()
                fw = _remote(landed, landed, send_sems.at[w, 3 + r], recv_sems.at[w, 3 + r], sibling)
                fw.start()
                sends.append(fw)
        for w in range(n):
            for r, chip in enumerate(chips):
                jr = 2 * chip[0] + chip[1]
                got = half(outs[w], w, jr, 1 - c)
                _remote(got, got, send_sems.at[w, 3 + r], recv_sems.at[w, 3 + r], sibling).wait_recv()
        for cp in sends:
            cp.wait_send()

    return pl.pallas_call(
        body, name="allgather_weights",
        out_shape=[jax.ShapeDtypeStruct((N_CHIPS,) + sh.shape, sh.dtype) for sh in shards],
        in_specs=[HBM_SPEC] * n, out_specs=[HBM_SPEC] * n,
        scratch_shapes=[pltpu.SemaphoreType.DMA((n, 6)), pltpu.SemaphoreType.DMA((n, 6))],
    )(*shards)


def _exchange_halves(grads):
    n = len(grads)

    def body(*refs):
        ins, outs = refs[:n], refs[n:2 * n]
        send_sems, recv_sems = refs[2 * n:]
        x, y, c = _mesh_pos()
        sibling = (x, y, 1 - c)
        cps = []
        for w in range(n):
            hr = grads[w].shape[1] // 2
            cp = _remote(ins[w].at[:, pl.ds((1 - c) * hr, hr), :], outs[w], send_sems.at[w], recv_sems.at[w], sibling)
            cp.start()
            cps.append(cp)
        for cp in cps:
            cp.wait_recv()
        for cp in cps:
            cp.wait_send()

    return pl.pallas_call(
        body, name="grad_exchange_halves",
        out_shape=[jax.ShapeDtypeStruct((N_CHIPS, g.shape[1] // 2, g.shape[2]), g.dtype) for g in grads],
        in_specs=[HBM_SPEC] * n, out_specs=[HBM_SPEC] * n,
        scratch_shapes=[pltpu.SemaphoreType.DMA((n,)), pltpu.SemaphoreType.DMA((n,))],
    )(*grads)


def _scatter_to_owners(parts):
    n = len(parts)

    def body(*refs):
        ins, outs = refs[:n], refs[n:2 * n]
        send_sems, recv_sems = refs[2 * n:]
        x, y, c = _mesh_pos()
        chips = _other_chips(x, y)
        sends = []
        for w in range(n):
            for r, chip in enumerate(chips):
                jr = 2 * chip[0] + chip[1]
                cp = _remote(ins[w].at[jr], outs[w].at[r], send_sems.at[w, r], recv_sems.at[w, r], (*chip, c))
                cp.start()
                sends.append(cp)
        for cp in sends:
            cp.wait_recv()
        for cp in sends:
            cp.wait_send()

    return pl.pallas_call(
        body, name="grad_scatter_to_owners",
        out_shape=[jax.ShapeDtypeStruct((3,) + p.shape[1:], p.dtype) for p in parts],
        in_specs=[HBM_SPEC] * n, out_specs=[HBM_SPEC] * n,
        scratch_shapes=[pltpu.SemaphoreType.DMA((n, 3)), pltpu.SemaphoreType.DMA((n, 3))],
    )(*parts)


def _share_reduced_halves(fulls):
    n = len(fulls)

    def body(*refs):
        ins, outs = refs[:n], refs[n:2 * n]
        send_sems, recv_sems = refs[2 * n:]
        x, y, c = _mesh_pos()
        sibling = (x, y, 1 - c)
        sends = []
        for w in range(n):
            hr = fulls[w].shape[0] // 2
            mine = pl.ds(c * hr, hr)
            cp = _remote(ins[w].at[mine, :], outs[w].at[mine, :], send_sems.at[w], recv_sems.at[w], sibling)
            cp.start()
            sends.append(cp)
        for w in range(n):
            hr = fulls[w].shape[0] // 2
            theirs = outs[w].at[pl.ds((1 - c) * hr, hr), :]
            _remote(theirs, theirs, send_sems.at[w], recv_sems.at[w], sibling).wait_recv()
        for cp in sends:
            cp.wait_send()

    return pl.pallas_call(
        body, name="grad_share_reduced",
        out_shape=[jax.ShapeDtypeStruct(h.shape, h.dtype) for h in fulls],
        in_specs=[HBM_SPEC] * n, out_specs=[HBM_SPEC] * n, input_output_aliases={w: w for w in range(n)},
        scratch_shapes=[pltpu.SemaphoreType.DMA((n,)), pltpu.SemaphoreType.DMA((n,))],
    )(*fulls)


TILE_BYTES = 2 * 1024 * 1024


def _row_tile(rows, cols, itemsize=4):
    for t in (1024, 512, 256, 128, 64, 32, 16, 8):
        if rows % t == 0 and t * cols * itemsize <= TILE_BYTES:
            return t
    return rows


def _pair_sum(name, g, recv, core):
    _, hr, cols = recv.shape
    tr = _row_tile(hr, cols)
    nb = hr // tr

    def body(c_ref, g_ref, r_ref, o_ref):
        o_ref[...] = (g_ref[...] + r_ref[...]).astype(BF16)

    grid_spec = pltpu.PrefetchScalarGridSpec(
        num_scalar_prefetch=1, grid=(N_CHIPS, nb),
        in_specs=[pl.BlockSpec((None, tr, cols), lambda j, i, cr: (j, cr[0] * nb + i, 0)),
                  pl.BlockSpec((None, tr, cols), lambda j, i, cr: (j, i, 0))],
        out_specs=pl.BlockSpec((None, tr, cols), lambda j, i, cr: (j, i, 0)))
    return pl.pallas_call(body, name=name, out_shape=jax.ShapeDtypeStruct(recv.shape, BF16), grid_spec=grid_spec,
                          compiler_params=_cp("parallel", "parallel"))(core, g, recv)


def _quad_sum(name, own, landed, chip_core):
    _, hr, cols = landed.shape
    tr = _row_tile(hr, cols)
    nb = hr // tr

    def body(cc_ref, own_ref, l_ref, o_ref):
        o_ref[...] = ((own_ref[...].astype(F32) + l_ref[0].astype(F32)) + l_ref[1].astype(F32)) + l_ref[2].astype(F32)

    grid_spec = pltpu.PrefetchScalarGridSpec(
        num_scalar_prefetch=1, grid=(nb,),
        in_specs=[pl.BlockSpec((None, tr, cols), lambda i, cc: (cc[0], i, 0)),
                  pl.BlockSpec((3, tr, cols), lambda i, cc: (0, i, 0))],
        out_specs=pl.BlockSpec((tr, cols), lambda i, cc: (cc[1] * nb + i, 0)))
    return pl.pallas_call(body, name=name, out_shape=jax.ShapeDtypeStruct((2 * hr, cols), F32), grid_spec=grid_spec,
                          compiler_params=_cp("arbitrary"))(chip_core, own, landed)


def _device_sum(name, gathered):
    def body(g_ref, o_ref):
        total = g_ref[0]
        for k in range(1, N_DEV):
            total = total + g_ref[k]
        o_ref[...] = total

    return pl.pallas_call(body, name=name, out_shape=jax.ShapeDtypeStruct(gathered.shape[1:], F32))(gathered)


def _adamw(name, w, g, m, v):
    rows, cols = w.shape
    tr = _row_tile(rows, cols)
    bc1 = 1.0 - ADAM_B1 ** ADAM_STEP
    bc2 = 1.0 - ADAM_B2 ** ADAM_STEP

    def body(w_ref, g_ref, m_ref, v_ref, d_ref, mo_ref, vo_ref):
        gv = g_ref[...]
        mn = ADAM_B1 * m_ref[...] + (1.0 - ADAM_B1) * gv
        vn = ADAM_B2 * v_ref[...] + (1.0 - ADAM_B2) * (gv * gv)
        mo_ref[...] = mn
        vo_ref[...] = vn
        d_ref[...] = -ADAM_LR * ((mn / bc1) / (jnp.sqrt(vn / bc2) + ADAM_EPS) + ADAM_WD * w_ref[...])

    spec = pl.BlockSpec((tr, cols), lambda i: (i, 0))
    return pl.pallas_call(
        body, name=name, out_shape=[jax.ShapeDtypeStruct((rows, cols), F32)] * 3, grid=(rows // tr,),
        in_specs=[spec] * 4, out_specs=[spec] * 3, compiler_params=_cp("parallel"),
    )(w, g, m, v)


WEIGHTS = ["w_ada", "b_ada", "ffn1_w_in", "ffn1_w_out", "ln1_g", "ln1_b", "w_mix_in", "rel_bias", "w_alpha2",
           "b_alpha", "gla_norm_g", "w_proj_a", "w_proj_b", "w_mix_out", "ln2_g", "ln2_b", "ffn2_w_in", "ffn2_w_out",
           "ln3_g", "ln3_b"]
BIG = {"ffn1_w_in": True, "ffn1_w_out": False, "w_mix_in": True, "w_proj_a": True, "w_proj_b": True,
       "w_mix_out": False, "ffn2_w_in": True, "ffn2_w_out": False}
STACKED = ("ffn1_w_in", "ffn2_w_in")
SMALL = ["ln1_g", "ln1_b", "ln2_g", "ln2_b", "ln3_g", "ln3_b", "b_alpha", "gla_norm_g", "rel_bias", "w_alpha2"]


def _pad_rows(vec, rows=SUBLANES):
    per = -(-vec.shape[0] // (rows * LANES)) * LANES
    return jnp.pad(vec, (0, rows * per - vec.shape[0])).reshape(rows, per)


def _silu(v):
    return v * _sigmoid(v)


def _step(args):
    x_pos, y_pos, c_pos = _mesh_pos()
    chip = 2 * x_pos + y_pos
    dev = 4 * x_pos + 2 * y_pos + c_pos
    w = {k: args[k][0] for k in WEIGHTS}
    mom = {k: args["m_" + k][0] for k in WEIGHTS}
    vel = {k: args["v_" + k][0] for k in WEIGHTS}
    x = args["x"][0]
    target = args["loss_target"][0]
    s, d = x.shape
    kd = d // 4
    rel_sh = w["rel_bias"].shape[1]
    wa2_sh = w["w_alpha2"].shape[1]
    ada_sh = w["w_ada"].shape[1]

    n_rel, n_wa2 = A_HEADS * rel_sh, GATE_RANK * wa2_sh
    packed = _pad_rows(jnp.concatenate([args["c"].reshape(-1), w["rel_bias"].reshape(-1), w["w_alpha2"].reshape(-1)]))
    got = _allgather_rows("gather_small_inputs", packed).reshape(N_DEV, -1)
    c_all = got[:, :d]
    per_chip = got[0::2]
    rel_bias = per_chip[:, d:d + n_rel].reshape(N_CHIPS, A_HEADS, rel_sh).transpose(1, 0, 2).reshape(A_HEADS, -1)
    w_alpha2 = per_chip[:, d + n_rel:d + n_rel + n_wa2].reshape(N_CHIPS, GATE_RANK, wa2_sh).transpose(1, 0, 2)
    w_alpha2 = w_alpha2.reshape(GATE_RANK, -1)

    b_shard = lax.dynamic_slice(w["b_ada"], (chip * ada_sh,), (ada_sh,))
    mod_shard = _mm("ada_fwd", "nn", c_all, w["w_ada"], (N_DEV, ada_sh, d), tm=N_DEV, tn=_tile(ada_sh, (512, 128)),
                    tk=d, precision=HIGHEST, a_fn=_silu, add=jnp.broadcast_to(b_shard[None], (N_DEV, ada_sh)))
    mod_all = _allgather_rows("gather_mod", mod_shard).reshape(N_DEV, N_DEV, ada_sh)[0::2]
    mod_all = mod_all.transpose(1, 0, 2).reshape(N_DEV, N_MOD * d)
    mod = lax.dynamic_index_in_dim(mod_all, dev, 0, keepdims=False).reshape(N_MOD, d)

    names = list(BIG)
    shards16 = [w[k].astype(BF16) for k in names]
    gathered = _allgather_weights(shards16)
    big = {}
    for k, g, own in zip(names, gathered, shards16):
        _, r, cc = g.shape
        g = lax.dynamic_update_slice(g, own[None], (chip, 0, 0))
        if k in STACKED:
            big[k] = g
        else:
            big[k] = g.transpose(1, 0, 2).reshape(r, N_CHIPS * cc) if BIG[k] else g.reshape(N_CHIPS * r, cc)

    small = dict(rel_bias=rel_bias, w_alpha2=w_alpha2, b_alpha=w["b_alpha"][None], gla_norm_g=w["gla_norm_g"][None])
    for k in ("ln1_g", "ln1_b", "ln2_g", "ln2_b", "ln3_g", "ln3_b"):
        small[k] = w[k][None]
    loss_local, grad_x, big_grads, small_grads, dmod = _device_step(x, target, mod, big, small)
    loss = lax.psum(loss_local, ("x", "y", "c"))

    flat = jnp.concatenate([small_grads[k].reshape(-1) for k in SMALL] + [dmod.reshape(-1)])
    n_small = flat.shape[0] - N_MOD * d
    packed = _pad_rows(flat)
    all_small = _allgather_rows("gather_small_grads", packed).reshape(N_DEV, SUBLANES, -1)
    summed = _device_sum("small_grad_sum", all_small).reshape(-1)
    dmod_all = all_small.reshape(N_DEV, -1)[:, n_small:n_small + N_MOD * d]
    dmod_shard = lax.dynamic_slice(dmod_all, (0, chip * ada_sh), (N_DEV, ada_sh))
    grads = {"b_ada": summed[n_small:n_small + N_MOD * d]}
    off = 0
    for k in SMALL:
        size = small_grads[k].size
        grads[k] = summed[off:off + size].reshape(small_grads[k].shape)
        off += size
    grads["rel_bias"] = lax.dynamic_slice(grads["rel_bias"], (0, chip * rel_sh), (A_HEADS, rel_sh))
    grads["w_alpha2"] = lax.dynamic_slice(grads["w_alpha2"], (0, chip * wa2_sh), (GATE_RANK, wa2_sh))
    grads["w_ada"] = _mm("ada_bwd", "nn", jnp.pad(c_all.T, ((0, 0), (0, LANES - N_DEV))),
                         jnp.pad(dmod_shard, ((0, LANES - N_DEV), (0, 0))), (d, ada_sh, LANES), tm=_tile(d, (1024,)),
                         tn=_tile(ada_sh, (512, 128)), tk=LANES, precision=HIGHEST, a_fn=_silu)

    stacked = []
    for k in names:
        g = big_grads[k]
        cc = w[k].shape[1]
        r = w[k].shape[0]
        if k in STACKED:
            stacked.append(g)
        else:
            stacked.append(g.reshape(r, N_CHIPS, cc).transpose(1, 0, 2) if BIG[k] else g.reshape(N_CHIPS, r, cc))
    core = c_pos.astype(jnp.int32).reshape(1)
    chip_core = jnp.stack([chip, c_pos]).astype(jnp.int32)
    from_sibling = _exchange_halves(stacked)
    chip_sums = [_pair_sum(f"pair_sum_{k}", g, r, core) for k, g, r in zip(names, stacked, from_sibling)]
    landed = _scatter_to_owners(chip_sums)
    halves = [_quad_sum(f"quad_sum_{k}", o, l, chip_core) for k, o, l in zip(names, chip_sums, landed)]
    for k, g in zip(names, _share_reduced_halves(halves)):
        grads[k] = g

    delta, new_m, new_v = {}, {}, {}
    for k in ["w_ada"] + names:
        delta[k], new_m[k], new_v[k] = _adamw(f"adamw_{k}", w[k], grads[k], mom[k], vel[k])
    tiny = ["b_ada"] + SMALL
    pack = lambda src: _pad_rows(jnp.concatenate([src[k].reshape(-1) for k in tiny]), rows=1).reshape(-1, LANES)
    outs = _adamw("adamw_small", pack(w), pack(grads), pack(mom), pack(vel))
    off = 0
    for k in tiny:
        size = w[k].size
        for dst, src in zip((delta, new_m, new_v), outs):
            dst[k] = src.reshape(-1)[off:off + size].reshape(w[k].shape)
        off += size

    lead = lambda t: t[None]
    return (loss, lead(grad_x), *[lead(grads[k]) for k in WEIGHTS], *[lead(delta[k]) for k in WEIGHTS],
            *[lead(new_m[k]) for k in WEIGHTS], *[lead(new_v[k]) for k in WEIGHTS])


def kernel(x, c, w_ada, b_ada, ffn1_w_in, ffn1_w_out, ln1_g, ln1_b, w_mix_in, rel_bias, w_alpha2, b_alpha, gla_norm_g, w_proj_a, w_proj_b, w_mix_out, ln2_g, ln2_b, ffn2_w_in, ffn2_w_out, ln3_g, ln3_b, loss_target, m_w_ada, m_b_ada, m_ffn1_w_in, m_ffn1_w_out, m_ln1_g, m_ln1_b, m_w_mix_in, m_rel_bias, m_w_alpha2, m_b_alpha, m_gla_norm_g, m_w_proj_a, m_w_proj_b, m_w_mix_out, m_ln2_g, m_ln2_b, m_ffn2_w_in, m_ffn2_w_out, m_ln3_g, m_ln3_b, v_w_ada, v_b_ada, v_ffn1_w_in, v_ffn1_w_out, v_ln1_g, v_ln1_b, v_w_mix_in, v_rel_bias, v_w_alpha2, v_b_alpha, v_gla_norm_g, v_w_proj_a, v_w_proj_b, v_w_mix_out, v_ln2_g, v_ln2_b, v_ffn2_w_in, v_ffn2_w_out, v_ln3_g, v_ln3_b):
    return _step(dict(locals()))
```

```python
import functools

import jax
import jax.numpy as jnp
from jax import lax
from jax.experimental import pallas as pl
from jax.experimental.pallas import tpu as pltpu

F32 = jnp.float32
BF16 = jnp.bfloat16
MESH = pl.DeviceIdType.MESH
HIGHEST = lax.Precision.HIGHEST

VMEM_LIMIT_BYTES = 56 * 1024 * 1024
LANES = 128
SUBLANES = 8

CHUNK = 64
A_HEADS = 16
A_HEAD_DIM = 64
A_PAST_CHUNKS = 8
A_BAND = (A_PAST_CHUNKS + 1) * CHUNK
A_PAD = A_PAST_CHUNKS * CHUNK
REL_CLIP = 256
REL_SIZE = REL_CLIP + CHUNK
B_HEADS = 4
GATE_RANK = 16
GATE_TAU = 16.0
N_MOD = 9
DEPTH = 1
ALPHA = (2.0 * DEPTH) ** 0.25
LN_EPS = 1e-5
RMS_EPS = 1e-6
ADAM_LR = 0.001
ADAM_B1 = 0.9
ADAM_B2 = 0.999
ADAM_EPS = 1e-08
ADAM_WD = 0.01
ADAM_STEP = 10
NEG_BIG = -1e30

N_CHIPS = 4
N_DEV = 8


def _cp(*sem):
    return pltpu.CompilerParams(dimension_semantics=sem, vmem_limit_bytes=VMEM_LIMIT_BYTES)


class _Stage:
    def __init__(self, arrays, out_shapes, n_sems, start, finish, aliases=None):
        self.arrays, self.out_shapes, self.n_sems = list(arrays), list(out_shapes), n_sems
        self.start, self.finish, self.aliases = start, finish, dict(aliases or {})


def _pcall(body, stages, *, name, out_shape, in_specs, out_specs, grid=(), scratch_shapes=(), compiler_params=None):
    single = not isinstance(out_shape, (list, tuple))
    outs = [out_shape] if single else list(out_shape)
    ospecs = [out_specs] if single else list(out_specs)
    in_specs, scratch_shapes = list(in_specs), list(scratch_shapes)
    n_in, n_out, n_sc = len(in_specs), len(outs), len(scratch_shapes)
    stages = list(stages or [])
    c_in = [a for st in stages for a in st.arrays]
    c_out = [o for st in stages for o in st.out_shapes]
    aliases = {}
    io, oo = n_in, n_out
    for st in stages:
        for a, b in st.aliases.items():
            aliases[io + a] = oo + b
        io += len(st.arrays)
        oo += len(st.out_shapes)

    def wrapped(*refs):
        ins = refs[:n_in]
        cins = refs[n_in:n_in + len(c_in)]
        base = n_in + len(c_in)
        mouts = refs[base:base + n_out]
        couts = refs[base + n_out:base + n_out + len(c_out)]
        base += n_out + len(c_out)
        scr = refs[base:base + n_sc]
        sems = refs[base + n_sc:]

        def each(phase):
            i = o = 0
            for k, st in enumerate(stages):
                fn = st.start if phase == 0 else st.finish
                fn(cins[i:i + len(st.arrays)], couts[o:o + len(st.out_shapes)], sems[2 * k], sems[2 * k + 1])
                i += len(st.arrays)
                o += len(st.out_shapes)

        if stages and grid:
            first = functools.reduce(jnp.logical_and, [pl.program_id(a) == 0 for a in range(len(grid))])
            last = functools.reduce(jnp.logical_and, [pl.program_id(a) == g - 1 for a, g in enumerate(grid)])
            pl.when(first)(lambda: each(0))
            if body is not None:
                body(*ins, *mouts, *scr)
            pl.when(last)(lambda: each(1))
        else:
            each(0)
            if body is not None:
                body(*ins, *mouts, *scr)
            each(1)

    sem_shapes = []
    for st in stages:
        sem_shapes += [pltpu.SemaphoreType.DMA((st.n_sems,)), pltpu.SemaphoreType.DMA((st.n_sems,))]
    kwargs = dict(grid=grid) if grid else {}
    if compiler_params is not None:
        kwargs["compiler_params"] = compiler_params

    def run(*operands):
        res = pl.pallas_call(
            wrapped, name=name, out_shape=outs + c_out, in_specs=in_specs + [HBM_SPEC] * len(c_in),
            out_specs=ospecs + [HBM_SPEC] * len(c_out), scratch_shapes=scratch_shapes + sem_shapes,
            input_output_aliases=aliases, **kwargs)(*operands, *c_in)
        main = res[0] if single else tuple(res[:n_out])
        if not stages:
            return main
        comm, o = [], n_out
        for st in stages:
            comm.append(list(res[o:o + len(st.out_shapes)]))
            o += len(st.out_shapes)
        return main, comm

    return run


def _tile(n, prefs):
    for t in prefs:
        if t <= n and n % t == 0:
            return t
    return n


_DIMS = {"nn": (((1,), (0,)), ((), ())), "nt": (((1,), (1,)), ((), ())), "tn": (((0,), (0,)), ((), ()))}


def _dot(a, b, mode="nn", precision=None):
    return lax.dot_general(a, b, _DIMS[mode], precision=precision, preferred_element_type=F32)


def _sigmoid(x):
    return 1.0 / (1.0 + jnp.exp(-x))


def _mm(name, mode, a, b, mnk, *, tm, tn, tk, out_dtype=F32, precision=None, a_spec=None, b_spec=None,
        out_shape=None, o_spec=None, add=None, a_fn=None, stages=None):
    m, n, k = mnk
    assert m % tm == 0 and n % tn == 0 and k % tk == 0, (name, mnk, tm, tn, tk)
    nk = k // tk
    if a_spec is None:
        a_spec = {"nn": pl.BlockSpec((tm, tk), lambda i, j, kk: (i, kk)),
                  "nt": pl.BlockSpec((tm, tk), lambda i, j, kk: (i, kk)),
                  "tn": pl.BlockSpec((tk, tm), lambda i, j, kk: (kk, i))}[mode]
    if b_spec is None:
        b_spec = {"nn": pl.BlockSpec((tk, tn), lambda i, j, kk: (kk, j)),
                  "nt": pl.BlockSpec((tn, tk), lambda i, j, kk: (j, kk)),
                  "tn": pl.BlockSpec((tk, tn), lambda i, j, kk: (kk, j))}[mode]
    if o_spec is None:
        o_spec = pl.BlockSpec((tm, tn), lambda i, j, kk: (i, j))
    if out_shape is None:
        out_shape = (m, n)
    has_add = add is not None

    def body(*refs):
        a_ref, b_ref = refs[0], refs[1]
        add_ref = refs[2] if has_add else None
        o_ref = refs[3] if has_add else refs[2]
        av = a_ref[...]
        if a_fn is not None:
            av = a_fn(av)
        part = _dot(av, b_ref[...], mode, precision)

        def finish(total):
            if has_add:
                total = total + add_ref[...]
            o_ref[...] = total.astype(out_dtype)

        if nk == 1:
            finish(part)
        else:
            acc_ref = refs[-1]
            kk = pl.program_id(2)

            @pl.when(kk == 0)
            def _():
                acc_ref[...] = part

            @pl.when(kk > 0)
            def _():
                acc_ref[...] += part

            @pl.when(kk == nk - 1)
            def _():
                finish(acc_ref[...])

    in_specs = [a_spec, b_spec]
    operands = [a, b]
    if has_add:
        in_specs.append(pl.BlockSpec((tm, tn), lambda i, j, kk: (i, j)))
        operands.append(add)
    return _pcall(
        body, stages, name=name, out_shape=jax.ShapeDtypeStruct(out_shape, out_dtype), grid=(m // tm, n // tn, nk),
        in_specs=in_specs, out_specs=o_spec,
        scratch_shapes=[pltpu.VMEM((tm, tn), F32)] if nk > 1 else [],
        compiler_params=_cp("arbitrary", "arbitrary", "arbitrary") if stages else _cp("parallel", "parallel", "arbitrary"),
    )(*operands)


def _row_spec(tr, d):
    return pl.BlockSpec((tr, d), lambda i: (i, 0))


def _vec_spec(d, rows=1):
    return pl.BlockSpec((rows, d), lambda i: (0, 0))


def _modulate(name, x, sh, sc):
    s, d = x.shape
    tr = _tile(s, (512, 256))

    def body(x_ref, sh_ref, sc_ref, o_ref):
        o_ref[...] = (x_ref[...] * (1.0 + sc_ref[...]) + sh_ref[...]).astype(BF16)

    return pl.pallas_call(
        body, name=name, out_shape=jax.ShapeDtypeStruct((s, d), BF16), grid=(s // tr,),
        in_specs=[_row_spec(tr, d), _vec_spec(d), _vec_spec(d)], out_specs=_row_spec(tr, d),
        compiler_params=_cp("parallel"),
    )(x, sh, sc)


def _ln_stats(r):
    mu = jnp.mean(r, axis=-1, keepdims=True)
    xc = r - mu
    var = jnp.mean(xc * xc, axis=-1, keepdims=True)
    rstd = lax.rsqrt(var + LN_EPS)
    return xc * rstd, rstd


def _resid_ln_fwd(name, x, f, gate, ln_g, ln_b, sh_n, sc_n, coef):
    s, d = x.shape
    tr = _tile(s, (256,))

    def body(x_ref, f_ref, gate_ref, g_ref, b_ref, sh_ref, sc_ref, h_ref, u_ref):
        r = ALPHA * x_ref[...] + (coef * gate_ref[...]) * f_ref[...]
        xhat, _ = _ln_stats(r)
        h = xhat * g_ref[...] + b_ref[...]
        h_ref[...] = h
        u_ref[...] = (h * (1.0 + sc_ref[...]) + sh_ref[...]).astype(BF16)

    return pl.pallas_call(
        body, name=name, out_shape=(jax.ShapeDtypeStruct((s, d), F32), jax.ShapeDtypeStruct((s, d), BF16)),
        grid=(s // tr,), in_specs=[_row_spec(tr, d), _row_spec(tr, d)] + [_vec_spec(d)] * 5,
        out_specs=(_row_spec(tr, d), _row_spec(tr, d)), compiler_params=_cp("parallel"),
    )(x, f, gate, ln_g, ln_b, sh_n, sc_n)


ROW_DSC, ROW_DSH, ROW_DLN_G, ROW_DLN_B, ROW_DGATE, ROW_LOSS = 0, 1, 2, 3, 4, 5


def _ln_bwd_core(dy, xhat, rstd, ln_g):
    dxhat = dy * ln_g
    m1 = jnp.mean(dxhat, axis=-1, keepdims=True)
    m2 = jnp.mean(dxhat * xhat, axis=-1, keepdims=True)
    return rstd * (dxhat - m1 - xhat * m2)


def _colsum(v):
    return jnp.sum(v, axis=0, keepdims=True)


def _final_ln_loss_bwd(name, x, f, target, gate, ln_g, ln_b, coef):
    s, d = x.shape
    tr = _tile(s, (256,))
    inv_d = 1.0 / d

    def body(x_ref, f_ref, t_ref, gate_ref, g_ref, b_ref, dr_ref, df_ref, acc_ref):
        @pl.when(pl.program_id(0) == 0)
        def _():
            acc_ref[...] = jnp.zeros_like(acc_ref)

        fv = f_ref[...]
        r = ALPHA * x_ref[...] + (coef * gate_ref[...]) * fv
        xhat, rstd = _ln_stats(r)
        h = xhat * g_ref[...] + b_ref[...]
        err = h - t_ref[...]
        dy = err * inv_d
        dr = _ln_bwd_core(dy, xhat, rstd, g_ref[...])
        dr_ref[...] = dr
        df_ref[...] = ((coef * gate_ref[...]) * dr).astype(BF16)
        acc_ref[ROW_DLN_G:ROW_DLN_G + 1, :] += _colsum(dy * xhat)
        acc_ref[ROW_DLN_B:ROW_DLN_B + 1, :] += _colsum(dy)
        acc_ref[ROW_DGATE:ROW_DGATE + 1, :] += _colsum((coef * dr) * fv)
        acc_ref[ROW_LOSS:ROW_LOSS + 1, :] += _colsum(err * err) * (0.5 * inv_d)

    return pl.pallas_call(
        body, name=name,
        out_shape=(jax.ShapeDtypeStruct((s, d), F32), jax.ShapeDtypeStruct((s, d), BF16),
                   jax.ShapeDtypeStruct((SUBLANES, d), F32)),
        grid=(s // tr,), in_specs=[_row_spec(tr, d)] * 3 + [_vec_spec(d)] * 3,
        out_specs=(_row_spec(tr, d), _row_spec(tr, d), _vec_spec(d, SUBLANES)),
        compiler_params=_cp("arbitrary"),
    )(x, f, target, gate, ln_g, ln_b)


def _resid_ln_bwd(name, du_n, dr_n, x, f, sc_n, gate, ln_g, ln_b, coef):
    s, d = x.shape
    tr = _tile(s, (256,))

    def body(du_ref, drn_ref, x_ref, f_ref, sc_ref, gate_ref, g_ref, b_ref, dr_ref, df_ref, acc_ref):
        @pl.when(pl.program_id(0) == 0)
        def _():
            acc_ref[...] = jnp.zeros_like(acc_ref)

        fv = f_ref[...]
        du = du_ref[...]
        r = ALPHA * x_ref[...] + (coef * gate_ref[...]) * fv
        xhat, rstd = _ln_stats(r)
        h = xhat * g_ref[...] + b_ref[...]
        dy = du * (1.0 + sc_ref[...]) + ALPHA * drn_ref[...]
        dr = _ln_bwd_core(dy, xhat, rstd, g_ref[...])
        dr_ref[...] = dr
        df_ref[...] = ((coef * gate_ref[...]) * dr).astype(BF16)
        acc_ref[ROW_DSC:ROW_DSC + 1, :] += _colsum(du * h)
        acc_ref[ROW_DSH:ROW_DSH + 1, :] += _colsum(du)
        acc_ref[ROW_DLN_G:ROW_DLN_G + 1, :] += _colsum(dy * xhat)
        acc_ref[ROW_DLN_B:ROW_DLN_B + 1, :] += _colsum(dy)
        acc_ref[ROW_DGATE:ROW_DGATE + 1, :] += _colsum((coef * dr) * fv)

    return pl.pallas_call(
        body, name=name,
        out_shape=(jax.ShapeDtypeStruct((s, d), F32), jax.ShapeDtypeStruct((s, d), BF16),
                   jax.ShapeDtypeStruct((SUBLANES, d), F32)),
        grid=(s // tr,), in_specs=[_row_spec(tr, d)] * 4 + [_vec_spec(d)] * 4,
        out_specs=(_row_spec(tr, d), _row_spec(tr, d), _vec_spec(d, SUBLANES)),
        compiler_params=_cp("arbitrary"),
    )(du_n, dr_n, x, f, sc_n, gate, ln_g, ln_b)


def _input_grad(name, du, dr, x, sc):
    s, d = x.shape
    tr = _tile(s, (256,))

    def body(du_ref, dr_ref, x_ref, sc_ref, gx_ref, acc_ref):
        @pl.when(pl.program_id(0) == 0)
        def _():
            acc_ref[...] = jnp.zeros_like(acc_ref)

        du = du_ref[...]
        gx_ref[...] = du * (1.0 + sc_ref[...]) + ALPHA * dr_ref[...]
        acc_ref[ROW_DSC:ROW_DSC + 1, :] += _colsum(du * x_ref[...])
        acc_ref[ROW_DSH:ROW_DSH + 1, :] += _colsum(du)

    return pl.pallas_call(
        body, name=name,
        out_shape=(jax.ShapeDtypeStruct((s, d), F32), jax.ShapeDtypeStruct((SUBLANES, d), F32)),
        grid=(s // tr,), in_specs=[_row_spec(tr, d)] * 3 + [_vec_spec(d)],
        out_specs=(_row_spec(tr, d), _vec_spec(d, SUBLANES)), compiler_params=_cp("arbitrary"),
    )(du, dr, x, sc)


def _ffn_in_fwd(name, u, w_in, stages=None):
    s, d = u.shape
    cs = w_in.shape[2]
    f = 2 * cs
    tm, tn = _tile(s, (2048, 1024, 512)), _tile(cs, (256, 128))
    nb = f // tn
    nbs = cs // tn

    def body(u_ref, wa_ref, wb_ref, ab_ref, act_ref):
        uv = u_ref[...]
        a = _dot(uv, wa_ref[...])
        b = _dot(uv, wb_ref[...])
        ab_ref[0] = a.astype(BF16)
        ab_ref[1] = b.astype(BF16)
        act_ref[...] = (a * _sigmoid(a) * b).astype(BF16)

    return _pcall(
        body, stages, name=name,
        out_shape=(jax.ShapeDtypeStruct((2, s, f), BF16), jax.ShapeDtypeStruct((s, f), BF16)),
        grid=(s // tm, nb),
        in_specs=[pl.BlockSpec((tm, d), lambda i, j: (i, 0)),
                  pl.BlockSpec((None, d, tn), lambda i, j: (j // nbs, 0, j % nbs)),
                  pl.BlockSpec((None, d, tn), lambda i, j: (2 + j // nbs, 0, j % nbs))],
        out_specs=(pl.BlockSpec((2, tm, tn), lambda i, j: (0, i, j)), pl.BlockSpec((tm, tn), lambda i, j: (i, j))),
        compiler_params=_cp("arbitrary", "arbitrary"),
    )(u, w_in, w_in)


def _ffn_out_bwd(name, df, w_out, ab, stages=None):
    s, d = df.shape
    f = w_out.shape[0]
    tm, tn = _tile(s, (1024, 512)), _tile(f, (512, 256, 128))

    def body(df_ref, w_ref, ab_ref, dab_ref):
        dact = _dot(df_ref[...], w_ref[...], "nt")
        a = ab_ref[0].astype(F32)
        b = ab_ref[1].astype(F32)
        sg = _sigmoid(a)
        dab_ref[0] = (dact * b * (sg * (1.0 + a * (1.0 - sg)))).astype(BF16)
        dab_ref[1] = (dact * (a * sg)).astype(BF16)

    return _pcall(
        body, stages, name=name, out_shape=jax.ShapeDtypeStruct((2, s, f), BF16), grid=(s // tm, f // tn),
        in_specs=[pl.BlockSpec((tm, d), lambda i, j: (i, 0)), pl.BlockSpec((tn, d), lambda i, j: (j, 0)),
                  pl.BlockSpec((2, tm, tn), lambda i, j: (0, i, j))],
        out_specs=pl.BlockSpec((2, tm, tn), lambda i, j: (0, i, j)),
        compiler_params=_cp("arbitrary", "arbitrary"),
    )(df, w_out, ab)


def _ffn_forward(tag, u, plan):
    w_in, w_out = plan.weight(f"{tag}_w_in"), plan.weight(f"{tag}_w_out")
    s, d = u.shape
    f = w_out.shape[0]
    ab, act = plan.host(f"{tag}_in_fwd", lambda st: _ffn_in_fwd(f"{tag}_in_fwd", u, w_in, st))
    out = plan.host(f"{tag}_out_fwd", lambda st: _mm(
        f"{tag}_out_fwd", "nn", act, w_out, (s, d, f), tm=_tile(s, (1024,)), tn=_tile(d, (1024,)),
        tk=_tile(f, (1408, 512, 128)), stages=st))
    return out, (ab, act)


def _ffn_backward(tag, df, u, saved, plan):
    w_in, w_out = plan.weight(f"{tag}_w_in"), plan.weight(f"{tag}_w_out")
    ab, act = saved
    s, d = u.shape
    f = w_out.shape[0]
    dab = plan.host(f"{tag}_out_bwd", lambda st: _ffn_out_bwd(f"{tag}_out_bwd", df, w_out, ab, st))
    cs = w_in.shape[2]
    tk = _tile(cs, (1408, 256, 128))
    nkh, nks = f // tk, cs // tk
    tmd = _tile(d, (1024,))
    tn = tk
    nbh, nbs = f // tn, cs // tn
    tks = _tile(s, (1024,))
    plan.grad(f"{tag}_w_in", plan.host(f"{tag}_dw_in", lambda st: _mm(
        f"{tag}_dw_in", "tn", u, dab, (d, 2 * f, s), tm=tmd, tn=tn, tk=tks,
        b_spec=pl.BlockSpec((None, tks, tn), lambda i, j, kk: (j // nbh, kk, j % nbh)), out_shape=(N_CHIPS, d, cs),
        o_spec=pl.BlockSpec((None, tmd, tn), lambda i, j, kk: (j // nbs, i, j % nbs)), stages=st)))
    plan.grad(f"{tag}_w_out", plan.host(f"{tag}_dw_out", lambda st: _mm(
        f"{tag}_dw_out", "tn", act, df, (f, d, s), tm=_tile(f, (1408, 512, 128)), tn=tmd, tk=tks, stages=st)))
    return plan.host(f"{tag}_du", lambda st: _mm(
        f"{tag}_du", "nt", dab, w_in, (s, d, 2 * f), tm=_tile(s, (1024,)), tn=tmd, tk=tk,
        a_spec=pl.BlockSpec((None, _tile(s, (1024,)), tk), lambda i, j, kk: (kk // nkh, i, kk % nkh)),
        b_spec=pl.BlockSpec((None, tmd, tk), lambda i, j, kk: (kk // nks, j, kk % nks)), stages=st))


def _attn_scores(qc, kw, bias, key0):
    sc = _dot(qc, kw, "nt") * (A_HEAD_DIM ** -0.5) + bias
    ks = lax.broadcasted_iota(jnp.int32, sc.shape, 1)
    sc = jnp.where(key0 + ks >= 0, sc, NEG_BIG)
    m = jnp.max(sc, axis=-1, keepdims=True)
    p = jnp.exp(sc - m)
    return p, jnp.sum(p, axis=-1, keepdims=True)


def _attn_fwd(q, kp, vp, bias, stages=None):
    h, s, dh = q.shape
    tq = _tile(s, (256, 128, 64))
    ncq = tq // CHUNK

    def body(q_ref, k_ref, v_ref, b_ref, o_ref):
        base = pl.program_id(1) * tq
        bias_v = b_ref[...]
        for c in range(ncq):
            start = pl.multiple_of(base + c * CHUNK, CHUNK)
            kw = k_ref[pl.ds(start, A_BAND), :]
            vw = v_ref[pl.ds(start, A_BAND), :]
            p, l = _attn_scores(q_ref[c * CHUNK:(c + 1) * CHUNK, :], kw, bias_v, start - A_PAD)
            o = _dot((p / l).astype(BF16), vw)
            o_ref[c * CHUNK:(c + 1) * CHUNK, :] = o.astype(BF16)

    return _pcall(
        body, stages, name="attn_fwd", out_shape=jax.ShapeDtypeStruct((h, s, dh), BF16), grid=(h, s // tq),
        in_specs=[pl.BlockSpec((None, tq, dh), lambda hh, i: (hh, i, 0)),
                  pl.BlockSpec((None, s + A_PAD, dh), lambda hh, i: (hh, 0, 0)),
                  pl.BlockSpec((None, s + A_PAD, dh), lambda hh, i: (hh, 0, 0)),
                  pl.BlockSpec((None, CHUNK, A_BAND), lambda hh, i: (hh, 0, 0))],
        out_specs=pl.BlockSpec((None, tq, dh), lambda hh, i: (hh, i, 0)),
        compiler_params=_cp("arbitrary", "arbitrary"),
    )(q, kp, vp, bias)


def _attn_bwd(q, kp, vp, bias, do, stages=None):
    h, s, dh = q.shape
    tq = _tile(s, (256, 128, 64))
    ncq = tq // CHUNK
    scale = A_HEAD_DIM ** -0.5

    def body(q_ref, k_ref, v_ref, b_ref, do_ref, dq_ref, dk_ref, dv_ref, db_ref):
        @pl.when(pl.program_id(1) == 0)
        def _():
            dk_ref[...] = jnp.zeros_like(dk_ref)
            dv_ref[...] = jnp.zeros_like(dv_ref)
            db_ref[...] = jnp.zeros_like(db_ref)

        base = pl.program_id(1) * tq
        bias_v = b_ref[...]
        for c in range(ncq):
            start = pl.multiple_of(base + c * CHUNK, CHUNK)
            rows = slice(c * CHUNK, (c + 1) * CHUNK)
            kw = k_ref[pl.ds(start, A_BAND), :]
            vw = v_ref[pl.ds(start, A_BAND), :]
            qc = q_ref[rows, :]
            doc = do_ref[rows, :]
            p, l = _attn_scores(qc, kw, bias_v, start - A_PAD)
            p = p / l
            dp = _dot(doc, vw, "nt")
            delta = jnp.sum(p * dp, axis=-1, keepdims=True)
            ds = p * (dp - delta)
            db_ref[...] += ds
            dsb = (ds * scale).astype(BF16)
            dq_ref[rows, :] = _dot(dsb, kw).astype(BF16)
            dk_ref[pl.ds(start, A_BAND), :] += _dot(dsb, qc, "tn")
            dv_ref[pl.ds(start, A_BAND), :] += _dot(p.astype(BF16), doc, "tn")

    kv_spec = pl.BlockSpec((None, s + A_PAD, dh), lambda hh, i: (hh, 0, 0))
    q_spec = pl.BlockSpec((None, tq, dh), lambda hh, i: (hh, i, 0))
    b_spec = pl.BlockSpec((None, CHUNK, A_BAND), lambda hh, i: (hh, 0, 0))
    return _pcall(
        body, stages, name="attn_bwd",
        out_shape=(jax.ShapeDtypeStruct((h, s, dh), BF16), jax.ShapeDtypeStruct((h, s + A_PAD, dh), F32),
                   jax.ShapeDtypeStruct((h, s + A_PAD, dh), F32), jax.ShapeDtypeStruct((h, CHUNK, A_BAND), F32)),
        grid=(h, s // tq), in_specs=[q_spec, kv_spec, kv_spec, b_spec, q_spec],
        out_specs=(q_spec, kv_spec, kv_spec, b_spec), compiler_params=_cp("arbitrary", "arbitrary"),
    )(q, kp, vp, bias, do)


def _rel_onehot():
    qi = jnp.arange(CHUNK)[:, None]
    ks = jnp.arange(A_BAND)[None, :]
    idx = (jnp.clip(ks - A_PAD - qi, -REL_CLIP, CHUNK - 1) + REL_CLIP).reshape(1, CHUNK * A_BAND)
    return (jnp.arange(REL_SIZE)[:, None] == idx).astype(F32)


def _gla_gate(lr, wa2, balpha):
    z = _dot(lr, wa2) + balpha
    la = (jnp.minimum(z, 0.0) - jnp.log(1.0 + jnp.exp(-jnp.abs(z)))) * (1.0 / GATE_TAU)
    row = lax.broadcasted_iota(jnp.int32, (CHUNK, CHUNK), 0)
    col = lax.broadcasted_iota(jnp.int32, (CHUNK, CHUNK), 1)
    cum = _dot((row >= col).astype(F32), la, precision=HIGHEST)
    return z, la, cum


def _gla_dims(p2):
    kd = p2.shape[1] // 6
    hk = kd // B_HEADS
    hv = 2 * hk
    return kd, hk, hv


def _gla_fwd(p2, lrp, wa2p, balpha, gnorm, stages=None):
    s = p2.shape[0]
    kd, hk, hv = _gla_dims(p2)
    nc = s // CHUNK
    qscale = hk ** -0.5

    def body(p_ref, lr_ref, wa_ref, ba_ref, gn_ref, yb_ref, st_ref, state):
        @pl.when(pl.program_id(0) == 0)
        def _():
            state[...] = jnp.zeros_like(state)

        _, _, cum = _gla_gate(lr_ref[...], wa_ref[...], ba_ref[...])
        last = cum[CHUNK - 1:CHUNK, :]
        e = jnp.exp(last - cum)
        dch = jnp.exp(last)
        gn = gn_ref[...]
        for hh in range(B_HEADS):
            ks = slice(hh * hk, (hh + 1) * hk)
            q = p_ref[:, hh * hk:(hh + 1) * hk].astype(F32)
            k = p_ref[:, kd + hh * hk:kd + (hh + 1) * hk].astype(F32)
            v = p_ref[:, 2 * kd + hh * hv:2 * kd + (hh + 1) * hv]
            rg = p_ref[:, 4 * kd + hh * hv:4 * kd + (hh + 1) * hv].astype(F32)
            kdec = (k * e[:, ks]).astype(BF16)
            st = state[hh] * dch[:, ks] + _dot(v, kdec, "tn")
            state[hh] = st
            st_ref[hh] = st
            o = _dot((q * qscale).astype(BF16), st.astype(BF16), "nt")
            rinv = lax.rsqrt(jnp.mean(o * o, axis=-1, keepdims=True) + RMS_EPS)
            yb_ref[:, hh * hv:(hh + 1) * hv] = ((o * rinv * gn) * (rg * _sigmoid(rg))).astype(BF16)

    return _pcall(
        body, stages, name="gla_fwd",
        out_shape=(jax.ShapeDtypeStruct((s, 2 * kd), BF16), jax.ShapeDtypeStruct((nc, B_HEADS, hv, hk), F32)),
        grid=(nc,),
        in_specs=[pl.BlockSpec((CHUNK, 6 * kd), lambda i: (i, 0)), pl.BlockSpec((CHUNK, LANES), lambda i: (i, 0)),
                  pl.BlockSpec((LANES, kd), lambda i: (0, 0)), pl.BlockSpec((1, kd), lambda i: (0, 0)),
                  pl.BlockSpec((1, hv), lambda i: (0, 0))],
        out_specs=(pl.BlockSpec((CHUNK, 2 * kd), lambda i: (i, 0)),
                   pl.BlockSpec((None, B_HEADS, hv, hk), lambda i: (i, 0, 0, 0))),
        scratch_shapes=[pltpu.VMEM((B_HEADS, hv, hk), F32)], compiler_params=_cp("arbitrary"),
    )(p2, lrp, wa2p, balpha, gnorm)


GLA_ROW_DBALPHA, GLA_ROW_DGNORM = 0, 1


def _gla_bwd(p2, lrp, wa2p, balpha, gnorm, states, dyb, stages=None):
    s = p2.shape[0]
    kd, hk, hv = _gla_dims(p2)
    nc = s // CHUNK
    qscale = hk ** -0.5

    def body(p_ref, lr_ref, wa_ref, ba_ref, gn_ref, st_ref, sp_ref, dy_ref, dp_ref, dz_ref, sm_ref, gcar):
        i = pl.program_id(0)

        @pl.when(i == 0)
        def _():
            gcar[...] = jnp.zeros_like(gcar)
            sm_ref[...] = jnp.zeros_like(sm_ref)

        has_prev = (i < nc - 1).astype(F32)
        z, _, cum = _gla_gate(lr_ref[...], wa_ref[...], ba_ref[...])
        last = cum[CHUNK - 1:CHUNK, :]
        e = jnp.exp(last - cum)
        dch = jnp.exp(last)
        sgn = _sigmoid(-z) * (1.0 / GATE_TAU)
        gn = gn_ref[...]
        row = lax.broadcasted_iota(jnp.int32, (CHUNK, CHUNK), 0)
        col = lax.broadcasted_iota(jnp.int32, (CHUNK, CHUNK), 1)
        tri_strict = (row > col).astype(F32)
        for hh in range(B_HEADS):
            ks = slice(hh * hk, (hh + 1) * hk)
            q = p_ref[:, hh * hk:(hh + 1) * hk].astype(F32)
            k = p_ref[:, kd + hh * hk:kd + (hh + 1) * hk].astype(F32)
            v = p_ref[:, 2 * kd + hh * hv:2 * kd + (hh + 1) * hv]
            rg = p_ref[:, 4 * kd + hh * hv:4 * kd + (hh + 1) * hv].astype(F32)
            kdecf = k * e[:, ks]
            kdec = kdecf.astype(BF16)
            st16 = st_ref[hh].astype(BF16)
            qs = (q * qscale).astype(BF16)
            o = _dot(qs, st16, "nt")
            rinv = lax.rsqrt(jnp.mean(o * o, axis=-1, keepdims=True) + RMS_EPS)
            dy = dy_ref[:, hh * hv:(hh + 1) * hv].astype(F32)
            sg = _sigmoid(rg)
            onorm = o * rinv
            drg = dy * (onorm * gn) * (sg * (1.0 + rg * (1.0 - sg)))
            dob = dy * (rg * sg)
            sm_ref[GLA_ROW_DGNORM:GLA_ROW_DGNORM + 1, 0:hv] += _colsum(dob * onorm)
            t = dob * gn
            do = rinv * (t - onorm * jnp.mean(t * onorm, axis=-1, keepdims=True))
            do16 = do.astype(BF16)
            dq = _dot(do16, st16) * qscale
            gt = _dot(do16, qs, "tn") + gcar[hh]
            gcar[hh] = gt * dch[:, ks]
            dd = _colsum(gt * sp_ref[hh]) * has_prev
            gt16 = gt.astype(BF16)
            dkdec = _dot(v, gt16)
            dv = _dot(kdec, gt16, "nt")
            dla = dd * dch[:, ks] + _dot(tri_strict, dkdec * kdecf, precision=HIGHEST)
            dzh = dla * sgn[:, ks]
            sm_ref[GLA_ROW_DBALPHA:GLA_ROW_DBALPHA + 1, hh * hk:(hh + 1) * hk] += _colsum(dzh)
            dz_ref[:, hh * hk:(hh + 1) * hk] = dzh.astype(BF16)
            dp_ref[:, hh * hk:(hh + 1) * hk] = dq.astype(BF16)
            dp_ref[:, kd + hh * hk:kd + (hh + 1) * hk] = (dkdec * e[:, ks]).astype(BF16)
            dp_ref[:, 2 * kd + hh * hv:2 * kd + (hh + 1) * hv] = dv.astype(BF16)
            dp_ref[:, 4 * kd + hh * hv:4 * kd + (hh + 1) * hv] = drg.astype(BF16)

    rev = lambda i: (nc - 1 - i, 0)
    return _pcall(
        body, stages, name="gla_bwd",
        out_shape=(jax.ShapeDtypeStruct((s, 6 * kd), BF16), jax.ShapeDtypeStruct((s, kd), BF16),
                   jax.ShapeDtypeStruct((SUBLANES, kd), F32)),
        grid=(nc,),
        in_specs=[pl.BlockSpec((CHUNK, 6 * kd), rev), pl.BlockSpec((CHUNK, LANES), rev),
                  pl.BlockSpec((LANES, kd), lambda i: (0, 0)), pl.BlockSpec((1, kd), lambda i: (0, 0)),
                  pl.BlockSpec((1, hv), lambda i: (0, 0)),
                  pl.BlockSpec((None, B_HEADS, hv, hk), lambda i: (nc - 1 - i, 0, 0, 0)),
                  pl.BlockSpec((None, B_HEADS, hv, hk), lambda i: (jnp.maximum(nc - 2 - i, 0), 0, 0, 0)),
                  pl.BlockSpec((CHUNK, 2 * kd), rev)],
        out_specs=(pl.BlockSpec((CHUNK, 6 * kd), rev), pl.BlockSpec((CHUNK, kd), rev),
                   pl.BlockSpec((SUBLANES, kd), lambda i: (0, 0))),
        scratch_shapes=[pltpu.VMEM((B_HEADS, hv, hk), F32)], compiler_params=_cp("arbitrary"),
    )(p2, lrp, wa2p, balpha, gnorm, states, states, dyb)


def _merge_fwd(ya, yb, wpa, wpb, g):
    s, ka = ya.shape
    kb = yb.shape[1]
    d = wpa.shape[1]
    tm, tn = _tile(s, (1024, 512)), _tile(d, (512,))

    def body(ya_ref, yb_ref, wa_ref, wb_ref, g_ref, m_ref, pab_ref):
        pa = _dot(ya_ref[...], wa_ref[...])
        pb = _dot(yb_ref[...], wb_ref[...])
        m_ref[...] = (_sigmoid(g_ref[0].astype(F32)) * pa + _sigmoid(g_ref[1].astype(F32)) * pb).astype(BF16)
        pab_ref[0] = pa.astype(BF16)
        pab_ref[1] = pb.astype(BF16)

    st = pl.BlockSpec((2, tm, tn), lambda i, j: (0, i, j))
    return pl.pallas_call(
        body, name="merge_fwd",
        out_shape=(jax.ShapeDtypeStruct((s, d), BF16), jax.ShapeDtypeStruct((2, s, d), BF16)),
        grid=(s // tm, d // tn),
        in_specs=[pl.BlockSpec((tm, ka), lambda i, j: (i, 0)), pl.BlockSpec((tm, kb), lambda i, j: (i, 0)),
                  pl.BlockSpec((ka, tn), lambda i, j: (0, j)), pl.BlockSpec((kb, tn), lambda i, j: (0, j)), st],
        out_specs=(pl.BlockSpec((tm, tn), lambda i, j: (i, j)), st),
        compiler_params=_cp("parallel", "parallel"),
    )(ya, yb, wpa, wpb, g)


def _merge_bwd(dm, wmo, g, pab, stages=None):
    s, d = dm.shape
    tm, tn = _tile(s, (1024, 512)), _tile(d, (512,))

    def body(dm_ref, w_ref, g_ref, pab_ref, dpab_ref, dg_ref):
        dmg = _dot(dm_ref[...], w_ref[...], "nt")
        for j in range(2):
            sg = _sigmoid(g_ref[j].astype(F32))
            dpab_ref[j] = (dmg * sg).astype(BF16)
            dg_ref[j] = (dmg * pab_ref[j].astype(F32) * (sg * (1.0 - sg))).astype(BF16)

    st = pl.BlockSpec((2, tm, tn), lambda i, j: (0, i, j))
    return _pcall(
        body, stages, name="merge_bwd",
        out_shape=(jax.ShapeDtypeStruct((2, s, d), BF16), jax.ShapeDtypeStruct((2, s, d), BF16)),
        grid=(s // tm, d // tn),
        in_specs=[pl.BlockSpec((tm, d), lambda i, j: (i, 0)), pl.BlockSpec((tn, d), lambda i, j: (j, 0)), st, st],
        out_specs=(st, st), compiler_params=_cp("arbitrary", "arbitrary"),
    )(dm, wmo, g, pab)


def _split_mix_in(w_mix_in):
    aw = A_HEADS * A_HEAD_DIM
    d = w_mix_in.shape[0]
    kd = d // 4
    o1 = 3 * aw
    o2 = o1 + 6 * kd
    o3 = o2 + GATE_RANK
    w_lr = jnp.pad(w_mix_in[:, o2:o3], ((0, 0), (0, LANES - GATE_RANK)))
    return w_mix_in[:, :o1], w_mix_in[:, o1:o2], w_lr, w_mix_in[:, o3:]


def _heads_major(t, s):
    return t.reshape(s, A_HEADS, A_HEAD_DIM).transpose(1, 0, 2)


def _mix_weights(plan):
    return plan.memo("mix_weights", lambda: _split_mix_in(plan.weight("w_mix_in")) + (
        plan.weight("w_proj_a"), plan.weight("w_proj_b"), plan.weight("w_mix_out")))


def _mix_forward(u2, plan, small):
    s, d = u2.shape
    w1, w2, w_lr, w_g, wpa, wpb, wmo = _mix_weights(plan)
    bias, wa2p, balpha, gnorm = small
    aw = A_HEADS * A_HEAD_DIM
    tm = _tile(s, (1024,))
    p1 = _mm("mix_in_a", "nn", u2, w1, (s, 3 * aw, d), tm=tm, tn=1024, tk=d, out_dtype=BF16)
    p2 = _mm("mix_in_b", "nn", u2, w2, (s, w2.shape[1], d), tm=tm, tn=1024, tk=d, out_dtype=BF16)
    lrp = _mm("mix_in_lr", "nn", u2, w_lr, (s, LANES, d), tm=tm, tn=LANES, tk=d, out_dtype=BF16)
    nbg = d // 1024
    g = _mm("mix_in_g", "nn", u2, w_g, (s, 2 * d, d), tm=tm, tn=1024, tk=d, out_dtype=BF16, out_shape=(2, s, d),
            o_spec=pl.BlockSpec((None, tm, 1024), lambda i, j, kk: (j // nbg, i, j % nbg)))
    q = _heads_major(p1[:, :aw], s)
    kp = jnp.pad(_heads_major(p1[:, aw:2 * aw], s), ((0, 0), (A_PAD, 0), (0, 0)))
    vp = jnp.pad(_heads_major(p1[:, 2 * aw:], s), ((0, 0), (A_PAD, 0), (0, 0)))
    ya = plan.host("attn_fwd", lambda st: _attn_fwd(q, kp, vp, bias, st)).transpose(1, 0, 2).reshape(s, aw)
    yb, states = plan.host("gla_fwd", lambda st: _gla_fwd(p2, lrp, wa2p, balpha, gnorm, st))
    merged, pab = _merge_fwd(ya, yb, wpa, wpb, g)
    m = _mm("mix_out", "nn", merged, wmo, (s, d, d), tm=tm, tn=1024, tk=d)
    return m, (q, kp, vp, p2, lrp, states, ya, yb, g, pab, merged)


def _mix_backward(dm, u2, saved, plan, small):
    s, d = u2.shape
    w1, w2, w_lr, w_g, wpa, wpb, wmo = _mix_weights(plan)
    bias, wa2p, balpha, gnorm = small
    q, kp, vp, p2, lrp, states, ya, yb, g, pab, merged = saved
    aw = A_HEADS * A_HEAD_DIM
    kd = d // 4
    tm = _tile(s, (1024,))
    tks = _tile(s, (1024,))

    plan.grad("w_mix_out", _mm("mix_dw_out", "tn", merged, dm, (d, d, s), tm=1024, tn=1024, tk=tks))
    dpab, dg = plan.host("merge_bwd", lambda st: _merge_bwd(dm, wmo, g, pab, st))
    sel = lambda j: pl.BlockSpec((None, tm, d), lambda i, jj, kk: (j, i, 0))
    dya = _mm("mix_dya", "nt", dpab, wpa, (s, aw, d), tm=tm, tn=1024, tk=d, out_dtype=BF16, a_spec=sel(0))
    dyb = _mm("mix_dyb", "nt", dpab, wpb, (s, 2 * kd, d), tm=tm, tn=1024, tk=d, out_dtype=BF16, a_spec=sel(1))
    selk = lambda j: pl.BlockSpec((None, tks, 1024), lambda i, jj, kk: (j, kk, jj))
    plan.grad("w_proj_a", _mm("mix_dwpa", "tn", ya, dpab, (aw, d, s), tm=1024, tn=1024, tk=tks, b_spec=selk(0)))
    plan.grad("w_proj_b", _mm("mix_dwpb", "tn", yb, dpab, (2 * kd, d, s), tm=1024, tn=1024, tk=tks, b_spec=selk(1)))

    do = _heads_major(dya, s)
    dq, dkp, dvp, dbias = plan.host("attn_bwd", lambda st: _attn_bwd(q, kp, vp, bias, do, st))
    unheads = lambda t: t.transpose(1, 0, 2).reshape(s, aw)
    dp1 = jnp.concatenate([unheads(dq), unheads(dkp[:, A_PAD:].astype(BF16)), unheads(dvp[:, A_PAD:].astype(BF16))],
                          axis=1)
    dp2, dz, gsm = plan.host("gla_bwd", lambda st: _gla_bwd(p2, lrp, wa2p, balpha, gnorm, states, dyb, st))
    dlrp = _mm("gla_dlr", "nt", dz, wa2p, (s, LANES, kd), tm=tm, tn=LANES, tk=kd, out_dtype=BF16)
    dwa2p = _mm("gla_dwa2", "tn", lrp, dz, (LANES, kd, s), tm=LANES, tn=kd, tk=tks)

    du = _mm("mix_du_a", "nt", dp1, w1, (s, d, 3 * aw), tm=tm, tn=1024, tk=1024)
    du = _mm("mix_du_b", "nt", dp2, w2, (s, d, 6 * kd), tm=tm, tn=1024, tk=1024, add=du)
    du = _mm("mix_du_lr", "nt", dlrp, w_lr, (s, d, LANES), tm=tm, tn=1024, tk=LANES, add=du)
    nkg = d // 1024
    du = _mm("mix_du_g", "nt", dg, w_g, (s, d, 2 * d), tm=tm, tn=1024, tk=1024, add=du,
             a_spec=pl.BlockSpec((None, tm, 1024), lambda i, j, kk: (kk // nkg, i, kk % nkg)))
    dw1 = _mm("mix_dw_a", "tn", u2, dp1, (d, 3 * aw, s), tm=1024, tn=1024, tk=tks)
    dw2 = _mm("mix_dw_b", "tn", u2, dp2, (d, 6 * kd, s), tm=1024, tn=1024, tk=tks)
    dwlr = _mm("mix_dw_lr", "tn", u2, dlrp, (d, LANES, s), tm=1024, tn=LANES, tk=tks)
    dwg = _mm("mix_dw_g", "tn", u2, dg, (d, 2 * d, s), tm=1024, tn=1024, tk=tks,
              b_spec=pl.BlockSpec((None, tks, 1024), lambda i, j, kk: (j // nkg, kk, j % nkg)))
    plan.grad("w_mix_in", jnp.concatenate([dw1, dw2, dwlr[:, :GATE_RANK], dwg], axis=1))
    return du, (dbias, dwa2p[:GATE_RANK], gsm)


def _device_step(x, target, mod, small, plan):
    s, d = x.shape
    row = lambda i: mod[i:i + 1]
    sh1, sc1, g1, sh2, sc2, g2, sh3, sc3, g3 = (row(i) for i in range(N_MOD))

    onehot = _rel_onehot()
    bias = _mm("rel_bias_expand", "nn", small["rel_bias"], onehot, (A_HEADS, CHUNK * A_BAND, REL_SIZE),
               tm=A_HEADS, tn=4608, tk=REL_SIZE, precision=HIGHEST).reshape(A_HEADS, CHUNK, A_BAND)
    wa2p = jnp.pad(small["w_alpha2"], ((0, LANES - GATE_RANK), (0, 0))).astype(BF16)
    mix_small = (bias, wa2p, small["b_alpha"], small["gla_norm_g"])

    u1 = _modulate("mod1", x, sh1, sc1)
    f1, sv1 = _ffn_forward("ffn1", u1, plan)
    h1, u2 = _resid_ln_fwd("ln1_fwd", x, f1, g1, small["ln1_g"], small["ln1_b"], sh2, sc2, 0.5)
    m, svm = _mix_forward(u2, plan, mix_small)
    h2, u3 = _resid_ln_fwd("ln2_fwd", h1, m, g2, small["ln2_g"], small["ln2_b"], sh3, sc3, 1.0)
    f2, sv2 = _ffn_forward("ffn2", u3, plan)

    dr3, df2, acc3 = _final_ln_loss_bwd("ln3_loss_bwd", h2, f2, target, g3, small["ln3_g"], small["ln3_b"], 0.5)
    du3 = _ffn_backward("ffn2", df2, u3, sv2, plan)
    dr2, dmx, acc2 = _resid_ln_bwd("ln2_bwd", du3, dr3, h1, m, sc3, g2, small["ln2_g"], small["ln2_b"], 1.0)
    du2, (dbias, dwa2, gsm) = _mix_backward(dmx, u2, svm, plan, mix_small)
    dr1, df1, acc1 = _resid_ln_bwd("ln1_bwd", du2, dr2, x, f1, sc2, g1, small["ln1_g"], small["ln1_b"], 0.5)
    du1 = _ffn_backward("ffn1", df1, u1, sv1, plan)
    grad_x, acc0 = _input_grad("input_grad", du1, dr1, x, sc1)

    drel = _mm("rel_bias_grad", "nt", dbias.reshape(A_HEADS, CHUNK * A_BAND), onehot,
               (A_HEADS, REL_SIZE, CHUNK * A_BAND), tm=A_HEADS, tn=REL_SIZE, tk=4608, precision=HIGHEST)
    loss = jnp.sum(acc3[ROW_LOSS])
    dmod = jnp.stack([acc0[ROW_DSH], acc0[ROW_DSC], acc1[ROW_DGATE], acc1[ROW_DSH], acc1[ROW_DSC], acc2[ROW_DGATE],
                      acc2[ROW_DSH], acc2[ROW_DSC], acc3[ROW_DGATE]])
    kd = d // 4
    small_grads = dict(ln1_g=acc1[ROW_DLN_G], ln1_b=acc1[ROW_DLN_B], ln2_g=acc2[ROW_DLN_G], ln2_b=acc2[ROW_DLN_B],
                       ln3_g=acc3[ROW_DLN_G], ln3_b=acc3[ROW_DLN_B], b_alpha=gsm[GLA_ROW_DBALPHA],
                       gla_norm_g=gsm[GLA_ROW_DGNORM, :kd // B_HEADS * 2], rel_bias=drel, w_alpha2=dwa2)
    return loss, grad_x, small_grads, dmod


HBM_SPEC = pl.BlockSpec(memory_space=pl.ANY)


def _mesh_pos():
    return lax.axis_index("x"), lax.axis_index("y"), lax.axis_index("c")


def _other_chips(x, y):
    return [(1 - x, y), (x, 1 - y), (1 - x, 1 - y)]


def _remote(src, dst, send_sem, recv_sem, to):
    return pltpu.make_async_remote_copy(src_ref=src, dst_ref=dst, send_sem=send_sem, recv_sem=recv_sem,
                                        device_id=to, device_id_type=MESH)


def _allgather_rows(name, v):
    m_per, n = v.shape

    def body(x_ref, out_ref, send_sems, recv_sems, local_sem):
        x, y, c = _mesh_pos()
        me, sibling = (x, y, c), (x, y, 1 - c)
        chips = _other_chips(x, y)

        def rows(px, py, pc):
            return out_ref.at[pl.ds((4 * px + 2 * py + pc) * m_per, m_per), :]

        def copy(k, block, to, src=None):
            return _remote(rows(*block) if src is None else src, rows(*block), send_sems.at[k], recv_sems.at[k], to)

        mine = pltpu.make_async_copy(x_ref, rows(*me), local_sem)
        mine.start()
        first = [copy(0, me, sibling, src=x_ref)]
        first += [copy(1 + j, me, (*chip, c), src=x_ref) for j, chip in enumerate(chips)]
        for cp in first:
            cp.start()
        passed = [copy(4 + j, (*chip, c), sibling) for j, chip in enumerate(chips)]
        for j, chip in enumerate(chips):
            copy(1 + j, (*chip, c), me).wait_recv()
            passed[j].start()
        copy(0, sibling, me).wait_recv()
        for j, chip in enumerate(chips):
            copy(4 + j, (*chip, 1 - c), me).wait_recv()
        for cp in first + passed:
            cp.wait_send()
        mine.wait()

    return pl.pallas_call(
        body, name=name, out_shape=jax.ShapeDtypeStruct((N_DEV * m_per, n), v.dtype),
        in_specs=[pl.BlockSpec(memory_space=pltpu.VMEM)], out_specs=pl.BlockSpec(memory_space=pltpu.VMEM),
        scratch_shapes=[pltpu.SemaphoreType.DMA((7,)), pltpu.SemaphoreType.DMA((7,)), pltpu.SemaphoreType.DMA],
    )(v)


def _allgather_weights(shards):
    n = len(shards)

    def body(*refs):
        ins, outs = refs[:n], refs[n:2 * n]
        send_sems, recv_sems = refs[2 * n:]
        x, y, c = _mesh_pos()
        sibling = (x, y, 1 - c)
        chips = _other_chips(x, y)
        j0 = 2 * x + y

        def half(ref, w, j, hc):
            hr = shards[w].shape[0] // 2
            return ref.at[j, pl.ds(hc * hr, hr), :]

        sends = []
        for w in range(n):
            hr = shards[w].shape[0] // 2
            for r, chip in enumerate(chips):
                cp = _remote(ins[w].at[pl.ds(c * hr, hr), :], half(outs[w], w, j0, c), send_sems.at[w, r],
                             recv_sems.at[w, r], (*chip, c))
                cp.start()
                sends.append(cp)
        for w in range(n):
            for r, chip in enumerate(chips):
                jr = 2 * chip[0] + chip[1]
                landed = half(outs[w], w, jr, c)
                _remote(landed, landed, send_sems.at[w, r], recv_sems.at[w, r], (*chip, c)).wait_recv()
                fw = _remote(landed, landed, send_sems.at[w, 3 + r], recv_sems.at[w, 3 + r], sibling)
                fw.start()
                sends.append(fw)
        for w in range(n):
            for r, chip in enumerate(chips):
                jr = 2 * chip[0] + chip[1]
                got = half(outs[w], w, jr, 1 - c)
                _remote(got, got, send_sems.at[w, 3 + r], recv_sems.at[w, 3 + r], sibling).wait_recv()
        for cp in sends:
            cp.wait_send()

    return pl.pallas_call(
        body, name="allgather_weights",
        out_shape=[jax.ShapeDtypeStruct((N_CHIPS,) + sh.shape, sh.dtype) for sh in shards],
        in_specs=[HBM_SPEC] * n, out_specs=[HBM_SPEC] * n,
        scratch_shapes=[pltpu.SemaphoreType.DMA((n, 6)), pltpu.SemaphoreType.DMA((n, 6))],
    )(*shards)


def _stage_gather_ici(shards):
    n = len(shards)

    def copies(ins, outs, send, recv):
        x, y, c = _mesh_pos()
        j0 = 2 * x + y
        for w in range(n):
            hr = shards[w].shape[0] // 2
            for r, chip in enumerate(_other_chips(x, y)):
                jr = 2 * chip[0] + chip[1]
                mine = _remote(ins[w].at[pl.ds(c * hr, hr), :], outs[w].at[j0, pl.ds(c * hr, hr), :],
                               send.at[3 * w + r], recv.at[3 * w + r], (*chip, c))
                landed = outs[w].at[jr, pl.ds(c * hr, hr), :]
                yield mine, _remote(landed, landed, send.at[3 * w + r], recv.at[3 * w + r], (*chip, c))

    def start(*refs):
        for mine, _ in copies(*refs):
            mine.start()

    def finish(*refs):
        pairs = list(copies(*refs))
        for _, theirs in pairs:
            theirs.wait_recv()
        for mine, _ in pairs:
            mine.wait_send()

    outs = [jax.ShapeDtypeStruct((N_CHIPS,) + sh.shape, sh.dtype) for sh in shards]
    return _Stage(shards, outs, 3 * n, start, finish)


def _stage_gather_d2d(partial):
    n = len(partial)

    def copies(ins, outs, send, recv):
        x, y, c = _mesh_pos()
        for w in range(n):
            hr = partial[w].shape[1] // 2
            for r, chip in enumerate(_other_chips(x, y)):
                jr = 2 * chip[0] + chip[1]
                have = pl.ds(c * hr, hr)
                mine = _remote(ins[w].at[jr, have, :], outs[w].at[jr, have, :], send.at[3 * w + r], recv.at[3 * w + r],
                               (x, y, 1 - c))
                got = outs[w].at[jr, pl.ds((1 - c) * hr, hr), :]
                yield mine, _remote(got, got, send.at[3 * w + r], recv.at[3 * w + r], (x, y, 1 - c))

    def start(*refs):
        for mine, _ in copies(*refs):
            mine.start()

    def finish(*refs):
        pairs = list(copies(*refs))
        for _, theirs in pairs:
            theirs.wait_recv()
        for mine, _ in pairs:
            mine.wait_send()

    outs = [jax.ShapeDtypeStruct(p.shape, p.dtype) for p in partial]
    return _Stage(partial, outs, 3 * n, start, finish, aliases={w: w for w in range(n)})


def _stage_exchange_halves(grads):
    n = len(grads)

    def copies(ins, outs, send, recv):
        x, y, c = _mesh_pos()
        for w in range(n):
            hr = grads[w].shape[1] // 2
            yield _remote(ins[w].at[:, pl.ds((1 - c) * hr, hr), :], outs[w], send.at[w], recv.at[w], (x, y, 1 - c))

    def start(*refs):
        for cp in copies(*refs):
            cp.start()

    def finish(*refs):
        cps = list(copies(*refs))
        for cp in cps:
            cp.wait_recv()
        for cp in cps:
            cp.wait_send()

    outs = [jax.ShapeDtypeStruct((N_CHIPS, g.shape[1] // 2, g.shape[2]), g.dtype) for g in grads]
    return _Stage(grads, outs, n, start, finish)


def _stage_scatter(parts):
    n = len(parts)

    def copies(ins, outs, send, recv):
        x, y, c = _mesh_pos()
        for w in range(n):
            for r, chip in enumerate(_other_chips(x, y)):
                jr = 2 * chip[0] + chip[1]
                yield _remote(ins[w].at[jr], outs[w].at[r], send.at[3 * w + r], recv.at[3 * w + r], (*chip, c))

    def start(*refs):
        for cp in copies(*refs):
            cp.start()

    def finish(*refs):
        cps = list(copies(*refs))
        for cp in cps:
            cp.wait_recv()
        for cp in cps:
            cp.wait_send()

    outs = [jax.ShapeDtypeStruct((3,) + p.shape[1:], p.dtype) for p in parts]
    return _Stage(parts, outs, 3 * n, start, finish)


def _stage_share(fulls):
    n = len(fulls)

    def copies(ins, outs, send, recv):
        x, y, c = _mesh_pos()
        for w in range(n):
            hr = fulls[w].shape[0] // 2
            mine = pl.ds(c * hr, hr)
            theirs = outs[w].at[pl.ds((1 - c) * hr, hr), :]
            yield (_remote(ins[w].at[mine, :], outs[w].at[mine, :], send.at[w], recv.at[w], (x, y, 1 - c)),
                   _remote(theirs, theirs, send.at[w], recv.at[w], (x, y, 1 - c)))

    def start(*refs):
        for mine, _ in copies(*refs):
            mine.start()

    def finish(*refs):
        pairs = list(copies(*refs))
        for _, theirs in pairs:
            theirs.wait_recv()
        for mine, _ in pairs:
            mine.wait_send()

    outs = [jax.ShapeDtypeStruct(h.shape, h.dtype) for h in fulls]
    return _Stage(fulls, outs, n, start, finish, aliases={w: w for w in range(n)})


def _run_stages(name, stages):
    return _pcall(None, stages, name=name, out_shape=[], in_specs=[], out_specs=[])()[1]


TILE_BYTES = 2 * 1024 * 1024


def _row_tile(rows, cols, itemsize=4):
    for t in (1024, 512, 256, 128, 64, 32, 16, 8):
        if rows % t == 0 and t * cols * itemsize <= TILE_BYTES:
            return t
    return rows


def _pair_sum(name, g, recv, core):
    _, hr, cols = recv.shape
    tr = _row_tile(hr, cols)
    nb = hr // tr

    def body(c_ref, g_ref, r_ref, o_ref):
        o_ref[...] = (g_ref[...] + r_ref[...]).astype(BF16)

    grid_spec = pltpu.PrefetchScalarGridSpec(
        num_scalar_prefetch=1, grid=(N_CHIPS, nb),
        in_specs=[pl.BlockSpec((None, tr, cols), lambda j, i, cr: (j, cr[0] * nb + i, 0)),
                  pl.BlockSpec((None, tr, cols), lambda j, i, cr: (j, i, 0))],
        out_specs=pl.BlockSpec((None, tr, cols), lambda j, i, cr: (j, i, 0)))
    return pl.pallas_call(body, name=name, out_shape=jax.ShapeDtypeStruct(recv.shape, BF16), grid_spec=grid_spec,
                          compiler_params=_cp("parallel", "parallel"))(core, g, recv)


def _quad_sum(name, own, landed, chip_core):
    _, hr, cols = landed.shape
    tr = _row_tile(hr, cols)
    nb = hr // tr

    def body(cc_ref, own_ref, l_ref, o_ref):
        o_ref[...] = ((own_ref[...].astype(F32) + l_ref[0].astype(F32)) + l_ref[1].astype(F32)) + l_ref[2].astype(F32)

    grid_spec = pltpu.PrefetchScalarGridSpec(
        num_scalar_prefetch=1, grid=(nb,),
        in_specs=[pl.BlockSpec((None, tr, cols), lambda i, cc: (cc[0], i, 0)),
                  pl.BlockSpec((3, tr, cols), lambda i, cc: (0, i, 0))],
        out_specs=pl.BlockSpec((tr, cols), lambda i, cc: (cc[1] * nb + i, 0)))
    return pl.pallas_call(body, name=name, out_shape=jax.ShapeDtypeStruct((2 * hr, cols), F32), grid_spec=grid_spec,
                          compiler_params=_cp("arbitrary"))(chip_core, own, landed)


def _device_sum(name, gathered):
    def body(g_ref, o_ref):
        total = g_ref[0]
        for k in range(1, N_DEV):
            total = total + g_ref[k]
        o_ref[...] = total

    return pl.pallas_call(body, name=name, out_shape=jax.ShapeDtypeStruct(gathered.shape[1:], F32))(gathered)


def _adamw(name, w, g, m, v):
    rows, cols = w.shape
    tr = _row_tile(rows, cols)
    bc1 = 1.0 - ADAM_B1 ** ADAM_STEP
    bc2 = 1.0 - ADAM_B2 ** ADAM_STEP

    def body(w_ref, g_ref, m_ref, v_ref, d_ref, mo_ref, vo_ref):
        gv = g_ref[...]
        mn = ADAM_B1 * m_ref[...] + (1.0 - ADAM_B1) * gv
        vn = ADAM_B2 * v_ref[...] + (1.0 - ADAM_B2) * (gv * gv)
        mo_ref[...] = mn
        vo_ref[...] = vn
        d_ref[...] = -ADAM_LR * ((mn / bc1) / (jnp.sqrt(vn / bc2) + ADAM_EPS) + ADAM_WD * w_ref[...])

    spec = pl.BlockSpec((tr, cols), lambda i: (i, 0))
    return pl.pallas_call(
        body, name=name, out_shape=[jax.ShapeDtypeStruct((rows, cols), F32)] * 3, grid=(rows // tr,),
        in_specs=[spec] * 4, out_specs=[spec] * 3, compiler_params=_cp("parallel"),
    )(w, g, m, v)


WEIGHTS = ["w_ada", "b_ada", "ffn1_w_in", "ffn1_w_out", "ln1_g", "ln1_b", "w_mix_in", "rel_bias", "w_alpha2",
           "b_alpha", "gla_norm_g", "w_proj_a", "w_proj_b", "w_mix_out", "ln2_g", "ln2_b", "ffn2_w_in", "ffn2_w_out",
           "ln3_g", "ln3_b"]
BIG = {"ffn1_w_in": True, "ffn1_w_out": False, "w_mix_in": True, "w_proj_a": True, "w_proj_b": True,
       "w_mix_out": False, "ffn2_w_in": True, "ffn2_w_out": False}
STACKED = ("ffn1_w_in", "ffn2_w_in")
GROUP_FFN1 = ("ffn1_w_in", "ffn1_w_out")
GROUP_MIX = ("w_mix_in", "w_proj_a", "w_proj_b", "w_mix_out")
GROUP_FFN2 = ("ffn2_w_in", "ffn2_w_out")
SMALL = ["ln1_g", "ln1_b", "ln2_g", "ln2_b", "ln3_g", "ln3_b", "b_alpha", "gla_norm_g", "rel_bias", "w_alpha2"]


def _pad_rows(vec, rows=SUBLANES):
    per = -(-vec.shape[0] // (rows * LANES)) * LANES
    return jnp.pad(vec, (0, rows * per - vec.shape[0])).reshape(rows, per)


def _silu(v):
    return v * _sigmoid(v)


class _MeshPlan:
    def __init__(self, shards, chip, core):
        self.shards, self.chip = shards, chip
        self.core1 = core.astype(jnp.int32).reshape(1)
        self.chip_core = jnp.stack([chip, core]).astype(jnp.int32)
        self.partial, self.full, self.local, self.pair, self.half, self.final, self.memos = {}, {}, {}, {}, {}, {}, {}
        ici, d2d, x1, x2, x3 = self.gather_ici, self.gather_d2d, self.exchange, self.scatter, self.share
        self.schedule = {
            "ffn1_in_fwd": [ici(GROUP_MIX)], "ffn1_out_fwd": [d2d(GROUP_MIX)],
            "attn_fwd": [ici(GROUP_FFN2)], "gla_fwd": [d2d(GROUP_FFN2)],
            "merge_bwd": [x1(GROUP_FFN2)], "attn_bwd": [x2(GROUP_FFN2)], "gla_bwd": [x3(GROUP_FFN2)],
            "ffn1_out_bwd": [x1(GROUP_MIX)], "ffn1_dw_in": [x2(GROUP_MIX)],
            "ffn1_dw_out": [x3(GROUP_MIX), x1(GROUP_FFN1[:1])],
            "ffn1_du": [x2(GROUP_FFN1[:1]), x1(GROUP_FFN1[1:])],
        }

    def weight(self, k):
        return self.full[k]

    def grad(self, k, g):
        r, cc = self.shards[k].shape
        if k not in STACKED:
            g = g.reshape(r, N_CHIPS, cc).transpose(1, 0, 2) if BIG[k] else g.reshape(N_CHIPS, r, cc)
        self.local[k] = g

    def memo(self, key, make):
        if key not in self.memos:
            self.memos[key] = make()
        return self.memos[key]

    def host(self, name, call):
        builders = self.schedule.get(name)
        if not builders:
            return call(None)
        built = [b() for b in builders]
        main, comm = call([st for st, _ in built])
        for (_, post), res in zip(built, comm):
            post(res)
        return main

    def run(self, name, builders):
        built = [b() for b in builders]
        for (_, post), res in zip(built, _run_stages(name, [st for st, _ in built])):
            post(res)

    def set_gathered(self, names, gathered):
        for k, g in zip(names, gathered):
            _, r, cc = g.shape
            g = lax.dynamic_update_slice(g, self.shards[k][None], (self.chip, 0, 0))
            if k not in STACKED:
                g = g.transpose(1, 0, 2).reshape(r, N_CHIPS * cc) if BIG[k] else g.reshape(N_CHIPS * r, cc)
            self.full[k] = g

    def gather_ici(self, names):
        def post(res):
            self.partial.update(zip(names, res))
        return lambda: (_stage_gather_ici([self.shards[k] for k in names]), post)

    def gather_d2d(self, names):
        return lambda: (_stage_gather_d2d([self.partial[k] for k in names]), lambda res: self.set_gathered(names, res))

    def exchange(self, names):
        def post(res):
            for k, r in zip(names, res):
                self.pair[k] = _pair_sum(f"pair_sum_{k}", self.local[k], r, self.core1)
        return lambda: (_stage_exchange_halves([self.local[k] for k in names]), post)

    def scatter(self, names):
        def post(res):
            for k, landed in zip(names, res):
                self.half[k] = _quad_sum(f"quad_sum_{k}", self.pair[k], landed, self.chip_core)
        return lambda: (_stage_scatter([self.pair[k] for k in names]), post)

    def share(self, names):
        def post(res):
            self.final.update(zip(names, res))
        return lambda: (_stage_share([self.half[k] for k in names]), post)


def _step(args):
    x_pos, y_pos, c_pos = _mesh_pos()
    chip = 2 * x_pos + y_pos
    dev = 4 * x_pos + 2 * y_pos + c_pos
    w = {k: args[k][0] for k in WEIGHTS}
    mom = {k: args["m_" + k][0] for k in WEIGHTS}
    vel = {k: args["v_" + k][0] for k in WEIGHTS}
    x = args["x"][0]
    target = args["loss_target"][0]
    s, d = x.shape
    kd = d // 4
    rel_sh = w["rel_bias"].shape[1]
    wa2_sh = w["w_alpha2"].shape[1]
    ada_sh = w["w_ada"].shape[1]

    n_rel, n_wa2 = A_HEADS * rel_sh, GATE_RANK * wa2_sh
    packed = _pad_rows(jnp.concatenate([args["c"].reshape(-1), w["rel_bias"].reshape(-1), w["w_alpha2"].reshape(-1)]))
    got = _allgather_rows("gather_small_inputs", packed).reshape(N_DEV, -1)
    c_all = got[:, :d]
    per_chip = got[0::2]
    rel_bias = per_chip[:, d:d + n_rel].reshape(N_CHIPS, A_HEADS, rel_sh).transpose(1, 0, 2).reshape(A_HEADS, -1)
    w_alpha2 = per_chip[:, d + n_rel:d + n_rel + n_wa2].reshape(N_CHIPS, GATE_RANK, wa2_sh).transpose(1, 0, 2)
    w_alpha2 = w_alpha2.reshape(GATE_RANK, -1)

    b_shard = lax.dynamic_slice(w["b_ada"], (chip * ada_sh,), (ada_sh,))
    mod_shard = _mm("ada_fwd", "nn", c_all, w["w_ada"], (N_DEV, ada_sh, d), tm=N_DEV, tn=_tile(ada_sh, (512, 128)),
                    tk=d, precision=HIGHEST, a_fn=_silu, add=jnp.broadcast_to(b_shard[None], (N_DEV, ada_sh)))
    mod_all = _allgather_rows("gather_mod", mod_shard).reshape(N_DEV, N_DEV, ada_sh)[0::2]
    mod_all = mod_all.transpose(1, 0, 2).reshape(N_DEV, N_MOD * d)
    mod = lax.dynamic_index_in_dim(mod_all, dev, 0, keepdims=False).reshape(N_MOD, d)

    names = list(BIG)
    plan = _MeshPlan({k: w[k].astype(BF16) for k in names}, chip, c_pos)
    plan.set_gathered(GROUP_FFN1, _allgather_weights([plan.shards[k] for k in GROUP_FFN1]))

    small = dict(rel_bias=rel_bias, w_alpha2=w_alpha2, b_alpha=w["b_alpha"][None], gla_norm_g=w["gla_norm_g"][None])
    for k in ("ln1_g", "ln1_b", "ln2_g", "ln2_b", "ln3_g", "ln3_b"):
        small[k] = w[k][None]
    loss_local, grad_x, small_grads, dmod = _device_step(x, target, mod, small, plan)
    loss = lax.psum(loss_local, ("x", "y", "c"))
    plan.run("grad_tail_scatter", [plan.scatter(GROUP_FFN1[1:]), plan.share(GROUP_FFN1[:1])])
    plan.run("grad_tail_share", [plan.share(GROUP_FFN1[1:])])

    flat = jnp.concatenate([small_grads[k].reshape(-1) for k in SMALL] + [dmod.reshape(-1)])
    n_small = flat.shape[0] - N_MOD * d
    packed = _pad_rows(flat)
    all_small = _allgather_rows("gather_small_grads", packed).reshape(N_DEV, SUBLANES, -1)
    summed = _device_sum("small_grad_sum", all_small).reshape(-1)
    dmod_all = all_small.reshape(N_DEV, -1)[:, n_small:n_small + N_MOD * d]
    dmod_shard = lax.dynamic_slice(dmod_all, (0, chip * ada_sh), (N_DEV, ada_sh))
    grads = {"b_ada": summed[n_small:n_small + N_MOD * d]}
    off = 0
    for k in SMALL:
        size = small_grads[k].size
        grads[k] = summed[off:off + size].reshape(small_grads[k].shape)
        off += size
    grads["rel_bias"] = lax.dynamic_slice(grads["rel_bias"], (0, chip * rel_sh), (A_HEADS, rel_sh))
    grads["w_alpha2"] = lax.dynamic_slice(grads["w_alpha2"], (0, chip * wa2_sh), (GATE_RANK, wa2_sh))
    grads["w_ada"] = _mm("ada_bwd", "nn", jnp.pad(c_all.T, ((0, 0), (0, LANES - N_DEV))),
                         jnp.pad(dmod_shard, ((0, LANES - N_DEV), (0, 0))), (d, ada_sh, LANES), tm=_tile(d, (1024,)),
                         tn=_tile(ada_sh, (512, 128)), tk=LANES, precision=HIGHEST, a_fn=_silu)

    grads.update(plan.final)

    delta, new_m, new_v = {}, {}, {}
    for k in ["w_ada"] + names:
        delta[k], new_m[k], new_v[k] = _adamw(f"adamw_{k}", w[k], grads[k], mom[k], vel[k])
    tiny = ["b_ada"] + SMALL
    pack = lambda src: _pad_rows(jnp.concatenate([src[k].reshape(-1) for k in tiny]), rows=1).reshape(-1, LANES)
    outs = _adamw("adamw_small", pack(w), pack(grads), pack(mom), pack(vel))
    off = 0
    for k in tiny:
        size = w[k].size
        for dst, src in zip((delta, new_m, new_v), outs):
            dst[k] = src.reshape(-1)[off:off + size].reshape(w[k].shape)
        off += size

    lead = lambda t: t[None]
    return (loss, lead(grad_x), *[lead(grads[k]) for k in WEIGHTS], *[lead(delta[k]) for k in WEIGHTS],
            *[lead(new_m[k]) for k in WEIGHTS], *[lead(new_v[k]) for k in WEIGHTS])


def kernel(x, c, w_ada, b_ada, ffn1_w_in, ffn1_w_out, ln1_g, ln1_b, w_mix_in, rel_bias, w_alpha2, b_alpha, gla_norm_g, w_proj_a, w_proj_b, w_mix_out, ln2_g, ln2_b, ffn2_w_in, ffn2_w_out, ln3_g, ln3_b, loss_target, m_w_ada, m_b_ada, m_ffn1_w_in, m_ffn1_w_out, m_ln1_g, m_ln1_b, m_w_mix_in, m_rel_bias, m_w_alpha2, m_b_alpha, m_gla_norm_g, m_w_proj_a, m_w_proj_b, m_w_mix_out, m_ln2_g, m_ln2_b, m_ffn2_w_in, m_ffn2_w_out, m_ln3_g, m_ln3_b, v_w_ada, v_b_ada, v_ffn1_w_in, v_ffn1_w_out, v_ln1_g, v_ln1_b, v_w_mix_in, v_rel_bias, v_w_alpha2, v_b_alpha, v_gla_norm_g, v_w_proj_a, v_w_proj_b, v_w_mix_out, v_ln2_g, v_ln2_b, v_ffn2_w_in, v_ffn2_w_out, v_ln3_g, v_ln3_b):
    return _step(dict(locals()))
```

```python
import functools

import jax
import jax.numpy as jnp
from jax import lax
from jax.experimental import pallas as pl
from jax.experimental.pallas import tpu as pltpu

F32 = jnp.float32
BF16 = jnp.bfloat16
MESH = pl.DeviceIdType.MESH
HIGHEST = lax.Precision.HIGHEST

VMEM_LIMIT_BYTES = 56 * 1024 * 1024
LANES = 128
SUBLANES = 8

CHUNK = 64
A_HEADS = 16
A_HEAD_DIM = 64
A_PAST_CHUNKS = 8
A_BAND = (A_PAST_CHUNKS + 1) * CHUNK
A_PAD = A_PAST_CHUNKS * CHUNK
REL_CLIP = 256
REL_SIZE = REL_CLIP + CHUNK
B_HEADS = 4
GATE_RANK = 16
GATE_TAU = 16.0
N_MOD = 9
DEPTH = 1
ALPHA = (2.0 * DEPTH) ** 0.25
LN_EPS = 1e-5
RMS_EPS = 1e-6
ADAM_LR = 0.001
ADAM_B1 = 0.9
ADAM_B2 = 0.999
ADAM_EPS = 1e-08
ADAM_WD = 0.01
ADAM_STEP = 10
NEG_BIG = -1e30

N_CHIPS = 4
N_DEV = 8


def _cp(*sem):
    return pltpu.CompilerParams(dimension_semantics=sem, vmem_limit_bytes=VMEM_LIMIT_BYTES)


class _Stage:
    def __init__(self, arrays, out_shapes, n_sems, start, finish, aliases=None):
        self.arrays, self.out_shapes, self.n_sems = list(arrays), list(out_shapes), n_sems
        self.start, self.finish, self.aliases = start, finish, dict(aliases or {})


def _pcall(body, stages, *, name, out_shape, in_specs, out_specs, grid=(), scratch_shapes=(), compiler_params=None):
    single = not isinstance(out_shape, (list, tuple))
    outs = [out_shape] if single else list(out_shape)
    ospecs = [out_specs] if single else list(out_specs)
    in_specs, scratch_shapes = list(in_specs), list(scratch_shapes)
    n_in, n_out, n_sc = len(in_specs), len(outs), len(scratch_shapes)
    stages = list(stages or [])
    c_in = [a for st in stages for a in st.arrays]
    c_out = [o for st in stages for o in st.out_shapes]
    aliases = {}
    io, oo = n_in, n_out
    for st in stages:
        for a, b in st.aliases.items():
            aliases[io + a] = oo + b
        io += len(st.arrays)
        oo += len(st.out_shapes)

    def wrapped(*refs):
        ins = refs[:n_in]
        cins = refs[n_in:n_in + len(c_in)]
        base = n_in + len(c_in)
        mouts = refs[base:base + n_out]
        couts = refs[base + n_out:base + n_out + len(c_out)]
        base += n_out + len(c_out)
        scr = refs[base:base + n_sc]
        sems = refs[base + n_sc:]

        def each(phase):
            i = o = 0
            for k, st in enumerate(stages):
                fn = st.start if phase == 0 else st.finish
                fn(cins[i:i + len(st.arrays)], couts[o:o + len(st.out_shapes)], sems[2 * k], sems[2 * k + 1])
                i += len(st.arrays)
                o += len(st.out_shapes)

        if stages and grid:
            first = functools.reduce(jnp.logical_and, [pl.program_id(a) == 0 for a in range(len(grid))])
            last = functools.reduce(jnp.logical_and, [pl.program_id(a) == g - 1 for a, g in enumerate(grid)])
            pl.when(first)(lambda: each(0))
            if body is not None:
                body(*ins, *mouts, *scr)
            pl.when(last)(lambda: each(1))
        else:
            each(0)
            if body is not None:
                body(*ins, *mouts, *scr)
            each(1)

    sem_shapes = []
    for st in stages:
        sem_shapes += [pltpu.SemaphoreType.DMA((st.n_sems,)), pltpu.SemaphoreType.DMA((st.n_sems,))]
    kwargs = dict(grid=grid) if grid else {}
    if compiler_params is not None:
        kwargs["compiler_params"] = compiler_params

    def run(*operands):
        res = pl.pallas_call(
            wrapped, name=name, out_shape=outs + c_out, in_specs=in_specs + [HBM_SPEC] * len(c_in),
            out_specs=ospecs + [HBM_SPEC] * len(c_out), scratch_shapes=scratch_shapes + sem_shapes,
            input_output_aliases=aliases, **kwargs)(*operands, *c_in)
        main = res[0] if single else tuple(res[:n_out])
        if not stages:
            return main
        comm, o = [], n_out
        for st in stages:
            comm.append(list(res[o:o + len(st.out_shapes)]))
            o += len(st.out_shapes)
        return main, comm

    return run


def _tile(n, prefs):
    for t in prefs:
        if t <= n and n % t == 0:
            return t
    return n


_DIMS = {"nn": (((1,), (0,)), ((), ())), "nt": (((1,), (1,)), ((), ())), "tn": (((0,), (0,)), ((), ()))}


def _dot(a, b, mode="nn", precision=None):
    return lax.dot_general(a, b, _DIMS[mode], precision=precision, preferred_element_type=F32)


def _sigmoid(x):
    return 1.0 / (1.0 + jnp.exp(-x))


def _mm(name, mode, a, b, mnk, *, tm, tn, tk, out_dtype=F32, precision=None, a_spec=None, b_spec=None,
        out_shape=None, o_spec=None, add=None, a_fn=None, stages=None):
    m, n, k = mnk
    assert m % tm == 0 and n % tn == 0 and k % tk == 0, (name, mnk, tm, tn, tk)
    nk = k // tk
    if a_spec is None:
        a_spec = {"nn": pl.BlockSpec((tm, tk), lambda i, j, kk: (i, kk)),
                  "nt": pl.BlockSpec((tm, tk), lambda i, j, kk: (i, kk)),
                  "tn": pl.BlockSpec((tk, tm), lambda i, j, kk: (kk, i))}[mode]
    if b_spec is None:
        b_spec = {"nn": pl.BlockSpec((tk, tn), lambda i, j, kk: (kk, j)),
                  "nt": pl.BlockSpec((tn, tk), lambda i, j, kk: (j, kk)),
                  "tn": pl.BlockSpec((tk, tn), lambda i, j, kk: (kk, j))}[mode]
    if o_spec is None:
        o_spec = pl.BlockSpec((tm, tn), lambda i, j, kk: (i, j))
    if out_shape is None:
        out_shape = (m, n)
    has_add = add is not None

    def body(*refs):
        a_ref, b_ref = refs[0], refs[1]
        add_ref = refs[2] if has_add else None
        o_ref = refs[3] if has_add else refs[2]
        av = a_ref[...]
        if a_fn is not None:
            av = a_fn(av)
        part = _dot(av, b_ref[...], mode, precision)

        def finish(total):
            if has_add:
                total = total + add_ref[...]
            o_ref[...] = total.astype(out_dtype)

        if nk == 1:
            finish(part)
        else:
            acc_ref = refs[-1]
            kk = pl.program_id(2)

            @pl.when(kk == 0)
            def _():
                acc_ref[...] = part

            @pl.when(kk > 0)
            def _():
                acc_ref[...] += part

            @pl.when(kk == nk - 1)
            def _():
                finish(acc_ref[...])

    in_specs = [a_spec, b_spec]
    operands = [a, b]
    if has_add:
        in_specs.append(pl.BlockSpec((tm, tn), lambda i, j, kk: (i, j)))
        operands.append(add)
    return _pcall(
        body, stages, name=name, out_shape=jax.ShapeDtypeStruct(out_shape, out_dtype), grid=(m // tm, n // tn, nk),
        in_specs=in_specs, out_specs=o_spec,
        scratch_shapes=[pltpu.VMEM((tm, tn), F32)] if nk > 1 else [],
        compiler_params=_cp("arbitrary", "arbitrary", "arbitrary") if stages else _cp("parallel", "parallel", "arbitrary"),
    )(*operands)


def _row_spec(tr, d):
    return pl.BlockSpec((tr, d), lambda i: (i, 0))


def _vec_spec(d, rows=1):
    return pl.BlockSpec((rows, d), lambda i: (0, 0))


def _modulate(name, x, sh, sc):
    s, d = x.shape
    tr = _tile(s, (512, 256))

    def body(x_ref, sh_ref, sc_ref, o_ref):
        o_ref[...] = (x_ref[...] * (1.0 + sc_ref[...]) + sh_ref[...]).astype(BF16)

    return pl.pallas_call(
        body, name=name, out_shape=jax.ShapeDtypeStruct((s, d), BF16), grid=(s // tr,),
        in_specs=[_row_spec(tr, d), _vec_spec(d), _vec_spec(d)], out_specs=_row_spec(tr, d),
        compiler_params=_cp("parallel"),
    )(x, sh, sc)


def _ln_stats(r):
    mu = jnp.mean(r, axis=-1, keepdims=True)
    xc = r - mu
    var = jnp.mean(xc * xc, axis=-1, keepdims=True)
    rstd = lax.rsqrt(var + LN_EPS)
    return xc * rstd, rstd


def _resid_ln_fwd(name, x, f, gate, ln_g, ln_b, sh_n, sc_n, coef):
    s, d = x.shape
    tr = _tile(s, (256,))

    def body(x_ref, f_ref, gate_ref, g_ref, b_ref, sh_ref, sc_ref, h_ref, u_ref):
        r = ALPHA * x_ref[...] + (coef * gate_ref[...]) * f_ref[...]
        xhat, _ = _ln_stats(r)
        h = xhat * g_ref[...] + b_ref[...]
        h_ref[...] = h
        u_ref[...] = (h * (1.0 + sc_ref[...]) + sh_ref[...]).astype(BF16)

    return pl.pallas_call(
        body, name=name, out_shape=(jax.ShapeDtypeStruct((s, d), F32), jax.ShapeDtypeStruct((s, d), BF16)),
        grid=(s // tr,), in_specs=[_row_spec(tr, d), _row_spec(tr, d)] + [_vec_spec(d)] * 5,
        out_specs=(_row_spec(tr, d), _row_spec(tr, d)), compiler_params=_cp("parallel"),
    )(x, f, gate, ln_g, ln_b, sh_n, sc_n)


ROW_DSC, ROW_DSH, ROW_DLN_G, ROW_DLN_B, ROW_DGATE, ROW_LOSS = 0, 1, 2, 3, 4, 5


def _ln_bwd_core(dy, xhat, rstd, ln_g):
    dxhat = dy * ln_g
    m1 = jnp.mean(dxhat, axis=-1, keepdims=True)
    m2 = jnp.mean(dxhat * xhat, axis=-1, keepdims=True)
    return rstd * (dxhat - m1 - xhat * m2)


def _colsum(v):
    return jnp.sum(v, axis=0, keepdims=True)


def _final_ln_loss_bwd(name, x, f, target, gate, ln_g, ln_b, coef):
    s, d = x.shape
    tr = _tile(s, (256,))
    inv_d = 1.0 / d

    def body(x_ref, f_ref, t_ref, gate_ref, g_ref, b_ref, dr_ref, df_ref, acc_ref):
        @pl.when(pl.program_id(0) == 0)
        def _():
            acc_ref[...] = jnp.zeros_like(acc_ref)

        fv = f_ref[...]
        r = ALPHA * x_ref[...] + (coef * gate_ref[...]) * fv
        xhat, rstd = _ln_stats(r)
        h = xhat * g_ref[...] + b_ref[...]
        err = h - t_ref[...]
        dy = err * inv_d
        dr = _ln_bwd_core(dy, xhat, rstd, g_ref[...])
        dr_ref[...] = dr
        df_ref[...] = ((coef * gate_ref[...]) * dr).astype(BF16)
        acc_ref[ROW_DLN_G:ROW_DLN_G + 1, :] += _colsum(dy * xhat)
        acc_ref[ROW_DLN_B:ROW_DLN_B + 1, :] += _colsum(dy)
        acc_ref[ROW_DGATE:ROW_DGATE + 1, :] += _colsum((coef * dr) * fv)
        acc_ref[ROW_LOSS:ROW_LOSS + 1, :] += _colsum(err * err) * (0.5 * inv_d)

    return pl.pallas_call(
        body, name=name,
        out_shape=(jax.ShapeDtypeStruct((s, d), F32), jax.ShapeDtypeStruct((s, d), BF16),
                   jax.ShapeDtypeStruct((SUBLANES, d), F32)),
        grid=(s // tr,), in_specs=[_row_spec(tr, d)] * 3 + [_vec_spec(d)] * 3,
        out_specs=(_row_spec(tr, d), _row_spec(tr, d), _vec_spec(d, SUBLANES)),
        compiler_params=_cp("arbitrary"),
    )(x, f, target, gate, ln_g, ln_b)


def _resid_ln_bwd(name, du_n, dr_n, x, f, sc_n, gate, ln_g, ln_b, coef):
    s, d = x.shape
    tr = _tile(s, (256,))

    def body(du_ref, drn_ref, x_ref, f_ref, sc_ref, gate_ref, g_ref, b_ref, dr_ref, df_ref, acc_ref):
        @pl.when(pl.program_id(0) == 0)
        def _():
            acc_ref[...] = jnp.zeros_like(acc_ref)

        fv = f_ref[...]
        du = du_ref[...]
        r = ALPHA * x_ref[...] + (coef * gate_ref[...]) * fv
        xhat, rstd = _ln_stats(r)
        h = xhat * g_ref[...] + b_ref[...]
        dy = du * (1.0 + sc_ref[...]) + ALPHA * drn_ref[...]
        dr = _ln_bwd_core(dy, xhat, rstd, g_ref[...])
        dr_ref[...] = dr
        df_ref[...] = ((coef * gate_ref[...]) * dr).astype(BF16)
        acc_ref[ROW_DSC:ROW_DSC + 1, :] += _colsum(du * h)
        acc_ref[ROW_DSH:ROW_DSH + 1, :] += _colsum(du)
        acc_ref[ROW_DLN_G:ROW_DLN_G + 1, :] += _colsum(dy * xhat)
        acc_ref[ROW_DLN_B:ROW_DLN_B + 1, :] += _colsum(dy)
        acc_ref[ROW_DGATE:ROW_DGATE + 1, :] += _colsum((coef * dr) * fv)

    return pl.pallas_call(
        body, name=name,
        out_shape=(jax.ShapeDtypeStruct((s, d), F32), jax.ShapeDtypeStruct((s, d), BF16),
                   jax.ShapeDtypeStruct((SUBLANES, d), F32)),
        grid=(s // tr,), in_specs=[_row_spec(tr, d)] * 4 + [_vec_spec(d)] * 4,
        out_specs=(_row_spec(tr, d), _row_spec(tr, d), _vec_spec(d, SUBLANES)),
        compiler_params=_cp("arbitrary"),
    )(du_n, dr_n, x, f, sc_n, gate, ln_g, ln_b)


def _input_grad(name, du, dr, x, sc):
    s, d = x.shape
    tr = _tile(s, (256,))

    def body(du_ref, dr_ref, x_ref, sc_ref, gx_ref, acc_ref):
        @pl.when(pl.program_id(0) == 0)
        def _():
            acc_ref[...] = jnp.zeros_like(acc_ref)

        du = du_ref[...]
        gx_ref[...] = du * (1.0 + sc_ref[...]) + ALPHA * dr_ref[...]
        acc_ref[ROW_DSC:ROW_DSC + 1, :] += _colsum(du * x_ref[...])
        acc_ref[ROW_DSH:ROW_DSH + 1, :] += _colsum(du)

    return pl.pallas_call(
        body, name=name,
        out_shape=(jax.ShapeDtypeStruct((s, d), F32), jax.ShapeDtypeStruct((SUBLANES, d), F32)),
        grid=(s // tr,), in_specs=[_row_spec(tr, d)] * 3 + [_vec_spec(d)],
        out_specs=(_row_spec(tr, d), _vec_spec(d, SUBLANES)), compiler_params=_cp("arbitrary"),
    )(du, dr, x, sc)


def _ffn_in_fwd(name, u, w_in, stages=None):
    s, d = u.shape
    cs = w_in.shape[2]
    f = 2 * cs
    tm, tn = _tile(s, (2048, 1024, 512)), _tile(cs, (256, 128))
    nb = f // tn
    nbs = cs // tn

    def body(u_ref, wa_ref, wb_ref, ab_ref, act_ref):
        uv = u_ref[...]
        a = _dot(uv, wa_ref[...])
        b = _dot(uv, wb_ref[...])
        ab_ref[0] = a.astype(BF16)
        ab_ref[1] = b.astype(BF16)
        act_ref[...] = (a * _sigmoid(a) * b).astype(BF16)

    return _pcall(
        body, stages, name=name,
        out_shape=(jax.ShapeDtypeStruct((2, s, f), BF16), jax.ShapeDtypeStruct((s, f), BF16)),
        grid=(s // tm, nb),
        in_specs=[pl.BlockSpec((tm, d), lambda i, j: (i, 0)),
                  pl.BlockSpec((None, d, tn), lambda i, j: (j // nbs, 0, j % nbs)),
                  pl.BlockSpec((None, d, tn), lambda i, j: (2 + j // nbs, 0, j % nbs))],
        out_specs=(pl.BlockSpec((2, tm, tn), lambda i, j: (0, i, j)), pl.BlockSpec((tm, tn), lambda i, j: (i, j))),
        compiler_params=_cp("arbitrary", "arbitrary"),
    )(u, w_in, w_in)


def _ffn_out_bwd(name, df, w_out, ab, stages=None):
    s, d = df.shape
    f = w_out.shape[0]
    tm, tn = _tile(s, (1024, 512)), _tile(f, (512, 256, 128))

    def body(df_ref, w_ref, ab_ref, dab_ref):
        dact = _dot(df_ref[...], w_ref[...], "nt")
        a = ab_ref[0].astype(F32)
        b = ab_ref[1].astype(F32)
        sg = _sigmoid(a)
        dab_ref[0] = (dact * b * (sg * (1.0 + a * (1.0 - sg)))).astype(BF16)
        dab_ref[1] = (dact * (a * sg)).astype(BF16)

    return _pcall(
        body, stages, name=name, out_shape=jax.ShapeDtypeStruct((2, s, f), BF16), grid=(s // tm, f // tn),
        in_specs=[pl.BlockSpec((tm, d), lambda i, j: (i, 0)), pl.BlockSpec((tn, d), lambda i, j: (j, 0)),
                  pl.BlockSpec((2, tm, tn), lambda i, j: (0, i, j))],
        out_specs=pl.BlockSpec((2, tm, tn), lambda i, j: (0, i, j)),
        compiler_params=_cp("arbitrary", "arbitrary"),
    )(df, w_out, ab)


def _ffn_forward(tag, u, plan):
    w_in, w_out = plan.weight(f"{tag}_w_in"), plan.weight(f"{tag}_w_out")
    s, d = u.shape
    f = w_out.shape[0]
    ab, act = plan.host(f"{tag}_in_fwd", lambda st: _ffn_in_fwd(f"{tag}_in_fwd", u, w_in, st))
    out = plan.host(f"{tag}_out_fwd", lambda st: _mm(
        f"{tag}_out_fwd", "nn", act, w_out, (s, d, f), tm=_tile(s, (1024,)), tn=_tile(d, (1024,)),
        tk=_tile(f, (1408, 512, 128)), stages=st))
    return out, (ab, act)


def _ffn_backward(tag, df, u, saved, plan):
    w_in, w_out = plan.weight(f"{tag}_w_in"), plan.weight(f"{tag}_w_out")
    ab, act = saved
    s, d = u.shape
    f = w_out.shape[0]
    dab = plan.host(f"{tag}_out_bwd", lambda st: _ffn_out_bwd(f"{tag}_out_bwd", df, w_out, ab, st))
    cs = w_in.shape[2]
    tk = _tile(cs, (1408, 256, 128))
    nkh, nks = f // tk, cs // tk
    tmd = _tile(d, (1024,))
    tn = tk
    nbh, nbs = f // tn, cs // tn
    tks = _tile(s, (1024,))
    plan.grad(f"{tag}_w_in", plan.host(f"{tag}_dw_in", lambda st: _mm(
        f"{tag}_dw_in", "tn", u, dab, (d, 2 * f, s), tm=tmd, tn=tn, tk=tks,
        b_spec=pl.BlockSpec((None, tks, tn), lambda i, j, kk: (j // nbh, kk, j % nbh)), out_shape=(N_CHIPS, d, cs),
        o_spec=pl.BlockSpec((None, tmd, tn), lambda i, j, kk: (j // nbs, i, j % nbs)), stages=st)))
    plan.grad(f"{tag}_w_out", plan.host(f"{tag}_dw_out", lambda st: _mm(
        f"{tag}_dw_out", "tn", act, df, (f, d, s), tm=_tile(f, (1408, 512, 128)), tn=tmd, tk=tks, stages=st)))
    return plan.host(f"{tag}_du", lambda st: _mm(
        f"{tag}_du", "nt", dab, w_in, (s, d, 2 * f), tm=_tile(s, (1024,)), tn=tmd, tk=tk,
        a_spec=pl.BlockSpec((None, _tile(s, (1024,)), tk), lambda i, j, kk: (kk // nkh, i, kk % nkh)),
        b_spec=pl.BlockSpec((None, tmd, tk), lambda i, j, kk: (kk // nks, j, kk % nks)), stages=st))


ATTN_Q = 4 * CHUNK
ATTN_W = ATTN_Q + A_PAD


def _band_bias(bias):
    n = ATTN_Q // CHUNK
    rows = [jnp.pad(bias, ((0, 0), (0, 0), (i * CHUNK, (n - 1 - i) * CHUNK)), constant_values=NEG_BIG)
            for i in range(n)]
    return jnp.concatenate(rows, axis=1)


def _band_bias_grad(dband):
    n = ATTN_Q // CHUNK
    parts = [dband[:, i * CHUNK:(i + 1) * CHUNK, i * CHUNK:i * CHUNK + A_BAND] for i in range(n)]
    return functools.reduce(jnp.add, parts)


def _attn_probs(q, kw, bias, key0):
    sc = _dot(q, kw, "nt") * (A_HEAD_DIM ** -0.5) + bias
    ks = lax.broadcasted_iota(jnp.int32, sc.shape, 1)
    sc = jnp.where(key0 + ks >= 0, sc, NEG_BIG)
    p = jnp.exp(sc - jnp.max(sc, axis=-1, keepdims=True))
    return p / jnp.sum(p, axis=-1, keepdims=True)


def _attn_fwd(q, kp, vp, band, stages=None):
    h, s, dh = q.shape
    assert s % ATTN_Q == 0

    def body(q_ref, k_ref, v_ref, b_ref, o_ref):
        base = pl.multiple_of(pl.program_id(1) * ATTN_Q, ATTN_Q)
        p = _attn_probs(q_ref[...], k_ref[pl.ds(base, ATTN_W), :], b_ref[...], base - A_PAD)
        o_ref[...] = _dot(p.astype(BF16), v_ref[pl.ds(base, ATTN_W), :]).astype(BF16)

    return _pcall(
        body, stages, name="attn_fwd", out_shape=jax.ShapeDtypeStruct((h, s, dh), BF16), grid=(h, s // ATTN_Q),
        in_specs=[pl.BlockSpec((None, ATTN_Q, dh), lambda hh, i: (hh, i, 0)),
                  pl.BlockSpec((None, s + A_PAD, dh), lambda hh, i: (hh, 0, 0)),
                  pl.BlockSpec((None, s + A_PAD, dh), lambda hh, i: (hh, 0, 0)),
                  pl.BlockSpec((None, ATTN_Q, ATTN_W), lambda hh, i: (hh, 0, 0))],
        out_specs=pl.BlockSpec((None, ATTN_Q, dh), lambda hh, i: (hh, i, 0)),
        compiler_params=_cp("arbitrary", "arbitrary"),
    )(q, kp, vp, band)


def _attn_bwd(q, kp, vp, band, do, stages=None):
    h, s, dh = q.shape
    scale = A_HEAD_DIM ** -0.5

    def body(q_ref, k_ref, v_ref, b_ref, do_ref, dq_ref, dk_ref, dv_ref, db_ref):
        @pl.when(pl.program_id(1) == 0)
        def _():
            dk_ref[...] = jnp.zeros_like(dk_ref)
            dv_ref[...] = jnp.zeros_like(dv_ref)
            db_ref[...] = jnp.zeros_like(db_ref)

        base = pl.multiple_of(pl.program_id(1) * ATTN_Q, ATTN_Q)
        window = pl.ds(base, ATTN_W)
        kw = k_ref[window, :]
        qv = q_ref[...]
        dov = do_ref[...]
        p = _attn_probs(qv, kw, b_ref[...], base - A_PAD)
        dp = _dot(dov, v_ref[window, :], "nt")
        ds = p * (dp - jnp.sum(p * dp, axis=-1, keepdims=True))
        db_ref[...] += ds
        dsb = (ds * scale).astype(BF16)
        dq_ref[...] = _dot(dsb, kw).astype(BF16)
        dk_ref[window, :] += _dot(dsb, qv, "tn")
        dv_ref[window, :] += _dot(p.astype(BF16), dov, "tn")

    kv_spec = pl.BlockSpec((None, s + A_PAD, dh), lambda hh, i: (hh, 0, 0))
    q_spec = pl.BlockSpec((None, ATTN_Q, dh), lambda hh, i: (hh, i, 0))
    b_spec = pl.BlockSpec((None, ATTN_Q, ATTN_W), lambda hh, i: (hh, 0, 0))
    return _pcall(
        body, stages, name="attn_bwd",
        out_shape=(jax.ShapeDtypeStruct((h, s, dh), BF16), jax.ShapeDtypeStruct((h, s + A_PAD, dh), F32),
                   jax.ShapeDtypeStruct((h, s + A_PAD, dh), F32), jax.ShapeDtypeStruct((h, ATTN_Q, ATTN_W), F32)),
        grid=(h, s // ATTN_Q), in_specs=[q_spec, kv_spec, kv_spec, b_spec, q_spec],
        out_specs=(q_spec, kv_spec, kv_spec, b_spec), compiler_params=_cp("arbitrary", "arbitrary"),
    )(q, kp, vp, band, do)


def _rel_onehot():
    qi = jnp.arange(CHUNK)[:, None]
    ks = jnp.arange(A_BAND)[None, :]
    idx = (jnp.clip(ks - A_PAD - qi, -REL_CLIP, CHUNK - 1) + REL_CLIP).reshape(1, CHUNK * A_BAND)
    return (jnp.arange(REL_SIZE)[:, None] == idx).astype(F32)


def _gla_gate(lr, wa2, balpha):
    z = _dot(lr, wa2) + balpha
    la = (jnp.minimum(z, 0.0) - jnp.log(1.0 + jnp.exp(-jnp.abs(z)))) * (1.0 / GATE_TAU)
    row = lax.broadcasted_iota(jnp.int32, (CHUNK, CHUNK), 0)
    col = lax.broadcasted_iota(jnp.int32, (CHUNK, CHUNK), 1)
    cum = _dot((row >= col).astype(F32), la, precision=HIGHEST)
    return z, la, cum


def _gla_dims(p2):
    kd = p2.shape[1] // 6
    hk = kd // B_HEADS
    hv = 2 * hk
    return kd, hk, hv


def _gla_fwd(p2, lrp, wa2p, balpha, gnorm, stages=None):
    s = p2.shape[0]
    kd, hk, hv = _gla_dims(p2)
    nc = s // CHUNK
    qscale = hk ** -0.5

    def body(p_ref, lr_ref, wa_ref, ba_ref, gn_ref, yb_ref, st_ref, state):
        @pl.when(pl.program_id(0) == 0)
        def _():
            state[...] = jnp.zeros_like(state)

        _, _, cum = _gla_gate(lr_ref[...], wa_ref[...], ba_ref[...])
        last = cum[CHUNK - 1:CHUNK, :]
        e = jnp.exp(last - cum)
        dch = jnp.exp(last)
        gn = gn_ref[...]
        for hh in range(B_HEADS):
            ks = slice(hh * hk, (hh + 1) * hk)
            q = p_ref[:, hh * hk:(hh + 1) * hk].astype(F32)
            k = p_ref[:, kd + hh * hk:kd + (hh + 1) * hk].astype(F32)
            v = p_ref[:, 2 * kd + hh * hv:2 * kd + (hh + 1) * hv]
            rg = p_ref[:, 4 * kd + hh * hv:4 * kd + (hh + 1) * hv].astype(F32)
            kdec = (k * e[:, ks]).astype(BF16)
            st = state[hh] * dch[:, ks] + _dot(v, kdec, "tn")
            state[hh] = st
            st_ref[hh] = st
            o = _dot((q * qscale).astype(BF16), st.astype(BF16), "nt")
            rinv = lax.rsqrt(jnp.mean(o * o, axis=-1, keepdims=True) + RMS_EPS)
            yb_ref[:, hh * hv:(hh + 1) * hv] = ((o * rinv * gn) * (rg * _sigmoid(rg))).astype(BF16)

    return _pcall(
        body, stages, name="gla_fwd",
        out_shape=(jax.ShapeDtypeStruct((s, 2 * kd), BF16), jax.ShapeDtypeStruct((nc, B_HEADS, hv, hk), F32)),
        grid=(nc,),
        in_specs=[pl.BlockSpec((CHUNK, 6 * kd), lambda i: (i, 0)), pl.BlockSpec((CHUNK, LANES), lambda i: (i, 0)),
                  pl.BlockSpec((LANES, kd), lambda i: (0, 0)), pl.BlockSpec((1, kd), lambda i: (0, 0)),
                  pl.BlockSpec((1, hv), lambda i: (0, 0))],
        out_specs=(pl.BlockSpec((CHUNK, 2 * kd), lambda i: (i, 0)),
                   pl.BlockSpec((None, B_HEADS, hv, hk), lambda i: (i, 0, 0, 0))),
        scratch_shapes=[pltpu.VMEM((B_HEADS, hv, hk), F32)], compiler_params=_cp("arbitrary"),
    )(p2, lrp, wa2p, balpha, gnorm)


GLA_ROW_DBALPHA, GLA_ROW_DGNORM = 0, 1


def _gla_bwd(p2, lrp, wa2p, balpha, gnorm, states, dyb, stages=None):
    s = p2.shape[0]
    kd, hk, hv = _gla_dims(p2)
    nc = s // CHUNK
    qscale = hk ** -0.5

    def body(p_ref, lr_ref, wa_ref, ba_ref, gn_ref, st_ref, sp_ref, dy_ref, dp_ref, dz_ref, sm_ref, gcar):
        i = pl.program_id(0)

        @pl.when(i == 0)
        def _():
            gcar[...] = jnp.zeros_like(gcar)
            sm_ref[...] = jnp.zeros_like(sm_ref)

        has_prev = (i < nc - 1).astype(F32)
        z, _, cum = _gla_gate(lr_ref[...], wa_ref[...], ba_ref[...])
        last = cum[CHUNK - 1:CHUNK, :]
        e = jnp.exp(last - cum)
        dch = jnp.exp(last)
        sgn = _sigmoid(-z) * (1.0 / GATE_TAU)
        gn = gn_ref[...]
        row = lax.broadcasted_iota(jnp.int32, (CHUNK, CHUNK), 0)
        col = lax.broadcasted_iota(jnp.int32, (CHUNK, CHUNK), 1)
        tri_strict = (row > col).astype(F32)
        for hh in range(B_HEADS):
            ks = slice(hh * hk, (hh + 1) * hk)
            q = p_ref[:, hh * hk:(hh + 1) * hk].astype(F32)
            k = p_ref[:, kd + hh * hk:kd + (hh + 1) * hk].astype(F32)
            v = p_ref[:, 2 * kd + hh * hv:2 * kd + (hh + 1) * hv]
            rg = p_ref[:, 4 * kd + hh * hv:4 * kd + (hh + 1) * hv].astype(F32)
            kdecf = k * e[:, ks]
            kdec = kdecf.astype(BF16)
            st16 = st_ref[hh].astype(BF16)
            qs = (q * qscale).astype(BF16)
            o = _dot(qs, st16, "nt")
            rinv = lax.rsqrt(jnp.mean(o * o, axis=-1, keepdims=True) + RMS_EPS)
            dy = dy_ref[:, hh * hv:(hh + 1) * hv].astype(F32)
            sg = _sigmoid(rg)
            onorm = o * rinv
            drg = dy * (onorm * gn) * (sg * (1.0 + rg * (1.0 - sg)))
            dob = dy * (rg * sg)
            sm_ref[GLA_ROW_DGNORM:GLA_ROW_DGNORM + 1, 0:hv] += _colsum(dob * onorm)
            t = dob * gn
            do = rinv * (t - onorm * jnp.mean(t * onorm, axis=-1, keepdims=True))
            do16 = do.astype(BF16)
            dq = _dot(do16, st16) * qscale
            gt = _dot(do16, qs, "tn") + gcar[hh]
            gcar[hh] = gt * dch[:, ks]
            dd = _colsum(gt * sp_ref[hh]) * has_prev
            gt16 = gt.astype(BF16)
            dkdec = _dot(v, gt16)
            dv = _dot(kdec, gt16, "nt")
            dla = dd * dch[:, ks] + _dot(tri_strict, dkdec * kdecf, precision=HIGHEST)
            dzh = dla * sgn[:, ks]
            sm_ref[GLA_ROW_DBALPHA:GLA_ROW_DBALPHA + 1, hh * hk:(hh + 1) * hk] += _colsum(dzh)
            dz_ref[:, hh * hk:(hh + 1) * hk] = dzh.astype(BF16)
            dp_ref[:, hh * hk:(hh + 1) * hk] = dq.astype(BF16)
            dp_ref[:, kd + hh * hk:kd + (hh + 1) * hk] = (dkdec * e[:, ks]).astype(BF16)
            dp_ref[:, 2 * kd + hh * hv:2 * kd + (hh + 1) * hv] = dv.astype(BF16)
            dp_ref[:, 4 * kd + hh * hv:4 * kd + (hh + 1) * hv] = drg.astype(BF16)

    rev = lambda i: (nc - 1 - i, 0)
    return _pcall(
        body, stages, name="gla_bwd",
        out_shape=(jax.ShapeDtypeStruct((s, 6 * kd), BF16), jax.ShapeDtypeStruct((s, kd), BF16),
                   jax.ShapeDtypeStruct((SUBLANES, kd), F32)),
        grid=(nc,),
        in_specs=[pl.BlockSpec((CHUNK, 6 * kd), rev), pl.BlockSpec((CHUNK, LANES), rev),
                  pl.BlockSpec((LANES, kd), lambda i: (0, 0)), pl.BlockSpec((1, kd), lambda i: (0, 0)),
                  pl.BlockSpec((1, hv), lambda i: (0, 0)),
                  pl.BlockSpec((None, B_HEADS, hv, hk), lambda i: (nc - 1 - i, 0, 0, 0)),
                  pl.BlockSpec((None, B_HEADS, hv, hk), lambda i: (jnp.maximum(nc - 2 - i, 0), 0, 0, 0)),
                  pl.BlockSpec((CHUNK, 2 * kd), rev)],
        out_specs=(pl.BlockSpec((CHUNK, 6 * kd), rev), pl.BlockSpec((CHUNK, kd), rev),
                   pl.BlockSpec((SUBLANES, kd), lambda i: (0, 0))),
        scratch_shapes=[pltpu.VMEM((B_HEADS, hv, hk), F32)], compiler_params=_cp("arbitrary"),
    )(p2, lrp, wa2p, balpha, gnorm, states, states, dyb)


def _merge_fwd(ya, yb, wpa, wpb, g):
    s, ka = ya.shape
    kb = yb.shape[1]
    d = wpa.shape[1]
    tm, tn = _tile(s, (1024, 512)), _tile(d, (512,))

    def body(ya_ref, yb_ref, wa_ref, wb_ref, g_ref, m_ref, pab_ref):
        pa = _dot(ya_ref[...], wa_ref[...])
        pb = _dot(yb_ref[...], wb_ref[...])
        m_ref[...] = (_sigmoid(g_ref[0].astype(F32)) * pa + _sigmoid(g_ref[1].astype(F32)) * pb).astype(BF16)
        pab_ref[0] = pa.astype(BF16)
        pab_ref[1] = pb.astype(BF16)

    st = pl.BlockSpec((2, tm, tn), lambda i, j: (0, i, j))
    return pl.pallas_call(
        body, name="merge_fwd",
        out_shape=(jax.ShapeDtypeStruct((s, d), BF16), jax.ShapeDtypeStruct((2, s, d), BF16)),
        grid=(s // tm, d // tn),
        in_specs=[pl.BlockSpec((tm, ka), lambda i, j: (i, 0)), pl.BlockSpec((tm, kb), lambda i, j: (i, 0)),
                  pl.BlockSpec((ka, tn), lambda i, j: (0, j)), pl.BlockSpec((kb, tn), lambda i, j: (0, j)), st],
        out_specs=(pl.BlockSpec((tm, tn), lambda i, j: (i, j)), st),
        compiler_params=_cp("parallel", "parallel"),
    )(ya, yb, wpa, wpb, g)


def _merge_bwd(dm, wmo, g, pab, stages=None):
    s, d = dm.shape
    tm, tn = _tile(s, (1024, 512)), _tile(d, (512,))

    def body(dm_ref, w_ref, g_ref, pab_ref, dpab_ref, dg_ref):
        dmg = _dot(dm_ref[...], w_ref[...], "nt")
        for j in range(2):
            sg = _sigmoid(g_ref[j].astype(F32))
            dpab_ref[j] = (dmg * sg).astype(BF16)
            dg_ref[j] = (dmg * pab_ref[j].astype(F32) * (sg * (1.0 - sg))).astype(BF16)

    st = pl.BlockSpec((2, tm, tn), lambda i, j: (0, i, j))
    return _pcall(
        body, stages, name="merge_bwd",
        out_shape=(jax.ShapeDtypeStruct((2, s, d), BF16), jax.ShapeDtypeStruct((2, s, d), BF16)),
        grid=(s // tm, d // tn),
        in_specs=[pl.BlockSpec((tm, d), lambda i, j: (i, 0)), pl.BlockSpec((tn, d), lambda i, j: (j, 0)), st, st],
        out_specs=(st, st), compiler_params=_cp("arbitrary", "arbitrary"),
    )(dm, wmo, g, pab)


def _split_mix_in(w_mix_in):
    aw = A_HEADS * A_HEAD_DIM
    d = w_mix_in.shape[0]
    kd = d // 4
    o1 = 3 * aw
    o2 = o1 + 6 * kd
    o3 = o2 + GATE_RANK
    w_lr = jnp.pad(w_mix_in[:, o2:o3], ((0, 0), (0, LANES - GATE_RANK)))
    return w_mix_in[:, :o1], w_mix_in[:, o1:o2], w_lr, w_mix_in[:, o3:]


def _heads_major(t, s):
    return t.reshape(s, A_HEADS, A_HEAD_DIM).transpose(1, 0, 2)


def _mix_weights(plan):
    return plan.memo("mix_weights", lambda: _split_mix_in(plan.weight("w_mix_in")) + (
        plan.weight("w_proj_a"), plan.weight("w_proj_b"), plan.weight("w_mix_out")))


def _mix_forward(u2, plan, small):
    s, d = u2.shape
    w1, w2, w_lr, w_g, wpa, wpb, wmo = _mix_weights(plan)
    bias, wa2p, balpha, gnorm = small
    aw = A_HEADS * A_HEAD_DIM
    tm = _tile(s, (1024,))
    p1 = _mm("mix_in_a", "nn", u2, w1, (s, 3 * aw, d), tm=tm, tn=1024, tk=d, out_dtype=BF16)
    p2 = _mm("mix_in_b", "nn", u2, w2, (s, w2.shape[1], d), tm=tm, tn=1024, tk=d, out_dtype=BF16)
    lrp = _mm("mix_in_lr", "nn", u2, w_lr, (s, LANES, d), tm=tm, tn=LANES, tk=d, out_dtype=BF16)
    nbg = d // 1024
    g = _mm("mix_in_g", "nn", u2, w_g, (s, 2 * d, d), tm=tm, tn=1024, tk=d, out_dtype=BF16, out_shape=(2, s, d),
            o_spec=pl.BlockSpec((None, tm, 1024), lambda i, j, kk: (j // nbg, i, j % nbg)))
    q = _heads_major(p1[:, :aw], s)
    kp = jnp.pad(_heads_major(p1[:, aw:2 * aw], s), ((0, 0), (A_PAD, 0), (0, 0)))
    vp = jnp.pad(_heads_major(p1[:, 2 * aw:], s), ((0, 0), (A_PAD, 0), (0, 0)))
    ya = plan.host("attn_fwd", lambda st: _attn_fwd(q, kp, vp, bias, st)).transpose(1, 0, 2).reshape(s, aw)
    yb, states = plan.host("gla_fwd", lambda st: _gla_fwd(p2, lrp, wa2p, balpha, gnorm, st))
    merged, pab = _merge_fwd(ya, yb, wpa, wpb, g)
    m = _mm("mix_out", "nn", merged, wmo, (s, d, d), tm=tm, tn=1024, tk=d)
    return m, (q, kp, vp, p2, lrp, states, ya, yb, g, pab, merged)


def _mix_backward(dm, u2, saved, plan, small):
    s, d = u2.shape
    w1, w2, w_lr, w_g, wpa, wpb, wmo = _mix_weights(plan)
    bias, wa2p, balpha, gnorm = small
    q, kp, vp, p2, lrp, states, ya, yb, g, pab, merged = saved
    aw = A_HEADS * A_HEAD_DIM
    kd = d // 4
    tm = _tile(s, (1024,))
    tks = _tile(s, (1024,))

    plan.grad("w_mix_out", _mm("mix_dw_out", "tn", merged, dm, (d, d, s), tm=1024, tn=1024, tk=tks))
    dpab, dg = plan.host("merge_bwd", lambda st: _merge_bwd(dm, wmo, g, pab, st))
    sel = lambda j: pl.BlockSpec((None, tm, d), lambda i, jj, kk: (j, i, 0))
    dya = _mm("mix_dya", "nt", dpab, wpa, (s, aw, d), tm=tm, tn=1024, tk=d, out_dtype=BF16, a_spec=sel(0))
    dyb = _mm("mix_dyb", "nt", dpab, wpb, (s, 2 * kd, d), tm=tm, tn=1024, tk=d, out_dtype=BF16, a_spec=sel(1))
    selk = lambda j: pl.BlockSpec((None, tks, 1024), lambda i, jj, kk: (j, kk, jj))
    plan.grad("w_proj_a", _mm("mix_dwpa", "tn", ya, dpab, (aw, d, s), tm=1024, tn=1024, tk=tks, b_spec=selk(0)))
    plan.grad("w_proj_b", _mm("mix_dwpb", "tn", yb, dpab, (2 * kd, d, s), tm=1024, tn=1024, tk=tks, b_spec=selk(1)))

    do = _heads_major(dya, s)
    dq, dkp, dvp, dbias = plan.host("attn_bwd", lambda st: _attn_bwd(q, kp, vp, bias, do, st))
    unheads = lambda t: t.transpose(1, 0, 2).reshape(s, aw)
    dp1 = jnp.concatenate([unheads(dq), unheads(dkp[:, A_PAD:].astype(BF16)), unheads(dvp[:, A_PAD:].astype(BF16))],
                          axis=1)
    dp2, dz, gsm = plan.host("gla_bwd", lambda st: _gla_bwd(p2, lrp, wa2p, balpha, gnorm, states, dyb, st))
    dlrp = _mm("gla_dlr", "nt", dz, wa2p, (s, LANES, kd), tm=tm, tn=LANES, tk=kd, out_dtype=BF16)
    dwa2p = _mm("gla_dwa2", "tn", lrp, dz, (LANES, kd, s), tm=LANES, tn=kd, tk=tks)

    du = _mm("mix_du_a", "nt", dp1, w1, (s, d, 3 * aw), tm=tm, tn=1024, tk=1024)
    du = _mm("mix_du_b", "nt", dp2, w2, (s, d, 6 * kd), tm=tm, tn=1024, tk=1024, add=du)
    du = _mm("mix_du_lr", "nt", dlrp, w_lr, (s, d, LANES), tm=tm, tn=1024, tk=LANES, add=du)
    nkg = d // 1024
    du = _mm("mix_du_g", "nt", dg, w_g, (s, d, 2 * d), tm=tm, tn=1024, tk=1024, add=du,
             a_spec=pl.BlockSpec((None, tm, 1024), lambda i, j, kk: (kk // nkg, i, kk % nkg)))
    dw1 = _mm("mix_dw_a", "tn", u2, dp1, (d, 3 * aw, s), tm=1024, tn=1024, tk=tks)
    dw2 = _mm("mix_dw_b", "tn", u2, dp2, (d, 6 * kd, s), tm=1024, tn=1024, tk=tks)
    dwlr = _mm("mix_dw_lr", "tn", u2, dlrp, (d, LANES, s), tm=1024, tn=LANES, tk=tks)
    dwg = _mm("mix_dw_g", "tn", u2, dg, (d, 2 * d, s), tm=1024, tn=1024, tk=tks,
              b_spec=pl.BlockSpec((None, tks, 1024), lambda i, j, kk: (j // nkg, kk, j % nkg)))
    plan.grad("w_mix_in", jnp.concatenate([dw1, dw2, dwlr[:, :GATE_RANK], dwg], axis=1))
    return du, (dbias, dwa2p[:GATE_RANK], gsm)


def _device_step(x, target, mod, small, plan):
    s, d = x.shape
    row = lambda i: mod[i:i + 1]
    sh1, sc1, g1, sh2, sc2, g2, sh3, sc3, g3 = (row(i) for i in range(N_MOD))

    onehot = _rel_onehot()
    bias = _mm("rel_bias_expand", "nn", small["rel_bias"], onehot, (A_HEADS, CHUNK * A_BAND, REL_SIZE),
               tm=A_HEADS, tn=4608, tk=REL_SIZE, precision=HIGHEST).reshape(A_HEADS, CHUNK, A_BAND)
    bias = _band_bias(bias)
    wa2p = jnp.pad(small["w_alpha2"], ((0, LANES - GATE_RANK), (0, 0))).astype(BF16)
    mix_small = (bias, wa2p, small["b_alpha"], small["gla_norm_g"])

    u1 = _modulate("mod1", x, sh1, sc1)
    f1, sv1 = _ffn_forward("ffn1", u1, plan)
    h1, u2 = _resid_ln_fwd("ln1_fwd", x, f1, g1, small["ln1_g"], small["ln1_b"], sh2, sc2, 0.5)
    m, svm = _mix_forward(u2, plan, mix_small)
    h2, u3 = _resid_ln_fwd("ln2_fwd", h1, m, g2, small["ln2_g"], small["ln2_b"], sh3, sc3, 1.0)
    f2, sv2 = _ffn_forward("ffn2", u3, plan)

    dr3, df2, acc3 = _final_ln_loss_bwd("ln3_loss_bwd", h2, f2, target, g3, small["ln3_g"], small["ln3_b"], 0.5)
    du3 = _ffn_backward("ffn2", df2, u3, sv2, plan)
    dr2, dmx, acc2 = _resid_ln_bwd("ln2_bwd", du3, dr3, h1, m, sc3, g2, small["ln2_g"], small["ln2_b"], 1.0)
    du2, (dbias, dwa2, gsm) = _mix_backward(dmx, u2, svm, plan, mix_small)
    dr1, df1, acc1 = _resid_ln_bwd("ln1_bwd", du2, dr2, x, f1, sc2, g1, small["ln1_g"], small["ln1_b"], 0.5)
    du1 = _ffn_backward("ffn1", df1, u1, sv1, plan)
    grad_x, acc0 = _input_grad("input_grad", du1, dr1, x, sc1)

    drel = _mm("rel_bias_grad", "nt", _band_bias_grad(dbias).reshape(A_HEADS, CHUNK * A_BAND), onehot,
               (A_HEADS, REL_SIZE, CHUNK * A_BAND), tm=A_HEADS, tn=REL_SIZE, tk=4608, precision=HIGHEST)
    loss = jnp.sum(acc3[ROW_LOSS])
    dmod = jnp.stack([acc0[ROW_DSH], acc0[ROW_DSC], acc1[ROW_DGATE], acc1[ROW_DSH], acc1[ROW_DSC], acc2[ROW_DGATE],
                      acc2[ROW_DSH], acc2[ROW_DSC], acc3[ROW_DGATE]])
    kd = d // 4
    small_grads = dict(ln1_g=acc1[ROW_DLN_G], ln1_b=acc1[ROW_DLN_B], ln2_g=acc2[ROW_DLN_G], ln2_b=acc2[ROW_DLN_B],
                       ln3_g=acc3[ROW_DLN_G], ln3_b=acc3[ROW_DLN_B], b_alpha=gsm[GLA_ROW_DBALPHA],
                       gla_norm_g=gsm[GLA_ROW_DGNORM, :kd // B_HEADS * 2], rel_bias=drel, w_alpha2=dwa2)
    return loss, grad_x, small_grads, dmod


HBM_SPEC = pl.BlockSpec(memory_space=pl.ANY)


def _mesh_pos():
    return lax.axis_index("x"), lax.axis_index("y"), lax.axis_index("c")


def _other_chips(x, y):
    return [(1 - x, y), (x, 1 - y), (1 - x, 1 - y)]


def _remote(src, dst, send_sem, recv_sem, to):
    return pltpu.make_async_remote_copy(src_ref=src, dst_ref=dst, send_sem=send_sem, recv_sem=recv_sem,
                                        device_id=to, device_id_type=MESH)


def _allgather_rows(name, v):
    m_per, n = v.shape

    def body(x_ref, out_ref, send_sems, recv_sems, local_sem):
        x, y, c = _mesh_pos()
        me, sibling = (x, y, c), (x, y, 1 - c)
        chips = _other_chips(x, y)

        def rows(px, py, pc):
            return out_ref.at[pl.ds((4 * px + 2 * py + pc) * m_per, m_per), :]

        def copy(k, block, to, src=None):
            return _remote(rows(*block) if src is None else src, rows(*block), send_sems.at[k], recv_sems.at[k], to)

        mine = pltpu.make_async_copy(x_ref, rows(*me), local_sem)
        mine.start()
        first = [copy(0, me, sibling, src=x_ref)]
        first += [copy(1 + j, me, (*chip, c), src=x_ref) for j, chip in enumerate(chips)]
        for cp in first:
            cp.start()
        passed = [copy(4 + j, (*chip, c), sibling) for j, chip in enumerate(chips)]
        for j, chip in enumerate(chips):
            copy(1 + j, (*chip, c), me).wait_recv()
            passed[j].start()
        copy(0, sibling, me).wait_recv()
        for j, chip in enumerate(chips):
            copy(4 + j, (*chip, 1 - c), me).wait_recv()
        for cp in first + passed:
            cp.wait_send()
        mine.wait()

    return pl.pallas_call(
        body, name=name, out_shape=jax.ShapeDtypeStruct((N_DEV * m_per, n), v.dtype),
        in_specs=[pl.BlockSpec(memory_space=pltpu.VMEM)], out_specs=pl.BlockSpec(memory_space=pltpu.VMEM),
        scratch_shapes=[pltpu.SemaphoreType.DMA((7,)), pltpu.SemaphoreType.DMA((7,)), pltpu.SemaphoreType.DMA],
    )(v)


def _allgather_weights(shards):
    n = len(shards)

    def body(*refs):
        ins, outs = refs[:n], refs[n:2 * n]
        send_sems, recv_sems = refs[2 * n:]
        x, y, c = _mesh_pos()
        sibling = (x, y, 1 - c)
        chips = _other_chips(x, y)
        j0 = 2 * x + y

        def half(ref, w, j, hc):
            hr = shards[w].shape[0] // 2
            return ref.at[j, pl.ds(hc * hr, hr), :]

        sends = []
        for w in range(n):
            hr = shards[w].shape[0] // 2
            for r, chip in enumerate(chips):
                cp = _remote(ins[w].at[pl.ds(c * hr, hr), :], half(outs[w], w, j0, c), send_sems.at[w, r],
                             recv_sems.at[w, r], (*chip, c))
                cp.start()
                sends.append(cp)
        for w in range(n):
            for r, chip in enumerate(chips):
                jr = 2 * chip[0] + chip[1]
                landed = half(outs[w], w, jr, c)
                _remote(landed, landed, send_sems.at[w, r], recv_sems.at[w, r], (*chip, c)).wait_recv()
                fw = _remote(landed, landed, send_sems.at[w, 3 + r], recv_sems.at[w, 3 + r], sibling)
                fw.start()
                sends.append(fw)
        for w in range(n):
            for r, chip in enumerate(chips):
                jr = 2 * chip[0] + chip[1]
                got = half(outs[w], w, jr, 1 - c)
                _remote(got, got, send_sems.at[w, 3 + r], recv_sems.at[w, 3 + r], sibling).wait_recv()
        for cp in sends:
            cp.wait_send()

    return pl.pallas_call(
        body, name="allgather_weights",
        out_shape=[jax.ShapeDtypeStruct((N_CHIPS,) + sh.shape, sh.dtype) for sh in shards],
        in_specs=[HBM_SPEC] * n, out_specs=[HBM_SPEC] * n,
        scratch_shapes=[pltpu.SemaphoreType.DMA((n, 6)), pltpu.SemaphoreType.DMA((n, 6))],
    )(*shards)


def _stage_gather_ici(shards):
    n = len(shards)

    def copies(ins, outs, send, recv):
        x, y, c = _mesh_pos()
        j0 = 2 * x + y
        for w in range(n):
            hr = shards[w].shape[0] // 2
            for r, chip in enumerate(_other_chips(x, y)):
                jr = 2 * chip[0] + chip[1]
                mine = _remote(ins[w].at[pl.ds(c * hr, hr), :], outs[w].at[j0, pl.ds(c * hr, hr), :],
                               send.at[3 * w + r], recv.at[3 * w + r], (*chip, c))
                landed = outs[w].at[jr, pl.ds(c * hr, hr), :]
                yield mine, _remote(landed, landed, send.at[3 * w + r], recv.at[3 * w + r], (*chip, c))

    def start(*refs):
        for mine, _ in copies(*refs):
            mine.start()

    def finish(*refs):
        pairs = list(copies(*refs))
        for _, theirs in pairs:
            theirs.wait_recv()
        for mine, _ in pairs:
            mine.wait_send()

    outs = [jax.ShapeDtypeStruct((N_CHIPS,) + sh.shape, sh.dtype) for sh in shards]
    return _Stage(shards, outs, 3 * n, start, finish)


def _stage_gather_d2d(partial):
    n = len(partial)

    def copies(ins, outs, send, recv):
        x, y, c = _mesh_pos()
        for w in range(n):
            hr = partial[w].shape[1] // 2
            for r, chip in enumerate(_other_chips(x, y)):
                jr = 2 * chip[0] + chip[1]
                have = pl.ds(c * hr, hr)
                mine = _remote(ins[w].at[jr, have, :], outs[w].at[jr, have, :], send.at[3 * w + r], recv.at[3 * w + r],
                               (x, y, 1 - c))
                got = outs[w].at[jr, pl.ds((1 - c) * hr, hr), :]
                yield mine, _remote(got, got, send.at[3 * w + r], recv.at[3 * w + r], (x, y, 1 - c))

    def start(*refs):
        for mine, _ in copies(*refs):
            mine.start()

    def finish(*refs):
        pairs = list(copies(*refs))
        for _, theirs in pairs:
            theirs.wait_recv()
        for mine, _ in pairs:
            mine.wait_send()

    outs = [jax.ShapeDtypeStruct(p.shape, p.dtype) for p in partial]
    return _Stage(partial, outs, 3 * n, start, finish, aliases={w: w for w in range(n)})


def _stage_exchange_halves(grads):
    n = len(grads)

    def copies(ins, outs, send, recv):
        x, y, c = _mesh_pos()
        for w in range(n):
            hr = grads[w].shape[1] // 2
            yield _remote(ins[w].at[:, pl.ds((1 - c) * hr, hr), :], outs[w], send.at[w], recv.at[w], (x, y, 1 - c))

    def start(*refs):
        for cp in copies(*refs):
            cp.start()

    def finish(*refs):
        cps = list(copies(*refs))
        for cp in cps:
            cp.wait_recv()
        for cp in cps:
            cp.wait_send()

    outs = [jax.ShapeDtypeStruct((N_CHIPS, g.shape[1] // 2, g.shape[2]), g.dtype) for g in grads]
    return _Stage(grads, outs, n, start, finish)


def _stage_scatter(parts):
    n = len(parts)

    def copies(ins, outs, send, recv):
        x, y, c = _mesh_pos()
        for w in range(n):
            for r, chip in enumerate(_other_chips(x, y)):
                jr = 2 * chip[0] + chip[1]
                yield _remote(ins[w].at[jr], outs[w].at[r], send.at[3 * w + r], recv.at[3 * w + r], (*chip, c))

    def start(*refs):
        for cp in copies(*refs):
            cp.start()

    def finish(*refs):
        cps = list(copies(*refs))
        for cp in cps:
            cp.wait_recv()
        for cp in cps:
            cp.wait_send()

    outs = [jax.ShapeDtypeStruct((3,) + p.shape[1:], p.dtype) for p in parts]
    return _Stage(parts, outs, 3 * n, start, finish)


def _stage_share(fulls):
    n = len(fulls)

    def copies(ins, outs, send, recv):
        x, y, c = _mesh_pos()
        for w in range(n):
            hr = fulls[w].shape[0] // 2
            mine = pl.ds(c * hr, hr)
            theirs = outs[w].at[pl.ds((1 - c) * hr, hr), :]
            yield (_remote(ins[w].at[mine, :], outs[w].at[mine, :], send.at[w], recv.at[w], (x, y, 1 - c)),
                   _remote(theirs, theirs, send.at[w], recv.at[w], (x, y, 1 - c)))

    def start(*refs):
        for mine, _ in copies(*refs):
            mine.start()

    def finish(*refs):
        pairs = list(copies(*refs))
        for _, theirs in pairs:
            theirs.wait_recv()
        for mine, _ in pairs:
            mine.wait_send()

    outs = [jax.ShapeDtypeStruct(h.shape, h.dtype) for h in fulls]
    return _Stage(fulls, outs, n, start, finish, aliases={w: w for w in range(n)})


def _run_stages(name, stages):
    return _pcall(None, stages, name=name, out_shape=[], in_specs=[], out_specs=[])()[1]


TILE_BYTES = 2 * 1024 * 1024


def _row_tile(rows, cols, itemsize=4):
    for t in (1024, 512, 256, 128, 64, 32, 16, 8):
        if rows % t == 0 and t * cols * itemsize <= TILE_BYTES:
            return t
    return rows


def _pair_sum(name, g, recv, core):
    _, hr, cols = recv.shape
    tr = _row_tile(hr, cols)
    nb = hr // tr

    def body(c_ref, g_ref, r_ref, o_ref):
        o_ref[...] = (g_ref[...] + r_ref[...]).astype(BF16)

    grid_spec = pltpu.PrefetchScalarGridSpec(
        num_scalar_prefetch=1, grid=(N_CHIPS, nb),
        in_specs=[pl.BlockSpec((None, tr, cols), lambda j, i, cr: (j, cr[0] * nb + i, 0)),
                  pl.BlockSpec((None, tr, cols), lambda j, i, cr: (j, i, 0))],
        out_specs=pl.BlockSpec((None, tr, cols), lambda j, i, cr: (j, i, 0)))
    return pl.pallas_call(body, name=name, out_shape=jax.ShapeDtypeStruct(recv.shape, BF16), grid_spec=grid_spec,
                          compiler_params=_cp("parallel", "parallel"))(core, g, recv)


def _quad_sum(name, own, landed, chip_core):
    _, hr, cols = landed.shape
    tr = _row_tile(hr, cols)
    nb = hr // tr

    def body(cc_ref, own_ref, l_ref, o_ref):
        o_ref[...] = ((own_ref[...].astype(F32) + l_ref[0].astype(F32)) + l_ref[1].astype(F32)) + l_ref[2].astype(F32)

    grid_spec = pltpu.PrefetchScalarGridSpec(
        num_scalar_prefetch=1, grid=(nb,),
        in_specs=[pl.BlockSpec((None, tr, cols), lambda i, cc: (cc[0], i, 0)),
                  pl.BlockSpec((3, tr, cols), lambda i, cc: (0, i, 0))],
        out_specs=pl.BlockSpec((tr, cols), lambda i, cc: (cc[1] * nb + i, 0)))
    return pl.pallas_call(body, name=name, out_shape=jax.ShapeDtypeStruct((2 * hr, cols), F32), grid_spec=grid_spec,
                          compiler_params=_cp("arbitrary"))(chip_core, own, landed)


def _device_sum(name, gathered):
    def body(g_ref, o_ref):
        total = g_ref[0]
        for k in range(1, N_DEV):
            total = total + g_ref[k]
        o_ref[...] = total

    return pl.pallas_call(body, name=name, out_shape=jax.ShapeDtypeStruct(gathered.shape[1:], F32))(gathered)


def _adamw(name, w, g, m, v):
    rows, cols = w.shape
    tr = _row_tile(rows, cols)
    bc1 = 1.0 - ADAM_B1 ** ADAM_STEP
    bc2 = 1.0 - ADAM_B2 ** ADAM_STEP

    def body(w_ref, g_ref, m_ref, v_ref, d_ref, mo_ref, vo_ref):
        gv = g_ref[...]
        mn = ADAM_B1 * m_ref[...] + (1.0 - ADAM_B1) * gv
        vn = ADAM_B2 * v_ref[...] + (1.0 - ADAM_B2) * (gv * gv)
        mo_ref[...] = mn
        vo_ref[...] = vn
        d_ref[...] = -ADAM_LR * ((mn / bc1) / (jnp.sqrt(vn / bc2) + ADAM_EPS) + ADAM_WD * w_ref[...])

    spec = pl.BlockSpec((tr, cols), lambda i: (i, 0))
    return pl.pallas_call(
        body, name=name, out_shape=[jax.ShapeDtypeStruct((rows, cols), F32)] * 3, grid=(rows // tr,),
        in_specs=[spec] * 4, out_specs=[spec] * 3, compiler_params=_cp("parallel"),
    )(w, g, m, v)


WEIGHTS = ["w_ada", "b_ada", "ffn1_w_in", "ffn1_w_out", "ln1_g", "ln1_b", "w_mix_in", "rel_bias", "w_alpha2",
           "b_alpha", "gla_norm_g", "w_proj_a", "w_proj_b", "w_mix_out", "ln2_g", "ln2_b", "ffn2_w_in", "ffn2_w_out",
           "ln3_g", "ln3_b"]
BIG = {"ffn1_w_in": True, "ffn1_w_out": False, "w_mix_in": True, "w_proj_a": True, "w_proj_b": True,
       "w_mix_out": False, "ffn2_w_in": True, "ffn2_w_out": False}
STACKED = ("ffn1_w_in", "ffn2_w_in")
GROUP_FFN1 = ("ffn1_w_in", "ffn1_w_out")
GROUP_MIX = ("w_mix_in", "w_proj_a", "w_proj_b", "w_mix_out")
GROUP_FFN2 = ("ffn2_w_in", "ffn2_w_out")
SMALL = ["ln1_g", "ln1_b", "ln2_g", "ln2_b", "ln3_g", "ln3_b", "b_alpha", "gla_norm_g", "rel_bias", "w_alpha2"]


def _pad_rows(vec, rows=SUBLANES):
    per = -(-vec.shape[0] // (rows * LANES)) * LANES
    return jnp.pad(vec, (0, rows * per - vec.shape[0])).reshape(rows, per)


def _silu(v):
    return v * _sigmoid(v)


class _MeshPlan:
    def __init__(self, shards, chip, core):
        self.shards, self.chip = shards, chip
        self.core1 = core.astype(jnp.int32).reshape(1)
        self.chip_core = jnp.stack([chip, core]).astype(jnp.int32)
        self.partial, self.full, self.local, self.pair, self.half, self.final, self.memos = {}, {}, {}, {}, {}, {}, {}
        ici, d2d, x1, x2, x3 = self.gather_ici, self.gather_d2d, self.exchange, self.scatter, self.share
        self.schedule = {
            "ffn1_in_fwd": [ici(GROUP_MIX)], "ffn1_out_fwd": [d2d(GROUP_MIX)],
            "attn_fwd": [ici(GROUP_FFN2)], "gla_fwd": [d2d(GROUP_FFN2)],
            "merge_bwd": [x1(GROUP_FFN2)], "attn_bwd": [x2(GROUP_FFN2)], "gla_bwd": [x3(GROUP_FFN2)],
            "ffn1_out_bwd": [x1(GROUP_MIX)], "ffn1_dw_in": [x2(GROUP_MIX)],
            "ffn1_dw_out": [x3(GROUP_MIX), x1(GROUP_FFN1[:1])],
            "ffn1_du": [x2(GROUP_FFN1[:1]), x1(GROUP_FFN1[1:])],
        }

    def weight(self, k):
        return self.full[k]

    def grad(self, k, g):
        r, cc = self.shards[k].shape
        if k not in STACKED:
            g = g.reshape(r, N_CHIPS, cc).transpose(1, 0, 2) if BIG[k] else g.reshape(N_CHIPS, r, cc)
        self.local[k] = g

    def memo(self, key, make):
        if key not in self.memos:
            self.memos[key] = make()
        return self.memos[key]

    def host(self, name, call):
        builders = self.schedule.get(name)
        if not builders:
            return call(None)
        built = [b() for b in builders]
        main, comm = call([st for st, _ in built])
        for (_, post), res in zip(built, comm):
            post(res)
        return main

    def run(self, name, builders):
        built = [b() for b in builders]
        for (_, post), res in zip(built, _run_stages(name, [st for st, _ in built])):
            post(res)

    def set_gathered(self, names, gathered):
        for k, g in zip(names, gathered):
            _, r, cc = g.shape
            g = lax.dynamic_update_slice(g, self.shards[k][None], (self.chip, 0, 0))
            if k not in STACKED:
                g = g.transpose(1, 0, 2).reshape(r, N_CHIPS * cc) if BIG[k] else g.reshape(N_CHIPS * r, cc)
            self.full[k] = g

    def gather_ici(self, names):
        def post(res):
            self.partial.update(zip(names, res))
        return lambda: (_stage_gather_ici([self.shards[k] for k in names]), post)

    def gather_d2d(self, names):
        return lambda: (_stage_gather_d2d([self.partial[k] for k in names]), lambda res: self.set_gathered(names, res))

    def exchange(self, names):
        def post(res):
            for k, r in zip(names, res):
                self.pair[k] = _pair_sum(f"pair_sum_{k}", self.local[k], r, self.core1)
        return lambda: (_stage_exchange_halves([self.local[k] for k in names]), post)

    def scatter(self, names):
        def post(res):
            for k, landed in zip(names, res):
                self.half[k] = _quad_sum(f"quad_sum_{k}", self.pair[k], landed, self.chip_core)
        return lambda: (_stage_scatter([self.pair[k] for k in names]), post)

    def share(self, names):
        def post(res):
            self.final.update(zip(names, res))
        return lambda: (_stage_share([self.half[k] for k in names]), post)


def _step(args):
    x_pos, y_pos, c_pos = _mesh_pos()
    chip = 2 * x_pos + y_pos
    dev = 4 * x_pos + 2 * y_pos + c_pos
    w = {k: args[k][0] for k in WEIGHTS}
    mom = {k: args["m_" + k][0] for k in WEIGHTS}
    vel = {k: args["v_" + k][0] for k in WEIGHTS}
    x = args["x"][0]
    target = args["loss_target"][0]
    s, d = x.shape
    kd = d // 4
    rel_sh = w["rel_bias"].shape[1]
    wa2_sh = w["w_alpha2"].shape[1]
    ada_sh = w["w_ada"].shape[1]

    n_rel, n_wa2 = A_HEADS * rel_sh, GATE_RANK * wa2_sh
    packed = _pad_rows(jnp.concatenate([args["c"].reshape(-1), w["rel_bias"].reshape(-1), w["w_alpha2"].reshape(-1)]))
    got = _allgather_rows("gather_small_inputs", packed).reshape(N_DEV, -1)
    c_all = got[:, :d]
    per_chip = got[0::2]
    rel_bias = per_chip[:, d:d + n_rel].reshape(N_CHIPS, A_HEADS, rel_sh).transpose(1, 0, 2).reshape(A_HEADS, -1)
    w_alpha2 = per_chip[:, d + n_rel:d + n_rel + n_wa2].reshape(N_CHIPS, GATE_RANK, wa2_sh).transpose(1, 0, 2)
    w_alpha2 = w_alpha2.reshape(GATE_RANK, -1)

    b_shard = lax.dynamic_slice(w["b_ada"], (chip * ada_sh,), (ada_sh,))
    mod_shard = _mm("ada_fwd", "nn", c_all, w["w_ada"], (N_DEV, ada_sh, d), tm=N_DEV, tn=_tile(ada_sh, (512, 128)),
                    tk=d, precision=HIGHEST, a_fn=_silu, add=jnp.broadcast_to(b_shard[None], (N_DEV, ada_sh)))
    mod_all = _allgather_rows("gather_mod", mod_shard).reshape(N_DEV, N_DEV, ada_sh)[0::2]
    mod_all = mod_all.transpose(1, 0, 2).reshape(N_DEV, N_MOD * d)
    mod = lax.dynamic_index_in_dim(mod_all, dev, 0, keepdims=False).reshape(N_MOD, d)

    names = list(BIG)
    plan = _MeshPlan({k: w[k].astype(BF16) for k in names}, chip, c_pos)
    plan.set_gathered(GROUP_FFN1, _allgather_weights([plan.shards[k] for k in GROUP_FFN1]))

    small = dict(rel_bias=rel_bias, w_alpha2=w_alpha2, b_alpha=w["b_alpha"][None], gla_norm_g=w["gla_norm_g"][None])
    for k in ("ln1_g", "ln1_b", "ln2_g", "ln2_b", "ln3_g", "ln3_b"):
        small[k] = w[k][None]
    loss_local, grad_x, small_grads, dmod = _device_step(x, target, mod, small, plan)
    loss = lax.psum(loss_local, ("x", "y", "c"))
    plan.run("grad_tail_scatter", [plan.scatter(GROUP_FFN1[1:]), plan.share(GROUP_FFN1[:1])])
    plan.run("grad_tail_share", [plan.share(GROUP_FFN1[1:])])

    flat = jnp.concatenate([small_grads[k].reshape(-1) for k in SMALL] + [dmod.reshape(-1)])
    n_small = flat.shape[0] - N_MOD * d
    packed = _pad_rows(flat)
    all_small = _allgather_rows("gather_small_grads", packed).reshape(N_DEV, SUBLANES, -1)
    summed = _device_sum("small_grad_sum", all_small).reshape(-1)
    dmod_all = all_small.reshape(N_DEV, -1)[:, n_small:n_small + N_MOD * d]
    dmod_shard = lax.dynamic_slice(dmod_all, (0, chip * ada_sh), (N_DEV, ada_sh))
    grads = {"b_ada": summed[n_small:n_small + N_MOD * d]}
    off = 0
    for k in SMALL:
        size = small_grads[k].size
        grads[k] = summed[off:off + size].reshape(small_grads[k].shape)
        off += size
    grads["rel_bias"] = lax.dynamic_slice(grads["rel_bias"], (0, chip * rel_sh), (A_HEADS, rel_sh))
    grads["w_alpha2"] = lax.dynamic_slice(grads["w_alpha2"], (0, chip * wa2_sh), (GATE_RANK, wa2_sh))
    grads["w_ada"] = _mm("ada_bwd", "nn", jnp.pad(c_all.T, ((0, 0), (0, LANES - N_DEV))),
                         jnp.pad(dmod_shard, ((0, LANES - N_DEV), (0, 0))), (d, ada_sh, LANES), tm=_tile(d, (1024,)),
                         tn=_tile(ada_sh, (512, 128)), tk=LANES, precision=HIGHEST, a_fn=_silu)

    grads.update(plan.final)

    delta, new_m, new_v = {}, {}, {}
    for k in ["w_ada"] + names:
        delta[k], new_m[k], new_v[k] = _adamw(f"adamw_{k}", w[k], grads[k], mom[k], vel[k])
    tiny = ["b_ada"] + SMALL
    pack = lambda src: _pad_rows(jnp.concatenate([src[k].reshape(-1) for k in tiny]), rows=1).reshape(-1, LANES)
    outs = _adamw("adamw_small", pack(w), pack(grads), pack(mom), pack(vel))
    off = 0
    for k in tiny:
        size = w[k].size
        for dst, src in zip((delta, new_m, new_v), outs):
            dst[k] = src.reshape(-1)[off:off + size].reshape(w[k].shape)
        off += size

    lead = lambda t: t[None]
    return (loss, lead(grad_x), *[lead(grads[k]) for k in WEIGHTS], *[lead(delta[k]) for k in WEIGHTS],
            *[lead(new_m[k]) for k in WEIGHTS], *[lead(new_v[k]) for k in WEIGHTS])


def kernel(x, c, w_ada, b_ada, ffn1_w_in, ffn1_w_out, ln1_g, ln1_b, w_mix_in, rel_bias, w_alpha2, b_alpha, gla_norm_g, w_proj_a, w_proj_b, w_mix_out, ln2_g, ln2_b, ffn2_w_in, ffn2_w_out, ln3_g, ln3_b, loss_target, m_w_ada, m_b_ada, m_ffn1_w_in, m_ffn1_w_out, m_ln1_g, m_ln1_b, m_w_mix_in, m_rel_bias, m_w_alpha2, m_b_alpha, m_gla_norm_g, m_w_proj_a, m_w_proj_b, m_w_mix_out, m_ln2_g, m_ln2_b, m_ffn2_w_in, m_ffn2_w_out, m_ln3_g, m_ln3_b, v_w_ada, v_b_ada, v_ffn1_w_in, v_ffn1_w_out, v_ln1_g, v_ln1_b, v_w_mix_in, v_rel_bias, v_w_alpha2, v_b_alpha, v_gla_norm_g, v_w_proj_a, v_w_proj_b, v_w_mix_out, v_ln2_g, v_ln2_b, v_ffn2_w_in, v_ffn2_w_out, v_ln3_g, v_ln3_b):
    return _step(dict(locals()))
```

```python
import functools

import jax
import jax.numpy as jnp
from jax import lax
from jax.experimental import pallas as pl
from jax.experimental.pallas import tpu as pltpu

F32 = jnp.float32
BF16 = jnp.bfloat16
MESH = pl.DeviceIdType.MESH
HIGHEST = lax.Precision.HIGHEST

VMEM_LIMIT_BYTES = 56 * 1024 * 1024
LANES = 128
SUBLANES = 8

CHUNK = 64
A_HEADS = 16
A_HEAD_DIM = 64
A_PAST_CHUNKS = 8
A_BAND = (A_PAST_CHUNKS + 1) * CHUNK
A_PAD = A_PAST_CHUNKS * CHUNK
REL_CLIP = 256
REL_SIZE = REL_CLIP + CHUNK
B_HEADS = 4
GATE_RANK = 16
GATE_TAU = 16.0
N_MOD = 9
DEPTH = 1
ALPHA = (2.0 * DEPTH) ** 0.25
LN_EPS = 1e-5
RMS_EPS = 1e-6
ADAM_LR = 0.001
ADAM_B1 = 0.9
ADAM_B2 = 0.999
ADAM_EPS = 1e-08
ADAM_WD = 0.01
ADAM_STEP = 10
NEG_BIG = -1e30

N_CHIPS = 4
N_DEV = 8


def _cp(*sem):
    return pltpu.CompilerParams(dimension_semantics=sem, vmem_limit_bytes=VMEM_LIMIT_BYTES)


class _Stage:
    def __init__(self, arrays, out_shapes, n_sems, start, finish, aliases=None):
        self.arrays, self.out_shapes, self.n_sems = list(arrays), list(out_shapes), n_sems
        self.start, self.finish, self.aliases = start, finish, dict(aliases or {})


def _pcall(body, stages, *, name, out_shape, in_specs, out_specs, grid=(), scratch_shapes=(), compiler_params=None):
    single = not isinstance(out_shape, (list, tuple))
    outs = [out_shape] if single else list(out_shape)
    ospecs = [out_specs] if single else list(out_specs)
    in_specs, scratch_shapes = list(in_specs), list(scratch_shapes)
    n_in, n_out, n_sc = len(in_specs), len(outs), len(scratch_shapes)
    stages = list(stages or [])
    c_in = [a for st in stages for a in st.arrays]
    c_out = [o for st in stages for o in st.out_shapes]
    aliases = {}
    io, oo = n_in, n_out
    for st in stages:
        for a, b in st.aliases.items():
            aliases[io + a] = oo + b
        io += len(st.arrays)
        oo += len(st.out_shapes)

    def wrapped(*refs):
        ins = refs[:n_in]
        cins = refs[n_in:n_in + len(c_in)]
        base = n_in + len(c_in)
        mouts = refs[base:base + n_out]
        couts = refs[base + n_out:base + n_out + len(c_out)]
        base += n_out + len(c_out)
        scr = refs[base:base + n_sc]
        sems = refs[base + n_sc:]

        def each(phase):
            i = o = 0
            for k, st in enumerate(stages):
                fn = st.start if phase == 0 else st.finish
                fn(cins[i:i + len(st.arrays)], couts[o:o + len(st.out_shapes)], sems[2 * k], sems[2 * k + 1])
                i += len(st.arrays)
                o += len(st.out_shapes)

        if stages and grid:
            first = functools.reduce(jnp.logical_and, [pl.program_id(a) == 0 for a in range(len(grid))])
            last = functools.reduce(jnp.logical_and, [pl.program_id(a) == g - 1 for a, g in enumerate(grid)])
            pl.when(first)(lambda: each(0))
            if body is not None:
                body(*ins, *mouts, *scr)
            pl.when(last)(lambda: each(1))
        else:
            each(0)
            if body is not None:
                body(*ins, *mouts, *scr)
            each(1)

    sem_shapes = []
    for st in stages:
        sem_shapes += [pltpu.SemaphoreType.DMA((st.n_sems,)), pltpu.SemaphoreType.DMA((st.n_sems,))]
    kwargs = dict(grid=grid) if grid else {}
    if compiler_params is not None:
        kwargs["compiler_params"] = compiler_params

    def run(*operands):
        res = pl.pallas_call(
            wrapped, name=name, out_shape=outs + c_out, in_specs=in_specs + [HBM_SPEC] * len(c_in),
            out_specs=ospecs + [HBM_SPEC] * len(c_out), scratch_shapes=scratch_shapes + sem_shapes,
            input_output_aliases=aliases, **kwargs)(*operands, *c_in)
        main = res[0] if single else tuple(res[:n_out])
        if not stages:
            return main
        comm, o = [], n_out
        for st in stages:
            comm.append(list(res[o:o + len(st.out_shapes)]))
            o += len(st.out_shapes)
        return main, comm

    return run


def _tile(n, prefs):
    for t in prefs:
        if t <= n and n % t == 0:
            return t
    return n


_DIMS = {"nn": (((1,), (0,)), ((), ())), "nt": (((1,), (1,)), ((), ())), "tn": (((0,), (0,)), ((), ()))}


def _dot(a, b, mode="nn", precision=None):
    return lax.dot_general(a, b, _DIMS[mode], precision=precision, preferred_element_type=F32)


def _sigmoid(x):
    return 1.0 / (1.0 + jnp.exp(-x))


def _mm(name, mode, a, b, mnk, *, tm, tn, tk, out_dtype=F32, precision=None, a_spec=None, b_spec=None,
        out_shape=None, o_spec=None, add=None, a_fn=None, stages=None):
    m, n, k = mnk
    assert m % tm == 0 and n % tn == 0 and k % tk == 0, (name, mnk, tm, tn, tk)
    nk = k // tk
    if a_spec is None:
        a_spec = {"nn": pl.BlockSpec((tm, tk), lambda i, j, kk: (i, kk)),
                  "nt": pl.BlockSpec((tm, tk), lambda i, j, kk: (i, kk)),
                  "tn": pl.BlockSpec((tk, tm), lambda i, j, kk: (kk, i))}[mode]
    if b_spec is None:
        b_spec = {"nn": pl.BlockSpec((tk, tn), lambda i, j, kk: (kk, j)),
                  "nt": pl.BlockSpec((tn, tk), lambda i, j, kk: (j, kk)),
                  "tn": pl.BlockSpec((tk, tn), lambda i, j, kk: (kk, j))}[mode]
    if o_spec is None:
        o_spec = pl.BlockSpec((tm, tn), lambda i, j, kk: (i, j))
    if out_shape is None:
        out_shape = (m, n)
    has_add = add is not None

    def body(*refs):
        a_ref, b_ref = refs[0], refs[1]
        add_ref = refs[2] if has_add else None
        o_ref = refs[3] if has_add else refs[2]
        av = a_ref[...]
        if a_fn is not None:
            av = a_fn(av)
        part = _dot(av, b_ref[...], mode, precision)

        def finish(total):
            if has_add:
                total = total + add_ref[...]
            o_ref[...] = total.astype(out_dtype)

        if nk == 1:
            finish(part)
        else:
            acc_ref = refs[-1]
            kk = pl.program_id(2)

            @pl.when(kk == 0)
            def _():
                acc_ref[...] = part

            @pl.when(kk > 0)
            def _():
                acc_ref[...] += part

            @pl.when(kk == nk - 1)
            def _():
                finish(acc_ref[...])

    in_specs = [a_spec, b_spec]
    operands = [a, b]
    if has_add:
        in_specs.append(pl.BlockSpec((tm, tn), lambda i, j, kk: (i, j)))
        operands.append(add)
    return _pcall(
        body, stages, name=name, out_shape=jax.ShapeDtypeStruct(out_shape, out_dtype), grid=(m // tm, n // tn, nk),
        in_specs=in_specs, out_specs=o_spec,
        scratch_shapes=[pltpu.VMEM((tm, tn), F32)] if nk > 1 else [],
        compiler_params=_cp("arbitrary", "arbitrary", "arbitrary") if stages else _cp("parallel", "parallel", "arbitrary"),
    )(*operands)


def _row_spec(tr, d):
    return pl.BlockSpec((tr, d), lambda i: (i, 0))


def _vec_spec(d, rows=1):
    return pl.BlockSpec((rows, d), lambda i: (0, 0))


def _modulate(name, x, sh, sc):
    s, d = x.shape
    tr = _tile(s, (512, 256))

    def body(x_ref, sh_ref, sc_ref, o_ref):
        o_ref[...] = (x_ref[...] * (1.0 + sc_ref[...]) + sh_ref[...]).astype(BF16)

    return pl.pallas_call(
        body, name=name, out_shape=jax.ShapeDtypeStruct((s, d), BF16), grid=(s // tr,),
        in_specs=[_row_spec(tr, d), _vec_spec(d), _vec_spec(d)], out_specs=_row_spec(tr, d),
        compiler_params=_cp("parallel"),
    )(x, sh, sc)


def _ln_stats(r):
    mu = jnp.mean(r, axis=-1, keepdims=True)
    xc = r - mu
    var = jnp.mean(xc * xc, axis=-1, keepdims=True)
    rstd = lax.rsqrt(var + LN_EPS)
    return xc * rstd, rstd


def _resid_ln_fwd(name, x, f, gate, ln_g, ln_b, sh_n, sc_n, coef):
    s, d = x.shape
    tr = _tile(s, (256,))

    def body(x_ref, f_ref, gate_ref, g_ref, b_ref, sh_ref, sc_ref, h_ref, u_ref):
        r = ALPHA * x_ref[...] + (coef * gate_ref[...]) * f_ref[...]
        xhat, _ = _ln_stats(r)
        h = xhat * g_ref[...] + b_ref[...]
        h_ref[...] = h
        u_ref[...] = (h * (1.0 + sc_ref[...]) + sh_ref[...]).astype(BF16)

    return pl.pallas_call(
        body, name=name, out_shape=(jax.ShapeDtypeStruct((s, d), F32), jax.ShapeDtypeStruct((s, d), BF16)),
        grid=(s // tr,), in_specs=[_row_spec(tr, d), _row_spec(tr, d)] + [_vec_spec(d)] * 5,
        out_specs=(_row_spec(tr, d), _row_spec(tr, d)), compiler_params=_cp("parallel"),
    )(x, f, gate, ln_g, ln_b, sh_n, sc_n)


ROW_DSC, ROW_DSH, ROW_DLN_G, ROW_DLN_B, ROW_DGATE, ROW_LOSS = 0, 1, 2, 3, 4, 5


def _ln_bwd_core(dy, xhat, rstd, ln_g):
    dxhat = dy * ln_g
    m1 = jnp.mean(dxhat, axis=-1, keepdims=True)
    m2 = jnp.mean(dxhat * xhat, axis=-1, keepdims=True)
    return rstd * (dxhat - m1 - xhat * m2)


def _colsum(v):
    return jnp.sum(v, axis=0, keepdims=True)


def _final_ln_loss_bwd(name, x, f, target, gate, ln_g, ln_b, coef):
    s, d = x.shape
    tr = _tile(s, (256,))
    inv_d = 1.0 / d

    def body(x_ref, f_ref, t_ref, gate_ref, g_ref, b_ref, dr_ref, df_ref, acc_ref):
        @pl.when(pl.program_id(0) == 0)
        def _():
            acc_ref[...] = jnp.zeros_like(acc_ref)

        fv = f_ref[...]
        r = ALPHA * x_ref[...] + (coef * gate_ref[...]) * fv
        xhat, rstd = _ln_stats(r)
        h = xhat * g_ref[...] + b_ref[...]
        err = h - t_ref[...]
        dy = err * inv_d
        dr = _ln_bwd_core(dy, xhat, rstd, g_ref[...])
        dr_ref[...] = dr
        df_ref[...] = ((coef * gate_ref[...]) * dr).astype(BF16)
        acc_ref[ROW_DLN_G:ROW_DLN_G + 1, :] += _colsum(dy * xhat)
        acc_ref[ROW_DLN_B:ROW_DLN_B + 1, :] += _colsum(dy)
        acc_ref[ROW_DGATE:ROW_DGATE + 1, :] += _colsum((coef * dr) * fv)
        acc_ref[ROW_LOSS:ROW_LOSS + 1, :] += _colsum(err * err) * (0.5 * inv_d)

    return pl.pallas_call(
        body, name=name,
        out_shape=(jax.ShapeDtypeStruct((s, d), F32), jax.ShapeDtypeStruct((s, d), BF16),
                   jax.ShapeDtypeStruct((SUBLANES, d), F32)),
        grid=(s // tr,), in_specs=[_row_spec(tr, d)] * 3 + [_vec_spec(d)] * 3,
        out_specs=(_row_spec(tr, d), _row_spec(tr, d), _vec_spec(d, SUBLANES)),
        compiler_params=_cp("arbitrary"),
    )(x, f, target, gate, ln_g, ln_b)


def _resid_ln_bwd(name, du_n, dr_n, x, f, sc_n, gate, ln_g, ln_b, coef):
    s, d = x.shape
    tr = _tile(s, (256,))

    def body(du_ref, drn_ref, x_ref, f_ref, sc_ref, gate_ref, g_ref, b_ref, dr_ref, df_ref, acc_ref):
        @pl.when(pl.program_id(0) == 0)
        def _():
            acc_ref[...] = jnp.zeros_like(acc_ref)

        fv = f_ref[...]
        du = du_ref[...]
        r = ALPHA * x_ref[...] + (coef * gate_ref[...]) * fv
        xhat, rstd = _ln_stats(r)
        h = xhat * g_ref[...] + b_ref[...]
        dy = du * (1.0 + sc_ref[...]) + ALPHA * drn_ref[...]
        dr = _ln_bwd_core(dy, xhat, rstd, g_ref[...])
        dr_ref[...] = dr
        df_ref[...] = ((coef * gate_ref[...]) * dr).astype(BF16)
        acc_ref[ROW_DSC:ROW_DSC + 1, :] += _colsum(du * h)
        acc_ref[ROW_DSH:ROW_DSH + 1, :] += _colsum(du)
        acc_ref[ROW_DLN_G:ROW_DLN_G + 1, :] += _colsum(dy * xhat)
        acc_ref[ROW_DLN_B:ROW_DLN_B + 1, :] += _colsum(dy)
        acc_ref[ROW_DGATE:ROW_DGATE + 1, :] += _colsum((coef * dr) * fv)

    return pl.pallas_call(
        body, name=name,
        out_shape=(jax.ShapeDtypeStruct((s, d), F32), jax.ShapeDtypeStruct((s, d), BF16),
                   jax.ShapeDtypeStruct((SUBLANES, d), F32)),
        grid=(s // tr,), in_specs=[_row_spec(tr, d)] * 4 + [_vec_spec(d)] * 4,
        out_specs=(_row_spec(tr, d), _row_spec(tr, d), _vec_spec(d, SUBLANES)),
        compiler_params=_cp("arbitrary"),
    )(du_n, dr_n, x, f, sc_n, gate, ln_g, ln_b)


def _input_grad(name, du, dr, x, sc):
    s, d = x.shape
    tr = _tile(s, (256,))

    def body(du_ref, dr_ref, x_ref, sc_ref, gx_ref, acc_ref):
        @pl.when(pl.program_id(0) == 0)
        def _():
            acc_ref[...] = jnp.zeros_like(acc_ref)

        du = du_ref[...]
        gx_ref[...] = du * (1.0 + sc_ref[...]) + ALPHA * dr_ref[...]
        acc_ref[ROW_DSC:ROW_DSC + 1, :] += _colsum(du * x_ref[...])
        acc_ref[ROW_DSH:ROW_DSH + 1, :] += _colsum(du)

    return pl.pallas_call(
        body, name=name,
        out_shape=(jax.ShapeDtypeStruct((s, d), F32), jax.ShapeDtypeStruct((SUBLANES, d), F32)),
        grid=(s // tr,), in_specs=[_row_spec(tr, d)] * 3 + [_vec_spec(d)],
        out_specs=(_row_spec(tr, d), _vec_spec(d, SUBLANES)), compiler_params=_cp("arbitrary"),
    )(du, dr, x, sc)


def _ffn_in_fwd(name, u, w_in, stages=None):
    s, d = u.shape
    cs = w_in.shape[2]
    f = 2 * cs
    tm, tn = _tile(s, (2048, 1024, 512)), _tile(cs, (256, 128))
    nb = f // tn
    nbs = cs // tn

    def body(u_ref, wa_ref, wb_ref, ab_ref, act_ref):
        uv = u_ref[...]
        a = _dot(uv, wa_ref[...])
        b = _dot(uv, wb_ref[...])
        ab_ref[0] = a.astype(BF16)
        ab_ref[1] = b.astype(BF16)
        act_ref[...] = (a * _sigmoid(a) * b).astype(BF16)

    return _pcall(
        body, stages, name=name,
        out_shape=(jax.ShapeDtypeStruct((2, s, f), BF16), jax.ShapeDtypeStruct((s, f), BF16)),
        grid=(s // tm, nb),
        in_specs=[pl.BlockSpec((tm, d), lambda i, j: (i, 0)),
                  pl.BlockSpec((None, d, tn), lambda i, j: (j // nbs, 0, j % nbs)),
                  pl.BlockSpec((None, d, tn), lambda i, j: (2 + j // nbs, 0, j % nbs))],
        out_specs=(pl.BlockSpec((2, tm, tn), lambda i, j: (0, i, j)), pl.BlockSpec((tm, tn), lambda i, j: (i, j))),
        compiler_params=_cp("arbitrary", "arbitrary"),
    )(u, w_in, w_in)


def _ffn_out_bwd(name, df, w_out, ab, stages=None):
    s, d = df.shape
    f = w_out.shape[0]
    tm, tn = _tile(s, (1024, 512)), _tile(f, (512, 256, 128))

    def body(df_ref, w_ref, ab_ref, dab_ref):
        dact = _dot(df_ref[...], w_ref[...], "nt")
        a = ab_ref[0].astype(F32)
        b = ab_ref[1].astype(F32)
        sg = _sigmoid(a)
        dab_ref[0] = (dact * b * (sg * (1.0 + a * (1.0 - sg)))).astype(BF16)
        dab_ref[1] = (dact * (a * sg)).astype(BF16)

    return _pcall(
        body, stages, name=name, out_shape=jax.ShapeDtypeStruct((2, s, f), BF16), grid=(s // tm, f // tn),
        in_specs=[pl.BlockSpec((tm, d), lambda i, j: (i, 0)), pl.BlockSpec((tn, d), lambda i, j: (j, 0)),
                  pl.BlockSpec((2, tm, tn), lambda i, j: (0, i, j))],
        out_specs=pl.BlockSpec((2, tm, tn), lambda i, j: (0, i, j)),
        compiler_params=_cp("arbitrary", "arbitrary"),
    )(df, w_out, ab)


def _ffn_forward(tag, u, plan):
    w_in, w_out = plan.weight(f"{tag}_w_in"), plan.weight(f"{tag}_w_out")
    s, d = u.shape
    f = w_out.shape[0]
    ab, act = plan.host(f"{tag}_in_fwd", lambda st: _ffn_in_fwd(f"{tag}_in_fwd", u, w_in, st))
    out = plan.host(f"{tag}_out_fwd", lambda st: _mm(
        f"{tag}_out_fwd", "nn", act, w_out, (s, d, f), tm=_tile(s, (1024,)), tn=_tile(d, (1024,)),
        tk=_tile(f, (1408, 512, 128)), stages=st))
    return out, (ab, act)


def _ffn_backward(tag, df, u, saved, plan, in_first):
    w_in, w_out = plan.weight(f"{tag}_w_in"), plan.weight(f"{tag}_w_out")
    ab, act = saved
    s, d = u.shape
    f = w_out.shape[0]
    dab = plan.host(f"{tag}_out_bwd", lambda st: _ffn_out_bwd(f"{tag}_out_bwd", df, w_out, ab, st))
    cs = w_in.shape[2]
    tk = _tile(cs, (1408, 256, 128))
    nkh, nks = f // tk, cs // tk
    tmd = _tile(d, (1024,))
    tn = tk
    nbh, nbs = f // tn, cs // tn
    tks = _tile(s, (1024,))

    def dw_in():
        plan.grad(f"{tag}_w_in", plan.host(f"{tag}_dw_in", lambda st: _mm(
            f"{tag}_dw_in", "tn", u, dab, (d, 2 * f, s), tm=tmd, tn=tn, tk=tks,
            b_spec=pl.BlockSpec((None, tks, tn), lambda i, j, kk: (j // nbh, kk, j % nbh)), out_shape=(N_CHIPS, d, cs),
            o_spec=pl.BlockSpec((None, tmd, tn), lambda i, j, kk: (j // nbs, i, j % nbs)), stages=st)))

    def dw_out():
        plan.grad(f"{tag}_w_out", plan.host(f"{tag}_dw_out", lambda st: _mm(
            f"{tag}_dw_out", "tn", act, df, (f, d, s), tm=_tile(f, (1408, 512, 128)), tn=tmd, tk=tks, stages=st)))

    for step in ((dw_in, dw_out) if in_first else (dw_out, dw_in)):
        step()
    return plan.host(f"{tag}_du", lambda st: _mm(
        f"{tag}_du", "nt", dab, w_in, (s, d, 2 * f), tm=_tile(s, (1024,)), tn=tmd, tk=tk,
        a_spec=pl.BlockSpec((None, _tile(s, (1024,)), tk), lambda i, j, kk: (kk // nkh, i, kk % nkh)),
        b_spec=pl.BlockSpec((None, tmd, tk), lambda i, j, kk: (kk // nks, j, kk % nks)), stages=st))


ATTN_Q = 4 * CHUNK
ATTN_W = ATTN_Q + A_PAD


def _band_bias(bias):
    n = ATTN_Q // CHUNK
    rows = [jnp.pad(bias, ((0, 0), (0, 0), (i * CHUNK, (n - 1 - i) * CHUNK)), constant_values=NEG_BIG)
            for i in range(n)]
    return jnp.concatenate(rows, axis=1)


def _band_bias_grad(dband):
    n = ATTN_Q // CHUNK
    parts = [dband[:, i * CHUNK:(i + 1) * CHUNK, i * CHUNK:i * CHUNK + A_BAND] for i in range(n)]
    return functools.reduce(jnp.add, parts)


def _attn_probs(q, kw, bias, key0):
    sc = _dot(q, kw, "nt") * (A_HEAD_DIM ** -0.5) + bias
    ks = lax.broadcasted_iota(jnp.int32, sc.shape, 1)
    sc = jnp.where(key0 + ks >= 0, sc, NEG_BIG)
    p = jnp.exp(sc - jnp.max(sc, axis=-1, keepdims=True))
    return p / jnp.sum(p, axis=-1, keepdims=True)


def _attn_fwd(q, kp, vp, band, stages=None):
    h, s, dh = q.shape
    assert s % ATTN_Q == 0

    def body(q_ref, k_ref, v_ref, b_ref, o_ref):
        base = pl.multiple_of(pl.program_id(1) * ATTN_Q, ATTN_Q)
        p = _attn_probs(q_ref[...], k_ref[pl.ds(base, ATTN_W), :], b_ref[...], base - A_PAD)
        o_ref[...] = _dot(p.astype(BF16), v_ref[pl.ds(base, ATTN_W), :]).astype(BF16)

    return _pcall(
        body, stages, name="attn_fwd", out_shape=jax.ShapeDtypeStruct((h, s, dh), BF16), grid=(h, s // ATTN_Q),
        in_specs=[pl.BlockSpec((None, ATTN_Q, dh), lambda hh, i: (hh, i, 0)),
                  pl.BlockSpec((None, s + A_PAD, dh), lambda hh, i: (hh, 0, 0)),
                  pl.BlockSpec((None, s + A_PAD, dh), lambda hh, i: (hh, 0, 0)),
                  pl.BlockSpec((None, ATTN_Q, ATTN_W), lambda hh, i: (hh, 0, 0))],
        out_specs=pl.BlockSpec((None, ATTN_Q, dh), lambda hh, i: (hh, i, 0)),
        compiler_params=_cp("arbitrary", "arbitrary"),
    )(q, kp, vp, band)


def _attn_bwd(q, kp, vp, band, do, stages=None):
    h, s, dh = q.shape
    scale = A_HEAD_DIM ** -0.5

    def body(q_ref, k_ref, v_ref, b_ref, do_ref, dq_ref, dk_ref, dv_ref, db_ref):
        @pl.when(pl.program_id(1) == 0)
        def _():
            dk_ref[...] = jnp.zeros_like(dk_ref)
            dv_ref[...] = jnp.zeros_like(dv_ref)
            db_ref[...] = jnp.zeros_like(db_ref)

        base = pl.multiple_of(pl.program_id(1) * ATTN_Q, ATTN_Q)
        window = pl.ds(base, ATTN_W)
        kw = k_ref[window, :]
        qv = q_ref[...]
        dov = do_ref[...]
        p = _attn_probs(qv, kw, b_ref[...], base - A_PAD)
        dp = _dot(dov, v_ref[window, :], "nt")
        ds = p * (dp - jnp.sum(p * dp, axis=-1, keepdims=True))
        db_ref[...] += ds
        dsb = (ds * scale).astype(BF16)
        dq_ref[...] = _dot(dsb, kw).astype(BF16)
        dk_ref[window, :] += _dot(dsb, qv, "tn")
        dv_ref[window, :] += _dot(p.astype(BF16), dov, "tn")

    kv_spec = pl.BlockSpec((None, s + A_PAD, dh), lambda hh, i: (hh, 0, 0))
    q_spec = pl.BlockSpec((None, ATTN_Q, dh), lambda hh, i: (hh, i, 0))
    b_spec = pl.BlockSpec((None, ATTN_Q, ATTN_W), lambda hh, i: (hh, 0, 0))
    return _pcall(
        body, stages, name="attn_bwd",
        out_shape=(jax.ShapeDtypeStruct((h, s, dh), BF16), jax.ShapeDtypeStruct((h, s + A_PAD, dh), F32),
                   jax.ShapeDtypeStruct((h, s + A_PAD, dh), F32), jax.ShapeDtypeStruct((h, ATTN_Q, ATTN_W), F32)),
        grid=(h, s // ATTN_Q), in_specs=[q_spec, kv_spec, kv_spec, b_spec, q_spec],
        out_specs=(q_spec, kv_spec, kv_spec, b_spec), compiler_params=_cp("arbitrary", "arbitrary"),
    )(q, kp, vp, band, do)


def _rel_onehot():
    qi = jnp.arange(CHUNK)[:, None]
    ks = jnp.arange(A_BAND)[None, :]
    idx = (jnp.clip(ks - A_PAD - qi, -REL_CLIP, CHUNK - 1) + REL_CLIP).reshape(1, CHUNK * A_BAND)
    return (jnp.arange(REL_SIZE)[:, None] == idx).astype(F32)


def _gla_gate(lr, wa2, balpha):
    z = _dot(lr, wa2) + balpha
    la = (jnp.minimum(z, 0.0) - jnp.log(1.0 + jnp.exp(-jnp.abs(z)))) * (1.0 / GATE_TAU)
    row = lax.broadcasted_iota(jnp.int32, (CHUNK, CHUNK), 0)
    col = lax.broadcasted_iota(jnp.int32, (CHUNK, CHUNK), 1)
    cum = _dot((row >= col).astype(F32), la, precision=HIGHEST)
    return z, la, cum


def _gla_dims(p2):
    kd = p2.shape[1] // 6
    hk = kd // B_HEADS
    hv = 2 * hk
    return kd, hk, hv


def _gla_fwd(p2, lrp, wa2p, balpha, gnorm, stages=None):
    s = p2.shape[0]
    kd, hk, hv = _gla_dims(p2)
    nc = s // CHUNK
    qscale = hk ** -0.5

    def body(p_ref, lr_ref, wa_ref, ba_ref, gn_ref, yb_ref, st_ref, state):
        @pl.when(pl.program_id(0) == 0)
        def _():
            state[...] = jnp.zeros_like(state)

        _, _, cum = _gla_gate(lr_ref[...], wa_ref[...], ba_ref[...])
        last = cum[CHUNK - 1:CHUNK, :]
        e = jnp.exp(last - cum)
        dch = jnp.exp(last)
        gn = gn_ref[...]
        for hh in range(B_HEADS):
            ks = slice(hh * hk, (hh + 1) * hk)
            q = p_ref[:, hh * hk:(hh + 1) * hk].astype(F32)
            k = p_ref[:, kd + hh * hk:kd + (hh + 1) * hk].astype(F32)
            v = p_ref[:, 2 * kd + hh * hv:2 * kd + (hh + 1) * hv]
            rg = p_ref[:, 4 * kd + hh * hv:4 * kd + (hh + 1) * hv].astype(F32)
            kdec = (k * e[:, ks]).astype(BF16)
            st = state[hh] * dch[:, ks] + _dot(v, kdec, "tn")
            state[hh] = st
            st_ref[hh] = st
            o = _dot((q * qscale).astype(BF16), st.astype(BF16), "nt")
            rinv = lax.rsqrt(jnp.mean(o * o, axis=-1, keepdims=True) + RMS_EPS)
            yb_ref[:, hh * hv:(hh + 1) * hv] = ((o * rinv * gn) * (rg * _sigmoid(rg))).astype(BF16)

    return _pcall(
        body, stages, name="gla_fwd",
        out_shape=(jax.ShapeDtypeStruct((s, 2 * kd), BF16), jax.ShapeDtypeStruct((nc, B_HEADS, hv, hk), F32)),
        grid=(nc,),
        in_specs=[pl.BlockSpec((CHUNK, 6 * kd), lambda i: (i, 0)), pl.BlockSpec((CHUNK, LANES), lambda i: (i, 0)),
                  pl.BlockSpec((LANES, kd), lambda i: (0, 0)), pl.BlockSpec((1, kd), lambda i: (0, 0)),
                  pl.BlockSpec((1, hv), lambda i: (0, 0))],
        out_specs=(pl.BlockSpec((CHUNK, 2 * kd), lambda i: (i, 0)),
                   pl.BlockSpec((None, B_HEADS, hv, hk), lambda i: (i, 0, 0, 0))),
        scratch_shapes=[pltpu.VMEM((B_HEADS, hv, hk), F32)], compiler_params=_cp("arbitrary"),
    )(p2, lrp, wa2p, balpha, gnorm)


GLA_ROW_DBALPHA, GLA_ROW_DGNORM = 0, 1


def _gla_bwd(p2, lrp, wa2p, balpha, gnorm, states, dyb, stages=None):
    s = p2.shape[0]
    kd, hk, hv = _gla_dims(p2)
    nc = s // CHUNK
    qscale = hk ** -0.5

    def body(p_ref, lr_ref, wa_ref, ba_ref, gn_ref, st_ref, sp_ref, dy_ref, dp_ref, dz_ref, sm_ref, gcar):
        i = pl.program_id(0)

        @pl.when(i == 0)
        def _():
            gcar[...] = jnp.zeros_like(gcar)
            sm_ref[...] = jnp.zeros_like(sm_ref)

        has_prev = (i < nc - 1).astype(F32)
        z, _, cum = _gla_gate(lr_ref[...], wa_ref[...], ba_ref[...])
        last = cum[CHUNK - 1:CHUNK, :]
        e = jnp.exp(last - cum)
        dch = jnp.exp(last)
        sgn = _sigmoid(-z) * (1.0 / GATE_TAU)
        gn = gn_ref[...]
        row = lax.broadcasted_iota(jnp.int32, (CHUNK, CHUNK), 0)
        col = lax.broadcasted_iota(jnp.int32, (CHUNK, CHUNK), 1)
        tri_strict = (row > col).astype(F32)
        for hh in range(B_HEADS):
            ks = slice(hh * hk, (hh + 1) * hk)
            q = p_ref[:, hh * hk:(hh + 1) * hk].astype(F32)
            k = p_ref[:, kd + hh * hk:kd + (hh + 1) * hk].astype(F32)
            v = p_ref[:, 2 * kd + hh * hv:2 * kd + (hh + 1) * hv]
            rg = p_ref[:, 4 * kd + hh * hv:4 * kd + (hh + 1) * hv].astype(F32)
            kdecf = k * e[:, ks]
            kdec = kdecf.astype(BF16)
            st16 = st_ref[hh].astype(BF16)
            qs = (q * qscale).astype(BF16)
            o = _dot(qs, st16, "nt")
            rinv = lax.rsqrt(jnp.mean(o * o, axis=-1, keepdims=True) + RMS_EPS)
            dy = dy_ref[:, hh * hv:(hh + 1) * hv].astype(F32)
            sg = _sigmoid(rg)
            onorm = o * rinv
            drg = dy * (onorm * gn) * (sg * (1.0 + rg * (1.0 - sg)))
            dob = dy * (rg * sg)
            sm_ref[GLA_ROW_DGNORM:GLA_ROW_DGNORM + 1, 0:hv] += _colsum(dob * onorm)
            t = dob * gn
            do = rinv * (t - onorm * jnp.mean(t * onorm, axis=-1, keepdims=True))
            do16 = do.astype(BF16)
            dq = _dot(do16, st16) * qscale
            gt = _dot(do16, qs, "tn") + gcar[hh]
            gcar[hh] = gt * dch[:, ks]
            dd = _colsum(gt * sp_ref[hh]) * has_prev
            gt16 = gt.astype(BF16)
            dkdec = _dot(v, gt16)
            dv = _dot(kdec, gt16, "nt")
            dla = dd * dch[:, ks] + _dot(tri_strict, dkdec * kdecf, precision=HIGHEST)
            dzh = dla * sgn[:, ks]
            sm_ref[GLA_ROW_DBALPHA:GLA_ROW_DBALPHA + 1, hh * hk:(hh + 1) * hk] += _colsum(dzh)
            dz_ref[:, hh * hk:(hh + 1) * hk] = dzh.astype(BF16)
            dp_ref[:, hh * hk:(hh + 1) * hk] = dq.astype(BF16)
            dp_ref[:, kd + hh * hk:kd + (hh + 1) * hk] = (dkdec * e[:, ks]).astype(BF16)
            dp_ref[:, 2 * kd + hh * hv:2 * kd + (hh + 1) * hv] = dv.astype(BF16)
            dp_ref[:, 4 * kd + hh * hv:4 * kd + (hh + 1) * hv] = drg.astype(BF16)

    rev = lambda i: (nc - 1 - i, 0)
    return _pcall(
        body, stages, name="gla_bwd",
        out_shape=(jax.ShapeDtypeStruct((s, 6 * kd), BF16), jax.ShapeDtypeStruct((s, kd), BF16),
                   jax.ShapeDtypeStruct((SUBLANES, kd), F32)),
        grid=(nc,),
        in_specs=[pl.BlockSpec((CHUNK, 6 * kd), rev), pl.BlockSpec((CHUNK, LANES), rev),
                  pl.BlockSpec((LANES, kd), lambda i: (0, 0)), pl.BlockSpec((1, kd), lambda i: (0, 0)),
                  pl.BlockSpec((1, hv), lambda i: (0, 0)),
                  pl.BlockSpec((None, B_HEADS, hv, hk), lambda i: (nc - 1 - i, 0, 0, 0)),
                  pl.BlockSpec((None, B_HEADS, hv, hk), lambda i: (jnp.maximum(nc - 2 - i, 0), 0, 0, 0)),
                  pl.BlockSpec((CHUNK, 2 * kd), rev)],
        out_specs=(pl.BlockSpec((CHUNK, 6 * kd), rev), pl.BlockSpec((CHUNK, kd), rev),
                   pl.BlockSpec((SUBLANES, kd), lambda i: (0, 0))),
        scratch_shapes=[pltpu.VMEM((B_HEADS, hv, hk), F32)], compiler_params=_cp("arbitrary"),
    )(p2, lrp, wa2p, balpha, gnorm, states, states, dyb)


def _merge_fwd(ya, yb, wpa, wpb, g):
    s, ka = ya.shape
    kb = yb.shape[1]
    d = wpa.shape[1]
    tm, tn = _tile(s, (1024, 512)), _tile(d, (512,))

    def body(ya_ref, yb_ref, wa_ref, wb_ref, g_ref, m_ref, pab_ref):
        pa = _dot(ya_ref[...], wa_ref[...])
        pb = _dot(yb_ref[...], wb_ref[...])
        m_ref[...] = (_sigmoid(g_ref[0].astype(F32)) * pa + _sigmoid(g_ref[1].astype(F32)) * pb).astype(BF16)
        pab_ref[0] = pa.astype(BF16)
        pab_ref[1] = pb.astype(BF16)

    st = pl.BlockSpec((2, tm, tn), lambda i, j: (0, i, j))
    return pl.pallas_call(
        body, name="merge_fwd",
        out_shape=(jax.ShapeDtypeStruct((s, d), BF16), jax.ShapeDtypeStruct((2, s, d), BF16)),
        grid=(s // tm, d // tn),
        in_specs=[pl.BlockSpec((tm, ka), lambda i, j: (i, 0)), pl.BlockSpec((tm, kb), lambda i, j: (i, 0)),
                  pl.BlockSpec((ka, tn), lambda i, j: (0, j)), pl.BlockSpec((kb, tn), lambda i, j: (0, j)), st],
        out_specs=(pl.BlockSpec((tm, tn), lambda i, j: (i, j)), st),
        compiler_params=_cp("parallel", "parallel"),
    )(ya, yb, wpa, wpb, g)


def _merge_bwd(dm, wmo, g, pab, stages=None):
    s, d = dm.shape
    tm, tn = _tile(s, (1024, 512)), _tile(d, (512,))

    def body(dm_ref, w_ref, g_ref, pab_ref, dpab_ref, dg_ref):
        dmg = _dot(dm_ref[...], w_ref[...], "nt")
        for j in range(2):
            sg = _sigmoid(g_ref[j].astype(F32))
            dpab_ref[j] = (dmg * sg).astype(BF16)
            dg_ref[j] = (dmg * pab_ref[j].astype(F32) * (sg * (1.0 - sg))).astype(BF16)

    st = pl.BlockSpec((2, tm, tn), lambda i, j: (0, i, j))
    return _pcall(
        body, stages, name="merge_bwd",
        out_shape=(jax.ShapeDtypeStruct((2, s, d), BF16), jax.ShapeDtypeStruct((2, s, d), BF16)),
        grid=(s // tm, d // tn),
        in_specs=[pl.BlockSpec((tm, d), lambda i, j: (i, 0)), pl.BlockSpec((tn, d), lambda i, j: (j, 0)), st, st],
        out_specs=(st, st), compiler_params=_cp("arbitrary", "arbitrary"),
    )(dm, wmo, g, pab)


def _split_mix_in(wt):
    aw = A_HEADS * A_HEAD_DIM
    d = wt.shape[1]
    o2 = 3 * aw + 6 * (d // 4)
    o3 = o2 + GATE_RANK
    return jnp.pad(wt[o2:o3], ((0, LANES - GATE_RANK), (0, 0))), wt[o3:]


def _heads_major(t, s):
    return t.reshape(s, A_HEADS, A_HEAD_DIM).transpose(1, 0, 2)


MIX_TILE = 1024


def _mix_in_weights(plan):
    wt = plan.weight("w_mix_in")
    return (wt,) + plan.memo("mix_in_weights", lambda: _split_mix_in(wt))


def _hosted_mm(plan):
    return lambda name, *a, **k: plan.host(name, lambda st: _mm(name, *a, stages=st, **k))


def _mix_forward(u2, plan, small):
    s, d = u2.shape
    wt, wt_lr, wt_g = _mix_in_weights(plan)
    bias, wa2p, balpha, gnorm = small
    mm = _hosted_mm(plan)
    aw = A_HEADS * A_HEAD_DIM
    nb = 6 * (d // 4)
    tm, tn = _tile(s, (1024,)), MIX_TILE
    assert (3 * aw) % tn == 0
    ob = 3 * aw // tn
    p1 = mm("mix_in_a", "nt", u2, wt, (s, 3 * aw, d), tm=tm, tn=tn, tk=d, out_dtype=BF16)
    p2 = mm("mix_in_b", "nt", u2, wt, (s, nb, d), tm=tm, tn=tn, tk=d, out_dtype=BF16,
            b_spec=pl.BlockSpec((tn, d), lambda i, j, kk: (ob + j, 0)))
    lrp = mm("mix_in_lr", "nt", u2, wt_lr, (s, LANES, d), tm=tm, tn=LANES, tk=d, out_dtype=BF16)
    nbg = d // tn
    g = mm("mix_in_g", "nt", u2, wt_g, (s, 2 * d, d), tm=tm, tn=tn, tk=d, out_dtype=BF16, out_shape=(2, s, d),
           o_spec=pl.BlockSpec((None, tm, tn), lambda i, j, kk: (j // nbg, i, j % nbg)))
    q = _heads_major(p1[:, :aw], s)
    kp = jnp.pad(_heads_major(p1[:, aw:2 * aw], s), ((0, 0), (A_PAD, 0), (0, 0)))
    vp = jnp.pad(_heads_major(p1[:, 2 * aw:], s), ((0, 0), (A_PAD, 0), (0, 0)))
    ya = plan.host("attn_fwd", lambda st: _attn_fwd(q, kp, vp, bias, st)).transpose(1, 0, 2).reshape(s, aw)
    yb, states = plan.host("gla_fwd", lambda st: _gla_fwd(p2, lrp, wa2p, balpha, gnorm, st))
    merged, pab = _merge_fwd(ya, yb, plan.weight("w_proj_a"), plan.weight("w_proj_b"), g)
    m = mm("mix_out", "nn", merged, plan.weight("w_mix_out"), (s, d, d), tm=tm, tn=tn, tk=d)
    return m, (q, kp, vp, p2, lrp, states, ya, yb, g, pab, merged)


def _mix_backward(dm, u2, saved, plan, small):
    s, d = u2.shape
    wt, wt_lr, wt_g = _mix_in_weights(plan)
    wpa, wpb, wmo = plan.weight("w_proj_a"), plan.weight("w_proj_b"), plan.weight("w_mix_out")
    bias, wa2p, balpha, gnorm = small
    q, kp, vp, p2, lrp, states, ya, yb, g, pab, merged = saved
    mm = _hosted_mm(plan)
    aw = A_HEADS * A_HEAD_DIM
    kd = d // 4
    t = MIX_TILE
    tm = _tile(s, (1024,))
    tks = _tile(s, (1024,))
    ob = 3 * aw // t

    plan.grad("w_mix_out", mm("mix_dw_out", "tn", merged, dm, (d, d, s), tm=t, tn=t, tk=tks))
    dpab, dg = plan.host("merge_bwd", lambda st: _merge_bwd(dm, wmo, g, pab, st))
    sel = lambda j: pl.BlockSpec((None, tm, d), lambda i, jj, kk: (j, i, 0))
    dya = mm("mix_dya", "nt", dpab, wpa, (s, aw, d), tm=tm, tn=t, tk=d, out_dtype=BF16, a_spec=sel(0))
    dyb = mm("mix_dyb", "nt", dpab, wpb, (s, 2 * kd, d), tm=tm, tn=t, tk=d, out_dtype=BF16, a_spec=sel(1))
    selk = lambda j: pl.BlockSpec((None, tks, t), lambda i, jj, kk: (j, kk, jj))
    plan.grad("w_proj_a", mm("mix_dwpa", "tn", ya, dpab, (aw, d, s), tm=t, tn=t, tk=tks, b_spec=selk(0)))
    plan.grad("w_proj_b", mm("mix_dwpb", "tn", yb, dpab, (2 * kd, d, s), tm=t, tn=t, tk=tks, b_spec=selk(1)))

    do = _heads_major(dya, s)
    dq, dkp, dvp, dbias = plan.host("attn_bwd", lambda st: _attn_bwd(q, kp, vp, bias, do, st))
    unheads = lambda a: a.transpose(1, 0, 2).reshape(s, aw)
    dp1 = jnp.concatenate([unheads(dq), unheads(dkp[:, A_PAD:].astype(BF16)), unheads(dvp[:, A_PAD:].astype(BF16))],
                          axis=1)
    dp2, dz, gsm = plan.host("gla_bwd", lambda st: _gla_bwd(p2, lrp, wa2p, balpha, gnorm, states, dyb, st))
    dlrp = mm("gla_dlr", "nt", dz, wa2p, (s, LANES, kd), tm=tm, tn=LANES, tk=kd, out_dtype=BF16)
    dwa2p = mm("gla_dwa2", "tn", lrp, dz, (LANES, kd, s), tm=LANES, tn=kd, tk=tks)

    du = mm("mix_du_a", "nn", dp1, wt, (s, d, 3 * aw), tm=tm, tn=t, tk=t)
    du = mm("mix_du_b", "nn", dp2, wt, (s, d, 6 * kd), tm=tm, tn=t, tk=t, add=du,
            b_spec=pl.BlockSpec((t, t), lambda i, j, kk: (ob + kk, j)))
    du = mm("mix_du_lr", "nn", dlrp, wt_lr, (s, d, LANES), tm=tm, tn=t, tk=LANES, add=du)
    nkg = d // t
    du = mm("mix_du_g", "nn", dg, wt_g, (s, d, 2 * d), tm=tm, tn=t, tk=t, add=du,
            a_spec=pl.BlockSpec((None, tm, t), lambda i, j, kk: (kk // nkg, i, kk % nkg)))
    dw1 = mm("mix_dw_a", "tn", dp1, u2, (3 * aw, d, s), tm=t, tn=t, tk=tks)
    dw2 = mm("mix_dw_b", "tn", dp2, u2, (6 * kd, d, s), tm=t, tn=t, tk=tks)
    dwlr = mm("mix_dw_lr", "tn", dlrp, u2, (LANES, d, s), tm=LANES, tn=t, tk=tks)
    dwg = mm("mix_dw_g", "tn", dg, u2, (2 * d, d, s), tm=t, tn=t, tk=tks,
             a_spec=pl.BlockSpec((None, tks, t), lambda i, j, kk: (i // nkg, kk, i % nkg)))
    plan.grad("w_mix_in", jnp.concatenate([dw1, dw2, dwlr[:GATE_RANK], dwg], axis=0))
    return du, (dbias, dwa2p[:GATE_RANK], gsm)


def _device_step(x, target, mod, small, plan):
    s, d = x.shape
    row = lambda i: mod[i:i + 1]
    sh1, sc1, g1, sh2, sc2, g2, sh3, sc3, g3 = (row(i) for i in range(N_MOD))

    onehot = _rel_onehot()
    bias = _mm("rel_bias_expand", "nn", small["rel_bias"], onehot, (A_HEADS, CHUNK * A_BAND, REL_SIZE),
               tm=A_HEADS, tn=4608, tk=REL_SIZE, precision=HIGHEST).reshape(A_HEADS, CHUNK, A_BAND)
    bias = _band_bias(bias)
    wa2p = jnp.pad(small["w_alpha2"], ((0, LANES - GATE_RANK), (0, 0))).astype(BF16)
    mix_small = (bias, wa2p, small["b_alpha"], small["gla_norm_g"])

    u1 = _modulate("mod1", x, sh1, sc1)
    f1, sv1 = _ffn_forward("ffn1", u1, plan)
    h1, u2 = _resid_ln_fwd("ln1_fwd", x, f1, g1, small["ln1_g"], small["ln1_b"], sh2, sc2, 0.5)
    m, svm = _mix_forward(u2, plan, mix_small)
    h2, u3 = _resid_ln_fwd("ln2_fwd", h1, m, g2, small["ln2_g"], small["ln2_b"], sh3, sc3, 1.0)
    f2, sv2 = _ffn_forward("ffn2", u3, plan)

    dr3, df2, acc3 = _final_ln_loss_bwd("ln3_loss_bwd", h2, f2, target, g3, small["ln3_g"], small["ln3_b"], 0.5)
    du3 = _ffn_backward("ffn2", df2, u3, sv2, plan, in_first=False)
    dr2, dmx, acc2 = _resid_ln_bwd("ln2_bwd", du3, dr3, h1, m, sc3, g2, small["ln2_g"], small["ln2_b"], 1.0)
    du2, (dbias, dwa2, gsm) = _mix_backward(dmx, u2, svm, plan, mix_small)
    dr1, df1, acc1 = _resid_ln_bwd("ln1_bwd", du2, dr2, x, f1, sc2, g1, small["ln1_g"], small["ln1_b"], 0.5)
    du1 = _ffn_backward("ffn1", df1, u1, sv1, plan, in_first=True)
    grad_x, acc0 = _input_grad("input_grad", du1, dr1, x, sc1)

    drel = _mm("rel_bias_grad", "nt", _band_bias_grad(dbias).reshape(A_HEADS, CHUNK * A_BAND), onehot,
               (A_HEADS, REL_SIZE, CHUNK * A_BAND), tm=A_HEADS, tn=REL_SIZE, tk=4608, precision=HIGHEST)
    loss = jnp.sum(acc3[ROW_LOSS])
    dmod = jnp.stack([acc0[ROW_DSH], acc0[ROW_DSC], acc1[ROW_DGATE], acc1[ROW_DSH], acc1[ROW_DSC], acc2[ROW_DGATE],
                      acc2[ROW_DSH], acc2[ROW_DSC], acc3[ROW_DGATE]])
    kd = d // 4
    small_grads = dict(ln1_g=acc1[ROW_DLN_G], ln1_b=acc1[ROW_DLN_B], ln2_g=acc2[ROW_DLN_G], ln2_b=acc2[ROW_DLN_B],
                       ln3_g=acc3[ROW_DLN_G], ln3_b=acc3[ROW_DLN_B], b_alpha=gsm[GLA_ROW_DBALPHA],
                       gla_norm_g=gsm[GLA_ROW_DGNORM, :kd // B_HEADS * 2], rel_bias=drel, w_alpha2=dwa2)
    return loss, grad_x, small_grads, dmod


HBM_SPEC = pl.BlockSpec(memory_space=pl.ANY)


def _mesh_pos():
    return lax.axis_index("x"), lax.axis_index("y"), lax.axis_index("c")


def _other_chips(x, y):
    return [(1 - x, y), (x, 1 - y), (1 - x, 1 - y)]


def _remote(src, dst, send_sem, recv_sem, to):
    return pltpu.make_async_remote_copy(src_ref=src, dst_ref=dst, send_sem=send_sem, recv_sem=recv_sem,
                                        device_id=to, device_id_type=MESH)


def _allgather_rows(name, v):
    m_per, n = v.shape

    def body(x_ref, out_ref, send_sems, recv_sems, local_sem):
        x, y, c = _mesh_pos()
        me, sibling = (x, y, c), (x, y, 1 - c)
        chips = _other_chips(x, y)

        def rows(px, py, pc):
            return out_ref.at[pl.ds((4 * px + 2 * py + pc) * m_per, m_per), :]

        def copy(k, block, to, src=None):
            return _remote(rows(*block) if src is None else src, rows(*block), send_sems.at[k], recv_sems.at[k], to)

        mine = pltpu.make_async_copy(x_ref, rows(*me), local_sem)
        mine.start()
        first = [copy(0, me, sibling, src=x_ref)]
        first += [copy(1 + j, me, (*chip, c), src=x_ref) for j, chip in enumerate(chips)]
        for cp in first:
            cp.start()
        passed = [copy(4 + j, (*chip, c), sibling) for j, chip in enumerate(chips)]
        for j, chip in enumerate(chips):
            copy(1 + j, (*chip, c), me).wait_recv()
            passed[j].start()
        copy(0, sibling, me).wait_recv()
        for j, chip in enumerate(chips):
            copy(4 + j, (*chip, 1 - c), me).wait_recv()
        for cp in first + passed:
            cp.wait_send()
        mine.wait()

    return pl.pallas_call(
        body, name=name, out_shape=jax.ShapeDtypeStruct((N_DEV * m_per, n), v.dtype),
        in_specs=[pl.BlockSpec(memory_space=pltpu.VMEM)], out_specs=pl.BlockSpec(memory_space=pltpu.VMEM),
        scratch_shapes=[pltpu.SemaphoreType.DMA((7,)), pltpu.SemaphoreType.DMA((7,)), pltpu.SemaphoreType.DMA],
    )(v)


def _allgather_weights(shards):
    n = len(shards)

    def body(*refs):
        ins, outs = refs[:n], refs[n:2 * n]
        send_sems, recv_sems = refs[2 * n:]
        x, y, c = _mesh_pos()
        sibling = (x, y, 1 - c)
        chips = _other_chips(x, y)
        j0 = 2 * x + y

        def half(ref, w, j, hc):
            hr = shards[w].shape[0] // 2
            return ref.at[j, pl.ds(hc * hr, hr), :]

        sends = []
        for w in range(n):
            hr = shards[w].shape[0] // 2
            for r, chip in enumerate(chips):
                cp = _remote(ins[w].at[pl.ds(c * hr, hr), :], half(outs[w], w, j0, c), send_sems.at[w, r],
                             recv_sems.at[w, r], (*chip, c))
                cp.start()
                sends.append(cp)
        for w in range(n):
            for r, chip in enumerate(chips):
                jr = 2 * chip[0] + chip[1]
                landed = half(outs[w], w, jr, c)
                _remote(landed, landed, send_sems.at[w, r], recv_sems.at[w, r], (*chip, c)).wait_recv()
                fw = _remote(landed, landed, send_sems.at[w, 3 + r], recv_sems.at[w, 3 + r], sibling)
                fw.start()
                sends.append(fw)
        for w in range(n):
            for r, chip in enumerate(chips):
                jr = 2 * chip[0] + chip[1]
                got = half(outs[w], w, jr, 1 - c)
                _remote(got, got, send_sems.at[w, 3 + r], recv_sems.at[w, 3 + r], sibling).wait_recv()
        for cp in sends:
            cp.wait_send()

    return pl.pallas_call(
        body, name="allgather_weights",
        out_shape=[jax.ShapeDtypeStruct((N_CHIPS,) + sh.shape, sh.dtype) for sh in shards],
        in_specs=[HBM_SPEC] * n, out_specs=[HBM_SPEC] * n,
        scratch_shapes=[pltpu.SemaphoreType.DMA((n, 6)), pltpu.SemaphoreType.DMA((n, 6))],
    )(*shards)


def _half(ref, hc, col, *lead):
    rows, cols = ref.shape[-2:]
    if col:
        return ref.at[(*lead, slice(None), pl.ds(hc * (cols // 2), cols // 2))]
    return ref.at[(*lead, pl.ds(hc * (rows // 2), rows // 2), slice(None))]


def _half_shape(shape, col):
    return shape[:-2] + ((shape[-2], shape[-1] // 2) if col else (shape[-2] // 2, shape[-1]))


def _stage_gather_ici(shards, cols):
    n = len(shards)

    def copies(ins, outs, send, recv):
        x, y, c = _mesh_pos()
        j0 = 2 * x + y
        for w in range(n):
            for r, chip in enumerate(_other_chips(x, y)):
                jr = 2 * chip[0] + chip[1]
                mine = _remote(_half(ins[w], c, cols[w]), _half(outs[w], c, cols[w], j0),
                               send.at[3 * w + r], recv.at[3 * w + r], (*chip, c))
                landed = _half(outs[w], c, cols[w], jr)
                yield mine, _remote(landed, landed, send.at[3 * w + r], recv.at[3 * w + r], (*chip, c))

    def start(*refs):
        for mine, _ in copies(*refs):
            mine.start()

    def finish(*refs):
        pairs = list(copies(*refs))
        for _, theirs in pairs:
            theirs.wait_recv()
        for mine, _ in pairs:
            mine.wait_send()

    outs = [jax.ShapeDtypeStruct((N_CHIPS,) + sh.shape, sh.dtype) for sh in shards]
    return _Stage(shards, outs, 3 * n, start, finish)


def _stage_gather_d2d(partial, cols):
    n = len(partial)

    def copies(ins, outs, send, recv):
        x, y, c = _mesh_pos()
        for w in range(n):
            for r, chip in enumerate(_other_chips(x, y)):
                jr = 2 * chip[0] + chip[1]
                mine = _remote(_half(ins[w], c, cols[w], jr), _half(outs[w], c, cols[w], jr), send.at[3 * w + r],
                               recv.at[3 * w + r], (x, y, 1 - c))
                got = _half(outs[w], 1 - c, cols[w], jr)
                yield mine, _remote(got, got, send.at[3 * w + r], recv.at[3 * w + r], (x, y, 1 - c))

    def start(*refs):
        for mine, _ in copies(*refs):
            mine.start()

    def finish(*refs):
        pairs = list(copies(*refs))
        for _, theirs in pairs:
            theirs.wait_recv()
        for mine, _ in pairs:
            mine.wait_send()

    outs = [jax.ShapeDtypeStruct(p.shape, p.dtype) for p in partial]
    return _Stage(partial, outs, 3 * n, start, finish, aliases={w: w for w in range(n)})


def _stage_exchange_halves(grads, cols):
    n = len(grads)

    def copies(ins, outs, send, recv):
        x, y, c = _mesh_pos()
        for w in range(n):
            yield _remote(_half(ins[w], 1 - c, cols[w], slice(None)), outs[w], send.at[w], recv.at[w], (x, y, 1 - c))

    def start(*refs):
        for cp in copies(*refs):
            cp.start()

    def finish(*refs):
        cps = list(copies(*refs))
        for cp in cps:
            cp.wait_recv()
        for cp in cps:
            cp.wait_send()

    outs = [jax.ShapeDtypeStruct(_half_shape(g.shape, col), g.dtype) for g, col in zip(grads, cols)]
    return _Stage(grads, outs, n, start, finish)


def _stage_scatter(parts):
    n = len(parts)

    def copies(ins, outs, send, recv):
        x, y, c = _mesh_pos()
        for w in range(n):
            for r, chip in enumerate(_other_chips(x, y)):
                jr = 2 * chip[0] + chip[1]
                yield _remote(ins[w].at[jr], outs[w].at[r], send.at[3 * w + r], recv.at[3 * w + r], (*chip, c))

    def start(*refs):
        for cp in copies(*refs):
            cp.start()

    def finish(*refs):
        cps = list(copies(*refs))
        for cp in cps:
            cp.wait_recv()
        for cp in cps:
            cp.wait_send()

    outs = [jax.ShapeDtypeStruct((3,) + p.shape[1:], p.dtype) for p in parts]
    return _Stage(parts, outs, 3 * n, start, finish)


def _stage_share(fulls, cols):
    n = len(fulls)

    def copies(ins, outs, send, recv):
        x, y, c = _mesh_pos()
        for w in range(n):
            theirs = _half(outs[w], 1 - c, cols[w])
            yield (_remote(_half(ins[w], c, cols[w]), _half(outs[w], c, cols[w]), send.at[w], recv.at[w], (x, y, 1 - c)),
                   _remote(theirs, theirs, send.at[w], recv.at[w], (x, y, 1 - c)))

    def start(*refs):
        for mine, _ in copies(*refs):
            mine.start()

    def finish(*refs):
        pairs = list(copies(*refs))
        for _, theirs in pairs:
            theirs.wait_recv()
        for mine, _ in pairs:
            mine.wait_send()

    outs = [jax.ShapeDtypeStruct(h.shape, h.dtype) for h in fulls]
    return _Stage(fulls, outs, n, start, finish, aliases={w: w for w in range(n)})


def _run_stages(name, stages):
    return _pcall(None, stages, name=name, out_shape=[], in_specs=[], out_specs=[])()[1]


TILE_BYTES = 2 * 1024 * 1024


def _row_tile(rows, cols, itemsize=4):
    for t in (1024, 512, 256, 128, 64, 32, 16, 8):
        if rows % t == 0 and t * cols * itemsize <= TILE_BYTES:
            return t
    return rows


def _col_tile(rows, cols, itemsize=4):
    for t in (2048, 1024, 512, 256, 128):
        if cols % t == 0 and t * rows * itemsize <= TILE_BYTES:
            return t
    return cols


def _tiling(rows, cols, col):
    if col:
        tc = _col_tile(rows, cols)
        return (rows, tc), cols // tc
    tr = _row_tile(rows, cols)
    return (tr, cols), rows // tr


def _strip(col, i):
    return (0, i) if col else (i, 0)


def _pair_sum(name, g, recv, core, col):
    blk, nb = _tiling(*recv.shape[1:], col)

    def body(c_ref, g_ref, r_ref, o_ref):
        o_ref[...] = (g_ref[...] + r_ref[...]).astype(BF16)

    grid_spec = pltpu.PrefetchScalarGridSpec(
        num_scalar_prefetch=1, grid=(N_CHIPS, nb),
        in_specs=[pl.BlockSpec((None,) + blk, lambda j, i, cr: (j,) + _strip(col, cr[0] * nb + i)),
                  pl.BlockSpec((None,) + blk, lambda j, i, cr: (j,) + _strip(col, i))],
        out_specs=pl.BlockSpec((None,) + blk, lambda j, i, cr: (j,) + _strip(col, i)))
    return pl.pallas_call(body, name=name, out_shape=jax.ShapeDtypeStruct(recv.shape, BF16), grid_spec=grid_spec,
                          compiler_params=_cp("parallel", "parallel"))(core, g, recv)


def _quad_sum(name, own, landed, chip_core, col):
    rows, cols = landed.shape[1:]
    blk, nb = _tiling(rows, cols, col)
    full = (rows, 2 * cols) if col else (2 * rows, cols)

    def body(cc_ref, own_ref, l_ref, o_ref):
        o_ref[...] = ((own_ref[...].astype(F32) + l_ref[0].astype(F32)) + l_ref[1].astype(F32)) + l_ref[2].astype(F32)

    grid_spec = pltpu.PrefetchScalarGridSpec(
        num_scalar_prefetch=1, grid=(nb,),
        in_specs=[pl.BlockSpec((None,) + blk, lambda i, cc: (cc[0],) + _strip(col, i)),
                  pl.BlockSpec((3,) + blk, lambda i, cc: (0,) + _strip(col, i))],
        out_specs=pl.BlockSpec(blk, lambda i, cc: _strip(col, cc[1] * nb + i)))
    return pl.pallas_call(body, name=name, out_shape=jax.ShapeDtypeStruct(full, F32), grid_spec=grid_spec,
                          compiler_params=_cp("arbitrary"))(chip_core, own, landed)


def _device_sum(name, gathered):
    def body(g_ref, o_ref):
        total = g_ref[0]
        for k in range(1, N_DEV):
            total = total + g_ref[k]
        o_ref[...] = total

    return pl.pallas_call(body, name=name, out_shape=jax.ShapeDtypeStruct(gathered.shape[1:], F32))(gathered)


def _adamw(name, w, g, m, v):
    rows, cols = w.shape
    col = rows % SUBLANES != 0
    blk, nb = _tiling(rows, cols, col)
    bc1 = 1.0 - ADAM_B1 ** ADAM_STEP
    bc2 = 1.0 - ADAM_B2 ** ADAM_STEP

    def body(w_ref, g_ref, m_ref, v_ref, d_ref, mo_ref, vo_ref):
        gv = g_ref[...]
        mn = ADAM_B1 * m_ref[...] + (1.0 - ADAM_B1) * gv
        vn = ADAM_B2 * v_ref[...] + (1.0 - ADAM_B2) * (gv * gv)
        mo_ref[...] = mn
        vo_ref[...] = vn
        d_ref[...] = -ADAM_LR * ((mn / bc1) / (jnp.sqrt(vn / bc2) + ADAM_EPS) + ADAM_WD * w_ref[...])

    spec = pl.BlockSpec(blk, lambda i: _strip(col, i))
    return pl.pallas_call(
        body, name=name, out_shape=[jax.ShapeDtypeStruct((rows, cols), F32)] * 3, grid=(nb,),
        in_specs=[spec] * 4, out_specs=[spec] * 3, compiler_params=_cp("parallel"),
    )(w, g, m, v)


WEIGHTS = ["w_ada", "b_ada", "ffn1_w_in", "ffn1_w_out", "ln1_g", "ln1_b", "w_mix_in", "rel_bias", "w_alpha2",
           "b_alpha", "gla_norm_g", "w_proj_a", "w_proj_b", "w_mix_out", "ln2_g", "ln2_b", "ffn2_w_in", "ffn2_w_out",
           "ln3_g", "ln3_b"]
BIG = {"ffn1_w_in": True, "ffn1_w_out": False, "w_mix_in": False, "w_proj_a": True, "w_proj_b": True,
       "w_mix_out": False, "ffn2_w_in": True, "ffn2_w_out": False}
TRANSPOSED = ("w_mix_in",)
STACKED = ("ffn1_w_in", "ffn2_w_in")
GROUP_FFN1 = ("ffn1_w_in", "ffn1_w_out")
GROUP_PROJ = ("w_proj_a", "w_proj_b", "w_mix_out")
SMALL = ["ln1_g", "ln1_b", "ln2_g", "ln2_b", "ln3_g", "ln3_b", "b_alpha", "gla_norm_g", "rel_bias", "w_alpha2"]


def _pad_rows(vec, rows=SUBLANES):
    per = -(-vec.shape[0] // (rows * LANES)) * LANES
    return jnp.pad(vec, (0, rows * per - vec.shape[0])).reshape(rows, per)


def _silu(v):
    return v * _sigmoid(v)


class _MeshPlan:
    def __init__(self, shards, chip, core):
        self.shards, self.chip = shards, chip
        self.core1 = core.astype(jnp.int32).reshape(1)
        self.chip_core = jnp.stack([chip, core]).astype(jnp.int32)
        self.partial, self.full, self.local, self.pair, self.half, self.final, self.memos = {}, {}, {}, {}, {}, {}, {}
        ici, d2d, x1, x2, x3 = self.gather_ici, self.gather_d2d, self.exchange, self.scatter, self.share
        mix_in, in1, out1, in2, out2 = ("w_mix_in",), ("ffn1_w_in",), ("ffn1_w_out",), ("ffn2_w_in",), ("ffn2_w_out",)
        self.schedule = {
            "ffn1_in_fwd": [ici(mix_in)], "ffn1_out_fwd": [d2d(mix_in), ici(out2)],
            "mix_in_a": [d2d(out2)], "mix_in_g": [ici(GROUP_PROJ)],
            "attn_fwd": [ici(in2), d2d(GROUP_PROJ)], "gla_fwd": [d2d(in2)],
            "ffn2_dw_in": [x1(out2)], "ffn2_du": [x2(out2), x1(in2)], "mix_dw_out": [x3(out2)],
            "attn_bwd": [x2(in2)], "gla_bwd": [x3(in2), x1(GROUP_PROJ)],
            "mix_du_g": [x2(GROUP_PROJ)], "mix_dw_g": [x3(GROUP_PROJ)],
            "ffn1_out_bwd": [x1(mix_in)], "ffn1_dw_in": [x2(mix_in)], "ffn1_dw_out": [x3(mix_in), x1(in1)],
            "ffn1_du": [x2(in1), x1(out1)],
        }

    def weight(self, k):
        return self.full[k]

    def grad(self, k, g):
        r, cc = self.shards[k].shape
        if k not in STACKED:
            g = g.reshape(r, N_CHIPS, cc).transpose(1, 0, 2) if BIG[k] else g.reshape(N_CHIPS, r, cc)
        self.local[k] = g

    def memo(self, key, make):
        if key not in self.memos:
            self.memos[key] = make()
        return self.memos[key]

    def host(self, name, call):
        builders = self.schedule.get(name)
        if not builders:
            return call(None)
        built = [b() for b in builders]
        main, comm = call([st for st, _ in built])
        for (_, post), res in zip(built, comm):
            post(res)
        return main

    def run(self, name, builders):
        built = [b() for b in builders]
        for (_, post), res in zip(built, _run_stages(name, [st for st, _ in built])):
            post(res)

    def set_gathered(self, names, gathered):
        for k, g in zip(names, gathered):
            _, r, cc = g.shape
            g = lax.dynamic_update_slice(g, self.shards[k][None], (self.chip, 0, 0))
            if k not in STACKED:
                g = g.transpose(1, 0, 2).reshape(r, N_CHIPS * cc) if BIG[k] else g.reshape(N_CHIPS * r, cc)
            self.full[k] = g

    @staticmethod
    def cols(names):
        return [k in TRANSPOSED for k in names]

    def gather_ici(self, names):
        def post(res):
            self.partial.update(zip(names, res))
        return lambda: (_stage_gather_ici([self.shards[k] for k in names], self.cols(names)), post)

    def gather_d2d(self, names):
        return lambda: (_stage_gather_d2d([self.partial[k] for k in names], self.cols(names)),
                        lambda res: self.set_gathered(names, res))

    def exchange(self, names):
        def post(res):
            for k, r in zip(names, res):
                self.pair[k] = _pair_sum(f"pair_sum_{k}", self.local[k], r, self.core1, k in TRANSPOSED)
        return lambda: (_stage_exchange_halves([self.local[k] for k in names], self.cols(names)), post)

    def scatter(self, names):
        def post(res):
            for k, landed in zip(names, res):
                self.half[k] = _quad_sum(f"quad_sum_{k}", self.pair[k], landed, self.chip_core, k in TRANSPOSED)
        return lambda: (_stage_scatter([self.pair[k] for k in names]), post)

    def share(self, names):
        def post(res):
            self.final.update(zip(names, res))
        return lambda: (_stage_share([self.half[k] for k in names], self.cols(names)), post)


def _step(args):
    x_pos, y_pos, c_pos = _mesh_pos()
    chip = 2 * x_pos + y_pos
    dev = 4 * x_pos + 2 * y_pos + c_pos
    take = lambda name, k: args[name][0].T if k in TRANSPOSED else args[name][0]
    w = {k: take(k, k) for k in WEIGHTS}
    mom = {k: take("m_" + k, k) for k in WEIGHTS}
    vel = {k: take("v_" + k, k) for k in WEIGHTS}
    x = args["x"][0]
    target = args["loss_target"][0]
    s, d = x.shape
    kd = d // 4
    rel_sh = w["rel_bias"].shape[1]
    wa2_sh = w["w_alpha2"].shape[1]
    ada_sh = w["w_ada"].shape[1]

    n_rel, n_wa2 = A_HEADS * rel_sh, GATE_RANK * wa2_sh
    packed = _pad_rows(jnp.concatenate([args["c"].reshape(-1), w["rel_bias"].reshape(-1), w["w_alpha2"].reshape(-1)]))
    got = _allgather_rows("gather_small_inputs", packed).reshape(N_DEV, -1)
    c_all = got[:, :d]
    per_chip = got[0::2]
    rel_bias = per_chip[:, d:d + n_rel].reshape(N_CHIPS, A_HEADS, rel_sh).transpose(1, 0, 2).reshape(A_HEADS, -1)
    w_alpha2 = per_chip[:, d + n_rel:d + n_rel + n_wa2].reshape(N_CHIPS, GATE_RANK, wa2_sh).transpose(1, 0, 2)
    w_alpha2 = w_alpha2.reshape(GATE_RANK, -1)

    b_shard = lax.dynamic_slice(w["b_ada"], (chip * ada_sh,), (ada_sh,))
    mod_shard = _mm("ada_fwd", "nn", c_all, w["w_ada"], (N_DEV, ada_sh, d), tm=N_DEV, tn=_tile(ada_sh, (512, 128)),
                    tk=d, precision=HIGHEST, a_fn=_silu, add=jnp.broadcast_to(b_shard[None], (N_DEV, ada_sh)))
    mod_all = _allgather_rows("gather_mod", mod_shard).reshape(N_DEV, N_DEV, ada_sh)[0::2]
    mod_all = mod_all.transpose(1, 0, 2).reshape(N_DEV, N_MOD * d)
    mod = lax.dynamic_index_in_dim(mod_all, dev, 0, keepdims=False).reshape(N_MOD, d)

    names = list(BIG)
    plan = _MeshPlan({k: w[k].astype(BF16) for k in names}, chip, c_pos)
    plan.set_gathered(GROUP_FFN1, _allgather_weights([plan.shards[k] for k in GROUP_FFN1]))

    small = dict(rel_bias=rel_bias, w_alpha2=w_alpha2, b_alpha=w["b_alpha"][None], gla_norm_g=w["gla_norm_g"][None])
    for k in ("ln1_g", "ln1_b", "ln2_g", "ln2_b", "ln3_g", "ln3_b"):
        small[k] = w[k][None]
    loss_local, grad_x, small_grads, dmod = _device_step(x, target, mod, small, plan)
    loss = lax.psum(loss_local, ("x", "y", "c"))
    plan.run("grad_tail_scatter", [plan.scatter(GROUP_FFN1[1:]), plan.share(GROUP_FFN1[:1])])
    plan.run("grad_tail_share", [plan.share(GROUP_FFN1[1:])])

    flat = jnp.concatenate([small_grads[k].reshape(-1) for k in SMALL] + [dmod.reshape(-1)])
    n_small = flat.shape[0] - N_MOD * d
    packed = _pad_rows(flat)
    all_small = _allgather_rows("gather_small_grads", packed).reshape(N_DEV, SUBLANES, -1)
    summed = _device_sum("small_grad_sum", all_small).reshape(-1)
    dmod_all = all_small.reshape(N_DEV, -1)[:, n_small:n_small + N_MOD * d]
    dmod_shard = lax.dynamic_slice(dmod_all, (0, chip * ada_sh), (N_DEV, ada_sh))
    grads = {"b_ada": summed[n_small:n_small + N_MOD * d]}
    off = 0
    for k in SMALL:
        size = small_grads[k].size
        grads[k] = summed[off:off + size].reshape(small_grads[k].shape)
        off += size
    grads["rel_bias"] = lax.dynamic_slice(grads["rel_bias"], (0, chip * rel_sh), (A_HEADS, rel_sh))
    grads["w_alpha2"] = lax.dynamic_slice(grads["w_alpha2"], (0, chip * wa2_sh), (GATE_RANK, wa2_sh))
    grads["w_ada"] = _mm("ada_bwd", "nn", jnp.pad(c_all.T, ((0, 0), (0, LANES - N_DEV))),
                         jnp.pad(dmod_shard, ((0, LANES - N_DEV), (0, 0))), (d, ada_sh, LANES), tm=_tile(d, (1024,)),
                         tn=_tile(ada_sh, (512, 128)), tk=LANES, precision=HIGHEST, a_fn=_silu)

    grads.update(plan.final)

    delta, new_m, new_v = {}, {}, {}
    for k in ["w_ada"] + names:
        delta[k], new_m[k], new_v[k] = _adamw(f"adamw_{k}", w[k], grads[k], mom[k], vel[k])
    tiny = ["b_ada"] + SMALL
    pack = lambda src: _pad_rows(jnp.concatenate([src[k].reshape(-1) for k in tiny]), rows=1).reshape(-1, LANES)
    outs = _adamw("adamw_small", pack(w), pack(grads), pack(mom), pack(vel))
    off = 0
    for k in tiny:
        size = w[k].size
        for dst, src in zip((delta, new_m, new_v), outs):
            dst[k] = src.reshape(-1)[off:off + size].reshape(w[k].shape)
        off += size

    give = lambda src: [src[k].T[None] if k in TRANSPOSED else src[k][None] for k in WEIGHTS]
    return (loss, grad_x[None], *give(grads), *give(delta), *give(new_m), *give(new_v))


def kernel(x, c, w_ada, b_ada, ffn1_w_in, ffn1_w_out, ln1_g, ln1_b, w_mix_in, rel_bias, w_alpha2, b_alpha, gla_norm_g, w_proj_a, w_proj_b, w_mix_out, ln2_g, ln2_b, ffn2_w_in, ffn2_w_out, ln3_g, ln3_b, loss_target, m_w_ada, m_b_ada, m_ffn1_w_in, m_ffn1_w_out, m_ln1_g, m_ln1_b, m_w_mix_in, m_rel_bias, m_w_alpha2, m_b_alpha, m_gla_norm_g, m_w_proj_a, m_w_proj_b, m_w_mix_out, m_ln2_g, m_ln2_b, m_ffn2_w_in, m_ffn2_w_out, m_ln3_g, m_ln3_b, v_w_ada, v_b_ada, v_ffn1_w_in, v_ffn1_w_out, v_ln1_g, v_ln1_b, v_w_mix_in, v_rel_bias, v_w_alpha2, v_b_alpha, v_gla_norm_g, v_w_proj_a, v_w_proj_b, v_w_mix_out, v_ln2_g, v_ln2_b, v_ffn2_w_in, v_ffn2_w_out, v_ln3_g, v_ln3_b):
    return _step(dict(locals()))
```

```python
import functools

import jax
import jax.numpy as jnp
from jax import lax
from jax.experimental import pallas as pl
from jax.experimental.pallas import tpu as pltpu

F32 = jnp.float32
BF16 = jnp.bfloat16
MESH = pl.DeviceIdType.MESH
HIGHEST = lax.Precision.HIGHEST

VMEM_LIMIT_BYTES = 56 * 1024 * 1024
LANES = 128
SUBLANES = 8

CHUNK = 64
A_HEADS = 16
A_HEAD_DIM = 64
A_PAST_CHUNKS = 8
A_BAND = (A_PAST_CHUNKS + 1) * CHUNK
A_PAD = A_PAST_CHUNKS * CHUNK
REL_CLIP = 256
REL_SIZE = REL_CLIP + CHUNK
B_HEADS = 4
GATE_RANK = 16
GATE_TAU = 16.0
N_MOD = 9
DEPTH = 1
ALPHA = (2.0 * DEPTH) ** 0.25
LN_EPS = 1e-5
RMS_EPS = 1e-6
ADAM_LR = 0.001
ADAM_B1 = 0.9
ADAM_B2 = 0.999
ADAM_EPS = 1e-08
ADAM_WD = 0.01
ADAM_STEP = 10
NEG_BIG = -1e30

N_CHIPS = 4
N_DEV = 8


def _cp(*sem):
    return pltpu.CompilerParams(dimension_semantics=sem, vmem_limit_bytes=VMEM_LIMIT_BYTES)


class _Stage:
    def __init__(self, arrays, out_shapes, n_sems, start, finish, aliases=None):
        self.arrays, self.out_shapes, self.n_sems = list(arrays), list(out_shapes), n_sems
        self.start, self.finish, self.aliases = start, finish, dict(aliases or {})


def _pcall(body, stages, *, name, out_shape, in_specs, out_specs, grid=(), scratch_shapes=(), compiler_params=None):
    single = not isinstance(out_shape, (list, tuple))
    outs = [out_shape] if single else list(out_shape)
    ospecs = [out_specs] if single else list(out_specs)
    in_specs, scratch_shapes = list(in_specs), list(scratch_shapes)
    n_in, n_out, n_sc = len(in_specs), len(outs), len(scratch_shapes)
    stages = list(stages or [])
    c_in = [a for st in stages for a in st.arrays]
    c_out = [o for st in stages for o in st.out_shapes]
    aliases = {}
    io, oo = n_in, n_out
    for st in stages:
        for a, b in st.aliases.items():
            aliases[io + a] = oo + b
        io += len(st.arrays)
        oo += len(st.out_shapes)

    def wrapped(*refs):
        ins = refs[:n_in]
        cins = refs[n_in:n_in + len(c_in)]
        base = n_in + len(c_in)
        mouts = refs[base:base + n_out]
        couts = refs[base + n_out:base + n_out + len(c_out)]
        base += n_out + len(c_out)
        scr = refs[base:base + n_sc]
        sems = refs[base + n_sc:]

        def each(phase):
            i = o = 0
            for k, st in enumerate(stages):
                fn = st.start if phase == 0 else st.finish
                fn(cins[i:i + len(st.arrays)], couts[o:o + len(st.out_shapes)], sems[2 * k], sems[2 * k + 1])
                i += len(st.arrays)
                o += len(st.out_shapes)

        if stages and grid:
            first = functools.reduce(jnp.logical_and, [pl.program_id(a) == 0 for a in range(len(grid))])
            last = functools.reduce(jnp.logical_and, [pl.program_id(a) == g - 1 for a, g in enumerate(grid)])
            pl.when(first)(lambda: each(0))
            if body is not None:
                body(*ins, *mouts, *scr)
            pl.when(last)(lambda: each(1))
        else:
            each(0)
            if body is not None:
                body(*ins, *mouts, *scr)
            each(1)

    sem_shapes = []
    for st in stages:
        sem_shapes += [pltpu.SemaphoreType.DMA((st.n_sems,)), pltpu.SemaphoreType.DMA((st.n_sems,))]
    kwargs = dict(grid=grid) if grid else {}
    if compiler_params is not None:
        kwargs["compiler_params"] = compiler_params

    def run(*operands):
        res = pl.pallas_call(
            wrapped, name=name, out_shape=outs + c_out, in_specs=in_specs + [HBM_SPEC] * len(c_in),
            out_specs=ospecs + [HBM_SPEC] * len(c_out), scratch_shapes=scratch_shapes + sem_shapes,
            input_output_aliases=aliases, **kwargs)(*operands, *c_in)
        main = res[0] if single else tuple(res[:n_out])
        if not stages:
            return main
        comm, o = [], n_out
        for st in stages:
            comm.append(list(res[o:o + len(st.out_shapes)]))
            o += len(st.out_shapes)
        return main, comm

    return run


LONG_K = (2048, 1024)


def _tile(n, prefs):
    for t in prefs:
        if t <= n and n % t == 0:
            return t
    return n


_DIMS = {"nn": (((1,), (0,)), ((), ())), "nt": (((1,), (1,)), ((), ())), "tn": (((0,), (0,)), ((), ()))}


def _dot(a, b, mode="nn", precision=None):
    return lax.dot_general(a, b, _DIMS[mode], precision=precision, preferred_element_type=F32)


def _sigmoid(x):
    return 1.0 / (1.0 + jnp.exp(-x))


def _mm(name, mode, a, b, mnk, *, tm, tn, tk, out_dtype=F32, precision=None, a_spec=None, b_spec=None,
        out_shape=None, o_spec=None, add=None, a_fn=None, stages=None):
    m, n, k = mnk
    assert m % tm == 0 and n % tn == 0 and k % tk == 0, (name, mnk, tm, tn, tk)
    nk = k // tk
    if a_spec is None:
        a_spec = {"nn": pl.BlockSpec((tm, tk), lambda i, j, kk: (i, kk)),
                  "nt": pl.BlockSpec((tm, tk), lambda i, j, kk: (i, kk)),
                  "tn": pl.BlockSpec((tk, tm), lambda i, j, kk: (kk, i))}[mode]
    if b_spec is None:
        b_spec = {"nn": pl.BlockSpec((tk, tn), lambda i, j, kk: (kk, j)),
                  "nt": pl.BlockSpec((tn, tk), lambda i, j, kk: (j, kk)),
                  "tn": pl.BlockSpec((tk, tn), lambda i, j, kk: (kk, j))}[mode]
    if o_spec is None:
        o_spec = pl.BlockSpec((tm, tn), lambda i, j, kk: (i, j))
    if out_shape is None:
        out_shape = (m, n)
    has_add = add is not None

    def body(*refs):
        a_ref, b_ref = refs[0], refs[1]
        add_ref = refs[2] if has_add else None
        o_ref = refs[3] if has_add else refs[2]
        av = a_ref[...]
        if a_fn is not None:
            av = a_fn(av)
        part = _dot(av, b_ref[...], mode, precision)

        def finish(total):
            if has_add:
                total = total + add_ref[...]
            o_ref[...] = total.astype(out_dtype)

        if nk == 1:
            finish(part)
        else:
            acc_ref = refs[-1]
            kk = pl.program_id(2)

            @pl.when(kk == 0)
            def _():
                acc_ref[...] = part

            @pl.when(kk > 0)
            def _():
                acc_ref[...] += part

            @pl.when(kk == nk - 1)
            def _():
                finish(acc_ref[...])

    in_specs = [a_spec, b_spec]
    operands = [a, b]
    if has_add:
        in_specs.append(pl.BlockSpec((tm, tn), lambda i, j, kk: (i, j)))
        operands.append(add)
    return _pcall(
        body, stages, name=name, out_shape=jax.ShapeDtypeStruct(out_shape, out_dtype), grid=(m // tm, n // tn, nk),
        in_specs=in_specs, out_specs=o_spec,
        scratch_shapes=[pltpu.VMEM((tm, tn), F32)] if nk > 1 else [],
        compiler_params=_cp("arbitrary", "arbitrary", "arbitrary") if stages else _cp("parallel", "parallel", "arbitrary"),
    )(*operands)


def _row_spec(tr, d):
    return pl.BlockSpec((tr, d), lambda i: (i, 0))


def _vec_spec(d, rows=1):
    return pl.BlockSpec((rows, d), lambda i: (0, 0))


def _modulate(name, x, sh, sc):
    s, d = x.shape
    tr = _tile(s, (512, 256))

    def body(x_ref, sh_ref, sc_ref, o_ref):
        o_ref[...] = (x_ref[...] * (1.0 + sc_ref[...]) + sh_ref[...]).astype(BF16)

    return pl.pallas_call(
        body, name=name, out_shape=jax.ShapeDtypeStruct((s, d), BF16), grid=(s // tr,),
        in_specs=[_row_spec(tr, d), _vec_spec(d), _vec_spec(d)], out_specs=_row_spec(tr, d),
        compiler_params=_cp("parallel"),
    )(x, sh, sc)


def _ln_stats(r):
    mu = jnp.mean(r, axis=-1, keepdims=True)
    xc = r - mu
    var = jnp.mean(xc * xc, axis=-1, keepdims=True)
    rstd = lax.rsqrt(var + LN_EPS)
    return xc * rstd, rstd


def _resid_ln_fwd(name, x, f, gate, ln_g, ln_b, sh_n, sc_n, coef):
    s, d = x.shape
    tr = _tile(s, (256,))

    def body(x_ref, f_ref, gate_ref, g_ref, b_ref, sh_ref, sc_ref, h_ref, u_ref):
        r = ALPHA * x_ref[...] + (coef * gate_ref[...]) * f_ref[...]
        xhat, _ = _ln_stats(r)
        h = xhat * g_ref[...] + b_ref[...]
        h_ref[...] = h
        u_ref[...] = (h * (1.0 + sc_ref[...]) + sh_ref[...]).astype(BF16)

    return pl.pallas_call(
        body, name=name, out_shape=(jax.ShapeDtypeStruct((s, d), F32), jax.ShapeDtypeStruct((s, d), BF16)),
        grid=(s // tr,), in_specs=[_row_spec(tr, d), _row_spec(tr, d)] + [_vec_spec(d)] * 5,
        out_specs=(_row_spec(tr, d), _row_spec(tr, d)), compiler_params=_cp("parallel"),
    )(x, f, gate, ln_g, ln_b, sh_n, sc_n)


ROW_DSC, ROW_DSH, ROW_DLN_G, ROW_DLN_B, ROW_DGATE, ROW_LOSS = 0, 1, 2, 3, 4, 5


def _ln_bwd_core(dy, xhat, rstd, ln_g):
    dxhat = dy * ln_g
    m1 = jnp.mean(dxhat, axis=-1, keepdims=True)
    m2 = jnp.mean(dxhat * xhat, axis=-1, keepdims=True)
    return rstd * (dxhat - m1 - xhat * m2)


def _colsum(v):
    return jnp.sum(v, axis=0, keepdims=True)


def _final_ln_loss_bwd(name, x, f, target, gate, ln_g, ln_b, coef):
    s, d = x.shape
    tr = _tile(s, (256,))
    inv_d = 1.0 / d

    def body(x_ref, f_ref, t_ref, gate_ref, g_ref, b_ref, dr_ref, df_ref, acc_ref):
        @pl.when(pl.program_id(0) == 0)
        def _():
            acc_ref[...] = jnp.zeros_like(acc_ref)

        fv = f_ref[...]
        r = ALPHA * x_ref[...] + (coef * gate_ref[...]) * fv
        xhat, rstd = _ln_stats(r)
        h = xhat * g_ref[...] + b_ref[...]
        err = h - t_ref[...]
        dy = err * inv_d
        dr = _ln_bwd_core(dy, xhat, rstd, g_ref[...])
        dr_ref[...] = dr
        df_ref[...] = ((coef * gate_ref[...]) * dr).astype(BF16)
        acc_ref[ROW_DLN_G:ROW_DLN_G + 1, :] += _colsum(dy * xhat)
        acc_ref[ROW_DLN_B:ROW_DLN_B + 1, :] += _colsum(dy)
        acc_ref[ROW_DGATE:ROW_DGATE + 1, :] += _colsum((coef * dr) * fv)
        acc_ref[ROW_LOSS:ROW_LOSS + 1, :] += _colsum(err * err) * (0.5 * inv_d)

    return pl.pallas_call(
        body, name=name,
        out_shape=(jax.ShapeDtypeStruct((s, d), F32), jax.ShapeDtypeStruct((s, d), BF16),
                   jax.ShapeDtypeStruct((SUBLANES, d), F32)),
        grid=(s // tr,), in_specs=[_row_spec(tr, d)] * 3 + [_vec_spec(d)] * 3,
        out_specs=(_row_spec(tr, d), _row_spec(tr, d), _vec_spec(d, SUBLANES)),
        compiler_params=_cp("arbitrary"),
    )(x, f, target, gate, ln_g, ln_b)


def _resid_ln_bwd(name, du_n, dr_n, x, f, sc_n, gate, ln_g, ln_b, coef):
    s, d = x.shape
    tr = _tile(s, (256,))

    def body(du_ref, drn_ref, x_ref, f_ref, sc_ref, gate_ref, g_ref, b_ref, dr_ref, df_ref, acc_ref):
        @pl.when(pl.program_id(0) == 0)
        def _():
            acc_ref[...] = jnp.zeros_like(acc_ref)

        fv = f_ref[...]
        du = du_ref[...]
        r = ALPHA * x_ref[...] + (coef * gate_ref[...]) * fv
        xhat, rstd = _ln_stats(r)
        h = xhat * g_ref[...] + b_ref[...]
        dy = du * (1.0 + sc_ref[...]) + ALPHA * drn_ref[...]
        dr = _ln_bwd_core(dy, xhat, rstd, g_ref[...])
        dr_ref[...] = dr
        df_ref[...] = ((coef * gate_ref[...]) * dr).astype(BF16)
        acc_ref[ROW_DSC:ROW_DSC + 1, :] += _colsum(du * h)
        acc_ref[ROW_DSH:ROW_DSH + 1, :] += _colsum(du)
        acc_ref[ROW_DLN_G:ROW_DLN_G + 1, :] += _colsum(dy * xhat)
        acc_ref[ROW_DLN_B:ROW_DLN_B + 1, :] += _colsum(dy)
        acc_ref[ROW_DGATE:ROW_DGATE + 1, :] += _colsum((coef * dr) * fv)

    return pl.pallas_call(
        body, name=name,
        out_shape=(jax.ShapeDtypeStruct((s, d), F32), jax.ShapeDtypeStruct((s, d), BF16),
                   jax.ShapeDtypeStruct((SUBLANES, d), F32)),
        grid=(s // tr,), in_specs=[_row_spec(tr, d)] * 4 + [_vec_spec(d)] * 4,
        out_specs=(_row_spec(tr, d), _row_spec(tr, d), _vec_spec(d, SUBLANES)),
        compiler_params=_cp("arbitrary"),
    )(du_n, dr_n, x, f, sc_n, gate, ln_g, ln_b)


def _input_grad(name, du, dr, x, sc):
    s, d = x.shape
    tr = _tile(s, (256,))

    def body(du_ref, dr_ref, x_ref, sc_ref, gx_ref, acc_ref):
        @pl.when(pl.program_id(0) == 0)
        def _():
            acc_ref[...] = jnp.zeros_like(acc_ref)

        du = du_ref[...]
        gx_ref[...] = du * (1.0 + sc_ref[...]) + ALPHA * dr_ref[...]
        acc_ref[ROW_DSC:ROW_DSC + 1, :] += _colsum(du * x_ref[...])
        acc_ref[ROW_DSH:ROW_DSH + 1, :] += _colsum(du)

    return pl.pallas_call(
        body, name=name,
        out_shape=(jax.ShapeDtypeStruct((s, d), F32), jax.ShapeDtypeStruct((SUBLANES, d), F32)),
        grid=(s // tr,), in_specs=[_row_spec(tr, d)] * 3 + [_vec_spec(d)],
        out_specs=(_row_spec(tr, d), _vec_spec(d, SUBLANES)), compiler_params=_cp("arbitrary"),
    )(du, dr, x, sc)


def _ffn_in_fwd(name, u, w_in, stages=None):
    s, d = u.shape
    cs = w_in.shape[2]
    f = 2 * cs
    tm, tn = _tile(s, (2048, 1024, 512)), _tile(cs, (256, 128))
    nb = f // tn
    nbs = cs // tn

    def body(u_ref, wa_ref, wb_ref, ab_ref, act_ref):
        uv = u_ref[...]
        a = _dot(uv, wa_ref[...])
        b = _dot(uv, wb_ref[...])
        ab_ref[0] = a.astype(BF16)
        ab_ref[1] = b.astype(BF16)
        act_ref[...] = (a * _sigmoid(a) * b).astype(BF16)

    return _pcall(
        body, stages, name=name,
        out_shape=(jax.ShapeDtypeStruct((2, s, f), BF16), jax.ShapeDtypeStruct((s, f), BF16)),
        grid=(s // tm, nb),
        in_specs=[pl.BlockSpec((tm, d), lambda i, j: (i, 0)),
                  pl.BlockSpec((None, d, tn), lambda i, j: (j // nbs, 0, j % nbs)),
                  pl.BlockSpec((None, d, tn), lambda i, j: (2 + j // nbs, 0, j % nbs))],
        out_specs=(pl.BlockSpec((2, tm, tn), lambda i, j: (0, i, j)), pl.BlockSpec((tm, tn), lambda i, j: (i, j))),
        compiler_params=_cp("arbitrary", "arbitrary"),
    )(u, w_in, w_in)


def _ffn_out_bwd(name, df, w_out, ab, stages=None):
    s, d = df.shape
    f = w_out.shape[0]
    tm, tn = _tile(s, (1024, 512)), _tile(f, (512, 256, 128))

    def body(df_ref, w_ref, ab_ref, dab_ref):
        dact = _dot(df_ref[...], w_ref[...], "nt")
        a = ab_ref[0].astype(F32)
        b = ab_ref[1].astype(F32)
        sg = _sigmoid(a)
        dab_ref[0] = (dact * b * (sg * (1.0 + a * (1.0 - sg)))).astype(BF16)
        dab_ref[1] = (dact * (a * sg)).astype(BF16)

    return _pcall(
        body, stages, name=name, out_shape=jax.ShapeDtypeStruct((2, s, f), BF16), grid=(s // tm, f // tn),
        in_specs=[pl.BlockSpec((tm, d), lambda i, j: (i, 0)), pl.BlockSpec((tn, d), lambda i, j: (j, 0)),
                  pl.BlockSpec((2, tm, tn), lambda i, j: (0, i, j))],
        out_specs=pl.BlockSpec((2, tm, tn), lambda i, j: (0, i, j)),
        compiler_params=_cp("arbitrary", "arbitrary"),
    )(df, w_out, ab)


def _ffn_forward(tag, u, plan):
    w_in, w_out = plan.weight(f"{tag}_w_in"), plan.weight(f"{tag}_w_out")
    s, d = u.shape
    f = w_out.shape[0]
    ab, act = plan.host(f"{tag}_in_fwd", lambda st: _ffn_in_fwd(f"{tag}_in_fwd", u, w_in, st))
    out = plan.host(f"{tag}_out_fwd", lambda st: _mm(
        f"{tag}_out_fwd", "nn", act, w_out, (s, d, f), tm=_tile(s, (1024,)), tn=_tile(d, (1024,)),
        tk=_tile(f, (2816, 1408, 512, 128)), stages=st))
    return out, (ab, act)


def _ffn_backward(tag, df, u, saved, plan, in_first):
    w_in, w_out = plan.weight(f"{tag}_w_in"), plan.weight(f"{tag}_w_out")
    ab, act = saved
    s, d = u.shape
    f = w_out.shape[0]
    dab = plan.host(f"{tag}_out_bwd", lambda st: _ffn_out_bwd(f"{tag}_out_bwd", df, w_out, ab, st))
    cs = w_in.shape[2]
    tk = _tile(cs, (2816, 1408, 256, 128))
    nkh, nks = f // tk, cs // tk
    tmd = _tile(d, (1024,))
    tn = _tile(cs, (1408, 256, 128))
    nbh, nbs = f // tn, cs // tn
    tks = _tile(s, LONG_K)

    def dw_in():
        plan.grad(f"{tag}_w_in", plan.host(f"{tag}_dw_in", lambda st: _mm(
            f"{tag}_dw_in", "tn", u, dab, (d, 2 * f, s), tm=tmd, tn=tn, tk=tks,
            b_spec=pl.BlockSpec((None, tks, tn), lambda i, j, kk: (j // nbh, kk, j % nbh)), out_shape=(N_CHIPS, d, cs),
            o_spec=pl.BlockSpec((None, tmd, tn), lambda i, j, kk: (j // nbs, i, j % nbs)), stages=st)))

    def dw_out():
        plan.grad(f"{tag}_w_out", plan.host(f"{tag}_dw_out", lambda st: _mm(
            f"{tag}_dw_out", "tn", act, df, (f, d, s), tm=_tile(f, (1408, 512, 128)), tn=tmd, tk=tks, stages=st)))

    for step in ((dw_in, dw_out) if in_first else (dw_out, dw_in)):
        step()
    return plan.host(f"{tag}_du", lambda st: _mm(
        f"{tag}_du", "nt", dab, w_in, (s, d, 2 * f), tm=_tile(s, (1024,)), tn=tmd, tk=tk,
        a_spec=pl.BlockSpec((None, _tile(s, (1024,)), tk), lambda i, j, kk: (kk // nkh, i, kk % nkh)),
        b_spec=pl.BlockSpec((None, tmd, tk), lambda i, j, kk: (kk // nks, j, kk % nks)), stages=st))


ATTN_Q = 4 * CHUNK
ATTN_W = ATTN_Q + A_PAD


def _band_bias(bias):
    n = ATTN_Q // CHUNK
    rows = [jnp.pad(bias, ((0, 0), (0, 0), (i * CHUNK, (n - 1 - i) * CHUNK)), constant_values=NEG_BIG)
            for i in range(n)]
    return jnp.concatenate(rows, axis=1)


def _band_bias_grad(dband):
    n = ATTN_Q // CHUNK
    parts = [dband[:, i * CHUNK:(i + 1) * CHUNK, i * CHUNK:i * CHUNK + A_BAND] for i in range(n)]
    return functools.reduce(jnp.add, parts)


def _attn_probs(q, kw, bias, key0):
    sc = _dot(q, kw, "nt") * (A_HEAD_DIM ** -0.5) + bias
    ks = lax.broadcasted_iota(jnp.int32, sc.shape, 1)
    sc = jnp.where(key0 + ks >= 0, sc, NEG_BIG)
    p = jnp.exp(sc - jnp.max(sc, axis=-1, keepdims=True))
    return p / jnp.sum(p, axis=-1, keepdims=True)


def _head_masks():
    lane = lax.broadcasted_iota(jnp.int32, (1, LANES), 1)
    return [lane // A_HEAD_DIM == h for h in range(LANES // A_HEAD_DIM)]


def _attn_fwd(p1, kvp, band, stages=None):
    s = p1.shape[0]
    aw = A_HEADS * A_HEAD_DIM
    nblk = aw // LANES
    hpb = LANES // A_HEAD_DIM
    assert s % ATTN_Q == 0

    def body(q_ref, k_ref, v_ref, b_ref, o_ref):
        base = pl.multiple_of(pl.program_id(1) * ATTN_Q, ATTN_Q)
        qv = q_ref[...]
        kw = k_ref[pl.ds(base, ATTN_W), :]
        vw = v_ref[pl.ds(base, ATTN_W), :]
        out = jnp.zeros((ATTN_Q, LANES), F32)
        for h, mask in enumerate(_head_masks()):
            p = _attn_probs(jnp.where(mask, qv, jnp.zeros_like(qv)), kw, b_ref[h], base - A_PAD)
            out = jnp.where(mask, _dot(p.astype(BF16), vw), out)
        o_ref[...] = out.astype(BF16)

    kv_rows = s + A_PAD
    return _pcall(
        body, stages, name="attn_fwd", out_shape=jax.ShapeDtypeStruct((s, aw), BF16), grid=(nblk, s // ATTN_Q),
        in_specs=[pl.BlockSpec((ATTN_Q, LANES), lambda b, i: (i, b)),
                  pl.BlockSpec((kv_rows, LANES), lambda b, i: (0, b)),
                  pl.BlockSpec((kv_rows, LANES), lambda b, i: (0, nblk + b)),
                  pl.BlockSpec((hpb, ATTN_Q, ATTN_W), lambda b, i: (b, 0, 0))],
        out_specs=pl.BlockSpec((ATTN_Q, LANES), lambda b, i: (i, b)),
        compiler_params=_cp("arbitrary", "arbitrary"),
    )(p1, kvp, kvp, band)


def _attn_bwd(p1, kvp, band, dya, stages=None):
    s = p1.shape[0]
    aw = A_HEADS * A_HEAD_DIM
    nblk = aw // LANES
    hpb = LANES // A_HEAD_DIM
    scale = A_HEAD_DIM ** -0.5

    def body(q_ref, k_ref, v_ref, b_ref, do_ref, dq_ref, dk_ref, dv_ref, db_ref):
        @pl.when(pl.program_id(1) == 0)
        def _():
            dk_ref[...] = jnp.zeros_like(dk_ref)
            dv_ref[...] = jnp.zeros_like(dv_ref)
            db_ref[...] = jnp.zeros_like(db_ref)

        base = pl.multiple_of(pl.program_id(1) * ATTN_Q, ATTN_Q)
        window = pl.ds(base, ATTN_W)
        kw = k_ref[window, :]
        vw = v_ref[window, :]
        qv = q_ref[...]
        dov = do_ref[...]
        dq = jnp.zeros((ATTN_Q, LANES), F32)
        dk = jnp.zeros((ATTN_W, LANES), F32)
        dv = jnp.zeros((ATTN_W, LANES), F32)
        for h, mask in enumerate(_head_masks()):
            qh = jnp.where(mask, qv, jnp.zeros_like(qv))
            doh = jnp.where(mask, dov, jnp.zeros_like(dov))
            p = _attn_probs(qh, kw, b_ref[h], base - A_PAD)
            dp = _dot(doh, vw, "nt")
            ds = p * (dp - jnp.sum(p * dp, axis=-1, keepdims=True))
            db_ref[h] += ds
            dsb = (ds * scale).astype(BF16)
            dq = jnp.where(mask, _dot(dsb, kw), dq)
            dk = dk + _dot(dsb, qh, "tn")
            dv = dv + _dot(p.astype(BF16), doh, "tn")
        dq_ref[...] = dq.astype(BF16)
        dk_ref[window, :] += dk
        dv_ref[window, :] += dv

    kv_rows = s + A_PAD
    q_spec = pl.BlockSpec((ATTN_Q, LANES), lambda b, i: (i, b))
    acc_spec = pl.BlockSpec((kv_rows, LANES), lambda b, i: (0, b))
    b_spec = pl.BlockSpec((hpb, ATTN_Q, ATTN_W), lambda b, i: (b, 0, 0))
    return _pcall(
        body, stages, name="attn_bwd",
        out_shape=(jax.ShapeDtypeStruct((s, aw), BF16), jax.ShapeDtypeStruct((kv_rows, aw), F32),
                   jax.ShapeDtypeStruct((kv_rows, aw), F32), jax.ShapeDtypeStruct((A_HEADS, ATTN_Q, ATTN_W), F32)),
        grid=(nblk, s // ATTN_Q),
        in_specs=[q_spec, acc_spec, pl.BlockSpec((kv_rows, LANES), lambda b, i: (0, nblk + b)), b_spec, q_spec],
        out_specs=(q_spec, acc_spec, acc_spec, b_spec), compiler_params=_cp("arbitrary", "arbitrary"),
    )(p1, kvp, kvp, band, dya)


def _rel_onehot():
    qi = jnp.arange(CHUNK)[:, None]
    ks = jnp.arange(A_BAND)[None, :]
    idx = (jnp.clip(ks - A_PAD - qi, -REL_CLIP, CHUNK - 1) + REL_CLIP).reshape(1, CHUNK * A_BAND)
    return (jnp.arange(REL_SIZE)[:, None] == idx).astype(F32)


def _gla_gate(lr, wa2, balpha):
    z = _dot(lr, wa2) + balpha
    la = (jnp.minimum(z, 0.0) - jnp.log(1.0 + jnp.exp(-jnp.abs(z)))) * (1.0 / GATE_TAU)
    row = lax.broadcasted_iota(jnp.int32, (CHUNK, CHUNK), 0)
    col = lax.broadcasted_iota(jnp.int32, (CHUNK, CHUNK), 1)
    cum = _dot((row >= col).astype(F32), la, precision=HIGHEST)
    return z, la, cum


def _gla_dims(p2):
    kd = p2.shape[1] // 6
    hk = kd // B_HEADS
    hv = 2 * hk
    return kd, hk, hv


def _gla_fwd(p2, lrp, wa2p, balpha, gnorm, stages=None):
    s = p2.shape[0]
    kd, hk, hv = _gla_dims(p2)
    nc = s // CHUNK
    qscale = hk ** -0.5

    def body(p_ref, lr_ref, wa_ref, ba_ref, gn_ref, yb_ref, st_ref, state):
        @pl.when(pl.program_id(0) == 0)
        def _():
            state[...] = jnp.zeros_like(state)

        _, _, cum = _gla_gate(lr_ref[...], wa_ref[...], ba_ref[...])
        last = cum[CHUNK - 1:CHUNK, :]
        e = jnp.exp(last - cum)
        dch = jnp.exp(last)
        gn = gn_ref[...]
        for hh in range(B_HEADS):
            ks = slice(hh * hk, (hh + 1) * hk)
            q = p_ref[:, hh * hk:(hh + 1) * hk].astype(F32)
            k = p_ref[:, kd + hh * hk:kd + (hh + 1) * hk].astype(F32)
            v = p_ref[:, 2 * kd + hh * hv:2 * kd + (hh + 1) * hv]
            rg = p_ref[:, 4 * kd + hh * hv:4 * kd + (hh + 1) * hv].astype(F32)
            kdec = (k * e[:, ks]).astype(BF16)
            st = state[hh] * dch[:, ks] + _dot(v, kdec, "tn")
            state[hh] = st
            st_ref[hh] = st
            o = _dot((q * qscale).astype(BF16), st.astype(BF16), "nt")
            rinv = lax.rsqrt(jnp.mean(o * o, axis=-1, keepdims=True) + RMS_EPS)
            yb_ref[:, hh * hv:(hh + 1) * hv] = ((o * rinv * gn) * (rg * _sigmoid(rg))).astype(BF16)

    return _pcall(
        body, stages, name="gla_fwd",
        out_shape=(jax.ShapeDtypeStruct((s, 2 * kd), BF16), jax.ShapeDtypeStruct((nc, B_HEADS, hv, hk), F32)),
        grid=(nc,),
        in_specs=[pl.BlockSpec((CHUNK, 6 * kd), lambda i: (i, 0)), pl.BlockSpec((CHUNK, LANES), lambda i: (i, 0)),
                  pl.BlockSpec((LANES, kd), lambda i: (0, 0)), pl.BlockSpec((1, kd), lambda i: (0, 0)),
                  pl.BlockSpec((1, hv), lambda i: (0, 0))],
        out_specs=(pl.BlockSpec((CHUNK, 2 * kd), lambda i: (i, 0)),
                   pl.BlockSpec((None, B_HEADS, hv, hk), lambda i: (i, 0, 0, 0))),
        scratch_shapes=[pltpu.VMEM((B_HEADS, hv, hk), F32)], compiler_params=_cp("arbitrary"),
    )(p2, lrp, wa2p, balpha, gnorm)


GLA_ROW_DBALPHA, GLA_ROW_DGNORM = 0, 1


def _gla_bwd(p2, lrp, wa2p, balpha, gnorm, states, dyb, stages=None):
    s = p2.shape[0]
    kd, hk, hv = _gla_dims(p2)
    nc = s // CHUNK
    qscale = hk ** -0.5

    def body(p_ref, lr_ref, wa_ref, ba_ref, gn_ref, st_ref, sp_ref, dy_ref, dp_ref, dz_ref, sm_ref, gcar):
        i = pl.program_id(0)

        @pl.when(i == 0)
        def _():
            gcar[...] = jnp.zeros_like(gcar)
            sm_ref[...] = jnp.zeros_like(sm_ref)

        has_prev = (i < nc - 1).astype(F32)
        z, _, cum = _gla_gate(lr_ref[...], wa_ref[...], ba_ref[...])
        last = cum[CHUNK - 1:CHUNK, :]
        e = jnp.exp(last - cum)
        dch = jnp.exp(last)
        sgn = _sigmoid(-z) * (1.0 / GATE_TAU)
        gn = gn_ref[...]
        row = lax.broadcasted_iota(jnp.int32, (CHUNK, CHUNK), 0)
        col = lax.broadcasted_iota(jnp.int32, (CHUNK, CHUNK), 1)
        tri_strict = (row > col).astype(F32)
        for hh in range(B_HEADS):
            ks = slice(hh * hk, (hh + 1) * hk)
            q = p_ref[:, hh * hk:(hh + 1) * hk].astype(F32)
            k = p_ref[:, kd + hh * hk:kd + (hh + 1) * hk].astype(F32)
            v = p_ref[:, 2 * kd + hh * hv:2 * kd + (hh + 1) * hv]
            rg = p_ref[:, 4 * kd + hh * hv:4 * kd + (hh + 1) * hv].astype(F32)
            kdecf = k * e[:, ks]
            kdec = kdecf.astype(BF16)
            st16 = st_ref[hh].astype(BF16)
            qs = (q * qscale).astype(BF16)
            o = _dot(qs, st16, "nt")
            rinv = lax.rsqrt(jnp.mean(o * o, axis=-1, keepdims=True) + RMS_EPS)
            dy = dy_ref[:, hh * hv:(hh + 1) * hv].astype(F32)
            sg = _sigmoid(rg)
            onorm = o * rinv
            drg = dy * (onorm * gn) * (sg * (1.0 + rg * (1.0 - sg)))
            dob = dy * (rg * sg)
            sm_ref[GLA_ROW_DGNORM:GLA_ROW_DGNORM + 1, 0:hv] += _colsum(dob * onorm)
            t = dob * gn
            do = rinv * (t - onorm * jnp.mean(t * onorm, axis=-1, keepdims=True))
            do16 = do.astype(BF16)
            dq = _dot(do16, st16) * qscale
            gt = _dot(do16, qs, "tn") + gcar[hh]
            gcar[hh] = gt * dch[:, ks]
            dd = _colsum(gt * sp_ref[hh]) * has_prev
            gt16 = gt.astype(BF16)
            dkdec = _dot(v, gt16)
            dv = _dot(kdec, gt16, "nt")
            dla = dd * dch[:, ks] + _dot(tri_strict, dkdec * kdecf, precision=HIGHEST)
            dzh = dla * sgn[:, ks]
            sm_ref[GLA_ROW_DBALPHA:GLA_ROW_DBALPHA + 1, hh * hk:(hh + 1) * hk] += _colsum(dzh)
            dz_ref[:, hh * hk:(hh + 1) * hk] = dzh.astype(BF16)
            dp_ref[:, hh * hk:(hh + 1) * hk] = dq.astype(BF16)
            dp_ref[:, kd + hh * hk:kd + (hh + 1) * hk] = (dkdec * e[:, ks]).astype(BF16)
            dp_ref[:, 2 * kd + hh * hv:2 * kd + (hh + 1) * hv] = dv.astype(BF16)
            dp_ref[:, 4 * kd + hh * hv:4 * kd + (hh + 1) * hv] = drg.astype(BF16)

    rev = lambda i: (nc - 1 - i, 0)
    return _pcall(
        body, stages, name="gla_bwd",
        out_shape=(jax.ShapeDtypeStruct((s, 6 * kd), BF16), jax.ShapeDtypeStruct((s, kd), BF16),
                   jax.ShapeDtypeStruct((SUBLANES, kd), F32)),
        grid=(nc,),
        in_specs=[pl.BlockSpec((CHUNK, 6 * kd), rev), pl.BlockSpec((CHUNK, LANES), rev),
                  pl.BlockSpec((LANES, kd), lambda i: (0, 0)), pl.BlockSpec((1, kd), lambda i: (0, 0)),
                  pl.BlockSpec((1, hv), lambda i: (0, 0)),
                  pl.BlockSpec((None, B_HEADS, hv, hk), lambda i: (nc - 1 - i, 0, 0, 0)),
                  pl.BlockSpec((None, B_HEADS, hv, hk), lambda i: (jnp.maximum(nc - 2 - i, 0), 0, 0, 0)),
                  pl.BlockSpec((CHUNK, 2 * kd), rev)],
        out_specs=(pl.BlockSpec((CHUNK, 6 * kd), rev), pl.BlockSpec((CHUNK, kd), rev),
                   pl.BlockSpec((SUBLANES, kd), lambda i: (0, 0))),
        scratch_shapes=[pltpu.VMEM((B_HEADS, hv, hk), F32)], compiler_params=_cp("arbitrary"),
    )(p2, lrp, wa2p, balpha, gnorm, states, states, dyb)


def _merge_fwd(ya, yb, wpa, wpb, g):
    s, ka = ya.shape
    kb = yb.shape[1]
    d = wpa.shape[1]
    tm, tn = _tile(s, (1024, 512)), _tile(d, (512,))

    def body(ya_ref, yb_ref, wa_ref, wb_ref, g_ref, m_ref, pab_ref):
        pa = _dot(ya_ref[...], wa_ref[...])
        pb = _dot(yb_ref[...], wb_ref[...])
        m_ref[...] = (_sigmoid(g_ref[0].astype(F32)) * pa + _sigmoid(g_ref[1].astype(F32)) * pb).astype(BF16)
        pab_ref[0] = pa.astype(BF16)
        pab_ref[1] = pb.astype(BF16)

    st = pl.BlockSpec((2, tm, tn), lambda i, j: (0, i, j))
    return pl.pallas_call(
        body, name="merge_fwd",
        out_shape=(jax.ShapeDtypeStruct((s, d), BF16), jax.ShapeDtypeStruct((2, s, d), BF16)),
        grid=(s // tm, d // tn),
        in_specs=[pl.BlockSpec((tm, ka), lambda i, j: (i, 0)), pl.BlockSpec((tm, kb), lambda i, j: (i, 0)),
                  pl.BlockSpec((ka, tn), lambda i, j: (0, j)), pl.BlockSpec((kb, tn), lambda i, j: (0, j)), st],
        out_specs=(pl.BlockSpec((tm, tn), lambda i, j: (i, j)), st),
        compiler_params=_cp("parallel", "parallel"),
    )(ya, yb, wpa, wpb, g)


def _merge_bwd(dm, wmo, g, pab, stages=None):
    s, d = dm.shape
    tm, tn = _tile(s, (1024, 512)), _tile(d, (512,))

    def body(dm_ref, w_ref, g_ref, pab_ref, dpab_ref, dg_ref):
        dmg = _dot(dm_ref[...], w_ref[...], "nt")
        for j in range(2):
            sg = _sigmoid(g_ref[j].astype(F32))
            dpab_ref[j] = (dmg * sg).astype(BF16)
            dg_ref[j] = (dmg * pab_ref[j].astype(F32) * (sg * (1.0 - sg))).astype(BF16)

    st = pl.BlockSpec((2, tm, tn), lambda i, j: (0, i, j))
    return _pcall(
        body, stages, name="merge_bwd",
        out_shape=(jax.ShapeDtypeStruct((2, s, d), BF16), jax.ShapeDtypeStruct((2, s, d), BF16)),
        grid=(s // tm, d // tn),
        in_specs=[pl.BlockSpec((tm, d), lambda i, j: (i, 0)), pl.BlockSpec((tn, d), lambda i, j: (j, 0)), st, st],
        out_specs=(st, st), compiler_params=_cp("arbitrary", "arbitrary"),
    )(dm, wmo, g, pab)


def _split_mix_in(wt):
    aw = A_HEADS * A_HEAD_DIM
    d = wt.shape[1]
    o2 = 3 * aw + 6 * (d // 4)
    o3 = o2 + GATE_RANK
    return jnp.pad(wt[o2:o3], ((0, LANES - GATE_RANK), (0, 0))), wt[o3:]


MIX_TILE = 1024


def _mix_in_weights(plan):
    wt = plan.weight("w_mix_in")
    return (wt,) + plan.memo("mix_in_weights", lambda: _split_mix_in(wt))


def _hosted_mm(plan):
    return lambda name, *a, **k: plan.host(name, lambda st: _mm(name, *a, stages=st, **k))


def _mix_forward(u2, plan, small):
    s, d = u2.shape
    wt, wt_lr, wt_g = _mix_in_weights(plan)
    bias, wa2p, balpha, gnorm = small
    mm = _hosted_mm(plan)
    aw = A_HEADS * A_HEAD_DIM
    nb = 6 * (d // 4)
    tm, tn = _tile(s, (1024,)), MIX_TILE
    assert (3 * aw) % tn == 0
    ob = 3 * aw // tn
    p1 = mm("mix_in_a", "nt", u2, wt, (s, 3 * aw, d), tm=tm, tn=tn, tk=d, out_dtype=BF16)
    p2 = mm("mix_in_b", "nt", u2, wt, (s, nb, d), tm=tm, tn=tn, tk=d, out_dtype=BF16,
            b_spec=pl.BlockSpec((tn, d), lambda i, j, kk: (ob + j, 0)))
    lrp = mm("mix_in_lr", "nt", u2, wt_lr, (s, LANES, d), tm=tm, tn=LANES, tk=d, out_dtype=BF16)
    nbg = d // tn
    g = mm("mix_in_g", "nt", u2, wt_g, (s, 2 * d, d), tm=tm, tn=tn, tk=d, out_dtype=BF16, out_shape=(2, s, d),
           o_spec=pl.BlockSpec((None, tm, tn), lambda i, j, kk: (j // nbg, i, j % nbg)))
    kvp = jnp.pad(p1[:, aw:], ((A_PAD, 0), (0, 0)))
    ya = plan.host("attn_fwd", lambda st: _attn_fwd(p1, kvp, bias, st))
    yb, states = plan.host("gla_fwd", lambda st: _gla_fwd(p2, lrp, wa2p, balpha, gnorm, st))
    merged, pab = _merge_fwd(ya, yb, plan.weight("w_proj_a"), plan.weight("w_proj_b"), g)
    m = mm("mix_out", "nn", merged, plan.weight("w_mix_out"), (s, d, d), tm=tm, tn=tn, tk=d)
    return m, (p1, kvp, p2, lrp, states, ya, yb, g, pab, merged)


def _mix_backward(dm, u2, saved, plan, small):
    s, d = u2.shape
    wt, wt_lr, wt_g = _mix_in_weights(plan)
    wpa, wpb, wmo = plan.weight("w_proj_a"), plan.weight("w_proj_b"), plan.weight("w_mix_out")
    bias, wa2p, balpha, gnorm = small
    p1, kvp, p2, lrp, states, ya, yb, g, pab, merged = saved
    mm = _hosted_mm(plan)
    aw = A_HEADS * A_HEAD_DIM
    kd = d // 4
    t = MIX_TILE
    tm = _tile(s, (1024,))
    tks = _tile(s, LONG_K)

    plan.grad("w_mix_out", mm("mix_dw_out", "tn", merged, dm, (d, d, s), tm=t, tn=t, tk=tks))
    dpab, dg = plan.host("merge_bwd", lambda st: _merge_bwd(dm, wmo, g, pab, st))
    sel = lambda j: pl.BlockSpec((None, tm, d), lambda i, jj, kk: (j, i, 0))
    dya = mm("mix_dya", "nt", dpab, wpa, (s, aw, d), tm=tm, tn=t, tk=d, out_dtype=BF16, a_spec=sel(0))
    dyb = mm("mix_dyb", "nt", dpab, wpb, (s, 2 * kd, d), tm=tm, tn=t, tk=d, out_dtype=BF16, a_spec=sel(1))
    selk = lambda j: pl.BlockSpec((None, tks, t), lambda i, jj, kk: (j, kk, jj))
    plan.grad("w_proj_a", mm("mix_dwpa", "tn", ya, dpab, (aw, d, s), tm=t, tn=t, tk=tks, b_spec=selk(0)))
    plan.grad("w_proj_b", mm("mix_dwpb", "tn", yb, dpab, (2 * kd, d, s), tm=t, tn=t, tk=tks, b_spec=selk(1)))

    dq, dkp, dvp, dbias = plan.host("attn_bwd", lambda st: _attn_bwd(p1, kvp, bias, dya, st))
    dp1 = jnp.concatenate([dq, dkp[A_PAD:].astype(BF16), dvp[A_PAD:].astype(BF16)], axis=1)
    dp2, dz, gsm = plan.host("gla_bwd", lambda st: _gla_bwd(p2, lrp, wa2p, balpha, gnorm, states, dyb, st))
    dlrp = mm("gla_dlr", "nt", dz, wa2p, (s, LANES, kd), tm=tm, tn=LANES, tk=kd, out_dtype=BF16)
    dwa2p = mm("gla_dwa2", "tn", lrp, dz, (LANES, kd, s), tm=LANES, tn=kd, tk=tks)

    tka = 3 * aw
    assert 6 * kd == tka
    du = mm("mix_du_a", "nn", dp1, wt, (s, d, tka), tm=tm, tn=t, tk=tka)
    du = mm("mix_du_b", "nn", dp2, wt, (s, d, tka), tm=tm, tn=t, tk=tka, add=du,
            b_spec=pl.BlockSpec((tka, t), lambda i, j, kk: (1 + kk, j)))
    du = mm("mix_du_lr", "nn", dlrp, wt_lr, (s, d, LANES), tm=tm, tn=t, tk=LANES, add=du)
    du = mm("mix_du_g", "nn", dg, wt_g, (s, d, 2 * d), tm=tm, tn=t, tk=d, add=du,
            a_spec=pl.BlockSpec((None, tm, d), lambda i, j, kk: (kk, i, 0)))
    nkg = d // t
    dw1 = mm("mix_dw_a", "tn", dp1, u2, (3 * aw, d, s), tm=t, tn=t, tk=tks)
    dw2 = mm("mix_dw_b", "tn", dp2, u2, (6 * kd, d, s), tm=t, tn=t, tk=tks)
    dwlr = mm("mix_dw_lr", "tn", dlrp, u2, (LANES, d, s), tm=LANES, tn=t, tk=tks)
    dwg = mm("mix_dw_g", "tn", dg, u2, (2 * d, d, s), tm=t, tn=t, tk=tks,
             a_spec=pl.BlockSpec((None, tks, t), lambda i, j, kk: (i // nkg, kk, i % nkg)))
    plan.grad("w_mix_in", jnp.concatenate([dw1, dw2, dwlr[:GATE_RANK], dwg], axis=0))
    return du, (dbias, dwa2p[:GATE_RANK], gsm)


def _device_step(x, target, mod, small, plan):
    s, d = x.shape
    row = lambda i: mod[i:i + 1]
    sh1, sc1, g1, sh2, sc2, g2, sh3, sc3, g3 = (row(i) for i in range(N_MOD))

    onehot = _rel_onehot()
    bias = _mm("rel_bias_expand", "nn", small["rel_bias"], onehot, (A_HEADS, CHUNK * A_BAND, REL_SIZE),
               tm=A_HEADS, tn=4608, tk=REL_SIZE, precision=HIGHEST).reshape(A_HEADS, CHUNK, A_BAND)
    bias = _band_bias(bias)
    wa2p = jnp.pad(small["w_alpha2"], ((0, LANES - GATE_RANK), (0, 0))).astype(BF16)
    mix_small = (bias, wa2p, small["b_alpha"], small["gla_norm_g"])

    u1 = _modulate("mod1", x, sh1, sc1)
    f1, sv1 = _ffn_forward("ffn1", u1, plan)
    h1, u2 = _resid_ln_fwd("ln1_fwd", x, f1, g1, small["ln1_g"], small["ln1_b"], sh2, sc2, 0.5)
    m, svm = _mix_forward(u2, plan, mix_small)
    h2, u3 = _resid_ln_fwd("ln2_fwd", h1, m, g2, small["ln2_g"], small["ln2_b"], sh3, sc3, 1.0)
    f2, sv2 = _ffn_forward("ffn2", u3, plan)

    dr3, df2, acc3 = _final_ln_loss_bwd("ln3_loss_bwd", h2, f2, target, g3, small["ln3_g"], small["ln3_b"], 0.5)
    du3 = _ffn_backward("ffn2", df2, u3, sv2, plan, in_first=False)
    dr2, dmx, acc2 = _resid_ln_bwd("ln2_bwd", du3, dr3, h1, m, sc3, g2, small["ln2_g"], small["ln2_b"], 1.0)
    du2, (dbias, dwa2, gsm) = _mix_backward(dmx, u2, svm, plan, mix_small)
    dr1, df1, acc1 = _resid_ln_bwd("ln1_bwd", du2, dr2, x, f1, sc2, g1, small["ln1_g"], small["ln1_b"], 0.5)
    du1 = _ffn_backward("ffn1", df1, u1, sv1, plan, in_first=True)
    grad_x, acc0 = _input_grad("input_grad", du1, dr1, x, sc1)

    drel = _mm("rel_bias_grad", "nt", _band_bias_grad(dbias).reshape(A_HEADS, CHUNK * A_BAND), onehot,
               (A_HEADS, REL_SIZE, CHUNK * A_BAND), tm=A_HEADS, tn=REL_SIZE, tk=4608, precision=HIGHEST)
    loss = jnp.sum(acc3[ROW_LOSS])
    dmod = jnp.stack([acc0[ROW_DSH], acc0[ROW_DSC], acc1[ROW_DGATE], acc1[ROW_DSH], acc1[ROW_DSC], acc2[ROW_DGATE],
                      acc2[ROW_DSH], acc2[ROW_DSC], acc3[ROW_DGATE]])
    kd = d // 4
    small_grads = dict(ln1_g=acc1[ROW_DLN_G], ln1_b=acc1[ROW_DLN_B], ln2_g=acc2[ROW_DLN_G], ln2_b=acc2[ROW_DLN_B],
                       ln3_g=acc3[ROW_DLN_G], ln3_b=acc3[ROW_DLN_B], b_alpha=gsm[GLA_ROW_DBALPHA],
                       gla_norm_g=gsm[GLA_ROW_DGNORM, :kd // B_HEADS * 2], rel_bias=drel, w_alpha2=dwa2)
    return loss, grad_x, small_grads, dmod


HBM_SPEC = pl.BlockSpec(memory_space=pl.ANY)


def _mesh_pos():
    return lax.axis_index("x"), lax.axis_index("y"), lax.axis_index("c")


def _other_chips(x, y):
    return [(1 - x, y), (x, 1 - y), (1 - x, 1 - y)]


def _remote(src, dst, send_sem, recv_sem, to):
    return pltpu.make_async_remote_copy(src_ref=src, dst_ref=dst, send_sem=send_sem, recv_sem=recv_sem,
                                        device_id=to, device_id_type=MESH)


def _allgather_rows(name, v):
    m_per, n = v.shape

    def body(x_ref, out_ref, send_sems, recv_sems, local_sem):
        x, y, c = _mesh_pos()
        me, sibling = (x, y, c), (x, y, 1 - c)
        chips = _other_chips(x, y)

        def rows(px, py, pc):
            return out_ref.at[pl.ds((4 * px + 2 * py + pc) * m_per, m_per), :]

        def copy(k, block, to, src=None):
            return _remote(rows(*block) if src is None else src, rows(*block), send_sems.at[k], recv_sems.at[k], to)

        mine = pltpu.make_async_copy(x_ref, rows(*me), local_sem)
        mine.start()
        first = [copy(0, me, sibling, src=x_ref)]
        first += [copy(1 + j, me, (*chip, c), src=x_ref) for j, chip in enumerate(chips)]
        for cp in first:
            cp.start()
        passed = [copy(4 + j, (*chip, c), sibling) for j, chip in enumerate(chips)]
        for j, chip in enumerate(chips):
            copy(1 + j, (*chip, c), me).wait_recv()
            passed[j].start()
        copy(0, sibling, me).wait_recv()
        for j, chip in enumerate(chips):
            copy(4 + j, (*chip, 1 - c), me).wait_recv()
        for cp in first + passed:
            cp.wait_send()
        mine.wait()

    return pl.pallas_call(
        body, name=name, out_shape=jax.ShapeDtypeStruct((N_DEV * m_per, n), v.dtype),
        in_specs=[pl.BlockSpec(memory_space=pltpu.VMEM)], out_specs=pl.BlockSpec(memory_space=pltpu.VMEM),
        scratch_shapes=[pltpu.SemaphoreType.DMA((7,)), pltpu.SemaphoreType.DMA((7,)), pltpu.SemaphoreType.DMA],
    )(v)


def _allgather_weights(shards):
    n = len(shards)

    def body(*refs):
        ins, outs = refs[:n], refs[n:2 * n]
        send_sems, recv_sems = refs[2 * n:]
        x, y, c = _mesh_pos()
        sibling = (x, y, 1 - c)
        chips = _other_chips(x, y)
        j0 = 2 * x + y

        def half(ref, w, j, hc):
            hr = shards[w].shape[0] // 2
            return ref.at[j, pl.ds(hc * hr, hr), :]

        sends = []
        for w in range(n):
            hr = shards[w].shape[0] // 2
            for r, chip in enumerate(chips):
                cp = _remote(ins[w].at[pl.ds(c * hr, hr), :], half(outs[w], w, j0, c), send_sems.at[w, r],
                             recv_sems.at[w, r], (*chip, c))
                cp.start()
                sends.append(cp)
        for w in range(n):
            for r, chip in enumerate(chips):
                jr = 2 * chip[0] + chip[1]
                landed = half(outs[w], w, jr, c)
                _remote(landed, landed, send_sems.at[w, r], recv_sems.at[w, r], (*chip, c)).wait_recv()
                fw = _remote(landed, landed, send_sems.at[w, 3 + r], recv_sems.at[w, 3 + r], sibling)
                fw.start()
                sends.append(fw)
        for w in range(n):
            for r, chip in enumerate(chips):
                jr = 2 * chip[0] + chip[1]
                got = half(outs[w], w, jr, 1 - c)
                _remote(got, got, send_sems.at[w, 3 + r], recv_sems.at[w, 3 + r], sibling).wait_recv()
        for cp in sends:
            cp.wait_send()

    return pl.pallas_call(
        body, name="allgather_weights",
        out_shape=[jax.ShapeDtypeStruct((N_CHIPS,) + sh.shape, sh.dtype) for sh in shards],
        in_specs=[HBM_SPEC] * n, out_specs=[HBM_SPEC] * n,
        scratch_shapes=[pltpu.SemaphoreType.DMA((n, 6)), pltpu.SemaphoreType.DMA((n, 6))],
    )(*shards)


def _half(ref, hc, col, *lead):
    rows, cols = ref.shape[-2:]
    if col:
        return ref.at[(*lead, slice(None), pl.ds(hc * (cols // 2), cols // 2))]
    return ref.at[(*lead, pl.ds(hc * (rows // 2), rows // 2), slice(None))]


def _half_shape(shape, col):
    return shape[:-2] + ((shape[-2], shape[-1] // 2) if col else (shape[-2] // 2, shape[-1]))


def _stage_gather_ici(shards, cols):
    n = len(shards)

    def copies(ins, outs, send, recv):
        x, y, c = _mesh_pos()
        j0 = 2 * x + y
        for w in range(n):
            for r, chip in enumerate(_other_chips(x, y)):
                jr = 2 * chip[0] + chip[1]
                mine = _remote(_half(ins[w], c, cols[w]), _half(outs[w], c, cols[w], j0),
                               send.at[3 * w + r], recv.at[3 * w + r], (*chip, c))
                landed = _half(outs[w], c, cols[w], jr)
                yield mine, _remote(landed, landed, send.at[3 * w + r], recv.at[3 * w + r], (*chip, c))

    def start(*refs):
        for mine, _ in copies(*refs):
            mine.start()

    def finish(*refs):
        pairs = list(copies(*refs))
        for _, theirs in pairs:
            theirs.wait_recv()
        for mine, _ in pairs:
            mine.wait_send()

    outs = [jax.ShapeDtypeStruct((N_CHIPS,) + sh.shape, sh.dtype) for sh in shards]
    return _Stage(shards, outs, 3 * n, start, finish)


def _stage_gather_d2d(partial, cols):
    n = len(partial)

    def copies(ins, outs, send, recv):
        x, y, c = _mesh_pos()
        for w in range(n):
            for r, chip in enumerate(_other_chips(x, y)):
                jr = 2 * chip[0] + chip[1]
                mine = _remote(_half(ins[w], c, cols[w], jr), _half(outs[w], c, cols[w], jr), send.at[3 * w + r],
                               recv.at[3 * w + r], (x, y, 1 - c))
                got = _half(outs[w], 1 - c, cols[w], jr)
                yield mine, _remote(got, got, send.at[3 * w + r], recv.at[3 * w + r], (x, y, 1 - c))

    def start(*refs):
        for mine, _ in copies(*refs):
            mine.start()

    def finish(*refs):
        pairs = list(copies(*refs))
        for _, theirs in pairs:
            theirs.wait_recv()
        for mine, _ in pairs:
            mine.wait_send()

    outs = [jax.ShapeDtypeStruct(p.shape, p.dtype) for p in partial]
    return _Stage(partial, outs, 3 * n, start, finish, aliases={w: w for w in range(n)})


def _stage_exchange_halves(grads, cols):
    n = len(grads)

    def copies(ins, outs, send, recv):
        x, y, c = _mesh_pos()
        for w in range(n):
            yield _remote(_half(ins[w], 1 - c, cols[w], slice(None)), outs[w], send.at[w], recv.at[w], (x, y, 1 - c))

    def start(*refs):
        for cp in copies(*refs):
            cp.start()

    def finish(*refs):
        cps = list(copies(*refs))
        for cp in cps:
            cp.wait_recv()
        for cp in cps:
            cp.wait_send()

    outs = [jax.ShapeDtypeStruct(_half_shape(g.shape, col), g.dtype) for g, col in zip(grads, cols)]
    return _Stage(grads, outs, n, start, finish)


def _stage_scatter(parts):
    n = len(parts)

    def copies(ins, outs, send, recv):
        x, y, c = _mesh_pos()
        for w in range(n):
            for r, chip in enumerate(_other_chips(x, y)):
                jr = 2 * chip[0] + chip[1]
                yield _remote(ins[w].at[jr], outs[w].at[r], send.at[3 * w + r], recv.at[3 * w + r], (*chip, c))

    def start(*refs):
        for cp in copies(*refs):
            cp.start()

    def finish(*refs):
        cps = list(copies(*refs))
        for cp in cps:
            cp.wait_recv()
        for cp in cps:
            cp.wait_send()

    outs = [jax.ShapeDtypeStruct((3,) + p.shape[1:], p.dtype) for p in parts]
    return _Stage(parts, outs, 3 * n, start, finish)


def _stage_share(fulls, cols):
    n = len(fulls)

    def copies(ins, outs, send, recv):
        x, y, c = _mesh_pos()
        for w in range(n):
            theirs = _half(outs[w], 1 - c, cols[w])
            yield (_remote(_half(ins[w], c, cols[w]), _half(outs[w], c, cols[w]), send.at[w], recv.at[w], (x, y, 1 - c)),
                   _remote(theirs, theirs, send.at[w], recv.at[w], (x, y, 1 - c)))

    def start(*refs):
        for mine, _ in copies(*refs):
            mine.start()

    def finish(*refs):
        pairs = list(copies(*refs))
        for _, theirs in pairs:
            theirs.wait_recv()
        for mine, _ in pairs:
            mine.wait_send()

    outs = [jax.ShapeDtypeStruct(h.shape, h.dtype) for h in fulls]
    return _Stage(fulls, outs, n, start, finish, aliases={w: w for w in range(n)})


def _run_stages(name, stages):
    return _pcall(None, stages, name=name, out_shape=[], in_specs=[], out_specs=[])()[1]


TILE_BYTES = 2 * 1024 * 1024


def _row_tile(rows, cols, itemsize=4):
    for t in (1024, 512, 256, 128, 64, 32, 16, 8):
        if rows % t == 0 and t * cols * itemsize <= TILE_BYTES:
            return t
    return rows


def _col_tile(rows, cols, itemsize=4):
    for t in (2048, 1024, 512, 256, 128):
        if cols % t == 0 and t * rows * itemsize <= TILE_BYTES:
            return t
    return cols


def _tiling(rows, cols, col):
    if col:
        tc = _col_tile(rows, cols)
        return (rows, tc), cols // tc
    tr = _row_tile(rows, cols)
    return (tr, cols), rows // tr


def _strip(col, i):
    return (0, i) if col else (i, 0)


def _pair_sum(name, g, recv, core, col):
    blk, nb = _tiling(*recv.shape[1:], col)

    def body(c_ref, g_ref, r_ref, o_ref):
        o_ref[...] = (g_ref[...] + r_ref[...]).astype(BF16)

    grid_spec = pltpu.PrefetchScalarGridSpec(
        num_scalar_prefetch=1, grid=(N_CHIPS, nb),
        in_specs=[pl.BlockSpec((None,) + blk, lambda j, i, cr: (j,) + _strip(col, cr[0] * nb + i)),
                  pl.BlockSpec((None,) + blk, lambda j, i, cr: (j,) + _strip(col, i))],
        out_specs=pl.BlockSpec((None,) + blk, lambda j, i, cr: (j,) + _strip(col, i)))
    return pl.pallas_call(body, name=name, out_shape=jax.ShapeDtypeStruct(recv.shape, BF16), grid_spec=grid_spec,
                          compiler_params=_cp("parallel", "parallel"))(core, g, recv)


def _quad_sum(name, own, landed, chip_core, col):
    rows, cols = landed.shape[1:]
    blk, nb = _tiling(rows, cols, col)
    full = (rows, 2 * cols) if col else (2 * rows, cols)

    def body(cc_ref, own_ref, l_ref, o_ref):
        o_ref[...] = ((own_ref[...].astype(F32) + l_ref[0].astype(F32)) + l_ref[1].astype(F32)) + l_ref[2].astype(F32)

    grid_spec = pltpu.PrefetchScalarGridSpec(
        num_scalar_prefetch=1, grid=(nb,),
        in_specs=[pl.BlockSpec((None,) + blk, lambda i, cc: (cc[0],) + _strip(col, i)),
                  pl.BlockSpec((3,) + blk, lambda i, cc: (0,) + _strip(col, i))],
        out_specs=pl.BlockSpec(blk, lambda i, cc: _strip(col, cc[1] * nb + i)))
    return pl.pallas_call(body, name=name, out_shape=jax.ShapeDtypeStruct(full, F32), grid_spec=grid_spec,
                          compiler_params=_cp("arbitrary"))(chip_core, own, landed)


def _device_sum(name, gathered):
    def body(g_ref, o_ref):
        total = g_ref[0]
        for k in range(1, N_DEV):
            total = total + g_ref[k]
        o_ref[...] = total

    return pl.pallas_call(body, name=name, out_shape=jax.ShapeDtypeStruct(gathered.shape[1:], F32))(gathered)


def _adamw(name, w, g, m, v):
    rows, cols = w.shape
    col = rows % SUBLANES != 0
    blk, nb = _tiling(rows, cols, col)
    bc1 = 1.0 - ADAM_B1 ** ADAM_STEP
    bc2 = 1.0 - ADAM_B2 ** ADAM_STEP

    def body(w_ref, g_ref, m_ref, v_ref, d_ref, mo_ref, vo_ref):
        gv = g_ref[...]
        mn = ADAM_B1 * m_ref[...] + (1.0 - ADAM_B1) * gv
        vn = ADAM_B2 * v_ref[...] + (1.0 - ADAM_B2) * (gv * gv)
        mo_ref[...] = mn
        vo_ref[...] = vn
        d_ref[...] = -ADAM_LR * ((mn / bc1) / (jnp.sqrt(vn / bc2) + ADAM_EPS) + ADAM_WD * w_ref[...])

    spec = pl.BlockSpec(blk, lambda i: _strip(col, i))
    return pl.pallas_call(
        body, name=name, out_shape=[jax.ShapeDtypeStruct((rows, cols), F32)] * 3, grid=(nb,),
        in_specs=[spec] * 4, out_specs=[spec] * 3, compiler_params=_cp("parallel"),
    )(w, g, m, v)


WEIGHTS = ["w_ada", "b_ada", "ffn1_w_in", "ffn1_w_out", "ln1_g", "ln1_b", "w_mix_in", "rel_bias", "w_alpha2",
           "b_alpha", "gla_norm_g", "w_proj_a", "w_proj_b", "w_mix_out", "ln2_g", "ln2_b", "ffn2_w_in", "ffn2_w_out",
           "ln3_g", "ln3_b"]
BIG = {"ffn1_w_in": True, "ffn1_w_out": False, "w_mix_in": False, "w_proj_a": True, "w_proj_b": True,
       "w_mix_out": False, "ffn2_w_in": True, "ffn2_w_out": False}
TRANSPOSED = ("w_mix_in",)
STACKED = ("ffn1_w_in", "ffn2_w_in")
GROUP_FFN1 = ("ffn1_w_in", "ffn1_w_out")
GROUP_PROJ = ("w_proj_a", "w_proj_b", "w_mix_out")
SMALL = ["ln1_g", "ln1_b", "ln2_g", "ln2_b", "ln3_g", "ln3_b", "b_alpha", "gla_norm_g", "rel_bias", "w_alpha2"]


def _pad_rows(vec, rows=SUBLANES):
    per = -(-vec.shape[0] // (rows * LANES)) * LANES
    return jnp.pad(vec, (0, rows * per - vec.shape[0])).reshape(rows, per)


def _silu(v):
    return v * _sigmoid(v)


class _MeshPlan:
    def __init__(self, shards, chip, core):
        self.shards, self.chip = shards, chip
        self.core1 = core.astype(jnp.int32).reshape(1)
        self.chip_core = jnp.stack([chip, core]).astype(jnp.int32)
        self.partial, self.full, self.local, self.pair, self.half, self.final, self.memos = {}, {}, {}, {}, {}, {}, {}
        ici, d2d, x1, x2, x3 = self.gather_ici, self.gather_d2d, self.exchange, self.scatter, self.share
        mix_in, in1, out1, in2, out2 = ("w_mix_in",), ("ffn1_w_in",), ("ffn1_w_out",), ("ffn2_w_in",), ("ffn2_w_out",)
        self.schedule = {
            "ffn1_in_fwd": [ici(mix_in)], "ffn1_out_fwd": [d2d(mix_in), ici(out2)],
            "mix_in_a": [d2d(out2)], "mix_in_g": [ici(GROUP_PROJ)],
            "attn_fwd": [ici(in2), d2d(GROUP_PROJ)], "gla_fwd": [d2d(in2)],
            "ffn2_dw_in": [x1(out2)], "ffn2_du": [x2(out2), x1(in2)], "mix_dw_out": [x3(out2)],
            "attn_bwd": [x2(in2)], "gla_bwd": [x3(in2), x1(GROUP_PROJ)],
            "mix_du_g": [x2(GROUP_PROJ)], "mix_dw_g": [x3(GROUP_PROJ)],
            "ffn1_out_bwd": [x1(mix_in)], "ffn1_dw_in": [x2(mix_in)], "ffn1_dw_out": [x3(mix_in), x1(in1)],
            "ffn1_du": [x2(in1), x1(out1)],
        }

    def weight(self, k):
        return self.full[k]

    def grad(self, k, g):
        r, cc = self.shards[k].shape
        if k not in STACKED:
            g = g.reshape(r, N_CHIPS, cc).transpose(1, 0, 2) if BIG[k] else g.reshape(N_CHIPS, r, cc)
        self.local[k] = g

    def memo(self, key, make):
        if key not in self.memos:
            self.memos[key] = make()
        return self.memos[key]

    def host(self, name, call):
        builders = self.schedule.get(name)
        if not builders:
            return call(None)
        built = [b() for b in builders]
        main, comm = call([st for st, _ in built])
        for (_, post), res in zip(built, comm):
            post(res)
        return main

    def run(self, name, builders):
        built = [b() for b in builders]
        for (_, post), res in zip(built, _run_stages(name, [st for st, _ in built])):
            post(res)

    def set_gathered(self, names, gathered):
        for k, g in zip(names, gathered):
            _, r, cc = g.shape
            g = lax.dynamic_update_slice(g, self.shards[k][None], (self.chip, 0, 0))
            if k not in STACKED:
                g = g.transpose(1, 0, 2).reshape(r, N_CHIPS * cc) if BIG[k] else g.reshape(N_CHIPS * r, cc)
            self.full[k] = g

    @staticmethod
    def cols(names):
        return [k in TRANSPOSED for k in names]

    def gather_ici(self, names):
        def post(res):
            self.partial.update(zip(names, res))
        return lambda: (_stage_gather_ici([self.shards[k] for k in names], self.cols(names)), post)

    def gather_d2d(self, names):
        return lambda: (_stage_gather_d2d([self.partial[k] for k in names], self.cols(names)),
                        lambda res: self.set_gathered(names, res))

    def exchange(self, names):
        def post(res):
            for k, r in zip(names, res):
                self.pair[k] = _pair_sum(f"pair_sum_{k}", self.local[k], r, self.core1, k in TRANSPOSED)
        return lambda: (_stage_exchange_halves([self.local[k] for k in names], self.cols(names)), post)

    def scatter(self, names):
        def post(res):
            for k, landed in zip(names, res):
                self.half[k] = _quad_sum(f"quad_sum_{k}", self.pair[k], landed, self.chip_core, k in TRANSPOSED)
        return lambda: (_stage_scatter([self.pair[k] for k in names]), post)

    def share(self, names):
        def post(res):
            self.final.update(zip(names, res))
        return lambda: (_stage_share([self.half[k] for k in names], self.cols(names)), post)


def _step(args):
    x_pos, y_pos, c_pos = _mesh_pos()
    chip = 2 * x_pos + y_pos
    dev = 4 * x_pos + 2 * y_pos + c_pos
    take = lambda name, k: args[name][0].T if k in TRANSPOSED else args[name][0]
    w = {k: take(k, k) for k in WEIGHTS}
    mom = {k: take("m_" + k, k) for k in WEIGHTS}
    vel = {k: take("v_" + k, k) for k in WEIGHTS}
    x = args["x"][0]
    target = args["loss_target"][0]
    s, d = x.shape
    kd = d // 4
    rel_sh = w["rel_bias"].shape[1]
    wa2_sh = w["w_alpha2"].shape[1]
    ada_sh = w["w_ada"].shape[1]

    n_rel, n_wa2 = A_HEADS * rel_sh, GATE_RANK * wa2_sh
    packed = _pad_rows(jnp.concatenate([args["c"].reshape(-1), w["rel_bias"].reshape(-1), w["w_alpha2"].reshape(-1)]))
    got = _allgather_rows("gather_small_inputs", packed).reshape(N_DEV, -1)
    c_all = got[:, :d]
    per_chip = got[0::2]
    rel_bias = per_chip[:, d:d + n_rel].reshape(N_CHIPS, A_HEADS, rel_sh).transpose(1, 0, 2).reshape(A_HEADS, -1)
    w_alpha2 = per_chip[:, d + n_rel:d + n_rel + n_wa2].reshape(N_CHIPS, GATE_RANK, wa2_sh).transpose(1, 0, 2)
    w_alpha2 = w_alpha2.reshape(GATE_RANK, -1)

    b_shard = lax.dynamic_slice(w["b_ada"], (chip * ada_sh,), (ada_sh,))
    mod_shard = _mm("ada_fwd", "nn", c_all, w["w_ada"], (N_DEV, ada_sh, d), tm=N_DEV, tn=_tile(ada_sh, (512, 128)),
                    tk=d, precision=HIGHEST, a_fn=_silu, add=jnp.broadcast_to(b_shard[None], (N_DEV, ada_sh)))
    mod_all = _allgather_rows("gather_mod", mod_shard).reshape(N_DEV, N_DEV, ada_sh)[0::2]
    mod_all = mod_all.transpose(1, 0, 2).reshape(N_DEV, N_MOD * d)
    mod = lax.dynamic_index_in_dim(mod_all, dev, 0, keepdims=False).reshape(N_MOD, d)

    names = list(BIG)
    plan = _MeshPlan({k: w[k].astype(BF16) for k in names}, chip, c_pos)
    plan.set_gathered(GROUP_FFN1, _allgather_weights([plan.shards[k] for k in GROUP_FFN1]))

    small = dict(rel_bias=rel_bias, w_alpha2=w_alpha2, b_alpha=w["b_alpha"][None], gla_norm_g=w["gla_norm_g"][None])
    for k in ("ln1_g", "ln1_b", "ln2_g", "ln2_b", "ln3_g", "ln3_b"):
        small[k] = w[k][None]
    loss_local, grad_x, small_grads, dmod = _device_step(x, target, mod, small, plan)
    loss = lax.psum(loss_local, ("x", "y", "c"))
    plan.run("grad_tail_scatter", [plan.scatter(GROUP_FFN1[1:]), plan.share(GROUP_FFN1[:1])])
    plan.run("grad_tail_share", [plan.share(GROUP_FFN1[1:])])

    flat = jnp.concatenate([small_grads[k].reshape(-1) for k in SMALL] + [dmod.reshape(-1)])
    n_small = flat.shape[0] - N_MOD * d
    packed = _pad_rows(flat)
    all_small = _allgather_rows("gather_small_grads", packed).reshape(N_DEV, SUBLANES, -1)
    summed = _device_sum("small_grad_sum", all_small).reshape(-1)
    dmod_all = all_small.reshape(N_DEV, -1)[:, n_small:n_small + N_MOD * d]
    dmod_shard = lax.dynamic_slice(dmod_all, (0, chip * ada_sh), (N_DEV, ada_sh))
    grads = {"b_ada": summed[n_small:n_small + N_MOD * d]}
    off = 0
    for k in SMALL:
        size = small_grads[k].size
        grads[k] = summed[off:off + size].reshape(small_grads[k].shape)
        off += size
    grads["rel_bias"] = lax.dynamic_slice(grads["rel_bias"], (0, chip * rel_sh), (A_HEADS, rel_sh))
    grads["w_alpha2"] = lax.dynamic_slice(grads["w_alpha2"], (0, chip * wa2_sh), (GATE_RANK, wa2_sh))
    grads["w_ada"] = _mm("ada_bwd", "nn", jnp.pad(c_all.T, ((0, 0), (0, LANES - N_DEV))),
                         jnp.pad(dmod_shard, ((0, LANES - N_DEV), (0, 0))), (d, ada_sh, LANES), tm=_tile(d, (1024,)),
                         tn=_tile(ada_sh, (512, 128)), tk=LANES, precision=HIGHEST, a_fn=_silu)

    grads.update(plan.final)

    delta, new_m, new_v = {}, {}, {}
    for k in ["w_ada"] + names:
        delta[k], new_m[k], new_v[k] = _adamw(f"adamw_{k}", w[k], grads[k], mom[k], vel[k])
    tiny = ["b_ada"] + SMALL
    pack = lambda src: _pad_rows(jnp.concatenate([src[k].reshape(-1) for k in tiny]), rows=1).reshape(-1, LANES)
    outs = _adamw("adamw_small", pack(w), pack(grads), pack(mom), pack(vel))
    off = 0
    for k in tiny:
        size = w[k].size
        for dst, src in zip((delta, new_m, new_v), outs):
            dst[k] = src.reshape(-1)[off:off + size].reshape(w[k].shape)
        off += size

    give = lambda src: [src[k].T[None] if k in TRANSPOSED else src[k][None] for k in WEIGHTS]
    return (loss, grad_x[None], *give(grads), *give(delta), *give(new_m), *give(new_v))


def kernel(x, c, w_ada, b_ada, ffn1_w_in, ffn1_w_out, ln1_g, ln1_b, w_mix_in, rel_bias, w_alpha2, b_alpha, gla_norm_g, w_proj_a, w_proj_b, w_mix_out, ln2_g, ln2_b, ffn2_w_in, ffn2_w_out, ln3_g, ln3_b, loss_target, m_w_ada, m_b_ada, m_ffn1_w_in, m_ffn1_w_out, m_ln1_g, m_ln1_b, m_w_mix_in, m_rel_bias, m_w_alpha2, m_b_alpha, m_gla_norm_g, m_w_proj_a, m_w_proj_b, m_w_mix_out, m_ln2_g, m_ln2_b, m_ffn2_w_in, m_ffn2_w_out, m_ln3_g, m_ln3_b, v_w_ada, v_b_ada, v_ffn1_w_in, v_ffn1_w_out, v_ln1_g, v_ln1_b, v_w_mix_in, v_rel_bias, v_w_alpha2, v_b_alpha, v_gla_norm_g, v_w_proj_a, v_w_proj_b, v_w_mix_out, v_ln2_g, v_ln2_b, v_ffn2_w_in, v_ffn2_w_out, v_ln3_g, v_ln3_b):
    return _step(dict(locals()))
```

```python
import functools

import jax
import jax.numpy as jnp
from jax import lax
from jax.experimental import pallas as pl
from jax.experimental.pallas import tpu as pltpu

F32 = jnp.float32
BF16 = jnp.bfloat16
MESH = pl.DeviceIdType.MESH
HIGHEST = lax.Precision.HIGHEST

VMEM_LIMIT_BYTES = 56 * 1024 * 1024
LANES = 128
SUBLANES = 8

CHUNK = 64
A_HEADS = 16
A_HEAD_DIM = 64
A_PAST_CHUNKS = 8
A_BAND = (A_PAST_CHUNKS + 1) * CHUNK
A_PAD = A_PAST_CHUNKS * CHUNK
REL_CLIP = 256
REL_SIZE = REL_CLIP + CHUNK
B_HEADS = 4
GATE_RANK = 16
GATE_TAU = 16.0
N_MOD = 9
DEPTH = 1
ALPHA = (2.0 * DEPTH) ** 0.25
LN_EPS = 1e-5
RMS_EPS = 1e-6
ADAM_LR = 0.001
ADAM_B1 = 0.9
ADAM_B2 = 0.999
ADAM_EPS = 1e-08
ADAM_WD = 0.01
ADAM_STEP = 10
NEG_BIG = -1e30

N_CHIPS = 4
N_DEV = 8


def _cp(*sem):
    return pltpu.CompilerParams(dimension_semantics=sem, vmem_limit_bytes=VMEM_LIMIT_BYTES)


class _Stage:
    def __init__(self, arrays, out_shapes, n_sems, start, finish, aliases=None):
        self.arrays, self.out_shapes, self.n_sems = list(arrays), list(out_shapes), n_sems
        self.start, self.finish, self.aliases = start, finish, dict(aliases or {})


def _pcall(body, stages, *, name, out_shape, in_specs, out_specs, grid=(), scratch_shapes=(), compiler_params=None):
    single = not isinstance(out_shape, (list, tuple))
    outs = [out_shape] if single else list(out_shape)
    ospecs = [out_specs] if single else list(out_specs)
    in_specs, scratch_shapes = list(in_specs), list(scratch_shapes)
    n_in, n_out, n_sc = len(in_specs), len(outs), len(scratch_shapes)
    stages = list(stages or [])
    c_in = [a for st in stages for a in st.arrays]
    c_out = [o for st in stages for o in st.out_shapes]
    aliases = {}
    io, oo = n_in, n_out
    for st in stages:
        for a, b in st.aliases.items():
            aliases[io + a] = oo + b
        io += len(st.arrays)
        oo += len(st.out_shapes)

    def wrapped(*refs):
        ins = refs[:n_in]
        cins = refs[n_in:n_in + len(c_in)]
        base = n_in + len(c_in)
        mouts = refs[base:base + n_out]
        couts = refs[base + n_out:base + n_out + len(c_out)]
        base += n_out + len(c_out)
        scr = refs[base:base + n_sc]
        sems = refs[base + n_sc:]

        def each(phase):
            i = o = 0
            for k, st in enumerate(stages):
                fn = st.start if phase == 0 else st.finish
                fn(cins[i:i + len(st.arrays)], couts[o:o + len(st.out_shapes)], sems[2 * k], sems[2 * k + 1])
                i += len(st.arrays)
                o += len(st.out_shapes)

        if stages and grid:
            first = functools.reduce(jnp.logical_and, [pl.program_id(a) == 0 for a in range(len(grid))])
            last = functools.reduce(jnp.logical_and, [pl.program_id(a) == g - 1 for a, g in enumerate(grid)])
            pl.when(first)(lambda: each(0))
            if body is not None:
                body(*ins, *mouts, *scr)
            pl.when(last)(lambda: each(1))
        else:
            each(0)
            if body is not None:
                body(*ins, *mouts, *scr)
            each(1)

    sem_shapes = []
    for st in stages:
        sem_shapes += [pltpu.SemaphoreType.DMA((st.n_sems,)), pltpu.SemaphoreType.DMA((st.n_sems,))]
    kwargs = dict(grid=grid) if grid else {}
    if compiler_params is not None:
        kwargs["compiler_params"] = compiler_params

    def run(*operands):
        res = pl.pallas_call(
            wrapped, name=name, out_shape=outs + c_out, in_specs=in_specs + [HBM_SPEC] * len(c_in),
            out_specs=ospecs + [HBM_SPEC] * len(c_out), scratch_shapes=scratch_shapes + sem_shapes,
            input_output_aliases=aliases, **kwargs)(*operands, *c_in)
        main = res[0] if single else tuple(res[:n_out])
        if not stages:
            return main
        comm, o = [], n_out
        for st in stages:
            comm.append(list(res[o:o + len(st.out_shapes)]))
            o += len(st.out_shapes)
        return main, comm

    return run


LONG_K = (2048, 1024)


def _tile(n, prefs):
    for t in prefs:
        if t <= n and n % t == 0:
            return t
    return n


_DIMS = {"nn": (((1,), (0,)), ((), ())), "nt": (((1,), (1,)), ((), ())), "tn": (((0,), (0,)), ((), ()))}


def _dot(a, b, mode="nn", precision=None):
    return lax.dot_general(a, b, _DIMS[mode], precision=precision, preferred_element_type=F32)


def _sigmoid(x):
    return 1.0 / (1.0 + jnp.exp(-x))


def _mm(name, mode, a, b, mnk, *, tm, tn, tk, out_dtype=F32, precision=None, a_spec=None, b_spec=None,
        out_shape=None, o_spec=None, add=None, a_fn=None, stages=None):
    m, n, k = mnk
    assert m % tm == 0 and n % tn == 0 and k % tk == 0, (name, mnk, tm, tn, tk)
    nk = k // tk
    if a_spec is None:
        a_spec = {"nn": pl.BlockSpec((tm, tk), lambda i, j, kk: (i, kk)),
                  "nt": pl.BlockSpec((tm, tk), lambda i, j, kk: (i, kk)),
                  "tn": pl.BlockSpec((tk, tm), lambda i, j, kk: (kk, i))}[mode]
    if b_spec is None:
        b_spec = {"nn": pl.BlockSpec((tk, tn), lambda i, j, kk: (kk, j)),
                  "nt": pl.BlockSpec((tn, tk), lambda i, j, kk: (j, kk)),
                  "tn": pl.BlockSpec((tk, tn), lambda i, j, kk: (kk, j))}[mode]
    if o_spec is None:
        o_spec = pl.BlockSpec((tm, tn), lambda i, j, kk: (i, j))
    if out_shape is None:
        out_shape = (m, n)
    has_add = add is not None

    def body(*refs):
        a_ref, b_ref = refs[0], refs[1]
        add_ref = refs[2] if has_add else None
        o_ref = refs[3] if has_add else refs[2]
        av = a_ref[...]
        if a_fn is not None:
            av = a_fn(av)
        part = _dot(av, b_ref[...], mode, precision)

        def finish(total):
            if has_add:
                total = total + add_ref[...]
            o_ref[...] = total.astype(out_dtype)

        if nk == 1:
            finish(part)
        else:
            acc_ref = refs[-1]
            kk = pl.program_id(2)

            @pl.when(kk == 0)
            def _():
                acc_ref[...] = part

            @pl.when(kk > 0)
            def _():
                acc_ref[...] += part

            @pl.when(kk == nk - 1)
            def _():
                finish(acc_ref[...])

    in_specs = [a_spec, b_spec]
    operands = [a, b]
    if has_add:
        in_specs.append(pl.BlockSpec((tm, tn), lambda i, j, kk: (i, j)))
        operands.append(add)
    return _pcall(
        body, stages, name=name, out_shape=jax.ShapeDtypeStruct(out_shape, out_dtype), grid=(m // tm, n // tn, nk),
        in_specs=in_specs, out_specs=o_spec,
        scratch_shapes=[pltpu.VMEM((tm, tn), F32)] if nk > 1 else [],
        compiler_params=_cp("arbitrary", "arbitrary", "arbitrary") if stages else _cp("parallel", "parallel", "arbitrary"),
    )(*operands)


def _row_spec(tr, d):
    return pl.BlockSpec((tr, d), lambda i: (i, 0))


def _vec_spec(d, rows=1):
    return pl.BlockSpec((rows, d), lambda i: (0, 0))


def _modulate(name, x, sh, sc):
    s, d = x.shape
    tr = _tile(s, (512, 256))

    def body(x_ref, sh_ref, sc_ref, o_ref):
        o_ref[...] = (x_ref[...] * (1.0 + sc_ref[...]) + sh_ref[...]).astype(BF16)

    return pl.pallas_call(
        body, name=name, out_shape=jax.ShapeDtypeStruct((s, d), BF16), grid=(s // tr,),
        in_specs=[_row_spec(tr, d), _vec_spec(d), _vec_spec(d)], out_specs=_row_spec(tr, d),
        compiler_params=_cp("parallel"),
    )(x, sh, sc)


def _ln_stats(r):
    mu = jnp.mean(r, axis=-1, keepdims=True)
    xc = r - mu
    var = jnp.mean(xc * xc, axis=-1, keepdims=True)
    rstd = lax.rsqrt(var + LN_EPS)
    return xc * rstd, rstd


def _resid_ln_fwd(name, x, f, gate, ln_g, ln_b, sh_n, sc_n, coef):
    s, d = x.shape
    tr = _tile(s, (256,))

    def body(x_ref, f_ref, gate_ref, g_ref, b_ref, sh_ref, sc_ref, h_ref, u_ref):
        r = ALPHA * x_ref[...] + (coef * gate_ref[...]) * f_ref[...]
        xhat, _ = _ln_stats(r)
        h = xhat * g_ref[...] + b_ref[...]
        h_ref[...] = h
        u_ref[...] = (h * (1.0 + sc_ref[...]) + sh_ref[...]).astype(BF16)

    return pl.pallas_call(
        body, name=name, out_shape=(jax.ShapeDtypeStruct((s, d), F32), jax.ShapeDtypeStruct((s, d), BF16)),
        grid=(s // tr,), in_specs=[_row_spec(tr, d), _row_spec(tr, d)] + [_vec_spec(d)] * 5,
        out_specs=(_row_spec(tr, d), _row_spec(tr, d)), compiler_params=_cp("parallel"),
    )(x, f, gate, ln_g, ln_b, sh_n, sc_n)


ROW_DSC, ROW_DSH, ROW_DLN_G, ROW_DLN_B, ROW_DGATE, ROW_LOSS = 0, 1, 2, 3, 4, 5


def _ln_bwd_core(dy, xhat, rstd, ln_g):
    dxhat = dy * ln_g
    m1 = jnp.mean(dxhat, axis=-1, keepdims=True)
    m2 = jnp.mean(dxhat * xhat, axis=-1, keepdims=True)
    return rstd * (dxhat - m1 - xhat * m2)


def _colsum(v):
    return jnp.sum(v, axis=0, keepdims=True)


def _final_ln_loss_bwd(name, x, f, target, gate, ln_g, ln_b, coef):
    s, d = x.shape
    tr = _tile(s, (256,))
    inv_d = 1.0 / d

    def body(x_ref, f_ref, t_ref, gate_ref, g_ref, b_ref, dr_ref, df_ref, acc_ref):
        @pl.when(pl.program_id(0) == 0)
        def _():
            acc_ref[...] = jnp.zeros_like(acc_ref)

        fv = f_ref[...]
        r = ALPHA * x_ref[...] + (coef * gate_ref[...]) * fv
        xhat, rstd = _ln_stats(r)
        h = xhat * g_ref[...] + b_ref[...]
        err = h - t_ref[...]
        dy = err * inv_d
        dr = _ln_bwd_core(dy, xhat, rstd, g_ref[...])
        dr_ref[...] = dr
        df_ref[...] = ((coef * gate_ref[...]) * dr).astype(BF16)
        acc_ref[ROW_DLN_G:ROW_DLN_G + 1, :] += _colsum(dy * xhat)
        acc_ref[ROW_DLN_B:ROW_DLN_B + 1, :] += _colsum(dy)
        acc_ref[ROW_DGATE:ROW_DGATE + 1, :] += _colsum((coef * dr) * fv)
        acc_ref[ROW_LOSS:ROW_LOSS + 1, :] += _colsum(err * err) * (0.5 * inv_d)

    return pl.pallas_call(
        body, name=name,
        out_shape=(jax.ShapeDtypeStruct((s, d), F32), jax.ShapeDtypeStruct((s, d), BF16),
                   jax.ShapeDtypeStruct((SUBLANES, d), F32)),
        grid=(s // tr,), in_specs=[_row_spec(tr, d)] * 3 + [_vec_spec(d)] * 3,
        out_specs=(_row_spec(tr, d), _row_spec(tr, d), _vec_spec(d, SUBLANES)),
        compiler_params=_cp("arbitrary"),
    )(x, f, target, gate, ln_g, ln_b)


def _resid_ln_bwd(name, du_n, dr_n, x, f, sc_n, gate, ln_g, ln_b, coef):
    s, d = x.shape
    tr = _tile(s, (256,))

    def body(du_ref, drn_ref, x_ref, f_ref, sc_ref, gate_ref, g_ref, b_ref, dr_ref, df_ref, acc_ref):
        @pl.when(pl.program_id(0) == 0)
        def _():
            acc_ref[...] = jnp.zeros_like(acc_ref)

        fv = f_ref[...]
        du = du_ref[...]
        r = ALPHA * x_ref[...] + (coef * gate_ref[...]) * fv
        xhat, rstd = _ln_stats(r)
        h = xhat * g_ref[...] + b_ref[...]
        dy = du * (1.0 + sc_ref[...]) + ALPHA * drn_ref[...]
        dr = _ln_bwd_core(dy, xhat, rstd, g_ref[...])
        dr_ref[...] = dr
        df_ref[...] = ((coef * gate_ref[...]) * dr).astype(BF16)
        acc_ref[ROW_DSC:ROW_DSC + 1, :] += _colsum(du * h)
        acc_ref[ROW_DSH:ROW_DSH + 1, :] += _colsum(du)
        acc_ref[ROW_DLN_G:ROW_DLN_G + 1, :] += _colsum(dy * xhat)
        acc_ref[ROW_DLN_B:ROW_DLN_B + 1, :] += _colsum(dy)
        acc_ref[ROW_DGATE:ROW_DGATE + 1, :] += _colsum((coef * dr) * fv)

    return pl.pallas_call(
        body, name=name,
        out_shape=(jax.ShapeDtypeStruct((s, d), F32), jax.ShapeDtypeStruct((s, d), BF16),
                   jax.ShapeDtypeStruct((SUBLANES, d), F32)),
        grid=(s // tr,), in_specs=[_row_spec(tr, d)] * 4 + [_vec_spec(d)] * 4,
        out_specs=(_row_spec(tr, d), _row_spec(tr, d), _vec_spec(d, SUBLANES)),
        compiler_params=_cp("arbitrary"),
    )(du_n, dr_n, x, f, sc_n, gate, ln_g, ln_b)


def _input_grad(name, du, dr, x, sc):
    s, d = x.shape
    tr = _tile(s, (256,))

    def body(du_ref, dr_ref, x_ref, sc_ref, gx_ref, acc_ref):
        @pl.when(pl.program_id(0) == 0)
        def _():
            acc_ref[...] = jnp.zeros_like(acc_ref)

        du = du_ref[...]
        gx_ref[...] = du * (1.0 + sc_ref[...]) + ALPHA * dr_ref[...]
        acc_ref[ROW_DSC:ROW_DSC + 1, :] += _colsum(du * x_ref[...])
        acc_ref[ROW_DSH:ROW_DSH + 1, :] += _colsum(du)

    return pl.pallas_call(
        body, name=name,
        out_shape=(jax.ShapeDtypeStruct((s, d), F32), jax.ShapeDtypeStruct((SUBLANES, d), F32)),
        grid=(s // tr,), in_specs=[_row_spec(tr, d)] * 3 + [_vec_spec(d)],
        out_specs=(_row_spec(tr, d), _vec_spec(d, SUBLANES)), compiler_params=_cp("arbitrary"),
    )(du, dr, x, sc)


def _ffn_in_fwd(name, u, w_in, stages=None):
    s, d = u.shape
    cs = w_in.shape[2]
    f = 2 * cs
    tm, tn = _tile(s, (2048, 1024, 512)), _tile(cs, (256, 128))
    nb = f // tn
    nbs = cs // tn

    def body(u_ref, wa_ref, wb_ref, ab_ref, act_ref):
        uv = u_ref[...]
        a = _dot(uv, wa_ref[...])
        b = _dot(uv, wb_ref[...])
        ab_ref[0] = a.astype(BF16)
        ab_ref[1] = b.astype(BF16)
        act_ref[...] = (a * _sigmoid(a) * b).astype(BF16)

    return _pcall(
        body, stages, name=name,
        out_shape=(jax.ShapeDtypeStruct((2, s, f), BF16), jax.ShapeDtypeStruct((s, f), BF16)),
        grid=(s // tm, nb),
        in_specs=[pl.BlockSpec((tm, d), lambda i, j: (i, 0)),
                  pl.BlockSpec((None, d, tn), lambda i, j: (j // nbs, 0, j % nbs)),
                  pl.BlockSpec((None, d, tn), lambda i, j: (2 + j // nbs, 0, j % nbs))],
        out_specs=(pl.BlockSpec((2, tm, tn), lambda i, j: (0, i, j)), pl.BlockSpec((tm, tn), lambda i, j: (i, j))),
        compiler_params=_cp("arbitrary", "arbitrary"),
    )(u, w_in, w_in)


def _ffn_out_bwd(name, df, w_out, ab, stages=None):
    s, d = df.shape
    f = w_out.shape[0]
    tm, tn = _tile(s, (1024, 512)), _tile(f, (512, 256, 128))

    def body(df_ref, w_ref, ab_ref, dab_ref):
        dact = _dot(df_ref[...], w_ref[...], "nt")
        a = ab_ref[0].astype(F32)
        b = ab_ref[1].astype(F32)
        sg = _sigmoid(a)
        dab_ref[0] = (dact * b * (sg * (1.0 + a * (1.0 - sg)))).astype(BF16)
        dab_ref[1] = (dact * (a * sg)).astype(BF16)

    return _pcall(
        body, stages, name=name, out_shape=jax.ShapeDtypeStruct((2, s, f), BF16), grid=(s // tm, f // tn),
        in_specs=[pl.BlockSpec((tm, d), lambda i, j: (i, 0)), pl.BlockSpec((tn, d), lambda i, j: (j, 0)),
                  pl.BlockSpec((2, tm, tn), lambda i, j: (0, i, j))],
        out_specs=pl.BlockSpec((2, tm, tn), lambda i, j: (0, i, j)),
        compiler_params=_cp("arbitrary", "arbitrary"),
    )(df, w_out, ab)


def _ffn_forward(tag, u, plan):
    w_in, w_out = plan.weight(f"{tag}_w_in"), plan.weight(f"{tag}_w_out")
    s, d = u.shape
    f = w_out.shape[0]
    ab, act = plan.host(f"{tag}_in_fwd", lambda st: _ffn_in_fwd(f"{tag}_in_fwd", u, w_in, st))
    out = plan.host(f"{tag}_out_fwd", lambda st: _mm(
        f"{tag}_out_fwd", "nn", act, w_out, (s, d, f), tm=_tile(s, (1024,)), tn=_tile(d, (1024,)),
        tk=_tile(f, (2816, 1408, 512, 128)), stages=st))
    return out, (ab, act)


def _ffn_backward(tag, df, u, saved, plan, in_first):
    w_in, w_out = plan.weight(f"{tag}_w_in"), plan.weight(f"{tag}_w_out")
    ab, act = saved
    s, d = u.shape
    f = w_out.shape[0]
    dab = plan.host(f"{tag}_out_bwd", lambda st: _ffn_out_bwd(f"{tag}_out_bwd", df, w_out, ab, st))
    cs = w_in.shape[2]
    tk = _tile(cs, (2816, 1408, 256, 128))
    nkh, nks = f // tk, cs // tk
    tmd = _tile(d, (1024,))
    tn = _tile(cs, (1408, 256, 128))
    nbh, nbs = f // tn, cs // tn
    tks = _tile(s, LONG_K)

    def dw_in():
        plan.grad(f"{tag}_w_in", plan.host(f"{tag}_dw_in", lambda st: _mm(
            f"{tag}_dw_in", "tn", u, dab, (d, 2 * f, s), tm=tmd, tn=tn, tk=tks,
            b_spec=pl.BlockSpec((None, tks, tn), lambda i, j, kk: (j // nbh, kk, j % nbh)), out_shape=(N_CHIPS, d, cs),
            o_spec=pl.BlockSpec((None, tmd, tn), lambda i, j, kk: (j // nbs, i, j % nbs)), stages=st)))

    def dw_out():
        plan.grad(f"{tag}_w_out", plan.host(f"{tag}_dw_out", lambda st: _mm(
            f"{tag}_dw_out", "tn", act, df, (f, d, s), tm=_tile(f, (1408, 512, 128)), tn=tmd, tk=tks, stages=st)))

    for step in ((dw_in, dw_out) if in_first else (dw_out, dw_in)):
        step()
    return plan.host(f"{tag}_du", lambda st: _mm(
        f"{tag}_du", "nt", dab, w_in, (s, d, 2 * f), tm=_tile(s, (1024,)), tn=tmd, tk=tk,
        a_spec=pl.BlockSpec((None, _tile(s, (1024,)), tk), lambda i, j, kk: (kk // nkh, i, kk % nkh)),
        b_spec=pl.BlockSpec((None, tmd, tk), lambda i, j, kk: (kk // nks, j, kk % nks)), stages=st))


ATTN_Q = 4 * CHUNK
ATTN_W = ATTN_Q + A_PAD


def _band_bias(bias):
    n = ATTN_Q // CHUNK
    rows = [jnp.pad(bias, ((0, 0), (0, 0), (i * CHUNK, (n - 1 - i) * CHUNK)), constant_values=NEG_BIG)
            for i in range(n)]
    return jnp.concatenate(rows, axis=1)


def _band_bias_grad(dband):
    n = ATTN_Q // CHUNK
    parts = [dband[:, i * CHUNK:(i + 1) * CHUNK, i * CHUNK:i * CHUNK + A_BAND] for i in range(n)]
    return functools.reduce(jnp.add, parts)


def _attn_probs(q, kw, bias, key0):
    sc = _dot(q, kw, "nt") * (A_HEAD_DIM ** -0.5) + bias
    ks = lax.broadcasted_iota(jnp.int32, sc.shape, 1)
    sc = jnp.where(key0 + ks >= 0, sc, NEG_BIG)
    p = jnp.exp(sc - jnp.max(sc, axis=-1, keepdims=True))
    return p / jnp.sum(p, axis=-1, keepdims=True)


def _head_masks():
    lane = lax.broadcasted_iota(jnp.int32, (1, LANES), 1)
    return [lane // A_HEAD_DIM == h for h in range(LANES // A_HEAD_DIM)]


def _attn_fwd(p1, kvp, band, stages=None):
    s = p1.shape[0]
    aw = A_HEADS * A_HEAD_DIM
    nblk = aw // LANES
    hpb = LANES // A_HEAD_DIM
    assert s % ATTN_Q == 0

    def body(q_ref, k_ref, v_ref, b_ref, o_ref):
        base = pl.multiple_of(pl.program_id(1) * ATTN_Q, ATTN_Q)
        qv = q_ref[...]
        kw = k_ref[pl.ds(base, ATTN_W), :]
        vw = v_ref[pl.ds(base, ATTN_W), :]
        out = jnp.zeros((ATTN_Q, LANES), F32)
        for h, mask in enumerate(_head_masks()):
            p = _attn_probs(jnp.where(mask, qv, jnp.zeros_like(qv)), kw, b_ref[h], base - A_PAD)
            out = jnp.where(mask, _dot(p.astype(BF16), vw), out)
        o_ref[...] = out.astype(BF16)

    kv_rows = s + A_PAD
    return _pcall(
        body, stages, name="attn_fwd", out_shape=jax.ShapeDtypeStruct((s, aw), BF16), grid=(nblk, s // ATTN_Q),
        in_specs=[pl.BlockSpec((ATTN_Q, LANES), lambda b, i: (i, b)),
                  pl.BlockSpec((kv_rows, LANES), lambda b, i: (0, b)),
                  pl.BlockSpec((kv_rows, LANES), lambda b, i: (0, nblk + b)),
                  pl.BlockSpec((hpb, ATTN_Q, ATTN_W), lambda b, i: (b, 0, 0))],
        out_specs=pl.BlockSpec((ATTN_Q, LANES), lambda b, i: (i, b)),
        compiler_params=_cp("arbitrary", "arbitrary"),
    )(p1, kvp, kvp, band)


def _attn_bwd(p1, kvp, band, dya, stages=None):
    s = p1.shape[0]
    aw = A_HEADS * A_HEAD_DIM
    nblk = aw // LANES
    hpb = LANES // A_HEAD_DIM
    scale = A_HEAD_DIM ** -0.5

    def body(q_ref, k_ref, v_ref, b_ref, do_ref, dq_ref, dk_ref, dv_ref, db_ref):
        @pl.when(pl.program_id(1) == 0)
        def _():
            dk_ref[...] = jnp.zeros_like(dk_ref)
            dv_ref[...] = jnp.zeros_like(dv_ref)
            db_ref[...] = jnp.zeros_like(db_ref)

        base = pl.multiple_of(pl.program_id(1) * ATTN_Q, ATTN_Q)
        window = pl.ds(base, ATTN_W)
        kw = k_ref[window, :]
        vw = v_ref[window, :]
        qv = q_ref[...]
        dov = do_ref[...]
        dq = jnp.zeros((ATTN_Q, LANES), F32)
        dk = jnp.zeros((ATTN_W, LANES), F32)
        dv = jnp.zeros((ATTN_W, LANES), F32)
        for h, mask in enumerate(_head_masks()):
            qh = jnp.where(mask, qv, jnp.zeros_like(qv))
            doh = jnp.where(mask, dov, jnp.zeros_like(dov))
            p = _attn_probs(qh, kw, b_ref[h], base - A_PAD)
            dp = _dot(doh, vw, "nt")
            ds = p * (dp - jnp.sum(p * dp, axis=-1, keepdims=True))
            db_ref[h] += ds
            dsb = (ds * scale).astype(BF16)
            dq = jnp.where(mask, _dot(dsb, kw), dq)
            dk = dk + _dot(dsb, qh, "tn")
            dv = dv + _dot(p.astype(BF16), doh, "tn")
        dq_ref[...] = dq.astype(BF16)
        dk_ref[window, :] += dk
        dv_ref[window, :] += dv

    kv_rows = s + A_PAD
    q_spec = pl.BlockSpec((ATTN_Q, LANES), lambda b, i: (i, b))
    acc_spec = pl.BlockSpec((kv_rows, LANES), lambda b, i: (0, b))
    b_spec = pl.BlockSpec((hpb, ATTN_Q, ATTN_W), lambda b, i: (b, 0, 0))
    return _pcall(
        body, stages, name="attn_bwd",
        out_shape=(jax.ShapeDtypeStruct((s, aw), BF16), jax.ShapeDtypeStruct((kv_rows, aw), F32),
                   jax.ShapeDtypeStruct((kv_rows, aw), F32), jax.ShapeDtypeStruct((A_HEADS, ATTN_Q, ATTN_W), F32)),
        grid=(nblk, s // ATTN_Q),
        in_specs=[q_spec, acc_spec, pl.BlockSpec((kv_rows, LANES), lambda b, i: (0, nblk + b)), b_spec, q_spec],
        out_specs=(q_spec, acc_spec, acc_spec, b_spec), compiler_params=_cp("arbitrary", "arbitrary"),
    )(p1, kvp, kvp, band, dya)


def _rel_onehot():
    qi = jnp.arange(CHUNK)[:, None]
    ks = jnp.arange(A_BAND)[None, :]
    idx = (jnp.clip(ks - A_PAD - qi, -REL_CLIP, CHUNK - 1) + REL_CLIP).reshape(1, CHUNK * A_BAND)
    return (jnp.arange(REL_SIZE)[:, None] == idx).astype(F32)


def _gla_gate(lr, wa2, balpha):
    z = _dot(lr, wa2) + balpha
    la = (jnp.minimum(z, 0.0) - jnp.log(1.0 + jnp.exp(-jnp.abs(z)))) * (1.0 / GATE_TAU)
    row = lax.broadcasted_iota(jnp.int32, (CHUNK, CHUNK), 0)
    col = lax.broadcasted_iota(jnp.int32, (CHUNK, CHUNK), 1)
    cum = _dot((row >= col).astype(F32), la, precision=HIGHEST)
    return z, la, cum


def _gla_dims(p2):
    kd = p2.shape[1] // 6
    hk = kd // B_HEADS
    hv = 2 * hk
    return kd, hk, hv


def _gla_fwd(p2, lrp, wa2p, balpha, gnorm, stages=None):
    s = p2.shape[0]
    kd, hk, hv = _gla_dims(p2)
    nc = s // CHUNK
    qscale = hk ** -0.5

    def body(p_ref, lr_ref, wa_ref, ba_ref, gn_ref, yb_ref, st_ref, state):
        @pl.when(pl.program_id(0) == 0)
        def _():
            state[...] = jnp.zeros_like(state)

        _, _, cum = _gla_gate(lr_ref[...], wa_ref[...], ba_ref[...])
        last = cum[CHUNK - 1:CHUNK, :]
        e = jnp.exp(last - cum)
        dch = jnp.exp(last)
        gn = gn_ref[...]
        for hh in range(B_HEADS):
            ks = slice(hh * hk, (hh + 1) * hk)
            q = p_ref[:, hh * hk:(hh + 1) * hk].astype(F32)
            k = p_ref[:, kd + hh * hk:kd + (hh + 1) * hk].astype(F32)
            v = p_ref[:, 2 * kd + hh * hv:2 * kd + (hh + 1) * hv]
            rg = p_ref[:, 4 * kd + hh * hv:4 * kd + (hh + 1) * hv].astype(F32)
            kdec = (k * e[:, ks]).astype(BF16)
            st = state[hh] * dch[:, ks] + _dot(v, kdec, "tn")
            state[hh] = st
            st_ref[hh] = st
            o = _dot((q * qscale).astype(BF16), st.astype(BF16), "nt")
            rinv = lax.rsqrt(jnp.mean(o * o, axis=-1, keepdims=True) + RMS_EPS)
            yb_ref[:, hh * hv:(hh + 1) * hv] = ((o * rinv * gn) * (rg * _sigmoid(rg))).astype(BF16)

    return _pcall(
        body, stages, name="gla_fwd",
        out_shape=(jax.ShapeDtypeStruct((s, 2 * kd), BF16), jax.ShapeDtypeStruct((nc, B_HEADS, hv, hk), F32)),
        grid=(nc,),
        in_specs=[pl.BlockSpec((CHUNK, 6 * kd), lambda i: (i, 0)), pl.BlockSpec((CHUNK, LANES), lambda i: (i, 0)),
                  pl.BlockSpec((LANES, kd), lambda i: (0, 0)), pl.BlockSpec((1, kd), lambda i: (0, 0)),
                  pl.BlockSpec((1, hv), lambda i: (0, 0))],
        out_specs=(pl.BlockSpec((CHUNK, 2 * kd), lambda i: (i, 0)),
                   pl.BlockSpec((None, B_HEADS, hv, hk), lambda i: (i, 0, 0, 0))),
        scratch_shapes=[pltpu.VMEM((B_HEADS, hv, hk), F32)], compiler_params=_cp("arbitrary"),
    )(p2, lrp, wa2p, balpha, gnorm)


GLA_ROW_DBALPHA, GLA_ROW_DGNORM = 0, 1


def _gla_bwd(p2, lrp, wa2p, balpha, gnorm, states, dyb, stages=None):
    s = p2.shape[0]
    kd, hk, hv = _gla_dims(p2)
    nc = s // CHUNK
    qscale = hk ** -0.5

    def body(p_ref, lr_ref, wa_ref, ba_ref, gn_ref, st_ref, sp_ref, dy_ref, dp_ref, dz_ref, sm_ref, gcar):
        i = pl.program_id(0)

        @pl.when(i == 0)
        def _():
            gcar[...] = jnp.zeros_like(gcar)
            sm_ref[...] = jnp.zeros_like(sm_ref)

        has_prev = (i < nc - 1).astype(F32)
        z, _, cum = _gla_gate(lr_ref[...], wa_ref[...], ba_ref[...])
        last = cum[CHUNK - 1:CHUNK, :]
        e = jnp.exp(last - cum)
        dch = jnp.exp(last)
        sgn = _sigmoid(-z) * (1.0 / GATE_TAU)
        gn = gn_ref[...]
        row = lax.broadcasted_iota(jnp.int32, (CHUNK, CHUNK), 0)
        col = lax.broadcasted_iota(jnp.int32, (CHUNK, CHUNK), 1)
        tri_strict = (row > col).astype(F32)
        for hh in range(B_HEADS):
            ks = slice(hh * hk, (hh + 1) * hk)
            q = p_ref[:, hh * hk:(hh + 1) * hk].astype(F32)
            k = p_ref[:, kd + hh * hk:kd + (hh + 1) * hk].astype(F32)
            v = p_ref[:, 2 * kd + hh * hv:2 * kd + (hh + 1) * hv]
            rg = p_ref[:, 4 * kd + hh * hv:4 * kd + (hh + 1) * hv].astype(F32)
            kdecf = k * e[:, ks]
            kdec = kdecf.astype(BF16)
            st16 = st_ref[hh].astype(BF16)
            qs = (q * qscale).astype(BF16)
            o = _dot(qs, st16, "nt")
            rinv = lax.rsqrt(jnp.mean(o * o, axis=-1, keepdims=True) + RMS_EPS)
            dy = dy_ref[:, hh * hv:(hh + 1) * hv].astype(F32)
            sg = _sigmoid(rg)
            onorm = o * rinv
            drg = dy * (onorm * gn) * (sg * (1.0 + rg * (1.0 - sg)))
            dob = dy * (rg * sg)
            sm_ref[GLA_ROW_DGNORM:GLA_ROW_DGNORM + 1, 0:hv] += _colsum(dob * onorm)
            t = dob * gn
            do = rinv * (t - onorm * jnp.mean(t * onorm, axis=-1, keepdims=True))
            do16 = do.astype(BF16)
            dq = _dot(do16, st16) * qscale
            gt = _dot(do16, qs, "tn") + gcar[hh]
            gcar[hh] = gt * dch[:, ks]
            dd = _colsum(gt * sp_ref[hh]) * has_prev
            gt16 = gt.astype(BF16)
            dkdec = _dot(v, gt16)
            dv = _dot(kdec, gt16, "nt")
            dla = dd * dch[:, ks] + _dot(tri_strict, dkdec * kdecf, precision=HIGHEST)
            dzh = dla * sgn[:, ks]
            sm_ref[GLA_ROW_DBALPHA:GLA_ROW_DBALPHA + 1, hh * hk:(hh + 1) * hk] += _colsum(dzh)
            dz_ref[:, hh * hk:(hh + 1) * hk] = dzh.astype(BF16)
            dp_ref[:, hh * hk:(hh + 1) * hk] = dq.astype(BF16)
            dp_ref[:, kd + hh * hk:kd + (hh + 1) * hk] = (dkdec * e[:, ks]).astype(BF16)
            dp_ref[:, 2 * kd + hh * hv:2 * kd + (hh + 1) * hv] = dv.astype(BF16)
            dp_ref[:, 4 * kd + hh * hv:4 * kd + (hh + 1) * hv] = drg.astype(BF16)

    rev = lambda i: (nc - 1 - i, 0)
    return _pcall(
        body, stages, name="gla_bwd",
        out_shape=(jax.ShapeDtypeStruct((s, 6 * kd), BF16), jax.ShapeDtypeStruct((s, kd), BF16),
                   jax.ShapeDtypeStruct((SUBLANES, kd), F32)),
        grid=(nc,),
        in_specs=[pl.BlockSpec((CHUNK, 6 * kd), rev), pl.BlockSpec((CHUNK, LANES), rev),
                  pl.BlockSpec((LANES, kd), lambda i: (0, 0)), pl.BlockSpec((1, kd), lambda i: (0, 0)),
                  pl.BlockSpec((1, hv), lambda i: (0, 0)),
                  pl.BlockSpec((None, B_HEADS, hv, hk), lambda i: (nc - 1 - i, 0, 0, 0)),
                  pl.BlockSpec((None, B_HEADS, hv, hk), lambda i: (jnp.maximum(nc - 2 - i, 0), 0, 0, 0)),
                  pl.BlockSpec((CHUNK, 2 * kd), rev)],
        out_specs=(pl.BlockSpec((CHUNK, 6 * kd), rev), pl.BlockSpec((CHUNK, kd), rev),
                   pl.BlockSpec((SUBLANES, kd), lambda i: (0, 0))),
        scratch_shapes=[pltpu.VMEM((B_HEADS, hv, hk), F32)], compiler_params=_cp("arbitrary"),
    )(p2, lrp, wa2p, balpha, gnorm, states, states, dyb)


def _merge_fwd(ya, yb, wpa, wpb, g):
    s, ka = ya.shape
    kb = yb.shape[1]
    d = wpa.shape[1]
    tm, tn = _tile(s, (1024, 512)), _tile(d, (512,))

    def body(ya_ref, yb_ref, wa_ref, wb_ref, g_ref, m_ref, pab_ref):
        pa = _dot(ya_ref[...], wa_ref[...])
        pb = _dot(yb_ref[...], wb_ref[...])
        m_ref[...] = (_sigmoid(g_ref[0].astype(F32)) * pa + _sigmoid(g_ref[1].astype(F32)) * pb).astype(BF16)
        pab_ref[0] = pa.astype(BF16)
        pab_ref[1] = pb.astype(BF16)

    st = pl.BlockSpec((2, tm, tn), lambda i, j: (0, i, j))
    return pl.pallas_call(
        body, name="merge_fwd",
        out_shape=(jax.ShapeDtypeStruct((s, d), BF16), jax.ShapeDtypeStruct((2, s, d), BF16)),
        grid=(s // tm, d // tn),
        in_specs=[pl.BlockSpec((tm, ka), lambda i, j: (i, 0)), pl.BlockSpec((tm, kb), lambda i, j: (i, 0)),
                  pl.BlockSpec((ka, tn), lambda i, j: (0, j)), pl.BlockSpec((kb, tn), lambda i, j: (0, j)), st],
        out_specs=(pl.BlockSpec((tm, tn), lambda i, j: (i, j)), st),
        compiler_params=_cp("parallel", "parallel"),
    )(ya, yb, wpa, wpb, g)


def _merge_bwd(dm, wmo, g, pab, stages=None):
    s, d = dm.shape
    tm, tn = _tile(s, (1024, 512)), _tile(d, (512,))

    def body(dm_ref, w_ref, g_ref, pab_ref, dpab_ref, dg_ref):
        dmg = _dot(dm_ref[...], w_ref[...], "nt")
        for j in range(2):
            sg = _sigmoid(g_ref[j].astype(F32))
            dpab_ref[j] = (dmg * sg).astype(BF16)
            dg_ref[j] = (dmg * pab_ref[j].astype(F32) * (sg * (1.0 - sg))).astype(BF16)

    st = pl.BlockSpec((2, tm, tn), lambda i, j: (0, i, j))
    return _pcall(
        body, stages, name="merge_bwd",
        out_shape=(jax.ShapeDtypeStruct((2, s, d), BF16), jax.ShapeDtypeStruct((2, s, d), BF16)),
        grid=(s // tm, d // tn),
        in_specs=[pl.BlockSpec((tm, d), lambda i, j: (i, 0)), pl.BlockSpec((tn, d), lambda i, j: (j, 0)), st, st],
        out_specs=(st, st), compiler_params=_cp("arbitrary", "arbitrary"),
    )(dm, wmo, g, pab)


def _virtual_rows(parts, lo, hi):
    out, off = [], 0
    for p in parts:
        a, b = max(lo, off), min(hi, off + p.shape[0])
        if a < b:
            out.append(p[a - off:b - off])
        off += p.shape[0]
    return out[0] if len(out) == 1 else jnp.concatenate(out, axis=0)


def _mix_in_row_groups(d):
    o1 = 3 * A_HEADS * A_HEAD_DIM
    o2 = o1 + 6 * (d // 4)
    o3 = o2 + GATE_RANK
    return (0, o1), (o1, o2), (o2, o3), (o3, o3 + 2 * d)


def _split_mix_in(stacked):
    parts = [stacked[j] for j in range(stacked.shape[0])]
    a, b, lr, g = (_virtual_rows(parts, lo, hi) for lo, hi in _mix_in_row_groups(stacked.shape[2]))
    return a, b, jnp.pad(lr, ((0, LANES - GATE_RANK), (0, 0))), g


MIX_TILE = 1024


def _mix_in_weights(plan):
    return plan.memo("mix_in_weights", lambda: _split_mix_in(plan.weight("w_mix_in")))


def _hosted_mm(plan):
    return lambda name, *a, **k: plan.host(name, lambda st: _mm(name, *a, stages=st, **k))


def _mix_forward(u2, plan, small):
    s, d = u2.shape
    wt_a, wt_b, wt_lr, wt_g = _mix_in_weights(plan)
    bias, wa2p, balpha, gnorm = small
    mm = _hosted_mm(plan)
    aw = A_HEADS * A_HEAD_DIM
    tm, tn = _tile(s, (1024,)), MIX_TILE
    p1 = mm("mix_in_a", "nt", u2, wt_a, (s, wt_a.shape[0], d), tm=tm, tn=tn, tk=d, out_dtype=BF16)
    p2 = mm("mix_in_b", "nt", u2, wt_b, (s, wt_b.shape[0], d), tm=tm, tn=tn, tk=d, out_dtype=BF16)
    lrp = mm("mix_in_lr", "nt", u2, wt_lr, (s, LANES, d), tm=tm, tn=LANES, tk=d, out_dtype=BF16)
    nbg = d // tn
    g = mm("mix_in_g", "nt", u2, wt_g, (s, 2 * d, d), tm=tm, tn=tn, tk=d, out_dtype=BF16, out_shape=(2, s, d),
           o_spec=pl.BlockSpec((None, tm, tn), lambda i, j, kk: (j // nbg, i, j % nbg)))
    kvp = jnp.pad(p1[:, aw:], ((A_PAD, 0), (0, 0)))
    ya = plan.host("attn_fwd", lambda st: _attn_fwd(p1, kvp, bias, st))
    yb, states = plan.host("gla_fwd", lambda st: _gla_fwd(p2, lrp, wa2p, balpha, gnorm, st))
    merged, pab = _merge_fwd(ya, yb, plan.weight("w_proj_a"), plan.weight("w_proj_b"), g)
    m = mm("mix_out", "nn", merged, plan.weight("w_mix_out"), (s, d, d), tm=tm, tn=tn, tk=d)
    return m, (p1, kvp, p2, lrp, states, ya, yb, g, pab, merged)


def _mix_backward(dm, u2, saved, plan, small):
    s, d = u2.shape
    wt_a, wt_b, wt_lr, wt_g = _mix_in_weights(plan)
    wpa, wpb, wmo = plan.weight("w_proj_a"), plan.weight("w_proj_b"), plan.weight("w_mix_out")
    bias, wa2p, balpha, gnorm = small
    p1, kvp, p2, lrp, states, ya, yb, g, pab, merged = saved
    mm = _hosted_mm(plan)
    aw = A_HEADS * A_HEAD_DIM
    kd = d // 4
    t = MIX_TILE
    tm = _tile(s, (1024,))
    tks = _tile(s, LONG_K)

    plan.grad("w_mix_out", mm("mix_dw_out", "tn", merged, dm, (d, d, s), tm=t, tn=t, tk=tks))
    dpab, dg = plan.host("merge_bwd", lambda st: _merge_bwd(dm, wmo, g, pab, st))
    sel = lambda j: pl.BlockSpec((None, tm, d), lambda i, jj, kk: (j, i, 0))
    dya = mm("mix_dya", "nt", dpab, wpa, (s, aw, d), tm=tm, tn=t, tk=d, out_dtype=BF16, a_spec=sel(0))
    dyb = mm("mix_dyb", "nt", dpab, wpb, (s, 2 * kd, d), tm=tm, tn=t, tk=d, out_dtype=BF16, a_spec=sel(1))
    selk = lambda j: pl.BlockSpec((None, tks, t), lambda i, jj, kk: (j, kk, jj))
    plan.grad("w_proj_a", mm("mix_dwpa", "tn", ya, dpab, (aw, d, s), tm=t, tn=t, tk=tks, b_spec=selk(0)))
    plan.grad("w_proj_b", mm("mix_dwpb", "tn", yb, dpab, (2 * kd, d, s), tm=t, tn=t, tk=tks, b_spec=selk(1)))

    dq, dkp, dvp, dbias = plan.host("attn_bwd", lambda st: _attn_bwd(p1, kvp, bias, dya, st))
    dp1 = jnp.concatenate([dq, dkp[A_PAD:].astype(BF16), dvp[A_PAD:].astype(BF16)], axis=1)
    dp2, dz, gsm = plan.host("gla_bwd", lambda st: _gla_bwd(p2, lrp, wa2p, balpha, gnorm, states, dyb, st))
    dlrp = mm("gla_dlr", "nt", dz, wa2p, (s, LANES, kd), tm=tm, tn=LANES, tk=kd, out_dtype=BF16)
    dwa2p = mm("gla_dwa2", "tn", lrp, dz, (LANES, kd, s), tm=LANES, tn=kd, tk=tks)

    tka = 3 * aw
    assert 6 * kd == tka
    du = mm("mix_du_a", "nn", dp1, wt_a, (s, d, tka), tm=tm, tn=t, tk=tka)
    du = mm("mix_du_b", "nn", dp2, wt_b, (s, d, tka), tm=tm, tn=t, tk=tka, add=du)
    du = mm("mix_du_lr", "nn", dlrp, wt_lr, (s, d, LANES), tm=tm, tn=t, tk=LANES, add=du)
    du = mm("mix_du_g", "nn", dg, wt_g, (s, d, 2 * d), tm=tm, tn=t, tk=d, add=du,
            a_spec=pl.BlockSpec((None, tm, d), lambda i, j, kk: (kk, i, 0)))
    nkg = d // t
    dw1 = mm("mix_dw_a", "tn", dp1, u2, (3 * aw, d, s), tm=t, tn=t, tk=tks)
    dw2 = mm("mix_dw_b", "tn", dp2, u2, (6 * kd, d, s), tm=t, tn=t, tk=tks)
    dwlr = mm("mix_dw_lr", "tn", dlrp, u2, (LANES, d, s), tm=LANES, tn=t, tk=tks)
    dwg = mm("mix_dw_g", "tn", dg, u2, (2 * d, d, s), tm=t, tn=t, tk=tks,
             a_spec=pl.BlockSpec((None, tks, t), lambda i, j, kk: (i // nkg, kk, i % nkg)))
    pieces = [dw1, dw2, dwlr[:GATE_RANK], dwg]
    shard_rows = sum(p.shape[0] for p in pieces) // N_CHIPS
    plan.grad("w_mix_in", jnp.stack([_virtual_rows(pieces, j * shard_rows, (j + 1) * shard_rows)
                                     for j in range(N_CHIPS)]))
    return du, (dbias, dwa2p[:GATE_RANK], gsm)


def _device_step(x, target, mod, small, plan):
    s, d = x.shape
    row = lambda i: mod[i:i + 1]
    sh1, sc1, g1, sh2, sc2, g2, sh3, sc3, g3 = (row(i) for i in range(N_MOD))

    onehot = _rel_onehot()
    bias = _mm("rel_bias_expand", "nn", small["rel_bias"], onehot, (A_HEADS, CHUNK * A_BAND, REL_SIZE),
               tm=A_HEADS, tn=4608, tk=REL_SIZE, precision=HIGHEST).reshape(A_HEADS, CHUNK, A_BAND)
    bias = _band_bias(bias)
    wa2p = jnp.pad(small["w_alpha2"], ((0, LANES - GATE_RANK), (0, 0))).astype(BF16)
    mix_small = (bias, wa2p, small["b_alpha"], small["gla_norm_g"])

    u1 = _modulate("mod1", x, sh1, sc1)
    f1, sv1 = _ffn_forward("ffn1", u1, plan)
    h1, u2 = _resid_ln_fwd("ln1_fwd", x, f1, g1, small["ln1_g"], small["ln1_b"], sh2, sc2, 0.5)
    m, svm = _mix_forward(u2, plan, mix_small)
    h2, u3 = _resid_ln_fwd("ln2_fwd", h1, m, g2, small["ln2_g"], small["ln2_b"], sh3, sc3, 1.0)
    f2, sv2 = _ffn_forward("ffn2", u3, plan)

    dr3, df2, acc3 = _final_ln_loss_bwd("ln3_loss_bwd", h2, f2, target, g3, small["ln3_g"], small["ln3_b"], 0.5)
    du3 = _ffn_backward("ffn2", df2, u3, sv2, plan, in_first=False)
    dr2, dmx, acc2 = _resid_ln_bwd("ln2_bwd", du3, dr3, h1, m, sc3, g2, small["ln2_g"], small["ln2_b"], 1.0)
    du2, (dbias, dwa2, gsm) = _mix_backward(dmx, u2, svm, plan, mix_small)
    dr1, df1, acc1 = _resid_ln_bwd("ln1_bwd", du2, dr2, x, f1, sc2, g1, small["ln1_g"], small["ln1_b"], 0.5)
    du1 = _ffn_backward("ffn1", df1, u1, sv1, plan, in_first=True)
    grad_x, acc0 = _input_grad("input_grad", du1, dr1, x, sc1)

    drel = _hosted_mm(plan)("rel_bias_grad", "nt", _band_bias_grad(dbias).reshape(A_HEADS, CHUNK * A_BAND), onehot,
                            (A_HEADS, REL_SIZE, CHUNK * A_BAND), tm=A_HEADS, tn=REL_SIZE, tk=4608, precision=HIGHEST)
    loss = jnp.sum(acc3[ROW_LOSS])
    dmod = jnp.stack([acc0[ROW_DSH], acc0[ROW_DSC], acc1[ROW_DGATE], acc1[ROW_DSH], acc1[ROW_DSC], acc2[ROW_DGATE],
                      acc2[ROW_DSH], acc2[ROW_DSC], acc3[ROW_DGATE]])
    kd = d // 4
    small_grads = dict(ln1_g=acc1[ROW_DLN_G], ln1_b=acc1[ROW_DLN_B], ln2_g=acc2[ROW_DLN_G], ln2_b=acc2[ROW_DLN_B],
                       ln3_g=acc3[ROW_DLN_G], ln3_b=acc3[ROW_DLN_B], b_alpha=gsm[GLA_ROW_DBALPHA],
                       gla_norm_g=gsm[GLA_ROW_DGNORM, :kd // B_HEADS * 2], rel_bias=drel, w_alpha2=dwa2)
    return loss, grad_x, small_grads, dmod


HBM_SPEC = pl.BlockSpec(memory_space=pl.ANY)


def _mesh_pos():
    return lax.axis_index("x"), lax.axis_index("y"), lax.axis_index("c")


def _other_chips(x, y):
    return [(1 - x, y), (x, 1 - y), (1 - x, 1 - y)]


def _remote(src, dst, send_sem, recv_sem, to):
    return pltpu.make_async_remote_copy(src_ref=src, dst_ref=dst, send_sem=send_sem, recv_sem=recv_sem,
                                        device_id=to, device_id_type=MESH)


def _allgather_rows(name, v):
    m_per, n = v.shape

    def body(x_ref, out_ref, send_sems, recv_sems, local_sem):
        x, y, c = _mesh_pos()
        me, sibling = (x, y, c), (x, y, 1 - c)
        chips = _other_chips(x, y)

        def rows(px, py, pc):
            return out_ref.at[pl.ds((4 * px + 2 * py + pc) * m_per, m_per), :]

        def copy(k, block, to, src=None):
            return _remote(rows(*block) if src is None else src, rows(*block), send_sems.at[k], recv_sems.at[k], to)

        mine = pltpu.make_async_copy(x_ref, rows(*me), local_sem)
        mine.start()
        first = [copy(0, me, sibling, src=x_ref)]
        first += [copy(1 + j, me, (*chip, c), src=x_ref) for j, chip in enumerate(chips)]
        for cp in first:
            cp.start()
        passed = [copy(4 + j, (*chip, c), sibling) for j, chip in enumerate(chips)]
        for j, chip in enumerate(chips):
            copy(1 + j, (*chip, c), me).wait_recv()
            passed[j].start()
        copy(0, sibling, me).wait_recv()
        for j, chip in enumerate(chips):
            copy(4 + j, (*chip, 1 - c), me).wait_recv()
        for cp in first + passed:
            cp.wait_send()
        mine.wait()

    return pl.pallas_call(
        body, name=name, out_shape=jax.ShapeDtypeStruct((N_DEV * m_per, n), v.dtype),
        in_specs=[pl.BlockSpec(memory_space=pltpu.VMEM)], out_specs=pl.BlockSpec(memory_space=pltpu.VMEM),
        scratch_shapes=[pltpu.SemaphoreType.DMA((7,)), pltpu.SemaphoreType.DMA((7,)), pltpu.SemaphoreType.DMA],
    )(v)


def _allgather_weights(shards):
    n = len(shards)
    TO_X, TO_Y, PASS_TO_X, PASS_TO_Y, SIB_X, SIB_Y, SIB_D0, SIB_D1 = range(8)

    def body(*refs):
        ins, outs = refs[:n], refs[n:2 * n]
        send_sems, recv_sems = refs[2 * n:]
        x, y, c = _mesh_pos()
        sibling = (x, y, 1 - c)
        xn, yn, dg = _other_chips(x, y)
        j0, jx, jy, jd = (2 * p[0] + p[1] for p in ((x, y), xn, yn, dg))
        sends = []

        def rows(w, hc, quarter=None):
            hr = shards[w].shape[0] // 2
            if quarter is None:
                return pl.ds(hc * hr, hr)
            return pl.ds(hc * hr + quarter * (hr // 2), hr // 2)

        def push(src, dst, w, k, to):
            cp = _remote(src, dst, send_sems.at[w, k], recv_sems.at[w, k], to)
            cp.start()
            sends.append(cp)

        def landed(piece, w, k):
            _remote(piece, piece, send_sems.at[w, k], recv_sems.at[w, k], sibling).wait_recv()

        for w in range(n):
            mine = rows(w, c)
            push(ins[w].at[mine, :], outs[w].at[j0, mine, :], w, TO_X, (*xn, c))
            push(ins[w].at[mine, :], outs[w].at[j0, mine, :], w, TO_Y, (*yn, c))
        for w in range(n):
            half_x = outs[w].at[jx, rows(w, c), :]
            landed(half_x, w, TO_X)
            quarter = outs[w].at[jx, rows(w, c, 1), :]
            push(quarter, quarter, w, PASS_TO_Y, (*yn, c))
            push(half_x, half_x, w, SIB_X, sibling)
            half_y = outs[w].at[jy, rows(w, c), :]
            landed(half_y, w, TO_Y)
            quarter = outs[w].at[jy, rows(w, c, 0), :]
            push(quarter, quarter, w, PASS_TO_X, (*xn, c))
            push(half_y, half_y, w, SIB_Y, sibling)
        for w in range(n):
            for q, arrives_on, on in ((0, PASS_TO_X, SIB_D0), (1, PASS_TO_Y, SIB_D1)):
                piece = outs[w].at[jd, rows(w, c, q), :]
                landed(piece, w, arrives_on)
                push(piece, piece, w, on, sibling)
        for w in range(n):
            landed(outs[w].at[jx, rows(w, 1 - c), :], w, SIB_X)
            landed(outs[w].at[jy, rows(w, 1 - c), :], w, SIB_Y)
            landed(outs[w].at[jd, rows(w, 1 - c, 0), :], w, SIB_D0)
            landed(outs[w].at[jd, rows(w, 1 - c, 1), :], w, SIB_D1)
        for cp in sends:
            cp.wait_send()

    return pl.pallas_call(
        body, name="allgather_weights",
        out_shape=[jax.ShapeDtypeStruct((N_CHIPS,) + sh.shape, sh.dtype) for sh in shards],
        in_specs=[HBM_SPEC] * n, out_specs=[HBM_SPEC] * n,
        scratch_shapes=[pltpu.SemaphoreType.DMA((n, 8)), pltpu.SemaphoreType.DMA((n, 8))],
    )(*shards)


def _half(ref, hc, col, *lead):
    rows, cols = ref.shape[-2:]
    if col:
        return ref.at[(*lead, slice(None), pl.ds(hc * (cols // 2), cols // 2))]
    return ref.at[(*lead, pl.ds(hc * (rows // 2), rows // 2), slice(None))]


def _half_shape(shape, col):
    return shape[:-2] + ((shape[-2], shape[-1] // 2) if col else (shape[-2] // 2, shape[-1]))


def _stage_gather_ici(shards, cols):
    n = len(shards)

    def copies(ins, outs, send, recv):
        x, y, c = _mesh_pos()
        j0 = 2 * x + y
        for w in range(n):
            for r, chip in enumerate(_other_chips(x, y)):
                jr = 2 * chip[0] + chip[1]
                mine = _remote(_half(ins[w], c, cols[w]), _half(outs[w], c, cols[w], j0),
                               send.at[3 * w + r], recv.at[3 * w + r], (*chip, c))
                landed = _half(outs[w], c, cols[w], jr)
                yield mine, _remote(landed, landed, send.at[3 * w + r], recv.at[3 * w + r], (*chip, c))

    def start(*refs):
        for mine, _ in copies(*refs):
            mine.start()

    def finish(*refs):
        pairs = list(copies(*refs))
        for _, theirs in pairs:
            theirs.wait_recv()
        for mine, _ in pairs:
            mine.wait_send()

    outs = [jax.ShapeDtypeStruct((N_CHIPS,) + sh.shape, sh.dtype) for sh in shards]
    return _Stage(shards, outs, 3 * n, start, finish)


def _stage_gather_d2d(partial, cols):
    n = len(partial)

    def copies(ins, outs, send, recv):
        x, y, c = _mesh_pos()
        for w in range(n):
            for r, chip in enumerate(_other_chips(x, y)):
                jr = 2 * chip[0] + chip[1]
                mine = _remote(_half(ins[w], c, cols[w], jr), _half(outs[w], c, cols[w], jr), send.at[3 * w + r],
                               recv.at[3 * w + r], (x, y, 1 - c))
                got = _half(outs[w], 1 - c, cols[w], jr)
                yield mine, _remote(got, got, send.at[3 * w + r], recv.at[3 * w + r], (x, y, 1 - c))

    def start(*refs):
        for mine, _ in copies(*refs):
            mine.start()

    def finish(*refs):
        pairs = list(copies(*refs))
        for _, theirs in pairs:
            theirs.wait_recv()
        for mine, _ in pairs:
            mine.wait_send()

    outs = [jax.ShapeDtypeStruct(p.shape, p.dtype) for p in partial]
    return _Stage(partial, outs, 3 * n, start, finish, aliases={w: w for w in range(n)})


def _stage_exchange_halves(grads, cols):
    n = len(grads)

    def copies(ins, outs, send, recv):
        x, y, c = _mesh_pos()
        for w in range(n):
            yield _remote(_half(ins[w], 1 - c, cols[w], slice(None)), outs[w], send.at[w], recv.at[w], (x, y, 1 - c))

    def start(*refs):
        for cp in copies(*refs):
            cp.start()

    def finish(*refs):
        cps = list(copies(*refs))
        for cp in cps:
            cp.wait_recv()
        for cp in cps:
            cp.wait_send()

    outs = [jax.ShapeDtypeStruct(_half_shape(g.shape, col), g.dtype) for g, col in zip(grads, cols)]
    return _Stage(grads, outs, n, start, finish)


def _stage_scatter(parts):
    n = len(parts)

    def copies(ins, outs, send, recv):
        x, y, c = _mesh_pos()
        for w in range(n):
            for r, chip in enumerate(_other_chips(x, y)):
                jr = 2 * chip[0] + chip[1]
                yield _remote(ins[w].at[jr], outs[w].at[r], send.at[3 * w + r], recv.at[3 * w + r], (*chip, c))

    def start(*refs):
        for cp in copies(*refs):
            cp.start()

    def finish(*refs):
        cps = list(copies(*refs))
        for cp in cps:
            cp.wait_recv()
        for cp in cps:
            cp.wait_send()

    outs = [jax.ShapeDtypeStruct((3,) + p.shape[1:], p.dtype) for p in parts]
    return _Stage(parts, outs, 3 * n, start, finish)


def _stage_share(fulls, cols):
    n = len(fulls)

    def copies(ins, outs, send, recv):
        x, y, c = _mesh_pos()
        for w in range(n):
            theirs = _half(outs[w], 1 - c, cols[w])
            yield (_remote(_half(ins[w], c, cols[w]), _half(outs[w], c, cols[w]), send.at[w], recv.at[w], (x, y, 1 - c)),
                   _remote(theirs, theirs, send.at[w], recv.at[w], (x, y, 1 - c)))

    def start(*refs):
        for mine, _ in copies(*refs):
            mine.start()

    def finish(*refs):
        pairs = list(copies(*refs))
        for _, theirs in pairs:
            theirs.wait_recv()
        for mine, _ in pairs:
            mine.wait_send()

    outs = [jax.ShapeDtypeStruct(h.shape, h.dtype) for h in fulls]
    return _Stage(fulls, outs, n, start, finish, aliases={w: w for w in range(n)})


def _run_stages(name, stages):
    return _pcall(None, stages, name=name, out_shape=[], in_specs=[], out_specs=[])()[1]


TILE_BYTES = 2 * 1024 * 1024


def _row_tile(rows, cols, itemsize=4):
    for t in (1024, 512, 256, 128, 64, 32, 16, 8):
        if rows % t == 0 and t * cols * itemsize <= TILE_BYTES:
            return t
    return rows


def _col_tile(rows, cols, itemsize=4):
    for t in (2048, 1024, 512, 256, 128):
        if cols % t == 0 and t * rows * itemsize <= TILE_BYTES:
            return t
    return cols


def _tiling(rows, cols, col):
    if col:
        tc = _col_tile(rows, cols)
        return (rows, tc), cols // tc
    tr = _row_tile(rows, cols)
    return (tr, cols), rows // tr


def _strip(col, i):
    return (0, i) if col else (i, 0)


def _pair_sum(name, g, recv, core, col):
    blk, nb = _tiling(*recv.shape[1:], col)

    def body(c_ref, g_ref, r_ref, o_ref):
        o_ref[...] = (g_ref[...] + r_ref[...]).astype(BF16)

    grid_spec = pltpu.PrefetchScalarGridSpec(
        num_scalar_prefetch=1, grid=(N_CHIPS, nb),
        in_specs=[pl.BlockSpec((None,) + blk, lambda j, i, cr: (j,) + _strip(col, cr[0] * nb + i)),
                  pl.BlockSpec((None,) + blk, lambda j, i, cr: (j,) + _strip(col, i))],
        out_specs=pl.BlockSpec((None,) + blk, lambda j, i, cr: (j,) + _strip(col, i)))
    return pl.pallas_call(body, name=name, out_shape=jax.ShapeDtypeStruct(recv.shape, BF16), grid_spec=grid_spec,
                          compiler_params=_cp("parallel", "parallel"))(core, g, recv)


def _quad_sum(name, own, landed, chip_core, col):
    rows, cols = landed.shape[1:]
    blk, nb = _tiling(rows, cols, col)
    full = (rows, 2 * cols) if col else (2 * rows, cols)

    def body(cc_ref, own_ref, l_ref, o_ref):
        o_ref[...] = ((own_ref[...].astype(F32) + l_ref[0].astype(F32)) + l_ref[1].astype(F32)) + l_ref[2].astype(F32)

    grid_spec = pltpu.PrefetchScalarGridSpec(
        num_scalar_prefetch=1, grid=(nb,),
        in_specs=[pl.BlockSpec((None,) + blk, lambda i, cc: (cc[0],) + _strip(col, i)),
                  pl.BlockSpec((3,) + blk, lambda i, cc: (0,) + _strip(col, i))],
        out_specs=pl.BlockSpec(blk, lambda i, cc: _strip(col, cc[1] * nb + i)))
    return pl.pallas_call(body, name=name, out_shape=jax.ShapeDtypeStruct(full, F32), grid_spec=grid_spec,
                          compiler_params=_cp("arbitrary"))(chip_core, own, landed)


def _device_sum(name, gathered):
    def body(g_ref, o_ref):
        total = g_ref[0]
        for k in range(1, N_DEV):
            total = total + g_ref[k]
        o_ref[...] = total

    return pl.pallas_call(body, name=name, out_shape=jax.ShapeDtypeStruct(gathered.shape[1:], F32))(gathered)


def _adamw(name, w, g, m, v):
    rows, cols = w.shape
    col = rows % SUBLANES != 0
    blk, nb = _tiling(rows, cols, col)
    bc1 = 1.0 - ADAM_B1 ** ADAM_STEP
    bc2 = 1.0 - ADAM_B2 ** ADAM_STEP

    def body(w_ref, g_ref, m_ref, v_ref, d_ref, mo_ref, vo_ref):
        gv = g_ref[...]
        mn = ADAM_B1 * m_ref[...] + (1.0 - ADAM_B1) * gv
        vn = ADAM_B2 * v_ref[...] + (1.0 - ADAM_B2) * (gv * gv)
        mo_ref[...] = mn
        vo_ref[...] = vn
        d_ref[...] = -ADAM_LR * ((mn / bc1) / (jnp.sqrt(vn / bc2) + ADAM_EPS) + ADAM_WD * w_ref[...])

    spec = pl.BlockSpec(blk, lambda i: _strip(col, i))
    return pl.pallas_call(
        body, name=name, out_shape=[jax.ShapeDtypeStruct((rows, cols), F32)] * 3, grid=(nb,),
        in_specs=[spec] * 4, out_specs=[spec] * 3, compiler_params=_cp("parallel"),
    )(w, g, m, v)


WEIGHTS = ["w_ada", "b_ada", "ffn1_w_in", "ffn1_w_out", "ln1_g", "ln1_b", "w_mix_in", "rel_bias", "w_alpha2",
           "b_alpha", "gla_norm_g", "w_proj_a", "w_proj_b", "w_mix_out", "ln2_g", "ln2_b", "ffn2_w_in", "ffn2_w_out",
           "ln3_g", "ln3_b"]
BIG = {"ffn1_w_in": True, "ffn1_w_out": False, "w_mix_in": False, "w_proj_a": True, "w_proj_b": True,
       "w_mix_out": False, "ffn2_w_in": True, "ffn2_w_out": False}
TRANSPOSED = ("w_mix_in",)
STACKED = ("ffn1_w_in", "ffn2_w_in", "w_mix_in")
GROUP_FFN1 = ("ffn1_w_in", "ffn1_w_out")
GROUP_PROJ = ("w_proj_a", "w_proj_b", "w_mix_out")
SMALL = ["ln1_g", "ln1_b", "ln2_g", "ln2_b", "ln3_g", "ln3_b", "b_alpha", "gla_norm_g", "rel_bias", "w_alpha2"]


def _pad_rows(vec, rows=SUBLANES):
    per = -(-vec.shape[0] // (rows * LANES)) * LANES
    return jnp.pad(vec, (0, rows * per - vec.shape[0])).reshape(rows, per)


def _silu(v):
    return v * _sigmoid(v)


class _MeshPlan:
    def __init__(self, shards, chip, core):
        self.shards, self.chip = shards, chip
        self.core1 = core.astype(jnp.int32).reshape(1)
        self.chip_core = jnp.stack([chip, core]).astype(jnp.int32)
        self.partial, self.full, self.local, self.pair, self.half, self.final, self.memos = {}, {}, {}, {}, {}, {}, {}
        ici, d2d, x1, x2, x3 = self.gather_ici, self.gather_d2d, self.exchange, self.scatter, self.share
        mix_in, in1, out1, in2, out2 = ("w_mix_in",), ("ffn1_w_in",), ("ffn1_w_out",), ("ffn2_w_in",), ("ffn2_w_out",)
        self.schedule = {
            "ffn1_in_fwd": [ici(mix_in)], "ffn1_out_fwd": [d2d(mix_in), ici(out2)],
            "mix_in_a": [d2d(out2)], "mix_in_g": [ici(GROUP_PROJ)],
            "attn_fwd": [ici(in2), d2d(GROUP_PROJ)], "gla_fwd": [d2d(in2)],
            "ffn2_dw_in": [x1(out2)], "ffn2_du": [x2(out2), x1(in2)], "mix_dw_out": [x3(out2)],
            "attn_bwd": [x2(in2)], "gla_bwd": [x3(in2), x1(GROUP_PROJ)],
            "mix_du_g": [x2(GROUP_PROJ)], "mix_dw_g": [x3(GROUP_PROJ)],
            "ffn1_out_bwd": [x1(mix_in)], "ffn1_dw_in": [x2(mix_in)], "ffn1_dw_out": [x3(mix_in), x1(in1)],
            "ffn1_du": [x2(in1), x1(out1)], "rel_bias_grad": [x2(out1), x3(in1)],
        }

    def weight(self, k):
        return self.full[k]

    def grad(self, k, g):
        r, cc = self.shards[k].shape
        if k not in STACKED:
            g = g.reshape(r, N_CHIPS, cc).transpose(1, 0, 2) if BIG[k] else g.reshape(N_CHIPS, r, cc)
        self.local[k] = g

    def memo(self, key, make):
        if key not in self.memos:
            self.memos[key] = make()
        return self.memos[key]

    def host(self, name, call):
        builders = self.schedule.get(name)
        if not builders:
            return call(None)
        built = [b() for b in builders]
        main, comm = call([st for st, _ in built])
        for (_, post), res in zip(built, comm):
            post(res)
        return main

    def run(self, name, builders):
        built = [b() for b in builders]
        for (_, post), res in zip(built, _run_stages(name, [st for st, _ in built])):
            post(res)

    def set_gathered(self, names, gathered):
        for k, g in zip(names, gathered):
            _, r, cc = g.shape
            g = lax.dynamic_update_slice(g, self.shards[k][None], (self.chip, 0, 0))
            if k not in STACKED:
                g = g.transpose(1, 0, 2).reshape(r, N_CHIPS * cc) if BIG[k] else g.reshape(N_CHIPS * r, cc)
            self.full[k] = g

    @staticmethod
    def cols(names):
        return [k in TRANSPOSED for k in names]

    def gather_ici(self, names):
        def post(res):
            self.partial.update(zip(names, res))
        return lambda: (_stage_gather_ici([self.shards[k] for k in names], self.cols(names)), post)

    def gather_d2d(self, names):
        return lambda: (_stage_gather_d2d([self.partial[k] for k in names], self.cols(names)),
                        lambda res: self.set_gathered(names, res))

    def exchange(self, names):
        def post(res):
            for k, r in zip(names, res):
                self.pair[k] = _pair_sum(f"pair_sum_{k}", self.local[k], r, self.core1, k in TRANSPOSED)
        return lambda: (_stage_exchange_halves([self.local[k] for k in names], self.cols(names)), post)

    def scatter(self, names):
        def post(res):
            for k, landed in zip(names, res):
                self.half[k] = _quad_sum(f"quad_sum_{k}", self.pair[k], landed, self.chip_core, k in TRANSPOSED)
        return lambda: (_stage_scatter([self.pair[k] for k in names]), post)

    def share(self, names):
        def post(res):
            self.final.update(zip(names, res))
        return lambda: (_stage_share([self.half[k] for k in names], self.cols(names)), post)


def _step(args):
    x_pos, y_pos, c_pos = _mesh_pos()
    chip = 2 * x_pos + y_pos
    dev = 4 * x_pos + 2 * y_pos + c_pos
    take = lambda name, k: args[name][0].T if k in TRANSPOSED else args[name][0]
    w = {k: take(k, k) for k in WEIGHTS}
    mom = {k: take("m_" + k, k) for k in WEIGHTS}
    vel = {k: take("v_" + k, k) for k in WEIGHTS}
    x = args["x"][0]
    target = args["loss_target"][0]
    s, d = x.shape
    kd = d // 4
    rel_sh = w["rel_bias"].shape[1]
    wa2_sh = w["w_alpha2"].shape[1]
    ada_sh = w["w_ada"].shape[1]

    n_rel, n_wa2 = A_HEADS * rel_sh, GATE_RANK * wa2_sh
    packed = _pad_rows(jnp.concatenate([args["c"].reshape(-1), w["rel_bias"].reshape(-1), w["w_alpha2"].reshape(-1)]))
    got = _allgather_rows("gather_small_inputs", packed).reshape(N_DEV, -1)
    c_all = got[:, :d]
    per_chip = got[0::2]
    rel_bias = per_chip[:, d:d + n_rel].reshape(N_CHIPS, A_HEADS, rel_sh).transpose(1, 0, 2).reshape(A_HEADS, -1)
    w_alpha2 = per_chip[:, d + n_rel:d + n_rel + n_wa2].reshape(N_CHIPS, GATE_RANK, wa2_sh).transpose(1, 0, 2)
    w_alpha2 = w_alpha2.reshape(GATE_RANK, -1)

    b_shard = lax.dynamic_slice(w["b_ada"], (chip * ada_sh,), (ada_sh,))
    mod_shard = _mm("ada_fwd", "nn", c_all, w["w_ada"], (N_DEV, ada_sh, d), tm=N_DEV, tn=_tile(ada_sh, (512, 128)),
                    tk=d, precision=HIGHEST, a_fn=_silu, add=jnp.broadcast_to(b_shard[None], (N_DEV, ada_sh)))
    mod_all = _allgather_rows("gather_mod", mod_shard).reshape(N_DEV, N_DEV, ada_sh)[0::2]
    mod_all = mod_all.transpose(1, 0, 2).reshape(N_DEV, N_MOD * d)
    mod = lax.dynamic_index_in_dim(mod_all, dev, 0, keepdims=False).reshape(N_MOD, d)

    names = list(BIG)
    plan = _MeshPlan({k: w[k].astype(BF16) for k in names}, chip, c_pos)
    plan.set_gathered(GROUP_FFN1, _allgather_weights([plan.shards[k] for k in GROUP_FFN1]))

    small = dict(rel_bias=rel_bias, w_alpha2=w_alpha2, b_alpha=w["b_alpha"][None], gla_norm_g=w["gla_norm_g"][None])
    for k in ("ln1_g", "ln1_b", "ln2_g", "ln2_b", "ln3_g", "ln3_b"):
        small[k] = w[k][None]
    loss_local, grad_x, small_grads, dmod = _device_step(x, target, mod, small, plan)
    loss = lax.psum(loss_local, ("x", "y", "c"))
    plan.run("grad_tail_share", [plan.share(GROUP_FFN1[1:])])

    flat = jnp.concatenate([small_grads[k].reshape(-1) for k in SMALL] + [dmod.reshape(-1)])
    n_small = flat.shape[0] - N_MOD * d
    packed = _pad_rows(flat)
    all_small = _allgather_rows("gather_small_grads", packed).reshape(N_DEV, SUBLANES, -1)
    summed = _device_sum("small_grad_sum", all_small).reshape(-1)
    dmod_all = all_small.reshape(N_DEV, -1)[:, n_small:n_small + N_MOD * d]
    dmod_shard = lax.dynamic_slice(dmod_all, (0, chip * ada_sh), (N_DEV, ada_sh))
    grads = {"b_ada": summed[n_small:n_small + N_MOD * d]}
    off = 0
    for k in SMALL:
        size = small_grads[k].size
        grads[k] = summed[off:off + size].reshape(small_grads[k].shape)
        off += size
    grads["rel_bias"] = lax.dynamic_slice(grads["rel_bias"], (0, chip * rel_sh), (A_HEADS, rel_sh))
    grads["w_alpha2"] = lax.dynamic_slice(grads["w_alpha2"], (0, chip * wa2_sh), (GATE_RANK, wa2_sh))
    grads["w_ada"] = _mm("ada_bwd", "nn", jnp.pad(c_all.T, ((0, 0), (0, LANES - N_DEV))),
                         jnp.pad(dmod_shard, ((0, LANES - N_DEV), (0, 0))), (d, ada_sh, LANES), tm=_tile(d, (1024,)),
                         tn=_tile(ada_sh, (512, 128)), tk=LANES, precision=HIGHEST, a_fn=_silu)

    grads.update(plan.final)

    delta, new_m, new_v = {}, {}, {}
    for k in ["w_ada"] + names:
        delta[k], new_m[k], new_v[k] = _adamw(f"adamw_{k}", w[k], grads[k], mom[k], vel[k])
    tiny = ["b_ada"] + SMALL
    pack = lambda src: _pad_rows(jnp.concatenate([src[k].reshape(-1) for k in tiny]), rows=1).reshape(-1, LANES)
    outs = _adamw("adamw_small", pack(w), pack(grads), pack(mom), pack(vel))
    off = 0
    for k in tiny:
        size = w[k].size
        for dst, src in zip((delta, new_m, new_v), outs):
            dst[k] = src.reshape(-1)[off:off + size].reshape(w[k].shape)
        off += size

    give = lambda src: [src[k].T[None] if k in TRANSPOSED else src[k][None] for k in WEIGHTS]
    return (loss, grad_x[None], *give(grads), *give(delta), *give(new_m), *give(new_v))


def kernel(x, c, w_ada, b_ada, ffn1_w_in, ffn1_w_out, ln1_g, ln1_b, w_mix_in, rel_bias, w_alpha2, b_alpha, gla_norm_g, w_proj_a, w_proj_b, w_mix_out, ln2_g, ln2_b, ffn2_w_in, ffn2_w_out, ln3_g, ln3_b, loss_target, m_w_ada, m_b_ada, m_ffn1_w_in, m_ffn1_w_out, m_ln1_g, m_ln1_b, m_w_mix_in, m_rel_bias, m_w_alpha2, m_b_alpha, m_gla_norm_g, m_w_proj_a, m_w_proj_b, m_w_mix_out, m_ln2_g, m_ln2_b, m_ffn2_w_in, m_ffn2_w_out, m_ln3_g, m_ln3_b, v_w_ada, v_b_ada, v_ffn1_w_in, v_ffn1_w_out, v_ln1_g, v_ln1_b, v_w_mix_in, v_rel_bias, v_w_alpha2, v_b_alpha, v_gla_norm_g, v_w_proj_a, v_w_proj_b, v_w_mix_out, v_ln2_g, v_ln2_b, v_ffn2_w_in, v_ffn2_w_out, v_ln3_g, v_ln3_b):
    return _step(dict(locals()))
```

```python
import functools

import jax
import jax.numpy as jnp
from jax import lax
from jax.experimental import pallas as pl
from jax.experimental.pallas import tpu as pltpu

F32 = jnp.float32
BF16 = jnp.bfloat16
MESH = pl.DeviceIdType.MESH
HIGHEST = lax.Precision.HIGHEST

VMEM_LIMIT_BYTES = 56 * 1024 * 1024
LANES = 128
SUBLANES = 8

CHUNK = 64
A_HEADS = 16
A_HEAD_DIM = 64
A_PAST_CHUNKS = 8
A_BAND = (A_PAST_CHUNKS + 1) * CHUNK
A_PAD = A_PAST_CHUNKS * CHUNK
REL_CLIP = 256
REL_SIZE = REL_CLIP + CHUNK
B_HEADS = 4
GATE_RANK = 16
GATE_TAU = 16.0
N_MOD = 9
DEPTH = 1
ALPHA = (2.0 * DEPTH) ** 0.25
LN_EPS = 1e-5
RMS_EPS = 1e-6
ADAM_LR = 0.001
ADAM_B1 = 0.9
ADAM_B2 = 0.999
ADAM_EPS = 1e-08
ADAM_WD = 0.01
ADAM_STEP = 10
NEG_BIG = -1e30

N_CHIPS = 4
N_DEV = 8


def _cp(*sem):
    return pltpu.CompilerParams(dimension_semantics=sem, vmem_limit_bytes=VMEM_LIMIT_BYTES)


class _Stage:
    def __init__(self, arrays, out_shapes, n_sems, start, finish, aliases=None, relay=None):
        self.arrays, self.out_shapes, self.n_sems = list(arrays), list(out_shapes), n_sems
        self.start, self.finish, self.relay, self.aliases = start, finish, relay, dict(aliases or {})


def _pcall(body, stages, *, name, out_shape, in_specs, out_specs, grid=(), scratch_shapes=(), compiler_params=None):
    single = not isinstance(out_shape, (list, tuple))
    outs = [out_shape] if single else list(out_shape)
    ospecs = [out_specs] if single else list(out_specs)
    in_specs, scratch_shapes = list(in_specs), list(scratch_shapes)
    n_in, n_out, n_sc = len(in_specs), len(outs), len(scratch_shapes)
    stages = list(stages or [])
    c_in = [a for st in stages for a in st.arrays]
    c_out = [o for st in stages for o in st.out_shapes]
    aliases = {}
    io, oo = n_in, n_out
    for st in stages:
        for a, b in st.aliases.items():
            aliases[io + a] = oo + b
        io += len(st.arrays)
        oo += len(st.out_shapes)

    def wrapped(*refs):
        ins = refs[:n_in]
        cins = refs[n_in:n_in + len(c_in)]
        base = n_in + len(c_in)
        mouts = refs[base:base + n_out]
        couts = refs[base + n_out:base + n_out + len(c_out)]
        base += n_out + len(c_out)
        scr = refs[base:base + n_sc]
        sems = refs[base + n_sc:]

        def each(phase):
            i = o = 0
            for k, st in enumerate(stages):
                fn = (st.start, st.relay, st.finish)[phase]
                if fn is not None:
                    fn(cins[i:i + len(st.arrays)], couts[o:o + len(st.out_shapes)], sems[2 * k], sems[2 * k + 1])
                i += len(st.arrays)
                o += len(st.out_shapes)

        if stages and grid:
            step = functools.reduce(lambda acc, a: acc * grid[a] + pl.program_id(a), range(len(grid)), 0)
            steps = functools.reduce(lambda a, b: a * b, grid)
            pl.when(step == 0)(lambda: each(0))
            if any(st.relay for st in stages):
                pl.when(step == (2 * steps) // 3)(lambda: each(1))
            if body is not None:
                body(*ins, *mouts, *scr)
            pl.when(step == steps - 1)(lambda: each(2))
        else:
            each(0)
            each(1)
            if body is not None:
                body(*ins, *mouts, *scr)
            each(2)

    sem_shapes = []
    for st in stages:
        sem_shapes += [pltpu.SemaphoreType.DMA((st.n_sems,)), pltpu.SemaphoreType.DMA((st.n_sems,))]
    kwargs = dict(grid=grid) if grid else {}
    if compiler_params is not None:
        kwargs["compiler_params"] = compiler_params

    def run(*operands):
        res = pl.pallas_call(
            wrapped, name=name, out_shape=outs + c_out, in_specs=in_specs + [HBM_SPEC] * len(c_in),
            out_specs=ospecs + [HBM_SPEC] * len(c_out), scratch_shapes=scratch_shapes + sem_shapes,
            input_output_aliases=aliases, **kwargs)(*operands, *c_in)
        main = res[0] if single else tuple(res[:n_out])
        if not stages:
            return main
        comm, o = [], n_out
        for st in stages:
            comm.append(list(res[o:o + len(st.out_shapes)]))
            o += len(st.out_shapes)
        return main, comm

    return run


LONG_K = (2048, 1024)


def _tile(n, prefs):
    for t in prefs:
        if t <= n and n % t == 0:
            return t
    return n


_DIMS = {"nn": (((1,), (0,)), ((), ())), "nt": (((1,), (1,)), ((), ())), "tn": (((0,), (0,)), ((), ()))}


def _dot(a, b, mode="nn", precision=None):
    return lax.dot_general(a, b, _DIMS[mode], precision=precision, preferred_element_type=F32)


def _sigmoid(x):
    return 0.5 * jnp.tanh(0.5 * x) + 0.5


EPILOGUE_STRIP = 256


def _strips(n, width=EPILOGUE_STRIP):
    width = width if n % width == 0 else n
    return [slice(j, j + width) for j in range(0, n, width)]


def _mm(name, mode, a, b, mnk, *, tm, tn, tk, out_dtype=F32, precision=None, a_spec=None, b_spec=None,
        out_shape=None, o_spec=None, add=None, a_fn=None, stages=None):
    m, n, k = mnk
    assert m % tm == 0 and n % tn == 0 and k % tk == 0, (name, mnk, tm, tn, tk)
    nk = k // tk
    if a_spec is None:
        a_spec = {"nn": pl.BlockSpec((tm, tk), lambda i, j, kk: (i, kk)),
                  "nt": pl.BlockSpec((tm, tk), lambda i, j, kk: (i, kk)),
                  "tn": pl.BlockSpec((tk, tm), lambda i, j, kk: (kk, i))}[mode]
    if b_spec is None:
        b_spec = {"nn": pl.BlockSpec((tk, tn), lambda i, j, kk: (kk, j)),
                  "nt": pl.BlockSpec((tn, tk), lambda i, j, kk: (j, kk)),
                  "tn": pl.BlockSpec((tk, tn), lambda i, j, kk: (kk, j))}[mode]
    if o_spec is None:
        o_spec = pl.BlockSpec((tm, tn), lambda i, j, kk: (i, j))
    if out_shape is None:
        out_shape = (m, n)
    has_add = add is not None

    def body(*refs):
        a_ref, b_ref = refs[0], refs[1]
        add_ref = refs[2] if has_add else None
        o_ref = refs[3] if has_add else refs[2]
        av = a_ref[...]
        if a_fn is not None:
            av = a_fn(av)
        part = _dot(av, b_ref[...], mode, precision)

        def finish(total):
            if has_add:
                total = total + add_ref[...]
            o_ref[...] = total.astype(out_dtype)

        if nk == 1:
            finish(part)
        else:
            acc_ref = refs[-1]
            kk = pl.program_id(2)

            @pl.when(kk == 0)
            def _():
                acc_ref[...] = part

            @pl.when(kk > 0)
            def _():
                acc_ref[...] += part

            @pl.when(kk == nk - 1)
            def _():
                finish(acc_ref[...])

    in_specs = [a_spec, b_spec]
    operands = [a, b]
    if has_add:
        in_specs.append(pl.BlockSpec((tm, tn), lambda i, j, kk: (i, j)))
        operands.append(add)
    return _pcall(
        body, stages, name=name, out_shape=jax.ShapeDtypeStruct(out_shape, out_dtype), grid=(m // tm, n // tn, nk),
        in_specs=in_specs, out_specs=o_spec,
        scratch_shapes=[pltpu.VMEM((tm, tn), F32)] if nk > 1 else [],
        compiler_params=_cp("arbitrary", "arbitrary", "arbitrary") if stages else _cp("parallel", "parallel", "arbitrary"),
    )(*operands)


def _row_spec(tr, d):
    return pl.BlockSpec((tr, d), lambda i: (i, 0))


def _vec_spec(d, rows=1):
    return pl.BlockSpec((rows, d), lambda i: (0, 0))


def _modulate(name, x, sh, sc):
    s, d = x.shape
    tr = _tile(s, (512, 256))

    def body(x_ref, sh_ref, sc_ref, o_ref):
        o_ref[...] = (x_ref[...] * (1.0 + sc_ref[...]) + sh_ref[...]).astype(BF16)

    return pl.pallas_call(
        body, name=name, out_shape=jax.ShapeDtypeStruct((s, d), BF16), grid=(s // tr,),
        in_specs=[_row_spec(tr, d), _vec_spec(d), _vec_spec(d)], out_specs=_row_spec(tr, d),
        compiler_params=_cp("parallel"),
    )(x, sh, sc)


def _ln_stats(r):
    mu = jnp.mean(r, axis=-1, keepdims=True)
    xc = r - mu
    var = jnp.mean(xc * xc, axis=-1, keepdims=True)
    rstd = lax.rsqrt(var + LN_EPS)
    return xc * rstd, rstd


def _resid_ln_fwd(name, x, f, gate, ln_g, ln_b, sh_n, sc_n, coef):
    s, d = x.shape
    tr = _tile(s, (256,))

    def body(x_ref, f_ref, gate_ref, g_ref, b_ref, sh_ref, sc_ref, h_ref, u_ref):
        r = ALPHA * x_ref[...] + (coef * gate_ref[...]) * f_ref[...]
        xhat, _ = _ln_stats(r)
        h = xhat * g_ref[...] + b_ref[...]
        h_ref[...] = h
        u_ref[...] = (h * (1.0 + sc_ref[...]) + sh_ref[...]).astype(BF16)

    return pl.pallas_call(
        body, name=name, out_shape=(jax.ShapeDtypeStruct((s, d), F32), jax.ShapeDtypeStruct((s, d), BF16)),
        grid=(s // tr,), in_specs=[_row_spec(tr, d), _row_spec(tr, d)] + [_vec_spec(d)] * 5,
        out_specs=(_row_spec(tr, d), _row_spec(tr, d)), compiler_params=_cp("parallel"),
    )(x, f, gate, ln_g, ln_b, sh_n, sc_n)


ROW_DSC, ROW_DSH, ROW_DLN_G, ROW_DLN_B, ROW_DGATE, ROW_LOSS = 0, 1, 2, 3, 4, 5


def _ln_bwd_core(dy, xhat, rstd, ln_g):
    dxhat = dy * ln_g
    m1 = jnp.mean(dxhat, axis=-1, keepdims=True)
    m2 = jnp.mean(dxhat * xhat, axis=-1, keepdims=True)
    return rstd * (dxhat - m1 - xhat * m2)


def _colsum(v):
    return jnp.sum(v, axis=0, keepdims=True)


def _final_ln_loss_bwd(name, x, f, target, gate, ln_g, ln_b, coef):
    s, d = x.shape
    tr = _tile(s, (256,))
    inv_d = 1.0 / d

    def body(x_ref, f_ref, t_ref, gate_ref, g_ref, b_ref, dr_ref, df_ref, acc_ref):
        @pl.when(pl.program_id(0) == 0)
        def _():
            acc_ref[...] = jnp.zeros_like(acc_ref)

        fv = f_ref[...]
        r = ALPHA * x_ref[...] + (coef * gate_ref[...]) * fv
        xhat, rstd = _ln_stats(r)
        h = xhat * g_ref[...] + b_ref[...]
        err = h - t_ref[...]
        dy = err * inv_d
        dr = _ln_bwd_core(dy, xhat, rstd, g_ref[...])
        dr_ref[...] = dr
        df_ref[...] = ((coef * gate_ref[...]) * dr).astype(BF16)
        acc_ref[ROW_DLN_G:ROW_DLN_G + 1, :] += _colsum(dy * xhat)
        acc_ref[ROW_DLN_B:ROW_DLN_B + 1, :] += _colsum(dy)
        acc_ref[ROW_DGATE:ROW_DGATE + 1, :] += _colsum((coef * dr) * fv)
        acc_ref[ROW_LOSS:ROW_LOSS + 1, :] += _colsum(err * err) * (0.5 * inv_d)

    return pl.pallas_call(
        body, name=name,
        out_shape=(jax.ShapeDtypeStruct((s, d), F32), jax.ShapeDtypeStruct((s, d), BF16),
                   jax.ShapeDtypeStruct((SUBLANES, d), F32)),
        grid=(s // tr,), in_specs=[_row_spec(tr, d)] * 3 + [_vec_spec(d)] * 3,
        out_specs=(_row_spec(tr, d), _row_spec(tr, d), _vec_spec(d, SUBLANES)),
        compiler_params=_cp("arbitrary"),
    )(x, f, target, gate, ln_g, ln_b)


def _resid_ln_bwd(name, du_n, dr_n, x, f, sc_n, gate, ln_g, ln_b, coef):
    s, d = x.shape
    tr = _tile(s, (256,))

    def body(du_ref, drn_ref, x_ref, f_ref, sc_ref, gate_ref, g_ref, b_ref, dr_ref, df_ref, acc_ref):
        @pl.when(pl.program_id(0) == 0)
        def _():
            acc_ref[...] = jnp.zeros_like(acc_ref)

        fv = f_ref[...]
        du = du_ref[...]
        r = ALPHA * x_ref[...] + (coef * gate_ref[...]) * fv
        xhat, rstd = _ln_stats(r)
        h = xhat * g_ref[...] + b_ref[...]
        dy = du * (1.0 + sc_ref[...]) + ALPHA * drn_ref[...]
        dr = _ln_bwd_core(dy, xhat, rstd, g_ref[...])
        dr_ref[...] = dr
        df_ref[...] = ((coef * gate_ref[...]) * dr).astype(BF16)
        acc_ref[ROW_DSC:ROW_DSC + 1, :] += _colsum(du * h)
        acc_ref[ROW_DSH:ROW_DSH + 1, :] += _colsum(du)
        acc_ref[ROW_DLN_G:ROW_DLN_G + 1, :] += _colsum(dy * xhat)
        acc_ref[ROW_DLN_B:ROW_DLN_B + 1, :] += _colsum(dy)
        acc_ref[ROW_DGATE:ROW_DGATE + 1, :] += _colsum((coef * dr) * fv)

    return pl.pallas_call(
        body, name=name,
        out_shape=(jax.ShapeDtypeStruct((s, d), F32), jax.ShapeDtypeStruct((s, d), BF16),
                   jax.ShapeDtypeStruct((SUBLANES, d), F32)),
        grid=(s // tr,), in_specs=[_row_spec(tr, d)] * 4 + [_vec_spec(d)] * 4,
        out_specs=(_row_spec(tr, d), _row_spec(tr, d), _vec_spec(d, SUBLANES)),
        compiler_params=_cp("arbitrary"),
    )(du_n, dr_n, x, f, sc_n, gate, ln_g, ln_b)


def _input_grad(name, du, dr, x, sc):
    s, d = x.shape
    tr = _tile(s, (256,))

    def body(du_ref, dr_ref, x_ref, sc_ref, gx_ref, acc_ref):
        @pl.when(pl.program_id(0) == 0)
        def _():
            acc_ref[...] = jnp.zeros_like(acc_ref)

        du = du_ref[...]
        gx_ref[...] = du * (1.0 + sc_ref[...]) + ALPHA * dr_ref[...]
        acc_ref[ROW_DSC:ROW_DSC + 1, :] += _colsum(du * x_ref[...])
        acc_ref[ROW_DSH:ROW_DSH + 1, :] += _colsum(du)

    return pl.pallas_call(
        body, name=name,
        out_shape=(jax.ShapeDtypeStruct((s, d), F32), jax.ShapeDtypeStruct((SUBLANES, d), F32)),
        grid=(s // tr,), in_specs=[_row_spec(tr, d)] * 3 + [_vec_spec(d)],
        out_specs=(_row_spec(tr, d), _vec_spec(d, SUBLANES)), compiler_params=_cp("arbitrary"),
    )(du, dr, x, sc)


def _ffn_in_fwd(name, u, w_in, stages=None):
    s, d = u.shape
    cs = w_in.shape[2]
    f = 2 * cs
    tm, tn = _tile(s, (2048, 1024, 512)), _tile(cs, (256, 128))
    nb = f // tn
    nbs = cs // tn

    def body(u_ref, wa_ref, wb_ref, ab_ref, act_ref):
        for rows in _strips(tm, 512):
            uv = u_ref[rows, :]
            a = _dot(uv, wa_ref[...])
            b = _dot(uv, wb_ref[...])
            ab_ref[0, rows, :] = a.astype(BF16)
            ab_ref[1, rows, :] = b.astype(BF16)
            act_ref[rows, :] = (a * _sigmoid(a) * b).astype(BF16)

    return _pcall(
        body, stages, name=name,
        out_shape=(jax.ShapeDtypeStruct((2, s, f), BF16), jax.ShapeDtypeStruct((s, f), BF16)),
        grid=(s // tm, nb),
        in_specs=[pl.BlockSpec((tm, d), lambda i, j: (i, 0)),
                  pl.BlockSpec((None, d, tn), lambda i, j: (j // nbs, 0, j % nbs)),
                  pl.BlockSpec((None, d, tn), lambda i, j: (2 + j // nbs, 0, j % nbs))],
        out_specs=(pl.BlockSpec((2, tm, tn), lambda i, j: (0, i, j)), pl.BlockSpec((tm, tn), lambda i, j: (i, j))),
        compiler_params=_cp("arbitrary", "arbitrary"),
    )(u, w_in, w_in)


def _ffn_out_bwd(name, df, w_out, ab, stages=None):
    s, d = df.shape
    f = w_out.shape[0]
    tm, tn = _tile(s, (1024, 512)), _tile(f, (512, 256, 128))

    def body(df_ref, w_ref, ab_ref, dab_ref):
        dfv = df_ref[...]
        for cols in _strips(tn):
            dact = _dot(dfv, w_ref[cols, :], "nt")
            a = ab_ref[0, :, cols].astype(F32)
            b = ab_ref[1, :, cols].astype(F32)
            sg = _sigmoid(a)
            dab_ref[0, :, cols] = (dact * b * (sg * (1.0 + a * (1.0 - sg)))).astype(BF16)
            dab_ref[1, :, cols] = (dact * (a * sg)).astype(BF16)

    return _pcall(
        body, stages, name=name, out_shape=jax.ShapeDtypeStruct((2, s, f), BF16), grid=(s // tm, f // tn),
        in_specs=[pl.BlockSpec((tm, d), lambda i, j: (i, 0)), pl.BlockSpec((tn, d), lambda i, j: (j, 0)),
                  pl.BlockSpec((2, tm, tn), lambda i, j: (0, i, j))],
        out_specs=pl.BlockSpec((2, tm, tn), lambda i, j: (0, i, j)),
        compiler_params=_cp("arbitrary", "arbitrary"),
    )(df, w_out, ab)


def _ffn_forward(tag, u, plan):
    w_in, w_out = plan.weight(f"{tag}_w_in"), plan.weight(f"{tag}_w_out")
    s, d = u.shape
    f = w_out.shape[0]
    ab, act = plan.host(f"{tag}_in_fwd", lambda st: _ffn_in_fwd(f"{tag}_in_fwd", u, w_in, st))
    out = plan.host(f"{tag}_out_fwd", lambda st: _mm(
        f"{tag}_out_fwd", "nn", act, w_out, (s, d, f), tm=_tile(s, (1024,)), tn=_tile(d, (1024,)),
        tk=_tile(f, (2816, 1408, 512, 128)), stages=st))
    return out, (ab, act)


def _ffn_backward(tag, df, u, saved, plan, in_first):
    w_in, w_out = plan.weight(f"{tag}_w_in"), plan.weight(f"{tag}_w_out")
    ab, act = saved
    s, d = u.shape
    f = w_out.shape[0]
    dab = plan.host(f"{tag}_out_bwd", lambda st: _ffn_out_bwd(f"{tag}_out_bwd", df, w_out, ab, st))
    cs = w_in.shape[2]
    tk = _tile(cs, (2816, 1408, 256, 128))
    nkh, nks = f // tk, cs // tk
    tmd = _tile(d, (1024,))
    tn = _tile(cs, (1408, 256, 128))
    nbh, nbs = f // tn, cs // tn
    tks = _tile(s, LONG_K)

    def dw_in():
        plan.grad(f"{tag}_w_in", plan.host(f"{tag}_dw_in", lambda st: _mm(
            f"{tag}_dw_in", "tn", u, dab, (d, 2 * f, s), tm=tmd, tn=tn, tk=tks,
            b_spec=pl.BlockSpec((None, tks, tn), lambda i, j, kk: (j // nbh, kk, j % nbh)), out_shape=(N_CHIPS, d, cs),
            o_spec=pl.BlockSpec((None, tmd, tn), lambda i, j, kk: (j // nbs, i, j % nbs)), stages=st)))

    def dw_out():
        plan.grad(f"{tag}_w_out", plan.host(f"{tag}_dw_out", lambda st: _mm(
            f"{tag}_dw_out", "tn", act, df, (f, d, s), tm=_tile(f, (1408, 512, 128)), tn=tmd, tk=tks, stages=st)))

    for step in ((dw_in, dw_out) if in_first else (dw_out, dw_in)):
        step()
    return plan.host(f"{tag}_du", lambda st: _mm(
        f"{tag}_du", "nt", dab, w_in, (s, d, 2 * f), tm=_tile(s, (1024,)), tn=tmd, tk=tk,
        a_spec=pl.BlockSpec((None, _tile(s, (1024,)), tk), lambda i, j, kk: (kk // nkh, i, kk % nkh)),
        b_spec=pl.BlockSpec((None, tmd, tk), lambda i, j, kk: (kk // nks, j, kk % nks)), stages=st))


ATTN_Q = 4 * CHUNK
ATTN_W = ATTN_Q + A_PAD


def _band_bias(bias):
    n = ATTN_Q // CHUNK
    rows = [jnp.pad(bias, ((0, 0), (0, 0), (i * CHUNK, (n - 1 - i) * CHUNK)), constant_values=NEG_BIG)
            for i in range(n)]
    return jnp.concatenate(rows, axis=1)


def _band_bias_grad(dband):
    n = ATTN_Q // CHUNK
    parts = [dband[:, i * CHUNK:(i + 1) * CHUNK, i * CHUNK:i * CHUNK + A_BAND] for i in range(n)]
    return functools.reduce(jnp.add, parts)


def _attn_probs(q, kw, bias, key0):
    sc = _dot(q, kw, "nt") * (A_HEAD_DIM ** -0.5) + bias
    ks = lax.broadcasted_iota(jnp.int32, sc.shape, 1)
    sc = jnp.where(key0 + ks >= 0, sc, NEG_BIG)
    p = jnp.exp(sc - jnp.max(sc, axis=-1, keepdims=True))
    return p / jnp.sum(p, axis=-1, keepdims=True)


def _head_masks():
    lane = lax.broadcasted_iota(jnp.int32, (1, LANES), 1)
    return [lane // A_HEAD_DIM == h for h in range(LANES // A_HEAD_DIM)]


def _attn_fwd(p1, kvp, band, stages=None):
    s = p1.shape[0]
    aw = A_HEADS * A_HEAD_DIM
    nblk = aw // LANES
    hpb = LANES // A_HEAD_DIM
    assert s % ATTN_Q == 0

    def body(q_ref, k_ref, v_ref, b_ref, o_ref):
        base = pl.multiple_of(pl.program_id(1) * ATTN_Q, ATTN_Q)
        qv = q_ref[...]
        kw = k_ref[pl.ds(base, ATTN_W), :]
        vw = v_ref[pl.ds(base, ATTN_W), :]
        out = jnp.zeros((ATTN_Q, LANES), F32)
        for h, mask in enumerate(_head_masks()):
            p = _attn_probs(jnp.where(mask, qv, jnp.zeros_like(qv)), kw, b_ref[h], base - A_PAD)
            out = jnp.where(mask, _dot(p.astype(BF16), vw), out)
        o_ref[...] = out.astype(BF16)

    kv_rows = s + A_PAD
    return _pcall(
        body, stages, name="attn_fwd", out_shape=jax.ShapeDtypeStruct((s, aw), BF16), grid=(nblk, s // ATTN_Q),
        in_specs=[pl.BlockSpec((ATTN_Q, LANES), lambda b, i: (i, b)),
                  pl.BlockSpec((kv_rows, LANES), lambda b, i: (0, b)),
                  pl.BlockSpec((kv_rows, LANES), lambda b, i: (0, nblk + b)),
                  pl.BlockSpec((hpb, ATTN_Q, ATTN_W), lambda b, i: (b, 0, 0))],
        out_specs=pl.BlockSpec((ATTN_Q, LANES), lambda b, i: (i, b)),
        compiler_params=_cp("arbitrary", "arbitrary"),
    )(p1, kvp, kvp, band)


def _attn_bwd(p1, kvp, band, dya, stages=None):
    s = p1.shape[0]
    aw = A_HEADS * A_HEAD_DIM
    nblk = aw // LANES
    hpb = LANES // A_HEAD_DIM
    scale = A_HEAD_DIM ** -0.5

    def body(q_ref, k_ref, v_ref, b_ref, do_ref, dq_ref, dk_ref, dv_ref, db_ref):
        @pl.when(pl.program_id(1) == 0)
        def _():
            dk_ref[...] = jnp.zeros_like(dk_ref)
            dv_ref[...] = jnp.zeros_like(dv_ref)
            db_ref[...] = jnp.zeros_like(db_ref)

        base = pl.multiple_of(pl.program_id(1) * ATTN_Q, ATTN_Q)
        window = pl.ds(base, ATTN_W)
        kw = k_ref[window, :]
        vw = v_ref[window, :]
        qv = q_ref[...]
        dov = do_ref[...]
        dq = jnp.zeros((ATTN_Q, LANES), F32)
        dk = jnp.zeros((ATTN_W, LANES), F32)
        dv = jnp.zeros((ATTN_W, LANES), F32)
        for h, mask in enumerate(_head_masks()):
            qh = jnp.where(mask, qv, jnp.zeros_like(qv))
            doh = jnp.where(mask, dov, jnp.zeros_like(dov))
            p = _attn_probs(qh, kw, b_ref[h], base - A_PAD)
            dp = _dot(doh, vw, "nt")
            ds = p * (dp - jnp.sum(p * dp, axis=-1, keepdims=True))
            db_ref[h] += ds
            dsb = (ds * scale).astype(BF16)
            dq = jnp.where(mask, _dot(dsb, kw), dq)
            dk = dk + _dot(dsb, qh, "tn")
            dv = dv + _dot(p.astype(BF16), doh, "tn")
        dq_ref[...] = dq.astype(BF16)
        dk_ref[window, :] += dk
        dv_ref[window, :] += dv

    kv_rows = s + A_PAD
    q_spec = pl.BlockSpec((ATTN_Q, LANES), lambda b, i: (i, b))
    acc_spec = pl.BlockSpec((kv_rows, LANES), lambda b, i: (0, b))
    b_spec = pl.BlockSpec((hpb, ATTN_Q, ATTN_W), lambda b, i: (b, 0, 0))
    return _pcall(
        body, stages, name="attn_bwd",
        out_shape=(jax.ShapeDtypeStruct((s, aw), BF16), jax.ShapeDtypeStruct((kv_rows, aw), F32),
                   jax.ShapeDtypeStruct((kv_rows, aw), F32), jax.ShapeDtypeStruct((A_HEADS, ATTN_Q, ATTN_W), F32)),
        grid=(nblk, s // ATTN_Q),
        in_specs=[q_spec, acc_spec, pl.BlockSpec((kv_rows, LANES), lambda b, i: (0, nblk + b)), b_spec, q_spec],
        out_specs=(q_spec, acc_spec, acc_spec, b_spec), compiler_params=_cp("arbitrary", "arbitrary"),
    )(p1, kvp, kvp, band, dya)


def _rel_onehot():
    qi = jnp.arange(CHUNK)[:, None]
    ks = jnp.arange(A_BAND)[None, :]
    idx = (jnp.clip(ks - A_PAD - qi, -REL_CLIP, CHUNK - 1) + REL_CLIP).reshape(1, CHUNK * A_BAND)
    return (jnp.arange(REL_SIZE)[:, None] == idx).astype(F32)


def _gla_gate(lr, wa2, balpha):
    z = _dot(lr, wa2) + balpha
    la = (jnp.minimum(z, 0.0) - jnp.log(1.0 + jnp.exp(-jnp.abs(z)))) * (1.0 / GATE_TAU)
    row = lax.broadcasted_iota(jnp.int32, (CHUNK, CHUNK), 0)
    col = lax.broadcasted_iota(jnp.int32, (CHUNK, CHUNK), 1)
    cum = _dot((row >= col).astype(F32), la, precision=HIGHEST)
    return z, la, cum


def _gla_dims(p2):
    kd = p2.shape[1] // 6
    hk = kd // B_HEADS
    hv = 2 * hk
    return kd, hk, hv


def _gla_fwd(p2, lrp, wa2p, balpha, gnorm, stages=None):
    s = p2.shape[0]
    kd, hk, hv = _gla_dims(p2)
    nc = s // CHUNK
    qscale = hk ** -0.5

    def body(p_ref, lr_ref, wa_ref, ba_ref, gn_ref, yb_ref, st_ref, state):
        @pl.when(pl.program_id(0) == 0)
        def _():
            state[...] = jnp.zeros_like(state)

        _, _, cum = _gla_gate(lr_ref[...], wa_ref[...], ba_ref[...])
        last = cum[CHUNK - 1:CHUNK, :]
        e = jnp.exp(last - cum)
        dch = jnp.exp(last)
        gn = gn_ref[...]
        for hh in range(B_HEADS):
            ks = slice(hh * hk, (hh + 1) * hk)
            q = p_ref[:, hh * hk:(hh + 1) * hk].astype(F32)
            k = p_ref[:, kd + hh * hk:kd + (hh + 1) * hk].astype(F32)
            v = p_ref[:, 2 * kd + hh * hv:2 * kd + (hh + 1) * hv]
            rg = p_ref[:, 4 * kd + hh * hv:4 * kd + (hh + 1) * hv].astype(F32)
            kdec = (k * e[:, ks]).astype(BF16)
            st = state[hh] * dch[:, ks] + _dot(v, kdec, "tn")
            state[hh] = st
            st_ref[hh] = st
            o = _dot((q * qscale).astype(BF16), st.astype(BF16), "nt")
            rinv = lax.rsqrt(jnp.mean(o * o, axis=-1, keepdims=True) + RMS_EPS)
            yb_ref[:, hh * hv:(hh + 1) * hv] = ((o * rinv * gn) * (rg * _sigmoid(rg))).astype(BF16)

    return _pcall(
        body, stages, name="gla_fwd",
        out_shape=(jax.ShapeDtypeStruct((s, 2 * kd), BF16), jax.ShapeDtypeStruct((nc, B_HEADS, hv, hk), F32)),
        grid=(nc,),
        in_specs=[pl.BlockSpec((CHUNK, 6 * kd), lambda i: (i, 0)), pl.BlockSpec((CHUNK, LANES), lambda i: (i, 0)),
                  pl.BlockSpec((LANES, kd), lambda i: (0, 0)), pl.BlockSpec((1, kd), lambda i: (0, 0)),
                  pl.BlockSpec((1, hv), lambda i: (0, 0))],
        out_specs=(pl.BlockSpec((CHUNK, 2 * kd), lambda i: (i, 0)),
                   pl.BlockSpec((None, B_HEADS, hv, hk), lambda i: (i, 0, 0, 0))),
        scratch_shapes=[pltpu.VMEM((B_HEADS, hv, hk), F32)], compiler_params=_cp("arbitrary"),
    )(p2, lrp, wa2p, balpha, gnorm)


GLA_ROW_DBALPHA, GLA_ROW_DGNORM = 0, 1


def _gla_bwd(p2, lrp, wa2p, balpha, gnorm, states, dyb, stages=None):
    s = p2.shape[0]
    kd, hk, hv = _gla_dims(p2)
    nc = s // CHUNK
    qscale = hk ** -0.5

    def body(p_ref, lr_ref, wa_ref, ba_ref, gn_ref, st_ref, sp_ref, dy_ref, dp_ref, dz_ref, sm_ref, gcar):
        i = pl.program_id(0)

        @pl.when(i == 0)
        def _():
            gcar[...] = jnp.zeros_like(gcar)
            sm_ref[...] = jnp.zeros_like(sm_ref)

        has_prev = (i < nc - 1).astype(F32)
        z, _, cum = _gla_gate(lr_ref[...], wa_ref[...], ba_ref[...])
        last = cum[CHUNK - 1:CHUNK, :]
        e = jnp.exp(last - cum)
        dch = jnp.exp(last)
        sgn = _sigmoid(-z) * (1.0 / GATE_TAU)
        gn = gn_ref[...]
        row = lax.broadcasted_iota(jnp.int32, (CHUNK, CHUNK), 0)
        col = lax.broadcasted_iota(jnp.int32, (CHUNK, CHUNK), 1)
        tri_strict = (row > col).astype(F32)
        for hh in range(B_HEADS):
            ks = slice(hh * hk, (hh + 1) * hk)
            q = p_ref[:, hh * hk:(hh + 1) * hk].astype(F32)
            k = p_ref[:, kd + hh * hk:kd + (hh + 1) * hk].astype(F32)
            v = p_ref[:, 2 * kd + hh * hv:2 * kd + (hh + 1) * hv]
            rg = p_ref[:, 4 * kd + hh * hv:4 * kd + (hh + 1) * hv].astype(F32)
            kdecf = k * e[:, ks]
            kdec = kdecf.astype(BF16)
            st16 = st_ref[hh].astype(BF16)
            qs = (q * qscale).astype(BF16)
            o = _dot(qs, st16, "nt")
            rinv = lax.rsqrt(jnp.mean(o * o, axis=-1, keepdims=True) + RMS_EPS)
            dy = dy_ref[:, hh * hv:(hh + 1) * hv].astype(F32)
            sg = _sigmoid(rg)
            onorm = o * rinv
            drg = dy * (onorm * gn) * (sg * (1.0 + rg * (1.0 - sg)))
            dob = dy * (rg * sg)
            sm_ref[GLA_ROW_DGNORM:GLA_ROW_DGNORM + 1, 0:hv] += _colsum(dob * onorm)
            t = dob * gn
            do = rinv * (t - onorm * jnp.mean(t * onorm, axis=-1, keepdims=True))
            do16 = do.astype(BF16)
            dq = _dot(do16, st16) * qscale
            gt = _dot(do16, qs, "tn") + gcar[hh]
            gcar[hh] = gt * dch[:, ks]
            dd = _colsum(gt * sp_ref[hh]) * has_prev
            gt16 = gt.astype(BF16)
            dkdec = _dot(v, gt16)
            dv = _dot(kdec, gt16, "nt")
            dla = dd * dch[:, ks] + _dot(tri_strict, dkdec * kdecf, precision=HIGHEST)
            dzh = dla * sgn[:, ks]
            sm_ref[GLA_ROW_DBALPHA:GLA_ROW_DBALPHA + 1, hh * hk:(hh + 1) * hk] += _colsum(dzh)
            dz_ref[:, hh * hk:(hh + 1) * hk] = dzh.astype(BF16)
            dp_ref[:, hh * hk:(hh + 1) * hk] = dq.astype(BF16)
            dp_ref[:, kd + hh * hk:kd + (hh + 1) * hk] = (dkdec * e[:, ks]).astype(BF16)
            dp_ref[:, 2 * kd + hh * hv:2 * kd + (hh + 1) * hv] = dv.astype(BF16)
            dp_ref[:, 4 * kd + hh * hv:4 * kd + (hh + 1) * hv] = drg.astype(BF16)

    rev = lambda i: (nc - 1 - i, 0)
    return _pcall(
        body, stages, name="gla_bwd",
        out_shape=(jax.ShapeDtypeStruct((s, 6 * kd), BF16), jax.ShapeDtypeStruct((s, kd), BF16),
                   jax.ShapeDtypeStruct((SUBLANES, kd), F32)),
        grid=(nc,),
        in_specs=[pl.BlockSpec((CHUNK, 6 * kd), rev), pl.BlockSpec((CHUNK, LANES), rev),
                  pl.BlockSpec((LANES, kd), lambda i: (0, 0)), pl.BlockSpec((1, kd), lambda i: (0, 0)),
                  pl.BlockSpec((1, hv), lambda i: (0, 0)),
                  pl.BlockSpec((None, B_HEADS, hv, hk), lambda i: (nc - 1 - i, 0, 0, 0)),
                  pl.BlockSpec((None, B_HEADS, hv, hk), lambda i: (jnp.maximum(nc - 2 - i, 0), 0, 0, 0)),
                  pl.BlockSpec((CHUNK, 2 * kd), rev)],
        out_specs=(pl.BlockSpec((CHUNK, 6 * kd), rev), pl.BlockSpec((CHUNK, kd), rev),
                   pl.BlockSpec((SUBLANES, kd), lambda i: (0, 0))),
        scratch_shapes=[pltpu.VMEM((B_HEADS, hv, hk), F32)], compiler_params=_cp("arbitrary"),
    )(p2, lrp, wa2p, balpha, gnorm, states, states, dyb)


def _merge_fwd(ya, yb, wpa, wpb, g):
    s, ka = ya.shape
    kb = yb.shape[1]
    d = wpa.shape[1]
    tm, tn = _tile(s, (1024, 512)), _tile(d, (512,))

    def body(ya_ref, yb_ref, wa_ref, wb_ref, g_ref, m_ref, pab_ref):
        yav, ybv = ya_ref[...], yb_ref[...]
        for cols in _strips(tn):
            pa = _dot(yav, wa_ref[:, cols])
            pb = _dot(ybv, wb_ref[:, cols])
            m_ref[:, cols] = (_sigmoid(g_ref[0, :, cols].astype(F32)) * pa
                              + _sigmoid(g_ref[1, :, cols].astype(F32)) * pb).astype(BF16)
            pab_ref[0, :, cols] = pa.astype(BF16)
            pab_ref[1, :, cols] = pb.astype(BF16)

    st = pl.BlockSpec((2, tm, tn), lambda i, j: (0, i, j))
    return pl.pallas_call(
        body, name="merge_fwd",
        out_shape=(jax.ShapeDtypeStruct((s, d), BF16), jax.ShapeDtypeStruct((2, s, d), BF16)),
        grid=(s // tm, d // tn),
        in_specs=[pl.BlockSpec((tm, ka), lambda i, j: (i, 0)), pl.BlockSpec((tm, kb), lambda i, j: (i, 0)),
                  pl.BlockSpec((ka, tn), lambda i, j: (0, j)), pl.BlockSpec((kb, tn), lambda i, j: (0, j)), st],
        out_specs=(pl.BlockSpec((tm, tn), lambda i, j: (i, j)), st),
        compiler_params=_cp("parallel", "parallel"),
    )(ya, yb, wpa, wpb, g)


def _merge_bwd(dm, wmo, g, pab, stages=None):
    s, d = dm.shape
    tm, tn = _tile(s, (1024, 512)), _tile(d, (512,))

    def body(dm_ref, w_ref, g_ref, pab_ref, dpab_ref, dg_ref):
        dmv = dm_ref[...]
        for cols in _strips(tn):
            dmg = _dot(dmv, w_ref[cols, :], "nt")
            for j in range(2):
                sg = _sigmoid(g_ref[j, :, cols].astype(F32))
                dpab_ref[j, :, cols] = (dmg * sg).astype(BF16)
                dg_ref[j, :, cols] = (dmg * pab_ref[j, :, cols].astype(F32) * (sg * (1.0 - sg))).astype(BF16)

    st = pl.BlockSpec((2, tm, tn), lambda i, j: (0, i, j))
    return _pcall(
        body, stages, name="merge_bwd",
        out_shape=(jax.ShapeDtypeStruct((2, s, d), BF16), jax.ShapeDtypeStruct((2, s, d), BF16)),
        grid=(s // tm, d // tn),
        in_specs=[pl.BlockSpec((tm, d), lambda i, j: (i, 0)), pl.BlockSpec((tn, d), lambda i, j: (j, 0)), st, st],
        out_specs=(st, st), compiler_params=_cp("arbitrary", "arbitrary"),
    )(dm, wmo, g, pab)


def _virtual_rows(parts, lo, hi):
    out, off = [], 0
    for p in parts:
        a, b = max(lo, off), min(hi, off + p.shape[0])
        if a < b:
            out.append(p[a - off:b - off])
        off += p.shape[0]
    return out[0] if len(out) == 1 else jnp.concatenate(out, axis=0)


def _mix_in_row_groups(d):
    o1 = 3 * A_HEADS * A_HEAD_DIM
    o2 = o1 + 6 * (d // 4)
    o3 = o2 + GATE_RANK
    return (0, o1), (o1, o2), (o2, o3), (o3, o3 + 2 * d)


def _split_mix_in(stacked):
    d = stacked.shape[2]
    flat = stacked.reshape(-1, d)
    _, _, (lo, hi), (glo, ghi) = _mix_in_row_groups(d)
    return flat, jnp.pad(flat[lo:hi], ((0, LANES - GATE_RANK), (0, 0))), flat[glo:ghi]


MIX_TILE = 1024


def _mix_in_weights(plan):
    return plan.memo("mix_in_weights", lambda: _split_mix_in(plan.weight("w_mix_in")))


def _hosted_mm(plan):
    return lambda name, *a, **k: plan.host(name, lambda st: _mm(name, *a, stages=st, **k))


def _mix_forward(u2, plan, small):
    s, d = u2.shape
    wt, wt_lr, wt_g = _mix_in_weights(plan)
    bias, wa2p, balpha, gnorm = small
    mm = _hosted_mm(plan)
    aw = A_HEADS * A_HEAD_DIM
    tm, tn = _tile(s, (1024,)), MIX_TILE
    (_, na), (_, nab) = _mix_in_row_groups(d)[:2]
    assert na % tn == 0 and nab % tn == 0
    p1 = mm("mix_in_a", "nt", u2, wt, (s, na, d), tm=tm, tn=tn, tk=d, out_dtype=BF16)
    p2 = mm("mix_in_b", "nt", u2, wt, (s, nab - na, d), tm=tm, tn=tn, tk=d, out_dtype=BF16,
            b_spec=pl.BlockSpec((tn, d), lambda i, j, kk: (na // tn + j, 0)))
    lrp = mm("mix_in_lr", "nt", u2, wt_lr, (s, LANES, d), tm=tm, tn=LANES, tk=d, out_dtype=BF16)
    nbg = d // tn
    g = mm("mix_in_g", "nt", u2, wt_g, (s, 2 * d, d), tm=tm, tn=tn, tk=d, out_dtype=BF16, out_shape=(2, s, d),
           o_spec=pl.BlockSpec((None, tm, tn), lambda i, j, kk: (j // nbg, i, j % nbg)))
    kvp = jnp.pad(p1[:, aw:], ((A_PAD, 0), (0, 0)))
    ya = plan.host("attn_fwd", lambda st: _attn_fwd(p1, kvp, bias, st))
    yb, states = plan.host("gla_fwd", lambda st: _gla_fwd(p2, lrp, wa2p, balpha, gnorm, st))
    merged, pab = _merge_fwd(ya, yb, plan.weight("w_proj_a"), plan.weight("w_proj_b"), g)
    m = mm("mix_out", "nn", merged, plan.weight("w_mix_out"), (s, d, d), tm=tm, tn=tn, tk=d)
    return m, (p1, kvp, p2, lrp, states, ya, yb, g, pab, merged)


def _mix_backward(dm, u2, saved, plan, small):
    s, d = u2.shape
    wt, wt_lr, wt_g = _mix_in_weights(plan)
    wpa, wpb, wmo = plan.weight("w_proj_a"), plan.weight("w_proj_b"), plan.weight("w_mix_out")
    bias, wa2p, balpha, gnorm = small
    p1, kvp, p2, lrp, states, ya, yb, g, pab, merged = saved
    mm = _hosted_mm(plan)
    aw = A_HEADS * A_HEAD_DIM
    kd = d // 4
    t = MIX_TILE
    tm = _tile(s, (1024,))
    tks = _tile(s, LONG_K)

    plan.grad("w_mix_out", mm("mix_dw_out", "tn", merged, dm, (d, d, s), tm=t, tn=t, tk=tks))
    dpab, dg = plan.host("merge_bwd", lambda st: _merge_bwd(dm, wmo, g, pab, st))
    sel = lambda j: pl.BlockSpec((None, tm, d), lambda i, jj, kk: (j, i, 0))
    dya = mm("mix_dya", "nt", dpab, wpa, (s, aw, d), tm=tm, tn=t, tk=d, out_dtype=BF16, a_spec=sel(0))
    dyb = mm("mix_dyb", "nt", dpab, wpb, (s, 2 * kd, d), tm=tm, tn=t, tk=d, out_dtype=BF16, a_spec=sel(1))
    selk = lambda j: pl.BlockSpec((None, tks, t), lambda i, jj, kk: (j, kk, jj))
    plan.grad("w_proj_a", mm("mix_dwpa", "tn", ya, dpab, (aw, d, s), tm=t, tn=t, tk=tks, b_spec=selk(0)))
    plan.grad("w_proj_b", mm("mix_dwpb", "tn", yb, dpab, (2 * kd, d, s), tm=t, tn=t, tk=tks, b_spec=selk(1)))

    dq, dkp, dvp, dbias = plan.host("attn_bwd", lambda st: _attn_bwd(p1, kvp, bias, dya, st))
    dp1 = jnp.concatenate([dq, dkp[A_PAD:].astype(BF16), dvp[A_PAD:].astype(BF16)], axis=1)
    dp2, dz, gsm = plan.host("gla_bwd", lambda st: _gla_bwd(p2, lrp, wa2p, balpha, gnorm, states, dyb, st))
    dlrp = mm("gla_dlr", "nt", dz, wa2p, (s, LANES, kd), tm=tm, tn=LANES, tk=kd, out_dtype=BF16)
    dwa2p = mm("gla_dwa2", "tn", lrp, dz, (LANES, kd, s), tm=LANES, tn=kd, tk=tks)

    tka = 3 * aw
    assert 6 * kd == tka
    du = mm("mix_du_a", "nn", dp1, wt, (s, d, tka), tm=tm, tn=t, tk=tka)
    du = mm("mix_du_b", "nn", dp2, wt, (s, d, tka), tm=tm, tn=t, tk=tka, add=du,
            b_spec=pl.BlockSpec((tka, t), lambda i, j, kk: (1 + kk, j)))
    du = mm("mix_du_lr", "nn", dlrp, wt_lr, (s, d, LANES), tm=tm, tn=t, tk=LANES, add=du)
    du = mm("mix_du_g", "nn", dg, wt_g, (s, d, 2 * d), tm=tm, tn=t, tk=d, add=du,
            a_spec=pl.BlockSpec((None, tm, d), lambda i, j, kk: (kk, i, 0)))
    nkg = d // t
    dw1 = mm("mix_dw_a", "tn", dp1, u2, (3 * aw, d, s), tm=t, tn=t, tk=tks)
    dw2 = mm("mix_dw_b", "tn", dp2, u2, (6 * kd, d, s), tm=t, tn=t, tk=tks)
    dwlr = mm("mix_dw_lr", "tn", dlrp, u2, (LANES, d, s), tm=LANES, tn=t, tk=tks)
    dwg = mm("mix_dw_g", "tn", dg, u2, (2 * d, d, s), tm=t, tn=t, tk=tks,
             a_spec=pl.BlockSpec((None, tks, t), lambda i, j, kk: (i // nkg, kk, i % nkg)))
    pieces = [dw1, dw2, dwlr[:GATE_RANK], dwg]
    shard_rows = sum(p.shape[0] for p in pieces) // N_CHIPS
    plan.grad("w_mix_in", jnp.stack([_virtual_rows(pieces, j * shard_rows, (j + 1) * shard_rows)
                                     for j in range(N_CHIPS)]))
    return du, (dbias, dwa2p[:GATE_RANK], gsm)


def _device_step(x, target, mod, small, plan):
    s, d = x.shape
    row = lambda i: mod[i:i + 1]
    sh1, sc1, g1, sh2, sc2, g2, sh3, sc3, g3 = (row(i) for i in range(N_MOD))

    onehot = _rel_onehot()
    bias = _mm("rel_bias_expand", "nn", small["rel_bias"], onehot, (A_HEADS, CHUNK * A_BAND, REL_SIZE),
               tm=A_HEADS, tn=4608, tk=REL_SIZE, precision=HIGHEST).reshape(A_HEADS, CHUNK, A_BAND)
    bias = _band_bias(bias)
    wa2p = jnp.pad(small["w_alpha2"], ((0, LANES - GATE_RANK), (0, 0))).astype(BF16)
    mix_small = (bias, wa2p, small["b_alpha"], small["gla_norm_g"])

    u1 = _modulate("mod1", x, sh1, sc1)
    f1, sv1 = _ffn_forward("ffn1", u1, plan)
    h1, u2 = _resid_ln_fwd("ln1_fwd", x, f1, g1, small["ln1_g"], small["ln1_b"], sh2, sc2, 0.5)
    m, svm = _mix_forward(u2, plan, mix_small)
    h2, u3 = _resid_ln_fwd("ln2_fwd", h1, m, g2, small["ln2_g"], small["ln2_b"], sh3, sc3, 1.0)
    f2, sv2 = _ffn_forward("ffn2", u3, plan)

    dr3, df2, acc3 = _final_ln_loss_bwd("ln3_loss_bwd", h2, f2, target, g3, small["ln3_g"], small["ln3_b"], 0.5)
    du3 = _ffn_backward("ffn2", df2, u3, sv2, plan, in_first=False)
    dr2, dmx, acc2 = _resid_ln_bwd("ln2_bwd", du3, dr3, h1, m, sc3, g2, small["ln2_g"], small["ln2_b"], 1.0)
    du2, (dbias, dwa2, gsm) = _mix_backward(dmx, u2, svm, plan, mix_small)
    dr1, df1, acc1 = _resid_ln_bwd("ln1_bwd", du2, dr2, x, f1, sc2, g1, small["ln1_g"], small["ln1_b"], 0.5)
    du1 = _ffn_backward("ffn1", df1, u1, sv1, plan, in_first=True)
    grad_x, acc0 = _input_grad("input_grad", du1, dr1, x, sc1)

    drel = _hosted_mm(plan)("rel_bias_grad", "nt", _band_bias_grad(dbias).reshape(A_HEADS, CHUNK * A_BAND), onehot,
                            (A_HEADS, REL_SIZE, CHUNK * A_BAND), tm=A_HEADS, tn=REL_SIZE, tk=4608, precision=HIGHEST)
    loss = jnp.sum(acc3[ROW_LOSS])
    dmod = jnp.stack([acc0[ROW_DSH], acc0[ROW_DSC], acc1[ROW_DGATE], acc1[ROW_DSH], acc1[ROW_DSC], acc2[ROW_DGATE],
                      acc2[ROW_DSH], acc2[ROW_DSC], acc3[ROW_DGATE]])
    kd = d // 4
    small_grads = dict(ln1_g=acc1[ROW_DLN_G], ln1_b=acc1[ROW_DLN_B], ln2_g=acc2[ROW_DLN_G], ln2_b=acc2[ROW_DLN_B],
                       ln3_g=acc3[ROW_DLN_G], ln3_b=acc3[ROW_DLN_B], b_alpha=gsm[GLA_ROW_DBALPHA],
                       gla_norm_g=gsm[GLA_ROW_DGNORM, :kd // B_HEADS * 2], rel_bias=drel, w_alpha2=dwa2)
    return loss, grad_x, small_grads, dmod


HBM_SPEC = pl.BlockSpec(memory_space=pl.ANY)


def _mesh_pos():
    return lax.axis_index("x"), lax.axis_index("y"), lax.axis_index("c")


def _other_chips(x, y):
    return [(1 - x, y), (x, 1 - y), (1 - x, 1 - y)]


def _remote(src, dst, send_sem, recv_sem, to):
    return pltpu.make_async_remote_copy(src_ref=src, dst_ref=dst, send_sem=send_sem, recv_sem=recv_sem,
                                        device_id=to, device_id_type=MESH)


def _allgather_rows(name, v):
    m_per, n = v.shape

    def body(x_ref, out_ref, send_sems, recv_sems, local_sem):
        x, y, c = _mesh_pos()
        me, sibling = (x, y, c), (x, y, 1 - c)
        chips = _other_chips(x, y)

        def rows(px, py, pc):
            return out_ref.at[pl.ds((4 * px + 2 * py + pc) * m_per, m_per), :]

        def copy(k, block, to, src=None):
            return _remote(rows(*block) if src is None else src, rows(*block), send_sems.at[k], recv_sems.at[k], to)

        mine = pltpu.make_async_copy(x_ref, rows(*me), local_sem)
        mine.start()
        first = [copy(0, me, sibling, src=x_ref)]
        first += [copy(1 + j, me, (*chip, c), src=x_ref) for j, chip in enumerate(chips)]
        for cp in first:
            cp.start()
        passed = [copy(4 + j, (*chip, c), sibling) for j, chip in enumerate(chips)]
        for j, chip in enumerate(chips):
            copy(1 + j, (*chip, c), me).wait_recv()
            passed[j].start()
        copy(0, sibling, me).wait_recv()
        for j, chip in enumerate(chips):
            copy(4 + j, (*chip, 1 - c), me).wait_recv()
        for cp in first + passed:
            cp.wait_send()
        mine.wait()

    return pl.pallas_call(
        body, name=name, out_shape=jax.ShapeDtypeStruct((N_DEV * m_per, n), v.dtype),
        in_specs=[pl.BlockSpec(memory_space=pltpu.VMEM)], out_specs=pl.BlockSpec(memory_space=pltpu.VMEM),
        scratch_shapes=[pltpu.SemaphoreType.DMA((7,)), pltpu.SemaphoreType.DMA((7,)), pltpu.SemaphoreType.DMA],
    )(v)


def _allgather_weights(shards):
    n = len(shards)
    TO_X, TO_Y, PASS_TO_X, PASS_TO_Y, SIB_X, SIB_Y, SIB_D0, SIB_D1 = range(8)

    def body(*refs):
        ins, outs = refs[:n], refs[n:2 * n]
        send_sems, recv_sems = refs[2 * n:]
        x, y, c = _mesh_pos()
        sibling = (x, y, 1 - c)
        xn, yn, dg = _other_chips(x, y)
        j0, jx, jy, jd = (2 * p[0] + p[1] for p in ((x, y), xn, yn, dg))
        sends = []

        def rows(w, hc, quarter=None):
            hr = shards[w].shape[0] // 2
            if quarter is None:
                return pl.ds(hc * hr, hr)
            return pl.ds(hc * hr + quarter * (hr // 2), hr // 2)

        def push(src, dst, w, k, to):
            cp = _remote(src, dst, send_sems.at[w, k], recv_sems.at[w, k], to)
            cp.start()
            sends.append(cp)

        def landed(piece, w, k):
            _remote(piece, piece, send_sems.at[w, k], recv_sems.at[w, k], sibling).wait_recv()

        for w in range(n):
            mine = rows(w, c)
            push(ins[w].at[mine, :], outs[w].at[j0, mine, :], w, TO_X, (*xn, c))
            push(ins[w].at[mine, :], outs[w].at[j0, mine, :], w, TO_Y, (*yn, c))
        for w in range(n):
            half_x = outs[w].at[jx, rows(w, c), :]
            landed(half_x, w, TO_X)
            quarter = outs[w].at[jx, rows(w, c, 1), :]
            push(quarter, quarter, w, PASS_TO_Y, (*yn, c))
            push(half_x, half_x, w, SIB_X, sibling)
            half_y = outs[w].at[jy, rows(w, c), :]
            landed(half_y, w, TO_Y)
            quarter = outs[w].at[jy, rows(w, c, 0), :]
            push(quarter, quarter, w, PASS_TO_X, (*xn, c))
            push(half_y, half_y, w, SIB_Y, sibling)
        for w in range(n):
            for q, arrives_on, on in ((0, PASS_TO_X, SIB_D0), (1, PASS_TO_Y, SIB_D1)):
                piece = outs[w].at[jd, rows(w, c, q), :]
                landed(piece, w, arrives_on)
                push(piece, piece, w, on, sibling)
        for w in range(n):
            landed(outs[w].at[jx, rows(w, 1 - c), :], w, SIB_X)
            landed(outs[w].at[jy, rows(w, 1 - c), :], w, SIB_Y)
            landed(outs[w].at[jd, rows(w, 1 - c, 0), :], w, SIB_D0)
            landed(outs[w].at[jd, rows(w, 1 - c, 1), :], w, SIB_D1)
        for cp in sends:
            cp.wait_send()

    return pl.pallas_call(
        body, name="allgather_weights",
        out_shape=[jax.ShapeDtypeStruct((N_CHIPS,) + sh.shape, sh.dtype) for sh in shards],
        in_specs=[HBM_SPEC] * n, out_specs=[HBM_SPEC] * n,
        scratch_shapes=[pltpu.SemaphoreType.DMA((n, 8)), pltpu.SemaphoreType.DMA((n, 8))],
    )(*shards)


def _half(ref, hc, col, *lead):
    rows, cols = ref.shape[-2:]
    if col:
        return ref.at[(*lead, slice(None), pl.ds(hc * (cols // 2), cols // 2))]
    return ref.at[(*lead, pl.ds(hc * (rows // 2), rows // 2), slice(None))]


def _half_shape(shape, col):
    return shape[:-2] + ((shape[-2], shape[-1] // 2) if col else (shape[-2] // 2, shape[-1]))


def _quarter(ref, hc, q, col, *lead):
    rows, cols = ref.shape[-2:]
    if col:
        return ref.at[(*lead, slice(None), pl.ds(hc * (cols // 2) + q * (cols // 4), cols // 4))]
    return ref.at[(*lead, pl.ds(hc * (rows // 2) + q * (rows // 4), rows // 4), slice(None))]


def _stage_gather_ici(shards, cols):
    n = len(shards)
    TO_X, TO_Y, PASS_TO_X, PASS_TO_Y = range(4)

    def places():
        x, y, c = _mesh_pos()
        xn, yn, dg = _other_chips(x, y)
        return c, (*xn, c), (*yn, c), [2 * p[0] + p[1] for p in ((x, y), xn, yn, dg)]

    def remote(src, dst, send, recv, w, k, to):
        return _remote(src, dst, send.at[4 * w + k], recv.at[4 * w + k], to)

    def own(ins, outs, send, recv):
        c, to_x, to_y, (j0, _, _, _) = places()
        for w in range(n):
            for k, to in ((TO_X, to_x), (TO_Y, to_y)):
                yield remote(_half(ins[w], c, cols[w]), _half(outs[w], c, cols[w], j0), send, recv, w, k, to)

    def relays(ins, outs, send, recv):
        c, to_x, to_y, (_, jx, jy, _) = places()
        for w in range(n):
            for j, k, q, pass_k, to in ((jx, TO_X, 1, PASS_TO_Y, to_y), (jy, TO_Y, 0, PASS_TO_X, to_x)):
                half = _half(outs[w], c, cols[w], j)
                piece = _quarter(outs[w], c, q, cols[w], j)
                yield remote(half, half, send, recv, w, k, to), remote(piece, piece, send, recv, w, pass_k, to)

    def passed(ins, outs, send, recv):
        c, to_x, _, (_, _, _, jd) = places()
        for w in range(n):
            for q, k in ((0, PASS_TO_X), (1, PASS_TO_Y)):
                piece = _quarter(outs[w], c, q, cols[w], jd)
                yield remote(piece, piece, send, recv, w, k, to_x)

    def start(*refs):
        for cp in own(*refs):
            cp.start()

    def relay(*refs):
        for arrived, onward in relays(*refs):
            arrived.wait_recv()
            onward.start()

    def finish(*refs):
        for cp in passed(*refs):
            cp.wait_recv()
        for cp in own(*refs):
            cp.wait_send()
        for _, onward in relays(*refs):
            onward.wait_send()

    outs = [jax.ShapeDtypeStruct((N_CHIPS,) + sh.shape, sh.dtype) for sh in shards]
    return _Stage(shards, outs, 4 * n, start, finish, relay=relay)


def _stage_gather_d2d(partial, cols):
    n = len(partial)

    def copies(ins, outs, send, recv):
        x, y, c = _mesh_pos()
        for w in range(n):
            for r, chip in enumerate(_other_chips(x, y)):
                jr = 2 * chip[0] + chip[1]
                mine = _remote(_half(ins[w], c, cols[w], jr), _half(outs[w], c, cols[w], jr), send.at[3 * w + r],
                               recv.at[3 * w + r], (x, y, 1 - c))
                got = _half(outs[w], 1 - c, cols[w], jr)
                yield mine, _remote(got, got, send.at[3 * w + r], recv.at[3 * w + r], (x, y, 1 - c))

    def start(*refs):
        for mine, _ in copies(*refs):
            mine.start()

    def finish(*refs):
        pairs = list(copies(*refs))
        for _, theirs in pairs:
            theirs.wait_recv()
        for mine, _ in pairs:
            mine.wait_send()

    outs = [jax.ShapeDtypeStruct(p.shape, p.dtype) for p in partial]
    return _Stage(partial, outs, 3 * n, start, finish, aliases={w: w for w in range(n)})


def _stage_exchange_halves(grads, cols):
    n = len(grads)

    def copies(ins, outs, send, recv):
        x, y, c = _mesh_pos()
        for w in range(n):
            yield _remote(_half(ins[w], 1 - c, cols[w], slice(None)), outs[w], send.at[w], recv.at[w], (x, y, 1 - c))

    def start(*refs):
        for cp in copies(*refs):
            cp.start()

    def finish(*refs):
        cps = list(copies(*refs))
        for cp in cps:
            cp.wait_recv()
        for cp in cps:
            cp.wait_send()

    outs = [jax.ShapeDtypeStruct(_half_shape(g.shape, col), g.dtype) for g, col in zip(grads, cols)]
    return _Stage(grads, outs, n, start, finish)


def _stage_scatter(parts):
    n = len(parts)

    def copies(ins, outs, send, recv):
        x, y, c = _mesh_pos()
        for w in range(n):
            for r, chip in enumerate(_other_chips(x, y)):
                jr = 2 * chip[0] + chip[1]
                yield _remote(ins[w].at[jr], outs[w].at[r], send.at[3 * w + r], recv.at[3 * w + r], (*chip, c))

    def start(*refs):
        for cp in copies(*refs):
            cp.start()

    def finish(*refs):
        cps = list(copies(*refs))
        for cp in cps:
            cp.wait_recv()
        for cp in cps:
            cp.wait_send()

    outs = [jax.ShapeDtypeStruct((3,) + p.shape[1:], p.dtype) for p in parts]
    return _Stage(parts, outs, 3 * n, start, finish)


def _stage_share(fulls, cols):
    n = len(fulls)

    def copies(ins, outs, send, recv):
        x, y, c = _mesh_pos()
        for w in range(n):
            theirs = _half(outs[w], 1 - c, cols[w])
            yield (_remote(_half(ins[w], c, cols[w]), _half(outs[w], c, cols[w]), send.at[w], recv.at[w], (x, y, 1 - c)),
                   _remote(theirs, theirs, send.at[w], recv.at[w], (x, y, 1 - c)))

    def start(*refs):
        for mine, _ in copies(*refs):
            mine.start()

    def finish(*refs):
        pairs = list(copies(*refs))
        for _, theirs in pairs:
            theirs.wait_recv()
        for mine, _ in pairs:
            mine.wait_send()

    outs = [jax.ShapeDtypeStruct(h.shape, h.dtype) for h in fulls]
    return _Stage(fulls, outs, n, start, finish, aliases={w: w for w in range(n)})


def _run_stages(name, stages):
    return _pcall(None, stages, name=name, out_shape=[], in_specs=[], out_specs=[])()[1]


TILE_BYTES = 2 * 1024 * 1024


def _row_tile(rows, cols, itemsize=4):
    for t in (1024, 512, 256, 128, 64, 32, 16, 8):
        if rows % t == 0 and t * cols * itemsize <= TILE_BYTES:
            return t
    return rows


def _col_tile(rows, cols, itemsize=4):
    for t in (2048, 1024, 512, 256, 128):
        if cols % t == 0 and t * rows * itemsize <= TILE_BYTES:
            return t
    return cols


def _tiling(rows, cols, col):
    if col:
        tc = _col_tile(rows, cols)
        return (rows, tc), cols // tc
    tr = _row_tile(rows, cols)
    return (tr, cols), rows // tr


def _strip(col, i):
    return (0, i) if col else (i, 0)


def _pair_sum(name, g, recv, core, col):
    blk, nb = _tiling(*recv.shape[1:], col)

    def body(c_ref, g_ref, r_ref, o_ref):
        o_ref[...] = (g_ref[...] + r_ref[...]).astype(BF16)

    grid_spec = pltpu.PrefetchScalarGridSpec(
        num_scalar_prefetch=1, grid=(N_CHIPS, nb),
        in_specs=[pl.BlockSpec((None,) + blk, lambda j, i, cr: (j,) + _strip(col, cr[0] * nb + i)),
                  pl.BlockSpec((None,) + blk, lambda j, i, cr: (j,) + _strip(col, i))],
        out_specs=pl.BlockSpec((None,) + blk, lambda j, i, cr: (j,) + _strip(col, i)))
    return pl.pallas_call(body, name=name, out_shape=jax.ShapeDtypeStruct(recv.shape, BF16), grid_spec=grid_spec,
                          compiler_params=_cp("parallel", "parallel"))(core, g, recv)


def _quad_sum(name, own, landed, chip_core, col):
    rows, cols = landed.shape[1:]
    blk, nb = _tiling(rows, cols, col)
    full = (rows, 2 * cols) if col else (2 * rows, cols)

    def body(cc_ref, own_ref, l_ref, o_ref):
        o_ref[...] = ((own_ref[...].astype(F32) + l_ref[0].astype(F32)) + l_ref[1].astype(F32)) + l_ref[2].astype(F32)

    grid_spec = pltpu.PrefetchScalarGridSpec(
        num_scalar_prefetch=1, grid=(nb,),
        in_specs=[pl.BlockSpec((None,) + blk, lambda i, cc: (cc[0],) + _strip(col, i)),
                  pl.BlockSpec((3,) + blk, lambda i, cc: (0,) + _strip(col, i))],
        out_specs=pl.BlockSpec(blk, lambda i, cc: _strip(col, cc[1] * nb + i)))
    return pl.pallas_call(body, name=name, out_shape=jax.ShapeDtypeStruct(full, F32), grid_spec=grid_spec,
                          compiler_params=_cp("arbitrary"))(chip_core, own, landed)


def _device_sum(name, gathered):
    def body(g_ref, o_ref):
        total = g_ref[0]
        for k in range(1, N_DEV):
            total = total + g_ref[k]
        o_ref[...] = total

    return pl.pallas_call(body, name=name, out_shape=jax.ShapeDtypeStruct(gathered.shape[1:], F32))(gathered)


def _adamw(name, w, g, m, v):
    rows, cols = w.shape
    col = rows % SUBLANES != 0
    blk, nb = _tiling(rows, cols, col)
    bc1 = 1.0 - ADAM_B1 ** ADAM_STEP
    bc2 = 1.0 - ADAM_B2 ** ADAM_STEP

    def body(w_ref, g_ref, m_ref, v_ref, d_ref, mo_ref, vo_ref):
        gv = g_ref[...]
        mn = ADAM_B1 * m_ref[...] + (1.0 - ADAM_B1) * gv
        vn = ADAM_B2 * v_ref[...] + (1.0 - ADAM_B2) * (gv * gv)
        mo_ref[...] = mn
        vo_ref[...] = vn
        d_ref[...] = -ADAM_LR * ((mn / bc1) / (jnp.sqrt(vn / bc2) + ADAM_EPS) + ADAM_WD * w_ref[...])

    spec = pl.BlockSpec(blk, lambda i: _strip(col, i))
    return pl.pallas_call(
        body, name=name, out_shape=[jax.ShapeDtypeStruct((rows, cols), F32)] * 3, grid=(nb,),
        in_specs=[spec] * 4, out_specs=[spec] * 3, compiler_params=_cp("parallel"),
    )(w, g, m, v)


WEIGHTS = ["w_ada", "b_ada", "ffn1_w_in", "ffn1_w_out", "ln1_g", "ln1_b", "w_mix_in", "rel_bias", "w_alpha2",
           "b_alpha", "gla_norm_g", "w_proj_a", "w_proj_b", "w_mix_out", "ln2_g", "ln2_b", "ffn2_w_in", "ffn2_w_out",
           "ln3_g", "ln3_b"]
BIG = {"ffn1_w_in": True, "ffn1_w_out": False, "w_mix_in": False, "w_proj_a": True, "w_proj_b": True,
       "w_mix_out": False, "ffn2_w_in": True, "ffn2_w_out": False}
TRANSPOSED = ("w_mix_in",)
STACKED = ("ffn1_w_in", "ffn2_w_in", "w_mix_in")
GROUP_FFN1 = ("ffn1_w_in", "ffn1_w_out")
GROUP_PROJ = ("w_proj_a", "w_proj_b", "w_mix_out")
SMALL = ["ln1_g", "ln1_b", "ln2_g", "ln2_b", "ln3_g", "ln3_b", "b_alpha", "gla_norm_g", "rel_bias", "w_alpha2"]


def _pad_rows(vec, rows=SUBLANES):
    per = -(-vec.shape[0] // (rows * LANES)) * LANES
    return jnp.pad(vec, (0, rows * per - vec.shape[0])).reshape(rows, per)


def _silu(v):
    return v * _sigmoid(v)


class _MeshPlan:
    def __init__(self, shards, chip, core):
        self.shards, self.chip = shards, chip
        self.core1 = core.astype(jnp.int32).reshape(1)
        self.chip_core = jnp.stack([chip, core]).astype(jnp.int32)
        self.partial, self.full, self.local, self.pair, self.half, self.final, self.memos = {}, {}, {}, {}, {}, {}, {}
        ici, d2d, x1, x2, x3 = self.gather_ici, self.gather_d2d, self.exchange, self.scatter, self.share
        mix_in, in1, out1, in2, out2 = ("w_mix_in",), ("ffn1_w_in",), ("ffn1_w_out",), ("ffn2_w_in",), ("ffn2_w_out",)
        self.schedule = {
            "ffn1_in_fwd": [ici(mix_in)], "ffn1_out_fwd": [d2d(mix_in), ici(out2)],
            "mix_in_a": [d2d(out2)], "mix_in_g": [ici(GROUP_PROJ)],
            "attn_fwd": [ici(in2), d2d(GROUP_PROJ)], "gla_fwd": [d2d(in2)],
            "ffn2_dw_in": [x1(out2)], "ffn2_du": [x2(out2), x1(in2)], "mix_dw_out": [x3(out2)],
            "attn_bwd": [x2(in2)], "gla_bwd": [x3(in2), x1(GROUP_PROJ)],
            "mix_du_g": [x2(GROUP_PROJ)], "mix_dw_g": [x3(GROUP_PROJ)],
            "ffn1_out_bwd": [x1(mix_in)], "ffn1_dw_in": [x2(mix_in)], "ffn1_dw_out": [x3(mix_in), x1(in1)],
            "ffn1_du": [x2(in1), x1(out1)], "rel_bias_grad": [x2(out1), x3(in1)],
        }

    def weight(self, k):
        return self.full[k]

    def grad(self, k, g):
        r, cc = self.shards[k].shape
        if k not in STACKED:
            g = g.reshape(r, N_CHIPS, cc).transpose(1, 0, 2) if BIG[k] else g.reshape(N_CHIPS, r, cc)
        self.local[k] = g

    def memo(self, key, make):
        if key not in self.memos:
            self.memos[key] = make()
        return self.memos[key]

    def host(self, name, call):
        builders = self.schedule.get(name)
        if not builders:
            return call(None)
        built = [b() for b in builders]
        main, comm = call([st for st, _ in built])
        for (_, post), res in zip(built, comm):
            post(res)
        return main

    def run(self, name, builders):
        built = [b() for b in builders]
        for (_, post), res in zip(built, _run_stages(name, [st for st, _ in built])):
            post(res)

    def set_gathered(self, names, gathered):
        for k, g in zip(names, gathered):
            _, r, cc = g.shape
            g = lax.dynamic_update_slice(g, self.shards[k][None], (self.chip, 0, 0))
            if k not in STACKED:
                g = g.transpose(1, 0, 2).reshape(r, N_CHIPS * cc) if BIG[k] else g.reshape(N_CHIPS * r, cc)
            self.full[k] = g

    @staticmethod
    def cols(names):
        return [k in TRANSPOSED for k in names]

    def gather_ici(self, names):
        def post(res):
            self.partial.update(zip(names, res))
        return lambda: (_stage_gather_ici([self.shards[k] for k in names], self.cols(names)), post)

    def gather_d2d(self, names):
        return lambda: (_stage_gather_d2d([self.partial[k] for k in names], self.cols(names)),
                        lambda res: self.set_gathered(names, res))

    def exchange(self, names):
        def post(res):
            for k, r in zip(names, res):
                self.pair[k] = _pair_sum(f"pair_sum_{k}", self.local[k], r, self.core1, k in TRANSPOSED)
        return lambda: (_stage_exchange_halves([self.local[k] for k in names], self.cols(names)), post)

    def scatter(self, names):
        def post(res):
            for k, landed in zip(names, res):
                self.half[k] = _quad_sum(f"quad_sum_{k}", self.pair[k], landed, self.chip_core, k in TRANSPOSED)
        return lambda: (_stage_scatter([self.pair[k] for k in names]), post)

    def share(self, names):
        def post(res):
            self.final.update(zip(names, res))
        return lambda: (_stage_share([self.half[k] for k in names], self.cols(names)), post)


def _step(args):
    x_pos, y_pos, c_pos = _mesh_pos()
    chip = 2 * x_pos + y_pos
    dev = 4 * x_pos + 2 * y_pos + c_pos
    take = lambda name, k: args[name][0].T if k in TRANSPOSED else args[name][0]
    w = {k: take(k, k) for k in WEIGHTS}
    mom = {k: take("m_" + k, k) for k in WEIGHTS}
    vel = {k: take("v_" + k, k) for k in WEIGHTS}
    x = args["x"][0]
    target = args["loss_target"][0]
    s, d = x.shape
    kd = d // 4
    rel_sh = w["rel_bias"].shape[1]
    wa2_sh = w["w_alpha2"].shape[1]
    ada_sh = w["w_ada"].shape[1]

    n_rel, n_wa2 = A_HEADS * rel_sh, GATE_RANK * wa2_sh
    packed = _pad_rows(jnp.concatenate([args["c"].reshape(-1), w["rel_bias"].reshape(-1), w["w_alpha2"].reshape(-1)]))
    got = _allgather_rows("gather_small_inputs", packed).reshape(N_DEV, -1)
    c_all = got[:, :d]
    per_chip = got[0::2]
    rel_bias = per_chip[:, d:d + n_rel].reshape(N_CHIPS, A_HEADS, rel_sh).transpose(1, 0, 2).reshape(A_HEADS, -1)
    w_alpha2 = per_chip[:, d + n_rel:d + n_rel + n_wa2].reshape(N_CHIPS, GATE_RANK, wa2_sh).transpose(1, 0, 2)
    w_alpha2 = w_alpha2.reshape(GATE_RANK, -1)

    b_shard = lax.dynamic_slice(w["b_ada"], (chip * ada_sh,), (ada_sh,))
    mod_shard = _mm("ada_fwd", "nn", c_all, w["w_ada"], (N_DEV, ada_sh, d), tm=N_DEV, tn=_tile(ada_sh, (512, 128)),
                    tk=d, precision=HIGHEST, a_fn=_silu, add=jnp.broadcast_to(b_shard[None], (N_DEV, ada_sh)))
    mod_all = _allgather_rows("gather_mod", mod_shard).reshape(N_DEV, N_DEV, ada_sh)[0::2]
    mod_all = mod_all.transpose(1, 0, 2).reshape(N_DEV, N_MOD * d)
    mod = lax.dynamic_index_in_dim(mod_all, dev, 0, keepdims=False).reshape(N_MOD, d)

    names = list(BIG)
    plan = _MeshPlan({k: w[k].astype(BF16) for k in names}, chip, c_pos)
    plan.set_gathered(GROUP_FFN1, _allgather_weights([plan.shards[k] for k in GROUP_FFN1]))

    small = dict(rel_bias=rel_bias, w_alpha2=w_alpha2, b_alpha=w["b_alpha"][None], gla_norm_g=w["gla_norm_g"][None])
    for k in ("ln1_g", "ln1_b", "ln2_g", "ln2_b", "ln3_g", "ln3_b"):
        small[k] = w[k][None]
    loss_local, grad_x, small_grads, dmod = _device_step(x, target, mod, small, plan)
    loss = lax.psum(loss_local, ("x", "y", "c"))
    plan.run("grad_tail_share", [plan.share(GROUP_FFN1[1:])])

    flat = jnp.concatenate([small_grads[k].reshape(-1) for k in SMALL] + [dmod.reshape(-1)])
    n_small = flat.shape[0] - N_MOD * d
    packed = _pad_rows(flat)
    all_small = _allgather_rows("gather_small_grads", packed).reshape(N_DEV, SUBLANES, -1)
    summed = _device_sum("small_grad_sum", all_small).reshape(-1)
    dmod_all = all_small.reshape(N_DEV, -1)[:, n_small:n_small + N_MOD * d]
    dmod_shard = lax.dynamic_slice(dmod_all, (0, chip * ada_sh), (N_DEV, ada_sh))
    grads = {"b_ada": summed[n_small:n_small + N_MOD * d]}
    off = 0
    for k in SMALL:
        size = small_grads[k].size
        grads[k] = summed[off:off + size].reshape(small_grads[k].shape)
        off += size
    grads["rel_bias"] = lax.dynamic_slice(grads["rel_bias"], (0, chip * rel_sh), (A_HEADS, rel_sh))
    grads["w_alpha2"] = lax.dynamic_slice(grads["w_alpha2"], (0, chip * wa2_sh), (GATE_RANK, wa2_sh))
    grads["w_ada"] = _mm("ada_bwd", "nn", jnp.pad(c_all.T, ((0, 0), (0, LANES - N_DEV))),
                         jnp.pad(dmod_shard, ((0, LANES - N_DEV), (0, 0))), (d, ada_sh, LANES), tm=_tile(d, (1024,)),
                         tn=_tile(ada_sh, (512, 128)), tk=LANES, precision=HIGHEST, a_fn=_silu)

    grads.update(plan.final)

    delta, new_m, new_v = {}, {}, {}
    for k in ["w_ada"] + names:
        delta[k], new_m[k], new_v[k] = _adamw(f"adamw_{k}", w[k], grads[k], mom[k], vel[k])
    tiny = ["b_ada"] + SMALL
    pack = lambda src: _pad_rows(jnp.concatenate([src[k].reshape(-1) for k in tiny]), rows=1).reshape(-1, LANES)
    outs = _adamw("adamw_small", pack(w), pack(grads), pack(mom), pack(vel))
    off = 0
    for k in tiny:
        size = w[k].size
        for dst, src in zip((delta, new_m, new_v), outs):
            dst[k] = src.reshape(-1)[off:off + size].reshape(w[k].shape)
        off += size

    give = lambda src: [src[k].T[None] if k in TRANSPOSED else src[k][None] for k in WEIGHTS]
    return (loss, grad_x[None], *give(grads), *give(delta), *give(new_m), *give(new_v))


def kernel(x, c, w_ada, b_ada, ffn1_w_in, ffn1_w_out, ln1_g, ln1_b, w_mix_in, rel_bias, w_alpha2, b_alpha, gla_norm_g, w_proj_a, w_proj_b, w_mix_out, ln2_g, ln2_b, ffn2_w_in, ffn2_w_out, ln3_g, ln3_b, loss_target, m_w_ada, m_b_ada, m_ffn1_w_in, m_ffn1_w_out, m_ln1_g, m_ln1_b, m_w_mix_in, m_rel_bias, m_w_alpha2, m_b_alpha, m_gla_norm_g, m_w_proj_a, m_w_proj_b, m_w_mix_out, m_ln2_g, m_ln2_b, m_ffn2_w_in, m_ffn2_w_out, m_ln3_g, m_ln3_b, v_w_ada, v_b_ada, v_ffn1_w_in, v_ffn1_w_out, v_ln1_g, v_ln1_b, v_w_mix_in, v_rel_bias, v_w_alpha2, v_b_alpha, v_gla_norm_g, v_w_proj_a, v_w_proj_b, v_w_mix_out, v_ln2_g, v_ln2_b, v_ffn2_w_in, v_ffn2_w_out, v_ln3_g, v_ln3_b):
    return _step(dict(locals()))
```

```python
import functools

import jax
import jax.numpy as jnp
from jax import lax
from jax.experimental import pallas as pl
from jax.experimental.pallas import tpu as pltpu

F32 = jnp.float32
BF16 = jnp.bfloat16
MESH = pl.DeviceIdType.MESH
HIGHEST = lax.Precision.HIGHEST

VMEM_LIMIT_BYTES = 56 * 1024 * 1024
LANES = 128
SUBLANES = 8

CHUNK = 64
A_HEADS = 16
A_HEAD_DIM = 64
A_PAST_CHUNKS = 8
A_BAND = (A_PAST_CHUNKS + 1) * CHUNK
A_PAD = A_PAST_CHUNKS * CHUNK
REL_CLIP = 256
REL_SIZE = REL_CLIP + CHUNK
B_HEADS = 4
GATE_RANK = 16
GATE_TAU = 16.0
N_MOD = 9
DEPTH = 1
ALPHA = (2.0 * DEPTH) ** 0.25
LN_EPS = 1e-5
RMS_EPS = 1e-6
ADAM_LR = 0.001
ADAM_B1 = 0.9
ADAM_B2 = 0.999
ADAM_EPS = 1e-08
ADAM_WD = 0.01
ADAM_STEP = 10
NEG_BIG = -1e30

N_CHIPS = 4
N_DEV = 8


def _cp(*sem):
    return pltpu.CompilerParams(dimension_semantics=sem, vmem_limit_bytes=VMEM_LIMIT_BYTES)


class _Stage:
    def __init__(self, arrays, out_shapes, n_sems, start, finish, aliases=None, relay=None):
        self.arrays, self.out_shapes, self.n_sems = list(arrays), list(out_shapes), n_sems
        self.start, self.finish, self.relay, self.aliases = start, finish, relay, dict(aliases or {})


def _pcall(body, stages, *, name, out_shape, in_specs, out_specs, grid=(), scratch_shapes=(), compiler_params=None):
    single = not isinstance(out_shape, (list, tuple))
    outs = [out_shape] if single else list(out_shape)
    ospecs = [out_specs] if single else list(out_specs)
    in_specs, scratch_shapes = list(in_specs), list(scratch_shapes)
    n_in, n_out, n_sc = len(in_specs), len(outs), len(scratch_shapes)
    stages = list(stages or [])
    c_in = [a for st in stages for a in st.arrays]
    c_out = [o for st in stages for o in st.out_shapes]
    aliases = {}
    io, oo = n_in, n_out
    for st in stages:
        for a, b in st.aliases.items():
            aliases[io + a] = oo + b
        io += len(st.arrays)
        oo += len(st.out_shapes)

    def wrapped(*refs):
        ins = refs[:n_in]
        cins = refs[n_in:n_in + len(c_in)]
        base = n_in + len(c_in)
        mouts = refs[base:base + n_out]
        couts = refs[base + n_out:base + n_out + len(c_out)]
        base += n_out + len(c_out)
        scr = refs[base:base + n_sc]
        sems = refs[base + n_sc:]

        def each(phase):
            i = o = 0
            for k, st in enumerate(stages):
                fn = (st.start, st.relay, st.finish)[phase]
                if fn is not None:
                    fn(cins[i:i + len(st.arrays)], couts[o:o + len(st.out_shapes)], sems[2 * k], sems[2 * k + 1])
                i += len(st.arrays)
                o += len(st.out_shapes)

        if stages and grid:
            step = functools.reduce(lambda acc, a: acc * grid[a] + pl.program_id(a), range(len(grid)), 0)
            steps = functools.reduce(lambda a, b: a * b, grid)
            pl.when(step == 0)(lambda: each(0))
            if any(st.relay for st in stages):
                pl.when(step == (2 * steps) // 3)(lambda: each(1))
            if body is not None:
                body(*ins, *mouts, *scr)
            pl.when(step == steps - 1)(lambda: each(2))
        else:
            each(0)
            each(1)
            if body is not None:
                body(*ins, *mouts, *scr)
            each(2)

    sem_shapes = []
    for st in stages:
        sem_shapes += [pltpu.SemaphoreType.DMA((st.n_sems,)), pltpu.SemaphoreType.DMA((st.n_sems,))]
    kwargs = dict(grid=grid) if grid else {}
    if compiler_params is not None:
        kwargs["compiler_params"] = compiler_params

    def run(*operands):
        res = pl.pallas_call(
            wrapped, name=name, out_shape=outs + c_out, in_specs=in_specs + [HBM_SPEC] * len(c_in),
            out_specs=ospecs + [HBM_SPEC] * len(c_out), scratch_shapes=scratch_shapes + sem_shapes,
            input_output_aliases=aliases, **kwargs)(*operands, *c_in)
        main = res[0] if single else tuple(res[:n_out])
        if not stages:
            return main
        comm, o = [], n_out
        for st in stages:
            comm.append(list(res[o:o + len(st.out_shapes)]))
            o += len(st.out_shapes)
        return main, comm

    return run


LONG_K = (2048, 1024)


def _tile(n, prefs):
    for t in prefs:
        if t <= n and n % t == 0:
            return t
    return n


_DIMS = {"nn": (((1,), (0,)), ((), ())), "nt": (((1,), (1,)), ((), ())), "tn": (((0,), (0,)), ((), ()))}


def _dot(a, b, mode="nn", precision=None):
    return lax.dot_general(a, b, _DIMS[mode], precision=precision, preferred_element_type=F32)


def _sigmoid(x):
    return 0.5 * jnp.tanh(0.5 * x) + 0.5


EPILOGUE_STRIP = 256


def _strips(n, width=EPILOGUE_STRIP):
    width = width if n % width == 0 else n
    return [slice(j, j + width) for j in range(0, n, width)]


def _mm(name, mode, a, b, mnk, *, tm, tn, tk, out_dtype=F32, precision=None, a_spec=None, b_spec=None,
        out_shape=None, o_spec=None, add=None, a_fn=None, thin=None, stages=None):
    m, n, k = mnk
    assert m % tm == 0 and n % tn == 0 and k % tk == 0, (name, mnk, tm, tn, tk)
    nk = k // tk
    if a_spec is None:
        a_spec = {"nn": pl.BlockSpec((tm, tk), lambda i, j, kk: (i, kk)),
                  "nt": pl.BlockSpec((tm, tk), lambda i, j, kk: (i, kk)),
                  "tn": pl.BlockSpec((tk, tm), lambda i, j, kk: (kk, i))}[mode]
    if b_spec is None:
        b_spec = {"nn": pl.BlockSpec((tk, tn), lambda i, j, kk: (kk, j)),
                  "nt": pl.BlockSpec((tn, tk), lambda i, j, kk: (j, kk)),
                  "tn": pl.BlockSpec((tk, tn), lambda i, j, kk: (kk, j))}[mode]
    if o_spec is None:
        o_spec = pl.BlockSpec((tm, tn), lambda i, j, kk: (i, j))
    if out_shape is None:
        out_shape = (m, n)
    has_add = add is not None
    n_in = 2 + has_add + (2 if thin else 0)

    def body(*refs):
        a_ref, b_ref = refs[0], refs[1]
        add_ref = refs[2] if has_add else None
        o_ref = refs[n_in]
        av = a_ref[...]
        if a_fn is not None:
            av = a_fn(av)
        part = _dot(av, b_ref[...], mode, precision)

        def finish(total):
            if has_add:
                total = total + add_ref[...]
            if thin:
                total = total + _dot(refs[n_in - 2][...], refs[n_in - 1][...])
            o_ref[...] = total.astype(out_dtype)

        if nk == 1:
            finish(part)
        else:
            acc_ref = refs[-1]
            kk = pl.program_id(2)

            @pl.when(kk == 0)
            def _():
                acc_ref[...] = part

            @pl.when(kk > 0)
            def _():
                acc_ref[...] += part

            @pl.when(kk == nk - 1)
            def _():
                finish(acc_ref[...])

    in_specs = [a_spec, b_spec]
    operands = [a, b]
    if has_add:
        in_specs.append(pl.BlockSpec((tm, tn), lambda i, j, kk: (i, j)))
        operands.append(add)
    if thin:
        k2 = thin[0].shape[1]
        in_specs += [pl.BlockSpec((tm, k2), lambda i, j, kk: (i, 0)), pl.BlockSpec((k2, tn), lambda i, j, kk: (0, j))]
        operands += list(thin)
    return _pcall(
        body, stages, name=name, out_shape=jax.ShapeDtypeStruct(out_shape, out_dtype), grid=(m // tm, n // tn, nk),
        in_specs=in_specs, out_specs=o_spec,
        scratch_shapes=[pltpu.VMEM((tm, tn), F32)] if nk > 1 else [],
        compiler_params=_cp("arbitrary", "arbitrary", "arbitrary") if stages else _cp("parallel", "parallel", "arbitrary"),
    )(*operands)


def _row_spec(tr, d):
    return pl.BlockSpec((tr, d), lambda i: (i, 0))


def _vec_spec(d, rows=1):
    return pl.BlockSpec((rows, d), lambda i: (0, 0))


def _modulate(name, x, sh, sc):
    s, d = x.shape
    tr = _tile(s, (512, 256))

    def body(x_ref, sh_ref, sc_ref, o_ref):
        o_ref[...] = (x_ref[...] * (1.0 + sc_ref[...]) + sh_ref[...]).astype(BF16)

    return pl.pallas_call(
        body, name=name, out_shape=jax.ShapeDtypeStruct((s, d), BF16), grid=(s // tr,),
        in_specs=[_row_spec(tr, d), _vec_spec(d), _vec_spec(d)], out_specs=_row_spec(tr, d),
        compiler_params=_cp("parallel"),
    )(x, sh, sc)


def _ln_stats(r):
    mu = jnp.mean(r, axis=-1, keepdims=True)
    xc = r - mu
    var = jnp.mean(xc * xc, axis=-1, keepdims=True)
    rstd = lax.rsqrt(var + LN_EPS)
    return xc * rstd, rstd


def _resid_ln_fwd(name, x, f, gate, ln_g, ln_b, sh_n, sc_n, coef):
    s, d = x.shape
    tr = _tile(s, (256,))

    def body(x_ref, f_ref, gate_ref, g_ref, b_ref, sh_ref, sc_ref, h_ref, u_ref):
        r = ALPHA * x_ref[...] + (coef * gate_ref[...]) * f_ref[...]
        xhat, _ = _ln_stats(r)
        h = xhat * g_ref[...] + b_ref[...]
        h_ref[...] = h
        u_ref[...] = (h * (1.0 + sc_ref[...]) + sh_ref[...]).astype(BF16)

    return pl.pallas_call(
        body, name=name, out_shape=(jax.ShapeDtypeStruct((s, d), F32), jax.ShapeDtypeStruct((s, d), BF16)),
        grid=(s // tr,), in_specs=[_row_spec(tr, d), _row_spec(tr, d)] + [_vec_spec(d)] * 5,
        out_specs=(_row_spec(tr, d), _row_spec(tr, d)), compiler_params=_cp("parallel"),
    )(x, f, gate, ln_g, ln_b, sh_n, sc_n)


ROW_DSC, ROW_DSH, ROW_DLN_G, ROW_DLN_B, ROW_DGATE, ROW_LOSS = 0, 1, 2, 3, 4, 5


def _ln_bwd_core(dy, xhat, rstd, ln_g):
    dxhat = dy * ln_g
    m1 = jnp.mean(dxhat, axis=-1, keepdims=True)
    m2 = jnp.mean(dxhat * xhat, axis=-1, keepdims=True)
    return rstd * (dxhat - m1 - xhat * m2)


def _colsum(v):
    return jnp.sum(v, axis=0, keepdims=True)


def _final_ln_loss_bwd(name, x, f, target, gate, ln_g, ln_b, coef):
    s, d = x.shape
    tr = _tile(s, (256,))
    inv_d = 1.0 / d

    def body(x_ref, f_ref, t_ref, gate_ref, g_ref, b_ref, dr_ref, df_ref, acc_ref):
        @pl.when(pl.program_id(0) == 0)
        def _():
            acc_ref[...] = jnp.zeros_like(acc_ref)

        fv = f_ref[...]
        r = ALPHA * x_ref[...] + (coef * gate_ref[...]) * fv
        xhat, rstd = _ln_stats(r)
        h = xhat * g_ref[...] + b_ref[...]
        err = h - t_ref[...]
        dy = err * inv_d
        dr = _ln_bwd_core(dy, xhat, rstd, g_ref[...])
        dr_ref[...] = dr
        df_ref[...] = ((coef * gate_ref[...]) * dr).astype(BF16)
        acc_ref[ROW_DLN_G:ROW_DLN_G + 1, :] += _colsum(dy * xhat)
        acc_ref[ROW_DLN_B:ROW_DLN_B + 1, :] += _colsum(dy)
        acc_ref[ROW_DGATE:ROW_DGATE + 1, :] += _colsum((coef * dr) * fv)
        acc_ref[ROW_LOSS:ROW_LOSS + 1, :] += _colsum(err * err) * (0.5 * inv_d)

    return pl.pallas_call(
        body, name=name,
        out_shape=(jax.ShapeDtypeStruct((s, d), F32), jax.ShapeDtypeStruct((s, d), BF16),
                   jax.ShapeDtypeStruct((SUBLANES, d), F32)),
        grid=(s // tr,), in_specs=[_row_spec(tr, d)] * 3 + [_vec_spec(d)] * 3,
        out_specs=(_row_spec(tr, d), _row_spec(tr, d), _vec_spec(d, SUBLANES)),
        compiler_params=_cp("arbitrary"),
    )(x, f, target, gate, ln_g, ln_b)


def _resid_ln_bwd(name, du_n, dr_n, x, f, sc_n, gate, ln_g, ln_b, coef):
    s, d = x.shape
    tr = _tile(s, (256,))

    def body(du_ref, drn_ref, x_ref, f_ref, sc_ref, gate_ref, g_ref, b_ref, dr_ref, df_ref, acc_ref):
        @pl.when(pl.program_id(0) == 0)
        def _():
            acc_ref[...] = jnp.zeros_like(acc_ref)

        fv = f_ref[...]
        du = du_ref[...]
        r = ALPHA * x_ref[...] + (coef * gate_ref[...]) * fv
        xhat, rstd = _ln_stats(r)
        h = xhat * g_ref[...] + b_ref[...]
        dy = du * (1.0 + sc_ref[...]) + ALPHA * drn_ref[...]
        dr = _ln_bwd_core(dy, xhat, rstd, g_ref[...])
        dr_ref[...] = dr
        df_ref[...] = ((coef * gate_ref[...]) * dr).astype(BF16)
        acc_ref[ROW_DSC:ROW_DSC + 1, :] += _colsum(du * h)
        acc_ref[ROW_DSH:ROW_DSH + 1, :] += _colsum(du)
        acc_ref[ROW_DLN_G:ROW_DLN_G + 1, :] += _colsum(dy * xhat)
        acc_ref[ROW_DLN_B:ROW_DLN_B + 1, :] += _colsum(dy)
        acc_ref[ROW_DGATE:ROW_DGATE + 1, :] += _colsum((coef * dr) * fv)

    return pl.pallas_call(
        body, name=name,
        out_shape=(jax.ShapeDtypeStruct((s, d), F32), jax.ShapeDtypeStruct((s, d), BF16),
                   jax.ShapeDtypeStruct((SUBLANES, d), F32)),
        grid=(s // tr,), in_specs=[_row_spec(tr, d)] * 4 + [_vec_spec(d)] * 4,
        out_specs=(_row_spec(tr, d), _row_spec(tr, d), _vec_spec(d, SUBLANES)),
        compiler_params=_cp("arbitrary"),
    )(du_n, dr_n, x, f, sc_n, gate, ln_g, ln_b)


def _input_grad(name, du, dr, x, sc):
    s, d = x.shape
    tr = _tile(s, (256,))

    def body(du_ref, dr_ref, x_ref, sc_ref, gx_ref, acc_ref):
        @pl.when(pl.program_id(0) == 0)
        def _():
            acc_ref[...] = jnp.zeros_like(acc_ref)

        du = du_ref[...]
        gx_ref[...] = du * (1.0 + sc_ref[...]) + ALPHA * dr_ref[...]
        acc_ref[ROW_DSC:ROW_DSC + 1, :] += _colsum(du * x_ref[...])
        acc_ref[ROW_DSH:ROW_DSH + 1, :] += _colsum(du)

    return pl.pallas_call(
        body, name=name,
        out_shape=(jax.ShapeDtypeStruct((s, d), F32), jax.ShapeDtypeStruct((SUBLANES, d), F32)),
        grid=(s // tr,), in_specs=[_row_spec(tr, d)] * 3 + [_vec_spec(d)],
        out_specs=(_row_spec(tr, d), _vec_spec(d, SUBLANES)), compiler_params=_cp("arbitrary"),
    )(du, dr, x, sc)


def _ffn_in_fwd(name, u, w_in, stages=None):
    s, d = u.shape
    cs = w_in.shape[2]
    f = 2 * cs
    tm, tn = _tile(s, (2048, 1024, 512)), _tile(cs, (256, 128))
    nb = f // tn
    nbs = cs // tn

    def body(u_ref, wa_ref, wb_ref, ab_ref, act_ref):
        for rows in _strips(tm, 512):
            uv = u_ref[rows, :]
            a = _dot(uv, wa_ref[...])
            b = _dot(uv, wb_ref[...])
            sg = _sigmoid(a)
            silu = a * sg
            ab_ref[0, rows, :] = (b * (sg + silu * (1.0 - sg))).astype(BF16)
            ab_ref[1, rows, :] = silu.astype(BF16)
            act_ref[rows, :] = (silu * b).astype(BF16)

    return _pcall(
        body, stages, name=name,
        out_shape=(jax.ShapeDtypeStruct((2, s, f), BF16), jax.ShapeDtypeStruct((s, f), BF16)),
        grid=(s // tm, nb),
        in_specs=[pl.BlockSpec((tm, d), lambda i, j: (i, 0)),
                  pl.BlockSpec((None, d, tn), lambda i, j: (j // nbs, 0, j % nbs)),
                  pl.BlockSpec((None, d, tn), lambda i, j: (2 + j // nbs, 0, j % nbs))],
        out_specs=(pl.BlockSpec((2, tm, tn), lambda i, j: (0, i, j)), pl.BlockSpec((tm, tn), lambda i, j: (i, j))),
        compiler_params=_cp("arbitrary", "arbitrary"),
    )(u, w_in, w_in)


def _ffn_out_bwd(name, df, w_out, ab, stages=None):
    s, d = df.shape
    f = w_out.shape[0]
    tm, tn = _tile(s, (1024, 512)), _tile(f, (512, 256, 128))

    def body(df_ref, w_ref, ab_ref, dab_ref):
        dfv = df_ref[...]
        for cols in _strips(tn):
            dact = _dot(dfv, w_ref[cols, :], "nt")
            dab_ref[0, :, cols] = (dact * ab_ref[0, :, cols].astype(F32)).astype(BF16)
            dab_ref[1, :, cols] = (dact * ab_ref[1, :, cols].astype(F32)).astype(BF16)

    return _pcall(
        body, stages, name=name, out_shape=jax.ShapeDtypeStruct((2, s, f), BF16), grid=(s // tm, f // tn),
        in_specs=[pl.BlockSpec((tm, d), lambda i, j: (i, 0)), pl.BlockSpec((tn, d), lambda i, j: (j, 0)),
                  pl.BlockSpec((2, tm, tn), lambda i, j: (0, i, j))],
        out_specs=pl.BlockSpec((2, tm, tn), lambda i, j: (0, i, j)),
        compiler_params=_cp("arbitrary", "arbitrary"),
    )(df, w_out, ab)


def _ffn_forward(tag, u, plan):
    w_in, w_out = plan.weight(f"{tag}_w_in"), plan.weight(f"{tag}_w_out")
    s, d = u.shape
    f = w_out.shape[0]
    ab, act = plan.host(f"{tag}_in_fwd", lambda st: _ffn_in_fwd(f"{tag}_in_fwd", u, w_in, st))
    out = plan.host(f"{tag}_out_fwd", lambda st: _mm(
        f"{tag}_out_fwd", "nn", act, w_out, (s, d, f), tm=_tile(s, (1024,)), tn=_tile(d, (1024,)),
        tk=_tile(f, (2816, 1408, 512, 128)), stages=st))
    return out, (ab, act)


def _ffn_backward(tag, df, u, saved, plan, in_first):
    w_in, w_out = plan.weight(f"{tag}_w_in"), plan.weight(f"{tag}_w_out")
    ab, act = saved
    s, d = u.shape
    f = w_out.shape[0]
    dab = plan.host(f"{tag}_out_bwd", lambda st: _ffn_out_bwd(f"{tag}_out_bwd", df, w_out, ab, st))
    cs = w_in.shape[2]
    tk = _tile(cs, (2816, 1408, 256, 128))
    nkh, nks = f // tk, cs // tk
    tmd = _tile(d, (1024,))
    tn = _tile(cs, (1408, 256, 128))
    nbh, nbs = f // tn, cs // tn
    tks = _tile(s, LONG_K)

    def dw_in():
        plan.grad(f"{tag}_w_in", plan.host(f"{tag}_dw_in", lambda st: _mm(
            f"{tag}_dw_in", "tn", u, dab, (d, 2 * f, s), tm=tmd, tn=tn, tk=tks,
            b_spec=pl.BlockSpec((None, tks, tn), lambda i, j, kk: (j // nbh, kk, j % nbh)), out_shape=(N_CHIPS, d, cs),
            o_spec=pl.BlockSpec((None, tmd, tn), lambda i, j, kk: (j // nbs, i, j % nbs)), stages=st)))

    def dw_out():
        plan.grad(f"{tag}_w_out", plan.host(f"{tag}_dw_out", lambda st: _mm(
            f"{tag}_dw_out", "tn", act, df, (f, d, s), tm=_tile(f, (1408, 512, 128)), tn=tmd, tk=tks, stages=st)))

    for step in ((dw_in, dw_out) if in_first else (dw_out, dw_in)):
        step()
    return plan.host(f"{tag}_du", lambda st: _mm(
        f"{tag}_du", "nt", dab, w_in, (s, d, 2 * f), tm=_tile(s, (1024,)), tn=tmd, tk=tk,
        a_spec=pl.BlockSpec((None, _tile(s, (1024,)), tk), lambda i, j, kk: (kk // nkh, i, kk % nkh)),
        b_spec=pl.BlockSpec((None, tmd, tk), lambda i, j, kk: (kk // nks, j, kk % nks)), stages=st))


ATTN_Q = 4 * CHUNK
ATTN_W = ATTN_Q + A_PAD


def _band_bias(bias):
    n = ATTN_Q // CHUNK
    rows = [jnp.pad(bias, ((0, 0), (0, 0), (i * CHUNK, (n - 1 - i) * CHUNK)), constant_values=NEG_BIG)
            for i in range(n)]
    return jnp.concatenate(rows, axis=1)


def _band_bias_grad(dband):
    n = ATTN_Q // CHUNK
    parts = [dband[:, i * CHUNK:(i + 1) * CHUNK, i * CHUNK:i * CHUNK + A_BAND] for i in range(n)]
    return functools.reduce(jnp.add, parts)


def _attn_probs(q, kw, bias, key0):
    sc = _dot(q, kw, "nt") * (A_HEAD_DIM ** -0.5) + bias
    ks = lax.broadcasted_iota(jnp.int32, sc.shape, 1)
    sc = jnp.where(key0 + ks >= 0, sc, NEG_BIG)
    p = jnp.exp(sc - jnp.max(sc, axis=-1, keepdims=True))
    return p / jnp.sum(p, axis=-1, keepdims=True)


def _head_masks():
    lane = lax.broadcasted_iota(jnp.int32, (1, LANES), 1)
    return [lane // A_HEAD_DIM == h for h in range(LANES // A_HEAD_DIM)]


def _attn_fwd(p1, kvp, band, stages=None):
    s = p1.shape[0]
    aw = A_HEADS * A_HEAD_DIM
    nblk = aw // LANES
    hpb = LANES // A_HEAD_DIM
    assert s % ATTN_Q == 0

    def body(q_ref, k_ref, v_ref, b_ref, o_ref):
        base = pl.multiple_of(pl.program_id(1) * ATTN_Q, ATTN_Q)
        qv = q_ref[...]
        kw = k_ref[pl.ds(base, ATTN_W), :]
        vw = v_ref[pl.ds(base, ATTN_W), :]
        out = jnp.zeros((ATTN_Q, LANES), F32)
        for h, mask in enumerate(_head_masks()):
            p = _attn_probs(jnp.where(mask, qv, jnp.zeros_like(qv)), kw, b_ref[h], base - A_PAD)
            out = jnp.where(mask, _dot(p.astype(BF16), vw), out)
        o_ref[...] = out.astype(BF16)

    kv_rows = s + A_PAD
    return _pcall(
        body, stages, name="attn_fwd", out_shape=jax.ShapeDtypeStruct((s, aw), BF16), grid=(nblk, s // ATTN_Q),
        in_specs=[pl.BlockSpec((ATTN_Q, LANES), lambda b, i: (i, b)),
                  pl.BlockSpec((kv_rows, LANES), lambda b, i: (0, b)),
                  pl.BlockSpec((kv_rows, LANES), lambda b, i: (0, nblk + b)),
                  pl.BlockSpec((hpb, ATTN_Q, ATTN_W), lambda b, i: (b, 0, 0))],
        out_specs=pl.BlockSpec((ATTN_Q, LANES), lambda b, i: (i, b)),
        compiler_params=_cp("arbitrary", "arbitrary"),
    )(p1, kvp, kvp, band)


def _attn_bwd(p1, kvp, band, dya, stages=None):
    s = p1.shape[0]
    aw = A_HEADS * A_HEAD_DIM
    nblk = aw // LANES
    hpb = LANES // A_HEAD_DIM
    scale = A_HEAD_DIM ** -0.5

    def body(q_ref, k_ref, v_ref, b_ref, do_ref, dq_ref, dk_ref, dv_ref, db_ref):
        @pl.when(pl.program_id(1) == 0)
        def _():
            dk_ref[...] = jnp.zeros_like(dk_ref)
            dv_ref[...] = jnp.zeros_like(dv_ref)
            db_ref[...] = jnp.zeros_like(db_ref)

        base = pl.multiple_of(pl.program_id(1) * ATTN_Q, ATTN_Q)
        window = pl.ds(base, ATTN_W)
        kw = k_ref[window, :]
        vw = v_ref[window, :]
        qv = q_ref[...]
        dov = do_ref[...]
        dq = jnp.zeros((ATTN_Q, LANES), F32)
        dk = jnp.zeros((ATTN_W, LANES), F32)
        dv = jnp.zeros((ATTN_W, LANES), F32)
        for h, mask in enumerate(_head_masks()):
            qh = jnp.where(mask, qv, jnp.zeros_like(qv))
            doh = jnp.where(mask, dov, jnp.zeros_like(dov))
            p = _attn_probs(qh, kw, b_ref[h], base - A_PAD)
            dp = _dot(doh, vw, "nt")
            ds = p * (dp - jnp.sum(p * dp, axis=-1, keepdims=True))
            db_ref[h] += ds
            dsb = (ds * scale).astype(BF16)
            dq = jnp.where(mask, _dot(dsb, kw), dq)
            dk = dk + _dot(dsb, qh, "tn")
            dv = dv + _dot(p.astype(BF16), doh, "tn")
        dq_ref[...] = dq.astype(BF16)
        dk_ref[window, :] += dk
        dv_ref[window, :] += dv

    kv_rows = s + A_PAD
    q_spec = pl.BlockSpec((ATTN_Q, LANES), lambda b, i: (i, b))
    acc_spec = pl.BlockSpec((kv_rows, LANES), lambda b, i: (0, b))
    b_spec = pl.BlockSpec((hpb, ATTN_Q, ATTN_W), lambda b, i: (b, 0, 0))
    return _pcall(
        body, stages, name="attn_bwd",
        out_shape=(jax.ShapeDtypeStruct((s, aw), BF16), jax.ShapeDtypeStruct((kv_rows, aw), F32),
                   jax.ShapeDtypeStruct((kv_rows, aw), F32), jax.ShapeDtypeStruct((A_HEADS, ATTN_Q, ATTN_W), F32)),
        grid=(nblk, s // ATTN_Q),
        in_specs=[q_spec, acc_spec, pl.BlockSpec((kv_rows, LANES), lambda b, i: (0, nblk + b)), b_spec, q_spec],
        out_specs=(q_spec, acc_spec, acc_spec, b_spec), compiler_params=_cp("arbitrary", "arbitrary"),
    )(p1, kvp, kvp, band, dya)


def _rel_onehot():
    qi = jnp.arange(CHUNK)[:, None]
    ks = jnp.arange(A_BAND)[None, :]
    idx = (jnp.clip(ks - A_PAD - qi, -REL_CLIP, CHUNK - 1) + REL_CLIP).reshape(1, CHUNK * A_BAND)
    return (jnp.arange(REL_SIZE)[:, None] == idx).astype(F32)


def _gla_gate(lr, wa2, balpha):
    z = _dot(lr, wa2) + balpha
    la = (jnp.minimum(z, 0.0) - jnp.log(1.0 + jnp.exp(-jnp.abs(z)))) * (1.0 / GATE_TAU)
    row = lax.broadcasted_iota(jnp.int32, (CHUNK, CHUNK), 0)
    col = lax.broadcasted_iota(jnp.int32, (CHUNK, CHUNK), 1)
    cum = _dot((row >= col).astype(F32), la, precision=HIGHEST)
    return z, la, cum


def _gla_dims(p2):
    kd = p2.shape[1] // 6
    hk = kd // B_HEADS
    hv = 2 * hk
    return kd, hk, hv


def _gla_fwd(p2, lrp, wa2p, balpha, gnorm, stages=None):
    s = p2.shape[0]
    kd, hk, hv = _gla_dims(p2)
    nc = s // CHUNK
    qscale = hk ** -0.5

    def body(p_ref, lr_ref, wa_ref, ba_ref, gn_ref, yb_ref, st_ref, state):
        @pl.when(pl.program_id(0) == 0)
        def _():
            state[...] = jnp.zeros_like(state)

        _, _, cum = _gla_gate(lr_ref[...], wa_ref[...], ba_ref[...])
        last = cum[CHUNK - 1:CHUNK, :]
        e = jnp.exp(last - cum)
        dch = jnp.exp(last)
        gn = gn_ref[...]
        for hh in range(B_HEADS):
            ks = slice(hh * hk, (hh + 1) * hk)
            q = p_ref[:, hh * hk:(hh + 1) * hk].astype(F32)
            k = p_ref[:, kd + hh * hk:kd + (hh + 1) * hk].astype(F32)
            v = p_ref[:, 2 * kd + hh * hv:2 * kd + (hh + 1) * hv]
            rg = p_ref[:, 4 * kd + hh * hv:4 * kd + (hh + 1) * hv].astype(F32)
            kdec = (k * e[:, ks]).astype(BF16)
            st = state[hh] * dch[:, ks] + _dot(v, kdec, "tn")
            state[hh] = st
            st_ref[hh] = st
            o = _dot((q * qscale).astype(BF16), st.astype(BF16), "nt")
            rinv = lax.rsqrt(jnp.mean(o * o, axis=-1, keepdims=True) + RMS_EPS)
            yb_ref[:, hh * hv:(hh + 1) * hv] = ((o * rinv * gn) * (rg * _sigmoid(rg))).astype(BF16)

    return _pcall(
        body, stages, name="gla_fwd",
        out_shape=(jax.ShapeDtypeStruct((s, 2 * kd), BF16), jax.ShapeDtypeStruct((nc, B_HEADS, hv, hk), F32)),
        grid=(nc,),
        in_specs=[pl.BlockSpec((CHUNK, 6 * kd), lambda i: (i, 0)), pl.BlockSpec((CHUNK, LANES), lambda i: (i, 0)),
                  pl.BlockSpec((LANES, kd), lambda i: (0, 0)), pl.BlockSpec((1, kd), lambda i: (0, 0)),
                  pl.BlockSpec((1, hv), lambda i: (0, 0))],
        out_specs=(pl.BlockSpec((CHUNK, 2 * kd), lambda i: (i, 0)),
                   pl.BlockSpec((None, B_HEADS, hv, hk), lambda i: (i, 0, 0, 0))),
        scratch_shapes=[pltpu.VMEM((B_HEADS, hv, hk), F32)], compiler_params=_cp("arbitrary"),
    )(p2, lrp, wa2p, balpha, gnorm)


GLA_ROW_DBALPHA, GLA_ROW_DGNORM = 0, 1


def _gla_bwd(p2, lrp, wa2p, balpha, gnorm, states, dyb, stages=None):
    s = p2.shape[0]
    kd, hk, hv = _gla_dims(p2)
    nc = s // CHUNK
    qscale = hk ** -0.5

    def body(p_ref, lr_ref, wa_ref, ba_ref, gn_ref, st_ref, sp_ref, dy_ref, dp_ref, dz_ref, sm_ref, gcar):
        i = pl.program_id(0)

        @pl.when(i == 0)
        def _():
            gcar[...] = jnp.zeros_like(gcar)
            sm_ref[...] = jnp.zeros_like(sm_ref)

        has_prev = (i < nc - 1).astype(F32)
        z, _, cum = _gla_gate(lr_ref[...], wa_ref[...], ba_ref[...])
        last = cum[CHUNK - 1:CHUNK, :]
        e = jnp.exp(last - cum)
        dch = jnp.exp(last)
        sgn = _sigmoid(-z) * (1.0 / GATE_TAU)
        gn = gn_ref[...]
        row = lax.broadcasted_iota(jnp.int32, (CHUNK, CHUNK), 0)
        col = lax.broadcasted_iota(jnp.int32, (CHUNK, CHUNK), 1)
        tri_strict = (row > col).astype(F32)
        for hh in range(B_HEADS):
            ks = slice(hh * hk, (hh + 1) * hk)
            q = p_ref[:, hh * hk:(hh + 1) * hk].astype(F32)
            k = p_ref[:, kd + hh * hk:kd + (hh + 1) * hk].astype(F32)
            v = p_ref[:, 2 * kd + hh * hv:2 * kd + (hh + 1) * hv]
            rg = p_ref[:, 4 * kd + hh * hv:4 * kd + (hh + 1) * hv].astype(F32)
            kdecf = k * e[:, ks]
            kdec = kdecf.astype(BF16)
            st16 = st_ref[hh].astype(BF16)
            qs = (q * qscale).astype(BF16)
            o = _dot(qs, st16, "nt")
            rinv = lax.rsqrt(jnp.mean(o * o, axis=-1, keepdims=True) + RMS_EPS)
            dy = dy_ref[:, hh * hv:(hh + 1) * hv].astype(F32)
            sg = _sigmoid(rg)
            onorm = o * rinv
            drg = dy * (onorm * gn) * (sg * (1.0 + rg * (1.0 - sg)))
            dob = dy * (rg * sg)
            sm_ref[GLA_ROW_DGNORM:GLA_ROW_DGNORM + 1, 0:hv] += _colsum(dob * onorm)
            t = dob * gn
            do = rinv * (t - onorm * jnp.mean(t * onorm, axis=-1, keepdims=True))
            do16 = do.astype(BF16)
            dq = _dot(do16, st16) * qscale
            gt = _dot(do16, qs, "tn") + gcar[hh]
            gcar[hh] = gt * dch[:, ks]
            dd = _colsum(gt * sp_ref[hh]) * has_prev
            gt16 = gt.astype(BF16)
            dkdec = _dot(v, gt16)
            dv = _dot(kdec, gt16, "nt")
            dla = dd * dch[:, ks] + _dot(tri_strict, dkdec * kdecf, precision=HIGHEST)
            dzh = dla * sgn[:, ks]
            sm_ref[GLA_ROW_DBALPHA:GLA_ROW_DBALPHA + 1, hh * hk:(hh + 1) * hk] += _colsum(dzh)
            dz_ref[:, hh * hk:(hh + 1) * hk] = dzh.astype(BF16)
            dp_ref[:, hh * hk:(hh + 1) * hk] = dq.astype(BF16)
            dp_ref[:, kd + hh * hk:kd + (hh + 1) * hk] = (dkdec * e[:, ks]).astype(BF16)
            dp_ref[:, 2 * kd + hh * hv:2 * kd + (hh + 1) * hv] = dv.astype(BF16)
            dp_ref[:, 4 * kd + hh * hv:4 * kd + (hh + 1) * hv] = drg.astype(BF16)

    rev = lambda i: (nc - 1 - i, 0)
    return _pcall(
        body, stages, name="gla_bwd",
        out_shape=(jax.ShapeDtypeStruct((s, 6 * kd), BF16), jax.ShapeDtypeStruct((s, kd), BF16),
                   jax.ShapeDtypeStruct((SUBLANES, kd), F32)),
        grid=(nc,),
        in_specs=[pl.BlockSpec((CHUNK, 6 * kd), rev), pl.BlockSpec((CHUNK, LANES), rev),
                  pl.BlockSpec((LANES, kd), lambda i: (0, 0)), pl.BlockSpec((1, kd), lambda i: (0, 0)),
                  pl.BlockSpec((1, hv), lambda i: (0, 0)),
                  pl.BlockSpec((None, B_HEADS, hv, hk), lambda i: (nc - 1 - i, 0, 0, 0)),
                  pl.BlockSpec((None, B_HEADS, hv, hk), lambda i: (jnp.maximum(nc - 2 - i, 0), 0, 0, 0)),
                  pl.BlockSpec((CHUNK, 2 * kd), rev)],
        out_specs=(pl.BlockSpec((CHUNK, 6 * kd), rev), pl.BlockSpec((CHUNK, kd), rev),
                   pl.BlockSpec((SUBLANES, kd), lambda i: (0, 0))),
        scratch_shapes=[pltpu.VMEM((B_HEADS, hv, hk), F32)], compiler_params=_cp("arbitrary"),
    )(p2, lrp, wa2p, balpha, gnorm, states, states, dyb)


def _merge_fwd(ya, yb, wpa, wpb, g):
    s, ka = ya.shape
    kb = yb.shape[1]
    d = wpa.shape[1]
    tm, tn = _tile(s, (1024, 512)), _tile(d, (512,))

    def body(ya_ref, yb_ref, wa_ref, wb_ref, g_ref, m_ref, pab_ref):
        yav, ybv = ya_ref[...], yb_ref[...]
        for cols in _strips(tn):
            pa = _dot(yav, wa_ref[:, cols])
            pb = _dot(ybv, wb_ref[:, cols])
            m_ref[:, cols] = (_sigmoid(g_ref[0, :, cols].astype(F32)) * pa
                              + _sigmoid(g_ref[1, :, cols].astype(F32)) * pb).astype(BF16)
            pab_ref[0, :, cols] = pa.astype(BF16)
            pab_ref[1, :, cols] = pb.astype(BF16)

    st = pl.BlockSpec((2, tm, tn), lambda i, j: (0, i, j))
    return pl.pallas_call(
        body, name="merge_fwd",
        out_shape=(jax.ShapeDtypeStruct((s, d), BF16), jax.ShapeDtypeStruct((2, s, d), BF16)),
        grid=(s // tm, d // tn),
        in_specs=[pl.BlockSpec((tm, ka), lambda i, j: (i, 0)), pl.BlockSpec((tm, kb), lambda i, j: (i, 0)),
                  pl.BlockSpec((ka, tn), lambda i, j: (0, j)), pl.BlockSpec((kb, tn), lambda i, j: (0, j)), st],
        out_specs=(pl.BlockSpec((tm, tn), lambda i, j: (i, j)), st),
        compiler_params=_cp("parallel", "parallel"),
    )(ya, yb, wpa, wpb, g)


def _merge_bwd(dm, wmo, g, pab, stages=None):
    s, d = dm.shape
    tm, tn = _tile(s, (1024, 512)), _tile(d, (512,))

    def body(dm_ref, w_ref, g_ref, pab_ref, dpab_ref, dg_ref):
        dmv = dm_ref[...]
        for cols in _strips(tn):
            dmg = _dot(dmv, w_ref[cols, :], "nt")
            for j in range(2):
                sg = _sigmoid(g_ref[j, :, cols].astype(F32))
                dpab_ref[j, :, cols] = (dmg * sg).astype(BF16)
                dg_ref[j, :, cols] = (dmg * pab_ref[j, :, cols].astype(F32) * (sg * (1.0 - sg))).astype(BF16)

    st = pl.BlockSpec((2, tm, tn), lambda i, j: (0, i, j))
    return _pcall(
        body, stages, name="merge_bwd",
        out_shape=(jax.ShapeDtypeStruct((2, s, d), BF16), jax.ShapeDtypeStruct((2, s, d), BF16)),
        grid=(s // tm, d // tn),
        in_specs=[pl.BlockSpec((tm, d), lambda i, j: (i, 0)), pl.BlockSpec((tn, d), lambda i, j: (j, 0)), st, st],
        out_specs=(st, st), compiler_params=_cp("arbitrary", "arbitrary"),
    )(dm, wmo, g, pab)


def _virtual_rows(parts, lo, hi):
    out, off = [], 0
    for p in parts:
        a, b = max(lo, off), min(hi, off + p.shape[0])
        if a < b:
            out.append(p[a - off:b - off])
        off += p.shape[0]
    return out[0] if len(out) == 1 else jnp.concatenate(out, axis=0)


def _mix_in_row_groups(d):
    o1 = 3 * A_HEADS * A_HEAD_DIM
    o2 = o1 + 6 * (d // 4)
    o3 = o2 + GATE_RANK
    return (0, o1), (o1, o2), (o2, o3), (o3, o3 + 2 * d)


def _split_mix_in(stacked):
    d = stacked.shape[2]
    flat = stacked.reshape(-1, d)
    _, _, (lo, hi), (glo, ghi) = _mix_in_row_groups(d)
    return flat, jnp.pad(flat[lo:hi], ((0, LANES - GATE_RANK), (0, 0))), flat[glo:ghi]


MIX_TILE = 1024


def _mix_in_weights(plan):
    return plan.memo("mix_in_weights", lambda: _split_mix_in(plan.weight("w_mix_in")))


def _hosted_mm(plan):
    return lambda name, *a, **k: plan.host(name, lambda st: _mm(name, *a, stages=st, **k))


def _mix_forward(u2, plan, small):
    s, d = u2.shape
    wt, wt_lr, wt_g = _mix_in_weights(plan)
    bias, wa2p, balpha, gnorm = small
    mm = _hosted_mm(plan)
    aw = A_HEADS * A_HEAD_DIM
    tm, tn = _tile(s, (1024,)), MIX_TILE
    (_, na), (_, nab) = _mix_in_row_groups(d)[:2]
    assert na % tn == 0 and nab % tn == 0
    p1 = mm("mix_in_a", "nt", u2, wt, (s, na, d), tm=tm, tn=tn, tk=d, out_dtype=BF16)
    p2 = mm("mix_in_b", "nt", u2, wt, (s, nab - na, d), tm=tm, tn=tn, tk=d, out_dtype=BF16,
            b_spec=pl.BlockSpec((tn, d), lambda i, j, kk: (na // tn + j, 0)))
    lrp = mm("mix_in_lr", "nt", u2, wt_lr, (s, LANES, d), tm=tm, tn=LANES, tk=d, out_dtype=BF16)
    nbg = d // tn
    g = mm("mix_in_g", "nt", u2, wt_g, (s, 2 * d, d), tm=tm, tn=tn, tk=d, out_dtype=BF16, out_shape=(2, s, d),
           o_spec=pl.BlockSpec((None, tm, tn), lambda i, j, kk: (j // nbg, i, j % nbg)))
    kvp = jnp.pad(p1[:, aw:], ((A_PAD, 0), (0, 0)))
    ya = plan.host("attn_fwd", lambda st: _attn_fwd(p1, kvp, bias, st))
    yb, states = plan.host("gla_fwd", lambda st: _gla_fwd(p2, lrp, wa2p, balpha, gnorm, st))
    merged, pab = _merge_fwd(ya, yb, plan.weight("w_proj_a"), plan.weight("w_proj_b"), g)
    m = mm("mix_out", "nn", merged, plan.weight("w_mix_out"), (s, d, d), tm=tm, tn=tn, tk=d)
    return m, (p1, kvp, p2, lrp, states, ya, yb, g, pab, merged)


def _mix_backward(dm, u2, saved, plan, small):
    s, d = u2.shape
    wt, wt_lr, wt_g = _mix_in_weights(plan)
    wpa, wpb, wmo = plan.weight("w_proj_a"), plan.weight("w_proj_b"), plan.weight("w_mix_out")
    bias, wa2p, balpha, gnorm = small
    p1, kvp, p2, lrp, states, ya, yb, g, pab, merged = saved
    mm = _hosted_mm(plan)
    aw = A_HEADS * A_HEAD_DIM
    kd = d // 4
    t = MIX_TILE
    tm = _tile(s, (1024,))
    tks = _tile(s, LONG_K)

    plan.grad("w_mix_out", mm("mix_dw_out", "tn", merged, dm, (d, d, s), tm=t, tn=t, tk=tks))
    dpab, dg = plan.host("merge_bwd", lambda st: _merge_bwd(dm, wmo, g, pab, st))
    sel = lambda j: pl.BlockSpec((None, tm, d), lambda i, jj, kk: (j, i, 0))
    dya = mm("mix_dya", "nt", dpab, wpa, (s, aw, d), tm=tm, tn=t, tk=d, out_dtype=BF16, a_spec=sel(0))
    dyb = mm("mix_dyb", "nt", dpab, wpb, (s, 2 * kd, d), tm=tm, tn=t, tk=d, out_dtype=BF16, a_spec=sel(1))
    selk = lambda j: pl.BlockSpec((None, tks, t), lambda i, jj, kk: (j, kk, jj))
    plan.grad("w_proj_a", mm("mix_dwpa", "tn", ya, dpab, (aw, d, s), tm=t, tn=t, tk=tks, b_spec=selk(0)))
    plan.grad("w_proj_b", mm("mix_dwpb", "tn", yb, dpab, (2 * kd, d, s), tm=t, tn=t, tk=tks, b_spec=selk(1)))

    dq, dkp, dvp, dbias = plan.host("attn_bwd", lambda st: _attn_bwd(p1, kvp, bias, dya, st))
    dp1 = jnp.concatenate([dq, dkp[A_PAD:].astype(BF16), dvp[A_PAD:].astype(BF16)], axis=1)
    dp2, dz, gsm = plan.host("gla_bwd", lambda st: _gla_bwd(p2, lrp, wa2p, balpha, gnorm, states, dyb, st))
    dlrp = mm("gla_dlr", "nt", dz, wa2p, (s, LANES, kd), tm=tm, tn=LANES, tk=kd, out_dtype=BF16)
    dwa2p = mm("gla_dwa2", "tn", lrp, dz, (LANES, kd, s), tm=LANES, tn=kd, tk=tks)

    tka = 3 * aw
    assert 6 * kd == tka
    du = mm("mix_du_a", "nn", dp1, wt, (s, d, tka), tm=tm, tn=t, tk=tka)
    du = mm("mix_du_b", "nn", dp2, wt, (s, d, tka), tm=tm, tn=t, tk=tka, add=du,
            b_spec=pl.BlockSpec((tka, t), lambda i, j, kk: (1 + kk, j)))
    du = mm("mix_du_g", "nn", dg, wt_g, (s, d, 2 * d), tm=tm, tn=t, tk=d, add=du, thin=(dlrp, wt_lr),
            a_spec=pl.BlockSpec((None, tm, d), lambda i, j, kk: (kk, i, 0)))
    nkg = d // t
    dw1 = mm("mix_dw_a", "tn", dp1, u2, (3 * aw, d, s), tm=t, tn=t, tk=tks)
    dw2 = mm("mix_dw_b", "tn", dp2, u2, (6 * kd, d, s), tm=t, tn=t, tk=tks)
    dwlr = mm("mix_dw_lr", "tn", dlrp, u2, (LANES, d, s), tm=LANES, tn=t, tk=tks)
    dwg = mm("mix_dw_g", "tn", dg, u2, (2 * d, d, s), tm=t, tn=t, tk=tks,
             a_spec=pl.BlockSpec((None, tks, t), lambda i, j, kk: (i // nkg, kk, i % nkg)))
    pieces = [dw1, dw2, dwlr[:GATE_RANK], dwg]
    shard_rows = sum(p.shape[0] for p in pieces) // N_CHIPS
    plan.grad("w_mix_in", jnp.stack([_virtual_rows(pieces, j * shard_rows, (j + 1) * shard_rows)
                                     for j in range(N_CHIPS)]))
    return du, (dbias, dwa2p[:GATE_RANK], gsm)


def _device_step(x, target, mod, small, plan):
    s, d = x.shape
    row = lambda i: mod[i:i + 1]
    sh1, sc1, g1, sh2, sc2, g2, sh3, sc3, g3 = (row(i) for i in range(N_MOD))

    onehot = _rel_onehot()
    bias = _mm("rel_bias_expand", "nn", small["rel_bias"], onehot, (A_HEADS, CHUNK * A_BAND, REL_SIZE),
               tm=A_HEADS, tn=4608, tk=REL_SIZE, precision=HIGHEST).reshape(A_HEADS, CHUNK, A_BAND)
    bias = _band_bias(bias)
    wa2p = jnp.pad(small["w_alpha2"], ((0, LANES - GATE_RANK), (0, 0))).astype(BF16)
    mix_small = (bias, wa2p, small["b_alpha"], small["gla_norm_g"])

    u1 = _modulate("mod1", x, sh1, sc1)
    f1, sv1 = _ffn_forward("ffn1", u1, plan)
    h1, u2 = _resid_ln_fwd("ln1_fwd", x, f1, g1, small["ln1_g"], small["ln1_b"], sh2, sc2, 0.5)
    m, svm = _mix_forward(u2, plan, mix_small)
    h2, u3 = _resid_ln_fwd("ln2_fwd", h1, m, g2, small["ln2_g"], small["ln2_b"], sh3, sc3, 1.0)
    f2, sv2 = _ffn_forward("ffn2", u3, plan)

    dr3, df2, acc3 = _final_ln_loss_bwd("ln3_loss_bwd", h2, f2, target, g3, small["ln3_g"], small["ln3_b"], 0.5)
    du3 = _ffn_backward("ffn2", df2, u3, sv2, plan, in_first=False)
    dr2, dmx, acc2 = _resid_ln_bwd("ln2_bwd", du3, dr3, h1, m, sc3, g2, small["ln2_g"], small["ln2_b"], 1.0)
    du2, (dbias, dwa2, gsm) = _mix_backward(dmx, u2, svm, plan, mix_small)
    dr1, df1, acc1 = _resid_ln_bwd("ln1_bwd", du2, dr2, x, f1, sc2, g1, small["ln1_g"], small["ln1_b"], 0.5)
    du1 = _ffn_backward("ffn1", df1, u1, sv1, plan, in_first=True)
    grad_x, acc0 = _input_grad("input_grad", du1, dr1, x, sc1)

    drel = _hosted_mm(plan)("rel_bias_grad", "nt", _band_bias_grad(dbias).reshape(A_HEADS, CHUNK * A_BAND), onehot,
                            (A_HEADS, REL_SIZE, CHUNK * A_BAND), tm=A_HEADS, tn=REL_SIZE, tk=4608, precision=HIGHEST)
    loss = jnp.sum(acc3[ROW_LOSS])
    dmod = jnp.stack([acc0[ROW_DSH], acc0[ROW_DSC], acc1[ROW_DGATE], acc1[ROW_DSH], acc1[ROW_DSC], acc2[ROW_DGATE],
                      acc2[ROW_DSH], acc2[ROW_DSC], acc3[ROW_DGATE]])
    kd = d // 4
    small_grads = dict(ln1_g=acc1[ROW_DLN_G], ln1_b=acc1[ROW_DLN_B], ln2_g=acc2[ROW_DLN_G], ln2_b=acc2[ROW_DLN_B],
                       ln3_g=acc3[ROW_DLN_G], ln3_b=acc3[ROW_DLN_B], b_alpha=gsm[GLA_ROW_DBALPHA],
                       gla_norm_g=gsm[GLA_ROW_DGNORM, :kd // B_HEADS * 2], rel_bias=drel, w_alpha2=dwa2)
    return loss, grad_x, small_grads, dmod


HBM_SPEC = pl.BlockSpec(memory_space=pl.ANY)


def _mesh_pos():
    return lax.axis_index("x"), lax.axis_index("y"), lax.axis_index("c")


def _other_chips(x, y):
    return [(1 - x, y), (x, 1 - y), (1 - x, 1 - y)]


def _remote(src, dst, send_sem, recv_sem, to):
    return pltpu.make_async_remote_copy(src_ref=src, dst_ref=dst, send_sem=send_sem, recv_sem=recv_sem,
                                        device_id=to, device_id_type=MESH)


def _allgather_rows(name, v):
    m_per, n = v.shape

    def body(x_ref, out_ref, send_sems, recv_sems, local_sem):
        x, y, c = _mesh_pos()
        me, sibling = (x, y, c), (x, y, 1 - c)
        chips = _other_chips(x, y)

        def rows(px, py, pc):
            return out_ref.at[pl.ds((4 * px + 2 * py + pc) * m_per, m_per), :]

        def copy(k, block, to, src=None):
            return _remote(rows(*block) if src is None else src, rows(*block), send_sems.at[k], recv_sems.at[k], to)

        mine = pltpu.make_async_copy(x_ref, rows(*me), local_sem)
        mine.start()
        first = [copy(0, me, sibling, src=x_ref)]
        first += [copy(1 + j, me, (*chip, c), src=x_ref) for j, chip in enumerate(chips)]
        for cp in first:
            cp.start()
        passed = [copy(4 + j, (*chip, c), sibling) for j, chip in enumerate(chips)]
        for j, chip in enumerate(chips):
            copy(1 + j, (*chip, c), me).wait_recv()
            passed[j].start()
        copy(0, sibling, me).wait_recv()
        for j, chip in enumerate(chips):
            copy(4 + j, (*chip, 1 - c), me).wait_recv()
        for cp in first + passed:
            cp.wait_send()
        mine.wait()

    return pl.pallas_call(
        body, name=name, out_shape=jax.ShapeDtypeStruct((N_DEV * m_per, n), v.dtype),
        in_specs=[pl.BlockSpec(memory_space=pltpu.VMEM)], out_specs=pl.BlockSpec(memory_space=pltpu.VMEM),
        scratch_shapes=[pltpu.SemaphoreType.DMA((7,)), pltpu.SemaphoreType.DMA((7,)), pltpu.SemaphoreType.DMA],
    )(v)


def _allgather_weights(shards):
    n = len(shards)
    TO_X, TO_Y, PASS_TO_X, PASS_TO_Y, SIB_X, SIB_Y, SIB_D0, SIB_D1 = range(8)

    def body(*refs):
        ins, outs = refs[:n], refs[n:2 * n]
        send_sems, recv_sems = refs[2 * n:]
        x, y, c = _mesh_pos()
        sibling = (x, y, 1 - c)
        xn, yn, dg = _other_chips(x, y)
        j0, jx, jy, jd = (2 * p[0] + p[1] for p in ((x, y), xn, yn, dg))
        sends = []

        def rows(w, hc, quarter=None):
            hr = shards[w].shape[0] // 2
            if quarter is None:
                return pl.ds(hc * hr, hr)
            return pl.ds(hc * hr + quarter * (hr // 2), hr // 2)

        def push(src, dst, w, k, to):
            cp = _remote(src, dst, send_sems.at[w, k], recv_sems.at[w, k], to)
            cp.start()
            sends.append(cp)

        def landed(piece, w, k):
            _remote(piece, piece, send_sems.at[w, k], recv_sems.at[w, k], sibling).wait_recv()

        for w in range(n):
            mine = rows(w, c)
            push(ins[w].at[mine, :], outs[w].at[j0, mine, :], w, TO_X, (*xn, c))
            push(ins[w].at[mine, :], outs[w].at[j0, mine, :], w, TO_Y, (*yn, c))
        for w in range(n):
            half_x = outs[w].at[jx, rows(w, c), :]
            landed(half_x, w, TO_X)
            quarter = outs[w].at[jx, rows(w, c, 1), :]
            push(quarter, quarter, w, PASS_TO_Y, (*yn, c))
            push(half_x, half_x, w, SIB_X, sibling)
            half_y = outs[w].at[jy, rows(w, c), :]
            landed(half_y, w, TO_Y)
            quarter = outs[w].at[jy, rows(w, c, 0), :]
            push(quarter, quarter, w, PASS_TO_X, (*xn, c))
            push(half_y, half_y, w, SIB_Y, sibling)
        for w in range(n):
            for q, arrives_on, on in ((0, PASS_TO_X, SIB_D0), (1, PASS_TO_Y, SIB_D1)):
                piece = outs[w].at[jd, rows(w, c, q), :]
                landed(piece, w, arrives_on)
                push(piece, piece, w, on, sibling)
        for w in range(n):
            landed(outs[w].at[jx, rows(w, 1 - c), :], w, SIB_X)
            landed(outs[w].at[jy, rows(w, 1 - c), :], w, SIB_Y)
            landed(outs[w].at[jd, rows(w, 1 - c, 0), :], w, SIB_D0)
            landed(outs[w].at[jd, rows(w, 1 - c, 1), :], w, SIB_D1)
        for cp in sends:
            cp.wait_send()

    return pl.pallas_call(
        body, name="allgather_weights",
        out_shape=[jax.ShapeDtypeStruct((N_CHIPS,) + sh.shape, sh.dtype) for sh in shards],
        in_specs=[HBM_SPEC] * n, out_specs=[HBM_SPEC] * n,
        scratch_shapes=[pltpu.SemaphoreType.DMA((n, 8)), pltpu.SemaphoreType.DMA((n, 8))],
    )(*shards)


def _half(ref, hc, col, *lead):
    rows, cols = ref.shape[-2:]
    if col:
        return ref.at[(*lead, slice(None), pl.ds(hc * (cols // 2), cols // 2))]
    return ref.at[(*lead, pl.ds(hc * (rows // 2), rows // 2), slice(None))]


def _half_shape(shape, col):
    return shape[:-2] + ((shape[-2], shape[-1] // 2) if col else (shape[-2] // 2, shape[-1]))


def _quarter(ref, hc, q, col, *lead):
    rows, cols = ref.shape[-2:]
    if col:
        return ref.at[(*lead, slice(None), pl.ds(hc * (cols // 2) + q * (cols // 4), cols // 4))]
    return ref.at[(*lead, pl.ds(hc * (rows // 2) + q * (rows // 4), rows // 4), slice(None))]


def _stage_gather_ici(shards, cols):
    n = len(shards)
    TO_X, TO_Y, PASS_TO_X, PASS_TO_Y = range(4)

    def places():
        x, y, c = _mesh_pos()
        xn, yn, dg = _other_chips(x, y)
        return c, (*xn, c), (*yn, c), [2 * p[0] + p[1] for p in ((x, y), xn, yn, dg)]

    def remote(src, dst, send, recv, w, k, to):
        return _remote(src, dst, send.at[4 * w + k], recv.at[4 * w + k], to)

    def own(ins, outs, send, recv):
        c, to_x, to_y, (j0, _, _, _) = places()
        for w in range(n):
            for k, to in ((TO_X, to_x), (TO_Y, to_y)):
                yield remote(_half(ins[w], c, cols[w]), _half(outs[w], c, cols[w], j0), send, recv, w, k, to)

    def relays(ins, outs, send, recv):
        c, to_x, to_y, (_, jx, jy, _) = places()
        for w in range(n):
            for j, k, q, pass_k, to in ((jx, TO_X, 1, PASS_TO_Y, to_y), (jy, TO_Y, 0, PASS_TO_X, to_x)):
                half = _half(outs[w], c, cols[w], j)
                piece = _quarter(outs[w], c, q, cols[w], j)
                yield remote(half, half, send, recv, w, k, to), remote(piece, piece, send, recv, w, pass_k, to)

    def passed(ins, outs, send, recv):
        c, to_x, _, (_, _, _, jd) = places()
        for w in range(n):
            for q, k in ((0, PASS_TO_X), (1, PASS_TO_Y)):
                piece = _quarter(outs[w], c, q, cols[w], jd)
                yield remote(piece, piece, send, recv, w, k, to_x)

    def start(*refs):
        for cp in own(*refs):
            cp.start()

    def relay(*refs):
        for arrived, onward in relays(*refs):
            arrived.wait_recv()
            onward.start()

    def finish(*refs):
        for cp in passed(*refs):
            cp.wait_recv()
        for cp in own(*refs):
            cp.wait_send()
        for _, onward in relays(*refs):
            onward.wait_send()

    outs = [jax.ShapeDtypeStruct((N_CHIPS,) + sh.shape, sh.dtype) for sh in shards]
    return _Stage(shards, outs, 4 * n, start, finish, relay=relay)


def _stage_gather_d2d(partial, cols):
    n = len(partial)

    def copies(ins, outs, send, recv):
        x, y, c = _mesh_pos()
        for w in range(n):
            for r, chip in enumerate(_other_chips(x, y)):
                jr = 2 * chip[0] + chip[1]
                mine = _remote(_half(ins[w], c, cols[w], jr), _half(outs[w], c, cols[w], jr), send.at[3 * w + r],
                               recv.at[3 * w + r], (x, y, 1 - c))
                got = _half(outs[w], 1 - c, cols[w], jr)
                yield mine, _remote(got, got, send.at[3 * w + r], recv.at[3 * w + r], (x, y, 1 - c))

    def start(*refs):
        for mine, _ in copies(*refs):
            mine.start()

    def finish(*refs):
        pairs = list(copies(*refs))
        for _, theirs in pairs:
            theirs.wait_recv()
        for mine, _ in pairs:
            mine.wait_send()

    outs = [jax.ShapeDtypeStruct(p.shape, p.dtype) for p in partial]
    return _Stage(partial, outs, 3 * n, start, finish, aliases={w: w for w in range(n)})


def _stage_exchange_halves(grads, cols):
    n = len(grads)

    def copies(ins, outs, send, recv):
        x, y, c = _mesh_pos()
        for w in range(n):
            yield _remote(_half(ins[w], 1 - c, cols[w], slice(None)), outs[w], send.at[w], recv.at[w], (x, y, 1 - c))

    def start(*refs):
        for cp in copies(*refs):
            cp.start()

    def finish(*refs):
        cps = list(copies(*refs))
        for cp in cps:
            cp.wait_recv()
        for cp in cps:
            cp.wait_send()

    outs = [jax.ShapeDtypeStruct(_half_shape(g.shape, col), g.dtype) for g, col in zip(grads, cols)]
    return _Stage(grads, outs, n, start, finish)


def _stage_scatter(parts):
    n = len(parts)

    def copies(ins, outs, send, recv):
        x, y, c = _mesh_pos()
        for w in range(n):
            for r, chip in enumerate(_other_chips(x, y)):
                jr = 2 * chip[0] + chip[1]
                yield _remote(ins[w].at[jr], outs[w].at[r], send.at[3 * w + r], recv.at[3 * w + r], (*chip, c))

    def start(*refs):
        for cp in copies(*refs):
            cp.start()

    def finish(*refs):
        cps = list(copies(*refs))
        for cp in cps:
            cp.wait_recv()
        for cp in cps:
            cp.wait_send()

    outs = [jax.ShapeDtypeStruct((3,) + p.shape[1:], p.dtype) for p in parts]
    return _Stage(parts, outs, 3 * n, start, finish)


def _stage_share(fulls, cols):
    n = len(fulls)

    def copies(ins, outs, send, recv):
        x, y, c = _mesh_pos()
        for w in range(n):
            theirs = _half(outs[w], 1 - c, cols[w])
            yield (_remote(_half(ins[w], c, cols[w]), _half(outs[w], c, cols[w]), send.at[w], recv.at[w], (x, y, 1 - c)),
                   _remote(theirs, theirs, send.at[w], recv.at[w], (x, y, 1 - c)))

    def start(*refs):
        for mine, _ in copies(*refs):
            mine.start()

    def finish(*refs):
        pairs = list(copies(*refs))
        for _, theirs in pairs:
            theirs.wait_recv()
        for mine, _ in pairs:
            mine.wait_send()

    outs = [jax.ShapeDtypeStruct(h.shape, h.dtype) for h in fulls]
    return _Stage(fulls, outs, n, start, finish, aliases={w: w for w in range(n)})


def _run_stages(name, stages):
    return _pcall(None, stages, name=name, out_shape=[], in_specs=[], out_specs=[])()[1]


TILE_BYTES = 2 * 1024 * 1024
SUM_TILE_BYTES = 4 * 1024 * 1024


def _row_tile(rows, cols, itemsize=4, tile_bytes=TILE_BYTES):
    for t in (1024, 512, 256, 128, 64, 32, 16, 8):
        if rows % t == 0 and t * cols * itemsize <= tile_bytes:
            return t
    return rows


def _col_tile(rows, cols, itemsize=4, tile_bytes=TILE_BYTES):
    for t in (2048, 1024, 512, 256, 128):
        if cols % t == 0 and t * rows * itemsize <= tile_bytes:
            return t
    return cols


def _tiling(rows, cols, col, tile_bytes=TILE_BYTES):
    if col:
        tc = _col_tile(rows, cols, tile_bytes=tile_bytes)
        return (rows, tc), cols // tc
    tr = _row_tile(rows, cols, tile_bytes=tile_bytes)
    return (tr, cols), rows // tr


def _strip(col, i):
    return (0, i) if col else (i, 0)


def _pair_sum(name, g, recv, core, col):
    blk, nb = _tiling(*recv.shape[1:], col, tile_bytes=SUM_TILE_BYTES)

    def body(c_ref, g_ref, r_ref, o_ref):
        o_ref[...] = (g_ref[...] + r_ref[...]).astype(BF16)

    grid_spec = pltpu.PrefetchScalarGridSpec(
        num_scalar_prefetch=1, grid=(N_CHIPS, nb),
        in_specs=[pl.BlockSpec((None,) + blk, lambda j, i, cr: (j,) + _strip(col, cr[0] * nb + i)),
                  pl.BlockSpec((None,) + blk, lambda j, i, cr: (j,) + _strip(col, i))],
        out_specs=pl.BlockSpec((None,) + blk, lambda j, i, cr: (j,) + _strip(col, i)))
    return pl.pallas_call(body, name=name, out_shape=jax.ShapeDtypeStruct(recv.shape, BF16), grid_spec=grid_spec,
                          compiler_params=_cp("parallel", "parallel"))(core, g, recv)


def _quad_sum(name, own, landed, chip_core, col):
    rows, cols = landed.shape[1:]
    blk, nb = _tiling(rows, cols, col, tile_bytes=SUM_TILE_BYTES)
    full = (rows, 2 * cols) if col else (2 * rows, cols)

    def body(cc_ref, own_ref, l_ref, o_ref):
        o_ref[...] = ((own_ref[...].astype(F32) + l_ref[0].astype(F32)) + l_ref[1].astype(F32)) + l_ref[2].astype(F32)

    grid_spec = pltpu.PrefetchScalarGridSpec(
        num_scalar_prefetch=1, grid=(nb,),
        in_specs=[pl.BlockSpec((None,) + blk, lambda i, cc: (cc[0],) + _strip(col, i)),
                  pl.BlockSpec((3,) + blk, lambda i, cc: (0,) + _strip(col, i))],
        out_specs=pl.BlockSpec(blk, lambda i, cc: _strip(col, cc[1] * nb + i)))
    return pl.pallas_call(body, name=name, out_shape=jax.ShapeDtypeStruct(full, F32), grid_spec=grid_spec,
                          compiler_params=_cp("arbitrary"))(chip_core, own, landed)


def _device_sum(name, gathered):
    def body(g_ref, o_ref):
        total = g_ref[0]
        for k in range(1, N_DEV):
            total = total + g_ref[k]
        o_ref[...] = total

    return pl.pallas_call(body, name=name, out_shape=jax.ShapeDtypeStruct(gathered.shape[1:], F32))(gathered)


def _adamw(name, w, g, m, v):
    rows, cols = w.shape
    col = rows % SUBLANES != 0
    blk, nb = _tiling(rows, cols, col)
    bc1 = 1.0 - ADAM_B1 ** ADAM_STEP
    bc2 = 1.0 - ADAM_B2 ** ADAM_STEP

    def body(w_ref, g_ref, m_ref, v_ref, d_ref, mo_ref, vo_ref):
        gv = g_ref[...]
        mn = ADAM_B1 * m_ref[...] + (1.0 - ADAM_B1) * gv
        vn = ADAM_B2 * v_ref[...] + (1.0 - ADAM_B2) * (gv * gv)
        mo_ref[...] = mn
        vo_ref[...] = vn
        d_ref[...] = -ADAM_LR * ((mn / bc1) / (jnp.sqrt(vn / bc2) + ADAM_EPS) + ADAM_WD * w_ref[...])

    spec = pl.BlockSpec(blk, lambda i: _strip(col, i))
    return pl.pallas_call(
        body, name=name, out_shape=[jax.ShapeDtypeStruct((rows, cols), F32)] * 3, grid=(nb,),
        in_specs=[spec] * 4, out_specs=[spec] * 3, compiler_params=_cp("parallel"),
    )(w, g, m, v)


WEIGHTS = ["w_ada", "b_ada", "ffn1_w_in", "ffn1_w_out", "ln1_g", "ln1_b", "w_mix_in", "rel_bias", "w_alpha2",
           "b_alpha", "gla_norm_g", "w_proj_a", "w_proj_b", "w_mix_out", "ln2_g", "ln2_b", "ffn2_w_in", "ffn2_w_out",
           "ln3_g", "ln3_b"]
BIG = {"ffn1_w_in": True, "ffn1_w_out": False, "w_mix_in": False, "w_proj_a": True, "w_proj_b": True,
       "w_mix_out": False, "ffn2_w_in": True, "ffn2_w_out": False}
TRANSPOSED = ("w_mix_in",)
STACKED = ("ffn1_w_in", "ffn2_w_in", "w_mix_in")
GROUP_FFN1 = ("ffn1_w_in", "ffn1_w_out")
GROUP_PROJ = ("w_proj_a", "w_proj_b", "w_mix_out")
SMALL = ["ln1_g", "ln1_b", "ln2_g", "ln2_b", "ln3_g", "ln3_b", "b_alpha", "gla_norm_g", "rel_bias", "w_alpha2"]


def _pad_rows(vec, rows=SUBLANES):
    per = -(-vec.shape[0] // (rows * LANES)) * LANES
    return jnp.pad(vec, (0, rows * per - vec.shape[0])).reshape(rows, per)


def _silu(v):
    return v * _sigmoid(v)


class _MeshPlan:
    def __init__(self, shards, chip, core):
        self.shards, self.chip = shards, chip
        self.core1 = core.astype(jnp.int32).reshape(1)
        self.chip_core = jnp.stack([chip, core]).astype(jnp.int32)
        self.partial, self.full, self.local, self.pair, self.half, self.final, self.memos = {}, {}, {}, {}, {}, {}, {}
        ici, d2d, x1, x2, x3 = self.gather_ici, self.gather_d2d, self.exchange, self.scatter, self.share
        mix_in, in1, out1, in2, out2 = ("w_mix_in",), ("ffn1_w_in",), ("ffn1_w_out",), ("ffn2_w_in",), ("ffn2_w_out",)
        self.schedule = {
            "ffn1_in_fwd": [ici(mix_in)], "ffn1_out_fwd": [d2d(mix_in), ici(out2)],
            "mix_in_a": [d2d(out2)], "mix_in_g": [ici(GROUP_PROJ)],
            "attn_fwd": [ici(in2), d2d(GROUP_PROJ)], "gla_fwd": [d2d(in2)],
            "ffn2_dw_in": [x1(out2)], "ffn2_du": [x2(out2), x1(in2)], "mix_dw_out": [x3(out2)],
            "attn_bwd": [x2(in2)], "gla_bwd": [x3(in2), x1(GROUP_PROJ)],
            "mix_du_g": [x2(GROUP_PROJ)], "mix_dw_g": [x3(GROUP_PROJ)],
            "ffn1_out_bwd": [x1(mix_in)], "ffn1_dw_in": [x2(mix_in)], "ffn1_dw_out": [x3(mix_in), x1(in1)],
            "ffn1_du": [x2(in1), x1(out1)], "rel_bias_grad": [x2(out1), x3(in1)],
        }

    def weight(self, k):
        return self.full[k]

    def grad(self, k, g):
        r, cc = self.shards[k].shape
        if k not in STACKED:
            g = g.reshape(r, N_CHIPS, cc).transpose(1, 0, 2) if BIG[k] else g.reshape(N_CHIPS, r, cc)
        self.local[k] = g

    def memo(self, key, make):
        if key not in self.memos:
            self.memos[key] = make()
        return self.memos[key]

    def host(self, name, call):
        builders = self.schedule.get(name)
        if not builders:
            return call(None)
        built = [b() for b in builders]
        main, comm = call([st for st, _ in built])
        for (_, post), res in zip(built, comm):
            post(res)
        return main

    def run(self, name, builders):
        built = [b() for b in builders]
        for (_, post), res in zip(built, _run_stages(name, [st for st, _ in built])):
            post(res)

    def set_gathered(self, names, gathered):
        for k, g in zip(names, gathered):
            _, r, cc = g.shape
            g = lax.dynamic_update_slice(g, self.shards[k][None], (self.chip, 0, 0))
            if k not in STACKED:
                g = g.transpose(1, 0, 2).reshape(r, N_CHIPS * cc) if BIG[k] else g.reshape(N_CHIPS * r, cc)
            self.full[k] = g

    @staticmethod
    def cols(names):
        return [k in TRANSPOSED for k in names]

    def gather_ici(self, names):
        def post(res):
            self.partial.update(zip(names, res))
        return lambda: (_stage_gather_ici([self.shards[k] for k in names], self.cols(names)), post)

    def gather_d2d(self, names):
        return lambda: (_stage_gather_d2d([self.partial[k] for k in names], self.cols(names)),
                        lambda res: self.set_gathered(names, res))

    def exchange(self, names):
        def post(res):
            for k, r in zip(names, res):
                self.pair[k] = _pair_sum(f"pair_sum_{k}", self.local[k], r, self.core1, k in TRANSPOSED)
        return lambda: (_stage_exchange_halves([self.local[k] for k in names], self.cols(names)), post)

    def scatter(self, names):
        def post(res):
            for k, landed in zip(names, res):
                self.half[k] = _quad_sum(f"quad_sum_{k}", self.pair[k], landed, self.chip_core, k in TRANSPOSED)
        return lambda: (_stage_scatter([self.pair[k] for k in names]), post)

    def share(self, names):
        def post(res):
            self.final.update(zip(names, res))
        return lambda: (_stage_share([self.half[k] for k in names], self.cols(names)), post)


def _step(args):
    x_pos, y_pos, c_pos = _mesh_pos()
    chip = 2 * x_pos + y_pos
    dev = 4 * x_pos + 2 * y_pos + c_pos
    take = lambda name, k: args[name][0].T if k in TRANSPOSED else args[name][0]
    w = {k: take(k, k) for k in WEIGHTS}
    mom = {k: take("m_" + k, k) for k in WEIGHTS}
    vel = {k: take("v_" + k, k) for k in WEIGHTS}
    x = args["x"][0]
    target = args["loss_target"][0]
    s, d = x.shape
    kd = d // 4
    rel_sh = w["rel_bias"].shape[1]
    wa2_sh = w["w_alpha2"].shape[1]
    ada_sh = w["w_ada"].shape[1]

    n_rel, n_wa2 = A_HEADS * rel_sh, GATE_RANK * wa2_sh
    packed = _pad_rows(jnp.concatenate([args["c"].reshape(-1), w["rel_bias"].reshape(-1), w["w_alpha2"].reshape(-1)]))
    got = _allgather_rows("gather_small_inputs", packed).reshape(N_DEV, -1)
    c_all = got[:, :d]
    per_chip = got[0::2]
    rel_bias = per_chip[:, d:d + n_rel].reshape(N_CHIPS, A_HEADS, rel_sh).transpose(1, 0, 2).reshape(A_HEADS, -1)
    w_alpha2 = per_chip[:, d + n_rel:d + n_rel + n_wa2].reshape(N_CHIPS, GATE_RANK, wa2_sh).transpose(1, 0, 2)
    w_alpha2 = w_alpha2.reshape(GATE_RANK, -1)

    b_shard = lax.dynamic_slice(w["b_ada"], (chip * ada_sh,), (ada_sh,))
    mod_shard = _mm("ada_fwd", "nn", c_all, w["w_ada"], (N_DEV, ada_sh, d), tm=N_DEV, tn=_tile(ada_sh, (512, 128)),
                    tk=d, precision=HIGHEST, a_fn=_silu, add=jnp.broadcast_to(b_shard[None], (N_DEV, ada_sh)))
    mod_all = _allgather_rows("gather_mod", mod_shard).reshape(N_DEV, N_DEV, ada_sh)[0::2]
    mod_all = mod_all.transpose(1, 0, 2).reshape(N_DEV, N_MOD * d)
    mod = lax.dynamic_index_in_dim(mod_all, dev, 0, keepdims=False).reshape(N_MOD, d)

    names = list(BIG)
    plan = _MeshPlan({k: w[k].astype(BF16) for k in names}, chip, c_pos)
    plan.set_gathered(GROUP_FFN1, _allgather_weights([plan.shards[k] for k in GROUP_FFN1]))

    small = dict(rel_bias=rel_bias, w_alpha2=w_alpha2, b_alpha=w["b_alpha"][None], gla_norm_g=w["gla_norm_g"][None])
    for k in ("ln1_g", "ln1_b", "ln2_g", "ln2_b", "ln3_g", "ln3_b"):
        small[k] = w[k][None]
    loss_local, grad_x, small_grads, dmod = _device_step(x, target, mod, small, plan)
    loss = lax.psum(loss_local, ("x", "y", "c"))
    plan.run("grad_tail_share", [plan.share(GROUP_FFN1[1:])])

    flat = jnp.concatenate([small_grads[k].reshape(-1) for k in SMALL] + [dmod.reshape(-1)])
    n_small = flat.shape[0] - N_MOD * d
    packed = _pad_rows(flat)
    all_small = _allgather_rows("gather_small_grads", packed).reshape(N_DEV, SUBLANES, -1)
    summed = _device_sum("small_grad_sum", all_small).reshape(-1)
    dmod_all = all_small.reshape(N_DEV, -1)[:, n_small:n_small + N_MOD * d]
    dmod_shard = lax.dynamic_slice(dmod_all, (0, chip * ada_sh), (N_DEV, ada_sh))
    grads = {"b_ada": summed[n_small:n_small + N_MOD * d]}
    off = 0
    for k in SMALL:
        size = small_grads[k].size
        grads[k] = summed[off:off + size].reshape(small_grads[k].shape)
        off += size
    grads["rel_bias"] = lax.dynamic_slice(grads["rel_bias"], (0, chip * rel_sh), (A_HEADS, rel_sh))
    grads["w_alpha2"] = lax.dynamic_slice(grads["w_alpha2"], (0, chip * wa2_sh), (GATE_RANK, wa2_sh))
    grads["w_ada"] = _mm("ada_bwd", "nn", jnp.pad(c_all.T, ((0, 0), (0, LANES - N_DEV))),
                         jnp.pad(dmod_shard, ((0, LANES - N_DEV), (0, 0))), (d, ada_sh, LANES), tm=_tile(d, (1024,)),
                         tn=_tile(ada_sh, (512, 128)), tk=LANES, precision=HIGHEST, a_fn=_silu)

    grads.update(plan.final)

    delta, new_m, new_v = {}, {}, {}
    for k in ["w_ada"] + names:
        delta[k], new_m[k], new_v[k] = _adamw(f"adamw_{k}", w[k], grads[k], mom[k], vel[k])
    tiny = ["b_ada"] + SMALL
    pack = lambda src: _pad_rows(jnp.concatenate([src[k].reshape(-1) for k in tiny]), rows=1).reshape(-1, LANES)
    outs = _adamw("adamw_small", pack(w), pack(grads), pack(mom), pack(vel))
    off = 0
    for k in tiny:
        size = w[k].size
        for dst, src in zip((delta, new_m, new_v), outs):
            dst[k] = src.reshape(-1)[off:off + size].reshape(w[k].shape)
        off += size

    give = lambda src: [src[k].T[None] if k in TRANSPOSED else src[k][None] for k in WEIGHTS]
    return (loss, grad_x[None], *give(grads), *give(delta), *give(new_m), *give(new_v))


def kernel(x, c, w_ada, b_ada, ffn1_w_in, ffn1_w_out, ln1_g, ln1_b, w_mix_in, rel_bias, w_alpha2, b_alpha, gla_norm_g, w_proj_a, w_proj_b, w_mix_out, ln2_g, ln2_b, ffn2_w_in, ffn2_w_out, ln3_g, ln3_b, loss_target, m_w_ada, m_b_ada, m_ffn1_w_in, m_ffn1_w_out, m_ln1_g, m_ln1_b, m_w_mix_in, m_rel_bias, m_w_alpha2, m_b_alpha, m_gla_norm_g, m_w_proj_a, m_w_proj_b, m_w_mix_out, m_ln2_g, m_ln2_b, m_ffn2_w_in, m_ffn2_w_out, m_ln3_g, m_ln3_b, v_w_ada, v_b_ada, v_ffn1_w_in, v_ffn1_w_out, v_ln1_g, v_ln1_b, v_w_mix_in, v_rel_bias, v_w_alpha2, v_b_alpha, v_gla_norm_g, v_w_proj_a, v_w_proj_b, v_w_mix_out, v_ln2_g, v_ln2_b, v_ffn2_w_in, v_ffn2_w_out, v_ln3_g, v_ln3_b):
    return _step(dict(locals()))
```

```python
import functools

import jax
import jax.numpy as jnp
from jax import lax
from jax.experimental import pallas as pl
from jax.experimental.pallas import tpu as pltpu

F32 = jnp.float32
BF16 = jnp.bfloat16
MESH = pl.DeviceIdType.MESH
HIGHEST = lax.Precision.HIGHEST

VMEM_LIMIT_BYTES = 56 * 1024 * 1024
LANES = 128
SUBLANES = 8

CHUNK = 64
A_HEADS = 16
A_HEAD_DIM = 64
A_PAST_CHUNKS = 8
A_BAND = (A_PAST_CHUNKS + 1) * CHUNK
A_PAD = A_PAST_CHUNKS * CHUNK
REL_CLIP = 256
REL_SIZE = REL_CLIP + CHUNK
B_HEADS = 4
GATE_RANK = 16
GATE_TAU = 16.0
N_MOD = 9
DEPTH = 1
ALPHA = (2.0 * DEPTH) ** 0.25
LN_EPS = 1e-5
RMS_EPS = 1e-6
ADAM_LR = 0.001
ADAM_B1 = 0.9
ADAM_B2 = 0.999
ADAM_EPS = 1e-08
ADAM_WD = 0.01
ADAM_STEP = 10
NEG_BIG = -1e30

N_CHIPS = 4
N_DEV = 8


def _cp(*sem):
    return pltpu.CompilerParams(dimension_semantics=sem, vmem_limit_bytes=VMEM_LIMIT_BYTES)


class _Stage:
    def __init__(self, arrays, out_shapes, n_sems, start, finish, aliases=None, relay=None):
        self.arrays, self.out_shapes, self.n_sems = list(arrays), list(out_shapes), n_sems
        self.start, self.finish, self.relay, self.aliases = start, finish, relay, dict(aliases or {})


def _pcall(body, stages, *, name, out_shape, in_specs, out_specs, grid=(), scratch_shapes=(), compiler_params=None):
    single = not isinstance(out_shape, (list, tuple))
    outs = [out_shape] if single else list(out_shape)
    ospecs = [out_specs] if single else list(out_specs)
    in_specs, scratch_shapes = list(in_specs), list(scratch_shapes)
    n_in, n_out, n_sc = len(in_specs), len(outs), len(scratch_shapes)
    stages = list(stages or [])
    c_in = [a for st in stages for a in st.arrays]
    c_out = [o for st in stages for o in st.out_shapes]
    aliases = {}
    io, oo = n_in, n_out
    for st in stages:
        for a, b in st.aliases.items():
            aliases[io + a] = oo + b
        io += len(st.arrays)
        oo += len(st.out_shapes)

    def wrapped(*refs):
        ins = refs[:n_in]
        cins = refs[n_in:n_in + len(c_in)]
        base = n_in + len(c_in)
        mouts = refs[base:base + n_out]
        couts = refs[base + n_out:base + n_out + len(c_out)]
        base += n_out + len(c_out)
        scr = refs[base:base + n_sc]
        sems = refs[base + n_sc:]

        def each(phase):
            i = o = 0
            for k, st in enumerate(stages):
                fn = (st.start, st.relay, st.finish)[phase]
                if fn is not None:
                    fn(cins[i:i + len(st.arrays)], couts[o:o + len(st.out_shapes)], sems[2 * k], sems[2 * k + 1])
                i += len(st.arrays)
                o += len(st.out_shapes)

        if stages and grid:
            step = functools.reduce(lambda acc, a: acc * grid[a] + pl.program_id(a), range(len(grid)), 0)
            steps = functools.reduce(lambda a, b: a * b, grid)
            pl.when(step == 0)(lambda: each(0))
            if any(st.relay for st in stages):
                pl.when(step == (2 * steps) // 3)(lambda: each(1))
            if body is not None:
                body(*ins, *mouts, *scr)
            pl.when(step == steps - 1)(lambda: each(2))
        else:
            each(0)
            each(1)
            if body is not None:
                body(*ins, *mouts, *scr)
            each(2)

    sem_shapes = []
    for st in stages:
        sem_shapes += [pltpu.SemaphoreType.DMA((st.n_sems,)), pltpu.SemaphoreType.DMA((st.n_sems,))]
    kwargs = dict(grid=grid) if grid else {}
    if compiler_params is not None:
        kwargs["compiler_params"] = compiler_params

    def run(*operands):
        res = pl.pallas_call(
            wrapped, name=name, out_shape=outs + c_out, in_specs=in_specs + [HBM_SPEC] * len(c_in),
            out_specs=ospecs + [HBM_SPEC] * len(c_out), scratch_shapes=scratch_shapes + sem_shapes,
            input_output_aliases=aliases, **kwargs)(*operands, *c_in)
        main = res[0] if single else tuple(res[:n_out])
        if not stages:
            return main
        comm, o = [], n_out
        for st in stages:
            comm.append(list(res[o:o + len(st.out_shapes)]))
            o += len(st.out_shapes)
        return main, comm

    return run


LONG_K = (2048, 1024)


def _tile(n, prefs):
    for t in prefs:
        if t <= n and n % t == 0:
            return t
    return n


_DIMS = {"nn": (((1,), (0,)), ((), ())), "nt": (((1,), (1,)), ((), ())), "tn": (((0,), (0,)), ((), ()))}


def _dot(a, b, mode="nn", precision=None):
    return lax.dot_general(a, b, _DIMS[mode], precision=precision, preferred_element_type=F32)


def _sigmoid(x):
    return 0.5 * jnp.tanh(0.5 * x) + 0.5


EPILOGUE_STRIP = 256


def _strips(n, width=EPILOGUE_STRIP):
    width = width if n % width == 0 else n
    return [slice(j, j + width) for j in range(0, n, width)]


def _mm(name, mode, a, b, mnk, *, tm, tn, tk, out_dtype=F32, precision=None, a_spec=None, b_spec=None,
        out_shape=None, o_spec=None, add=None, a_fn=None, thin=None, stages=None):
    m, n, k = mnk
    assert m % tm == 0 and n % tn == 0 and k % tk == 0, (name, mnk, tm, tn, tk)
    nk = k // tk
    if a_spec is None:
        a_spec = {"nn": pl.BlockSpec((tm, tk), lambda i, j, kk: (i, kk)),
                  "nt": pl.BlockSpec((tm, tk), lambda i, j, kk: (i, kk)),
                  "tn": pl.BlockSpec((tk, tm), lambda i, j, kk: (kk, i))}[mode]
    if b_spec is None:
        b_spec = {"nn": pl.BlockSpec((tk, tn), lambda i, j, kk: (kk, j)),
                  "nt": pl.BlockSpec((tn, tk), lambda i, j, kk: (j, kk)),
                  "tn": pl.BlockSpec((tk, tn), lambda i, j, kk: (kk, j))}[mode]
    if o_spec is None:
        o_spec = pl.BlockSpec((tm, tn), lambda i, j, kk: (i, j))
    if out_shape is None:
        out_shape = (m, n)
    has_add = add is not None
    n_in = 2 + has_add + (2 if thin else 0)

    def body(*refs):
        a_ref, b_ref = refs[0], refs[1]
        add_ref = refs[2] if has_add else None
        o_ref = refs[n_in]
        av = a_ref[...]
        if a_fn is not None:
            av = a_fn(av)
        part = _dot(av, b_ref[...], mode, precision)

        def finish(total):
            if has_add:
                total = total + add_ref[...]
            if thin:
                total = total + _dot(refs[n_in - 2][...], refs[n_in - 1][...])
            o_ref[...] = total.astype(out_dtype)

        if nk == 1:
            finish(part)
        else:
            acc_ref = refs[-1]
            kk = pl.program_id(2)

            @pl.when(kk == 0)
            def _():
                acc_ref[...] = part

            @pl.when(kk > 0)
            def _():
                acc_ref[...] += part

            @pl.when(kk == nk - 1)
            def _():
                finish(acc_ref[...])

    in_specs = [a_spec, b_spec]
    operands = [a, b]
    if has_add:
        in_specs.append(pl.BlockSpec((tm, tn), lambda i, j, kk: (i, j)))
        operands.append(add)
    if thin:
        k2 = thin[0].shape[1]
        in_specs += [pl.BlockSpec((tm, k2), lambda i, j, kk: (i, 0)), pl.BlockSpec((k2, tn), lambda i, j, kk: (0, j))]
        operands += list(thin)
    return _pcall(
        body, stages, name=name, out_shape=jax.ShapeDtypeStruct(out_shape, out_dtype), grid=(m // tm, n // tn, nk),
        in_specs=in_specs, out_specs=o_spec,
        scratch_shapes=[pltpu.VMEM((tm, tn), F32)] if nk > 1 else [],
        compiler_params=_cp("arbitrary", "arbitrary", "arbitrary") if stages else _cp("parallel", "parallel", "arbitrary"),
    )(*operands)


def _row_spec(tr, d):
    return pl.BlockSpec((tr, d), lambda i: (i, 0))


def _vec_spec(d, rows=1):
    return pl.BlockSpec((rows, d), lambda i: (0, 0))


def _modulate(name, x, sh, sc):
    s, d = x.shape
    tr = _tile(s, (512, 256))

    def body(x_ref, sh_ref, sc_ref, o_ref):
        o_ref[...] = (x_ref[...] * (1.0 + sc_ref[...]) + sh_ref[...]).astype(BF16)

    return pl.pallas_call(
        body, name=name, out_shape=jax.ShapeDtypeStruct((s, d), BF16), grid=(s // tr,),
        in_specs=[_row_spec(tr, d), _vec_spec(d), _vec_spec(d)], out_specs=_row_spec(tr, d),
        compiler_params=_cp("parallel"),
    )(x, sh, sc)


def _ln_stats(r):
    mu = jnp.mean(r, axis=-1, keepdims=True)
    xc = r - mu
    var = jnp.mean(xc * xc, axis=-1, keepdims=True)
    rstd = lax.rsqrt(var + LN_EPS)
    return xc * rstd, rstd


def _resid_ln_fwd(name, x, f, gate, ln_g, ln_b, sh_n, sc_n, coef):
    s, d = x.shape
    tr = _tile(s, (256,))

    def body(x_ref, f_ref, gate_ref, g_ref, b_ref, sh_ref, sc_ref, h_ref, u_ref):
        r = ALPHA * x_ref[...] + (coef * gate_ref[...]) * f_ref[...]
        xhat, _ = _ln_stats(r)
        h = xhat * g_ref[...] + b_ref[...]
        h_ref[...] = h
        u_ref[...] = (h * (1.0 + sc_ref[...]) + sh_ref[...]).astype(BF16)

    return pl.pallas_call(
        body, name=name, out_shape=(jax.ShapeDtypeStruct((s, d), F32), jax.ShapeDtypeStruct((s, d), BF16)),
        grid=(s // tr,), in_specs=[_row_spec(tr, d), _row_spec(tr, d)] + [_vec_spec(d)] * 5,
        out_specs=(_row_spec(tr, d), _row_spec(tr, d)), compiler_params=_cp("parallel"),
    )(x, f, gate, ln_g, ln_b, sh_n, sc_n)


ROW_DSC, ROW_DSH, ROW_DLN_G, ROW_DLN_B, ROW_DGATE, ROW_LOSS = 0, 1, 2, 3, 4, 5


def _ln_bwd_core(dy, xhat, rstd, ln_g):
    dxhat = dy * ln_g
    m1 = jnp.mean(dxhat, axis=-1, keepdims=True)
    m2 = jnp.mean(dxhat * xhat, axis=-1, keepdims=True)
    return rstd * (dxhat - m1 - xhat * m2)


def _colsum(v):
    return jnp.sum(v, axis=0, keepdims=True)


def _final_ln_loss_bwd(name, x, f, target, gate, ln_g, ln_b, coef):
    s, d = x.shape
    tr = _tile(s, (256,))
    inv_d = 1.0 / d

    def body(x_ref, f_ref, t_ref, gate_ref, g_ref, b_ref, dr_ref, df_ref, acc_ref):
        @pl.when(pl.program_id(0) == 0)
        def _():
            acc_ref[...] = jnp.zeros_like(acc_ref)

        fv = f_ref[...]
        r = ALPHA * x_ref[...] + (coef * gate_ref[...]) * fv
        xhat, rstd = _ln_stats(r)
        h = xhat * g_ref[...] + b_ref[...]
        err = h - t_ref[...]
        dy = err * inv_d
        dr = _ln_bwd_core(dy, xhat, rstd, g_ref[...])
        dr_ref[...] = dr
        df_ref[...] = ((coef * gate_ref[...]) * dr).astype(BF16)
        acc_ref[ROW_DLN_G:ROW_DLN_G + 1, :] += _colsum(dy * xhat)
        acc_ref[ROW_DLN_B:ROW_DLN_B + 1, :] += _colsum(dy)
        acc_ref[ROW_DGATE:ROW_DGATE + 1, :] += _colsum((coef * dr) * fv)
        acc_ref[ROW_LOSS:ROW_LOSS + 1, :] += _colsum(err * err) * (0.5 * inv_d)

    return pl.pallas_call(
        body, name=name,
        out_shape=(jax.ShapeDtypeStruct((s, d), F32), jax.ShapeDtypeStruct((s, d), BF16),
                   jax.ShapeDtypeStruct((SUBLANES, d), F32)),
        grid=(s // tr,), in_specs=[_row_spec(tr, d)] * 3 + [_vec_spec(d)] * 3,
        out_specs=(_row_spec(tr, d), _row_spec(tr, d), _vec_spec(d, SUBLANES)),
        compiler_params=_cp("arbitrary"),
    )(x, f, target, gate, ln_g, ln_b)


def _resid_ln_bwd(name, du_n, dr_n, x, f, sc_n, gate, ln_g, ln_b, coef):
    s, d = x.shape
    tr = _tile(s, (256,))

    def body(du_ref, drn_ref, x_ref, f_ref, sc_ref, gate_ref, g_ref, b_ref, dr_ref, df_ref, acc_ref):
        @pl.when(pl.program_id(0) == 0)
        def _():
            acc_ref[...] = jnp.zeros_like(acc_ref)

        fv = f_ref[...]
        du = du_ref[...]
        r = ALPHA * x_ref[...] + (coef * gate_ref[...]) * fv
        xhat, rstd = _ln_stats(r)
        h = xhat * g_ref[...] + b_ref[...]
        dy = du * (1.0 + sc_ref[...]) + ALPHA * drn_ref[...]
        dr = _ln_bwd_core(dy, xhat, rstd, g_ref[...])
        dr_ref[...] = dr
        df_ref[...] = ((coef * gate_ref[...]) * dr).astype(BF16)
        acc_ref[ROW_DSC:ROW_DSC + 1, :] += _colsum(du * h)
        acc_ref[ROW_DSH:ROW_DSH + 1, :] += _colsum(du)
        acc_ref[ROW_DLN_G:ROW_DLN_G + 1, :] += _colsum(dy * xhat)
        acc_ref[ROW_DLN_B:ROW_DLN_B + 1, :] += _colsum(dy)
        acc_ref[ROW_DGATE:ROW_DGATE + 1, :] += _colsum((coef * dr) * fv)

    return pl.pallas_call(
        body, name=name,
        out_shape=(jax.ShapeDtypeStruct((s, d), F32), jax.ShapeDtypeStruct((s, d), BF16),
                   jax.ShapeDtypeStruct((SUBLANES, d), F32)),
        grid=(s // tr,), in_specs=[_row_spec(tr, d)] * 4 + [_vec_spec(d)] * 4,
        out_specs=(_row_spec(tr, d), _row_spec(tr, d), _vec_spec(d, SUBLANES)),
        compiler_params=_cp("arbitrary"),
    )(du_n, dr_n, x, f, sc_n, gate, ln_g, ln_b)


def _input_grad(name, du, dr, x, sc):
    s, d = x.shape
    tr = _tile(s, (256,))

    def body(du_ref, dr_ref, x_ref, sc_ref, gx_ref, acc_ref):
        @pl.when(pl.program_id(0) == 0)
        def _():
            acc_ref[...] = jnp.zeros_like(acc_ref)

        du = du_ref[...]
        gx_ref[...] = du * (1.0 + sc_ref[...]) + ALPHA * dr_ref[...]
        acc_ref[ROW_DSC:ROW_DSC + 1, :] += _colsum(du * x_ref[...])
        acc_ref[ROW_DSH:ROW_DSH + 1, :] += _colsum(du)

    return pl.pallas_call(
        body, name=name,
        out_shape=(jax.ShapeDtypeStruct((s, d), F32), jax.ShapeDtypeStruct((SUBLANES, d), F32)),
        grid=(s // tr,), in_specs=[_row_spec(tr, d)] * 3 + [_vec_spec(d)],
        out_specs=(_row_spec(tr, d), _vec_spec(d, SUBLANES)), compiler_params=_cp("arbitrary"),
    )(du, dr, x, sc)


def _ffn_in_fwd(name, u, w_in, stages=None):
    s, d = u.shape
    cs = w_in.shape[2]
    f = 2 * cs
    tm, tn = _tile(s, (2048, 1024, 512)), _tile(cs, (256, 128))
    nb = f // tn
    nbs = cs // tn

    def body(u_ref, wa_ref, wb_ref, ab_ref, act_ref):
        for rows in _strips(tm, 512):
            uv = u_ref[rows, :]
            a = _dot(uv, wa_ref[...])
            b = _dot(uv, wb_ref[...])
            sg = _sigmoid(a)
            silu = a * sg
            ab_ref[0, rows, :] = (b * (sg + silu * (1.0 - sg))).astype(BF16)
            ab_ref[1, rows, :] = silu.astype(BF16)
            act_ref[rows, :] = (silu * b).astype(BF16)

    return _pcall(
        body, stages, name=name,
        out_shape=(jax.ShapeDtypeStruct((2, s, f), BF16), jax.ShapeDtypeStruct((s, f), BF16)),
        grid=(s // tm, nb),
        in_specs=[pl.BlockSpec((tm, d), lambda i, j: (i, 0)),
                  pl.BlockSpec((None, d, tn), lambda i, j: (j // nbs, 0, j % nbs)),
                  pl.BlockSpec((None, d, tn), lambda i, j: (2 + j // nbs, 0, j % nbs))],
        out_specs=(pl.BlockSpec((2, tm, tn), lambda i, j: (0, i, j)), pl.BlockSpec((tm, tn), lambda i, j: (i, j))),
        compiler_params=_cp("arbitrary", "arbitrary"),
    )(u, w_in, w_in)


def _ffn_out_bwd(name, df, w_out, ab, stages=None):
    s, d = df.shape
    f = w_out.shape[0]
    tm, tn = _tile(s, (1024, 512)), _tile(f, (512, 256, 128))

    def body(df_ref, w_ref, ab_ref, dab_ref):
        dfv = df_ref[...]
        for cols in _strips(tn):
            dact = _dot(dfv, w_ref[cols, :], "nt")
            dab_ref[0, :, cols] = (dact * ab_ref[0, :, cols].astype(F32)).astype(BF16)
            dab_ref[1, :, cols] = (dact * ab_ref[1, :, cols].astype(F32)).astype(BF16)

    return _pcall(
        body, stages, name=name, out_shape=jax.ShapeDtypeStruct((2, s, f), BF16), grid=(s // tm, f // tn),
        in_specs=[pl.BlockSpec((tm, d), lambda i, j: (i, 0)), pl.BlockSpec((tn, d), lambda i, j: (j, 0)),
                  pl.BlockSpec((2, tm, tn), lambda i, j: (0, i, j))],
        out_specs=pl.BlockSpec((2, tm, tn), lambda i, j: (0, i, j)),
        compiler_params=_cp("arbitrary", "arbitrary"),
    )(df, w_out, ab)


def _ffn_forward(tag, u, plan):
    w_in, w_out = plan.weight(f"{tag}_w_in"), plan.weight(f"{tag}_w_out")
    s, d = u.shape
    f = w_out.shape[0]
    ab, act = plan.host(f"{tag}_in_fwd", lambda st: _ffn_in_fwd(f"{tag}_in_fwd", u, w_in, st))
    out = plan.host(f"{tag}_out_fwd", lambda st: _mm(
        f"{tag}_out_fwd", "nn", act, w_out, (s, d, f), tm=_tile(s, (1024,)), tn=_tile(d, (1024,)),
        tk=_tile(f, (2816, 1408, 512, 128)), stages=st))
    return out, (ab, act)


def _ffn_backward(tag, df, u, saved, plan, in_first):
    w_in, w_out = plan.weight(f"{tag}_w_in"), plan.weight(f"{tag}_w_out")
    ab, act = saved
    s, d = u.shape
    f = w_out.shape[0]
    dab = plan.host(f"{tag}_out_bwd", lambda st: _ffn_out_bwd(f"{tag}_out_bwd", df, w_out, ab, st))
    cs = w_in.shape[2]
    tk = _tile(cs, (2816, 1408, 256, 128))
    nkh, nks = f // tk, cs // tk
    tmd = _tile(d, (1024,))
    tks = _tile(s, LONG_K)

    def dw_in():
        tw = _tile(cs, (256, 128))
        nwh, nws = f // tw, cs // tw
        plan.grad(f"{tag}_w_in", plan.host(f"{tag}_dw_in", lambda st: _mm(
            f"{tag}_dw_in", "nn", u.T, dab, (d, 2 * f, s), tm=tmd, tn=tw, tk=s,
            b_spec=pl.BlockSpec((None, s, tw), lambda i, j, kk: (j // nwh, 0, j % nwh)), out_shape=(N_CHIPS, d, cs),
            o_spec=pl.BlockSpec((None, tmd, tw), lambda i, j, kk: (j // nws, i, j % nws)), stages=st)))

    def dw_out():
        plan.grad(f"{tag}_w_out", plan.host(f"{tag}_dw_out", lambda st: _mm(
            f"{tag}_dw_out", "tn", act, df, (f, d, s), tm=_tile(f, (1408, 512, 128)), tn=tmd, tk=tks, stages=st)))

    for step in ((dw_in, dw_out) if in_first else (dw_out, dw_in)):
        step()
    return plan.host(f"{tag}_du", lambda st: _mm(
        f"{tag}_du", "nt", dab, w_in, (s, d, 2 * f), tm=_tile(s, (1024,)), tn=tmd, tk=tk,
        a_spec=pl.BlockSpec((None, _tile(s, (1024,)), tk), lambda i, j, kk: (kk // nkh, i, kk % nkh)),
        b_spec=pl.BlockSpec((None, tmd, tk), lambda i, j, kk: (kk // nks, j, kk % nks)), stages=st))


ATTN_Q = 4 * CHUNK
ATTN_W = ATTN_Q + A_PAD


def _band_bias(bias):
    n = ATTN_Q // CHUNK
    rows = [jnp.pad(bias, ((0, 0), (0, 0), (i * CHUNK, (n - 1 - i) * CHUNK)), constant_values=NEG_BIG)
            for i in range(n)]
    return jnp.concatenate(rows, axis=1)


def _band_bias_grad(dband):
    n = ATTN_Q // CHUNK
    parts = [dband[:, i * CHUNK:(i + 1) * CHUNK, i * CHUNK:i * CHUNK + A_BAND] for i in range(n)]
    return functools.reduce(jnp.add, parts)


def _attn_probs(q, kw, bias, key0):
    sc = _dot(q, kw, "nt") * (A_HEAD_DIM ** -0.5) + bias
    ks = lax.broadcasted_iota(jnp.int32, sc.shape, 1)
    sc = jnp.where(key0 + ks >= 0, sc, NEG_BIG)
    p = jnp.exp(sc - jnp.max(sc, axis=-1, keepdims=True))
    return p / jnp.sum(p, axis=-1, keepdims=True)


def _head_masks():
    lane = lax.broadcasted_iota(jnp.int32, (1, LANES), 1)
    return [lane // A_HEAD_DIM == h for h in range(LANES // A_HEAD_DIM)]


def _attn_fwd(p1, kvp, band, stages=None):
    s = p1.shape[0]
    aw = A_HEADS * A_HEAD_DIM
    nblk = aw // LANES
    hpb = LANES // A_HEAD_DIM
    assert s % ATTN_Q == 0

    def body(q_ref, k_ref, v_ref, b_ref, o_ref):
        base = pl.multiple_of(pl.program_id(1) * ATTN_Q, ATTN_Q)
        qv = q_ref[...]
        kw = k_ref[pl.ds(base, ATTN_W), :]
        vw = v_ref[pl.ds(base, ATTN_W), :]
        out = jnp.zeros((ATTN_Q, LANES), F32)
        for h, mask in enumerate(_head_masks()):
            p = _attn_probs(jnp.where(mask, qv, jnp.zeros_like(qv)), kw, b_ref[h], base - A_PAD)
            out = jnp.where(mask, _dot(p.astype(BF16), vw), out)
        o_ref[...] = out.astype(BF16)

    kv_rows = s + A_PAD
    return _pcall(
        body, stages, name="attn_fwd", out_shape=jax.ShapeDtypeStruct((s, aw), BF16), grid=(nblk, s // ATTN_Q),
        in_specs=[pl.BlockSpec((ATTN_Q, LANES), lambda b, i: (i, b)),
                  pl.BlockSpec((kv_rows, LANES), lambda b, i: (0, b)),
                  pl.BlockSpec((kv_rows, LANES), lambda b, i: (0, nblk + b)),
                  pl.BlockSpec((hpb, ATTN_Q, ATTN_W), lambda b, i: (b, 0, 0))],
        out_specs=pl.BlockSpec((ATTN_Q, LANES), lambda b, i: (i, b)),
        compiler_params=_cp("arbitrary", "arbitrary"),
    )(p1, kvp, kvp, band)


def _attn_bwd(p1, kvp, band, dya, stages=None):
    s = p1.shape[0]
    aw = A_HEADS * A_HEAD_DIM
    nblk = aw // LANES
    hpb = LANES // A_HEAD_DIM
    scale = A_HEAD_DIM ** -0.5

    def body(q_ref, k_ref, v_ref, b_ref, do_ref, dq_ref, dk_ref, dv_ref, db_ref):
        @pl.when(pl.program_id(1) == 0)
        def _():
            dk_ref[...] = jnp.zeros_like(dk_ref)
            dv_ref[...] = jnp.zeros_like(dv_ref)
            db_ref[...] = jnp.zeros_like(db_ref)

        base = pl.multiple_of(pl.program_id(1) * ATTN_Q, ATTN_Q)
        window = pl.ds(base, ATTN_W)
        kw = k_ref[window, :]
        vw = v_ref[window, :]
        qv = q_ref[...]
        dov = do_ref[...]
        dq = jnp.zeros((ATTN_Q, LANES), F32)
        dk = jnp.zeros((ATTN_W, LANES), F32)
        dv = jnp.zeros((ATTN_W, LANES), F32)
        for h, mask in enumerate(_head_masks()):
            qh = jnp.where(mask, qv, jnp.zeros_like(qv))
            doh = jnp.where(mask, dov, jnp.zeros_like(dov))
            p = _attn_probs(qh, kw, b_ref[h], base - A_PAD)
            dp = _dot(doh, vw, "nt")
            ds = p * (dp - jnp.sum(p * dp, axis=-1, keepdims=True))
            db_ref[h] += ds
            dsb = (ds * scale).astype(BF16)
            dq = jnp.where(mask, _dot(dsb, kw), dq)
            dk = dk + _dot(dsb, qh, "tn")
            dv = dv + _dot(p.astype(BF16), doh, "tn")
        dq_ref[...] = dq.astype(BF16)
        dk_ref[window, :] += dk
        dv_ref[window, :] += dv

    kv_rows = s + A_PAD
    q_spec = pl.BlockSpec((ATTN_Q, LANES), lambda b, i: (i, b))
    acc_spec = pl.BlockSpec((kv_rows, LANES), lambda b, i: (0, b))
    b_spec = pl.BlockSpec((hpb, ATTN_Q, ATTN_W), lambda b, i: (b, 0, 0))
    return _pcall(
        body, stages, name="attn_bwd",
        out_shape=(jax.ShapeDtypeStruct((s, aw), BF16), jax.ShapeDtypeStruct((kv_rows, aw), F32),
                   jax.ShapeDtypeStruct((kv_rows, aw), F32), jax.ShapeDtypeStruct((A_HEADS, ATTN_Q, ATTN_W), F32)),
        grid=(nblk, s // ATTN_Q),
        in_specs=[q_spec, acc_spec, pl.BlockSpec((kv_rows, LANES), lambda b, i: (0, nblk + b)), b_spec, q_spec],
        out_specs=(q_spec, acc_spec, acc_spec, b_spec), compiler_params=_cp("arbitrary", "arbitrary"),
    )(p1, kvp, kvp, band, dya)


def _rel_onehot():
    qi = jnp.arange(CHUNK)[:, None]
    ks = jnp.arange(A_BAND)[None, :]
    idx = (jnp.clip(ks - A_PAD - qi, -REL_CLIP, CHUNK - 1) + REL_CLIP).reshape(1, CHUNK * A_BAND)
    return (jnp.arange(REL_SIZE)[:, None] == idx).astype(F32)


def _gla_gate(lr, wa2, balpha):
    z = _dot(lr, wa2) + balpha
    la = (jnp.minimum(z, 0.0) - jnp.log(1.0 + jnp.exp(-jnp.abs(z)))) * (1.0 / GATE_TAU)
    row = lax.broadcasted_iota(jnp.int32, (CHUNK, CHUNK), 0)
    col = lax.broadcasted_iota(jnp.int32, (CHUNK, CHUNK), 1)
    cum = _dot((row >= col).astype(F32), la, precision=HIGHEST)
    return z, la, cum


def _gla_dims(p2):
    kd = p2.shape[1] // 6
    hk = kd // B_HEADS
    hv = 2 * hk
    return kd, hk, hv


GLA_CPS = 2


def _gla_fwd(p2, lrp, wa2p, balpha, gnorm, stages=None):
    s = p2.shape[0]
    kd, hk, hv = _gla_dims(p2)
    nc = s // CHUNK
    cps = GLA_CPS if nc % GLA_CPS == 0 else 1
    rows_per = cps * CHUNK
    qscale = hk ** -0.5

    def body(p_ref, lr_ref, wa_ref, ba_ref, gn_ref, yb_ref, st_ref, state):
        @pl.when(pl.program_id(0) == 0)
        def _():
            state[...] = jnp.zeros_like(state)

        gn = gn_ref[...]
        for sub in range(cps):
            rows = slice(sub * CHUNK, (sub + 1) * CHUNK)
            _, _, cum = _gla_gate(lr_ref[rows, :], wa_ref[...], ba_ref[...])
            last = cum[CHUNK - 1:CHUNK, :]
            e = jnp.exp(last - cum)
            dch = jnp.exp(last)
            for hh in range(B_HEADS):
                ks = slice(hh * hk, (hh + 1) * hk)
                q = p_ref[rows, hh * hk:(hh + 1) * hk].astype(F32)
                k = p_ref[rows, kd + hh * hk:kd + (hh + 1) * hk].astype(F32)
                v = p_ref[rows, 2 * kd + hh * hv:2 * kd + (hh + 1) * hv]
                rg = p_ref[rows, 4 * kd + hh * hv:4 * kd + (hh + 1) * hv].astype(F32)
                kdec = (k * e[:, ks]).astype(BF16)
                st = state[hh] * dch[:, ks] + _dot(v, kdec, "tn")
                state[hh] = st
                st_ref[sub, hh] = st
                o = _dot((q * qscale).astype(BF16), st.astype(BF16), "nt")
                rinv = lax.rsqrt(jnp.mean(o * o, axis=-1, keepdims=True) + RMS_EPS)
                yb_ref[rows, hh * hv:(hh + 1) * hv] = ((o * rinv * gn) * (rg * _sigmoid(rg))).astype(BF16)

    return _pcall(
        body, stages, name="gla_fwd",
        out_shape=(jax.ShapeDtypeStruct((s, 2 * kd), BF16), jax.ShapeDtypeStruct((nc, B_HEADS, hv, hk), F32)),
        grid=(nc // cps,),
        in_specs=[pl.BlockSpec((rows_per, 6 * kd), lambda i: (i, 0)), pl.BlockSpec((rows_per, LANES), lambda i: (i, 0)),
                  pl.BlockSpec((LANES, kd), lambda i: (0, 0)), pl.BlockSpec((1, kd), lambda i: (0, 0)),
                  pl.BlockSpec((1, hv), lambda i: (0, 0))],
        out_specs=(pl.BlockSpec((rows_per, 2 * kd), lambda i: (i, 0)),
                   pl.BlockSpec((cps, B_HEADS, hv, hk), lambda i: (i, 0, 0, 0))),
        scratch_shapes=[pltpu.VMEM((B_HEADS, hv, hk), F32)], compiler_params=_cp("arbitrary"),
    )(p2, lrp, wa2p, balpha, gnorm)


GLA_ROW_DBALPHA, GLA_ROW_DGNORM = 0, 1


def _gla_bwd(p2, lrp, wa2p, balpha, gnorm, states, dyb, stages=None):
    s = p2.shape[0]
    kd, hk, hv = _gla_dims(p2)
    nc = s // CHUNK
    cps = GLA_CPS if nc % GLA_CPS == 0 else 1
    rows_per = cps * CHUNK
    nblk = nc // cps
    qscale = hk ** -0.5

    def body(p_ref, lr_ref, wa_ref, ba_ref, gn_ref, st_ref, sp_ref, dy_ref, dp_ref, dz_ref, sm_ref, gcar):
        i = pl.program_id(0)

        @pl.when(i == 0)
        def _():
            gcar[...] = jnp.zeros_like(gcar)
            sm_ref[...] = jnp.zeros_like(sm_ref)

        block_has_prev = (i < nblk - 1).astype(F32)
        gn = gn_ref[...]
        row = lax.broadcasted_iota(jnp.int32, (CHUNK, CHUNK), 0)
        col = lax.broadcasted_iota(jnp.int32, (CHUNK, CHUNK), 1)
        tri_strict = (row > col).astype(F32)
        for sub in reversed(range(cps)):
            rows = slice(sub * CHUNK, (sub + 1) * CHUNK)
            z, _, cum = _gla_gate(lr_ref[rows, :], wa_ref[...], ba_ref[...])
            last = cum[CHUNK - 1:CHUNK, :]
            e = jnp.exp(last - cum)
            dch = jnp.exp(last)
            sgn = _sigmoid(-z) * (1.0 / GATE_TAU)
            for hh in range(B_HEADS):
                ks = slice(hh * hk, (hh + 1) * hk)
                q = p_ref[rows, hh * hk:(hh + 1) * hk].astype(F32)
                k = p_ref[rows, kd + hh * hk:kd + (hh + 1) * hk].astype(F32)
                v = p_ref[rows, 2 * kd + hh * hv:2 * kd + (hh + 1) * hv]
                rg = p_ref[rows, 4 * kd + hh * hv:4 * kd + (hh + 1) * hv].astype(F32)
                kdecf = k * e[:, ks]
                kdec = kdecf.astype(BF16)
                st16 = st_ref[sub, hh].astype(BF16)
                prev = st_ref[sub - 1, hh] if sub > 0 else sp_ref[hh] * block_has_prev
                qs = (q * qscale).astype(BF16)
                o = _dot(qs, st16, "nt")
                rinv = lax.rsqrt(jnp.mean(o * o, axis=-1, keepdims=True) + RMS_EPS)
                dy = dy_ref[rows, hh * hv:(hh + 1) * hv].astype(F32)
                sg = _sigmoid(rg)
                onorm = o * rinv
                drg = dy * (onorm * gn) * (sg * (1.0 + rg * (1.0 - sg)))
                dob = dy * (rg * sg)
                sm_ref[GLA_ROW_DGNORM:GLA_ROW_DGNORM + 1, 0:hv] += _colsum(dob * onorm)
                t = dob * gn
                do = rinv * (t - onorm * jnp.mean(t * onorm, axis=-1, keepdims=True))
                do16 = do.astype(BF16)
                dq = _dot(do16, st16) * qscale
                gt = _dot(do16, qs, "tn") + gcar[hh]
                gcar[hh] = gt * dch[:, ks]
                dd = _colsum(gt * prev)
                gt16 = gt.astype(BF16)
                dkdec = _dot(v, gt16)
                dv = _dot(kdec, gt16, "nt")
                dla = dd * dch[:, ks] + _dot(tri_strict, dkdec * kdecf, precision=HIGHEST)
                dzh = dla * sgn[:, ks]
                sm_ref[GLA_ROW_DBALPHA:GLA_ROW_DBALPHA + 1, hh * hk:(hh + 1) * hk] += _colsum(dzh)
                dz_ref[rows, hh * hk:(hh + 1) * hk] = dzh.astype(BF16)
                dp_ref[rows, hh * hk:(hh + 1) * hk] = dq.astype(BF16)
                dp_ref[rows, kd + hh * hk:kd + (hh + 1) * hk] = (dkdec * e[:, ks]).astype(BF16)
                dp_ref[rows, 2 * kd + hh * hv:2 * kd + (hh + 1) * hv] = dv.astype(BF16)
                dp_ref[rows, 4 * kd + hh * hv:4 * kd + (hh + 1) * hv] = drg.astype(BF16)

    rev = lambda i: (nblk - 1 - i, 0)
    return _pcall(
        body, stages, name="gla_bwd",
        out_shape=(jax.ShapeDtypeStruct((s, 6 * kd), BF16), jax.ShapeDtypeStruct((s, kd), BF16),
                   jax.ShapeDtypeStruct((SUBLANES, kd), F32)),
        grid=(nblk,),
        in_specs=[pl.BlockSpec((rows_per, 6 * kd), rev), pl.BlockSpec((rows_per, LANES), rev),
                  pl.BlockSpec((LANES, kd), lambda i: (0, 0)), pl.BlockSpec((1, kd), lambda i: (0, 0)),
                  pl.BlockSpec((1, hv), lambda i: (0, 0)),
                  pl.BlockSpec((cps, B_HEADS, hv, hk), lambda i: (nblk - 1 - i, 0, 0, 0)),
                  pl.BlockSpec((None, B_HEADS, hv, hk), lambda i: (jnp.maximum((nblk - 1 - i) * cps - 1, 0), 0, 0, 0)),
                  pl.BlockSpec((rows_per, 2 * kd), rev)],
        out_specs=(pl.BlockSpec((rows_per, 6 * kd), rev), pl.BlockSpec((rows_per, kd), rev),
                   pl.BlockSpec((SUBLANES, kd), lambda i: (0, 0))),
        scratch_shapes=[pltpu.VMEM((B_HEADS, hv, hk), F32)], compiler_params=_cp("arbitrary"),
    )(p2, lrp, wa2p, balpha, gnorm, states, states, dyb)


def _merge_fwd(ya, yb, wpa, wpb, g):
    s, ka = ya.shape
    kb = yb.shape[1]
    d = wpa.shape[1]
    tm, tn = _tile(s, (1024, 512)), _tile(d, (512,))

    def body(ya_ref, yb_ref, wa_ref, wb_ref, g_ref, m_ref, pab_ref):
        yav, ybv = ya_ref[...], yb_ref[...]
        for cols in _strips(tn):
            pa = _dot(yav, wa_ref[:, cols])
            pb = _dot(ybv, wb_ref[:, cols])
            m_ref[:, cols] = (_sigmoid(g_ref[0, :, cols].astype(F32)) * pa
                              + _sigmoid(g_ref[1, :, cols].astype(F32)) * pb).astype(BF16)
            pab_ref[0, :, cols] = pa.astype(BF16)
            pab_ref[1, :, cols] = pb.astype(BF16)

    st = pl.BlockSpec((2, tm, tn), lambda i, j: (0, i, j))
    return pl.pallas_call(
        body, name="merge_fwd",
        out_shape=(jax.ShapeDtypeStruct((s, d), BF16), jax.ShapeDtypeStruct((2, s, d), BF16)),
        grid=(s // tm, d // tn),
        in_specs=[pl.BlockSpec((tm, ka), lambda i, j: (i, 0)), pl.BlockSpec((tm, kb), lambda i, j: (i, 0)),
                  pl.BlockSpec((ka, tn), lambda i, j: (0, j)), pl.BlockSpec((kb, tn), lambda i, j: (0, j)), st],
        out_specs=(pl.BlockSpec((tm, tn), lambda i, j: (i, j)), st),
        compiler_params=_cp("parallel", "parallel"),
    )(ya, yb, wpa, wpb, g)


def _merge_bwd(dm, wmo, g, pab, stages=None):
    s, d = dm.shape
    tm, tn = _tile(s, (1024, 512)), _tile(d, (512,))

    def body(dm_ref, w_ref, g_ref, pab_ref, dpab_ref, dg_ref):
        dmv = dm_ref[...]
        for cols in _strips(tn):
            dmg = _dot(dmv, w_ref[cols, :], "nt")
            for j in range(2):
                sg = _sigmoid(g_ref[j, :, cols].astype(F32))
                dpab_ref[j, :, cols] = (dmg * sg).astype(BF16)
                dg_ref[j, :, cols] = (dmg * pab_ref[j, :, cols].astype(F32) * (sg * (1.0 - sg))).astype(BF16)

    st = pl.BlockSpec((2, tm, tn), lambda i, j: (0, i, j))
    return _pcall(
        body, stages, name="merge_bwd",
        out_shape=(jax.ShapeDtypeStruct((2, s, d), BF16), jax.ShapeDtypeStruct((2, s, d), BF16)),
        grid=(s // tm, d // tn),
        in_specs=[pl.BlockSpec((tm, d), lambda i, j: (i, 0)), pl.BlockSpec((tn, d), lambda i, j: (j, 0)), st, st],
        out_specs=(st, st), compiler_params=_cp("arbitrary", "arbitrary"),
    )(dm, wmo, g, pab)


def _virtual_rows(parts, lo, hi):
    out, off = [], 0
    for p in parts:
        a, b = max(lo, off), min(hi, off + p.shape[0])
        if a < b:
            out.append(p[a - off:b - off])
        off += p.shape[0]
    return out[0] if len(out) == 1 else jnp.concatenate(out, axis=0)


def _mix_in_row_groups(d):
    o1 = 3 * A_HEADS * A_HEAD_DIM
    o2 = o1 + 6 * (d // 4)
    o3 = o2 + GATE_RANK
    return (0, o1), (o1, o2), (o2, o3), (o3, o3 + 2 * d)


def _split_mix_in(stacked):
    d = stacked.shape[2]
    flat = stacked.reshape(-1, d)
    _, _, (lo, hi), (glo, ghi) = _mix_in_row_groups(d)
    return flat, jnp.pad(flat[lo:hi], ((0, LANES - GATE_RANK), (0, 0))), flat[glo:ghi]


MIX_TILE = 1024


def _mix_in_weights(plan):
    return plan.memo("mix_in_weights", lambda: _split_mix_in(plan.weight("w_mix_in")))


def _hosted_mm(plan):
    return lambda name, *a, **k: plan.host(name, lambda st: _mm(name, *a, stages=st, **k))


def _mix_forward(u2, plan, small):
    s, d = u2.shape
    wt, wt_lr, wt_g = _mix_in_weights(plan)
    bias, wa2p, balpha, gnorm = small
    mm = _hosted_mm(plan)
    aw = A_HEADS * A_HEAD_DIM
    tm, tn = _tile(s, (1024,)), MIX_TILE
    (_, na), (_, nab) = _mix_in_row_groups(d)[:2]
    assert na % tn == 0 and nab % tn == 0
    p1 = mm("mix_in_a", "nt", u2, wt, (s, na, d), tm=tm, tn=tn, tk=d, out_dtype=BF16)
    p2 = mm("mix_in_b", "nt", u2, wt, (s, nab - na, d), tm=tm, tn=tn, tk=d, out_dtype=BF16,
            b_spec=pl.BlockSpec((tn, d), lambda i, j, kk: (na // tn + j, 0)))
    lrp = mm("mix_in_lr", "nt", u2, wt_lr, (s, LANES, d), tm=tm, tn=LANES, tk=d, out_dtype=BF16)
    nbg = d // tn
    g = mm("mix_in_g", "nt", u2, wt_g, (s, 2 * d, d), tm=tm, tn=tn, tk=d, out_dtype=BF16, out_shape=(2, s, d),
           o_spec=pl.BlockSpec((None, tm, tn), lambda i, j, kk: (j // nbg, i, j % nbg)))
    kvp = jnp.pad(p1[:, aw:], ((A_PAD, 0), (0, 0)))
    ya = plan.host("attn_fwd", lambda st: _attn_fwd(p1, kvp, bias, st))
    yb, states = plan.host("gla_fwd", lambda st: _gla_fwd(p2, lrp, wa2p, balpha, gnorm, st))
    merged, pab = _merge_fwd(ya, yb, plan.weight("w_proj_a"), plan.weight("w_proj_b"), g)
    m = mm("mix_out", "nn", merged, plan.weight("w_mix_out"), (s, d, d), tm=tm, tn=tn, tk=d)
    return m, (p1, kvp, p2, lrp, states, ya, yb, g, pab, merged)


def _mix_backward(dm, u2, saved, plan, small):
    s, d = u2.shape
    wt, wt_lr, wt_g = _mix_in_weights(plan)
    wpa, wpb, wmo = plan.weight("w_proj_a"), plan.weight("w_proj_b"), plan.weight("w_mix_out")
    bias, wa2p, balpha, gnorm = small
    p1, kvp, p2, lrp, states, ya, yb, g, pab, merged = saved
    mm = _hosted_mm(plan)
    aw = A_HEADS * A_HEAD_DIM
    kd = d // 4
    t = MIX_TILE
    tm = _tile(s, (1024,))
    tks = _tile(s, LONG_K)

    plan.grad("w_mix_out", mm("mix_dw_out", "tn", merged, dm, (d, d, s), tm=t, tn=t, tk=tks))
    dpab, dg = plan.host("merge_bwd", lambda st: _merge_bwd(dm, wmo, g, pab, st))
    sel = lambda j: pl.BlockSpec((None, tm, d), lambda i, jj, kk: (j, i, 0))
    dya = mm("mix_dya", "nt", dpab, wpa, (s, aw, d), tm=tm, tn=t, tk=d, out_dtype=BF16, a_spec=sel(0))
    dyb = mm("mix_dyb", "nt", dpab, wpb, (s, 2 * kd, d), tm=tm, tn=t, tk=d, out_dtype=BF16, a_spec=sel(1))
    selk = lambda j: pl.BlockSpec((None, tks, t), lambda i, jj, kk: (j, kk, jj))
    plan.grad("w_proj_a", mm("mix_dwpa", "tn", ya, dpab, (aw, d, s), tm=t, tn=t, tk=tks, b_spec=selk(0)))
    plan.grad("w_proj_b", mm("mix_dwpb", "tn", yb, dpab, (2 * kd, d, s), tm=t, tn=t, tk=tks, b_spec=selk(1)))

    dq, dkp, dvp, dbias = plan.host("attn_bwd", lambda st: _attn_bwd(p1, kvp, bias, dya, st))
    dp1 = jnp.concatenate([dq, dkp[A_PAD:].astype(BF16), dvp[A_PAD:].astype(BF16)], axis=1)
    dp2, dz, gsm = plan.host("gla_bwd", lambda st: _gla_bwd(p2, lrp, wa2p, balpha, gnorm, states, dyb, st))
    dlrp = mm("gla_dlr", "nt", dz, wa2p, (s, LANES, kd), tm=tm, tn=LANES, tk=kd, out_dtype=BF16)
    dwa2p = mm("gla_dwa2", "tn", lrp, dz, (LANES, kd, s), tm=LANES, tn=kd, tk=tks)

    tka = 3 * aw
    assert 6 * kd == tka
    du = mm("mix_du_a", "nn", dp1, wt, (s, d, tka), tm=tm, tn=t, tk=tka)
    du = mm("mix_du_b", "nn", dp2, wt, (s, d, tka), tm=tm, tn=t, tk=tka, add=du,
            b_spec=pl.BlockSpec((tka, t), lambda i, j, kk: (1 + kk, j)))
    du = mm("mix_du_g", "nn", dg, wt_g, (s, d, 2 * d), tm=tm, tn=t, tk=d, add=du, thin=(dlrp, wt_lr),
            a_spec=pl.BlockSpec((None, tm, d), lambda i, j, kk: (kk, i, 0)))
    nkg = d // t
    dw1 = mm("mix_dw_a", "tn", dp1, u2, (3 * aw, d, s), tm=t, tn=t, tk=tks)
    dw2 = mm("mix_dw_b", "tn", dp2, u2, (6 * kd, d, s), tm=t, tn=t, tk=tks)
    dwlr = mm("mix_dw_lr", "tn", dlrp, u2, (LANES, d, s), tm=LANES, tn=t, tk=tks)
    dwg = mm("mix_dw_g", "tn", dg, u2, (2 * d, d, s), tm=t, tn=t, tk=tks,
             a_spec=pl.BlockSpec((None, tks, t), lambda i, j, kk: (i // nkg, kk, i % nkg)))
    pieces = [dw1, dw2, dwlr[:GATE_RANK], dwg]
    shard_rows = sum(p.shape[0] for p in pieces) // N_CHIPS
    plan.grad("w_mix_in", jnp.stack([_virtual_rows(pieces, j * shard_rows, (j + 1) * shard_rows)
                                     for j in range(N_CHIPS)]))
    return du, (dbias, dwa2p[:GATE_RANK], gsm)


def _device_step(x, target, mod, small, plan):
    s, d = x.shape
    row = lambda i: mod[i:i + 1]
    sh1, sc1, g1, sh2, sc2, g2, sh3, sc3, g3 = (row(i) for i in range(N_MOD))

    onehot = _rel_onehot()
    bias = _mm("rel_bias_expand", "nn", small["rel_bias"], onehot, (A_HEADS, CHUNK * A_BAND, REL_SIZE),
               tm=A_HEADS, tn=4608, tk=REL_SIZE, precision=HIGHEST).reshape(A_HEADS, CHUNK, A_BAND)
    bias = _band_bias(bias)
    wa2p = jnp.pad(small["w_alpha2"], ((0, LANES - GATE_RANK), (0, 0))).astype(BF16)
    mix_small = (bias, wa2p, small["b_alpha"], small["gla_norm_g"])

    u1 = _modulate("mod1", x, sh1, sc1)
    f1, sv1 = _ffn_forward("ffn1", u1, plan)
    h1, u2 = _resid_ln_fwd("ln1_fwd", x, f1, g1, small["ln1_g"], small["ln1_b"], sh2, sc2, 0.5)
    m, svm = _mix_forward(u2, plan, mix_small)
    h2, u3 = _resid_ln_fwd("ln2_fwd", h1, m, g2, small["ln2_g"], small["ln2_b"], sh3, sc3, 1.0)
    f2, sv2 = _ffn_forward("ffn2", u3, plan)

    dr3, df2, acc3 = _final_ln_loss_bwd("ln3_loss_bwd", h2, f2, target, g3, small["ln3_g"], small["ln3_b"], 0.5)
    du3 = _ffn_backward("ffn2", df2, u3, sv2, plan, in_first=False)
    dr2, dmx, acc2 = _resid_ln_bwd("ln2_bwd", du3, dr3, h1, m, sc3, g2, small["ln2_g"], small["ln2_b"], 1.0)
    du2, (dbias, dwa2, gsm) = _mix_backward(dmx, u2, svm, plan, mix_small)
    dr1, df1, acc1 = _resid_ln_bwd("ln1_bwd", du2, dr2, x, f1, sc2, g1, small["ln1_g"], small["ln1_b"], 0.5)
    du1 = _ffn_backward("ffn1", df1, u1, sv1, plan, in_first=True)
    grad_x, acc0 = _input_grad("input_grad", du1, dr1, x, sc1)

    drel = _hosted_mm(plan)("rel_bias_grad", "nt", _band_bias_grad(dbias).reshape(A_HEADS, CHUNK * A_BAND), onehot,
                            (A_HEADS, REL_SIZE, CHUNK * A_BAND), tm=A_HEADS, tn=REL_SIZE, tk=4608, precision=HIGHEST)
    loss = jnp.sum(acc3[ROW_LOSS])
    dmod = jnp.stack([acc0[ROW_DSH], acc0[ROW_DSC], acc1[ROW_DGATE], acc1[ROW_DSH], acc1[ROW_DSC], acc2[ROW_DGATE],
                      acc2[ROW_DSH], acc2[ROW_DSC], acc3[ROW_DGATE]])
    kd = d // 4
    small_grads = dict(ln1_g=acc1[ROW_DLN_G], ln1_b=acc1[ROW_DLN_B], ln2_g=acc2[ROW_DLN_G], ln2_b=acc2[ROW_DLN_B],
                       ln3_g=acc3[ROW_DLN_G], ln3_b=acc3[ROW_DLN_B], b_alpha=gsm[GLA_ROW_DBALPHA],
                       gla_norm_g=gsm[GLA_ROW_DGNORM, :kd // B_HEADS * 2], rel_bias=drel, w_alpha2=dwa2)
    return loss, grad_x, small_grads, dmod


HBM_SPEC = pl.BlockSpec(memory_space=pl.ANY)


def _mesh_pos():
    return lax.axis_index("x"), lax.axis_index("y"), lax.axis_index("c")


def _other_chips(x, y):
    return [(1 - x, y), (x, 1 - y), (1 - x, 1 - y)]


def _remote(src, dst, send_sem, recv_sem, to):
    return pltpu.make_async_remote_copy(src_ref=src, dst_ref=dst, send_sem=send_sem, recv_sem=recv_sem,
                                        device_id=to, device_id_type=MESH)


def _allgather_rows(name, v):
    m_per, n = v.shape

    def body(x_ref, out_ref, send_sems, recv_sems, local_sem):
        x, y, c = _mesh_pos()
        me, sibling = (x, y, c), (x, y, 1 - c)
        chips = _other_chips(x, y)

        def rows(px, py, pc):
            return out_ref.at[pl.ds((4 * px + 2 * py + pc) * m_per, m_per), :]

        def copy(k, block, to, src=None):
            return _remote(rows(*block) if src is None else src, rows(*block), send_sems.at[k], recv_sems.at[k], to)

        mine = pltpu.make_async_copy(x_ref, rows(*me), local_sem)
        mine.start()
        first = [copy(0, me, sibling, src=x_ref)]
        first += [copy(1 + j, me, (*chip, c), src=x_ref) for j, chip in enumerate(chips)]
        for cp in first:
            cp.start()
        passed = [copy(4 + j, (*chip, c), sibling) for j, chip in enumerate(chips)]
        for j, chip in enumerate(chips):
            copy(1 + j, (*chip, c), me).wait_recv()
            passed[j].start()
        copy(0, sibling, me).wait_recv()
        for j, chip in enumerate(chips):
            copy(4 + j, (*chip, 1 - c), me).wait_recv()
        for cp in first + passed:
            cp.wait_send()
        mine.wait()

    return pl.pallas_call(
        body, name=name, out_shape=jax.ShapeDtypeStruct((N_DEV * m_per, n), v.dtype),
        in_specs=[pl.BlockSpec(memory_space=pltpu.VMEM)], out_specs=pl.BlockSpec(memory_space=pltpu.VMEM),
        scratch_shapes=[pltpu.SemaphoreType.DMA((7,)), pltpu.SemaphoreType.DMA((7,)), pltpu.SemaphoreType.DMA],
    )(v)


def _allgather_weights(shards):
    n = len(shards)
    TO_X, TO_Y, PASS_TO_X, PASS_TO_Y, SIB_X, SIB_Y, SIB_D0, SIB_D1 = range(8)

    def body(*refs):
        ins, outs = refs[:n], refs[n:2 * n]
        send_sems, recv_sems = refs[2 * n:]
        x, y, c = _mesh_pos()
        sibling = (x, y, 1 - c)
        xn, yn, dg = _other_chips(x, y)
        j0, jx, jy, jd = (2 * p[0] + p[1] for p in ((x, y), xn, yn, dg))
        sends = []

        def rows(w, hc, quarter=None):
            hr = shards[w].shape[0] // 2
            if quarter is None:
                return pl.ds(hc * hr, hr)
            return pl.ds(hc * hr + quarter * (hr // 2), hr // 2)

        def push(src, dst, w, k, to):
            cp = _remote(src, dst, send_sems.at[w, k], recv_sems.at[w, k], to)
            cp.start()
            sends.append(cp)

        def landed(piece, w, k):
            _remote(piece, piece, send_sems.at[w, k], recv_sems.at[w, k], sibling).wait_recv()

        for w in range(n):
            mine = rows(w, c)
            push(ins[w].at[mine, :], outs[w].at[j0, mine, :], w, TO_X, (*xn, c))
            push(ins[w].at[mine, :], outs[w].at[j0, mine, :], w, TO_Y, (*yn, c))
        for w in range(n):
            half_x = outs[w].at[jx, rows(w, c), :]
            landed(half_x, w, TO_X)
            quarter = outs[w].at[jx, rows(w, c, 1), :]
            push(quarter, quarter, w, PASS_TO_Y, (*yn, c))
            push(half_x, half_x, w, SIB_X, sibling)
            half_y = outs[w].at[jy, rows(w, c), :]
            landed(half_y, w, TO_Y)
            quarter = outs[w].at[jy, rows(w, c, 0), :]
            push(quarter, quarter, w, PASS_TO_X, (*xn, c))
            push(half_y, half_y, w, SIB_Y, sibling)
        for w in range(n):
            for q, arrives_on, on in ((0, PASS_TO_X, SIB_D0), (1, PASS_TO_Y, SIB_D1)):
                piece = outs[w].at[jd, rows(w, c, q), :]
                landed(piece, w, arrives_on)
                push(piece, piece, w, on, sibling)
        for w in range(n):
            landed(outs[w].at[jx, rows(w, 1 - c), :], w, SIB_X)
            landed(outs[w].at[jy, rows(w, 1 - c), :], w, SIB_Y)
            landed(outs[w].at[jd, rows(w, 1 - c, 0), :], w, SIB_D0)
            landed(outs[w].at[jd, rows(w, 1 - c, 1), :], w, SIB_D1)
        for cp in sends:
            cp.wait_send()

    return pl.pallas_call(
        body, name="allgather_weights",
        out_shape=[jax.ShapeDtypeStruct((N_CHIPS,) + sh.shape, sh.dtype) for sh in shards],
        in_specs=[HBM_SPEC] * n, out_specs=[HBM_SPEC] * n,
        scratch_shapes=[pltpu.SemaphoreType.DMA((n, 8)), pltpu.SemaphoreType.DMA((n, 8))],
    )(*shards)


def _half(ref, hc, col, *lead):
    rows, cols = ref.shape[-2:]
    if col:
        return ref.at[(*lead, slice(None), pl.ds(hc * (cols // 2), cols // 2))]
    return ref.at[(*lead, pl.ds(hc * (rows // 2), rows // 2), slice(None))]


def _half_shape(shape, col):
    return shape[:-2] + ((shape[-2], shape[-1] // 2) if col else (shape[-2] // 2, shape[-1]))


def _quarter(ref, hc, q, col, *lead):
    rows, cols = ref.shape[-2:]
    if col:
        return ref.at[(*lead, slice(None), pl.ds(hc * (cols // 2) + q * (cols // 4), cols // 4))]
    return ref.at[(*lead, pl.ds(hc * (rows // 2) + q * (rows // 4), rows // 4), slice(None))]


def _stage_gather_ici(shards, cols):
    n = len(shards)
    TO_X, TO_Y, PASS_TO_X, PASS_TO_Y = range(4)

    def places():
        x, y, c = _mesh_pos()
        xn, yn, dg = _other_chips(x, y)
        return c, (*xn, c), (*yn, c), [2 * p[0] + p[1] for p in ((x, y), xn, yn, dg)]

    def remote(src, dst, send, recv, w, k, to):
        return _remote(src, dst, send.at[4 * w + k], recv.at[4 * w + k], to)

    def own(ins, outs, send, recv):
        c, to_x, to_y, (j0, _, _, _) = places()
        for w in range(n):
            for k, to in ((TO_X, to_x), (TO_Y, to_y)):
                yield remote(_half(ins[w], c, cols[w]), _half(outs[w], c, cols[w], j0), send, recv, w, k, to)

    def relays(ins, outs, send, recv):
        c, to_x, to_y, (_, jx, jy, _) = places()
        for w in range(n):
            for j, k, q, pass_k, to in ((jx, TO_X, 1, PASS_TO_Y, to_y), (jy, TO_Y, 0, PASS_TO_X, to_x)):
                half = _half(outs[w], c, cols[w], j)
                piece = _quarter(outs[w], c, q, cols[w], j)
                yield remote(half, half, send, recv, w, k, to), remote(piece, piece, send, recv, w, pass_k, to)

    def passed(ins, outs, send, recv):
        c, to_x, _, (_, _, _, jd) = places()
        for w in range(n):
            for q, k in ((0, PASS_TO_X), (1, PASS_TO_Y)):
                piece = _quarter(outs[w], c, q, cols[w], jd)
                yield remote(piece, piece, send, recv, w, k, to_x)

    def start(*refs):
        for cp in own(*refs):
            cp.start()

    def relay(*refs):
        for arrived, onward in relays(*refs):
            arrived.wait_recv()
            onward.start()

    def finish(*refs):
        for cp in passed(*refs):
            cp.wait_recv()
        for cp in own(*refs):
            cp.wait_send()
        for _, onward in relays(*refs):
            onward.wait_send()

    outs = [jax.ShapeDtypeStruct((N_CHIPS,) + sh.shape, sh.dtype) for sh in shards]
    return _Stage(shards, outs, 4 * n, start, finish, relay=relay)


def _stage_gather_d2d(partial, cols):
    n = len(partial)

    def copies(ins, outs, send, recv):
        x, y, c = _mesh_pos()
        for w in range(n):
            for r, chip in enumerate(_other_chips(x, y)):
                jr = 2 * chip[0] + chip[1]
                mine = _remote(_half(ins[w], c, cols[w], jr), _half(outs[w], c, cols[w], jr), send.at[3 * w + r],
                               recv.at[3 * w + r], (x, y, 1 - c))
                got = _half(outs[w], 1 - c, cols[w], jr)
                yield mine, _remote(got, got, send.at[3 * w + r], recv.at[3 * w + r], (x, y, 1 - c))

    def start(*refs):
        for mine, _ in copies(*refs):
            mine.start()

    def finish(*refs):
        pairs = list(copies(*refs))
        for _, theirs in pairs:
            theirs.wait_recv()
        for mine, _ in pairs:
            mine.wait_send()

    outs = [jax.ShapeDtypeStruct(p.shape, p.dtype) for p in partial]
    return _Stage(partial, outs, 3 * n, start, finish, aliases={w: w for w in range(n)})


def _stage_exchange_halves(grads, cols):
    n = len(grads)

    def copies(ins, outs, send, recv):
        x, y, c = _mesh_pos()
        for w in range(n):
            yield _remote(_half(ins[w], 1 - c, cols[w], slice(None)), outs[w], send.at[w], recv.at[w], (x, y, 1 - c))

    def start(*refs):
        for cp in copies(*refs):
            cp.start()

    def finish(*refs):
        cps = list(copies(*refs))
        for cp in cps:
            cp.wait_recv()
        for cp in cps:
            cp.wait_send()

    outs = [jax.ShapeDtypeStruct(_half_shape(g.shape, col), g.dtype) for g, col in zip(grads, cols)]
    return _Stage(grads, outs, n, start, finish)


def _stage_scatter(parts):
    n = len(parts)

    def copies(ins, outs, send, recv):
        x, y, c = _mesh_pos()
        for w in range(n):
            for r, chip in enumerate(_other_chips(x, y)):
                jr = 2 * chip[0] + chip[1]
                yield _remote(ins[w].at[jr], outs[w].at[r], send.at[3 * w + r], recv.at[3 * w + r], (*chip, c))

    def start(*refs):
        for cp in copies(*refs):
            cp.start()

    def finish(*refs):
        cps = list(copies(*refs))
        for cp in cps:
            cp.wait_recv()
        for cp in cps:
            cp.wait_send()

    outs = [jax.ShapeDtypeStruct((3,) + p.shape[1:], p.dtype) for p in parts]
    return _Stage(parts, outs, 3 * n, start, finish)


def _stage_share(fulls, cols):
    n = len(fulls)

    def copies(ins, outs, send, recv):
        x, y, c = _mesh_pos()
        for w in range(n):
            theirs = _half(outs[w], 1 - c, cols[w])
            yield (_remote(_half(ins[w], c, cols[w]), _half(outs[w], c, cols[w]), send.at[w], recv.at[w], (x, y, 1 - c)),
                   _remote(theirs, theirs, send.at[w], recv.at[w], (x, y, 1 - c)))

    def start(*refs):
        for mine, _ in copies(*refs):
            mine.start()

    def finish(*refs):
        pairs = list(copies(*refs))
        for _, theirs in pairs:
            theirs.wait_recv()
        for mine, _ in pairs:
            mine.wait_send()

    outs = [jax.ShapeDtypeStruct(h.shape, h.dtype) for h in fulls]
    return _Stage(fulls, outs, n, start, finish, aliases={w: w for w in range(n)})


def _run_stages(name, stages):
    return _pcall(None, stages, name=name, out_shape=[], in_specs=[], out_specs=[])()[1]


TILE_BYTES = 2 * 1024 * 1024
SUM_TILE_BYTES = 4 * 1024 * 1024


def _row_tile(rows, cols, itemsize=4, tile_bytes=TILE_BYTES):
    for t in (1024, 512, 256, 128, 64, 32, 16, 8):
        if rows % t == 0 and t * cols * itemsize <= tile_bytes:
            return t
    return rows


def _col_tile(rows, cols, itemsize=4, tile_bytes=TILE_BYTES):
    for t in (2048, 1024, 512, 256, 128):
        if cols % t == 0 and t * rows * itemsize <= tile_bytes:
            return t
    return cols


def _tiling(rows, cols, col, tile_bytes=TILE_BYTES):
    if col:
        tc = _col_tile(rows, cols, tile_bytes=tile_bytes)
        return (rows, tc), cols // tc
    tr = _row_tile(rows, cols, tile_bytes=tile_bytes)
    return (tr, cols), rows // tr


def _strip(col, i):
    return (0, i) if col else (i, 0)


def _pair_sum(name, g, recv, core, col):
    blk, nb = _tiling(*recv.shape[1:], col, tile_bytes=SUM_TILE_BYTES)

    def body(c_ref, g_ref, r_ref, o_ref):
        o_ref[...] = (g_ref[...] + r_ref[...]).astype(BF16)

    grid_spec = pltpu.PrefetchScalarGridSpec(
        num_scalar_prefetch=1, grid=(N_CHIPS, nb),
        in_specs=[pl.BlockSpec((None,) + blk, lambda j, i, cr: (j,) + _strip(col, cr[0] * nb + i)),
                  pl.BlockSpec((None,) + blk, lambda j, i, cr: (j,) + _strip(col, i))],
        out_specs=pl.BlockSpec((None,) + blk, lambda j, i, cr: (j,) + _strip(col, i)))
    return pl.pallas_call(body, name=name, out_shape=jax.ShapeDtypeStruct(recv.shape, BF16), grid_spec=grid_spec,
                          compiler_params=_cp("parallel", "parallel"))(core, g, recv)


def _quad_sum(name, own, landed, chip_core, col):
    rows, cols = landed.shape[1:]
    blk, nb = _tiling(rows, cols, col, tile_bytes=SUM_TILE_BYTES)
    full = (rows, 2 * cols) if col else (2 * rows, cols)

    def body(cc_ref, own_ref, l_ref, o_ref):
        o_ref[...] = ((own_ref[...].astype(F32) + l_ref[0].astype(F32)) + l_ref[1].astype(F32)) + l_ref[2].astype(F32)

    grid_spec = pltpu.PrefetchScalarGridSpec(
        num_scalar_prefetch=1, grid=(nb,),
        in_specs=[pl.BlockSpec((None,) + blk, lambda i, cc: (cc[0],) + _strip(col, i)),
                  pl.BlockSpec((3,) + blk, lambda i, cc: (0,) + _strip(col, i))],
        out_specs=pl.BlockSpec(blk, lambda i, cc: _strip(col, cc[1] * nb + i)))
    return pl.pallas_call(body, name=name, out_shape=jax.ShapeDtypeStruct(full, F32), grid_spec=grid_spec,
                          compiler_params=_cp("arbitrary"))(chip_core, own, landed)


def _device_sum(name, gathered):
    def body(g_ref, o_ref):
        total = g_ref[0]
        for k in range(1, N_DEV):
            total = total + g_ref[k]
        o_ref[...] = total

    return pl.pallas_call(body, name=name, out_shape=jax.ShapeDtypeStruct(gathered.shape[1:], F32))(gathered)


def _adamw(name, w, g, m, v):
    rows, cols = w.shape
    col = rows % SUBLANES != 0
    blk, nb = _tiling(rows, cols, col)
    bc1 = 1.0 - ADAM_B1 ** ADAM_STEP
    bc2 = 1.0 - ADAM_B2 ** ADAM_STEP

    def body(w_ref, g_ref, m_ref, v_ref, d_ref, mo_ref, vo_ref):
        gv = g_ref[...]
        mn = ADAM_B1 * m_ref[...] + (1.0 - ADAM_B1) * gv
        vn = ADAM_B2 * v_ref[...] + (1.0 - ADAM_B2) * (gv * gv)
        mo_ref[...] = mn
        vo_ref[...] = vn
        d_ref[...] = -ADAM_LR * ((mn / bc1) / (jnp.sqrt(vn / bc2) + ADAM_EPS) + ADAM_WD * w_ref[...])

    spec = pl.BlockSpec(blk, lambda i: _strip(col, i))
    return pl.pallas_call(
        body, name=name, out_shape=[jax.ShapeDtypeStruct((rows, cols), F32)] * 3, grid=(nb,),
        in_specs=[spec] * 4, out_specs=[spec] * 3, compiler_params=_cp("parallel"),
    )(w, g, m, v)


WEIGHTS = ["w_ada", "b_ada", "ffn1_w_in", "ffn1_w_out", "ln1_g", "ln1_b", "w_mix_in", "rel_bias", "w_alpha2",
           "b_alpha", "gla_norm_g", "w_proj_a", "w_proj_b", "w_mix_out", "ln2_g", "ln2_b", "ffn2_w_in", "ffn2_w_out",
           "ln3_g", "ln3_b"]
BIG = {"ffn1_w_in": True, "ffn1_w_out": False, "w_mix_in": False, "w_proj_a": True, "w_proj_b": True,
       "w_mix_out": False, "ffn2_w_in": True, "ffn2_w_out": False}
TRANSPOSED = ("w_mix_in",)
STACKED = ("ffn1_w_in", "ffn2_w_in", "w_mix_in")
GROUP_FFN1 = ("ffn1_w_in", "ffn1_w_out")
GROUP_PROJ = ("w_proj_a", "w_proj_b", "w_mix_out")
SMALL = ["ln1_g", "ln1_b", "ln2_g", "ln2_b", "ln3_g", "ln3_b", "b_alpha", "gla_norm_g", "rel_bias", "w_alpha2"]


def _pad_rows(vec, rows=SUBLANES):
    per = -(-vec.shape[0] // (rows * LANES)) * LANES
    return jnp.pad(vec, (0, rows * per - vec.shape[0])).reshape(rows, per)


def _silu(v):
    return v * _sigmoid(v)


class _MeshPlan:
    def __init__(self, shards, chip, core):
        self.shards, self.chip = shards, chip
        self.core1 = core.astype(jnp.int32).reshape(1)
        self.chip_core = jnp.stack([chip, core]).astype(jnp.int32)
        self.partial, self.full, self.local, self.pair, self.half, self.final, self.memos = {}, {}, {}, {}, {}, {}, {}
        ici, d2d, x1, x2, x3 = self.gather_ici, self.gather_d2d, self.exchange, self.scatter, self.share
        mix_in, in1, out1, in2, out2 = ("w_mix_in",), ("ffn1_w_in",), ("ffn1_w_out",), ("ffn2_w_in",), ("ffn2_w_out",)
        self.schedule = {
            "ffn1_in_fwd": [ici(mix_in)], "ffn1_out_fwd": [d2d(mix_in), ici(out2)],
            "mix_in_a": [d2d(out2)], "mix_in_g": [ici(GROUP_PROJ)],
            "attn_fwd": [ici(in2), d2d(GROUP_PROJ)], "gla_fwd": [d2d(in2)],
            "ffn2_dw_in": [x1(out2)], "ffn2_du": [x2(out2), x1(in2)], "mix_dw_out": [x3(out2)],
            "attn_bwd": [x2(in2)], "gla_bwd": [x3(in2), x1(GROUP_PROJ)],
            "mix_du_g": [x2(GROUP_PROJ)], "mix_dw_g": [x3(GROUP_PROJ)],
            "ffn1_out_bwd": [x1(mix_in)], "ffn1_dw_in": [x2(mix_in)], "ffn1_dw_out": [x3(mix_in), x1(in1)],
            "ffn1_du": [x2(in1), x1(out1)], "rel_bias_grad": [x2(out1), x3(in1)],
        }

    def weight(self, k):
        return self.full[k]

    def grad(self, k, g):
        r, cc = self.shards[k].shape
        if k not in STACKED:
            g = g.reshape(r, N_CHIPS, cc).transpose(1, 0, 2) if BIG[k] else g.reshape(N_CHIPS, r, cc)
        self.local[k] = g

    def memo(self, key, make):
        if key not in self.memos:
            self.memos[key] = make()
        return self.memos[key]

    def host(self, name, call):
        builders = self.schedule.get(name)
        if not builders:
            return call(None)
        built = [b() for b in builders]
        main, comm = call([st for st, _ in built])
        for (_, post), res in zip(built, comm):
            post(res)
        return main

    def run(self, name, builders):
        built = [b() for b in builders]
        for (_, post), res in zip(built, _run_stages(name, [st for st, _ in built])):
            post(res)

    def set_gathered(self, names, gathered):
        for k, g in zip(names, gathered):
            _, r, cc = g.shape
            g = lax.dynamic_update_slice(g, self.shards[k][None], (self.chip, 0, 0))
            if k not in STACKED:
                g = g.transpose(1, 0, 2).reshape(r, N_CHIPS * cc) if BIG[k] else g.reshape(N_CHIPS * r, cc)
            self.full[k] = g

    @staticmethod
    def cols(names):
        return [k in TRANSPOSED for k in names]

    def gather_ici(self, names):
        def post(res):
            self.partial.update(zip(names, res))
        return lambda: (_stage_gather_ici([self.shards[k] for k in names], self.cols(names)), post)

    def gather_d2d(self, names):
        return lambda: (_stage_gather_d2d([self.partial[k] for k in names], self.cols(names)),
                        lambda res: self.set_gathered(names, res))

    def exchange(self, names):
        def post(res):
            for k, r in zip(names, res):
                self.pair[k] = _pair_sum(f"pair_sum_{k}", self.local[k], r, self.core1, k in TRANSPOSED)
        return lambda: (_stage_exchange_halves([self.local[k] for k in names], self.cols(names)), post)

    def scatter(self, names):
        def post(res):
            for k, landed in zip(names, res):
                self.half[k] = _quad_sum(f"quad_sum_{k}", self.pair[k], landed, self.chip_core, k in TRANSPOSED)
        return lambda: (_stage_scatter([self.pair[k] for k in names]), post)

    def share(self, names):
        def post(res):
            self.final.update(zip(names, res))
        return lambda: (_stage_share([self.half[k] for k in names], self.cols(names)), post)


def _step(args):
    x_pos, y_pos, c_pos = _mesh_pos()
    chip = 2 * x_pos + y_pos
    dev = 4 * x_pos + 2 * y_pos + c_pos
    take = lambda name, k: args[name][0].T if k in TRANSPOSED else args[name][0]
    w = {k: take(k, k) for k in WEIGHTS}
    mom = {k: take("m_" + k, k) for k in WEIGHTS}
    vel = {k: take("v_" + k, k) for k in WEIGHTS}
    x = args["x"][0]
    target = args["loss_target"][0]
    s, d = x.shape
    kd = d // 4
    rel_sh = w["rel_bias"].shape[1]
    wa2_sh = w["w_alpha2"].shape[1]
    ada_sh = w["w_ada"].shape[1]

    n_rel, n_wa2 = A_HEADS * rel_sh, GATE_RANK * wa2_sh
    packed = _pad_rows(jnp.concatenate([args["c"].reshape(-1), w["rel_bias"].reshape(-1), w["w_alpha2"].reshape(-1)]))
    got = _allgather_rows("gather_small_inputs", packed).reshape(N_DEV, -1)
    c_all = got[:, :d]
    per_chip = got[0::2]
    rel_bias = per_chip[:, d:d + n_rel].reshape(N_CHIPS, A_HEADS, rel_sh).transpose(1, 0, 2).reshape(A_HEADS, -1)
    w_alpha2 = per_chip[:, d + n_rel:d + n_rel + n_wa2].reshape(N_CHIPS, GATE_RANK, wa2_sh).transpose(1, 0, 2)
    w_alpha2 = w_alpha2.reshape(GATE_RANK, -1)

    b_shard = lax.dynamic_slice(w["b_ada"], (chip * ada_sh,), (ada_sh,))
    mod_shard = _mm("ada_fwd", "nn", c_all, w["w_ada"], (N_DEV, ada_sh, d), tm=N_DEV, tn=_tile(ada_sh, (512, 128)),
                    tk=d, precision=HIGHEST, a_fn=_silu, add=jnp.broadcast_to(b_shard[None], (N_DEV, ada_sh)))
    mod_all = _allgather_rows("gather_mod", mod_shard).reshape(N_DEV, N_DEV, ada_sh)[0::2]
    mod_all = mod_all.transpose(1, 0, 2).reshape(N_DEV, N_MOD * d)
    mod = lax.dynamic_index_in_dim(mod_all, dev, 0, keepdims=False).reshape(N_MOD, d)

    names = list(BIG)
    plan = _MeshPlan({k: w[k].astype(BF16) for k in names}, chip, c_pos)
    plan.set_gathered(GROUP_FFN1, _allgather_weights([plan.shards[k] for k in GROUP_FFN1]))

    small = dict(rel_bias=rel_bias, w_alpha2=w_alpha2, b_alpha=w["b_alpha"][None], gla_norm_g=w["gla_norm_g"][None])
    for k in ("ln1_g", "ln1_b", "ln2_g", "ln2_b", "ln3_g", "ln3_b"):
        small[k] = w[k][None]
    loss_local, grad_x, small_grads, dmod = _device_step(x, target, mod, small, plan)
    loss = lax.psum(loss_local, ("x", "y", "c"))
    plan.run("grad_tail_share", [plan.share(GROUP_FFN1[1:])])

    flat = jnp.concatenate([small_grads[k].reshape(-1) for k in SMALL] + [dmod.reshape(-1)])
    n_small = flat.shape[0] - N_MOD * d
    packed = _pad_rows(flat)
    all_small = _allgather_rows("gather_small_grads", packed).reshape(N_DEV, SUBLANES, -1)
    summed = _device_sum("small_grad_sum", all_small).reshape(-1)
    dmod_all = all_small.reshape(N_DEV, -1)[:, n_small:n_small + N_MOD * d]
    dmod_shard = lax.dynamic_slice(dmod_all, (0, chip * ada_sh), (N_DEV, ada_sh))
    grads = {"b_ada": summed[n_small:n_small + N_MOD * d]}
    off = 0
    for k in SMALL:
        size = small_grads[k].size
        grads[k] = summed[off:off + size].reshape(small_grads[k].shape)
        off += size
    grads["rel_bias"] = lax.dynamic_slice(grads["rel_bias"], (0, chip * rel_sh), (A_HEADS, rel_sh))
    grads["w_alpha2"] = lax.dynamic_slice(grads["w_alpha2"], (0, chip * wa2_sh), (GATE_RANK, wa2_sh))
    grads["w_ada"] = _mm("ada_bwd", "nn", jnp.pad(c_all.T, ((0, 0), (0, LANES - N_DEV))),
                         jnp.pad(dmod_shard, ((0, LANES - N_DEV), (0, 0))), (d, ada_sh, LANES), tm=_tile(d, (1024,)),
                         tn=_tile(ada_sh, (512, 128)), tk=LANES, precision=HIGHEST, a_fn=_silu)

    grads.update(plan.final)

    delta, new_m, new_v = {}, {}, {}
    for k in ["w_ada"] + names:
        delta[k], new_m[k], new_v[k] = _adamw(f"adamw_{k}", w[k], grads[k], mom[k], vel[k])
    tiny = ["b_ada"] + SMALL
    pack = lambda src: _pad_rows(jnp.concatenate([src[k].reshape(-1) for k in tiny]), rows=1).reshape(-1, LANES)
    outs = _adamw("adamw_small", pack(w), pack(grads), pack(mom), pack(vel))
    off = 0
    for k in tiny:
        size = w[k].size
        for dst, src in zip((delta, new_m, new_v), outs):
            dst[k] = src.reshape(-1)[off:off + size].reshape(w[k].shape)
        off += size

    give = lambda src: [src[k].T[None] if k in TRANSPOSED else src[k][None] for k in WEIGHTS]
    return (loss, grad_x[None], *give(grads), *give(delta), *give(new_m), *give(new_v))


def kernel(x, c, w_ada, b_ada, ffn1_w_in, ffn1_w_out, ln1_g, ln1_b, w_mix_in, rel_bias, w_alpha2, b_alpha, gla_norm_g, w_proj_a, w_proj_b, w_mix_out, ln2_g, ln2_b, ffn2_w_in, ffn2_w_out, ln3_g, ln3_b, loss_target, m_w_ada, m_b_ada, m_ffn1_w_in, m_ffn1_w_out, m_ln1_g, m_ln1_b, m_w_mix_in, m_rel_bias, m_w_alpha2, m_b_alpha, m_gla_norm_g, m_w_proj_a, m_w_proj_b, m_w_mix_out, m_ln2_g, m_ln2_b, m_ffn2_w_in, m_ffn2_w_out, m_ln3_g, m_ln3_b, v_w_ada, v_b_ada, v_ffn1_w_in, v_ffn1_w_out, v_ln1_g, v_ln1_b, v_w_mix_in, v_rel_bias, v_w_alpha2, v_b_alpha, v_gla_norm_g, v_w_proj_a, v_w_proj_b, v_w_mix_out, v_ln2_g, v_ln2_b, v_ffn2_w_in, v_ffn2_w_out, v_ln3_g, v_ln3_b):
    return _step(dict(locals()))
```

```python
import functools

import jax
import jax.numpy as jnp
from jax import lax
from jax.experimental import pallas as pl
from jax.experimental.pallas import tpu as pltpu

F32 = jnp.float32
BF16 = jnp.bfloat16
MESH = pl.DeviceIdType.MESH
HIGHEST = lax.Precision.HIGHEST

VMEM_LIMIT_BYTES = 56 * 1024 * 1024
LANES = 128
SUBLANES = 8

CHUNK = 64
A_HEADS = 16
A_HEAD_DIM = 64
A_PAST_CHUNKS = 8
A_BAND = (A_PAST_CHUNKS + 1) * CHUNK
A_PAD = A_PAST_CHUNKS * CHUNK
REL_CLIP = 256
REL_SIZE = REL_CLIP + CHUNK
B_HEADS = 4
GATE_RANK = 16
GATE_TAU = 16.0
N_MOD = 9
DEPTH = 1
ALPHA = (2.0 * DEPTH) ** 0.25
LN_EPS = 1e-5
RMS_EPS = 1e-6
ADAM_LR = 0.001
ADAM_B1 = 0.9
ADAM_B2 = 0.999
ADAM_EPS = 1e-08
ADAM_WD = 0.01
ADAM_STEP = 10
NEG_BIG = -1e30

N_CHIPS = 4
N_DEV = 8


def _cp(*sem):
    return pltpu.CompilerParams(dimension_semantics=sem, vmem_limit_bytes=VMEM_LIMIT_BYTES)


class _Stage:
    def __init__(self, arrays, out_shapes, n_sems, start, finish, aliases=None, relay=None):
        self.arrays, self.out_shapes, self.n_sems = list(arrays), list(out_shapes), n_sems
        self.start, self.finish, self.relay, self.aliases = start, finish, relay, dict(aliases or {})


def _pcall(body, stages, *, name, out_shape, in_specs, out_specs, grid=(), scratch_shapes=(), compiler_params=None):
    single = not isinstance(out_shape, (list, tuple))
    outs = [out_shape] if single else list(out_shape)
    ospecs = [out_specs] if single else list(out_specs)
    in_specs, scratch_shapes = list(in_specs), list(scratch_shapes)
    n_in, n_out, n_sc = len(in_specs), len(outs), len(scratch_shapes)
    stages = list(stages or [])
    c_in = [a for st in stages for a in st.arrays]
    c_out = [o for st in stages for o in st.out_shapes]
    aliases = {}
    io, oo = n_in, n_out
    for st in stages:
        for a, b in st.aliases.items():
            aliases[io + a] = oo + b
        io += len(st.arrays)
        oo += len(st.out_shapes)

    def wrapped(*refs):
        ins = refs[:n_in]
        cins = refs[n_in:n_in + len(c_in)]
        base = n_in + len(c_in)
        mouts = refs[base:base + n_out]
        couts = refs[base + n_out:base + n_out + len(c_out)]
        base += n_out + len(c_out)
        scr = refs[base:base + n_sc]
        sems = refs[base + n_sc:]

        def each(phase):
            i = o = 0
            for k, st in enumerate(stages):
                fn = (st.start, st.relay, st.finish)[phase]
                if fn is not None:
                    fn(cins[i:i + len(st.arrays)], couts[o:o + len(st.out_shapes)], sems[2 * k], sems[2 * k + 1])
                i += len(st.arrays)
                o += len(st.out_shapes)

        if stages and grid:
            step = functools.reduce(lambda acc, a: acc * grid[a] + pl.program_id(a), range(len(grid)), 0)
            steps = functools.reduce(lambda a, b: a * b, grid)
            pl.when(step == 0)(lambda: each(0))
            if any(st.relay for st in stages):
                pl.when(step == (2 * steps) // 3)(lambda: each(1))
            if body is not None:
                body(*ins, *mouts, *scr)
            pl.when(step == steps - 1)(lambda: each(2))
        else:
            each(0)
            each(1)
            if body is not None:
                body(*ins, *mouts, *scr)
            each(2)

    sem_shapes = []
    for st in stages:
        sem_shapes += [pltpu.SemaphoreType.DMA((st.n_sems,)), pltpu.SemaphoreType.DMA((st.n_sems,))]
    kwargs = dict(grid=grid) if grid else {}
    if compiler_params is not None:
        kwargs["compiler_params"] = compiler_params

    def run(*operands):
        res = pl.pallas_call(
            wrapped, name=name, out_shape=outs + c_out, in_specs=in_specs + [HBM_SPEC] * len(c_in),
            out_specs=ospecs + [HBM_SPEC] * len(c_out), scratch_shapes=scratch_shapes + sem_shapes,
            input_output_aliases=aliases, **kwargs)(*operands, *c_in)
        main = res[0] if single else tuple(res[:n_out])
        if not stages:
            return main
        comm, o = [], n_out
        for st in stages:
            comm.append(list(res[o:o + len(st.out_shapes)]))
            o += len(st.out_shapes)
        return main, comm

    return run


LONG_K = (2048, 1024)


def _tile(n, prefs):
    for t in prefs:
        if t <= n and n % t == 0:
            return t
    return n


_DIMS = {"nn": (((1,), (0,)), ((), ())), "nt": (((1,), (1,)), ((), ())), "tn": (((0,), (0,)), ((), ()))}


def _dot(a, b, mode="nn", precision=None):
    return lax.dot_general(a, b, _DIMS[mode], precision=precision, preferred_element_type=F32)


def _sigmoid(x):
    return 0.5 * jnp.tanh(0.5 * x) + 0.5


EPILOGUE_STRIP = 256


def _strips(n, width=EPILOGUE_STRIP):
    width = width if n % width == 0 else n
    return [slice(j, j + width) for j in range(0, n, width)]


def _mm(name, mode, a, b, mnk, *, tm, tn, tk, out_dtype=F32, precision=None, a_spec=None, b_spec=None,
        out_shape=None, o_spec=None, add=None, a_fn=None, thin=None, stages=None):
    m, n, k = mnk
    assert m % tm == 0 and n % tn == 0 and k % tk == 0, (name, mnk, tm, tn, tk)
    nk = k // tk
    if a_spec is None:
        a_spec = {"nn": pl.BlockSpec((tm, tk), lambda i, j, kk: (i, kk)),
                  "nt": pl.BlockSpec((tm, tk), lambda i, j, kk: (i, kk)),
                  "tn": pl.BlockSpec((tk, tm), lambda i, j, kk: (kk, i))}[mode]
    if b_spec is None:
        b_spec = {"nn": pl.BlockSpec((tk, tn), lambda i, j, kk: (kk, j)),
                  "nt": pl.BlockSpec((tn, tk), lambda i, j, kk: (j, kk)),
                  "tn": pl.BlockSpec((tk, tn), lambda i, j, kk: (kk, j))}[mode]
    if o_spec is None:
        o_spec = pl.BlockSpec((tm, tn), lambda i, j, kk: (i, j))
    if out_shape is None:
        out_shape = (m, n)
    has_add = add is not None
    n_in = 2 + has_add + (2 if thin else 0)

    def body(*refs):
        a_ref, b_ref = refs[0], refs[1]
        add_ref = refs[2] if has_add else None
        o_ref = refs[n_in]
        av = a_ref[...]
        if a_fn is not None:
            av = a_fn(av)
        part = _dot(av, b_ref[...], mode, precision)

        def finish(total):
            if has_add:
                total = total + add_ref[...]
            if thin:
                total = total + _dot(refs[n_in - 2][...], refs[n_in - 1][...])
            o_ref[...] = total.astype(out_dtype)

        if nk == 1:
            finish(part)
        else:
            acc_ref = refs[-1]
            kk = pl.program_id(2)

            @pl.when(kk == 0)
            def _():
                acc_ref[...] = part

            @pl.when(kk > 0)
            def _():
                acc_ref[...] += part

            @pl.when(kk == nk - 1)
            def _():
                finish(acc_ref[...])

    in_specs = [a_spec, b_spec]
    operands = [a, b]
    if has_add:
        in_specs.append(pl.BlockSpec((tm, tn), lambda i, j, kk: (i, j)))
        operands.append(add)
    if thin:
        k2 = thin[0].shape[1]
        in_specs += [pl.BlockSpec((tm, k2), lambda i, j, kk: (i, 0)), pl.BlockSpec((k2, tn), lambda i, j, kk: (0, j))]
        operands += list(thin)
    return _pcall(
        body, stages, name=name, out_shape=jax.ShapeDtypeStruct(out_shape, out_dtype), grid=(m // tm, n // tn, nk),
        in_specs=in_specs, out_specs=o_spec,
        scratch_shapes=[pltpu.VMEM((tm, tn), F32)] if nk > 1 else [],
        compiler_params=_cp("arbitrary", "arbitrary", "arbitrary") if stages else _cp("parallel", "parallel", "arbitrary"),
    )(*operands)


def _row_spec(tr, d):
    return pl.BlockSpec((tr, d), lambda i: (i, 0))


def _vec_spec(d, rows=1):
    return pl.BlockSpec((rows, d), lambda i: (0, 0))


def _modulate(name, x, sh, sc):
    s, d = x.shape
    tr = _tile(s, (512, 256))

    def body(x_ref, sh_ref, sc_ref, o_ref):
        o_ref[...] = (x_ref[...] * (1.0 + sc_ref[...]) + sh_ref[...]).astype(BF16)

    return pl.pallas_call(
        body, name=name, out_shape=jax.ShapeDtypeStruct((s, d), BF16), grid=(s // tr,),
        in_specs=[_row_spec(tr, d), _vec_spec(d), _vec_spec(d)], out_specs=_row_spec(tr, d),
        compiler_params=_cp("parallel"),
    )(x, sh, sc)


def _ln_stats(r):
    mu = jnp.mean(r, axis=-1, keepdims=True)
    xc = r - mu
    var = jnp.mean(xc * xc, axis=-1, keepdims=True)
    rstd = lax.rsqrt(var + LN_EPS)
    return xc * rstd, rstd


def _resid_ln_fwd(name, x, f, gate, ln_g, ln_b, sh_n, sc_n, coef):
    s, d = x.shape
    tr = _tile(s, (256,))

    def body(x_ref, f_ref, gate_ref, g_ref, b_ref, sh_ref, sc_ref, h_ref, u_ref):
        r = ALPHA * x_ref[...] + (coef * gate_ref[...]) * f_ref[...]
        xhat, _ = _ln_stats(r)
        h = xhat * g_ref[...] + b_ref[...]
        h_ref[...] = h
        u_ref[...] = (h * (1.0 + sc_ref[...]) + sh_ref[...]).astype(BF16)

    return pl.pallas_call(
        body, name=name, out_shape=(jax.ShapeDtypeStruct((s, d), F32), jax.ShapeDtypeStruct((s, d), BF16)),
        grid=(s // tr,), in_specs=[_row_spec(tr, d), _row_spec(tr, d)] + [_vec_spec(d)] * 5,
        out_specs=(_row_spec(tr, d), _row_spec(tr, d)), compiler_params=_cp("parallel"),
    )(x, f, gate, ln_g, ln_b, sh_n, sc_n)


ROW_DSC, ROW_DSH, ROW_DLN_G, ROW_DLN_B, ROW_DGATE, ROW_LOSS = 0, 1, 2, 3, 4, 5


def _ln_bwd_core(dy, xhat, rstd, ln_g):
    dxhat = dy * ln_g
    m1 = jnp.mean(dxhat, axis=-1, keepdims=True)
    m2 = jnp.mean(dxhat * xhat, axis=-1, keepdims=True)
    return rstd * (dxhat - m1 - xhat * m2)


def _colsum(v):
    return jnp.sum(v, axis=0, keepdims=True)


def _final_ln_loss_bwd(name, x, f, target, gate, ln_g, ln_b, coef):
    s, d = x.shape
    tr = _tile(s, (256,))
    inv_d = 1.0 / d

    def body(x_ref, f_ref, t_ref, gate_ref, g_ref, b_ref, dr_ref, df_ref, acc_ref):
        @pl.when(pl.program_id(0) == 0)
        def _():
            acc_ref[...] = jnp.zeros_like(acc_ref)

        fv = f_ref[...]
        r = ALPHA * x_ref[...] + (coef * gate_ref[...]) * fv
        xhat, rstd = _ln_stats(r)
        h = xhat * g_ref[...] + b_ref[...]
        err = h - t_ref[...]
        dy = err * inv_d
        dr = _ln_bwd_core(dy, xhat, rstd, g_ref[...])
        dr_ref[...] = dr
        df_ref[...] = ((coef * gate_ref[...]) * dr).astype(BF16)
        acc_ref[ROW_DLN_G:ROW_DLN_G + 1, :] += _colsum(dy * xhat)
        acc_ref[ROW_DLN_B:ROW_DLN_B + 1, :] += _colsum(dy)
        acc_ref[ROW_DGATE:ROW_DGATE + 1, :] += _colsum((coef * dr) * fv)
        acc_ref[ROW_LOSS:ROW_LOSS + 1, :] += _colsum(err * err) * (0.5 * inv_d)

    return pl.pallas_call(
        body, name=name,
        out_shape=(jax.ShapeDtypeStruct((s, d), F32), jax.ShapeDtypeStruct((s, d), BF16),
                   jax.ShapeDtypeStruct((SUBLANES, d), F32)),
        grid=(s // tr,), in_specs=[_row_spec(tr, d)] * 3 + [_vec_spec(d)] * 3,
        out_specs=(_row_spec(tr, d), _row_spec(tr, d), _vec_spec(d, SUBLANES)),
        compiler_params=_cp("arbitrary"),
    )(x, f, target, gate, ln_g, ln_b)


def _resid_ln_bwd(name, du_n, dr_n, x, f, sc_n, gate, ln_g, ln_b, coef):
    s, d = x.shape
    tr = _tile(s, (256,))

    def body(du_ref, drn_ref, x_ref, f_ref, sc_ref, gate_ref, g_ref, b_ref, dr_ref, df_ref, acc_ref):
        @pl.when(pl.program_id(0) == 0)
        def _():
            acc_ref[...] = jnp.zeros_like(acc_ref)

        fv = f_ref[...]
        du = du_ref[...]
        r = ALPHA * x_ref[...] + (coef * gate_ref[...]) * fv
        xhat, rstd = _ln_stats(r)
        h = xhat * g_ref[...] + b_ref[...]
        dy = du * (1.0 + sc_ref[...]) + ALPHA * drn_ref[...]
        dr = _ln_bwd_core(dy, xhat, rstd, g_ref[...])
        dr_ref[...] = dr
        df_ref[...] = ((coef * gate_ref[...]) * dr).astype(BF16)
        acc_ref[ROW_DSC:ROW_DSC + 1, :] += _colsum(du * h)
        acc_ref[ROW_DSH:ROW_DSH + 1, :] += _colsum(du)
        acc_ref[ROW_DLN_G:ROW_DLN_G + 1, :] += _colsum(dy * xhat)
        acc_ref[ROW_DLN_B:ROW_DLN_B + 1, :] += _colsum(dy)
        acc_ref[ROW_DGATE:ROW_DGATE + 1, :] += _colsum((coef * dr) * fv)

    return pl.pallas_call(
        body, name=name,
        out_shape=(jax.ShapeDtypeStruct((s, d), F32), jax.ShapeDtypeStruct((s, d), BF16),
                   jax.ShapeDtypeStruct((SUBLANES, d), F32)),
        grid=(s // tr,), in_specs=[_row_spec(tr, d)] * 4 + [_vec_spec(d)] * 4,
        out_specs=(_row_spec(tr, d), _row_spec(tr, d), _vec_spec(d, SUBLANES)),
        compiler_params=_cp("arbitrary"),
    )(du_n, dr_n, x, f, sc_n, gate, ln_g, ln_b)


def _input_grad(name, du, dr, x, sc):
    s, d = x.shape
    tr = _tile(s, (256,))

    def body(du_ref, dr_ref, x_ref, sc_ref, gx_ref, acc_ref):
        @pl.when(pl.program_id(0) == 0)
        def _():
            acc_ref[...] = jnp.zeros_like(acc_ref)

        du = du_ref[...]
        gx_ref[...] = du * (1.0 + sc_ref[...]) + ALPHA * dr_ref[...]
        acc_ref[ROW_DSC:ROW_DSC + 1, :] += _colsum(du * x_ref[...])
        acc_ref[ROW_DSH:ROW_DSH + 1, :] += _colsum(du)

    return pl.pallas_call(
        body, name=name,
        out_shape=(jax.ShapeDtypeStruct((s, d), F32), jax.ShapeDtypeStruct((SUBLANES, d), F32)),
        grid=(s // tr,), in_specs=[_row_spec(tr, d)] * 3 + [_vec_spec(d)],
        out_specs=(_row_spec(tr, d), _vec_spec(d, SUBLANES)), compiler_params=_cp("arbitrary"),
    )(du, dr, x, sc)


def _ffn_in_fwd(name, u, w_in, stages=None):
    s, d = u.shape
    cs = w_in.shape[2]
    f = 2 * cs
    tm, tn = _tile(s, (2048, 1024, 512)), _tile(cs, (256, 128))
    nb = f // tn
    nbs = cs // tn

    def body(u_ref, wa_ref, wb_ref, ab_ref, act_ref):
        for rows in _strips(tm, 512):
            uv = u_ref[rows, :]
            a = _dot(uv, wa_ref[...])
            b = _dot(uv, wb_ref[...])
            sg = _sigmoid(a)
            silu = a * sg
            ab_ref[0, rows, :] = (b * (sg + silu * (1.0 - sg))).astype(BF16)
            ab_ref[1, rows, :] = silu.astype(BF16)
            act_ref[rows, :] = (silu * b).astype(BF16)

    return _pcall(
        body, stages, name=name,
        out_shape=(jax.ShapeDtypeStruct((2, s, f), BF16), jax.ShapeDtypeStruct((s, f), BF16)),
        grid=(s // tm, nb),
        in_specs=[pl.BlockSpec((tm, d), lambda i, j: (i, 0)),
                  pl.BlockSpec((None, d, tn), lambda i, j: (j // nbs, 0, j % nbs)),
                  pl.BlockSpec((None, d, tn), lambda i, j: (2 + j // nbs, 0, j % nbs))],
        out_specs=(pl.BlockSpec((2, tm, tn), lambda i, j: (0, i, j)), pl.BlockSpec((tm, tn), lambda i, j: (i, j))),
        compiler_params=_cp("arbitrary", "arbitrary"),
    )(u, w_in, w_in)


def _ffn_out_bwd(name, df, w_out, ab, stages=None):
    s, d = df.shape
    f = w_out.shape[0]
    tm, tn = _tile(s, (1024, 512)), _tile(f, (512, 256, 128))

    def body(df_ref, w_ref, ab_ref, dab_ref):
        dfv = df_ref[...]
        for cols in _strips(tn):
            dact = _dot(dfv, w_ref[cols, :], "nt")
            dab_ref[0, :, cols] = (dact * ab_ref[0, :, cols].astype(F32)).astype(BF16)
            dab_ref[1, :, cols] = (dact * ab_ref[1, :, cols].astype(F32)).astype(BF16)

    return _pcall(
        body, stages, name=name, out_shape=jax.ShapeDtypeStruct((2, s, f), BF16), grid=(s // tm, f // tn),
        in_specs=[pl.BlockSpec((tm, d), lambda i, j: (i, 0)), pl.BlockSpec((tn, d), lambda i, j: (j, 0)),
                  pl.BlockSpec((2, tm, tn), lambda i, j: (0, i, j))],
        out_specs=pl.BlockSpec((2, tm, tn), lambda i, j: (0, i, j)),
        compiler_params=_cp("arbitrary", "arbitrary"),
    )(df, w_out, ab)


def _ffn_forward(tag, u, plan):
    w_in, w_out = plan.weight(f"{tag}_w_in"), plan.weight(f"{tag}_w_out")
    s, d = u.shape
    f = w_out.shape[0]
    ab, act = plan.host(f"{tag}_in_fwd", lambda st: _ffn_in_fwd(f"{tag}_in_fwd", u, w_in, st))
    out = plan.host(f"{tag}_out_fwd", lambda st: _mm(
        f"{tag}_out_fwd", "nn", act, w_out, (s, d, f), tm=_tile(s, (1024,)), tn=_tile(d, (1024,)),
        tk=_tile(f, (2816, 1408, 512, 128)), stages=st))
    return out, (ab, act)


def _ffn_backward(tag, df, u, saved, plan, in_first):
    w_in, w_out = plan.weight(f"{tag}_w_in"), plan.weight(f"{tag}_w_out")
    ab, act = saved
    s, d = u.shape
    f = w_out.shape[0]
    dab = plan.host(f"{tag}_out_bwd", lambda st: _ffn_out_bwd(f"{tag}_out_bwd", df, w_out, ab, st))
    cs = w_in.shape[2]
    tk = _tile(cs, (2816, 1408, 256, 128))
    nkh, nks = f // tk, cs // tk
    tmd = _tile(d, (1024,))
    tks = _tile(s, LONG_K)

    def dw_in():
        tw = _tile(cs, (256, 128))
        nwh, nws = f // tw, cs // tw
        plan.grad(f"{tag}_w_in", plan.host(f"{tag}_dw_in", lambda st: _mm(
            f"{tag}_dw_in", "nn", u.T, dab, (d, 2 * f, s), tm=tmd, tn=tw, tk=s,
            b_spec=pl.BlockSpec((None, s, tw), lambda i, j, kk: (j // nwh, 0, j % nwh)), out_shape=(N_CHIPS, d, cs),
            o_spec=pl.BlockSpec((None, tmd, tw), lambda i, j, kk: (j // nws, i, j % nws)), stages=st)))

    def dw_out():
        plan.grad(f"{tag}_w_out", plan.host(f"{tag}_dw_out", lambda st: _mm(
            f"{tag}_dw_out", "tn", act, df, (f, d, s), tm=_tile(f, (1408, 512, 128)), tn=tmd, tk=tks, stages=st)))

    for step in ((dw_in, dw_out) if in_first else (dw_out, dw_in)):
        step()
    return plan.host(f"{tag}_du", lambda st: _mm(
        f"{tag}_du", "nt", dab, w_in, (s, d, 2 * f), tm=_tile(s, (1024,)), tn=tmd, tk=tk,
        a_spec=pl.BlockSpec((None, _tile(s, (1024,)), tk), lambda i, j, kk: (kk // nkh, i, kk % nkh)),
        b_spec=pl.BlockSpec((None, tmd, tk), lambda i, j, kk: (kk // nks, j, kk % nks)), stages=st))


ATTN_Q = 4 * CHUNK
ATTN_W = ATTN_Q + A_PAD


def _band_bias(bias):
    n = ATTN_Q // CHUNK
    rows = [jnp.pad(bias, ((0, 0), (0, 0), (i * CHUNK, (n - 1 - i) * CHUNK)), constant_values=NEG_BIG)
            for i in range(n)]
    return jnp.concatenate(rows, axis=1)


def _band_bias_grad(dband):
    n = ATTN_Q // CHUNK
    parts = [dband[:, i * CHUNK:(i + 1) * CHUNK, i * CHUNK:i * CHUNK + A_BAND] for i in range(n)]
    return functools.reduce(jnp.add, parts)


def _attn_probs(q, kw, bias, key0):
    sc = _dot(q, kw, "nt") * (A_HEAD_DIM ** -0.5) + bias
    ks = lax.broadcasted_iota(jnp.int32, sc.shape, 1)
    sc = jnp.where(key0 + ks >= 0, sc, NEG_BIG)
    p = jnp.exp(sc - jnp.max(sc, axis=-1, keepdims=True))
    return p / jnp.sum(p, axis=-1, keepdims=True)


def _head_masks():
    lane = lax.broadcasted_iota(jnp.int32, (1, LANES), 1)
    return [lane // A_HEAD_DIM == h for h in range(LANES // A_HEAD_DIM)]


def _attn_fwd(p1, kvp, band, stages=None):
    s = p1.shape[0]
    aw = A_HEADS * A_HEAD_DIM
    nblk = aw // LANES
    hpb = LANES // A_HEAD_DIM
    assert s % ATTN_Q == 0

    def body(q_ref, k_ref, v_ref, b_ref, o_ref):
        base = pl.multiple_of(pl.program_id(1) * ATTN_Q, ATTN_Q)
        qv = q_ref[...]
        kw = k_ref[pl.ds(base, ATTN_W), :]
        vw = v_ref[pl.ds(base, ATTN_W), :]
        out = jnp.zeros((ATTN_Q, LANES), F32)
        for h, mask in enumerate(_head_masks()):
            p = _attn_probs(jnp.where(mask, qv, jnp.zeros_like(qv)), kw, b_ref[h], base - A_PAD)
            out = jnp.where(mask, _dot(p.astype(BF16), vw), out)
        o_ref[...] = out.astype(BF16)

    kv_rows = s + A_PAD
    return _pcall(
        body, stages, name="attn_fwd", out_shape=jax.ShapeDtypeStruct((s, aw), BF16), grid=(nblk, s // ATTN_Q),
        in_specs=[pl.BlockSpec((ATTN_Q, LANES), lambda b, i: (i, b)),
                  pl.BlockSpec((kv_rows, LANES), lambda b, i: (0, b)),
                  pl.BlockSpec((kv_rows, LANES), lambda b, i: (0, nblk + b)),
                  pl.BlockSpec((hpb, ATTN_Q, ATTN_W), lambda b, i: (b, 0, 0))],
        out_specs=pl.BlockSpec((ATTN_Q, LANES), lambda b, i: (i, b)),
        compiler_params=_cp("arbitrary", "arbitrary"),
    )(p1, kvp, kvp, band)


def _attn_bwd(p1, kvp, band, dya, stages=None):
    s = p1.shape[0]
    aw = A_HEADS * A_HEAD_DIM
    nblk = aw // LANES
    hpb = LANES // A_HEAD_DIM
    scale = A_HEAD_DIM ** -0.5

    def body(q_ref, k_ref, v_ref, b_ref, do_ref, dq_ref, dk_ref, dv_ref, db_ref):
        @pl.when(pl.program_id(1) == 0)
        def _():
            dk_ref[...] = jnp.zeros_like(dk_ref)
            dv_ref[...] = jnp.zeros_like(dv_ref)
            db_ref[...] = jnp.zeros_like(db_ref)

        base = pl.multiple_of(pl.program_id(1) * ATTN_Q, ATTN_Q)
        window = pl.ds(base, ATTN_W)
        kw = k_ref[window, :]
        vw = v_ref[window, :]
        qv = q_ref[...]
        dov = do_ref[...]
        dq = jnp.zeros((ATTN_Q, LANES), F32)
        dk = jnp.zeros((ATTN_W, LANES), F32)
        dv = jnp.zeros((ATTN_W, LANES), F32)
        for h, mask in enumerate(_head_masks()):
            qh = jnp.where(mask, qv, jnp.zeros_like(qv))
            doh = jnp.where(mask, dov, jnp.zeros_like(dov))
            p = _attn_probs(qh, kw, b_ref[h], base - A_PAD)
            dp = _dot(doh, vw, "nt")
            ds = p * (dp - jnp.sum(p * dp, axis=-1, keepdims=True))
            db_ref[h] += ds
            dsb = (ds * scale).astype(BF16)
            dq = jnp.where(mask, _dot(dsb, kw), dq)
            dk = dk + _dot(dsb, qh, "tn")
            dv = dv + _dot(p.astype(BF16), doh, "tn")
        dq_ref[...] = dq.astype(BF16)
        dk_ref[window, :] += dk
        dv_ref[window, :] += dv

    kv_rows = s + A_PAD
    q_spec = pl.BlockSpec((ATTN_Q, LANES), lambda b, i: (i, b))
    acc_spec = pl.BlockSpec((kv_rows, LANES), lambda b, i: (0, b))
    b_spec = pl.BlockSpec((hpb, ATTN_Q, ATTN_W), lambda b, i: (b, 0, 0))
    return _pcall(
        body, stages, name="attn_bwd",
        out_shape=(jax.ShapeDtypeStruct((s, aw), BF16), jax.ShapeDtypeStruct((kv_rows, aw), F32),
                   jax.ShapeDtypeStruct((kv_rows, aw), F32), jax.ShapeDtypeStruct((A_HEADS, ATTN_Q, ATTN_W), F32)),
        grid=(nblk, s // ATTN_Q),
        in_specs=[q_spec, acc_spec, pl.BlockSpec((kv_rows, LANES), lambda b, i: (0, nblk + b)), b_spec, q_spec],
        out_specs=(q_spec, acc_spec, acc_spec, b_spec), compiler_params=_cp("arbitrary", "arbitrary"),
    )(p1, kvp, kvp, band, dya)


def _rel_onehot():
    qi = jnp.arange(CHUNK)[:, None]
    ks = jnp.arange(A_BAND)[None, :]
    idx = (jnp.clip(ks - A_PAD - qi, -REL_CLIP, CHUNK - 1) + REL_CLIP).reshape(1, CHUNK * A_BAND)
    return (jnp.arange(REL_SIZE)[:, None] == idx).astype(F32)


def _gla_gate(lr, wa2, balpha):
    z = _dot(lr, wa2) + balpha
    la = (jnp.minimum(z, 0.0) - jnp.log(1.0 + jnp.exp(-jnp.abs(z)))) * (1.0 / GATE_TAU)
    row = lax.broadcasted_iota(jnp.int32, (CHUNK, CHUNK), 0)
    col = lax.broadcasted_iota(jnp.int32, (CHUNK, CHUNK), 1)
    cum = _dot((row >= col).astype(F32), la, precision=HIGHEST)
    return z, la, cum


def _gla_dims(p2):
    kd = p2.shape[1] // 6
    hk = kd // B_HEADS
    hv = 2 * hk
    return kd, hk, hv


GLA_CPS = 4


def _gla_fwd(p2, lrp, wa2p, balpha, gnorm, stages=None):
    s = p2.shape[0]
    kd, hk, hv = _gla_dims(p2)
    nc = s // CHUNK
    cps = GLA_CPS if nc % GLA_CPS == 0 else 1
    rows_per = cps * CHUNK
    qscale = hk ** -0.5

    def body(p_ref, lr_ref, wa_ref, ba_ref, gn_ref, yb_ref, st_ref, state):
        @pl.when(pl.program_id(0) == 0)
        def _():
            state[...] = jnp.zeros_like(state)

        gn = gn_ref[...]
        for sub in range(cps):
            rows = slice(sub * CHUNK, (sub + 1) * CHUNK)
            _, _, cum = _gla_gate(lr_ref[rows, :], wa_ref[...], ba_ref[...])
            last = cum[CHUNK - 1:CHUNK, :]
            e = jnp.exp(last - cum)
            dch = jnp.exp(last)
            for hh in range(B_HEADS):
                ks = slice(hh * hk, (hh + 1) * hk)
                q = p_ref[rows, hh * hk:(hh + 1) * hk].astype(F32)
                k = p_ref[rows, kd + hh * hk:kd + (hh + 1) * hk].astype(F32)
                v = p_ref[rows, 2 * kd + hh * hv:2 * kd + (hh + 1) * hv]
                rg = p_ref[rows, 4 * kd + hh * hv:4 * kd + (hh + 1) * hv].astype(F32)
                kdec = (k * e[:, ks]).astype(BF16)
                st = state[hh] * dch[:, ks] + _dot(v, kdec, "tn")
                state[hh] = st
                st_ref[sub, hh] = st
                o = _dot((q * qscale).astype(BF16), st.astype(BF16), "nt")
                rinv = lax.rsqrt(jnp.mean(o * o, axis=-1, keepdims=True) + RMS_EPS)
                yb_ref[rows, hh * hv:(hh + 1) * hv] = ((o * rinv * gn) * (rg * _sigmoid(rg))).astype(BF16)

    return _pcall(
        body, stages, name="gla_fwd",
        out_shape=(jax.ShapeDtypeStruct((s, 2 * kd), BF16), jax.ShapeDtypeStruct((nc, B_HEADS, hv, hk), F32)),
        grid=(nc // cps,),
        in_specs=[pl.BlockSpec((rows_per, 6 * kd), lambda i: (i, 0)), pl.BlockSpec((rows_per, LANES), lambda i: (i, 0)),
                  pl.BlockSpec((LANES, kd), lambda i: (0, 0)), pl.BlockSpec((1, kd), lambda i: (0, 0)),
                  pl.BlockSpec((1, hv), lambda i: (0, 0))],
        out_specs=(pl.BlockSpec((rows_per, 2 * kd), lambda i: (i, 0)),
                   pl.BlockSpec((cps, B_HEADS, hv, hk), lambda i: (i, 0, 0, 0))),
        scratch_shapes=[pltpu.VMEM((B_HEADS, hv, hk), F32)], compiler_params=_cp("arbitrary"),
    )(p2, lrp, wa2p, balpha, gnorm)


GLA_ROW_DBALPHA, GLA_ROW_DGNORM = 0, 1


def _gla_bwd(p2, lrp, wa2p, balpha, gnorm, states, dyb, stages=None):
    s = p2.shape[0]
    kd, hk, hv = _gla_dims(p2)
    nc = s // CHUNK
    cps = GLA_CPS if nc % GLA_CPS == 0 else 1
    rows_per = cps * CHUNK
    nblk = nc // cps
    qscale = hk ** -0.5

    def body(p_ref, lr_ref, wa_ref, ba_ref, gn_ref, st_ref, sp_ref, dy_ref, dp_ref, dz_ref, sm_ref, gcar):
        i = pl.program_id(0)

        @pl.when(i == 0)
        def _():
            gcar[...] = jnp.zeros_like(gcar)
            sm_ref[...] = jnp.zeros_like(sm_ref)

        block_has_prev = (i < nblk - 1).astype(F32)
        gn = gn_ref[...]
        row = lax.broadcasted_iota(jnp.int32, (CHUNK, CHUNK), 0)
        col = lax.broadcasted_iota(jnp.int32, (CHUNK, CHUNK), 1)
        tri_strict = (row > col).astype(F32)
        for sub in reversed(range(cps)):
            rows = slice(sub * CHUNK, (sub + 1) * CHUNK)
            z, _, cum = _gla_gate(lr_ref[rows, :], wa_ref[...], ba_ref[...])
            last = cum[CHUNK - 1:CHUNK, :]
            e = jnp.exp(last - cum)
            dch = jnp.exp(last)
            sgn = _sigmoid(-z) * (1.0 / GATE_TAU)
            for hh in range(B_HEADS):
                ks = slice(hh * hk, (hh + 1) * hk)
                q = p_ref[rows, hh * hk:(hh + 1) * hk].astype(F32)
                k = p_ref[rows, kd + hh * hk:kd + (hh + 1) * hk].astype(F32)
                v = p_ref[rows, 2 * kd + hh * hv:2 * kd + (hh + 1) * hv]
                rg = p_ref[rows, 4 * kd + hh * hv:4 * kd + (hh + 1) * hv].astype(F32)
                kdecf = k * e[:, ks]
                kdec = kdecf.astype(BF16)
                st16 = st_ref[sub, hh].astype(BF16)
                prev = st_ref[sub - 1, hh] if sub > 0 else sp_ref[hh] * block_has_prev
                qs = (q * qscale).astype(BF16)
                o = _dot(qs, st16, "nt")
                rinv = lax.rsqrt(jnp.mean(o * o, axis=-1, keepdims=True) + RMS_EPS)
                dy = dy_ref[rows, hh * hv:(hh + 1) * hv].astype(F32)
                sg = _sigmoid(rg)
                onorm = o * rinv
                drg = dy * (onorm * gn) * (sg * (1.0 + rg * (1.0 - sg)))
                dob = dy * (rg * sg)
                sm_ref[GLA_ROW_DGNORM:GLA_ROW_DGNORM + 1, 0:hv] += _colsum(dob * onorm)
                t = dob * gn
                do = rinv * (t - onorm * jnp.mean(t * onorm, axis=-1, keepdims=True))
                do16 = do.astype(BF16)
                dq = _dot(do16, st16) * qscale
                gt = _dot(do16, qs, "tn") + gcar[hh]
                gcar[hh] = gt * dch[:, ks]
                dd = _colsum(gt * prev)
                gt16 = gt.astype(BF16)
                dkdec = _dot(v, gt16)
                dv = _dot(kdec, gt16, "nt")
                dla = dd * dch[:, ks] + _dot(tri_strict, dkdec * kdecf, precision=HIGHEST)
                dzh = dla * sgn[:, ks]
                sm_ref[GLA_ROW_DBALPHA:GLA_ROW_DBALPHA + 1, hh * hk:(hh + 1) * hk] += _colsum(dzh)
                dz_ref[rows, hh * hk:(hh + 1) * hk] = dzh.astype(BF16)
                dp_ref[rows, hh * hk:(hh + 1) * hk] = dq.astype(BF16)
                dp_ref[rows, kd + hh * hk:kd + (hh + 1) * hk] = (dkdec * e[:, ks]).astype(BF16)
                dp_ref[rows, 2 * kd + hh * hv:2 * kd + (hh + 1) * hv] = dv.astype(BF16)
                dp_ref[rows, 4 * kd + hh * hv:4 * kd + (hh + 1) * hv] = drg.astype(BF16)

    rev = lambda i: (nblk - 1 - i, 0)
    return _pcall(
        body, stages, name="gla_bwd",
        out_shape=(jax.ShapeDtypeStruct((s, 6 * kd), BF16), jax.ShapeDtypeStruct((s, kd), BF16),
                   jax.ShapeDtypeStruct((SUBLANES, kd), F32)),
        grid=(nblk,),
        in_specs=[pl.BlockSpec((rows_per, 6 * kd), rev), pl.BlockSpec((rows_per, LANES), rev),
                  pl.BlockSpec((LANES, kd), lambda i: (0, 0)), pl.BlockSpec((1, kd), lambda i: (0, 0)),
                  pl.BlockSpec((1, hv), lambda i: (0, 0)),
                  pl.BlockSpec((cps, B_HEADS, hv, hk), lambda i: (nblk - 1 - i, 0, 0, 0)),
                  pl.BlockSpec((None, B_HEADS, hv, hk), lambda i: (jnp.maximum((nblk - 1 - i) * cps - 1, 0), 0, 0, 0)),
                  pl.BlockSpec((rows_per, 2 * kd), rev)],
        out_specs=(pl.BlockSpec((rows_per, 6 * kd), rev), pl.BlockSpec((rows_per, kd), rev),
                   pl.BlockSpec((SUBLANES, kd), lambda i: (0, 0))),
        scratch_shapes=[pltpu.VMEM((B_HEADS, hv, hk), F32)], compiler_params=_cp("arbitrary"),
    )(p2, lrp, wa2p, balpha, gnorm, states, states, dyb)


def _merge_fwd(ya, yb, wpa, wpb, g):
    s, ka = ya.shape
    kb = yb.shape[1]
    d = wpa.shape[1]
    tm, tn = _tile(s, (1024, 512)), _tile(d, (512,))

    def body(ya_ref, yb_ref, wa_ref, wb_ref, g_ref, m_ref, pab_ref):
        yav, ybv = ya_ref[...], yb_ref[...]
        for cols in _strips(tn):
            pa = _dot(yav, wa_ref[:, cols])
            pb = _dot(ybv, wb_ref[:, cols])
            m_ref[:, cols] = (_sigmoid(g_ref[0, :, cols].astype(F32)) * pa
                              + _sigmoid(g_ref[1, :, cols].astype(F32)) * pb).astype(BF16)
            pab_ref[0, :, cols] = pa.astype(BF16)
            pab_ref[1, :, cols] = pb.astype(BF16)

    st = pl.BlockSpec((2, tm, tn), lambda i, j: (0, i, j))
    return pl.pallas_call(
        body, name="merge_fwd",
        out_shape=(jax.ShapeDtypeStruct((s, d), BF16), jax.ShapeDtypeStruct((2, s, d), BF16)),
        grid=(s // tm, d // tn),
        in_specs=[pl.BlockSpec((tm, ka), lambda i, j: (i, 0)), pl.BlockSpec((tm, kb), lambda i, j: (i, 0)),
                  pl.BlockSpec((ka, tn), lambda i, j: (0, j)), pl.BlockSpec((kb, tn), lambda i, j: (0, j)), st],
        out_specs=(pl.BlockSpec((tm, tn), lambda i, j: (i, j)), st),
        compiler_params=_cp("parallel", "parallel"),
    )(ya, yb, wpa, wpb, g)


def _merge_bwd(dm, wmo, g, pab, stages=None):
    s, d = dm.shape
    tm, tn = _tile(s, (1024, 512)), _tile(d, (512,))

    def body(dm_ref, w_ref, g_ref, pab_ref, dpab_ref, dg_ref):
        dmv = dm_ref[...]
        for cols in _strips(tn):
            dmg = _dot(dmv, w_ref[cols, :], "nt")
            for j in range(2):
                sg = _sigmoid(g_ref[j, :, cols].astype(F32))
                dpab_ref[j, :, cols] = (dmg * sg).astype(BF16)
                dg_ref[j, :, cols] = (dmg * pab_ref[j, :, cols].astype(F32) * (sg * (1.0 - sg))).astype(BF16)

    st = pl.BlockSpec((2, tm, tn), lambda i, j: (0, i, j))
    return _pcall(
        body, stages, name="merge_bwd",
        out_shape=(jax.ShapeDtypeStruct((2, s, d), BF16), jax.ShapeDtypeStruct((2, s, d), BF16)),
        grid=(s // tm, d // tn),
        in_specs=[pl.BlockSpec((tm, d), lambda i, j: (i, 0)), pl.BlockSpec((tn, d), lambda i, j: (j, 0)), st, st],
        out_specs=(st, st), compiler_params=_cp("arbitrary", "arbitrary"),
    )(dm, wmo, g, pab)


def _virtual_rows(parts, lo, hi):
    out, off = [], 0
    for p in parts:
        a, b = max(lo, off), min(hi, off + p.shape[0])
        if a < b:
            out.append(p[a - off:b - off])
        off += p.shape[0]
    return out[0] if len(out) == 1 else jnp.concatenate(out, axis=0)


def _mix_in_row_groups(d):
    o1 = 3 * A_HEADS * A_HEAD_DIM
    o2 = o1 + 6 * (d // 4)
    o3 = o2 + GATE_RANK
    return (0, o1), (o1, o2), (o2, o3), (o3, o3 + 2 * d)


def _split_mix_in(stacked):
    d = stacked.shape[2]
    flat = stacked.reshape(-1, d)
    _, _, (lo, hi), (glo, ghi) = _mix_in_row_groups(d)
    return flat, jnp.pad(flat[lo:hi], ((0, LANES - GATE_RANK), (0, 0))), flat[glo:ghi]


MIX_TILE = 1024


def _mix_in_weights(plan):
    return plan.memo("mix_in_weights", lambda: _split_mix_in(plan.weight("w_mix_in")))


def _hosted_mm(plan):
    return lambda name, *a, **k: plan.host(name, lambda st: _mm(name, *a, stages=st, **k))


def _mix_forward(u2, plan, small):
    s, d = u2.shape
    wt, wt_lr, wt_g = _mix_in_weights(plan)
    bias, wa2p, balpha, gnorm = small
    mm = _hosted_mm(plan)
    aw = A_HEADS * A_HEAD_DIM
    tm, tn = _tile(s, (1024,)), MIX_TILE
    (_, na), (_, nab) = _mix_in_row_groups(d)[:2]
    assert na % tn == 0 and nab % tn == 0
    p1 = mm("mix_in_a", "nt", u2, wt, (s, na, d), tm=tm, tn=tn, tk=d, out_dtype=BF16)
    p2 = mm("mix_in_b", "nt", u2, wt, (s, nab - na, d), tm=tm, tn=tn, tk=d, out_dtype=BF16,
            b_spec=pl.BlockSpec((tn, d), lambda i, j, kk: (na // tn + j, 0)))
    lrp = mm("mix_in_lr", "nt", u2, wt_lr, (s, LANES, d), tm=tm, tn=LANES, tk=d, out_dtype=BF16)
    nbg = d // tn
    g = mm("mix_in_g", "nt", u2, wt_g, (s, 2 * d, d), tm=tm, tn=tn, tk=d, out_dtype=BF16, out_shape=(2, s, d),
           o_spec=pl.BlockSpec((None, tm, tn), lambda i, j, kk: (j // nbg, i, j % nbg)))
    kvp = jnp.pad(p1[:, aw:], ((A_PAD, 0), (0, 0)))
    ya = plan.host("attn_fwd", lambda st: _attn_fwd(p1, kvp, bias, st))
    yb, states = plan.host("gla_fwd", lambda st: _gla_fwd(p2, lrp, wa2p, balpha, gnorm, st))
    merged, pab = _merge_fwd(ya, yb, plan.weight("w_proj_a"), plan.weight("w_proj_b"), g)
    m = mm("mix_out", "nn", merged, plan.weight("w_mix_out"), (s, d, d), tm=tm, tn=tn, tk=d)
    return m, (p1, kvp, p2, lrp, states, ya, yb, g, pab, merged)


def _mix_backward(dm, u2, saved, plan, small):
    s, d = u2.shape
    wt, wt_lr, wt_g = _mix_in_weights(plan)
    wpa, wpb, wmo = plan.weight("w_proj_a"), plan.weight("w_proj_b"), plan.weight("w_mix_out")
    bias, wa2p, balpha, gnorm = small
    p1, kvp, p2, lrp, states, ya, yb, g, pab, merged = saved
    mm = _hosted_mm(plan)
    aw = A_HEADS * A_HEAD_DIM
    kd = d // 4
    t = MIX_TILE
    tm = _tile(s, (1024,))
    tks = _tile(s, LONG_K)

    plan.grad("w_mix_out", mm("mix_dw_out", "tn", merged, dm, (d, d, s), tm=t, tn=t, tk=tks))
    dpab, dg = plan.host("merge_bwd", lambda st: _merge_bwd(dm, wmo, g, pab, st))
    sel = lambda j: pl.BlockSpec((None, tm, d), lambda i, jj, kk: (j, i, 0))
    dya = mm("mix_dya", "nt", dpab, wpa, (s, aw, d), tm=tm, tn=t, tk=d, out_dtype=BF16, a_spec=sel(0))
    dyb = mm("mix_dyb", "nt", dpab, wpb, (s, 2 * kd, d), tm=tm, tn=t, tk=d, out_dtype=BF16, a_spec=sel(1))
    selk = lambda j: pl.BlockSpec((None, tks, t), lambda i, jj, kk: (j, kk, jj))
    plan.grad("w_proj_a", mm("mix_dwpa", "tn", ya, dpab, (aw, d, s), tm=t, tn=t, tk=tks, b_spec=selk(0)))
    plan.grad("w_proj_b", mm("mix_dwpb", "tn", yb, dpab, (2 * kd, d, s), tm=t, tn=t, tk=tks, b_spec=selk(1)))

    dq, dkp, dvp, dbias = plan.host("attn_bwd", lambda st: _attn_bwd(p1, kvp, bias, dya, st))
    dp1 = jnp.concatenate([dq, dkp[A_PAD:].astype(BF16), dvp[A_PAD:].astype(BF16)], axis=1)
    dp2, dz, gsm = plan.host("gla_bwd", lambda st: _gla_bwd(p2, lrp, wa2p, balpha, gnorm, states, dyb, st))
    dlrp = mm("gla_dlr", "nt", dz, wa2p, (s, LANES, kd), tm=tm, tn=LANES, tk=kd, out_dtype=BF16)
    dwa2p = mm("gla_dwa2", "tn", lrp, dz, (LANES, kd, s), tm=LANES, tn=kd, tk=tks)

    tka = 3 * aw
    assert 6 * kd == tka
    du = mm("mix_du_a", "nn", dp1, wt, (s, d, tka), tm=tm, tn=t, tk=tka)
    du = mm("mix_du_b", "nn", dp2, wt, (s, d, tka), tm=tm, tn=t, tk=tka, add=du,
            b_spec=pl.BlockSpec((tka, t), lambda i, j, kk: (1 + kk, j)))
    du = mm("mix_du_g", "nn", dg, wt_g, (s, d, 2 * d), tm=tm, tn=t, tk=d, add=du, thin=(dlrp, wt_lr),
            a_spec=pl.BlockSpec((None, tm, d), lambda i, j, kk: (kk, i, 0)))
    nkg = d // t
    dw1 = mm("mix_dw_a", "tn", dp1, u2, (3 * aw, d, s), tm=t, tn=t, tk=tks)
    dw2 = mm("mix_dw_b", "tn", dp2, u2, (6 * kd, d, s), tm=t, tn=t, tk=tks)
    dwlr = mm("mix_dw_lr", "tn", dlrp, u2, (LANES, d, s), tm=LANES, tn=t, tk=tks)
    dwg = mm("mix_dw_g", "tn", dg, u2, (2 * d, d, s), tm=t, tn=t, tk=tks,
             a_spec=pl.BlockSpec((None, tks, t), lambda i, j, kk: (i // nkg, kk, i % nkg)))
    pieces = [dw1, dw2, dwlr[:GATE_RANK], dwg]
    shard_rows = sum(p.shape[0] for p in pieces) // N_CHIPS
    plan.grad("w_mix_in", jnp.stack([_virtual_rows(pieces, j * shard_rows, (j + 1) * shard_rows)
                                     for j in range(N_CHIPS)]))
    return du, (dbias, dwa2p[:GATE_RANK], gsm)


def _device_step(x, target, mod, small, plan):
    s, d = x.shape
    row = lambda i: mod[i:i + 1]
    sh1, sc1, g1, sh2, sc2, g2, sh3, sc3, g3 = (row(i) for i in range(N_MOD))

    onehot = _rel_onehot()
    bias = _mm("rel_bias_expand", "nn", small["rel_bias"], onehot, (A_HEADS, CHUNK * A_BAND, REL_SIZE),
               tm=A_HEADS, tn=4608, tk=REL_SIZE, precision=HIGHEST).reshape(A_HEADS, CHUNK, A_BAND)
    bias = _band_bias(bias)
    wa2p = jnp.pad(small["w_alpha2"], ((0, LANES - GATE_RANK), (0, 0))).astype(BF16)
    mix_small = (bias, wa2p, small["b_alpha"], small["gla_norm_g"])

    u1 = _modulate("mod1", x, sh1, sc1)
    f1, sv1 = _ffn_forward("ffn1", u1, plan)
    h1, u2 = _resid_ln_fwd("ln1_fwd", x, f1, g1, small["ln1_g"], small["ln1_b"], sh2, sc2, 0.5)
    m, svm = _mix_forward(u2, plan, mix_small)
    h2, u3 = _resid_ln_fwd("ln2_fwd", h1, m, g2, small["ln2_g"], small["ln2_b"], sh3, sc3, 1.0)
    f2, sv2 = _ffn_forward("ffn2", u3, plan)

    dr3, df2, acc3 = _final_ln_loss_bwd("ln3_loss_bwd", h2, f2, target, g3, small["ln3_g"], small["ln3_b"], 0.5)
    du3 = _ffn_backward("ffn2", df2, u3, sv2, plan, in_first=False)
    dr2, dmx, acc2 = _resid_ln_bwd("ln2_bwd", du3, dr3, h1, m, sc3, g2, small["ln2_g"], small["ln2_b"], 1.0)
    du2, (dbias, dwa2, gsm) = _mix_backward(dmx, u2, svm, plan, mix_small)
    dr1, df1, acc1 = _resid_ln_bwd("ln1_bwd", du2, dr2, x, f1, sc2, g1, small["ln1_g"], small["ln1_b"], 0.5)
    du1 = _ffn_backward("ffn1", df1, u1, sv1, plan, in_first=True)
    grad_x, acc0 = _input_grad("input_grad", du1, dr1, x, sc1)

    drel = _hosted_mm(plan)("rel_bias_grad", "nt", _band_bias_grad(dbias).reshape(A_HEADS, CHUNK * A_BAND), onehot,
                            (A_HEADS, REL_SIZE, CHUNK * A_BAND), tm=A_HEADS, tn=REL_SIZE, tk=4608, precision=HIGHEST)
    loss = jnp.sum(acc3[ROW_LOSS])
    dmod = jnp.stack([acc0[ROW_DSH], acc0[ROW_DSC], acc1[ROW_DGATE], acc1[ROW_DSH], acc1[ROW_DSC], acc2[ROW_DGATE],
                      acc2[ROW_DSH], acc2[ROW_DSC], acc3[ROW_DGATE]])
    kd = d // 4
    small_grads = dict(ln1_g=acc1[ROW_DLN_G], ln1_b=acc1[ROW_DLN_B], ln2_g=acc2[ROW_DLN_G], ln2_b=acc2[ROW_DLN_B],
                       ln3_g=acc3[ROW_DLN_G], ln3_b=acc3[ROW_DLN_B], b_alpha=gsm[GLA_ROW_DBALPHA],
                       gla_norm_g=gsm[GLA_ROW_DGNORM, :kd // B_HEADS * 2], rel_bias=drel, w_alpha2=dwa2)
    return loss, grad_x, small_grads, dmod


HBM_SPEC = pl.BlockSpec(memory_space=pl.ANY)


def _mesh_pos():
    return lax.axis_index("x"), lax.axis_index("y"), lax.axis_index("c")


def _other_chips(x, y):
    return [(1 - x, y), (x, 1 - y), (1 - x, 1 - y)]


def _remote(src, dst, send_sem, recv_sem, to):
    return pltpu.make_async_remote_copy(src_ref=src, dst_ref=dst, send_sem=send_sem, recv_sem=recv_sem,
                                        device_id=to, device_id_type=MESH)


def _allgather_rows(name, v):
    m_per, n = v.shape

    def body(x_ref, out_ref, send_sems, recv_sems, local_sem):
        x, y, c = _mesh_pos()
        me, sibling = (x, y, c), (x, y, 1 - c)
        chips = _other_chips(x, y)

        def rows(px, py, pc):
            return out_ref.at[pl.ds((4 * px + 2 * py + pc) * m_per, m_per), :]

        def copy(k, block, to, src=None):
            return _remote(rows(*block) if src is None else src, rows(*block), send_sems.at[k], recv_sems.at[k], to)

        mine = pltpu.make_async_copy(x_ref, rows(*me), local_sem)
        mine.start()
        first = [copy(0, me, sibling, src=x_ref)]
        first += [copy(1 + j, me, (*chip, c), src=x_ref) for j, chip in enumerate(chips)]
        for cp in first:
            cp.start()
        passed = [copy(4 + j, (*chip, c), sibling) for j, chip in enumerate(chips)]
        for j, chip in enumerate(chips):
            copy(1 + j, (*chip, c), me).wait_recv()
            passed[j].start()
        copy(0, sibling, me).wait_recv()
        for j, chip in enumerate(chips):
            copy(4 + j, (*chip, 1 - c), me).wait_recv()
        for cp in first + passed:
            cp.wait_send()
        mine.wait()

    return pl.pallas_call(
        body, name=name, out_shape=jax.ShapeDtypeStruct((N_DEV * m_per, n), v.dtype),
        in_specs=[pl.BlockSpec(memory_space=pltpu.VMEM)], out_specs=pl.BlockSpec(memory_space=pltpu.VMEM),
        scratch_shapes=[pltpu.SemaphoreType.DMA((7,)), pltpu.SemaphoreType.DMA((7,)), pltpu.SemaphoreType.DMA],
    )(v)


def _allgather_weights(bufs):
    n = len(bufs)
    TO_X, TO_Y, PASS_TO_X, PASS_TO_Y, SIB_X, SIB_Y, SIB_D0, SIB_D1 = range(8)

    def body(*refs):
        ins, outs = refs[:n], refs[n:2 * n]
        send_sems, recv_sems = refs[2 * n:]
        x, y, c = _mesh_pos()
        sibling = (x, y, 1 - c)
        xn, yn, dg = _other_chips(x, y)
        j0, jx, jy, jd = (2 * p[0] + p[1] for p in ((x, y), xn, yn, dg))
        sends = []

        def rows(w, hc, quarter=None):
            hr = bufs[w].shape[1] // 2
            if quarter is None:
                return pl.ds(hc * hr, hr)
            return pl.ds(hc * hr + quarter * (hr // 2), hr // 2)

        def push(src, dst, w, k, to):
            cp = _remote(src, dst, send_sems.at[w, k], recv_sems.at[w, k], to)
            cp.start()
            sends.append(cp)

        def landed(piece, w, k):
            _remote(piece, piece, send_sems.at[w, k], recv_sems.at[w, k], sibling).wait_recv()

        for w in range(n):
            mine = rows(w, c)
            push(ins[w].at[j0, mine, :], outs[w].at[j0, mine, :], w, TO_X, (*xn, c))
            push(ins[w].at[j0, mine, :], outs[w].at[j0, mine, :], w, TO_Y, (*yn, c))
        for w in range(n):
            half_x = outs[w].at[jx, rows(w, c), :]
            landed(half_x, w, TO_X)
            quarter = outs[w].at[jx, rows(w, c, 1), :]
            push(quarter, quarter, w, PASS_TO_Y, (*yn, c))
            push(half_x, half_x, w, SIB_X, sibling)
            half_y = outs[w].at[jy, rows(w, c), :]
            landed(half_y, w, TO_Y)
            quarter = outs[w].at[jy, rows(w, c, 0), :]
            push(quarter, quarter, w, PASS_TO_X, (*xn, c))
            push(half_y, half_y, w, SIB_Y, sibling)
        for w in range(n):
            for q, arrives_on, on in ((0, PASS_TO_X, SIB_D0), (1, PASS_TO_Y, SIB_D1)):
                piece = outs[w].at[jd, rows(w, c, q), :]
                landed(piece, w, arrives_on)
                push(piece, piece, w, on, sibling)
        for w in range(n):
            landed(outs[w].at[jx, rows(w, 1 - c), :], w, SIB_X)
            landed(outs[w].at[jy, rows(w, 1 - c), :], w, SIB_Y)
            landed(outs[w].at[jd, rows(w, 1 - c, 0), :], w, SIB_D0)
            landed(outs[w].at[jd, rows(w, 1 - c, 1), :], w, SIB_D1)
        for cp in sends:
            cp.wait_send()

    return pl.pallas_call(
        body, name="allgather_weights", out_shape=[jax.ShapeDtypeStruct(b.shape, b.dtype) for b in bufs],
        in_specs=[HBM_SPEC] * n, out_specs=[HBM_SPEC] * n, input_output_aliases={w: w for w in range(n)},
        scratch_shapes=[pltpu.SemaphoreType.DMA((n, 8)), pltpu.SemaphoreType.DMA((n, 8))],
    )(*bufs)


def _half(ref, hc, col, *lead):
    rows, cols = ref.shape[-2:]
    if col:
        return ref.at[(*lead, slice(None), pl.ds(hc * (cols // 2), cols // 2))]
    return ref.at[(*lead, pl.ds(hc * (rows // 2), rows // 2), slice(None))]


def _half_shape(shape, col):
    return shape[:-2] + ((shape[-2], shape[-1] // 2) if col else (shape[-2] // 2, shape[-1]))


def _quarter(ref, hc, q, col, *lead):
    rows, cols = ref.shape[-2:]
    if col:
        return ref.at[(*lead, slice(None), pl.ds(hc * (cols // 2) + q * (cols // 4), cols // 4))]
    return ref.at[(*lead, pl.ds(hc * (rows // 2) + q * (rows // 4), rows // 4), slice(None))]


def _stage_gather_ici(bufs, cols):
    n = len(bufs)
    TO_X, TO_Y, PASS_TO_X, PASS_TO_Y = range(4)

    def places():
        x, y, c = _mesh_pos()
        xn, yn, dg = _other_chips(x, y)
        return c, (*xn, c), (*yn, c), [2 * p[0] + p[1] for p in ((x, y), xn, yn, dg)]

    def remote(src, dst, send, recv, w, k, to):
        return _remote(src, dst, send.at[4 * w + k], recv.at[4 * w + k], to)

    def own(ins, outs, send, recv):
        c, to_x, to_y, (j0, _, _, _) = places()
        for w in range(n):
            for k, to in ((TO_X, to_x), (TO_Y, to_y)):
                yield remote(_half(ins[w], c, cols[w], j0), _half(outs[w], c, cols[w], j0), send, recv, w, k, to)

    def relays(ins, outs, send, recv):
        c, to_x, to_y, (_, jx, jy, _) = places()
        for w in range(n):
            for j, k, q, pass_k, to in ((jx, TO_X, 1, PASS_TO_Y, to_y), (jy, TO_Y, 0, PASS_TO_X, to_x)):
                half = _half(outs[w], c, cols[w], j)
                piece = _quarter(outs[w], c, q, cols[w], j)
                yield remote(half, half, send, recv, w, k, to), remote(piece, piece, send, recv, w, pass_k, to)

    def passed(ins, outs, send, recv):
        c, to_x, _, (_, _, _, jd) = places()
        for w in range(n):
            for q, k in ((0, PASS_TO_X), (1, PASS_TO_Y)):
                piece = _quarter(outs[w], c, q, cols[w], jd)
                yield remote(piece, piece, send, recv, w, k, to_x)

    def start(*refs):
        for cp in own(*refs):
            cp.start()

    def relay(*refs):
        for arrived, onward in relays(*refs):
            arrived.wait_recv()
            onward.start()

    def finish(*refs):
        for cp in passed(*refs):
            cp.wait_recv()
        for cp in own(*refs):
            cp.wait_send()
        for _, onward in relays(*refs):
            onward.wait_send()

    outs = [jax.ShapeDtypeStruct(b.shape, b.dtype) for b in bufs]
    return _Stage(bufs, outs, 4 * n, start, finish, aliases={w: w for w in range(n)}, relay=relay)


def _stage_gather_d2d(partial, cols):
    n = len(partial)

    def copies(ins, outs, send, recv):
        x, y, c = _mesh_pos()
        for w in range(n):
            for r, chip in enumerate(_other_chips(x, y)):
                jr = 2 * chip[0] + chip[1]
                mine = _remote(_half(ins[w], c, cols[w], jr), _half(outs[w], c, cols[w], jr), send.at[3 * w + r],
                               recv.at[3 * w + r], (x, y, 1 - c))
                got = _half(outs[w], 1 - c, cols[w], jr)
                yield mine, _remote(got, got, send.at[3 * w + r], recv.at[3 * w + r], (x, y, 1 - c))

    def start(*refs):
        for mine, _ in copies(*refs):
            mine.start()

    def finish(*refs):
        pairs = list(copies(*refs))
        for _, theirs in pairs:
            theirs.wait_recv()
        for mine, _ in pairs:
            mine.wait_send()

    outs = [jax.ShapeDtypeStruct(p.shape, p.dtype) for p in partial]
    return _Stage(partial, outs, 3 * n, start, finish, aliases={w: w for w in range(n)})


def _stage_exchange_halves(grads, cols):
    n = len(grads)

    def copies(ins, outs, send, recv):
        x, y, c = _mesh_pos()
        for w in range(n):
            yield _remote(_half(ins[w], 1 - c, cols[w], slice(None)), outs[w], send.at[w], recv.at[w], (x, y, 1 - c))

    def start(*refs):
        for cp in copies(*refs):
            cp.start()

    def finish(*refs):
        cps = list(copies(*refs))
        for cp in cps:
            cp.wait_recv()
        for cp in cps:
            cp.wait_send()

    outs = [jax.ShapeDtypeStruct(_half_shape(g.shape, col), g.dtype) for g, col in zip(grads, cols)]
    return _Stage(grads, outs, n, start, finish)


def _stage_scatter(parts):
    n = len(parts)

    def copies(ins, outs, send, recv):
        x, y, c = _mesh_pos()
        for w in range(n):
            for r, chip in enumerate(_other_chips(x, y)):
                jr = 2 * chip[0] + chip[1]
                yield _remote(ins[w].at[jr], outs[w].at[r], send.at[3 * w + r], recv.at[3 * w + r], (*chip, c))

    def start(*refs):
        for cp in copies(*refs):
            cp.start()

    def finish(*refs):
        cps = list(copies(*refs))
        for cp in cps:
            cp.wait_recv()
        for cp in cps:
            cp.wait_send()

    outs = [jax.ShapeDtypeStruct((3,) + p.shape[1:], p.dtype) for p in parts]
    return _Stage(parts, outs, 3 * n, start, finish)


def _stage_share(fulls, cols):
    n = len(fulls)

    def copies(ins, outs, send, recv):
        x, y, c = _mesh_pos()
        for w in range(n):
            theirs = _half(outs[w], 1 - c, cols[w])
            yield (_remote(_half(ins[w], c, cols[w]), _half(outs[w], c, cols[w]), send.at[w], recv.at[w], (x, y, 1 - c)),
                   _remote(theirs, theirs, send.at[w], recv.at[w], (x, y, 1 - c)))

    def start(*refs):
        for mine, _ in copies(*refs):
            mine.start()

    def finish(*refs):
        pairs = list(copies(*refs))
        for _, theirs in pairs:
            theirs.wait_recv()
        for mine, _ in pairs:
            mine.wait_send()

    outs = [jax.ShapeDtypeStruct(h.shape, h.dtype) for h in fulls]
    return _Stage(fulls, outs, n, start, finish, aliases={w: w for w in range(n)})


def _run_stages(name, stages):
    return _pcall(None, stages, name=name, out_shape=[], in_specs=[], out_specs=[])()[1]


TILE_BYTES = 2 * 1024 * 1024
SUM_TILE_BYTES = 4 * 1024 * 1024


def _row_tile(rows, cols, itemsize=4, tile_bytes=TILE_BYTES):
    for t in (1024, 512, 256, 128, 64, 32, 16, 8):
        if rows % t == 0 and t * cols * itemsize <= tile_bytes:
            return t
    return rows


def _col_tile(rows, cols, itemsize=4, tile_bytes=TILE_BYTES):
    for t in (2048, 1024, 512, 256, 128):
        if cols % t == 0 and t * rows * itemsize <= tile_bytes:
            return t
    return cols


def _tiling(rows, cols, col, tile_bytes=TILE_BYTES):
    if col:
        tc = _col_tile(rows, cols, tile_bytes=tile_bytes)
        return (rows, tc), cols // tc
    tr = _row_tile(rows, cols, tile_bytes=tile_bytes)
    return (tr, cols), rows // tr


def _strip(col, i):
    return (0, i) if col else (i, 0)


def _pair_sum(name, g, recv, core, col):
    blk, nb = _tiling(*recv.shape[1:], col, tile_bytes=SUM_TILE_BYTES)

    def body(c_ref, g_ref, r_ref, o_ref):
        o_ref[...] = (g_ref[...] + r_ref[...]).astype(BF16)

    grid_spec = pltpu.PrefetchScalarGridSpec(
        num_scalar_prefetch=1, grid=(N_CHIPS, nb),
        in_specs=[pl.BlockSpec((None,) + blk, lambda j, i, cr: (j,) + _strip(col, cr[0] * nb + i)),
                  pl.BlockSpec((None,) + blk, lambda j, i, cr: (j,) + _strip(col, i))],
        out_specs=pl.BlockSpec((None,) + blk, lambda j, i, cr: (j,) + _strip(col, i)))
    return pl.pallas_call(body, name=name, out_shape=jax.ShapeDtypeStruct(recv.shape, BF16), grid_spec=grid_spec,
                          compiler_params=_cp("parallel", "parallel"))(core, g, recv)


def _quad_sum(name, own, landed, chip_core, col):
    rows, cols = landed.shape[1:]
    blk, nb = _tiling(rows, cols, col, tile_bytes=SUM_TILE_BYTES)
    full = (rows, 2 * cols) if col else (2 * rows, cols)

    def body(cc_ref, own_ref, l_ref, o_ref):
        o_ref[...] = ((own_ref[...].astype(F32) + l_ref[0].astype(F32)) + l_ref[1].astype(F32)) + l_ref[2].astype(F32)

    grid_spec = pltpu.PrefetchScalarGridSpec(
        num_scalar_prefetch=1, grid=(nb,),
        in_specs=[pl.BlockSpec((None,) + blk, lambda i, cc: (cc[0],) + _strip(col, i)),
                  pl.BlockSpec((3,) + blk, lambda i, cc: (0,) + _strip(col, i))],
        out_specs=pl.BlockSpec(blk, lambda i, cc: _strip(col, cc[1] * nb + i)))
    return pl.pallas_call(body, name=name, out_shape=jax.ShapeDtypeStruct(full, F32), grid_spec=grid_spec,
                          compiler_params=_cp("arbitrary"))(chip_core, own, landed)


def _device_sum(name, gathered):
    def body(g_ref, o_ref):
        total = g_ref[0]
        for k in range(1, N_DEV):
            total = total + g_ref[k]
        o_ref[...] = total

    return pl.pallas_call(body, name=name, out_shape=jax.ShapeDtypeStruct(gathered.shape[1:], F32))(gathered)


def _adamw(name, w, g, m, v):
    rows, cols = w.shape
    col = rows % SUBLANES != 0
    blk, nb = _tiling(rows, cols, col)
    bc1 = 1.0 - ADAM_B1 ** ADAM_STEP
    bc2 = 1.0 - ADAM_B2 ** ADAM_STEP

    def body(w_ref, g_ref, m_ref, v_ref, d_ref, mo_ref, vo_ref):
        gv = g_ref[...]
        mn = ADAM_B1 * m_ref[...] + (1.0 - ADAM_B1) * gv
        vn = ADAM_B2 * v_ref[...] + (1.0 - ADAM_B2) * (gv * gv)
        mo_ref[...] = mn
        vo_ref[...] = vn
        d_ref[...] = -ADAM_LR * ((mn / bc1) / (jnp.sqrt(vn / bc2) + ADAM_EPS) + ADAM_WD * w_ref[...])

    spec = pl.BlockSpec(blk, lambda i: _strip(col, i))
    return pl.pallas_call(
        body, name=name, out_shape=[jax.ShapeDtypeStruct((rows, cols), F32)] * 3, grid=(nb,),
        in_specs=[spec] * 4, out_specs=[spec] * 3, compiler_params=_cp("parallel"),
    )(w, g, m, v)


WEIGHTS = ["w_ada", "b_ada", "ffn1_w_in", "ffn1_w_out", "ln1_g", "ln1_b", "w_mix_in", "rel_bias", "w_alpha2",
           "b_alpha", "gla_norm_g", "w_proj_a", "w_proj_b", "w_mix_out", "ln2_g", "ln2_b", "ffn2_w_in", "ffn2_w_out",
           "ln3_g", "ln3_b"]
BIG = {"ffn1_w_in": True, "ffn1_w_out": False, "w_mix_in": False, "w_proj_a": True, "w_proj_b": True,
       "w_mix_out": False, "ffn2_w_in": True, "ffn2_w_out": False}
TRANSPOSED = ("w_mix_in",)
STACKED = ("ffn1_w_in", "ffn2_w_in", "w_mix_in")
GROUP_FFN1 = ("ffn1_w_in", "ffn1_w_out")
GROUP_PROJ = ("w_proj_a", "w_proj_b", "w_mix_out")
SMALL = ["ln1_g", "ln1_b", "ln2_g", "ln2_b", "ln3_g", "ln3_b", "b_alpha", "gla_norm_g", "rel_bias", "w_alpha2"]


def _pad_rows(vec, rows=SUBLANES):
    per = -(-vec.shape[0] // (rows * LANES)) * LANES
    return jnp.pad(vec, (0, rows * per - vec.shape[0])).reshape(rows, per)


def _silu(v):
    return v * _sigmoid(v)


class _MeshPlan:
    def __init__(self, shards, chip, core):
        self.shapes = {k: v.shape for k, v in shards.items()}
        self.slots = {k: lax.dynamic_update_slice(lax.empty((N_CHIPS,) + v.shape, v.dtype), v[None], (chip, 0, 0))
                      for k, v in shards.items()}
        self.core1 = core.astype(jnp.int32).reshape(1)
        self.chip_core = jnp.stack([chip, core]).astype(jnp.int32)
        self.partial, self.full, self.local, self.pair, self.half, self.final, self.memos = {}, {}, {}, {}, {}, {}, {}
        ici, d2d, x1, x2, x3 = self.gather_ici, self.gather_d2d, self.exchange, self.scatter, self.share
        mix_in, in1, out1, in2, out2 = ("w_mix_in",), ("ffn1_w_in",), ("ffn1_w_out",), ("ffn2_w_in",), ("ffn2_w_out",)
        self.schedule = {
            "ffn1_in_fwd": [ici(mix_in)], "ffn1_out_fwd": [d2d(mix_in), ici(out2)],
            "mix_in_a": [d2d(out2)], "mix_in_g": [ici(GROUP_PROJ)],
            "attn_fwd": [ici(in2), d2d(GROUP_PROJ)], "gla_fwd": [d2d(in2)],
            "ffn2_dw_in": [x1(out2)], "ffn2_du": [x2(out2), x1(in2)], "mix_dw_out": [x3(out2)],
            "attn_bwd": [x2(in2)], "gla_bwd": [x3(in2), x1(GROUP_PROJ)],
            "mix_du_g": [x2(GROUP_PROJ)], "mix_dw_g": [x3(GROUP_PROJ)],
            "ffn1_out_bwd": [x1(mix_in)], "ffn1_dw_in": [x2(mix_in)], "ffn1_dw_out": [x3(mix_in), x1(in1)],
            "ffn1_du": [x2(in1), x1(out1)], "rel_bias_grad": [x2(out1), x3(in1)],
        }

    def weight(self, k):
        return self.full[k]

    def grad(self, k, g):
        r, cc = self.shapes[k]
        if k not in STACKED:
            g = g.reshape(r, N_CHIPS, cc).transpose(1, 0, 2) if BIG[k] else g.reshape(N_CHIPS, r, cc)
        self.local[k] = g

    def memo(self, key, make):
        if key not in self.memos:
            self.memos[key] = make()
        return self.memos[key]

    def host(self, name, call):
        builders = self.schedule.get(name)
        if not builders:
            return call(None)
        built = [b() for b in builders]
        main, comm = call([st for st, _ in built])
        for (_, post), res in zip(built, comm):
            post(res)
        return main

    def run(self, name, builders):
        built = [b() for b in builders]
        for (_, post), res in zip(built, _run_stages(name, [st for st, _ in built])):
            post(res)

    def set_gathered(self, names, gathered):
        for k, g in zip(names, gathered):
            _, r, cc = g.shape
            if k not in STACKED:
                g = g.transpose(1, 0, 2).reshape(r, N_CHIPS * cc) if BIG[k] else g.reshape(N_CHIPS * r, cc)
            self.full[k] = g

    @staticmethod
    def cols(names):
        return [k in TRANSPOSED for k in names]

    def gather_ici(self, names):
        def post(res):
            self.partial.update(zip(names, res))
        return lambda: (_stage_gather_ici([self.slots[k] for k in names], self.cols(names)), post)

    def gather_d2d(self, names):
        return lambda: (_stage_gather_d2d([self.partial[k] for k in names], self.cols(names)),
                        lambda res: self.set_gathered(names, res))

    def exchange(self, names):
        def post(res):
            for k, r in zip(names, res):
                self.pair[k] = _pair_sum(f"pair_sum_{k}", self.local[k], r, self.core1, k in TRANSPOSED)
        return lambda: (_stage_exchange_halves([self.local[k] for k in names], self.cols(names)), post)

    def scatter(self, names):
        def post(res):
            for k, landed in zip(names, res):
                self.half[k] = _quad_sum(f"quad_sum_{k}", self.pair[k], landed, self.chip_core, k in TRANSPOSED)
        return lambda: (_stage_scatter([self.pair[k] for k in names]), post)

    def share(self, names):
        def post(res):
            self.final.update(zip(names, res))
        return lambda: (_stage_share([self.half[k] for k in names], self.cols(names)), post)


def _step(args):
    x_pos, y_pos, c_pos = _mesh_pos()
    chip = 2 * x_pos + y_pos
    dev = 4 * x_pos + 2 * y_pos + c_pos
    take = lambda name, k: args[name][0].T if k in TRANSPOSED else args[name][0]
    w = {k: take(k, k) for k in WEIGHTS}
    mom = {k: take("m_" + k, k) for k in WEIGHTS}
    vel = {k: take("v_" + k, k) for k in WEIGHTS}
    x = args["x"][0]
    target = args["loss_target"][0]
    s, d = x.shape
    kd = d // 4
    rel_sh = w["rel_bias"].shape[1]
    wa2_sh = w["w_alpha2"].shape[1]
    ada_sh = w["w_ada"].shape[1]

    n_rel, n_wa2 = A_HEADS * rel_sh, GATE_RANK * wa2_sh
    packed = _pad_rows(jnp.concatenate([args["c"].reshape(-1), w["rel_bias"].reshape(-1), w["w_alpha2"].reshape(-1)]))
    got = _allgather_rows("gather_small_inputs", packed).reshape(N_DEV, -1)
    c_all = got[:, :d]
    per_chip = got[0::2]
    rel_bias = per_chip[:, d:d + n_rel].reshape(N_CHIPS, A_HEADS, rel_sh).transpose(1, 0, 2).reshape(A_HEADS, -1)
    w_alpha2 = per_chip[:, d + n_rel:d + n_rel + n_wa2].reshape(N_CHIPS, GATE_RANK, wa2_sh).transpose(1, 0, 2)
    w_alpha2 = w_alpha2.reshape(GATE_RANK, -1)

    b_shard = lax.dynamic_slice(w["b_ada"], (chip * ada_sh,), (ada_sh,))
    mod_shard = _mm("ada_fwd", "nn", c_all, w["w_ada"], (N_DEV, ada_sh, d), tm=N_DEV, tn=_tile(ada_sh, (512, 128)),
                    tk=d, precision=HIGHEST, a_fn=_silu, add=jnp.broadcast_to(b_shard[None], (N_DEV, ada_sh)))
    mod_all = _allgather_rows("gather_mod", mod_shard).reshape(N_DEV, N_DEV, ada_sh)[0::2]
    mod_all = mod_all.transpose(1, 0, 2).reshape(N_DEV, N_MOD * d)
    mod = lax.dynamic_index_in_dim(mod_all, dev, 0, keepdims=False).reshape(N_MOD, d)

    names = list(BIG)
    plan = _MeshPlan({k: w[k].astype(BF16) for k in names}, chip, c_pos)
    plan.set_gathered(GROUP_FFN1, _allgather_weights([plan.slots[k] for k in GROUP_FFN1]))

    small = dict(rel_bias=rel_bias, w_alpha2=w_alpha2, b_alpha=w["b_alpha"][None], gla_norm_g=w["gla_norm_g"][None])
    for k in ("ln1_g", "ln1_b", "ln2_g", "ln2_b", "ln3_g", "ln3_b"):
        small[k] = w[k][None]
    loss_local, grad_x, small_grads, dmod = _device_step(x, target, mod, small, plan)
    loss = lax.psum(loss_local, ("x", "y", "c"))
    plan.run("grad_tail_share", [plan.share(GROUP_FFN1[1:])])

    flat = jnp.concatenate([small_grads[k].reshape(-1) for k in SMALL] + [dmod.reshape(-1)])
    n_small = flat.shape[0] - N_MOD * d
    packed = _pad_rows(flat)
    all_small = _allgather_rows("gather_small_grads", packed).reshape(N_DEV, SUBLANES, -1)
    summed = _device_sum("small_grad_sum", all_small).reshape(-1)
    dmod_all = all_small.reshape(N_DEV, -1)[:, n_small:n_small + N_MOD * d]
    dmod_shard = lax.dynamic_slice(dmod_all, (0, chip * ada_sh), (N_DEV, ada_sh))
    grads = {"b_ada": summed[n_small:n_small + N_MOD * d]}
    off = 0
    for k in SMALL:
        size = small_grads[k].size
        grads[k] = summed[off:off + size].reshape(small_grads[k].shape)
        off += size
    grads["rel_bias"] = lax.dynamic_slice(grads["rel_bias"], (0, chip * rel_sh), (A_HEADS, rel_sh))
    grads["w_alpha2"] = lax.dynamic_slice(grads["w_alpha2"], (0, chip * wa2_sh), (GATE_RANK, wa2_sh))
    grads["w_ada"] = _mm("ada_bwd", "nn", jnp.pad(c_all.T, ((0, 0), (0, LANES - N_DEV))),
                         jnp.pad(dmod_shard, ((0, LANES - N_DEV), (0, 0))), (d, ada_sh, LANES), tm=_tile(d, (1024,)),
                         tn=_tile(ada_sh, (512, 128)), tk=LANES, precision=HIGHEST, a_fn=_silu)

    grads.update(plan.final)

    delta, new_m, new_v = {}, {}, {}
    for k in ["w_ada"] + names:
        delta[k], new_m[k], new_v[k] = _adamw(f"adamw_{k}", w[k], grads[k], mom[k], vel[k])
    tiny = ["b_ada"] + SMALL
    pack = lambda src: _pad_rows(jnp.concatenate([src[k].reshape(-1) for k in tiny]), rows=1).reshape(-1, LANES)
    outs = _adamw("adamw_small", pack(w), pack(grads), pack(mom), pack(vel))
    off = 0
    for k in tiny:
        size = w[k].size
        for dst, src in zip((delta, new_m, new_v), outs):
            dst[k] = src.reshape(-1)[off:off + size].reshape(w[k].shape)
        off += size

    give = lambda src: [src[k].T[None] if k in TRANSPOSED else src[k][None] for k in WEIGHTS]
    return (loss, grad_x[None], *give(grads), *give(delta), *give(new_m), *give(new_v))


def kernel(x, c, w_ada, b_ada, ffn1_w_in, ffn1_w_out, ln1_g, ln1_b, w_mix_in, rel_bias, w_alpha2, b_alpha, gla_norm_g, w_proj_a, w_proj_b, w_mix_out, ln2_g, ln2_b, ffn2_w_in, ffn2_w_out, ln3_g, ln3_b, loss_target, m_w_ada, m_b_ada, m_ffn1_w_in, m_ffn1_w_out, m_ln1_g, m_ln1_b, m_w_mix_in, m_rel_bias, m_w_alpha2, m_b_alpha, m_gla_norm_g, m_w_proj_a, m_w_proj_b, m_w_mix_out, m_ln2_g, m_ln2_b, m_ffn2_w_in, m_ffn2_w_out, m_ln3_g, m_ln3_b, v_w_ada, v_b_ada, v_ffn1_w_in, v_ffn1_w_out, v_ln1_g, v_ln1_b, v_w_mix_in, v_rel_bias, v_w_alpha2, v_b_alpha, v_gla_norm_g, v_w_proj_a, v_w_proj_b, v_w_mix_out, v_ln2_g, v_ln2_b, v_ffn2_w_in, v_ffn2_w_out, v_ln3_g, v_ln3_b):
    return _step(dict(locals()))
```

```python
import functools

import jax
import jax.numpy as jnp
from jax import lax
from jax.experimental import pallas as pl
from jax.experimental.pallas import tpu as pltpu

F32 = jnp.float32
BF16 = jnp.bfloat16
MESH = pl.DeviceIdType.MESH
HIGHEST = lax.Precision.HIGHEST

VMEM_LIMIT_BYTES = 56 * 1024 * 1024
LANES = 128
SUBLANES = 8

CHUNK = 64
A_HEADS = 16
A_HEAD_DIM = 64
A_PAST_CHUNKS = 8
A_BAND = (A_PAST_CHUNKS + 1) * CHUNK
A_PAD = A_PAST_CHUNKS * CHUNK
REL_CLIP = 256
REL_SIZE = REL_CLIP + CHUNK
B_HEADS = 4
GATE_RANK = 16
GATE_TAU = 16.0
N_MOD = 9
DEPTH = 1
ALPHA = (2.0 * DEPTH) ** 0.25
LN_EPS = 1e-5
RMS_EPS = 1e-6
ADAM_LR = 0.001
ADAM_B1 = 0.9
ADAM_B2 = 0.999
ADAM_EPS = 1e-08
ADAM_WD = 0.01
ADAM_STEP = 10
NEG_BIG = -1e30

N_CHIPS = 4
N_DEV = 8


def _cp(*sem):
    return pltpu.CompilerParams(dimension_semantics=sem, vmem_limit_bytes=VMEM_LIMIT_BYTES)


class _Stage:
    def __init__(self, arrays, out_shapes, n_sems, start, finish, aliases=None, relay=None):
        self.arrays, self.out_shapes, self.n_sems = list(arrays), list(out_shapes), n_sems
        self.start, self.finish, self.relay, self.aliases = start, finish, relay, dict(aliases or {})


def _pcall(body, stages, *, name, out_shape, in_specs, out_specs, grid=(), scratch_shapes=(), compiler_params=None):
    single = not isinstance(out_shape, (list, tuple))
    outs = [out_shape] if single else list(out_shape)
    ospecs = [out_specs] if single else list(out_specs)
    in_specs, scratch_shapes = list(in_specs), list(scratch_shapes)
    n_in, n_out, n_sc = len(in_specs), len(outs), len(scratch_shapes)
    stages = list(stages or [])
    c_in = [a for st in stages for a in st.arrays]
    c_out = [o for st in stages for o in st.out_shapes]
    aliases = {}
    io, oo = n_in, n_out
    for st in stages:
        for a, b in st.aliases.items():
            aliases[io + a] = oo + b
        io += len(st.arrays)
        oo += len(st.out_shapes)

    def wrapped(*refs):
        ins = refs[:n_in]
        cins = refs[n_in:n_in + len(c_in)]
        base = n_in + len(c_in)
        mouts = refs[base:base + n_out]
        couts = refs[base + n_out:base + n_out + len(c_out)]
        base += n_out + len(c_out)
        scr = refs[base:base + n_sc]
        sems = refs[base + n_sc:]

        def each(phase):
            i = o = 0
            for k, st in enumerate(stages):
                fn = (st.start, st.relay, st.finish)[phase]
                if fn is not None:
                    fn(cins[i:i + len(st.arrays)], couts[o:o + len(st.out_shapes)], sems[2 * k], sems[2 * k + 1])
                i += len(st.arrays)
                o += len(st.out_shapes)

        if stages and grid:
            step = functools.reduce(lambda acc, a: acc * grid[a] + pl.program_id(a), range(len(grid)), 0)
            steps = functools.reduce(lambda a, b: a * b, grid)
            pl.when(step == 0)(lambda: each(0))
            if any(st.relay for st in stages):
                pl.when(step == (2 * steps) // 3)(lambda: each(1))
            if body is not None:
                body(*ins, *mouts, *scr)
            pl.when(step == steps - 1)(lambda: each(2))
        else:
            each(0)
            each(1)
            if body is not None:
                body(*ins, *mouts, *scr)
            each(2)

    sem_shapes = []
    for st in stages:
        sem_shapes += [pltpu.SemaphoreType.DMA((st.n_sems,)), pltpu.SemaphoreType.DMA((st.n_sems,))]
    kwargs = dict(grid=grid) if grid else {}
    if compiler_params is not None:
        kwargs["compiler_params"] = compiler_params

    def run(*operands):
        res = pl.pallas_call(
            wrapped, name=name, out_shape=outs + c_out, in_specs=in_specs + [HBM_SPEC] * len(c_in),
            out_specs=ospecs + [HBM_SPEC] * len(c_out), scratch_shapes=scratch_shapes + sem_shapes,
            input_output_aliases=aliases, **kwargs)(*operands, *c_in)
        main = res[0] if single else tuple(res[:n_out])
        if not stages:
            return main
        comm, o = [], n_out
        for st in stages:
            comm.append(list(res[o:o + len(st.out_shapes)]))
            o += len(st.out_shapes)
        return main, comm

    return run


LONG_K = (2048, 1024)


def _tile(n, prefs):
    for t in prefs:
        if t <= n and n % t == 0:
            return t
    return n


_DIMS = {"nn": (((1,), (0,)), ((), ())), "nt": (((1,), (1,)), ((), ())), "tn": (((0,), (0,)), ((), ()))}


def _dot(a, b, mode="nn", precision=None):
    return lax.dot_general(a, b, _DIMS[mode], precision=precision, preferred_element_type=F32)


def _sigmoid(x):
    return 0.5 * jnp.tanh(0.5 * x) + 0.5


EPILOGUE_STRIP = 256


def _strips(n, width=EPILOGUE_STRIP):
    width = width if n % width == 0 else n
    return [slice(j, j + width) for j in range(0, n, width)]


def _mm(name, mode, a, b, mnk, *, tm, tn, tk, out_dtype=F32, precision=None, a_spec=None, b_spec=None,
        out_shape=None, o_spec=None, add=None, a_fn=None, thin=None, stages=None):
    m, n, k = mnk
    assert m % tm == 0 and n % tn == 0 and k % tk == 0, (name, mnk, tm, tn, tk)
    nk = k // tk
    if a_spec is None:
        a_spec = {"nn": pl.BlockSpec((tm, tk), lambda i, j, kk: (i, kk)),
                  "nt": pl.BlockSpec((tm, tk), lambda i, j, kk: (i, kk)),
                  "tn": pl.BlockSpec((tk, tm), lambda i, j, kk: (kk, i))}[mode]
    if b_spec is None:
        b_spec = {"nn": pl.BlockSpec((tk, tn), lambda i, j, kk: (kk, j)),
                  "nt": pl.BlockSpec((tn, tk), lambda i, j, kk: (j, kk)),
                  "tn": pl.BlockSpec((tk, tn), lambda i, j, kk: (kk, j))}[mode]
    if o_spec is None:
        o_spec = pl.BlockSpec((tm, tn), lambda i, j, kk: (i, j))
    if out_shape is None:
        out_shape = (m, n)
    has_add = add is not None
    n_in = 2 + has_add + (2 if thin else 0)

    def body(*refs):
        a_ref, b_ref = refs[0], refs[1]
        add_ref = refs[2] if has_add else None
        o_ref = refs[n_in]
        av = a_ref[...]
        if a_fn is not None:
            av = a_fn(av)
        part = _dot(av, b_ref[...], mode, precision)

        def finish(total):
            if has_add:
                total = total + add_ref[...]
            if thin:
                total = total + _dot(refs[n_in - 2][...], refs[n_in - 1][...])
            o_ref[...] = total.astype(out_dtype)

        if nk == 1:
            finish(part)
        else:
            acc_ref = refs[-1]
            kk = pl.program_id(2)

            @pl.when(kk == 0)
            def _():
                acc_ref[...] = part

            @pl.when(kk > 0)
            def _():
                acc_ref[...] += part

            @pl.when(kk == nk - 1)
            def _():
                finish(acc_ref[...])

    in_specs = [a_spec, b_spec]
    operands = [a, b]
    if has_add:
        in_specs.append(pl.BlockSpec((tm, tn), lambda i, j, kk: (i, j)))
        operands.append(add)
    if thin:
        k2 = thin[0].shape[1]
        in_specs += [pl.BlockSpec((tm, k2), lambda i, j, kk: (i, 0)), pl.BlockSpec((k2, tn), lambda i, j, kk: (0, j))]
        operands += list(thin)
    return _pcall(
        body, stages, name=name, out_shape=jax.ShapeDtypeStruct(out_shape, out_dtype), grid=(m // tm, n // tn, nk),
        in_specs=in_specs, out_specs=o_spec,
        scratch_shapes=[pltpu.VMEM((tm, tn), F32)] if nk > 1 else [],
        compiler_params=_cp("arbitrary", "arbitrary", "arbitrary") if stages else _cp("parallel", "parallel", "arbitrary"),
    )(*operands)


def _row_spec(tr, d):
    return pl.BlockSpec((tr, d), lambda i: (i, 0))


def _vec_spec(d, rows=1):
    return pl.BlockSpec((rows, d), lambda i: (0, 0))


def _col_spec(d, tr):
    return pl.BlockSpec((d, tr), lambda i: (0, i))


def _modulate(name, x, sh, sc):
    s, d = x.shape
    tr = _tile(s, (512, 256))

    def body(x_ref, sh_ref, sc_ref, o_ref, ot_ref):
        u = x_ref[...] * (1.0 + sc_ref[...]) + sh_ref[...]
        o_ref[...] = u.astype(BF16)
        ot_ref[...] = u.T.astype(BF16)

    return pl.pallas_call(
        body, name=name, out_shape=(jax.ShapeDtypeStruct((s, d), BF16), jax.ShapeDtypeStruct((d, s), BF16)),
        grid=(s // tr,), in_specs=[_row_spec(tr, d), _vec_spec(d), _vec_spec(d)],
        out_specs=(_row_spec(tr, d), _col_spec(d, tr)), compiler_params=_cp("parallel"),
    )(x, sh, sc)


def _ln_stats(r):
    mu = jnp.mean(r, axis=-1, keepdims=True)
    xc = r - mu
    var = jnp.mean(xc * xc, axis=-1, keepdims=True)
    rstd = lax.rsqrt(var + LN_EPS)
    return xc * rstd, rstd


def _resid_ln_fwd(name, x, f, gate, ln_g, ln_b, sh_n, sc_n, coef, transposed=False):
    s, d = x.shape
    tr = _tile(s, (256,))

    def body(x_ref, f_ref, gate_ref, g_ref, b_ref, sh_ref, sc_ref, h_ref, u_ref, *ut_ref):
        r = ALPHA * x_ref[...] + (coef * gate_ref[...]) * f_ref[...]
        xhat, _ = _ln_stats(r)
        h = xhat * g_ref[...] + b_ref[...]
        h_ref[...] = h
        u = h * (1.0 + sc_ref[...]) + sh_ref[...]
        u_ref[...] = u.astype(BF16)
        if transposed:
            ut_ref[0][...] = u.T.astype(BF16)

    extra_shape = (jax.ShapeDtypeStruct((d, s), BF16),) if transposed else ()
    extra_spec = (_col_spec(d, tr),) if transposed else ()
    return pl.pallas_call(
        body, name=name,
        out_shape=(jax.ShapeDtypeStruct((s, d), F32), jax.ShapeDtypeStruct((s, d), BF16)) + extra_shape,
        grid=(s // tr,), in_specs=[_row_spec(tr, d), _row_spec(tr, d)] + [_vec_spec(d)] * 5,
        out_specs=(_row_spec(tr, d), _row_spec(tr, d)) + extra_spec, compiler_params=_cp("parallel"),
    )(x, f, gate, ln_g, ln_b, sh_n, sc_n)


ROW_DSC, ROW_DSH, ROW_DLN_G, ROW_DLN_B, ROW_DGATE, ROW_LOSS = 0, 1, 2, 3, 4, 5


def _ln_bwd_core(dy, xhat, rstd, ln_g):
    dxhat = dy * ln_g
    m1 = jnp.mean(dxhat, axis=-1, keepdims=True)
    m2 = jnp.mean(dxhat * xhat, axis=-1, keepdims=True)
    return rstd * (dxhat - m1 - xhat * m2)


def _colsum(v):
    return jnp.sum(v, axis=0, keepdims=True)


def _final_ln_loss_bwd(name, x, f, target, gate, ln_g, ln_b, coef):
    s, d = x.shape
    tr = _tile(s, (256,))
    inv_d = 1.0 / d

    def body(x_ref, f_ref, t_ref, gate_ref, g_ref, b_ref, dr_ref, df_ref, acc_ref):
        @pl.when(pl.program_id(0) == 0)
        def _():
            acc_ref[...] = jnp.zeros_like(acc_ref)

        fv = f_ref[...]
        r = ALPHA * x_ref[...] + (coef * gate_ref[...]) * fv
        xhat, rstd = _ln_stats(r)
        h = xhat * g_ref[...] + b_ref[...]
        err = h - t_ref[...]
        dy = err * inv_d
        dr = _ln_bwd_core(dy, xhat, rstd, g_ref[...])
        dr_ref[...] = dr
        df_ref[...] = ((coef * gate_ref[...]) * dr).astype(BF16)
        acc_ref[ROW_DLN_G:ROW_DLN_G + 1, :] += _colsum(dy * xhat)
        acc_ref[ROW_DLN_B:ROW_DLN_B + 1, :] += _colsum(dy)
        acc_ref[ROW_DGATE:ROW_DGATE + 1, :] += _colsum((coef * dr) * fv)
        acc_ref[ROW_LOSS:ROW_LOSS + 1, :] += _colsum(err * err) * (0.5 * inv_d)

    return pl.pallas_call(
        body, name=name,
        out_shape=(jax.ShapeDtypeStruct((s, d), F32), jax.ShapeDtypeStruct((s, d), BF16),
                   jax.ShapeDtypeStruct((SUBLANES, d), F32)),
        grid=(s // tr,), in_specs=[_row_spec(tr, d)] * 3 + [_vec_spec(d)] * 3,
        out_specs=(_row_spec(tr, d), _row_spec(tr, d), _vec_spec(d, SUBLANES)),
        compiler_params=_cp("arbitrary"),
    )(x, f, target, gate, ln_g, ln_b)


def _resid_ln_bwd(name, du_n, dr_n, x, f, sc_n, gate, ln_g, ln_b, coef):
    s, d = x.shape
    tr = _tile(s, (256,))

    def body(du_ref, drn_ref, x_ref, f_ref, sc_ref, gate_ref, g_ref, b_ref, dr_ref, df_ref, acc_ref):
        @pl.when(pl.program_id(0) == 0)
        def _():
            acc_ref[...] = jnp.zeros_like(acc_ref)

        fv = f_ref[...]
        du = du_ref[...]
        r = ALPHA * x_ref[...] + (coef * gate_ref[...]) * fv
        xhat, rstd = _ln_stats(r)
        h = xhat * g_ref[...] + b_ref[...]
        dy = du * (1.0 + sc_ref[...]) + ALPHA * drn_ref[...]
        dr = _ln_bwd_core(dy, xhat, rstd, g_ref[...])
        dr_ref[...] = dr
        df_ref[...] = ((coef * gate_ref[...]) * dr).astype(BF16)
        acc_ref[ROW_DSC:ROW_DSC + 1, :] += _colsum(du * h)
        acc_ref[ROW_DSH:ROW_DSH + 1, :] += _colsum(du)
        acc_ref[ROW_DLN_G:ROW_DLN_G + 1, :] += _colsum(dy * xhat)
        acc_ref[ROW_DLN_B:ROW_DLN_B + 1, :] += _colsum(dy)
        acc_ref[ROW_DGATE:ROW_DGATE + 1, :] += _colsum((coef * dr) * fv)

    return pl.pallas_call(
        body, name=name,
        out_shape=(jax.ShapeDtypeStruct((s, d), F32), jax.ShapeDtypeStruct((s, d), BF16),
                   jax.ShapeDtypeStruct((SUBLANES, d), F32)),
        grid=(s // tr,), in_specs=[_row_spec(tr, d)] * 4 + [_vec_spec(d)] * 4,
        out_specs=(_row_spec(tr, d), _row_spec(tr, d), _vec_spec(d, SUBLANES)),
        compiler_params=_cp("arbitrary"),
    )(du_n, dr_n, x, f, sc_n, gate, ln_g, ln_b)


def _input_grad(name, du, dr, x, sc):
    s, d = x.shape
    tr = _tile(s, (256,))

    def body(du_ref, dr_ref, x_ref, sc_ref, gx_ref, acc_ref):
        @pl.when(pl.program_id(0) == 0)
        def _():
            acc_ref[...] = jnp.zeros_like(acc_ref)

        du = du_ref[...]
        gx_ref[...] = du * (1.0 + sc_ref[...]) + ALPHA * dr_ref[...]
        acc_ref[ROW_DSC:ROW_DSC + 1, :] += _colsum(du * x_ref[...])
        acc_ref[ROW_DSH:ROW_DSH + 1, :] += _colsum(du)

    return pl.pallas_call(
        body, name=name,
        out_shape=(jax.ShapeDtypeStruct((s, d), F32), jax.ShapeDtypeStruct((SUBLANES, d), F32)),
        grid=(s // tr,), in_specs=[_row_spec(tr, d)] * 3 + [_vec_spec(d)],
        out_specs=(_row_spec(tr, d), _vec_spec(d, SUBLANES)), compiler_params=_cp("arbitrary"),
    )(du, dr, x, sc)


def _ffn_in_fwd(name, u, w_in, stages=None):
    s, d = u.shape
    cs = w_in.shape[2]
    f = 2 * cs
    tm, tn = _tile(s, (2048, 1024, 512)), _tile(cs, (256, 128))
    nb = f // tn
    nbs = cs // tn

    def body(u_ref, wa_ref, wb_ref, ab_ref, act_ref):
        for rows in _strips(tm, 512):
            uv = u_ref[rows, :]
            a = _dot(uv, wa_ref[...])
            b = _dot(uv, wb_ref[...])
            sg = _sigmoid(a)
            silu = a * sg
            ab_ref[0, rows, :] = (b * (sg + silu * (1.0 - sg))).astype(BF16)
            ab_ref[1, rows, :] = silu.astype(BF16)
            act_ref[rows, :] = (silu * b).astype(BF16)

    return _pcall(
        body, stages, name=name,
        out_shape=(jax.ShapeDtypeStruct((2, s, f), BF16), jax.ShapeDtypeStruct((s, f), BF16)),
        grid=(s // tm, nb),
        in_specs=[pl.BlockSpec((tm, d), lambda i, j: (i, 0)),
                  pl.BlockSpec((None, d, tn), lambda i, j: (j // nbs, 0, j % nbs)),
                  pl.BlockSpec((None, d, tn), lambda i, j: (2 + j // nbs, 0, j % nbs))],
        out_specs=(pl.BlockSpec((2, tm, tn), lambda i, j: (0, i, j)), pl.BlockSpec((tm, tn), lambda i, j: (i, j))),
        compiler_params=_cp("arbitrary", "arbitrary"),
    )(u, w_in, w_in)


def _ffn_out_bwd(name, df, w_out, ab, stages=None):
    s, d = df.shape
    f = w_out.shape[0]
    tm, tn = _tile(s, (1024, 512)), _tile(f, (512, 256, 128))

    def body(df_ref, w_ref, ab_ref, dab_ref):
        dfv = df_ref[...]
        for cols in _strips(tn):
            dact = _dot(dfv, w_ref[cols, :], "nt")
            dab_ref[0, :, cols] = (dact * ab_ref[0, :, cols].astype(F32)).astype(BF16)
            dab_ref[1, :, cols] = (dact * ab_ref[1, :, cols].astype(F32)).astype(BF16)

    return _pcall(
        body, stages, name=name, out_shape=jax.ShapeDtypeStruct((2, s, f), BF16), grid=(s // tm, f // tn),
        in_specs=[pl.BlockSpec((tm, d), lambda i, j: (i, 0)), pl.BlockSpec((tn, d), lambda i, j: (j, 0)),
                  pl.BlockSpec((2, tm, tn), lambda i, j: (0, i, j))],
        out_specs=pl.BlockSpec((2, tm, tn), lambda i, j: (0, i, j)),
        compiler_params=_cp("arbitrary", "arbitrary"),
    )(df, w_out, ab)


def _ffn_forward(tag, u, plan):
    w_in, w_out = plan.weight(f"{tag}_w_in"), plan.weight(f"{tag}_w_out")
    s, d = u.shape
    f = w_out.shape[0]
    ab, act = plan.host(f"{tag}_in_fwd", lambda st: _ffn_in_fwd(f"{tag}_in_fwd", u, w_in, st))
    out = plan.host(f"{tag}_out_fwd", lambda st: _mm(
        f"{tag}_out_fwd", "nn", act, w_out, (s, d, f), tm=_tile(s, (1024,)), tn=_tile(d, (1024,)),
        tk=_tile(f, (2816, 1408, 512, 128)), stages=st))
    return out, (ab, act)


def _ffn_backward(tag, df, ut, saved, plan, in_first):
    w_in, w_out = plan.weight(f"{tag}_w_in"), plan.weight(f"{tag}_w_out")
    ab, act = saved
    d, s = ut.shape
    f = w_out.shape[0]
    dab = plan.host(f"{tag}_out_bwd", lambda st: _ffn_out_bwd(f"{tag}_out_bwd", df, w_out, ab, st))
    cs = w_in.shape[2]
    tk = _tile(cs, (2816, 1408, 256, 128))
    nkh, nks = f // tk, cs // tk
    tmd = _tile(d, (1024,))
    tks = _tile(s, LONG_K)

    def dw_in():
        tw = _tile(cs, (256, 128))
        nwh, nws = f // tw, cs // tw
        plan.grad(f"{tag}_w_in", plan.host(f"{tag}_dw_in", lambda st: _mm(
            f"{tag}_dw_in", "nn", ut, dab, (d, 2 * f, s), tm=tmd, tn=tw, tk=s,
            b_spec=pl.BlockSpec((None, s, tw), lambda i, j, kk: (j // nwh, 0, j % nwh)), out_shape=(N_CHIPS, d, cs),
            o_spec=pl.BlockSpec((None, tmd, tw), lambda i, j, kk: (j // nws, i, j % nws)), stages=st)))

    def dw_out():
        plan.grad(f"{tag}_w_out", plan.host(f"{tag}_dw_out", lambda st: _mm(
            f"{tag}_dw_out", "tn", act, df, (f, d, s), tm=_tile(f, (1408, 512, 128)), tn=tmd, tk=tks, stages=st)))

    for step in ((dw_in, dw_out) if in_first else (dw_out, dw_in)):
        step()
    return plan.host(f"{tag}_du", lambda st: _mm(
        f"{tag}_du", "nt", dab, w_in, (s, d, 2 * f), tm=_tile(s, (1024,)), tn=tmd, tk=tk,
        a_spec=pl.BlockSpec((None, _tile(s, (1024,)), tk), lambda i, j, kk: (kk // nkh, i, kk % nkh)),
        b_spec=pl.BlockSpec((None, tmd, tk), lambda i, j, kk: (kk // nks, j, kk % nks)), stages=st))


ATTN_Q = 4 * CHUNK
ATTN_W = ATTN_Q + A_PAD


def _band_bias(bias):
    n = ATTN_Q // CHUNK
    rows = [jnp.pad(bias, ((0, 0), (0, 0), (i * CHUNK, (n - 1 - i) * CHUNK)), constant_values=NEG_BIG)
            for i in range(n)]
    return jnp.concatenate(rows, axis=1)


def _band_bias_grad(dband):
    n = ATTN_Q // CHUNK
    parts = [dband[:, i * CHUNK:(i + 1) * CHUNK, i * CHUNK:i * CHUNK + A_BAND] for i in range(n)]
    return functools.reduce(jnp.add, parts)


def _attn_probs(q, kw, bias, key0):
    sc = _dot(q, kw, "nt") * (A_HEAD_DIM ** -0.5) + bias
    ks = lax.broadcasted_iota(jnp.int32, sc.shape, 1)
    sc = jnp.where(key0 + ks >= 0, sc, NEG_BIG)
    p = jnp.exp(sc - jnp.max(sc, axis=-1, keepdims=True))
    return p / jnp.sum(p, axis=-1, keepdims=True)


def _head_masks():
    lane = lax.broadcasted_iota(jnp.int32, (1, LANES), 1)
    return [lane // A_HEAD_DIM == h for h in range(LANES // A_HEAD_DIM)]


def _attn_fwd(p1, kvp, band, stages=None):
    s = p1.shape[0]
    aw = A_HEADS * A_HEAD_DIM
    nblk = aw // LANES
    hpb = LANES // A_HEAD_DIM
    assert s % ATTN_Q == 0

    def body(q_ref, k_ref, v_ref, b_ref, o_ref):
        base = pl.multiple_of(pl.program_id(1) * ATTN_Q, ATTN_Q)
        qv = q_ref[...]
        kw = k_ref[pl.ds(base, ATTN_W), :]
        vw = v_ref[pl.ds(base, ATTN_W), :]
        out = jnp.zeros((ATTN_Q, LANES), F32)
        for h, mask in enumerate(_head_masks()):
            p = _attn_probs(jnp.where(mask, qv, jnp.zeros_like(qv)), kw, b_ref[h], base - A_PAD)
            out = jnp.where(mask, _dot(p.astype(BF16), vw), out)
        o_ref[...] = out.astype(BF16)

    kv_rows = s + A_PAD
    return _pcall(
        body, stages, name="attn_fwd", out_shape=jax.ShapeDtypeStruct((s, aw), BF16), grid=(nblk, s // ATTN_Q),
        in_specs=[pl.BlockSpec((ATTN_Q, LANES), lambda b, i: (i, b)),
                  pl.BlockSpec((kv_rows, LANES), lambda b, i: (0, b)),
                  pl.BlockSpec((kv_rows, LANES), lambda b, i: (0, nblk + b)),
                  pl.BlockSpec((hpb, ATTN_Q, ATTN_W), lambda b, i: (b, 0, 0))],
        out_specs=pl.BlockSpec((ATTN_Q, LANES), lambda b, i: (i, b)),
        compiler_params=_cp("arbitrary", "arbitrary"),
    )(p1, kvp, kvp, band)


def _attn_bwd(p1, kvp, band, dya, stages=None):
    s = p1.shape[0]
    aw = A_HEADS * A_HEAD_DIM
    nblk = aw // LANES
    hpb = LANES // A_HEAD_DIM
    scale = A_HEAD_DIM ** -0.5

    def body(q_ref, k_ref, v_ref, b_ref, do_ref, dq_ref, dk_ref, dv_ref, db_ref):
        @pl.when(pl.program_id(1) == 0)
        def _():
            dk_ref[...] = jnp.zeros_like(dk_ref)
            dv_ref[...] = jnp.zeros_like(dv_ref)
            db_ref[...] = jnp.zeros_like(db_ref)

        base = pl.multiple_of(pl.program_id(1) * ATTN_Q, ATTN_Q)
        window = pl.ds(base, ATTN_W)
        kw = k_ref[window, :]
        vw = v_ref[window, :]
        qv = q_ref[...]
        dov = do_ref[...]
        dq = jnp.zeros((ATTN_Q, LANES), F32)
        dk = jnp.zeros((ATTN_W, LANES), F32)
        dv = jnp.zeros((ATTN_W, LANES), F32)
        for h, mask in enumerate(_head_masks()):
            qh = jnp.where(mask, qv, jnp.zeros_like(qv))
            doh = jnp.where(mask, dov, jnp.zeros_like(dov))
            p = _attn_probs(qh, kw, b_ref[h], base - A_PAD)
            dp = _dot(doh, vw, "nt")
            ds = p * (dp - jnp.sum(p * dp, axis=-1, keepdims=True))
            db_ref[h] += ds
            dsb = (ds * scale).astype(BF16)
            dq = jnp.where(mask, _dot(dsb, kw), dq)
            dk = dk + _dot(dsb, qh, "tn")
            dv = dv + _dot(p.astype(BF16), doh, "tn")
        dq_ref[...] = dq.astype(BF16)
        dk_ref[window, :] += dk
        dv_ref[window, :] += dv

    kv_rows = s + A_PAD
    q_spec = pl.BlockSpec((ATTN_Q, LANES), lambda b, i: (i, b))
    acc_spec = pl.BlockSpec((kv_rows, LANES), lambda b, i: (0, b))
    b_spec = pl.BlockSpec((hpb, ATTN_Q, ATTN_W), lambda b, i: (b, 0, 0))
    return _pcall(
        body, stages, name="attn_bwd",
        out_shape=(jax.ShapeDtypeStruct((s, aw), BF16), jax.ShapeDtypeStruct((kv_rows, aw), F32),
                   jax.ShapeDtypeStruct((kv_rows, aw), F32), jax.ShapeDtypeStruct((A_HEADS, ATTN_Q, ATTN_W), F32)),
        grid=(nblk, s // ATTN_Q),
        in_specs=[q_spec, acc_spec, pl.BlockSpec((kv_rows, LANES), lambda b, i: (0, nblk + b)), b_spec, q_spec],
        out_specs=(q_spec, acc_spec, acc_spec, b_spec), compiler_params=_cp("arbitrary", "arbitrary"),
    )(p1, kvp, kvp, band, dya)


def _rel_onehot():
    qi = jnp.arange(CHUNK)[:, None]
    ks = jnp.arange(A_BAND)[None, :]
    idx = (jnp.clip(ks - A_PAD - qi, -REL_CLIP, CHUNK - 1) + REL_CLIP).reshape(1, CHUNK * A_BAND)
    return (jnp.arange(REL_SIZE)[:, None] == idx).astype(F32)


def _gla_gate(lr, wa2, balpha):
    z = _dot(lr, wa2) + balpha
    la = (jnp.minimum(z, 0.0) - jnp.log(1.0 + jnp.exp(-jnp.abs(z)))) * (1.0 / GATE_TAU)
    row = lax.broadcasted_iota(jnp.int32, (CHUNK, CHUNK), 0)
    col = lax.broadcasted_iota(jnp.int32, (CHUNK, CHUNK), 1)
    cum = _dot((row >= col).astype(F32), la, precision=HIGHEST)
    return z, la, cum


def _gla_dims(p2):
    kd = p2.shape[1] // 6
    hk = kd // B_HEADS
    hv = 2 * hk
    return kd, hk, hv


GLA_CPS = 4


def _gla_fwd(p2, lrp, wa2p, balpha, gnorm, stages=None):
    s = p2.shape[0]
    kd, hk, hv = _gla_dims(p2)
    nc = s // CHUNK
    cps = GLA_CPS if nc % GLA_CPS == 0 else 1
    rows_per = cps * CHUNK
    qscale = hk ** -0.5

    def body(p_ref, lr_ref, wa_ref, ba_ref, gn_ref, yb_ref, st_ref, state):
        @pl.when(pl.program_id(0) == 0)
        def _():
            state[...] = jnp.zeros_like(state)

        gn = gn_ref[...]
        for sub in range(cps):
            rows = slice(sub * CHUNK, (sub + 1) * CHUNK)
            _, _, cum = _gla_gate(lr_ref[rows, :], wa_ref[...], ba_ref[...])
            last = cum[CHUNK - 1:CHUNK, :]
            e = jnp.exp(last - cum)
            dch = jnp.exp(last)
            for hh in range(B_HEADS):
                ks = slice(hh * hk, (hh + 1) * hk)
                q = p_ref[rows, hh * hk:(hh + 1) * hk].astype(F32)
                k = p_ref[rows, kd + hh * hk:kd + (hh + 1) * hk].astype(F32)
                v = p_ref[rows, 2 * kd + hh * hv:2 * kd + (hh + 1) * hv]
                rg = p_ref[rows, 4 * kd + hh * hv:4 * kd + (hh + 1) * hv].astype(F32)
                kdec = (k * e[:, ks]).astype(BF16)
                st = state[hh] * dch[:, ks] + _dot(v, kdec, "tn")
                state[hh] = st
                st_ref[sub, hh] = st
                o = _dot((q * qscale).astype(BF16), st.astype(BF16), "nt")
                rinv = lax.rsqrt(jnp.mean(o * o, axis=-1, keepdims=True) + RMS_EPS)
                yb_ref[rows, hh * hv:(hh + 1) * hv] = ((o * rinv * gn) * (rg * _sigmoid(rg))).astype(BF16)

    return _pcall(
        body, stages, name="gla_fwd",
        out_shape=(jax.ShapeDtypeStruct((s, 2 * kd), BF16), jax.ShapeDtypeStruct((nc, B_HEADS, hv, hk), F32)),
        grid=(nc // cps,),
        in_specs=[pl.BlockSpec((rows_per, 6 * kd), lambda i: (i, 0)), pl.BlockSpec((rows_per, LANES), lambda i: (i, 0)),
                  pl.BlockSpec((LANES, kd), lambda i: (0, 0)), pl.BlockSpec((1, kd), lambda i: (0, 0)),
                  pl.BlockSpec((1, hv), lambda i: (0, 0))],
        out_specs=(pl.BlockSpec((rows_per, 2 * kd), lambda i: (i, 0)),
                   pl.BlockSpec((cps, B_HEADS, hv, hk), lambda i: (i, 0, 0, 0))),
        scratch_shapes=[pltpu.VMEM((B_HEADS, hv, hk), F32)], compiler_params=_cp("arbitrary"),
    )(p2, lrp, wa2p, balpha, gnorm)


GLA_ROW_DBALPHA, GLA_ROW_DGNORM = 0, 1


def _gla_bwd(p2, lrp, wa2p, balpha, gnorm, states, dyb, stages=None):
    s = p2.shape[0]
    kd, hk, hv = _gla_dims(p2)
    nc = s // CHUNK
    cps = GLA_CPS if nc % GLA_CPS == 0 else 1
    rows_per = cps * CHUNK
    nblk = nc // cps
    qscale = hk ** -0.5

    def body(p_ref, lr_ref, wa_ref, ba_ref, gn_ref, st_ref, sp_ref, dy_ref, dp_ref, dz_ref, sm_ref, gcar):
        i = pl.program_id(0)

        @pl.when(i == 0)
        def _():
            gcar[...] = jnp.zeros_like(gcar)
            sm_ref[...] = jnp.zeros_like(sm_ref)

        block_has_prev = (i < nblk - 1).astype(F32)
        gn = gn_ref[...]
        row = lax.broadcasted_iota(jnp.int32, (CHUNK, CHUNK), 0)
        col = lax.broadcasted_iota(jnp.int32, (CHUNK, CHUNK), 1)
        tri_strict = (row > col).astype(F32)
        for sub in reversed(range(cps)):
            rows = slice(sub * CHUNK, (sub + 1) * CHUNK)
            z, _, cum = _gla_gate(lr_ref[rows, :], wa_ref[...], ba_ref[...])
            last = cum[CHUNK - 1:CHUNK, :]
            e = jnp.exp(last - cum)
            dch = jnp.exp(last)
            sgn = _sigmoid(-z) * (1.0 / GATE_TAU)
            for hh in range(B_HEADS):
                ks = slice(hh * hk, (hh + 1) * hk)
                q = p_ref[rows, hh * hk:(hh + 1) * hk].astype(F32)
                k = p_ref[rows, kd + hh * hk:kd + (hh + 1) * hk].astype(F32)
                v = p_ref[rows, 2 * kd + hh * hv:2 * kd + (hh + 1) * hv]
                rg = p_ref[rows, 4 * kd + hh * hv:4 * kd + (hh + 1) * hv].astype(F32)
                kdecf = k * e[:, ks]
                kdec = kdecf.astype(BF16)
                st16 = st_ref[sub, hh].astype(BF16)
                prev = st_ref[sub - 1, hh] if sub > 0 else sp_ref[hh] * block_has_prev
                qs = (q * qscale).astype(BF16)
                o = _dot(qs, st16, "nt")
                rinv = lax.rsqrt(jnp.mean(o * o, axis=-1, keepdims=True) + RMS_EPS)
                dy = dy_ref[rows, hh * hv:(hh + 1) * hv].astype(F32)
                sg = _sigmoid(rg)
                onorm = o * rinv
                drg = dy * (onorm * gn) * (sg * (1.0 + rg * (1.0 - sg)))
                dob = dy * (rg * sg)
                sm_ref[GLA_ROW_DGNORM:GLA_ROW_DGNORM + 1, 0:hv] += _colsum(dob * onorm)
                t = dob * gn
                do = rinv * (t - onorm * jnp.mean(t * onorm, axis=-1, keepdims=True))
                do16 = do.astype(BF16)
                dq = _dot(do16, st16) * qscale
                gt = _dot(do16, qs, "tn") + gcar[hh]
                gcar[hh] = gt * dch[:, ks]
                dd = _colsum(gt * prev)
                gt16 = gt.astype(BF16)
                dkdec = _dot(v, gt16)
                dv = _dot(kdec, gt16, "nt")
                dla = dd * dch[:, ks] + _dot(tri_strict, dkdec * kdecf, precision=HIGHEST)
                dzh = dla * sgn[:, ks]
                sm_ref[GLA_ROW_DBALPHA:GLA_ROW_DBALPHA + 1, hh * hk:(hh + 1) * hk] += _colsum(dzh)
                dz_ref[rows, hh * hk:(hh + 1) * hk] = dzh.astype(BF16)
                dp_ref[rows, hh * hk:(hh + 1) * hk] = dq.astype(BF16)
                dp_ref[rows, kd + hh * hk:kd + (hh + 1) * hk] = (dkdec * e[:, ks]).astype(BF16)
                dp_ref[rows, 2 * kd + hh * hv:2 * kd + (hh + 1) * hv] = dv.astype(BF16)
                dp_ref[rows, 4 * kd + hh * hv:4 * kd + (hh + 1) * hv] = drg.astype(BF16)

    rev = lambda i: (nblk - 1 - i, 0)
    return _pcall(
        body, stages, name="gla_bwd",
        out_shape=(jax.ShapeDtypeStruct((s, 6 * kd), BF16), jax.ShapeDtypeStruct((s, kd), BF16),
                   jax.ShapeDtypeStruct((SUBLANES, kd), F32)),
        grid=(nblk,),
        in_specs=[pl.BlockSpec((rows_per, 6 * kd), rev), pl.BlockSpec((rows_per, LANES), rev),
                  pl.BlockSpec((LANES, kd), lambda i: (0, 0)), pl.BlockSpec((1, kd), lambda i: (0, 0)),
                  pl.BlockSpec((1, hv), lambda i: (0, 0)),
                  pl.BlockSpec((cps, B_HEADS, hv, hk), lambda i: (nblk - 1 - i, 0, 0, 0)),
                  pl.BlockSpec((None, B_HEADS, hv, hk), lambda i: (jnp.maximum((nblk - 1 - i) * cps - 1, 0), 0, 0, 0)),
                  pl.BlockSpec((rows_per, 2 * kd), rev)],
        out_specs=(pl.BlockSpec((rows_per, 6 * kd), rev), pl.BlockSpec((rows_per, kd), rev),
                   pl.BlockSpec((SUBLANES, kd), lambda i: (0, 0))),
        scratch_shapes=[pltpu.VMEM((B_HEADS, hv, hk), F32)], compiler_params=_cp("arbitrary"),
    )(p2, lrp, wa2p, balpha, gnorm, states, states, dyb)


def _merge_fwd(ya, yb, wpa, wpb, g):
    s, ka = ya.shape
    kb = yb.shape[1]
    d = wpa.shape[1]
    tm, tn = _tile(s, (1024, 512)), _tile(d, (512,))

    def body(ya_ref, yb_ref, wa_ref, wb_ref, g_ref, m_ref, pab_ref):
        yav, ybv = ya_ref[...], yb_ref[...]
        for cols in _strips(tn):
            pa = _dot(yav, wa_ref[:, cols])
            pb = _dot(ybv, wb_ref[:, cols])
            m_ref[:, cols] = (_sigmoid(g_ref[0, :, cols].astype(F32)) * pa
                              + _sigmoid(g_ref[1, :, cols].astype(F32)) * pb).astype(BF16)
            pab_ref[0, :, cols] = pa.astype(BF16)
            pab_ref[1, :, cols] = pb.astype(BF16)

    st = pl.BlockSpec((2, tm, tn), lambda i, j: (0, i, j))
    return pl.pallas_call(
        body, name="merge_fwd",
        out_shape=(jax.ShapeDtypeStruct((s, d), BF16), jax.ShapeDtypeStruct((2, s, d), BF16)),
        grid=(s // tm, d // tn),
        in_specs=[pl.BlockSpec((tm, ka), lambda i, j: (i, 0)), pl.BlockSpec((tm, kb), lambda i, j: (i, 0)),
                  pl.BlockSpec((ka, tn), lambda i, j: (0, j)), pl.BlockSpec((kb, tn), lambda i, j: (0, j)), st],
        out_specs=(pl.BlockSpec((tm, tn), lambda i, j: (i, j)), st),
        compiler_params=_cp("parallel", "parallel"),
    )(ya, yb, wpa, wpb, g)


def _merge_bwd(dm, wmo, g, pab, stages=None):
    s, d = dm.shape
    tm, tn = _tile(s, (1024, 512)), _tile(d, (512,))

    def body(dm_ref, w_ref, g_ref, pab_ref, dpab_ref, dg_ref):
        dmv = dm_ref[...]
        for cols in _strips(tn):
            dmg = _dot(dmv, w_ref[cols, :], "nt")
            for j in range(2):
                sg = _sigmoid(g_ref[j, :, cols].astype(F32))
                dpab_ref[j, :, cols] = (dmg * sg).astype(BF16)
                dg_ref[j, :, cols] = (dmg * pab_ref[j, :, cols].astype(F32) * (sg * (1.0 - sg))).astype(BF16)

    st = pl.BlockSpec((2, tm, tn), lambda i, j: (0, i, j))
    return _pcall(
        body, stages, name="merge_bwd",
        out_shape=(jax.ShapeDtypeStruct((2, s, d), BF16), jax.ShapeDtypeStruct((2, s, d), BF16)),
        grid=(s // tm, d // tn),
        in_specs=[pl.BlockSpec((tm, d), lambda i, j: (i, 0)), pl.BlockSpec((tn, d), lambda i, j: (j, 0)), st, st],
        out_specs=(st, st), compiler_params=_cp("arbitrary", "arbitrary"),
    )(dm, wmo, g, pab)


def _virtual_rows(parts, lo, hi):
    out, off = [], 0
    for p in parts:
        a, b = max(lo, off), min(hi, off + p.shape[0])
        if a < b:
            out.append(p[a - off:b - off])
        off += p.shape[0]
    return out[0] if len(out) == 1 else jnp.concatenate(out, axis=0)


def _mix_in_row_groups(d):
    o1 = 3 * A_HEADS * A_HEAD_DIM
    o2 = o1 + 6 * (d // 4)
    o3 = o2 + GATE_RANK
    return (0, o1), (o1, o2), (o2, o3), (o3, o3 + 2 * d)


def _split_mix_in(stacked):
    d = stacked.shape[2]
    flat = stacked.reshape(-1, d)
    _, _, (lo, hi), (glo, ghi) = _mix_in_row_groups(d)
    return flat, jnp.pad(flat[lo:hi], ((0, LANES - GATE_RANK), (0, 0))), flat[glo:ghi]


MIX_TILE = 1024


def _mix_in_weights(plan):
    return plan.memo("mix_in_weights", lambda: _split_mix_in(plan.weight("w_mix_in")))


def _hosted_mm(plan):
    return lambda name, *a, **k: plan.host(name, lambda st: _mm(name, *a, stages=st, **k))


def _mix_forward(u2, plan, small):
    s, d = u2.shape
    wt, wt_lr, wt_g = _mix_in_weights(plan)
    bias, wa2p, balpha, gnorm = small
    mm = _hosted_mm(plan)
    aw = A_HEADS * A_HEAD_DIM
    tm, tn = _tile(s, (1024,)), MIX_TILE
    (_, na), (_, nab) = _mix_in_row_groups(d)[:2]
    assert na % tn == 0 and nab % tn == 0
    p1 = mm("mix_in_a", "nt", u2, wt, (s, na, d), tm=tm, tn=tn, tk=d, out_dtype=BF16)
    p2 = mm("mix_in_b", "nt", u2, wt, (s, nab - na, d), tm=tm, tn=tn, tk=d, out_dtype=BF16,
            b_spec=pl.BlockSpec((tn, d), lambda i, j, kk: (na // tn + j, 0)))
    lrp = mm("mix_in_lr", "nt", u2, wt_lr, (s, LANES, d), tm=tm, tn=LANES, tk=d, out_dtype=BF16)
    nbg = d // tn
    g = mm("mix_in_g", "nt", u2, wt_g, (s, 2 * d, d), tm=tm, tn=tn, tk=d, out_dtype=BF16, out_shape=(2, s, d),
           o_spec=pl.BlockSpec((None, tm, tn), lambda i, j, kk: (j // nbg, i, j % nbg)))
    kvp = jnp.pad(p1[:, aw:], ((A_PAD, 0), (0, 0)))
    ya = plan.host("attn_fwd", lambda st: _attn_fwd(p1, kvp, bias, st))
    yb, states = plan.host("gla_fwd", lambda st: _gla_fwd(p2, lrp, wa2p, balpha, gnorm, st))
    merged, pab = _merge_fwd(ya, yb, plan.weight("w_proj_a"), plan.weight("w_proj_b"), g)
    m = mm("mix_out", "nn", merged, plan.weight("w_mix_out"), (s, d, d), tm=tm, tn=tn, tk=d)
    return m, (p1, kvp, p2, lrp, states, ya, yb, g, pab, merged)


def _mix_backward(dm, u2, saved, plan, small):
    s, d = u2.shape
    wt, wt_lr, wt_g = _mix_in_weights(plan)
    wpa, wpb, wmo = plan.weight("w_proj_a"), plan.weight("w_proj_b"), plan.weight("w_mix_out")
    bias, wa2p, balpha, gnorm = small
    p1, kvp, p2, lrp, states, ya, yb, g, pab, merged = saved
    mm = _hosted_mm(plan)
    aw = A_HEADS * A_HEAD_DIM
    kd = d // 4
    t = MIX_TILE
    tm = _tile(s, (1024,))
    tks = _tile(s, LONG_K)

    plan.grad("w_mix_out", mm("mix_dw_out", "tn", merged, dm, (d, d, s), tm=t, tn=t, tk=tks))
    dpab, dg = plan.host("merge_bwd", lambda st: _merge_bwd(dm, wmo, g, pab, st))
    sel = lambda j: pl.BlockSpec((None, tm, d), lambda i, jj, kk: (j, i, 0))
    dya = mm("mix_dya", "nt", dpab, wpa, (s, aw, d), tm=tm, tn=t, tk=d, out_dtype=BF16, a_spec=sel(0))
    dyb = mm("mix_dyb", "nt", dpab, wpb, (s, 2 * kd, d), tm=tm, tn=t, tk=d, out_dtype=BF16, a_spec=sel(1))
    selk = lambda j: pl.BlockSpec((None, tks, t), lambda i, jj, kk: (j, kk, jj))
    plan.grad("w_proj_a", mm("mix_dwpa", "tn", ya, dpab, (aw, d, s), tm=t, tn=t, tk=tks, b_spec=selk(0)))
    plan.grad("w_proj_b", mm("mix_dwpb", "tn", yb, dpab, (2 * kd, d, s), tm=t, tn=t, tk=tks, b_spec=selk(1)))

    dq, dkp, dvp, dbias = plan.host("attn_bwd", lambda st: _attn_bwd(p1, kvp, bias, dya, st))
    dp1 = jnp.concatenate([dq, dkp[A_PAD:].astype(BF16), dvp[A_PAD:].astype(BF16)], axis=1)
    dp2, dz, gsm = plan.host("gla_bwd", lambda st: _gla_bwd(p2, lrp, wa2p, balpha, gnorm, states, dyb, st))
    dlrp = mm("gla_dlr", "nt", dz, wa2p, (s, LANES, kd), tm=tm, tn=LANES, tk=kd, out_dtype=BF16)
    dwa2p = mm("gla_dwa2", "tn", lrp, dz, (LANES, kd, s), tm=LANES, tn=kd, tk=tks)

    tka = 3 * aw
    assert 6 * kd == tka
    du = mm("mix_du_a", "nn", dp1, wt, (s, d, tka), tm=tm, tn=t, tk=tka)
    du = mm("mix_du_b", "nn", dp2, wt, (s, d, tka), tm=tm, tn=t, tk=tka, add=du,
            b_spec=pl.BlockSpec((tka, t), lambda i, j, kk: (1 + kk, j)))
    du = mm("mix_du_g", "nn", dg, wt_g, (s, d, 2 * d), tm=tm, tn=t, tk=d, add=du, thin=(dlrp, wt_lr),
            a_spec=pl.BlockSpec((None, tm, d), lambda i, j, kk: (kk, i, 0)))
    nkg = d // t
    dw1 = mm("mix_dw_a", "tn", dp1, u2, (3 * aw, d, s), tm=t, tn=t, tk=tks)
    dw2 = mm("mix_dw_b", "tn", dp2, u2, (6 * kd, d, s), tm=t, tn=t, tk=tks)
    dwlr = mm("mix_dw_lr", "tn", dlrp, u2, (LANES, d, s), tm=LANES, tn=t, tk=tks)
    dwg = mm("mix_dw_g", "tn", dg, u2, (2 * d, d, s), tm=t, tn=t, tk=tks,
             a_spec=pl.BlockSpec((None, tks, t), lambda i, j, kk: (i // nkg, kk, i % nkg)))
    pieces = [dw1, dw2, dwlr[:GATE_RANK], dwg]
    shard_rows = sum(p.shape[0] for p in pieces) // N_CHIPS
    plan.grad("w_mix_in", jnp.stack([_virtual_rows(pieces, j * shard_rows, (j + 1) * shard_rows)
                                     for j in range(N_CHIPS)]))
    return du, (dbias, dwa2p[:GATE_RANK], gsm)


def _device_step(x, target, mod, small, plan):
    s, d = x.shape
    row = lambda i: mod[i:i + 1]
    sh1, sc1, g1, sh2, sc2, g2, sh3, sc3, g3 = (row(i) for i in range(N_MOD))

    onehot = _rel_onehot()
    bias = _mm("rel_bias_expand", "nn", small["rel_bias"], onehot, (A_HEADS, CHUNK * A_BAND, REL_SIZE),
               tm=A_HEADS, tn=4608, tk=REL_SIZE, precision=HIGHEST).reshape(A_HEADS, CHUNK, A_BAND)
    bias = _band_bias(bias)
    wa2p = jnp.pad(small["w_alpha2"], ((0, LANES - GATE_RANK), (0, 0))).astype(BF16)
    mix_small = (bias, wa2p, small["b_alpha"], small["gla_norm_g"])

    u1, u1t = _modulate("mod1", x, sh1, sc1)
    f1, sv1 = _ffn_forward("ffn1", u1, plan)
    h1, u2 = _resid_ln_fwd("ln1_fwd", x, f1, g1, small["ln1_g"], small["ln1_b"], sh2, sc2, 0.5)
    m, svm = _mix_forward(u2, plan, mix_small)
    h2, u3, u3t = _resid_ln_fwd("ln2_fwd", h1, m, g2, small["ln2_g"], small["ln2_b"], sh3, sc3, 1.0, transposed=True)
    f2, sv2 = _ffn_forward("ffn2", u3, plan)

    dr3, df2, acc3 = _final_ln_loss_bwd("ln3_loss_bwd", h2, f2, target, g3, small["ln3_g"], small["ln3_b"], 0.5)
    du3 = _ffn_backward("ffn2", df2, u3t, sv2, plan, in_first=False)
    dr2, dmx, acc2 = _resid_ln_bwd("ln2_bwd", du3, dr3, h1, m, sc3, g2, small["ln2_g"], small["ln2_b"], 1.0)
    du2, (dbias, dwa2, gsm) = _mix_backward(dmx, u2, svm, plan, mix_small)
    dr1, df1, acc1 = _resid_ln_bwd("ln1_bwd", du2, dr2, x, f1, sc2, g1, small["ln1_g"], small["ln1_b"], 0.5)
    du1 = _ffn_backward("ffn1", df1, u1t, sv1, plan, in_first=True)
    grad_x, acc0 = _input_grad("input_grad", du1, dr1, x, sc1)

    drel = _hosted_mm(plan)("rel_bias_grad", "nt", _band_bias_grad(dbias).reshape(A_HEADS, CHUNK * A_BAND), onehot,
                            (A_HEADS, REL_SIZE, CHUNK * A_BAND), tm=A_HEADS, tn=REL_SIZE, tk=4608, precision=HIGHEST)
    loss = jnp.sum(acc3[ROW_LOSS])
    dmod = jnp.stack([acc0[ROW_DSH], acc0[ROW_DSC], acc1[ROW_DGATE], acc1[ROW_DSH], acc1[ROW_DSC], acc2[ROW_DGATE],
                      acc2[ROW_DSH], acc2[ROW_DSC], acc3[ROW_DGATE]])
    kd = d // 4
    small_grads = dict(ln1_g=acc1[ROW_DLN_G], ln1_b=acc1[ROW_DLN_B], ln2_g=acc2[ROW_DLN_G], ln2_b=acc2[ROW_DLN_B],
                       ln3_g=acc3[ROW_DLN_G], ln3_b=acc3[ROW_DLN_B], b_alpha=gsm[GLA_ROW_DBALPHA],
                       gla_norm_g=gsm[GLA_ROW_DGNORM, :kd // B_HEADS * 2], rel_bias=drel, w_alpha2=dwa2)
    return loss, grad_x, small_grads, dmod


HBM_SPEC = pl.BlockSpec(memory_space=pl.ANY)


def _mesh_pos():
    return lax.axis_index("x"), lax.axis_index("y"), lax.axis_index("c")


def _other_chips(x, y):
    return [(1 - x, y), (x, 1 - y), (1 - x, 1 - y)]


def _remote(src, dst, send_sem, recv_sem, to):
    return pltpu.make_async_remote_copy(src_ref=src, dst_ref=dst, send_sem=send_sem, recv_sem=recv_sem,
                                        device_id=to, device_id_type=MESH)


def _allgather_rows(name, v):
    m_per, n = v.shape

    def body(x_ref, out_ref, send_sems, recv_sems, local_sem):
        x, y, c = _mesh_pos()
        me, sibling = (x, y, c), (x, y, 1 - c)
        chips = _other_chips(x, y)

        def rows(px, py, pc):
            return out_ref.at[pl.ds((4 * px + 2 * py + pc) * m_per, m_per), :]

        def copy(k, block, to, src=None):
            return _remote(rows(*block) if src is None else src, rows(*block), send_sems.at[k], recv_sems.at[k], to)

        mine = pltpu.make_async_copy(x_ref, rows(*me), local_sem)
        mine.start()
        first = [copy(0, me, sibling, src=x_ref)]
        first += [copy(1 + j, me, (*chip, c), src=x_ref) for j, chip in enumerate(chips)]
        for cp in first:
            cp.start()
        passed = [copy(4 + j, (*chip, c), sibling) for j, chip in enumerate(chips)]
        for j, chip in enumerate(chips):
            copy(1 + j, (*chip, c), me).wait_recv()
            passed[j].start()
        copy(0, sibling, me).wait_recv()
        for j, chip in enumerate(chips):
            copy(4 + j, (*chip, 1 - c), me).wait_recv()
        for cp in first + passed:
            cp.wait_send()
        mine.wait()

    return pl.pallas_call(
        body, name=name, out_shape=jax.ShapeDtypeStruct((N_DEV * m_per, n), v.dtype),
        in_specs=[pl.BlockSpec(memory_space=pltpu.VMEM)], out_specs=pl.BlockSpec(memory_space=pltpu.VMEM),
        scratch_shapes=[pltpu.SemaphoreType.DMA((7,)), pltpu.SemaphoreType.DMA((7,)), pltpu.SemaphoreType.DMA],
    )(v)


def _allgather_weights(bufs):
    n = len(bufs)
    TO_X, TO_Y, PASS_TO_X, PASS_TO_Y, SIB_X, SIB_Y, SIB_D0, SIB_D1 = range(8)

    def body(*refs):
        ins, outs = refs[:n], refs[n:2 * n]
        send_sems, recv_sems = refs[2 * n:]
        x, y, c = _mesh_pos()
        sibling = (x, y, 1 - c)
        xn, yn, dg = _other_chips(x, y)
        j0, jx, jy, jd = (2 * p[0] + p[1] for p in ((x, y), xn, yn, dg))
        sends = []

        def rows(w, hc, quarter=None):
            hr = bufs[w].shape[1] // 2
            if quarter is None:
                return pl.ds(hc * hr, hr)
            return pl.ds(hc * hr + quarter * (hr // 2), hr // 2)

        def push(src, dst, w, k, to):
            cp = _remote(src, dst, send_sems.at[w, k], recv_sems.at[w, k], to)
            cp.start()
            sends.append(cp)

        def landed(piece, w, k):
            _remote(piece, piece, send_sems.at[w, k], recv_sems.at[w, k], sibling).wait_recv()

        for w in range(n):
            mine = rows(w, c)
            push(ins[w].at[j0, mine, :], outs[w].at[j0, mine, :], w, TO_X, (*xn, c))
            push(ins[w].at[j0, mine, :], outs[w].at[j0, mine, :], w, TO_Y, (*yn, c))
        for w in range(n):
            half_x = outs[w].at[jx, rows(w, c), :]
            landed(half_x, w, TO_X)
            quarter = outs[w].at[jx, rows(w, c, 1), :]
            push(quarter, quarter, w, PASS_TO_Y, (*yn, c))
            push(half_x, half_x, w, SIB_X, sibling)
            half_y = outs[w].at[jy, rows(w, c), :]
            landed(half_y, w, TO_Y)
            quarter = outs[w].at[jy, rows(w, c, 0), :]
            push(quarter, quarter, w, PASS_TO_X, (*xn, c))
            push(half_y, half_y, w, SIB_Y, sibling)
        for w in range(n):
            for q, arrives_on, on in ((0, PASS_TO_X, SIB_D0), (1, PASS_TO_Y, SIB_D1)):
                piece = outs[w].at[jd, rows(w, c, q), :]
                landed(piece, w, arrives_on)
                push(piece, piece, w, on, sibling)
        for w in range(n):
            landed(outs[w].at[jx, rows(w, 1 - c), :], w, SIB_X)
            landed(outs[w].at[jy, rows(w, 1 - c), :], w, SIB_Y)
            landed(outs[w].at[jd, rows(w, 1 - c, 0), :], w, SIB_D0)
            landed(outs[w].at[jd, rows(w, 1 - c, 1), :], w, SIB_D1)
        for cp in sends:
            cp.wait_send()

    return pl.pallas_call(
        body, name="allgather_weights", out_shape=[jax.ShapeDtypeStruct(b.shape, b.dtype) for b in bufs],
        in_specs=[HBM_SPEC] * n, out_specs=[HBM_SPEC] * n, input_output_aliases={w: w for w in range(n)},
        scratch_shapes=[pltpu.SemaphoreType.DMA((n, 8)), pltpu.SemaphoreType.DMA((n, 8))],
    )(*bufs)


def _half(ref, hc, col, *lead):
    rows, cols = ref.shape[-2:]
    if col:
        return ref.at[(*lead, slice(None), pl.ds(hc * (cols // 2), cols // 2))]
    return ref.at[(*lead, pl.ds(hc * (rows // 2), rows // 2), slice(None))]


def _half_shape(shape, col):
    return shape[:-2] + ((shape[-2], shape[-1] // 2) if col else (shape[-2] // 2, shape[-1]))


def _quarter(ref, hc, q, col, *lead):
    rows, cols = ref.shape[-2:]
    if col:
        return ref.at[(*lead, slice(None), pl.ds(hc * (cols // 2) + q * (cols // 4), cols // 4))]
    return ref.at[(*lead, pl.ds(hc * (rows // 2) + q * (rows // 4), rows // 4), slice(None))]


def _stage_gather_ici(bufs, cols):
    n = len(bufs)
    TO_X, TO_Y, PASS_TO_X, PASS_TO_Y = range(4)

    def places():
        x, y, c = _mesh_pos()
        xn, yn, dg = _other_chips(x, y)
        return c, (*xn, c), (*yn, c), [2 * p[0] + p[1] for p in ((x, y), xn, yn, dg)]

    def remote(src, dst, send, recv, w, k, to):
        return _remote(src, dst, send.at[4 * w + k], recv.at[4 * w + k], to)

    def own(ins, outs, send, recv):
        c, to_x, to_y, (j0, _, _, _) = places()
        for w in range(n):
            for k, to in ((TO_X, to_x), (TO_Y, to_y)):
                yield remote(_half(ins[w], c, cols[w], j0), _half(outs[w], c, cols[w], j0), send, recv, w, k, to)

    def relays(ins, outs, send, recv):
        c, to_x, to_y, (_, jx, jy, _) = places()
        for w in range(n):
            for j, k, q, pass_k, to in ((jx, TO_X, 1, PASS_TO_Y, to_y), (jy, TO_Y, 0, PASS_TO_X, to_x)):
                half = _half(outs[w], c, cols[w], j)
                piece = _quarter(outs[w], c, q, cols[w], j)
                yield remote(half, half, send, recv, w, k, to), remote(piece, piece, send, recv, w, pass_k, to)

    def passed(ins, outs, send, recv):
        c, to_x, _, (_, _, _, jd) = places()
        for w in range(n):
            for q, k in ((0, PASS_TO_X), (1, PASS_TO_Y)):
                piece = _quarter(outs[w], c, q, cols[w], jd)
                yield remote(piece, piece, send, recv, w, k, to_x)

    def start(*refs):
        for cp in own(*refs):
            cp.start()

    def relay(*refs):
        for arrived, onward in relays(*refs):
            arrived.wait_recv()
            onward.start()

    def finish(*refs):
        for cp in passed(*refs):
            cp.wait_recv()
        for cp in own(*refs):
            cp.wait_send()
        for _, onward in relays(*refs):
            onward.wait_send()

    outs = [jax.ShapeDtypeStruct(b.shape, b.dtype) for b in bufs]
    return _Stage(bufs, outs, 4 * n, start, finish, aliases={w: w for w in range(n)}, relay=relay)


def _stage_gather_d2d(partial, cols):
    n = len(partial)

    def copies(ins, outs, send, recv):
        x, y, c = _mesh_pos()
        for w in range(n):
            for r, chip in enumerate(_other_chips(x, y)):
                jr = 2 * chip[0] + chip[1]
                mine = _remote(_half(ins[w], c, cols[w], jr), _half(outs[w], c, cols[w], jr), send.at[3 * w + r],
                               recv.at[3 * w + r], (x, y, 1 - c))
                got = _half(outs[w], 1 - c, cols[w], jr)
                yield mine, _remote(got, got, send.at[3 * w + r], recv.at[3 * w + r], (x, y, 1 - c))

    def start(*refs):
        for mine, _ in copies(*refs):
            mine.start()

    def finish(*refs):
        pairs = list(copies(*refs))
        for _, theirs in pairs:
            theirs.wait_recv()
        for mine, _ in pairs:
            mine.wait_send()

    outs = [jax.ShapeDtypeStruct(p.shape, p.dtype) for p in partial]
    return _Stage(partial, outs, 3 * n, start, finish, aliases={w: w for w in range(n)})


def _stage_exchange_halves(grads, cols):
    n = len(grads)

    def copies(ins, outs, send, recv):
        x, y, c = _mesh_pos()
        for w in range(n):
            yield _remote(_half(ins[w], 1 - c, cols[w], slice(None)), outs[w], send.at[w], recv.at[w], (x, y, 1 - c))

    def start(*refs):
        for cp in copies(*refs):
            cp.start()

    def finish(*refs):
        cps = list(copies(*refs))
        for cp in cps:
            cp.wait_recv()
        for cp in cps:
            cp.wait_send()

    outs = [jax.ShapeDtypeStruct(_half_shape(g.shape, col), g.dtype) for g, col in zip(grads, cols)]
    return _Stage(grads, outs, n, start, finish)


def _stage_scatter(parts):
    n = len(parts)

    def copies(ins, outs, send, recv):
        x, y, c = _mesh_pos()
        for w in range(n):
            for r, chip in enumerate(_other_chips(x, y)):
                jr = 2 * chip[0] + chip[1]
                yield _remote(ins[w].at[jr], outs[w].at[r], send.at[3 * w + r], recv.at[3 * w + r], (*chip, c))

    def start(*refs):
        for cp in copies(*refs):
            cp.start()

    def finish(*refs):
        cps = list(copies(*refs))
        for cp in cps:
            cp.wait_recv()
        for cp in cps:
            cp.wait_send()

    outs = [jax.ShapeDtypeStruct((3,) + p.shape[1:], p.dtype) for p in parts]
    return _Stage(parts, outs, 3 * n, start, finish)


def _stage_share(fulls, cols):
    n = len(fulls)

    def copies(ins, outs, send, recv):
        x, y, c = _mesh_pos()
        for w in range(n):
            theirs = _half(outs[w], 1 - c, cols[w])
            yield (_remote(_half(ins[w], c, cols[w]), _half(outs[w], c, cols[w]), send.at[w], recv.at[w], (x, y, 1 - c)),
                   _remote(theirs, theirs, send.at[w], recv.at[w], (x, y, 1 - c)))

    def start(*refs):
        for mine, _ in copies(*refs):
            mine.start()

    def finish(*refs):
        pairs = list(copies(*refs))
        for _, theirs in pairs:
            theirs.wait_recv()
        for mine, _ in pairs:
            mine.wait_send()

    outs = [jax.ShapeDtypeStruct(h.shape, h.dtype) for h in fulls]
    return _Stage(fulls, outs, n, start, finish, aliases={w: w for w in range(n)})


def _run_stages(name, stages):
    return _pcall(None, stages, name=name, out_shape=[], in_specs=[], out_specs=[])()[1]


TILE_BYTES = 2 * 1024 * 1024
SUM_TILE_BYTES = 4 * 1024 * 1024


def _row_tile(rows, cols, itemsize=4, tile_bytes=TILE_BYTES):
    for t in range(min(rows, tile_bytes // (cols * itemsize)) // SUBLANES * SUBLANES, 0, -SUBLANES):
        if rows % t == 0:
            return t
    return rows


def _col_tile(rows, cols, itemsize=4, tile_bytes=TILE_BYTES):
    for t in (2048, 1024, 512, 256, 128):
        if cols % t == 0 and t * rows * itemsize <= tile_bytes:
            return t
    return cols


def _tiling(rows, cols, col, tile_bytes=TILE_BYTES):
    if col:
        tc = _col_tile(rows, cols, tile_bytes=tile_bytes)
        return (rows, tc), cols // tc
    tr = _row_tile(rows, cols, tile_bytes=tile_bytes)
    return (tr, cols), rows // tr


def _strip(col, i):
    return (0, i) if col else (i, 0)


def _pair_sum(name, g, recv, core, col):
    blk, nb = _tiling(*recv.shape[1:], col, tile_bytes=SUM_TILE_BYTES)

    def body(c_ref, g_ref, r_ref, o_ref):
        o_ref[...] = (g_ref[...] + r_ref[...]).astype(BF16)

    grid_spec = pltpu.PrefetchScalarGridSpec(
        num_scalar_prefetch=1, grid=(N_CHIPS, nb),
        in_specs=[pl.BlockSpec((None,) + blk, lambda j, i, cr: (j,) + _strip(col, cr[0] * nb + i)),
                  pl.BlockSpec((None,) + blk, lambda j, i, cr: (j,) + _strip(col, i))],
        out_specs=pl.BlockSpec((None,) + blk, lambda j, i, cr: (j,) + _strip(col, i)))
    return pl.pallas_call(body, name=name, out_shape=jax.ShapeDtypeStruct(recv.shape, BF16), grid_spec=grid_spec,
                          compiler_params=_cp("parallel", "parallel"))(core, g, recv)


def _quad_sum(name, own, landed, chip_core, col):
    rows, cols = landed.shape[1:]
    blk, nb = _tiling(rows, cols, col, tile_bytes=SUM_TILE_BYTES)
    full = (rows, 2 * cols) if col else (2 * rows, cols)

    def body(cc_ref, own_ref, l_ref, o_ref):
        o_ref[...] = ((own_ref[...].astype(F32) + l_ref[0].astype(F32)) + l_ref[1].astype(F32)) + l_ref[2].astype(F32)

    grid_spec = pltpu.PrefetchScalarGridSpec(
        num_scalar_prefetch=1, grid=(nb,),
        in_specs=[pl.BlockSpec((None,) + blk, lambda i, cc: (cc[0],) + _strip(col, i)),
                  pl.BlockSpec((3,) + blk, lambda i, cc: (0,) + _strip(col, i))],
        out_specs=pl.BlockSpec(blk, lambda i, cc: _strip(col, cc[1] * nb + i)))
    return pl.pallas_call(body, name=name, out_shape=jax.ShapeDtypeStruct(full, F32), grid_spec=grid_spec,
                          compiler_params=_cp("arbitrary"))(chip_core, own, landed)


def _device_sum(name, gathered):
    def body(g_ref, o_ref):
        total = g_ref[0]
        for k in range(1, N_DEV):
            total = total + g_ref[k]
        o_ref[...] = total

    return pl.pallas_call(body, name=name, out_shape=jax.ShapeDtypeStruct(gathered.shape[1:], F32))(gathered)


def _adamw(name, w, g, m, v):
    rows, cols = w.shape
    col = rows % SUBLANES != 0
    blk, nb = _tiling(rows, cols, col)
    bc1 = 1.0 - ADAM_B1 ** ADAM_STEP
    bc2 = 1.0 - ADAM_B2 ** ADAM_STEP

    def body(w_ref, g_ref, m_ref, v_ref, d_ref, mo_ref, vo_ref):
        gv = g_ref[...]
        mn = ADAM_B1 * m_ref[...] + (1.0 - ADAM_B1) * gv
        vn = ADAM_B2 * v_ref[...] + (1.0 - ADAM_B2) * (gv * gv)
        mo_ref[...] = mn
        vo_ref[...] = vn
        d_ref[...] = -ADAM_LR * ((mn / bc1) / (jnp.sqrt(vn / bc2) + ADAM_EPS) + ADAM_WD * w_ref[...])

    spec = pl.BlockSpec(blk, lambda i: _strip(col, i))
    return pl.pallas_call(
        body, name=name, out_shape=[jax.ShapeDtypeStruct((rows, cols), F32)] * 3, grid=(nb,),
        in_specs=[spec] * 4, out_specs=[spec] * 3, compiler_params=_cp("parallel"),
    )(w, g, m, v)


WEIGHTS = ["w_ada", "b_ada", "ffn1_w_in", "ffn1_w_out", "ln1_g", "ln1_b", "w_mix_in", "rel_bias", "w_alpha2",
           "b_alpha", "gla_norm_g", "w_proj_a", "w_proj_b", "w_mix_out", "ln2_g", "ln2_b", "ffn2_w_in", "ffn2_w_out",
           "ln3_g", "ln3_b"]
BIG = {"ffn1_w_in": True, "ffn1_w_out": False, "w_mix_in": False, "w_proj_a": True, "w_proj_b": True,
       "w_mix_out": False, "ffn2_w_in": True, "ffn2_w_out": False}
TRANSPOSED = ("w_mix_in",)
STACKED = ("ffn1_w_in", "ffn2_w_in", "w_mix_in")
GROUP_FFN1 = ("ffn1_w_in", "ffn1_w_out")
GROUP_PROJ = ("w_proj_a", "w_proj_b", "w_mix_out")
SMALL = ["ln1_g", "ln1_b", "ln2_g", "ln2_b", "ln3_g", "ln3_b", "b_alpha", "gla_norm_g", "rel_bias", "w_alpha2"]


def _pad_rows(vec, rows=SUBLANES):
    per = -(-vec.shape[0] // (rows * LANES)) * LANES
    return jnp.pad(vec, (0, rows * per - vec.shape[0])).reshape(rows, per)


def _silu(v):
    return v * _sigmoid(v)


class _MeshPlan:
    def __init__(self, shards, chip, core):
        self.shapes = {k: v.shape for k, v in shards.items()}
        self.slots = {k: lax.dynamic_update_slice(lax.empty((N_CHIPS,) + v.shape, v.dtype), v[None], (chip, 0, 0))
                      for k, v in shards.items()}
        self.core1 = core.astype(jnp.int32).reshape(1)
        self.chip_core = jnp.stack([chip, core]).astype(jnp.int32)
        self.partial, self.full, self.local, self.pair, self.half, self.final, self.memos = {}, {}, {}, {}, {}, {}, {}
        ici, d2d, x1, x2, x3 = self.gather_ici, self.gather_d2d, self.exchange, self.scatter, self.share
        mix_in, in1, out1, in2, out2 = ("w_mix_in",), ("ffn1_w_in",), ("ffn1_w_out",), ("ffn2_w_in",), ("ffn2_w_out",)
        self.schedule = {
            "ffn1_in_fwd": [ici(mix_in)], "ffn1_out_fwd": [d2d(mix_in), ici(out2)],
            "mix_in_a": [d2d(out2)], "mix_in_g": [ici(GROUP_PROJ)],
            "attn_fwd": [ici(in2), d2d(GROUP_PROJ)], "gla_fwd": [d2d(in2)],
            "ffn2_dw_in": [x1(out2)], "ffn2_du": [x2(out2), x1(in2)], "mix_dw_out": [x3(out2)],
            "attn_bwd": [x2(in2)], "gla_bwd": [x3(in2), x1(GROUP_PROJ)],
            "mix_du_g": [x2(GROUP_PROJ)], "mix_dw_g": [x3(GROUP_PROJ)],
            "ffn1_out_bwd": [x1(mix_in)], "ffn1_dw_in": [x2(mix_in)], "ffn1_dw_out": [x3(mix_in), x1(in1)],
            "ffn1_du": [x2(in1), x1(out1)], "rel_bias_grad": [x2(out1), x3(in1)],
        }

    def weight(self, k):
        return self.full[k]

    def grad(self, k, g):
        r, cc = self.shapes[k]
        if k not in STACKED:
            g = g.reshape(r, N_CHIPS, cc).transpose(1, 0, 2) if BIG[k] else g.reshape(N_CHIPS, r, cc)
        self.local[k] = g

    def memo(self, key, make):
        if key not in self.memos:
            self.memos[key] = make()
        return self.memos[key]

    def host(self, name, call):
        builders = self.schedule.get(name)
        if not builders:
            return call(None)
        built = [b() for b in builders]
        main, comm = call([st for st, _ in built])
        for (_, post), res in zip(built, comm):
            post(res)
        return main

    def run(self, name, builders):
        built = [b() for b in builders]
        for (_, post), res in zip(built, _run_stages(name, [st for st, _ in built])):
            post(res)

    def set_gathered(self, names, gathered):
        for k, g in zip(names, gathered):
            _, r, cc = g.shape
            if k not in STACKED:
                g = g.transpose(1, 0, 2).reshape(r, N_CHIPS * cc) if BIG[k] else g.reshape(N_CHIPS * r, cc)
            self.full[k] = g

    @staticmethod
    def cols(names):
        return [k in TRANSPOSED for k in names]

    def gather_ici(self, names):
        def post(res):
            self.partial.update(zip(names, res))
        return lambda: (_stage_gather_ici([self.slots[k] for k in names], self.cols(names)), post)

    def gather_d2d(self, names):
        return lambda: (_stage_gather_d2d([self.partial[k] for k in names], self.cols(names)),
                        lambda res: self.set_gathered(names, res))

    def exchange(self, names):
        def post(res):
            for k, r in zip(names, res):
                self.pair[k] = _pair_sum(f"pair_sum_{k}", self.local[k], r, self.core1, k in TRANSPOSED)
        return lambda: (_stage_exchange_halves([self.local[k] for k in names], self.cols(names)), post)

    def scatter(self, names):
        def post(res):
            for k, landed in zip(names, res):
                self.half[k] = _quad_sum(f"quad_sum_{k}", self.pair[k], landed, self.chip_core, k in TRANSPOSED)
        return lambda: (_stage_scatter([self.pair[k] for k in names]), post)

    def share(self, names):
        def post(res):
            self.final.update(zip(names, res))
        return lambda: (_stage_share([self.half[k] for k in names], self.cols(names)), post)


def _step(args):
    x_pos, y_pos, c_pos = _mesh_pos()
    chip = 2 * x_pos + y_pos
    dev = 4 * x_pos + 2 * y_pos + c_pos
    take = lambda name, k: args[name][0].T if k in TRANSPOSED else args[name][0]
    w = {k: take(k, k) for k in WEIGHTS}
    mom = {k: take("m_" + k, k) for k in WEIGHTS}
    vel = {k: take("v_" + k, k) for k in WEIGHTS}
    x = args["x"][0]
    target = args["loss_target"][0]
    s, d = x.shape
    kd = d // 4
    rel_sh = w["rel_bias"].shape[1]
    wa2_sh = w["w_alpha2"].shape[1]
    ada_sh = w["w_ada"].shape[1]

    n_rel, n_wa2 = A_HEADS * rel_sh, GATE_RANK * wa2_sh
    packed = _pad_rows(jnp.concatenate([args["c"].reshape(-1), w["rel_bias"].reshape(-1), w["w_alpha2"].reshape(-1)]))
    got = _allgather_rows("gather_small_inputs", packed).reshape(N_DEV, -1)
    c_all = got[:, :d]
    per_chip = got[0::2]
    rel_bias = per_chip[:, d:d + n_rel].reshape(N_CHIPS, A_HEADS, rel_sh).transpose(1, 0, 2).reshape(A_HEADS, -1)
    w_alpha2 = per_chip[:, d + n_rel:d + n_rel + n_wa2].reshape(N_CHIPS, GATE_RANK, wa2_sh).transpose(1, 0, 2)
    w_alpha2 = w_alpha2.reshape(GATE_RANK, -1)

    b_shard = lax.dynamic_slice(w["b_ada"], (chip * ada_sh,), (ada_sh,))
    mod_shard = _mm("ada_fwd", "nn", c_all, w["w_ada"], (N_DEV, ada_sh, d), tm=N_DEV, tn=_tile(ada_sh, (512, 128)),
                    tk=d, precision=HIGHEST, a_fn=_silu, add=jnp.broadcast_to(b_shard[None], (N_DEV, ada_sh)))
    mod_all = _allgather_rows("gather_mod", mod_shard).reshape(N_DEV, N_DEV, ada_sh)[0::2]
    mod_all = mod_all.transpose(1, 0, 2).reshape(N_DEV, N_MOD * d)
    mod = lax.dynamic_index_in_dim(mod_all, dev, 0, keepdims=False).reshape(N_MOD, d)

    names = list(BIG)
    plan = _MeshPlan({k: w[k].astype(BF16) for k in names}, chip, c_pos)
    plan.set_gathered(GROUP_FFN1, _allgather_weights([plan.slots[k] for k in GROUP_FFN1]))

    small = dict(rel_bias=rel_bias, w_alpha2=w_alpha2, b_alpha=w["b_alpha"][None], gla_norm_g=w["gla_norm_g"][None])
    for k in ("ln1_g", "ln1_b", "ln2_g", "ln2_b", "ln3_g", "ln3_b"):
        small[k] = w[k][None]
    loss_local, grad_x, small_grads, dmod = _device_step(x, target, mod, small, plan)
    loss = lax.psum(loss_local, ("x", "y", "c"))
    plan.run("grad_tail_share", [plan.share(GROUP_FFN1[1:])])

    flat = jnp.concatenate([small_grads[k].reshape(-1) for k in SMALL] + [dmod.reshape(-1)])
    n_small = flat.shape[0] - N_MOD * d
    packed = _pad_rows(flat)
    all_small = _allgather_rows("gather_small_grads", packed).reshape(N_DEV, SUBLANES, -1)
    summed = _device_sum("small_grad_sum", all_small).reshape(-1)
    dmod_all = all_small.reshape(N_DEV, -1)[:, n_small:n_small + N_MOD * d]
    dmod_shard = lax.dynamic_slice(dmod_all, (0, chip * ada_sh), (N_DEV, ada_sh))
    grads = {"b_ada": summed[n_small:n_small + N_MOD * d]}
    off = 0
    for k in SMALL:
        size = small_grads[k].size
        grads[k] = summed[off:off + size].reshape(small_grads[k].shape)
        off += size
    grads["rel_bias"] = lax.dynamic_slice(grads["rel_bias"], (0, chip * rel_sh), (A_HEADS, rel_sh))
    grads["w_alpha2"] = lax.dynamic_slice(grads["w_alpha2"], (0, chip * wa2_sh), (GATE_RANK, wa2_sh))
    grads["w_ada"] = _mm("ada_bwd", "nn", jnp.pad(c_all.T, ((0, 0), (0, LANES - N_DEV))),
                         jnp.pad(dmod_shard, ((0, LANES - N_DEV), (0, 0))), (d, ada_sh, LANES), tm=_tile(d, (1024,)),
                         tn=_tile(ada_sh, (512, 128)), tk=LANES, precision=HIGHEST, a_fn=_silu)

    grads.update(plan.final)

    delta, new_m, new_v = {}, {}, {}
    for k in ["w_ada"] + names:
        delta[k], new_m[k], new_v[k] = _adamw(f"adamw_{k}", w[k], grads[k], mom[k], vel[k])
    tiny = ["b_ada"] + SMALL
    pack = lambda src: _pad_rows(jnp.concatenate([src[k].reshape(-1) for k in tiny]), rows=1).reshape(-1, LANES)
    outs = _adamw("adamw_small", pack(w), pack(grads), pack(mom), pack(vel))
    off = 0
    for k in tiny:
        size = w[k].size
        for dst, src in zip((delta, new_m, new_v), outs):
            dst[k] = src.reshape(-1)[off:off + size].reshape(w[k].shape)
        off += size

    give = lambda src: [src[k].T[None] if k in TRANSPOSED else src[k][None] for k in WEIGHTS]
    return (loss, grad_x[None], *give(grads), *give(delta), *give(new_m), *give(new_v))


def kernel(x, c, w_ada, b_ada, ffn1_w_in, ffn1_w_out, ln1_g, ln1_b, w_mix_in, rel_bias, w_alpha2, b_alpha, gla_norm_g, w_proj_a, w_proj_b, w_mix_out, ln2_g, ln2_b, ffn2_w_in, ffn2_w_out, ln3_g, ln3_b, loss_target, m_w_ada, m_b_ada, m_ffn1_w_in, m_ffn1_w_out, m_ln1_g, m_ln1_b, m_w_mix_in, m_rel_bias, m_w_alpha2, m_b_alpha, m_gla_norm_g, m_w_proj_a, m_w_proj_b, m_w_mix_out, m_ln2_g, m_ln2_b, m_ffn2_w_in, m_ffn2_w_out, m_ln3_g, m_ln3_b, v_w_ada, v_b_ada, v_ffn1_w_in, v_ffn1_w_out, v_ln1_g, v_ln1_b, v_w_mix_in, v_rel_bias, v_w_alpha2, v_b_alpha, v_gla_norm_g, v_w_proj_a, v_w_proj_b, v_w_mix_out, v_ln2_g, v_ln2_b, v_ffn2_w_in, v_ffn2_w_out, v_ln3_g, v_ln3_b):
    return _step(dict(locals()))
```

```python
import functools

import jax
import jax.numpy as jnp
from jax import lax
from jax.experimental import pallas as pl
from jax.experimental.pallas import tpu as pltpu

F32 = jnp.float32
BF16 = jnp.bfloat16
MESH = pl.DeviceIdType.MESH
HIGHEST = lax.Precision.HIGHEST

VMEM_LIMIT_BYTES = 56 * 1024 * 1024
LANES = 128
SUBLANES = 8

CHUNK = 64
A_HEADS = 16
A_HEAD_DIM = 64
A_PAST_CHUNKS = 8
A_BAND = (A_PAST_CHUNKS + 1) * CHUNK
A_PAD = A_PAST_CHUNKS * CHUNK
REL_CLIP = 256
REL_SIZE = REL_CLIP + CHUNK
B_HEADS = 4
GATE_RANK = 16
GATE_TAU = 16.0
N_MOD = 9
DEPTH = 1
ALPHA = (2.0 * DEPTH) ** 0.25
LN_EPS = 1e-5
RMS_EPS = 1e-6
ADAM_LR = 0.001
ADAM_B1 = 0.9
ADAM_B2 = 0.999
ADAM_EPS = 1e-08
ADAM_WD = 0.01
ADAM_STEP = 10
NEG_BIG = -1e30

N_CHIPS = 4
N_DEV = 8


def _cp(*sem):
    return pltpu.CompilerParams(dimension_semantics=sem, vmem_limit_bytes=VMEM_LIMIT_BYTES)


class _Stage:
    def __init__(self, arrays, out_shapes, n_sems, start, finish, aliases=None, relay=None):
        self.arrays, self.out_shapes, self.n_sems = list(arrays), list(out_shapes), n_sems
        self.start, self.finish, self.relay, self.aliases = start, finish, relay, dict(aliases or {})


def _pcall(body, stages, *, name, out_shape, in_specs, out_specs, grid=(), scratch_shapes=(), compiler_params=None):
    single = not isinstance(out_shape, (list, tuple))
    outs = [out_shape] if single else list(out_shape)
    ospecs = [out_specs] if single else list(out_specs)
    in_specs, scratch_shapes = list(in_specs), list(scratch_shapes)
    n_in, n_out, n_sc = len(in_specs), len(outs), len(scratch_shapes)
    stages = list(stages or [])
    c_in = [a for st in stages for a in st.arrays]
    c_out = [o for st in stages for o in st.out_shapes]
    aliases = {}
    io, oo = n_in, n_out
    for st in stages:
        for a, b in st.aliases.items():
            aliases[io + a] = oo + b
        io += len(st.arrays)
        oo += len(st.out_shapes)

    def wrapped(*refs):
        ins = refs[:n_in]
        cins = refs[n_in:n_in + len(c_in)]
        base = n_in + len(c_in)
        mouts = refs[base:base + n_out]
        couts = refs[base + n_out:base + n_out + len(c_out)]
        base += n_out + len(c_out)
        scr = refs[base:base + n_sc]
        sems = refs[base + n_sc:]

        def each(phase):
            i = o = 0
            for k, st in enumerate(stages):
                fn = (st.start, st.relay, st.finish)[phase]
                if fn is not None:
                    fn(cins[i:i + len(st.arrays)], couts[o:o + len(st.out_shapes)], sems[2 * k], sems[2 * k + 1])
                i += len(st.arrays)
                o += len(st.out_shapes)

        if stages and grid:
            step = functools.reduce(lambda acc, a: acc * grid[a] + pl.program_id(a), range(len(grid)), 0)
            steps = functools.reduce(lambda a, b: a * b, grid)
            pl.when(step == 0)(lambda: each(0))
            if any(st.relay for st in stages):
                pl.when(step == (2 * steps) // 3)(lambda: each(1))
            if body is not None:
                body(*ins, *mouts, *scr)
            pl.when(step == steps - 1)(lambda: each(2))
        else:
            each(0)
            each(1)
            if body is not None:
                body(*ins, *mouts, *scr)
            each(2)

    sem_shapes = []
    for st in stages:
        sem_shapes += [pltpu.SemaphoreType.DMA((st.n_sems,)), pltpu.SemaphoreType.DMA((st.n_sems,))]
    kwargs = dict(grid=grid) if grid else {}
    if compiler_params is not None:
        kwargs["compiler_params"] = compiler_params

    def run(*operands):
        res = pl.pallas_call(
            wrapped, name=name, out_shape=outs + c_out, in_specs=in_specs + [HBM_SPEC] * len(c_in),
            out_specs=ospecs + [HBM_SPEC] * len(c_out), scratch_shapes=scratch_shapes + sem_shapes,
            input_output_aliases=aliases, **kwargs)(*operands, *c_in)
        main = res[0] if single else tuple(res[:n_out])
        if not stages:
            return main
        comm, o = [], n_out
        for st in stages:
            comm.append(list(res[o:o + len(st.out_shapes)]))
            o += len(st.out_shapes)
        return main, comm

    return run


LONG_K = (2048, 1024)


def _tile(n, prefs):
    for t in prefs:
        if t <= n and n % t == 0:
            return t
    return n


_DIMS = {"nn": (((1,), (0,)), ((), ())), "nt": (((1,), (1,)), ((), ())), "tn": (((0,), (0,)), ((), ()))}


def _dot(a, b, mode="nn", precision=None):
    return lax.dot_general(a, b, _DIMS[mode], precision=precision, preferred_element_type=F32)


def _sigmoid(x):
    return 0.5 * jnp.tanh(0.5 * x) + 0.5


EPILOGUE_STRIP = 256


def _strips(n, width=EPILOGUE_STRIP):
    width = width if n % width == 0 else n
    return [slice(j, j + width) for j in range(0, n, width)]


def _mm(name, mode, a, b, mnk, *, tm, tn, tk, out_dtype=F32, precision=None, a_spec=None, b_spec=None,
        out_shape=None, o_spec=None, add=None, a_fn=None, thin=None, stages=None):
    m, n, k = mnk
    assert m % tm == 0 and n % tn == 0 and k % tk == 0, (name, mnk, tm, tn, tk)
    nk = k // tk
    if a_spec is None:
        a_spec = {"nn": pl.BlockSpec((tm, tk), lambda i, j, kk: (i, kk)),
                  "nt": pl.BlockSpec((tm, tk), lambda i, j, kk: (i, kk)),
                  "tn": pl.BlockSpec((tk, tm), lambda i, j, kk: (kk, i))}[mode]
    if b_spec is None:
        b_spec = {"nn": pl.BlockSpec((tk, tn), lambda i, j, kk: (kk, j)),
                  "nt": pl.BlockSpec((tn, tk), lambda i, j, kk: (j, kk)),
                  "tn": pl.BlockSpec((tk, tn), lambda i, j, kk: (kk, j))}[mode]
    if o_spec is None:
        o_spec = pl.BlockSpec((tm, tn), lambda i, j, kk: (i, j))
    if out_shape is None:
        out_shape = (m, n)
    has_add = add is not None
    n_in = 2 + has_add + (2 if thin else 0)

    def body(*refs):
        a_ref, b_ref = refs[0], refs[1]
        add_ref = refs[2] if has_add else None
        o_ref = refs[n_in]
        av = a_ref[...]
        if a_fn is not None:
            av = a_fn(av)
        part = _dot(av, b_ref[...], mode, precision)

        def finish(total):
            if has_add:
                total = total + add_ref[...]
            if thin:
                total = total + _dot(refs[n_in - 2][...], refs[n_in - 1][...])
            o_ref[...] = total.astype(out_dtype)

        if nk == 1:
            finish(part)
        else:
            acc_ref = refs[-1]
            kk = pl.program_id(2)

            @pl.when(kk == 0)
            def _():
                acc_ref[...] = part

            @pl.when(kk > 0)
            def _():
                acc_ref[...] += part

            @pl.when(kk == nk - 1)
            def _():
                finish(acc_ref[...])

    in_specs = [a_spec, b_spec]
    operands = [a, b]
    if has_add:
        in_specs.append(pl.BlockSpec((tm, tn), lambda i, j, kk: (i, j)))
        operands.append(add)
    if thin:
        k2 = thin[0].shape[1]
        in_specs += [pl.BlockSpec((tm, k2), lambda i, j, kk: (i, 0)), pl.BlockSpec((k2, tn), lambda i, j, kk: (0, j))]
        operands += list(thin)
    return _pcall(
        body, stages, name=name, out_shape=jax.ShapeDtypeStruct(out_shape, out_dtype), grid=(m // tm, n // tn, nk),
        in_specs=in_specs, out_specs=o_spec,
        scratch_shapes=[pltpu.VMEM((tm, tn), F32)] if nk > 1 else [],
        compiler_params=_cp("arbitrary", "arbitrary", "arbitrary") if stages else _cp("parallel", "parallel", "arbitrary"),
    )(*operands)


def _row_spec(tr, d):
    return pl.BlockSpec((tr, d), lambda i: (i, 0))


def _vec_spec(d, rows=1):
    return pl.BlockSpec((rows, d), lambda i: (0, 0))


def _col_spec(d, tr):
    return pl.BlockSpec((d, tr), lambda i: (0, i))


def _modulate(name, x, sh, sc):
    s, d = x.shape
    tr = _tile(s, (512, 256))

    def body(x_ref, sh_ref, sc_ref, o_ref, ot_ref):
        u = x_ref[...] * (1.0 + sc_ref[...]) + sh_ref[...]
        o_ref[...] = u.astype(BF16)
        ot_ref[...] = u.T.astype(BF16)

    return pl.pallas_call(
        body, name=name, out_shape=(jax.ShapeDtypeStruct((s, d), BF16), jax.ShapeDtypeStruct((d, s), BF16)),
        grid=(s // tr,), in_specs=[_row_spec(tr, d), _vec_spec(d), _vec_spec(d)],
        out_specs=(_row_spec(tr, d), _col_spec(d, tr)), compiler_params=_cp("parallel"),
    )(x, sh, sc)


def _ln_stats(r):
    mu = jnp.mean(r, axis=-1, keepdims=True)
    xc = r - mu
    var = jnp.mean(xc * xc, axis=-1, keepdims=True)
    rstd = lax.rsqrt(var + LN_EPS)
    return xc * rstd, rstd


def _resid_ln_fwd(name, x, f, gate, ln_g, ln_b, sh_n, sc_n, coef, transposed=False):
    s, d = x.shape
    tr = _tile(s, (256,))

    def body(x_ref, f_ref, gate_ref, g_ref, b_ref, sh_ref, sc_ref, h_ref, u_ref, *ut_ref):
        r = ALPHA * x_ref[...] + (coef * gate_ref[...]) * f_ref[...]
        xhat, _ = _ln_stats(r)
        h = xhat * g_ref[...] + b_ref[...]
        h_ref[...] = h
        u = h * (1.0 + sc_ref[...]) + sh_ref[...]
        u_ref[...] = u.astype(BF16)
        if transposed:
            ut_ref[0][...] = u.T.astype(BF16)

    extra_shape = (jax.ShapeDtypeStruct((d, s), BF16),) if transposed else ()
    extra_spec = (_col_spec(d, tr),) if transposed else ()
    return pl.pallas_call(
        body, name=name,
        out_shape=(jax.ShapeDtypeStruct((s, d), F32), jax.ShapeDtypeStruct((s, d), BF16)) + extra_shape,
        grid=(s // tr,), in_specs=[_row_spec(tr, d), _row_spec(tr, d)] + [_vec_spec(d)] * 5,
        out_specs=(_row_spec(tr, d), _row_spec(tr, d)) + extra_spec, compiler_params=_cp("parallel"),
    )(x, f, gate, ln_g, ln_b, sh_n, sc_n)


ROW_DSC, ROW_DSH, ROW_DLN_G, ROW_DLN_B, ROW_DGATE, ROW_LOSS = 0, 1, 2, 3, 4, 5


def _ln_bwd_core(dy, xhat, rstd, ln_g):
    dxhat = dy * ln_g
    m1 = jnp.mean(dxhat, axis=-1, keepdims=True)
    m2 = jnp.mean(dxhat * xhat, axis=-1, keepdims=True)
    return rstd * (dxhat - m1 - xhat * m2)


def _colsum(v):
    return jnp.sum(v, axis=0, keepdims=True)


def _final_ln_loss_bwd(name, x, f, target, gate, ln_g, ln_b, coef):
    s, d = x.shape
    tr = _tile(s, (256,))
    inv_d = 1.0 / d

    def body(x_ref, f_ref, t_ref, gate_ref, g_ref, b_ref, dr_ref, df_ref, acc_ref):
        @pl.when(pl.program_id(0) == 0)
        def _():
            acc_ref[...] = jnp.zeros_like(acc_ref)

        fv = f_ref[...]
        r = ALPHA * x_ref[...] + (coef * gate_ref[...]) * fv
        xhat, rstd = _ln_stats(r)
        h = xhat * g_ref[...] + b_ref[...]
        err = h - t_ref[...]
        dy = err * inv_d
        dr = _ln_bwd_core(dy, xhat, rstd, g_ref[...])
        dr_ref[...] = dr
        df_ref[...] = ((coef * gate_ref[...]) * dr).astype(BF16)
        acc_ref[ROW_DLN_G:ROW_DLN_G + 1, :] += _colsum(dy * xhat)
        acc_ref[ROW_DLN_B:ROW_DLN_B + 1, :] += _colsum(dy)
        acc_ref[ROW_DGATE:ROW_DGATE + 1, :] += _colsum((coef * dr) * fv)
        acc_ref[ROW_LOSS:ROW_LOSS + 1, :] += _colsum(err * err) * (0.5 * inv_d)

    return pl.pallas_call(
        body, name=name,
        out_shape=(jax.ShapeDtypeStruct((s, d), F32), jax.ShapeDtypeStruct((s, d), BF16),
                   jax.ShapeDtypeStruct((SUBLANES, d), F32)),
        grid=(s // tr,), in_specs=[_row_spec(tr, d)] * 3 + [_vec_spec(d)] * 3,
        out_specs=(_row_spec(tr, d), _row_spec(tr, d), _vec_spec(d, SUBLANES)),
        compiler_params=_cp("arbitrary"),
    )(x, f, target, gate, ln_g, ln_b)


def _resid_ln_bwd(name, du_n, dr_n, x, f, sc_n, gate, ln_g, ln_b, coef):
    s, d = x.shape
    tr = _tile(s, (256,))

    def body(du_ref, drn_ref, x_ref, f_ref, sc_ref, gate_ref, g_ref, b_ref, dr_ref, df_ref, acc_ref):
        @pl.when(pl.program_id(0) == 0)
        def _():
            acc_ref[...] = jnp.zeros_like(acc_ref)

        fv = f_ref[...]
        du = du_ref[...]
        r = ALPHA * x_ref[...] + (coef * gate_ref[...]) * fv
        xhat, rstd = _ln_stats(r)
        h = xhat * g_ref[...] + b_ref[...]
        dy = du * (1.0 + sc_ref[...]) + ALPHA * drn_ref[...]
        dr = _ln_bwd_core(dy, xhat, rstd, g_ref[...])
        dr_ref[...] = dr
        df_ref[...] = ((coef * gate_ref[...]) * dr).astype(BF16)
        acc_ref[ROW_DSC:ROW_DSC + 1, :] += _colsum(du * h)
        acc_ref[ROW_DSH:ROW_DSH + 1, :] += _colsum(du)
        acc_ref[ROW_DLN_G:ROW_DLN_G + 1, :] += _colsum(dy * xhat)
        acc_ref[ROW_DLN_B:ROW_DLN_B + 1, :] += _colsum(dy)
        acc_ref[ROW_DGATE:ROW_DGATE + 1, :] += _colsum((coef * dr) * fv)

    return pl.pallas_call(
        body, name=name,
        out_shape=(jax.ShapeDtypeStruct((s, d), F32), jax.ShapeDtypeStruct((s, d), BF16),
                   jax.ShapeDtypeStruct((SUBLANES, d), F32)),
        grid=(s // tr,), in_specs=[_row_spec(tr, d)] * 4 + [_vec_spec(d)] * 4,
        out_specs=(_row_spec(tr, d), _row_spec(tr, d), _vec_spec(d, SUBLANES)),
        compiler_params=_cp("arbitrary"),
    )(du_n, dr_n, x, f, sc_n, gate, ln_g, ln_b)


def _input_grad(name, du, dr, x, sc):
    s, d = x.shape
    tr = _tile(s, (256,))

    def body(du_ref, dr_ref, x_ref, sc_ref, gx_ref, acc_ref):
        @pl.when(pl.program_id(0) == 0)
        def _():
            acc_ref[...] = jnp.zeros_like(acc_ref)

        du = du_ref[...]
        gx_ref[...] = du * (1.0 + sc_ref[...]) + ALPHA * dr_ref[...]
        acc_ref[ROW_DSC:ROW_DSC + 1, :] += _colsum(du * x_ref[...])
        acc_ref[ROW_DSH:ROW_DSH + 1, :] += _colsum(du)

    return pl.pallas_call(
        body, name=name,
        out_shape=(jax.ShapeDtypeStruct((s, d), F32), jax.ShapeDtypeStruct((SUBLANES, d), F32)),
        grid=(s // tr,), in_specs=[_row_spec(tr, d)] * 3 + [_vec_spec(d)],
        out_specs=(_row_spec(tr, d), _vec_spec(d, SUBLANES)), compiler_params=_cp("arbitrary"),
    )(du, dr, x, sc)


def _ffn_in_fwd(name, u, w_in, stages=None):
    s, d = u.shape
    cs = w_in.shape[2]
    f = 2 * cs
    tm, tn = _tile(s, (2048, 1024, 512)), _tile(cs, (256, 128))
    nb = f // tn
    nbs = cs // tn

    def body(u_ref, wa_ref, wb_ref, ab_ref, act_ref):
        for rows in _strips(tm, 512):
            uv = u_ref[rows, :]
            a = _dot(uv, wa_ref[...])
            b = _dot(uv, wb_ref[...])
            sg = _sigmoid(a)
            silu = a * sg
            ab_ref[0, rows, :] = (b * (sg + silu * (1.0 - sg))).astype(BF16)
            ab_ref[1, rows, :] = silu.astype(BF16)
            act_ref[rows, :] = (silu * b).astype(BF16)

    return _pcall(
        body, stages, name=name,
        out_shape=(jax.ShapeDtypeStruct((2, s, f), BF16), jax.ShapeDtypeStruct((s, f), BF16)),
        grid=(s // tm, nb),
        in_specs=[pl.BlockSpec((tm, d), lambda i, j: (i, 0)),
                  pl.BlockSpec((None, d, tn), lambda i, j: (j // nbs, 0, j % nbs)),
                  pl.BlockSpec((None, d, tn), lambda i, j: (2 + j // nbs, 0, j % nbs))],
        out_specs=(pl.BlockSpec((2, tm, tn), lambda i, j: (0, i, j)), pl.BlockSpec((tm, tn), lambda i, j: (i, j))),
        compiler_params=_cp("arbitrary", "arbitrary"),
    )(u, w_in, w_in)


def _ffn_out_bwd(name, df, w_out, ab, stages=None):
    s, d = df.shape
    f = w_out.shape[0]
    tm, tn = _tile(s, (1024, 512)), _tile(f, (512, 256, 128))

    def body(df_ref, w_ref, ab_ref, dab_ref):
        dfv = df_ref[...]
        for cols in _strips(tn):
            dact = _dot(dfv, w_ref[cols, :], "nt")
            dab_ref[0, :, cols] = (dact * ab_ref[0, :, cols].astype(F32)).astype(BF16)
            dab_ref[1, :, cols] = (dact * ab_ref[1, :, cols].astype(F32)).astype(BF16)

    return _pcall(
        body, stages, name=name, out_shape=jax.ShapeDtypeStruct((2, s, f), BF16), grid=(s // tm, f // tn),
        in_specs=[pl.BlockSpec((tm, d), lambda i, j: (i, 0)), pl.BlockSpec((tn, d), lambda i, j: (j, 0)),
                  pl.BlockSpec((2, tm, tn), lambda i, j: (0, i, j))],
        out_specs=pl.BlockSpec((2, tm, tn), lambda i, j: (0, i, j)),
        compiler_params=_cp("arbitrary", "arbitrary"),
    )(df, w_out, ab)


def _ffn_forward(tag, u, plan):
    w_in, w_out = plan.weight(f"{tag}_w_in"), plan.weight(f"{tag}_w_out")
    s, d = u.shape
    f = w_out.shape[0]
    ab, act = plan.host(f"{tag}_in_fwd", lambda st: _ffn_in_fwd(f"{tag}_in_fwd", u, w_in, st))
    out = plan.host(f"{tag}_out_fwd", lambda st: _mm(
        f"{tag}_out_fwd", "nn", act, w_out, (s, d, f), tm=_tile(s, (1024,)), tn=_tile(d, (1024,)),
        tk=_tile(f, (2816, 1408, 512, 128)), stages=st))
    return out, (ab, act)


def _ffn_backward(tag, df, ut, saved, plan, in_first):
    w_in, w_out = plan.weight(f"{tag}_w_in"), plan.weight(f"{tag}_w_out")
    ab, act = saved
    d, s = ut.shape
    f = w_out.shape[0]
    dab = plan.host(f"{tag}_out_bwd", lambda st: _ffn_out_bwd(f"{tag}_out_bwd", df, w_out, ab, st))
    cs = w_in.shape[2]
    tk = _tile(cs, (2816, 1408, 256, 128))
    nkh, nks = f // tk, cs // tk
    tmd = _tile(d, (1024,))
    tks = _tile(s, LONG_K)

    def dw_in():
        tw = _tile(cs, (256, 128))
        nwh, nws = f // tw, cs // tw
        plan.grad(f"{tag}_w_in", plan.host(f"{tag}_dw_in", lambda st: _mm(
            f"{tag}_dw_in", "nn", ut, dab, (d, 2 * f, s), tm=tmd, tn=tw, tk=s,
            b_spec=pl.BlockSpec((None, s, tw), lambda i, j, kk: (j // nwh, 0, j % nwh)), out_shape=(N_CHIPS, d, cs),
            o_spec=pl.BlockSpec((None, tmd, tw), lambda i, j, kk: (j // nws, i, j % nws)), stages=st)))

    def dw_out():
        plan.grad(f"{tag}_w_out", plan.host(f"{tag}_dw_out", lambda st: _mm(
            f"{tag}_dw_out", "tn", act, df, (f, d, s), tm=_tile(f, (1408, 512, 128)), tn=tmd, tk=tks, stages=st)))

    for step in ((dw_in, dw_out) if in_first else (dw_out, dw_in)):
        step()
    return plan.host(f"{tag}_du", lambda st: _mm(
        f"{tag}_du", "nt", dab, w_in, (s, d, 2 * f), tm=_tile(s, (1024,)), tn=tmd, tk=tk,
        a_spec=pl.BlockSpec((None, _tile(s, (1024,)), tk), lambda i, j, kk: (kk // nkh, i, kk % nkh)),
        b_spec=pl.BlockSpec((None, tmd, tk), lambda i, j, kk: (kk // nks, j, kk % nks)), stages=st))


ATTN_Q = 4 * CHUNK
ATTN_W = ATTN_Q + A_PAD


def _band_bias(bias):
    n = ATTN_Q // CHUNK
    rows = [jnp.pad(bias, ((0, 0), (0, 0), (i * CHUNK, (n - 1 - i) * CHUNK)), constant_values=NEG_BIG)
            for i in range(n)]
    return jnp.concatenate(rows, axis=1)


def _band_bias_grad(dband):
    n = ATTN_Q // CHUNK
    parts = [dband[:, i * CHUNK:(i + 1) * CHUNK, i * CHUNK:i * CHUNK + A_BAND] for i in range(n)]
    return functools.reduce(jnp.add, parts)


def _attn_probs(q, kw, bias, key0):
    sc = _dot(q, kw, "nt") * (A_HEAD_DIM ** -0.5) + bias
    ks = lax.broadcasted_iota(jnp.int32, sc.shape, 1)
    sc = jnp.where(key0 + ks >= 0, sc, NEG_BIG)
    p = jnp.exp(sc - jnp.max(sc, axis=-1, keepdims=True))
    return p / jnp.sum(p, axis=-1, keepdims=True)


def _head_masks():
    lane = lax.broadcasted_iota(jnp.int32, (1, LANES), 1)
    return [lane // A_HEAD_DIM == h for h in range(LANES // A_HEAD_DIM)]


def _attn_fwd(p1, kvp, band, stages=None):
    s = p1.shape[0]
    aw = A_HEADS * A_HEAD_DIM
    nblk = aw // LANES
    hpb = LANES // A_HEAD_DIM
    assert s % ATTN_Q == 0

    def body(q_ref, k_ref, v_ref, b_ref, o_ref):
        base = pl.multiple_of(pl.program_id(1) * ATTN_Q, ATTN_Q)
        qv = q_ref[...]
        kw = k_ref[pl.ds(base, ATTN_W), :]
        vw = v_ref[pl.ds(base, ATTN_W), :]
        out = jnp.zeros((ATTN_Q, LANES), F32)
        for h, mask in enumerate(_head_masks()):
            p = _attn_probs(jnp.where(mask, qv, jnp.zeros_like(qv)), kw, b_ref[h], base - A_PAD)
            out = jnp.where(mask, _dot(p.astype(BF16), vw), out)
        o_ref[...] = out.astype(BF16)

    kv_rows = s + A_PAD
    return _pcall(
        body, stages, name="attn_fwd", out_shape=jax.ShapeDtypeStruct((s, aw), BF16), grid=(nblk, s // ATTN_Q),
        in_specs=[pl.BlockSpec((ATTN_Q, LANES), lambda b, i: (i, b)),
                  pl.BlockSpec((kv_rows, LANES), lambda b, i: (0, b)),
                  pl.BlockSpec((kv_rows, LANES), lambda b, i: (0, nblk + b)),
                  pl.BlockSpec((hpb, ATTN_Q, ATTN_W), lambda b, i: (b, 0, 0))],
        out_specs=pl.BlockSpec((ATTN_Q, LANES), lambda b, i: (i, b)),
        compiler_params=_cp("arbitrary", "arbitrary"),
    )(p1, kvp, kvp, band)


def _attn_bwd(p1, kvp, band, dya, stages=None):
    s = p1.shape[0]
    aw = A_HEADS * A_HEAD_DIM
    nblk = aw // LANES
    hpb = LANES // A_HEAD_DIM
    scale = A_HEAD_DIM ** -0.5

    def body(q_ref, k_ref, v_ref, b_ref, do_ref, dq_ref, dk_ref, dv_ref, db_ref):
        @pl.when(pl.program_id(1) == 0)
        def _():
            dk_ref[...] = jnp.zeros_like(dk_ref)
            dv_ref[...] = jnp.zeros_like(dv_ref)
            db_ref[...] = jnp.zeros_like(db_ref)

        base = pl.multiple_of(pl.program_id(1) * ATTN_Q, ATTN_Q)
        window = pl.ds(base, ATTN_W)
        kw = k_ref[window, :]
        vw = v_ref[window, :]
        qv = q_ref[...]
        dov = do_ref[...]
        dq = jnp.zeros((ATTN_Q, LANES), F32)
        dk = jnp.zeros((ATTN_W, LANES), F32)
        dv = jnp.zeros((ATTN_W, LANES), F32)
        for h, mask in enumerate(_head_masks()):
            qh = jnp.where(mask, qv, jnp.zeros_like(qv))
            doh = jnp.where(mask, dov, jnp.zeros_like(dov))
            p = _attn_probs(qh, kw, b_ref[h], base - A_PAD)
            dp = _dot(doh, vw, "nt")
            ds = p * (dp - jnp.sum(p * dp, axis=-1, keepdims=True))
            db_ref[h] += ds
            dsb = (ds * scale).astype(BF16)
            dq = jnp.where(mask, _dot(dsb, kw), dq)
            dk = dk + _dot(dsb, qh, "tn")
            dv = dv + _dot(p.astype(BF16), doh, "tn")
        dq_ref[...] = dq.astype(BF16)
        dk_ref[window, :] += dk
        dv_ref[window, :] += dv

    kv_rows = s + A_PAD
    q_spec = pl.BlockSpec((ATTN_Q, LANES), lambda b, i: (i, b))
    acc_spec = pl.BlockSpec((kv_rows, LANES), lambda b, i: (0, b))
    b_spec = pl.BlockSpec((hpb, ATTN_Q, ATTN_W), lambda b, i: (b, 0, 0))
    return _pcall(
        body, stages, name="attn_bwd",
        out_shape=(jax.ShapeDtypeStruct((s, aw), BF16), jax.ShapeDtypeStruct((kv_rows, aw), F32),
                   jax.ShapeDtypeStruct((kv_rows, aw), F32), jax.ShapeDtypeStruct((A_HEADS, ATTN_Q, ATTN_W), F32)),
        grid=(nblk, s // ATTN_Q),
        in_specs=[q_spec, acc_spec, pl.BlockSpec((kv_rows, LANES), lambda b, i: (0, nblk + b)), b_spec, q_spec],
        out_specs=(q_spec, acc_spec, acc_spec, b_spec), compiler_params=_cp("arbitrary", "arbitrary"),
    )(p1, kvp, kvp, band, dya)


REL_TILE = CHUNK * A_BAND // 8


def _rel_onehot():
    qi = jnp.arange(CHUNK)[:, None]
    ks = jnp.arange(A_BAND)[None, :]
    idx = (jnp.clip(ks - A_PAD - qi, -REL_CLIP, CHUNK - 1) + REL_CLIP).reshape(1, CHUNK * A_BAND)
    return (jnp.arange(REL_SIZE)[:, None] == idx).astype(F32)


def _gla_gate(lr, wa2, balpha):
    z = _dot(lr, wa2) + balpha
    la = (jnp.minimum(z, 0.0) - jnp.log(1.0 + jnp.exp(-jnp.abs(z)))) * (1.0 / GATE_TAU)
    row = lax.broadcasted_iota(jnp.int32, (CHUNK, CHUNK), 0)
    col = lax.broadcasted_iota(jnp.int32, (CHUNK, CHUNK), 1)
    cum = _dot((row >= col).astype(F32), la, precision=HIGHEST)
    return z, la, cum


def _gla_dims(p2):
    kd = p2.shape[1] // 6
    hk = kd // B_HEADS
    hv = 2 * hk
    return kd, hk, hv


GLA_CPS = 4


def _gla_fwd(p2, lrp, wa2p, balpha, gnorm, stages=None):
    s = p2.shape[0]
    kd, hk, hv = _gla_dims(p2)
    nc = s // CHUNK
    cps = GLA_CPS if nc % GLA_CPS == 0 else 1
    rows_per = cps * CHUNK
    qscale = hk ** -0.5

    def body(p_ref, lr_ref, wa_ref, ba_ref, gn_ref, yb_ref, st_ref, state):
        @pl.when(pl.program_id(0) == 0)
        def _():
            state[...] = jnp.zeros_like(state)

        gn = gn_ref[...]
        for sub in range(cps):
            rows = slice(sub * CHUNK, (sub + 1) * CHUNK)
            _, _, cum = _gla_gate(lr_ref[rows, :], wa_ref[...], ba_ref[...])
            last = cum[CHUNK - 1:CHUNK, :]
            e = jnp.exp(last - cum)
            dch = jnp.exp(last)
            for hh in range(B_HEADS):
                ks = slice(hh * hk, (hh + 1) * hk)
                q = p_ref[rows, hh * hk:(hh + 1) * hk].astype(F32)
                k = p_ref[rows, kd + hh * hk:kd + (hh + 1) * hk].astype(F32)
                v = p_ref[rows, 2 * kd + hh * hv:2 * kd + (hh + 1) * hv]
                rg = p_ref[rows, 4 * kd + hh * hv:4 * kd + (hh + 1) * hv].astype(F32)
                kdec = (k * e[:, ks]).astype(BF16)
                st = state[hh] * dch[:, ks] + _dot(v, kdec, "tn")
                state[hh] = st
                st_ref[sub, hh] = st
                o = _dot((q * qscale).astype(BF16), st.astype(BF16), "nt")
                rinv = lax.rsqrt(jnp.mean(o * o, axis=-1, keepdims=True) + RMS_EPS)
                yb_ref[rows, hh * hv:(hh + 1) * hv] = ((o * rinv * gn) * (rg * _sigmoid(rg))).astype(BF16)

    return _pcall(
        body, stages, name="gla_fwd",
        out_shape=(jax.ShapeDtypeStruct((s, 2 * kd), BF16), jax.ShapeDtypeStruct((nc, B_HEADS, hv, hk), F32)),
        grid=(nc // cps,),
        in_specs=[pl.BlockSpec((rows_per, 6 * kd), lambda i: (i, 0)), pl.BlockSpec((rows_per, LANES), lambda i: (i, 0)),
                  pl.BlockSpec((LANES, kd), lambda i: (0, 0)), pl.BlockSpec((1, kd), lambda i: (0, 0)),
                  pl.BlockSpec((1, hv), lambda i: (0, 0))],
        out_specs=(pl.BlockSpec((rows_per, 2 * kd), lambda i: (i, 0)),
                   pl.BlockSpec((cps, B_HEADS, hv, hk), lambda i: (i, 0, 0, 0))),
        scratch_shapes=[pltpu.VMEM((B_HEADS, hv, hk), F32)], compiler_params=_cp("arbitrary"),
    )(p2, lrp, wa2p, balpha, gnorm)


GLA_ROW_DBALPHA, GLA_ROW_DGNORM = 0, 1


def _gla_bwd(p2, lrp, wa2p, balpha, gnorm, states, dyb, stages=None):
    s = p2.shape[0]
    kd, hk, hv = _gla_dims(p2)
    nc = s // CHUNK
    cps = GLA_CPS if nc % GLA_CPS == 0 else 1
    rows_per = cps * CHUNK
    nblk = nc // cps
    qscale = hk ** -0.5

    def body(p_ref, lr_ref, wa_ref, ba_ref, gn_ref, st_ref, sp_ref, dy_ref, dp_ref, dz_ref, sm_ref, gcar):
        i = pl.program_id(0)

        @pl.when(i == 0)
        def _():
            gcar[...] = jnp.zeros_like(gcar)
            sm_ref[...] = jnp.zeros_like(sm_ref)

        block_has_prev = (i < nblk - 1).astype(F32)
        gn = gn_ref[...]
        row = lax.broadcasted_iota(jnp.int32, (CHUNK, CHUNK), 0)
        col = lax.broadcasted_iota(jnp.int32, (CHUNK, CHUNK), 1)
        tri_strict = (row > col).astype(F32)
        for sub in reversed(range(cps)):
            rows = slice(sub * CHUNK, (sub + 1) * CHUNK)
            z, _, cum = _gla_gate(lr_ref[rows, :], wa_ref[...], ba_ref[...])
            last = cum[CHUNK - 1:CHUNK, :]
            e = jnp.exp(last - cum)
            dch = jnp.exp(last)
            sgn = _sigmoid(-z) * (1.0 / GATE_TAU)
            for hh in range(B_HEADS):
                ks = slice(hh * hk, (hh + 1) * hk)
                q = p_ref[rows, hh * hk:(hh + 1) * hk].astype(F32)
                k = p_ref[rows, kd + hh * hk:kd + (hh + 1) * hk].astype(F32)
                v = p_ref[rows, 2 * kd + hh * hv:2 * kd + (hh + 1) * hv]
                rg = p_ref[rows, 4 * kd + hh * hv:4 * kd + (hh + 1) * hv].astype(F32)
                kdecf = k * e[:, ks]
                kdec = kdecf.astype(BF16)
                st16 = st_ref[sub, hh].astype(BF16)
                prev = st_ref[sub - 1, hh] if sub > 0 else sp_ref[hh] * block_has_prev
                qs = (q * qscale).astype(BF16)
                o = _dot(qs, st16, "nt")
                rinv = lax.rsqrt(jnp.mean(o * o, axis=-1, keepdims=True) + RMS_EPS)
                dy = dy_ref[rows, hh * hv:(hh + 1) * hv].astype(F32)
                sg = _sigmoid(rg)
                onorm = o * rinv
                drg = dy * (onorm * gn) * (sg * (1.0 + rg * (1.0 - sg)))
                dob = dy * (rg * sg)
                sm_ref[GLA_ROW_DGNORM:GLA_ROW_DGNORM + 1, 0:hv] += _colsum(dob * onorm)
                t = dob * gn
                do = rinv * (t - onorm * jnp.mean(t * onorm, axis=-1, keepdims=True))
                do16 = do.astype(BF16)
                dq = _dot(do16, st16) * qscale
                gt = _dot(do16, qs, "tn") + gcar[hh]
                gcar[hh] = gt * dch[:, ks]
                dd = _colsum(gt * prev)
                gt16 = gt.astype(BF16)
                dkdec = _dot(v, gt16)
                dv = _dot(kdec, gt16, "nt")
                dla = dd * dch[:, ks] + _dot(tri_strict, dkdec * kdecf, precision=HIGHEST)
                dzh = dla * sgn[:, ks]
                sm_ref[GLA_ROW_DBALPHA:GLA_ROW_DBALPHA + 1, hh * hk:(hh + 1) * hk] += _colsum(dzh)
                dz_ref[rows, hh * hk:(hh + 1) * hk] = dzh.astype(BF16)
                dp_ref[rows, hh * hk:(hh + 1) * hk] = dq.astype(BF16)
                dp_ref[rows, kd + hh * hk:kd + (hh + 1) * hk] = (dkdec * e[:, ks]).astype(BF16)
                dp_ref[rows, 2 * kd + hh * hv:2 * kd + (hh + 1) * hv] = dv.astype(BF16)
                dp_ref[rows, 4 * kd + hh * hv:4 * kd + (hh + 1) * hv] = drg.astype(BF16)

    rev = lambda i: (nblk - 1 - i, 0)
    return _pcall(
        body, stages, name="gla_bwd",
        out_shape=(jax.ShapeDtypeStruct((s, 6 * kd), BF16), jax.ShapeDtypeStruct((s, kd), BF16),
                   jax.ShapeDtypeStruct((SUBLANES, kd), F32)),
        grid=(nblk,),
        in_specs=[pl.BlockSpec((rows_per, 6 * kd), rev), pl.BlockSpec((rows_per, LANES), rev),
                  pl.BlockSpec((LANES, kd), lambda i: (0, 0)), pl.BlockSpec((1, kd), lambda i: (0, 0)),
                  pl.BlockSpec((1, hv), lambda i: (0, 0)),
                  pl.BlockSpec((cps, B_HEADS, hv, hk), lambda i: (nblk - 1 - i, 0, 0, 0)),
                  pl.BlockSpec((None, B_HEADS, hv, hk), lambda i: (jnp.maximum((nblk - 1 - i) * cps - 1, 0), 0, 0, 0)),
                  pl.BlockSpec((rows_per, 2 * kd), rev)],
        out_specs=(pl.BlockSpec((rows_per, 6 * kd), rev), pl.BlockSpec((rows_per, kd), rev),
                   pl.BlockSpec((SUBLANES, kd), lambda i: (0, 0))),
        scratch_shapes=[pltpu.VMEM((B_HEADS, hv, hk), F32)], compiler_params=_cp("arbitrary"),
    )(p2, lrp, wa2p, balpha, gnorm, states, states, dyb)


def _merge_fwd(ya, yb, wpa, wpb, g):
    s, ka = ya.shape
    kb = yb.shape[1]
    d = wpa.shape[1]
    tm, tn = _tile(s, (1024, 512)), _tile(d, (512,))

    def body(ya_ref, yb_ref, wa_ref, wb_ref, g_ref, m_ref, pab_ref):
        yav, ybv = ya_ref[...], yb_ref[...]
        for cols in _strips(tn):
            pa = _dot(yav, wa_ref[:, cols])
            pb = _dot(ybv, wb_ref[:, cols])
            m_ref[:, cols] = (_sigmoid(g_ref[0, :, cols].astype(F32)) * pa
                              + _sigmoid(g_ref[1, :, cols].astype(F32)) * pb).astype(BF16)
            pab_ref[0, :, cols] = pa.astype(BF16)
            pab_ref[1, :, cols] = pb.astype(BF16)

    st = pl.BlockSpec((2, tm, tn), lambda i, j: (0, i, j))
    return pl.pallas_call(
        body, name="merge_fwd",
        out_shape=(jax.ShapeDtypeStruct((s, d), BF16), jax.ShapeDtypeStruct((2, s, d), BF16)),
        grid=(s // tm, d // tn),
        in_specs=[pl.BlockSpec((tm, ka), lambda i, j: (i, 0)), pl.BlockSpec((tm, kb), lambda i, j: (i, 0)),
                  pl.BlockSpec((ka, tn), lambda i, j: (0, j)), pl.BlockSpec((kb, tn), lambda i, j: (0, j)), st],
        out_specs=(pl.BlockSpec((tm, tn), lambda i, j: (i, j)), st),
        compiler_params=_cp("parallel", "parallel"),
    )(ya, yb, wpa, wpb, g)


def _merge_bwd(dm, wmo, g, pab, stages=None):
    s, d = dm.shape
    tm, tn = _tile(s, (1024, 512)), _tile(d, (512,))

    def body(dm_ref, w_ref, g_ref, pab_ref, dpab_ref, dg_ref):
        dmv = dm_ref[...]
        for cols in _strips(tn):
            dmg = _dot(dmv, w_ref[cols, :], "nt")
            for j in range(2):
                sg = _sigmoid(g_ref[j, :, cols].astype(F32))
                dpab_ref[j, :, cols] = (dmg * sg).astype(BF16)
                dg_ref[j, :, cols] = (dmg * pab_ref[j, :, cols].astype(F32) * (sg * (1.0 - sg))).astype(BF16)

    st = pl.BlockSpec((2, tm, tn), lambda i, j: (0, i, j))
    return _pcall(
        body, stages, name="merge_bwd",
        out_shape=(jax.ShapeDtypeStruct((2, s, d), BF16), jax.ShapeDtypeStruct((2, s, d), BF16)),
        grid=(s // tm, d // tn),
        in_specs=[pl.BlockSpec((tm, d), lambda i, j: (i, 0)), pl.BlockSpec((tn, d), lambda i, j: (j, 0)), st, st],
        out_specs=(st, st), compiler_params=_cp("arbitrary", "arbitrary"),
    )(dm, wmo, g, pab)


def _virtual_rows(parts, lo, hi):
    out, off = [], 0
    for p in parts:
        a, b = max(lo, off), min(hi, off + p.shape[0])
        if a < b:
            out.append(p[a - off:b - off])
        off += p.shape[0]
    return out[0] if len(out) == 1 else jnp.concatenate(out, axis=0)


def _mix_in_row_groups(d):
    o1 = 3 * A_HEADS * A_HEAD_DIM
    o2 = o1 + 6 * (d // 4)
    o3 = o2 + GATE_RANK
    return (0, o1), (o1, o2), (o2, o3), (o3, o3 + 2 * d)


def _split_mix_in(stacked):
    d = stacked.shape[2]
    flat = stacked.reshape(-1, d)
    _, _, (lo, hi), (glo, ghi) = _mix_in_row_groups(d)
    return flat, jnp.pad(flat[lo:hi], ((0, LANES - GATE_RANK), (0, 0))), flat[glo:ghi]


MIX_TILE = 1024


def _mix_in_weights(plan):
    return plan.memo("mix_in_weights", lambda: _split_mix_in(plan.weight("w_mix_in")))


def _hosted_mm(plan):
    return lambda name, *a, **k: plan.host(name, lambda st: _mm(name, *a, stages=st, **k))


def _mix_forward(u2, plan, small):
    s, d = u2.shape
    wt, wt_lr, wt_g = _mix_in_weights(plan)
    bias, wa2p, balpha, gnorm = small
    mm = _hosted_mm(plan)
    aw = A_HEADS * A_HEAD_DIM
    tm, tn = _tile(s, (1024,)), MIX_TILE
    (_, na), (_, nab) = _mix_in_row_groups(d)[:2]
    assert na % tn == 0 and nab % tn == 0
    p1 = mm("mix_in_a", "nt", u2, wt, (s, na, d), tm=tm, tn=tn, tk=d, out_dtype=BF16)
    p2 = mm("mix_in_b", "nt", u2, wt, (s, nab - na, d), tm=tm, tn=tn, tk=d, out_dtype=BF16,
            b_spec=pl.BlockSpec((tn, d), lambda i, j, kk: (na // tn + j, 0)))
    lrp = mm("mix_in_lr", "nt", u2, wt_lr, (s, LANES, d), tm=tm, tn=LANES, tk=d, out_dtype=BF16)
    nbg = d // tn
    g = mm("mix_in_g", "nt", u2, wt_g, (s, 2 * d, d), tm=tm, tn=tn, tk=d, out_dtype=BF16, out_shape=(2, s, d),
           o_spec=pl.BlockSpec((None, tm, tn), lambda i, j, kk: (j // nbg, i, j % nbg)))
    kvp = jnp.pad(p1[:, aw:], ((A_PAD, 0), (0, 0)))
    ya = plan.host("attn_fwd", lambda st: _attn_fwd(p1, kvp, bias, st))
    yb, states = plan.host("gla_fwd", lambda st: _gla_fwd(p2, lrp, wa2p, balpha, gnorm, st))
    merged, pab = _merge_fwd(ya, yb, plan.weight("w_proj_a"), plan.weight("w_proj_b"), g)
    m = mm("mix_out", "nn", merged, plan.weight("w_mix_out"), (s, d, d), tm=tm, tn=tn, tk=d)
    return m, (p1, kvp, p2, lrp, states, ya, yb, g, pab, merged)


def _mix_backward(dm, u2, saved, plan, small):
    s, d = u2.shape
    wt, wt_lr, wt_g = _mix_in_weights(plan)
    wpa, wpb, wmo = plan.weight("w_proj_a"), plan.weight("w_proj_b"), plan.weight("w_mix_out")
    bias, wa2p, balpha, gnorm = small
    p1, kvp, p2, lrp, states, ya, yb, g, pab, merged = saved
    mm = _hosted_mm(plan)
    aw = A_HEADS * A_HEAD_DIM
    kd = d // 4
    t = MIX_TILE
    tm = _tile(s, (1024,))
    tks = _tile(s, LONG_K)

    plan.grad("w_mix_out", mm("mix_dw_out", "tn", merged, dm, (d, d, s), tm=t, tn=t, tk=tks))
    dpab, dg = plan.host("merge_bwd", lambda st: _merge_bwd(dm, wmo, g, pab, st))
    sel = lambda j: pl.BlockSpec((None, tm, d), lambda i, jj, kk: (j, i, 0))
    dya = mm("mix_dya", "nt", dpab, wpa, (s, aw, d), tm=tm, tn=t, tk=d, out_dtype=BF16, a_spec=sel(0))
    dyb = mm("mix_dyb", "nt", dpab, wpb, (s, 2 * kd, d), tm=tm, tn=t, tk=d, out_dtype=BF16, a_spec=sel(1))
    selk = lambda j: pl.BlockSpec((None, tks, t), lambda i, jj, kk: (j, kk, jj))
    plan.grad("w_proj_a", mm("mix_dwpa", "tn", ya, dpab, (aw, d, s), tm=t, tn=t, tk=tks, b_spec=selk(0)))
    plan.grad("w_proj_b", mm("mix_dwpb", "tn", yb, dpab, (2 * kd, d, s), tm=t, tn=t, tk=tks, b_spec=selk(1)))

    dq, dkp, dvp, dbias = plan.host("attn_bwd", lambda st: _attn_bwd(p1, kvp, bias, dya, st))
    dp1 = jnp.concatenate([dq, dkp[A_PAD:].astype(BF16), dvp[A_PAD:].astype(BF16)], axis=1)
    dp2, dz, gsm = plan.host("gla_bwd", lambda st: _gla_bwd(p2, lrp, wa2p, balpha, gnorm, states, dyb, st))
    dlrp = mm("gla_dlr", "nt", dz, wa2p, (s, LANES, kd), tm=tm, tn=LANES, tk=kd, out_dtype=BF16)
    dwa2p = mm("gla_dwa2", "tn", lrp, dz, (LANES, kd, s), tm=LANES, tn=kd, tk=tks)

    tka = 3 * aw
    assert 6 * kd == tka
    du = mm("mix_du_a", "nn", dp1, wt, (s, d, tka), tm=tm, tn=t, tk=tka)
    du = mm("mix_du_b", "nn", dp2, wt, (s, d, tka), tm=tm, tn=t, tk=tka, add=du,
            b_spec=pl.BlockSpec((tka, t), lambda i, j, kk: (1 + kk, j)))
    du = mm("mix_du_g", "nn", dg, wt_g, (s, d, 2 * d), tm=tm, tn=t, tk=d, add=du, thin=(dlrp, wt_lr),
            a_spec=pl.BlockSpec((None, tm, d), lambda i, j, kk: (kk, i, 0)))
    nkg = d // t
    dw1 = mm("mix_dw_a", "tn", dp1, u2, (3 * aw, d, s), tm=t, tn=t, tk=tks)
    dw2 = mm("mix_dw_b", "tn", dp2, u2, (6 * kd, d, s), tm=t, tn=t, tk=tks)
    dwlr = mm("mix_dw_lr", "tn", dlrp, u2, (LANES, d, s), tm=LANES, tn=t, tk=tks)
    dwg = mm("mix_dw_g", "tn", dg, u2, (2 * d, d, s), tm=t, tn=t, tk=tks,
             a_spec=pl.BlockSpec((None, tks, t), lambda i, j, kk: (i // nkg, kk, i % nkg)))
    pieces = [dw1, dw2, dwlr[:GATE_RANK], dwg]
    shard_rows = sum(p.shape[0] for p in pieces) // N_CHIPS
    plan.grad("w_mix_in", jnp.stack([_virtual_rows(pieces, j * shard_rows, (j + 1) * shard_rows)
                                     for j in range(N_CHIPS)]))
    return du, (dbias, dwa2p[:GATE_RANK], gsm)


def _device_step(x, target, mod, small, plan):
    s, d = x.shape
    row = lambda i: mod[i:i + 1]
    sh1, sc1, g1, sh2, sc2, g2, sh3, sc3, g3 = (row(i) for i in range(N_MOD))

    onehot = _rel_onehot()
    bias = _mm("rel_bias_expand", "nn", small["rel_bias"], onehot, (A_HEADS, CHUNK * A_BAND, REL_SIZE),
               tm=A_HEADS, tn=REL_TILE, tk=REL_SIZE, precision=HIGHEST).reshape(A_HEADS, CHUNK, A_BAND)
    bias = _band_bias(bias)
    wa2p = jnp.pad(small["w_alpha2"], ((0, LANES - GATE_RANK), (0, 0))).astype(BF16)
    mix_small = (bias, wa2p, small["b_alpha"], small["gla_norm_g"])

    u1, u1t = _modulate("mod1", x, sh1, sc1)
    f1, sv1 = _ffn_forward("ffn1", u1, plan)
    h1, u2 = _resid_ln_fwd("ln1_fwd", x, f1, g1, small["ln1_g"], small["ln1_b"], sh2, sc2, 0.5)
    m, svm = _mix_forward(u2, plan, mix_small)
    h2, u3, u3t = _resid_ln_fwd("ln2_fwd", h1, m, g2, small["ln2_g"], small["ln2_b"], sh3, sc3, 1.0, transposed=True)
    f2, sv2 = _ffn_forward("ffn2", u3, plan)

    dr3, df2, acc3 = _final_ln_loss_bwd("ln3_loss_bwd", h2, f2, target, g3, small["ln3_g"], small["ln3_b"], 0.5)
    du3 = _ffn_backward("ffn2", df2, u3t, sv2, plan, in_first=False)
    dr2, dmx, acc2 = _resid_ln_bwd("ln2_bwd", du3, dr3, h1, m, sc3, g2, small["ln2_g"], small["ln2_b"], 1.0)
    du2, (dbias, dwa2, gsm) = _mix_backward(dmx, u2, svm, plan, mix_small)
    dr1, df1, acc1 = _resid_ln_bwd("ln1_bwd", du2, dr2, x, f1, sc2, g1, small["ln1_g"], small["ln1_b"], 0.5)
    du1 = _ffn_backward("ffn1", df1, u1t, sv1, plan, in_first=True)
    grad_x, acc0 = _input_grad("input_grad", du1, dr1, x, sc1)

    drel = _hosted_mm(plan)("rel_bias_grad", "nt", _band_bias_grad(dbias).reshape(A_HEADS, CHUNK * A_BAND), onehot,
                            (A_HEADS, REL_SIZE, CHUNK * A_BAND), tm=A_HEADS, tn=REL_SIZE, tk=REL_TILE, precision=HIGHEST)
    loss = jnp.sum(acc3[ROW_LOSS])
    dmod = jnp.stack([acc0[ROW_DSH], acc0[ROW_DSC], acc1[ROW_DGATE], acc1[ROW_DSH], acc1[ROW_DSC], acc2[ROW_DGATE],
                      acc2[ROW_DSH], acc2[ROW_DSC], acc3[ROW_DGATE]])
    kd = d // 4
    small_grads = dict(ln1_g=acc1[ROW_DLN_G], ln1_b=acc1[ROW_DLN_B], ln2_g=acc2[ROW_DLN_G], ln2_b=acc2[ROW_DLN_B],
                       ln3_g=acc3[ROW_DLN_G], ln3_b=acc3[ROW_DLN_B], b_alpha=gsm[GLA_ROW_DBALPHA],
                       gla_norm_g=gsm[GLA_ROW_DGNORM, :kd // B_HEADS * 2], rel_bias=drel, w_alpha2=dwa2)
    return loss, grad_x, small_grads, dmod


HBM_SPEC = pl.BlockSpec(memory_space=pl.ANY)


def _mesh_pos():
    return lax.axis_index("x"), lax.axis_index("y"), lax.axis_index("c")


def _other_chips(x, y):
    return [(1 - x, y), (x, 1 - y), (1 - x, 1 - y)]


def _remote(src, dst, send_sem, recv_sem, to):
    return pltpu.make_async_remote_copy(src_ref=src, dst_ref=dst, send_sem=send_sem, recv_sem=recv_sem,
                                        device_id=to, device_id_type=MESH)


def _allgather_rows(name, v):
    m_per, n = v.shape

    def body(x_ref, out_ref, send_sems, recv_sems, local_sem):
        x, y, c = _mesh_pos()
        me, sibling = (x, y, c), (x, y, 1 - c)
        chips = _other_chips(x, y)

        def rows(px, py, pc):
            return out_ref.at[pl.ds((4 * px + 2 * py + pc) * m_per, m_per), :]

        def copy(k, block, to, src=None):
            return _remote(rows(*block) if src is None else src, rows(*block), send_sems.at[k], recv_sems.at[k], to)

        mine = pltpu.make_async_copy(x_ref, rows(*me), local_sem)
        mine.start()
        first = [copy(0, me, sibling, src=x_ref)]
        first += [copy(1 + j, me, (*chip, c), src=x_ref) for j, chip in enumerate(chips)]
        for cp in first:
            cp.start()
        passed = [copy(4 + j, (*chip, c), sibling) for j, chip in enumerate(chips)]
        for j, chip in enumerate(chips):
            copy(1 + j, (*chip, c), me).wait_recv()
            passed[j].start()
        copy(0, sibling, me).wait_recv()
        for j, chip in enumerate(chips):
            copy(4 + j, (*chip, 1 - c), me).wait_recv()
        for cp in first + passed:
            cp.wait_send()
        mine.wait()

    return pl.pallas_call(
        body, name=name, out_shape=jax.ShapeDtypeStruct((N_DEV * m_per, n), v.dtype),
        in_specs=[pl.BlockSpec(memory_space=pltpu.VMEM)], out_specs=pl.BlockSpec(memory_space=pltpu.VMEM),
        scratch_shapes=[pltpu.SemaphoreType.DMA((7,)), pltpu.SemaphoreType.DMA((7,)), pltpu.SemaphoreType.DMA],
    )(v)


def _allgather_weights(bufs):
    n = len(bufs)
    TO_X, TO_Y, PASS_TO_X, PASS_TO_Y, SIB_X, SIB_Y, SIB_D0, SIB_D1 = range(8)

    def body(*refs):
        ins, outs = refs[:n], refs[n:2 * n]
        send_sems, recv_sems = refs[2 * n:]
        x, y, c = _mesh_pos()
        sibling = (x, y, 1 - c)
        xn, yn, dg = _other_chips(x, y)
        j0, jx, jy, jd = (2 * p[0] + p[1] for p in ((x, y), xn, yn, dg))
        sends = []

        def rows(w, hc, quarter=None):
            hr = bufs[w].shape[1] // 2
            if quarter is None:
                return pl.ds(hc * hr, hr)
            return pl.ds(hc * hr + quarter * (hr // 2), hr // 2)

        def push(src, dst, w, k, to):
            cp = _remote(src, dst, send_sems.at[w, k], recv_sems.at[w, k], to)
            cp.start()
            sends.append(cp)

        def landed(piece, w, k):
            _remote(piece, piece, send_sems.at[w, k], recv_sems.at[w, k], sibling).wait_recv()

        for w in range(n):
            mine = rows(w, c)
            push(ins[w].at[j0, mine, :], outs[w].at[j0, mine, :], w, TO_X, (*xn, c))
            push(ins[w].at[j0, mine, :], outs[w].at[j0, mine, :], w, TO_Y, (*yn, c))
        for w in range(n):
            half_x = outs[w].at[jx, rows(w, c), :]
            landed(half_x, w, TO_X)
            quarter = outs[w].at[jx, rows(w, c, 1), :]
            push(quarter, quarter, w, PASS_TO_Y, (*yn, c))
            push(half_x, half_x, w, SIB_X, sibling)
            half_y = outs[w].at[jy, rows(w, c), :]
            landed(half_y, w, TO_Y)
            quarter = outs[w].at[jy, rows(w, c, 0), :]
            push(quarter, quarter, w, PASS_TO_X, (*xn, c))
            push(half_y, half_y, w, SIB_Y, sibling)
        for w in range(n):
            for q, arrives_on, on in ((0, PASS_TO_X, SIB_D0), (1, PASS_TO_Y, SIB_D1)):
                piece = outs[w].at[jd, rows(w, c, q), :]
                landed(piece, w, arrives_on)
                push(piece, piece, w, on, sibling)
        for w in range(n):
            landed(outs[w].at[jx, rows(w, 1 - c), :], w, SIB_X)
            landed(outs[w].at[jy, rows(w, 1 - c), :], w, SIB_Y)
            landed(outs[w].at[jd, rows(w, 1 - c, 0), :], w, SIB_D0)
            landed(outs[w].at[jd, rows(w, 1 - c, 1), :], w, SIB_D1)
        for cp in sends:
            cp.wait_send()

    return pl.pallas_call(
        body, name="allgather_weights", out_shape=[jax.ShapeDtypeStruct(b.shape, b.dtype) for b in bufs],
        in_specs=[HBM_SPEC] * n, out_specs=[HBM_SPEC] * n, input_output_aliases={w: w for w in range(n)},
        scratch_shapes=[pltpu.SemaphoreType.DMA((n, 8)), pltpu.SemaphoreType.DMA((n, 8))],
    )(*bufs)


def _half(ref, hc, col, *lead):
    rows, cols = ref.shape[-2:]
    if col:
        return ref.at[(*lead, slice(None), pl.ds(hc * (cols // 2), cols // 2))]
    return ref.at[(*lead, pl.ds(hc * (rows // 2), rows // 2), slice(None))]


def _half_shape(shape, col):
    return shape[:-2] + ((shape[-2], shape[-1] // 2) if col else (shape[-2] // 2, shape[-1]))


def _quarter(ref, hc, q, col, *lead):
    rows, cols = ref.shape[-2:]
    if col:
        return ref.at[(*lead, slice(None), pl.ds(hc * (cols // 2) + q * (cols // 4), cols // 4))]
    return ref.at[(*lead, pl.ds(hc * (rows // 2) + q * (rows // 4), rows // 4), slice(None))]


def _stage_gather_ici(bufs, cols):
    n = len(bufs)
    TO_X, TO_Y, PASS_TO_X, PASS_TO_Y = range(4)

    def places():
        x, y, c = _mesh_pos()
        xn, yn, dg = _other_chips(x, y)
        return c, (*xn, c), (*yn, c), [2 * p[0] + p[1] for p in ((x, y), xn, yn, dg)]

    def remote(src, dst, send, recv, w, k, to):
        return _remote(src, dst, send.at[4 * w + k], recv.at[4 * w + k], to)

    def own(ins, outs, send, recv):
        c, to_x, to_y, (j0, _, _, _) = places()
        for w in range(n):
            for k, to in ((TO_X, to_x), (TO_Y, to_y)):
                yield remote(_half(ins[w], c, cols[w], j0), _half(outs[w], c, cols[w], j0), send, recv, w, k, to)

    def relays(ins, outs, send, recv):
        c, to_x, to_y, (_, jx, jy, _) = places()
        for w in range(n):
            for j, k, q, pass_k, to in ((jx, TO_X, 1, PASS_TO_Y, to_y), (jy, TO_Y, 0, PASS_TO_X, to_x)):
                half = _half(outs[w], c, cols[w], j)
                piece = _quarter(outs[w], c, q, cols[w], j)
                yield remote(half, half, send, recv, w, k, to), remote(piece, piece, send, recv, w, pass_k, to)

    def passed(ins, outs, send, recv):
        c, to_x, _, (_, _, _, jd) = places()
        for w in range(n):
            for q, k in ((0, PASS_TO_X), (1, PASS_TO_Y)):
                piece = _quarter(outs[w], c, q, cols[w], jd)
                yield remote(piece, piece, send, recv, w, k, to_x)

    def start(*refs):
        for cp in own(*refs):
            cp.start()

    def relay(*refs):
        for arrived, onward in relays(*refs):
            arrived.wait_recv()
            onward.start()

    def finish(*refs):
        for cp in passed(*refs):
            cp.wait_recv()
        for cp in own(*refs):
            cp.wait_send()
        for _, onward in relays(*refs):
            onward.wait_send()

    outs = [jax.ShapeDtypeStruct(b.shape, b.dtype) for b in bufs]
    return _Stage(bufs, outs, 4 * n, start, finish, aliases={w: w for w in range(n)}, relay=relay)


def _stage_gather_d2d(partial, cols):
    n = len(partial)

    def copies(ins, outs, send, recv):
        x, y, c = _mesh_pos()
        for w in range(n):
            for r, chip in enumerate(_other_chips(x, y)):
                jr = 2 * chip[0] + chip[1]
                mine = _remote(_half(ins[w], c, cols[w], jr), _half(outs[w], c, cols[w], jr), send.at[3 * w + r],
                               recv.at[3 * w + r], (x, y, 1 - c))
                got = _half(outs[w], 1 - c, cols[w], jr)
                yield mine, _remote(got, got, send.at[3 * w + r], recv.at[3 * w + r], (x, y, 1 - c))

    def start(*refs):
        for mine, _ in copies(*refs):
            mine.start()

    def finish(*refs):
        pairs = list(copies(*refs))
        for _, theirs in pairs:
            theirs.wait_recv()
        for mine, _ in pairs:
            mine.wait_send()

    outs = [jax.ShapeDtypeStruct(p.shape, p.dtype) for p in partial]
    return _Stage(partial, outs, 3 * n, start, finish, aliases={w: w for w in range(n)})


def _stage_exchange_halves(grads, cols):
    n = len(grads)

    def copies(ins, outs, send, recv):
        x, y, c = _mesh_pos()
        for w in range(n):
            yield _remote(_half(ins[w], 1 - c, cols[w], slice(None)), outs[w], send.at[w], recv.at[w], (x, y, 1 - c))

    def start(*refs):
        for cp in copies(*refs):
            cp.start()

    def finish(*refs):
        cps = list(copies(*refs))
        for cp in cps:
            cp.wait_recv()
        for cp in cps:
            cp.wait_send()

    outs = [jax.ShapeDtypeStruct(_half_shape(g.shape, col), g.dtype) for g, col in zip(grads, cols)]
    return _Stage(grads, outs, n, start, finish)


def _stage_scatter(parts):
    n = len(parts)

    def copies(ins, outs, send, recv):
        x, y, c = _mesh_pos()
        for w in range(n):
            for r, chip in enumerate(_other_chips(x, y)):
                jr = 2 * chip[0] + chip[1]
                yield _remote(ins[w].at[jr], outs[w].at[r], send.at[3 * w + r], recv.at[3 * w + r], (*chip, c))

    def start(*refs):
        for cp in copies(*refs):
            cp.start()

    def finish(*refs):
        cps = list(copies(*refs))
        for cp in cps:
            cp.wait_recv()
        for cp in cps:
            cp.wait_send()

    outs = [jax.ShapeDtypeStruct((3,) + p.shape[1:], p.dtype) for p in parts]
    return _Stage(parts, outs, 3 * n, start, finish)


def _stage_share(fulls, cols):
    n = len(fulls)

    def copies(ins, outs, send, recv):
        x, y, c = _mesh_pos()
        for w in range(n):
            theirs = _half(outs[w], 1 - c, cols[w])
            yield (_remote(_half(ins[w], c, cols[w]), _half(outs[w], c, cols[w]), send.at[w], recv.at[w], (x, y, 1 - c)),
                   _remote(theirs, theirs, send.at[w], recv.at[w], (x, y, 1 - c)))

    def start(*refs):
        for mine, _ in copies(*refs):
            mine.start()

    def finish(*refs):
        pairs = list(copies(*refs))
        for _, theirs in pairs:
            theirs.wait_recv()
        for mine, _ in pairs:
            mine.wait_send()

    outs = [jax.ShapeDtypeStruct(h.shape, h.dtype) for h in fulls]
    return _Stage(fulls, outs, n, start, finish, aliases={w: w for w in range(n)})


def _run_stages(name, stages):
    return _pcall(None, stages, name=name, out_shape=[], in_specs=[], out_specs=[])()[1]


TILE_BYTES = 2 * 1024 * 1024
SUM_TILE_BYTES = 4 * 1024 * 1024


def _row_tile(rows, cols, itemsize=4, tile_bytes=TILE_BYTES):
    for t in range(min(rows, tile_bytes // (cols * itemsize)) // SUBLANES * SUBLANES, 0, -SUBLANES):
        if rows % t == 0:
            return t
    return rows


def _col_tile(rows, cols, itemsize=4, tile_bytes=TILE_BYTES):
    for t in (2048, 1024, 512, 256, 128):
        if cols % t == 0 and t * rows * itemsize <= tile_bytes:
            return t
    return cols


def _tiling(rows, cols, col, tile_bytes=TILE_BYTES):
    if col:
        tc = _col_tile(rows, cols, tile_bytes=tile_bytes)
        return (rows, tc), cols // tc
    tr = _row_tile(rows, cols, tile_bytes=tile_bytes)
    return (tr, cols), rows // tr


def _strip(col, i):
    return (0, i) if col else (i, 0)


def _pair_sum(name, g, recv, core, col):
    blk, nb = _tiling(*recv.shape[1:], col, tile_bytes=SUM_TILE_BYTES)

    def body(c_ref, g_ref, r_ref, o_ref):
        o_ref[...] = (g_ref[...] + r_ref[...]).astype(BF16)

    grid_spec = pltpu.PrefetchScalarGridSpec(
        num_scalar_prefetch=1, grid=(N_CHIPS, nb),
        in_specs=[pl.BlockSpec((None,) + blk, lambda j, i, cr: (j,) + _strip(col, cr[0] * nb + i)),
                  pl.BlockSpec((None,) + blk, lambda j, i, cr: (j,) + _strip(col, i))],
        out_specs=pl.BlockSpec((None,) + blk, lambda j, i, cr: (j,) + _strip(col, i)))
    return pl.pallas_call(body, name=name, out_shape=jax.ShapeDtypeStruct(recv.shape, BF16), grid_spec=grid_spec,
                          compiler_params=_cp("parallel", "parallel"))(core, g, recv)


def _quad_sum(name, own, landed, chip_core, col):
    rows, cols = landed.shape[1:]
    blk, nb = _tiling(rows, cols, col, tile_bytes=SUM_TILE_BYTES)
    full = (rows, 2 * cols) if col else (2 * rows, cols)

    def body(cc_ref, own_ref, l_ref, o_ref):
        o_ref[...] = ((own_ref[...].astype(F32) + l_ref[0].astype(F32)) + l_ref[1].astype(F32)) + l_ref[2].astype(F32)

    grid_spec = pltpu.PrefetchScalarGridSpec(
        num_scalar_prefetch=1, grid=(nb,),
        in_specs=[pl.BlockSpec((None,) + blk, lambda i, cc: (cc[0],) + _strip(col, i)),
                  pl.BlockSpec((3,) + blk, lambda i, cc: (0,) + _strip(col, i))],
        out_specs=pl.BlockSpec(blk, lambda i, cc: _strip(col, cc[1] * nb + i)))
    return pl.pallas_call(body, name=name, out_shape=jax.ShapeDtypeStruct(full, F32), grid_spec=grid_spec,
                          compiler_params=_cp("arbitrary"))(chip_core, own, landed)


def _device_sum(name, gathered):
    def body(g_ref, o_ref):
        total = g_ref[0]
        for k in range(1, N_DEV):
            total = total + g_ref[k]
        o_ref[...] = total

    return pl.pallas_call(body, name=name, out_shape=jax.ShapeDtypeStruct(gathered.shape[1:], F32))(gathered)


def _adamw(name, w, g, m, v):
    rows, cols = w.shape
    col = rows % SUBLANES != 0
    blk, nb = _tiling(rows, cols, col)
    bc1 = 1.0 - ADAM_B1 ** ADAM_STEP
    bc2 = 1.0 - ADAM_B2 ** ADAM_STEP

    def body(w_ref, g_ref, m_ref, v_ref, d_ref, mo_ref, vo_ref):
        gv = g_ref[...]
        mn = ADAM_B1 * m_ref[...] + (1.0 - ADAM_B1) * gv
        vn = ADAM_B2 * v_ref[...] + (1.0 - ADAM_B2) * (gv * gv)
        mo_ref[...] = mn
        vo_ref[...] = vn
        d_ref[...] = -ADAM_LR * ((mn / bc1) / (jnp.sqrt(vn / bc2) + ADAM_EPS) + ADAM_WD * w_ref[...])

    spec = pl.BlockSpec(blk, lambda i: _strip(col, i))
    return pl.pallas_call(
        body, name=name, out_shape=[jax.ShapeDtypeStruct((rows, cols), F32)] * 3, grid=(nb,),
        in_specs=[spec] * 4, out_specs=[spec] * 3, compiler_params=_cp("parallel"),
    )(w, g, m, v)


WEIGHTS = ["w_ada", "b_ada", "ffn1_w_in", "ffn1_w_out", "ln1_g", "ln1_b", "w_mix_in", "rel_bias", "w_alpha2",
           "b_alpha", "gla_norm_g", "w_proj_a", "w_proj_b", "w_mix_out", "ln2_g", "ln2_b", "ffn2_w_in", "ffn2_w_out",
           "ln3_g", "ln3_b"]
BIG = {"ffn1_w_in": True, "ffn1_w_out": False, "w_mix_in": False, "w_proj_a": True, "w_proj_b": True,
       "w_mix_out": False, "ffn2_w_in": True, "ffn2_w_out": False}
TRANSPOSED = ("w_mix_in",)
STACKED = ("ffn1_w_in", "ffn2_w_in", "w_mix_in")
GROUP_FFN1 = ("ffn1_w_in", "ffn1_w_out")
GROUP_PROJ = ("w_proj_a", "w_proj_b", "w_mix_out")
SMALL = ["ln1_g", "ln1_b", "ln2_g", "ln2_b", "ln3_g", "ln3_b", "b_alpha", "gla_norm_g", "rel_bias", "w_alpha2"]


def _pad_rows(vec, rows=SUBLANES):
    per = -(-vec.shape[0] // (rows * LANES)) * LANES
    return jnp.pad(vec, (0, rows * per - vec.shape[0])).reshape(rows, per)


def _silu(v):
    return v * _sigmoid(v)


class _MeshPlan:
    def __init__(self, shards, chip, core):
        self.shapes = {k: v.shape for k, v in shards.items()}
        self.slots = {k: lax.dynamic_update_slice(lax.empty((N_CHIPS,) + v.shape, v.dtype), v[None], (chip, 0, 0))
                      for k, v in shards.items()}
        self.core1 = core.astype(jnp.int32).reshape(1)
        self.chip_core = jnp.stack([chip, core]).astype(jnp.int32)
        self.partial, self.full, self.local, self.pair, self.half, self.final, self.memos = {}, {}, {}, {}, {}, {}, {}
        ici, d2d, x1, x2, x3 = self.gather_ici, self.gather_d2d, self.exchange, self.scatter, self.share
        mix_in, in1, out1, in2, out2 = ("w_mix_in",), ("ffn1_w_in",), ("ffn1_w_out",), ("ffn2_w_in",), ("ffn2_w_out",)
        self.schedule = {
            "ffn1_in_fwd": [ici(mix_in)], "ffn1_out_fwd": [d2d(mix_in), ici(out2)],
            "mix_in_g": [ici(GROUP_PROJ), d2d(out2)],
            "attn_fwd": [ici(in2), d2d(GROUP_PROJ)], "gla_fwd": [d2d(in2)],
            "ffn2_dw_in": [x1(out2)], "ffn2_du": [x2(out2), x1(in2)],
            "attn_bwd": [x2(in2), x3(out2)], "gla_bwd": [x3(in2), x1(GROUP_PROJ)],
            "mix_du_g": [x2(GROUP_PROJ)],
            "ffn1_out_bwd": [x1(mix_in), x3(GROUP_PROJ)], "ffn1_dw_in": [x2(mix_in)], "ffn1_dw_out": [x3(mix_in), x1(in1)],
            "ffn1_du": [x2(in1), x1(out1)], "rel_bias_grad": [x2(out1), x3(in1)],
        }

    def weight(self, k):
        return self.full[k]

    def grad(self, k, g):
        r, cc = self.shapes[k]
        if k not in STACKED:
            g = g.reshape(r, N_CHIPS, cc).transpose(1, 0, 2) if BIG[k] else g.reshape(N_CHIPS, r, cc)
        self.local[k] = g

    def memo(self, key, make):
        if key not in self.memos:
            self.memos[key] = make()
        return self.memos[key]

    def host(self, name, call):
        builders = self.schedule.get(name)
        if not builders:
            return call(None)
        built = [b() for b in builders]
        main, comm = call([st for st, _ in built])
        for (_, post), res in zip(built, comm):
            post(res)
        return main

    def run(self, name, builders):
        built = [b() for b in builders]
        for (_, post), res in zip(built, _run_stages(name, [st for st, _ in built])):
            post(res)

    def set_gathered(self, names, gathered):
        for k, g in zip(names, gathered):
            _, r, cc = g.shape
            if k not in STACKED:
                g = g.transpose(1, 0, 2).reshape(r, N_CHIPS * cc) if BIG[k] else g.reshape(N_CHIPS * r, cc)
            self.full[k] = g

    @staticmethod
    def cols(names):
        return [k in TRANSPOSED for k in names]

    def gather_ici(self, names):
        def post(res):
            self.partial.update(zip(names, res))
        return lambda: (_stage_gather_ici([self.slots[k] for k in names], self.cols(names)), post)

    def gather_d2d(self, names):
        return lambda: (_stage_gather_d2d([self.partial[k] for k in names], self.cols(names)),
                        lambda res: self.set_gathered(names, res))

    def exchange(self, names):
        def post(res):
            for k, r in zip(names, res):
                self.pair[k] = _pair_sum(f"pair_sum_{k}", self.local[k], r, self.core1, k in TRANSPOSED)
        return lambda: (_stage_exchange_halves([self.local[k] for k in names], self.cols(names)), post)

    def scatter(self, names):
        def post(res):
            for k, landed in zip(names, res):
                self.half[k] = _quad_sum(f"quad_sum_{k}", self.pair[k], landed, self.chip_core, k in TRANSPOSED)
        return lambda: (_stage_scatter([self.pair[k] for k in names]), post)

    def share(self, names):
        def post(res):
            self.final.update(zip(names, res))
        return lambda: (_stage_share([self.half[k] for k in names], self.cols(names)), post)


def _step(args):
    x_pos, y_pos, c_pos = _mesh_pos()
    chip = 2 * x_pos + y_pos
    dev = 4 * x_pos + 2 * y_pos + c_pos
    take = lambda name, k: args[name][0].T if k in TRANSPOSED else args[name][0]
    w = {k: take(k, k) for k in WEIGHTS}
    mom = {k: take("m_" + k, k) for k in WEIGHTS}
    vel = {k: take("v_" + k, k) for k in WEIGHTS}
    x = args["x"][0]
    target = args["loss_target"][0]
    s, d = x.shape
    kd = d // 4
    rel_sh = w["rel_bias"].shape[1]
    wa2_sh = w["w_alpha2"].shape[1]
    ada_sh = w["w_ada"].shape[1]

    n_rel, n_wa2 = A_HEADS * rel_sh, GATE_RANK * wa2_sh
    packed = _pad_rows(jnp.concatenate([args["c"].reshape(-1), w["rel_bias"].reshape(-1), w["w_alpha2"].reshape(-1)]))
    got = _allgather_rows("gather_small_inputs", packed).reshape(N_DEV, -1)
    c_all = got[:, :d]
    per_chip = got[0::2]
    rel_bias = per_chip[:, d:d + n_rel].reshape(N_CHIPS, A_HEADS, rel_sh).transpose(1, 0, 2).reshape(A_HEADS, -1)
    w_alpha2 = per_chip[:, d + n_rel:d + n_rel + n_wa2].reshape(N_CHIPS, GATE_RANK, wa2_sh).transpose(1, 0, 2)
    w_alpha2 = w_alpha2.reshape(GATE_RANK, -1)

    b_shard = lax.dynamic_slice(w["b_ada"], (chip * ada_sh,), (ada_sh,))
    mod_shard = _mm("ada_fwd", "nn", c_all, w["w_ada"], (N_DEV, ada_sh, d), tm=N_DEV, tn=_tile(ada_sh, (512, 128)),
                    tk=d, precision=HIGHEST, a_fn=_silu, add=jnp.broadcast_to(b_shard[None], (N_DEV, ada_sh)))
    mod_all = _allgather_rows("gather_mod", mod_shard).reshape(N_DEV, N_DEV, ada_sh)[0::2]
    mod_all = mod_all.transpose(1, 0, 2).reshape(N_DEV, N_MOD * d)
    mod = lax.dynamic_index_in_dim(mod_all, dev, 0, keepdims=False).reshape(N_MOD, d)

    names = list(BIG)
    plan = _MeshPlan({k: w[k].astype(BF16) for k in names}, chip, c_pos)
    plan.set_gathered(GROUP_FFN1, _allgather_weights([plan.slots[k] for k in GROUP_FFN1]))

    small = dict(rel_bias=rel_bias, w_alpha2=w_alpha2, b_alpha=w["b_alpha"][None], gla_norm_g=w["gla_norm_g"][None])
    for k in ("ln1_g", "ln1_b", "ln2_g", "ln2_b", "ln3_g", "ln3_b"):
        small[k] = w[k][None]
    loss_local, grad_x, small_grads, dmod = _device_step(x, target, mod, small, plan)
    loss = lax.psum(loss_local, ("x", "y", "c"))
    plan.run("grad_tail_share", [plan.share(GROUP_FFN1[1:])])

    flat = jnp.concatenate([small_grads[k].reshape(-1) for k in SMALL] + [dmod.reshape(-1)])
    n_small = flat.shape[0] - N_MOD * d
    packed = _pad_rows(flat)
    all_small = _allgather_rows("gather_small_grads", packed).reshape(N_DEV, SUBLANES, -1)
    summed = _device_sum("small_grad_sum", all_small).reshape(-1)
    dmod_all = all_small.reshape(N_DEV, -1)[:, n_small:n_small + N_MOD * d]
    dmod_shard = lax.dynamic_slice(dmod_all, (0, chip * ada_sh), (N_DEV, ada_sh))
    grads = {"b_ada": summed[n_small:n_small + N_MOD * d]}
    off = 0
    for k in SMALL:
        size = small_grads[k].size
        grads[k] = summed[off:off + size].reshape(small_grads[k].shape)
        off += size
    grads["rel_bias"] = lax.dynamic_slice(grads["rel_bias"], (0, chip * rel_sh), (A_HEADS, rel_sh))
    grads["w_alpha2"] = lax.dynamic_slice(grads["w_alpha2"], (0, chip * wa2_sh), (GATE_RANK, wa2_sh))
    grads["w_ada"] = _mm("ada_bwd", "nn", jnp.pad(c_all.T, ((0, 0), (0, LANES - N_DEV))),
                         jnp.pad(dmod_shard, ((0, LANES - N_DEV), (0, 0))), (d, ada_sh, LANES), tm=_tile(d, (1024,)),
                         tn=_tile(ada_sh, (512, 128)), tk=LANES, precision=HIGHEST, a_fn=_silu)

    grads.update(plan.final)

    delta, new_m, new_v = {}, {}, {}
    for k in ["w_ada"] + names:
        delta[k], new_m[k], new_v[k] = _adamw(f"adamw_{k}", w[k], grads[k], mom[k], vel[k])
    tiny = ["b_ada"] + SMALL
    pack = lambda src: _pad_rows(jnp.concatenate([src[k].reshape(-1) for k in tiny]), rows=1).reshape(-1, LANES)
    outs = _adamw("adamw_small", pack(w), pack(grads), pack(mom), pack(vel))
    off = 0
    for k in tiny:
        size = w[k].size
        for dst, src in zip((delta, new_m, new_v), outs):
            dst[k] = src.reshape(-1)[off:off + size].reshape(w[k].shape)
        off += size

    give = lambda src: [src[k].T[None] if k in TRANSPOSED else src[k][None] for k in WEIGHTS]
    return (loss, grad_x[None], *give(grads), *give(delta), *give(new_m), *give(new_v))


def kernel(x, c, w_ada, b_ada, ffn1_w_in, ffn1_w_out, ln1_g, ln1_b, w_mix_in, rel_bias, w_alpha2, b_alpha, gla_norm_g, w_proj_a, w_proj_b, w_mix_out, ln2_g, ln2_b, ffn2_w_in, ffn2_w_out, ln3_g, ln3_b, loss_target, m_w_ada, m_b_ada, m_ffn1_w_in, m_ffn1_w_out, m_ln1_g, m_ln1_b, m_w_mix_in, m_rel_bias, m_w_alpha2, m_b_alpha, m_gla_norm_g, m_w_proj_a, m_w_proj_b, m_w_mix_out, m_ln2_g, m_ln2_b, m_ffn2_w_in, m_ffn2_w_out, m_ln3_g, m_ln3_b, v_w_ada, v_b_ada, v_ffn1_w_in, v_ffn1_w_out, v_ln1_g, v_ln1_b, v_w_mix_in, v_rel_bias, v_w_alpha2, v_b_alpha, v_gla_norm_g, v_w_proj_a, v_w_proj_b, v_w_mix_out, v_ln2_g, v_ln2_b, v_ffn2_w_in, v_ffn2_w_out, v_ln3_g, v_ln3_b):
    return _step(dict(locals()))
```

```python
import functools

import jax
import jax.numpy as jnp
from jax import lax
from jax.experimental import pallas as pl
from jax.experimental.pallas import tpu as pltpu

F32 = jnp.float32
BF16 = jnp.bfloat16
MESH = pl.DeviceIdType.MESH
HIGHEST = lax.Precision.HIGHEST

VMEM_LIMIT_BYTES = 56 * 1024 * 1024
LANES = 128
SUBLANES = 8

CHUNK = 64
A_HEADS = 16
A_HEAD_DIM = 64
A_PAST_CHUNKS = 8
A_BAND = (A_PAST_CHUNKS + 1) * CHUNK
A_PAD = A_PAST_CHUNKS * CHUNK
REL_CLIP = 256
REL_SIZE = REL_CLIP + CHUNK
B_HEADS = 4
GATE_RANK = 16
GATE_TAU = 16.0
N_MOD = 9
DEPTH = 1
ALPHA = (2.0 * DEPTH) ** 0.25
LN_EPS = 1e-5
RMS_EPS = 1e-6
ADAM_LR = 0.001
ADAM_B1 = 0.9
ADAM_B2 = 0.999
ADAM_EPS = 1e-08
ADAM_WD = 0.01
ADAM_STEP = 10
NEG_BIG = -1e30

N_CHIPS = 4
N_DEV = 8


def _cp(*sem):
    return pltpu.CompilerParams(dimension_semantics=sem, vmem_limit_bytes=VMEM_LIMIT_BYTES)


class _Stage:
    def __init__(self, arrays, out_shapes, n_sems, start, finish, aliases=None, relay=None):
        self.arrays, self.out_shapes, self.n_sems = list(arrays), list(out_shapes), n_sems
        self.start, self.finish, self.relay, self.aliases = start, finish, relay, dict(aliases or {})


def _pcall(body, stages, *, name, out_shape, in_specs, out_specs, grid=(), scratch_shapes=(), compiler_params=None):
    single = not isinstance(out_shape, (list, tuple))
    outs = [out_shape] if single else list(out_shape)
    ospecs = [out_specs] if single else list(out_specs)
    in_specs, scratch_shapes = list(in_specs), list(scratch_shapes)
    n_in, n_out, n_sc = len(in_specs), len(outs), len(scratch_shapes)
    stages = list(stages or [])
    c_in = [a for st in stages for a in st.arrays]
    c_out = [o for st in stages for o in st.out_shapes]
    aliases = {}
    io, oo = n_in, n_out
    for st in stages:
        for a, b in st.aliases.items():
            aliases[io + a] = oo + b
        io += len(st.arrays)
        oo += len(st.out_shapes)

    def wrapped(*refs):
        ins = refs[:n_in]
        cins = refs[n_in:n_in + len(c_in)]
        base = n_in + len(c_in)
        mouts = refs[base:base + n_out]
        couts = refs[base + n_out:base + n_out + len(c_out)]
        base += n_out + len(c_out)
        scr = refs[base:base + n_sc]
        sems = refs[base + n_sc:]

        def each(phase):
            i = o = 0
            for k, st in enumerate(stages):
                fn = (st.start, st.relay, st.finish)[phase]
                if fn is not None:
                    fn(cins[i:i + len(st.arrays)], couts[o:o + len(st.out_shapes)], sems[2 * k], sems[2 * k + 1])
                i += len(st.arrays)
                o += len(st.out_shapes)

        if stages and grid:
            step = functools.reduce(lambda acc, a: acc * grid[a] + pl.program_id(a), range(len(grid)), 0)
            steps = functools.reduce(lambda a, b: a * b, grid)
            pl.when(step == 0)(lambda: each(0))
            if any(st.relay for st in stages):
                pl.when(step == (2 * steps) // 3)(lambda: each(1))
            if body is not None:
                body(*ins, *mouts, *scr)
            pl.when(step == steps - 1)(lambda: each(2))
        else:
            each(0)
            each(1)
            if body is not None:
                body(*ins, *mouts, *scr)
            each(2)

    sem_shapes = []
    for st in stages:
        sem_shapes += [pltpu.SemaphoreType.DMA((st.n_sems,)), pltpu.SemaphoreType.DMA((st.n_sems,))]
    kwargs = dict(grid=grid) if grid else {}
    if compiler_params is not None:
        kwargs["compiler_params"] = compiler_params

    def run(*operands):
        res = pl.pallas_call(
            wrapped, name=name, out_shape=outs + c_out, in_specs=in_specs + [HBM_SPEC] * len(c_in),
            out_specs=ospecs + [HBM_SPEC] * len(c_out), scratch_shapes=scratch_shapes + sem_shapes,
            input_output_aliases=aliases, **kwargs)(*operands, *c_in)
        main = res[0] if single else tuple(res[:n_out])
        if not stages:
            return main
        comm, o = [], n_out
        for st in stages:
            comm.append(list(res[o:o + len(st.out_shapes)]))
            o += len(st.out_shapes)
        return main, comm

    return run


LONG_K = (2048, 1024)


def _tile(n, prefs):
    for t in prefs:
        if t <= n and n % t == 0:
            return t
    return n


_DIMS = {"nn": (((1,), (0,)), ((), ())), "nt": (((1,), (1,)), ((), ())), "tn": (((0,), (0,)), ((), ()))}


def _dot(a, b, mode="nn", precision=None):
    return lax.dot_general(a, b, _DIMS[mode], precision=precision, preferred_element_type=F32)


def _sigmoid(x):
    return 0.5 * jnp.tanh(0.5 * x) + 0.5


EPILOGUE_STRIP = 256


def _strips(n, width=EPILOGUE_STRIP):
    width = width if n % width == 0 else n
    return [slice(j, j + width) for j in range(0, n, width)]


def _mm(name, mode, a, b, mnk, *, tm, tn, tk, out_dtype=F32, precision=None, a_spec=None, b_spec=None,
        out_shape=None, o_spec=None, add=None, a_fn=None, thin=None, stages=None):
    m, n, k = mnk
    assert m % tm == 0 and n % tn == 0 and k % tk == 0, (name, mnk, tm, tn, tk)
    nk = k // tk
    if a_spec is None:
        a_spec = {"nn": pl.BlockSpec((tm, tk), lambda i, j, kk: (i, kk)),
                  "nt": pl.BlockSpec((tm, tk), lambda i, j, kk: (i, kk)),
                  "tn": pl.BlockSpec((tk, tm), lambda i, j, kk: (kk, i))}[mode]
    if b_spec is None:
        b_spec = {"nn": pl.BlockSpec((tk, tn), lambda i, j, kk: (kk, j)),
                  "nt": pl.BlockSpec((tn, tk), lambda i, j, kk: (j, kk)),
                  "tn": pl.BlockSpec((tk, tn), lambda i, j, kk: (kk, j))}[mode]
    if o_spec is None:
        o_spec = pl.BlockSpec((tm, tn), lambda i, j, kk: (i, j))
    if out_shape is None:
        out_shape = (m, n)
    has_add = add is not None
    n_in = 2 + has_add + (2 if thin else 0)

    def body(*refs):
        a_ref, b_ref = refs[0], refs[1]
        add_ref = refs[2] if has_add else None
        o_ref = refs[n_in]
        av = a_ref[...]
        if a_fn is not None:
            av = a_fn(av)
        part = _dot(av, b_ref[...], mode, precision)

        def finish(total):
            if has_add:
                total = total + add_ref[...]
            if thin:
                total = total + _dot(refs[n_in - 2][...], refs[n_in - 1][...])
            o_ref[...] = total.astype(out_dtype)

        if nk == 1:
            finish(part)
        else:
            acc_ref = refs[-1]
            kk = pl.program_id(2)

            @pl.when(kk == 0)
            def _():
                acc_ref[...] = part

            @pl.when(kk > 0)
            def _():
                acc_ref[...] += part

            @pl.when(kk == nk - 1)
            def _():
                finish(acc_ref[...])

    in_specs = [a_spec, b_spec]
    operands = [a, b]
    if has_add:
        in_specs.append(pl.BlockSpec((tm, tn), lambda i, j, kk: (i, j)))
        operands.append(add)
    if thin:
        k2 = thin[0].shape[1]
        in_specs += [pl.BlockSpec((tm, k2), lambda i, j, kk: (i, 0)), pl.BlockSpec((k2, tn), lambda i, j, kk: (0, j))]
        operands += list(thin)
    return _pcall(
        body, stages, name=name, out_shape=jax.ShapeDtypeStruct(out_shape, out_dtype), grid=(m // tm, n // tn, nk),
        in_specs=in_specs, out_specs=o_spec,
        scratch_shapes=[pltpu.VMEM((tm, tn), F32)] if nk > 1 else [],
        compiler_params=_cp("arbitrary", "arbitrary", "arbitrary") if stages else _cp("parallel", "parallel", "arbitrary"),
    )(*operands)


def _row_spec(tr, d):
    return pl.BlockSpec((tr, d), lambda i: (i, 0))


def _vec_spec(d, rows=1):
    return pl.BlockSpec((rows, d), lambda i: (0, 0))


def _col_spec(d, tr):
    return pl.BlockSpec((d, tr), lambda i: (0, i))


def _modulate(name, x, sh, sc):
    s, d = x.shape
    tr = _tile(s, (512, 256))

    def body(x_ref, sh_ref, sc_ref, o_ref, ot_ref):
        u = x_ref[...] * (1.0 + sc_ref[...]) + sh_ref[...]
        o_ref[...] = u.astype(BF16)
        ot_ref[...] = u.T.astype(BF16)

    return pl.pallas_call(
        body, name=name, out_shape=(jax.ShapeDtypeStruct((s, d), BF16), jax.ShapeDtypeStruct((d, s), BF16)),
        grid=(s // tr,), in_specs=[_row_spec(tr, d), _vec_spec(d), _vec_spec(d)],
        out_specs=(_row_spec(tr, d), _col_spec(d, tr)), compiler_params=_cp("parallel"),
    )(x, sh, sc)


def _ln_stats(r):
    mu = jnp.mean(r, axis=-1, keepdims=True)
    xc = r - mu
    var = jnp.mean(xc * xc, axis=-1, keepdims=True)
    rstd = lax.rsqrt(var + LN_EPS)
    return xc * rstd, rstd


def _resid_ln_fwd(name, x, f, gate, ln_g, ln_b, sh_n, sc_n, coef, transposed=False):
    s, d = x.shape
    tr = _tile(s, (256,))

    def body(x_ref, f_ref, gate_ref, g_ref, b_ref, sh_ref, sc_ref, h_ref, u_ref, *ut_ref):
        r = ALPHA * x_ref[...] + (coef * gate_ref[...]) * f_ref[...]
        xhat, _ = _ln_stats(r)
        h = xhat * g_ref[...] + b_ref[...]
        h_ref[...] = h
        u = h * (1.0 + sc_ref[...]) + sh_ref[...]
        u_ref[...] = u.astype(BF16)
        if transposed:
            ut_ref[0][...] = u.T.astype(BF16)

    extra_shape = (jax.ShapeDtypeStruct((d, s), BF16),) if transposed else ()
    extra_spec = (_col_spec(d, tr),) if transposed else ()
    return pl.pallas_call(
        body, name=name,
        out_shape=(jax.ShapeDtypeStruct((s, d), F32), jax.ShapeDtypeStruct((s, d), BF16)) + extra_shape,
        grid=(s // tr,), in_specs=[_row_spec(tr, d), _row_spec(tr, d)] + [_vec_spec(d)] * 5,
        out_specs=(_row_spec(tr, d), _row_spec(tr, d)) + extra_spec, compiler_params=_cp("parallel"),
    )(x, f, gate, ln_g, ln_b, sh_n, sc_n)


ROW_DSC, ROW_DSH, ROW_DLN_G, ROW_DLN_B, ROW_DGATE, ROW_LOSS = 0, 1, 2, 3, 4, 5


def _ln_bwd_core(dy, xhat, rstd, ln_g):
    dxhat = dy * ln_g
    m1 = jnp.mean(dxhat, axis=-1, keepdims=True)
    m2 = jnp.mean(dxhat * xhat, axis=-1, keepdims=True)
    return rstd * (dxhat - m1 - xhat * m2)


def _colsum(v):
    return jnp.sum(v, axis=0, keepdims=True)


def _final_ln_loss_bwd(name, x, f, target, gate, ln_g, ln_b, coef):
    s, d = x.shape
    tr = _tile(s, (256,))
    inv_d = 1.0 / d

    def body(x_ref, f_ref, t_ref, gate_ref, g_ref, b_ref, dr_ref, df_ref, acc_ref):
        @pl.when(pl.program_id(0) == 0)
        def _():
            acc_ref[...] = jnp.zeros_like(acc_ref)

        fv = f_ref[...]
        r = ALPHA * x_ref[...] + (coef * gate_ref[...]) * fv
        xhat, rstd = _ln_stats(r)
        h = xhat * g_ref[...] + b_ref[...]
        err = h - t_ref[...]
        dy = err * inv_d
        dr = _ln_bwd_core(dy, xhat, rstd, g_ref[...])
        dr_ref[...] = dr
        df_ref[...] = ((coef * gate_ref[...]) * dr).astype(BF16)
        acc_ref[ROW_DLN_G:ROW_DLN_G + 1, :] += _colsum(dy * xhat)
        acc_ref[ROW_DLN_B:ROW_DLN_B + 1, :] += _colsum(dy)
        acc_ref[ROW_DGATE:ROW_DGATE + 1, :] += _colsum((coef * dr) * fv)
        acc_ref[ROW_LOSS:ROW_LOSS + 1, :] += _colsum(err * err) * (0.5 * inv_d)

    return pl.pallas_call(
        body, name=name,
        out_shape=(jax.ShapeDtypeStruct((s, d), F32), jax.ShapeDtypeStruct((s, d), BF16),
                   jax.ShapeDtypeStruct((SUBLANES, d), F32)),
        grid=(s // tr,), in_specs=[_row_spec(tr, d)] * 3 + [_vec_spec(d)] * 3,
        out_specs=(_row_spec(tr, d), _row_spec(tr, d), _vec_spec(d, SUBLANES)),
        compiler_params=_cp("arbitrary"),
    )(x, f, target, gate, ln_g, ln_b)


def _resid_ln_bwd(name, du_n, dr_n, x, f, sc_n, gate, ln_g, ln_b, coef):
    s, d = x.shape
    tr = _tile(s, (256,))

    def body(du_ref, drn_ref, x_ref, f_ref, sc_ref, gate_ref, g_ref, b_ref, dr_ref, df_ref, acc_ref):
        @pl.when(pl.program_id(0) == 0)
        def _():
            acc_ref[...] = jnp.zeros_like(acc_ref)

        fv = f_ref[...]
        du = du_ref[...]
        r = ALPHA * x_ref[...] + (coef * gate_ref[...]) * fv
        xhat, rstd = _ln_stats(r)
        h = xhat * g_ref[...] + b_ref[...]
        dy = du * (1.0 + sc_ref[...]) + ALPHA * drn_ref[...]
        dr = _ln_bwd_core(dy, xhat, rstd, g_ref[...])
        dr_ref[...] = dr
        df_ref[...] = ((coef * gate_ref[...]) * dr).astype(BF16)
        acc_ref[ROW_DSC:ROW_DSC + 1, :] += _colsum(du * h)
        acc_ref[ROW_DSH:ROW_DSH + 1, :] += _colsum(du)
        acc_ref[ROW_DLN_G:ROW_DLN_G + 1, :] += _colsum(dy * xhat)
        acc_ref[ROW_DLN_B:ROW_DLN_B + 1, :] += _colsum(dy)
        acc_ref[ROW_DGATE:ROW_DGATE + 1, :] += _colsum((coef * dr) * fv)

    return pl.pallas_call(
        body, name=name,
        out_shape=(jax.ShapeDtypeStruct((s, d), F32), jax.ShapeDtypeStruct((s, d), BF16),
                   jax.ShapeDtypeStruct((SUBLANES, d), F32)),
        grid=(s // tr,), in_specs=[_row_spec(tr, d)] * 4 + [_vec_spec(d)] * 4,
        out_specs=(_row_spec(tr, d), _row_spec(tr, d), _vec_spec(d, SUBLANES)),
        compiler_params=_cp("arbitrary"),
    )(du_n, dr_n, x, f, sc_n, gate, ln_g, ln_b)


def _input_grad(name, du, dr, x, sc):
    s, d = x.shape
    tr = _tile(s, (256,))

    def body(du_ref, dr_ref, x_ref, sc_ref, gx_ref, acc_ref):
        @pl.when(pl.program_id(0) == 0)
        def _():
            acc_ref[...] = jnp.zeros_like(acc_ref)

        du = du_ref[...]
        gx_ref[...] = du * (1.0 + sc_ref[...]) + ALPHA * dr_ref[...]
        acc_ref[ROW_DSC:ROW_DSC + 1, :] += _colsum(du * x_ref[...])
        acc_ref[ROW_DSH:ROW_DSH + 1, :] += _colsum(du)

    return pl.pallas_call(
        body, name=name,
        out_shape=(jax.ShapeDtypeStruct((s, d), F32), jax.ShapeDtypeStruct((SUBLANES, d), F32)),
        grid=(s // tr,), in_specs=[_row_spec(tr, d)] * 3 + [_vec_spec(d)],
        out_specs=(_row_spec(tr, d), _vec_spec(d, SUBLANES)), compiler_params=_cp("arbitrary"),
    )(du, dr, x, sc)


def _ffn_in_fwd(name, u, w_in, stages=None):
    s, d = u.shape
    cs = w_in.shape[2]
    f = 2 * cs
    tm, tn = _tile(s, (2048, 1024, 512)), _tile(cs, (256, 128))
    nb = f // tn
    nbs = cs // tn

    def body(u_ref, wa_ref, wb_ref, ab_ref, act_ref):
        for rows in _strips(tm, 512):
            uv = u_ref[rows, :]
            a = _dot(uv, wa_ref[...])
            b = _dot(uv, wb_ref[...])
            sg = _sigmoid(a)
            silu = a * sg
            ab_ref[0, rows, :] = (b * (sg + silu * (1.0 - sg))).astype(BF16)
            ab_ref[1, rows, :] = silu.astype(BF16)
            act_ref[rows, :] = (silu * b).astype(BF16)

    return _pcall(
        body, stages, name=name,
        out_shape=(jax.ShapeDtypeStruct((2, s, f), BF16), jax.ShapeDtypeStruct((s, f), BF16)),
        grid=(s // tm, nb),
        in_specs=[pl.BlockSpec((tm, d), lambda i, j: (i, 0)),
                  pl.BlockSpec((None, d, tn), lambda i, j: (j // nbs, 0, j % nbs)),
                  pl.BlockSpec((None, d, tn), lambda i, j: (2 + j // nbs, 0, j % nbs))],
        out_specs=(pl.BlockSpec((2, tm, tn), lambda i, j: (0, i, j)), pl.BlockSpec((tm, tn), lambda i, j: (i, j))),
        compiler_params=_cp("arbitrary", "arbitrary"),
    )(u, w_in, w_in)


def _ffn_out_bwd(name, df, w_out, ab, stages=None):
    s, d = df.shape
    f = w_out.shape[0]
    tm, tn = _tile(s, (1024, 512)), _tile(f, (512, 256, 128))

    def body(df_ref, w_ref, ab_ref, dab_ref):
        dfv = df_ref[...]
        for cols in _strips(tn):
            dact = _dot(dfv, w_ref[cols, :], "nt")
            dab_ref[0, :, cols] = (dact * ab_ref[0, :, cols].astype(F32)).astype(BF16)
            dab_ref[1, :, cols] = (dact * ab_ref[1, :, cols].astype(F32)).astype(BF16)

    return _pcall(
        body, stages, name=name, out_shape=jax.ShapeDtypeStruct((2, s, f), BF16), grid=(s // tm, f // tn),
        in_specs=[pl.BlockSpec((tm, d), lambda i, j: (i, 0)), pl.BlockSpec((tn, d), lambda i, j: (j, 0)),
                  pl.BlockSpec((2, tm, tn), lambda i, j: (0, i, j))],
        out_specs=pl.BlockSpec((2, tm, tn), lambda i, j: (0, i, j)),
        compiler_params=_cp("arbitrary", "arbitrary"),
    )(df, w_out, ab)


def _ffn_forward(tag, u, plan):
    w_in, w_out = plan.weight(f"{tag}_w_in"), plan.weight(f"{tag}_w_out")
    s, d = u.shape
    f = w_out.shape[0]
    ab, act = plan.host(f"{tag}_in_fwd", lambda st: _ffn_in_fwd(f"{tag}_in_fwd", u, w_in, st))
    out = plan.host(f"{tag}_out_fwd", lambda st: _mm(
        f"{tag}_out_fwd", "nn", act, w_out, (s, d, f), tm=_tile(s, (1024,)), tn=_tile(d, (1024,)),
        tk=_tile(f, (2816, 1408, 512, 128)), stages=st))
    return out, (ab, act)


def _ffn_backward(tag, df, ut, saved, plan, in_first):
    w_in, w_out = plan.weight(f"{tag}_w_in"), plan.weight(f"{tag}_w_out")
    ab, act = saved
    d, s = ut.shape
    f = w_out.shape[0]
    dab = plan.host(f"{tag}_out_bwd", lambda st: _ffn_out_bwd(f"{tag}_out_bwd", df, w_out, ab, st))
    cs = w_in.shape[2]
    tk = _tile(cs, (2816, 1408, 256, 128))
    nkh, nks = f // tk, cs // tk
    tmd = _tile(d, (1024,))
    tks = _tile(s, LONG_K)

    def dw_in():
        tw = _tile(cs, (256, 128))
        nwh, nws = f // tw, cs // tw
        plan.grad(f"{tag}_w_in", plan.host(f"{tag}_dw_in", lambda st: _mm(
            f"{tag}_dw_in", "nn", ut, dab, (d, 2 * f, s), tm=tmd, tn=tw, tk=s,
            b_spec=pl.BlockSpec((None, s, tw), lambda i, j, kk: (j // nwh, 0, j % nwh)), out_shape=(N_CHIPS, d, cs),
            o_spec=pl.BlockSpec((None, tmd, tw), lambda i, j, kk: (j // nws, i, j % nws)), stages=st)))

    def dw_out():
        plan.grad(f"{tag}_w_out", plan.host(f"{tag}_dw_out", lambda st: _mm(
            f"{tag}_dw_out", "tn", act, df, (f, d, s), tm=_tile(f, (1408, 512, 128)), tn=tmd, tk=tks, stages=st)))

    for step in ((dw_in, dw_out) if in_first else (dw_out, dw_in)):
        step()
    return plan.host(f"{tag}_du", lambda st: _mm(
        f"{tag}_du", "nt", dab, w_in, (s, d, 2 * f), tm=_tile(s, (1024,)), tn=tmd, tk=tk,
        a_spec=pl.BlockSpec((None, _tile(s, (1024,)), tk), lambda i, j, kk: (kk // nkh, i, kk % nkh)),
        b_spec=pl.BlockSpec((None, tmd, tk), lambda i, j, kk: (kk // nks, j, kk % nks)), stages=st))


ATTN_Q = 4 * CHUNK
ATTN_W = ATTN_Q + A_PAD


def _band_bias(bias):
    n = ATTN_Q // CHUNK
    rows = [jnp.pad(bias, ((0, 0), (0, 0), (i * CHUNK, (n - 1 - i) * CHUNK)), constant_values=NEG_BIG)
            for i in range(n)]
    return jnp.concatenate(rows, axis=1)


def _band_bias_grad(dband):
    n = ATTN_Q // CHUNK
    parts = [dband[:, i * CHUNK:(i + 1) * CHUNK, i * CHUNK:i * CHUNK + A_BAND] for i in range(n)]
    return functools.reduce(jnp.add, parts)


def _attn_probs(q, kw, bias, key0):
    sc = _dot(q, kw, "nt") * (A_HEAD_DIM ** -0.5) + bias
    ks = lax.broadcasted_iota(jnp.int32, sc.shape, 1)
    sc = jnp.where(key0 + ks >= 0, sc, NEG_BIG)
    p = jnp.exp(sc - jnp.max(sc, axis=-1, keepdims=True))
    return p * (1.0 / jnp.sum(p, axis=-1, keepdims=True))


def _head_masks():
    lane = lax.broadcasted_iota(jnp.int32, (1, LANES), 1)
    return [lane // A_HEAD_DIM == h for h in range(LANES // A_HEAD_DIM)]


def _attn_fwd(p1, kvp, band, stages=None):
    s = p1.shape[0]
    aw = A_HEADS * A_HEAD_DIM
    nblk = aw // LANES
    hpb = LANES // A_HEAD_DIM
    assert s % ATTN_Q == 0

    def body(q_ref, k_ref, v_ref, b_ref, o_ref):
        base = pl.multiple_of(pl.program_id(1) * ATTN_Q, ATTN_Q)
        qv = q_ref[...]
        kw = k_ref[pl.ds(base, ATTN_W), :]
        vw = v_ref[pl.ds(base, ATTN_W), :]
        out = jnp.zeros((ATTN_Q, LANES), F32)
        for h, mask in enumerate(_head_masks()):
            p = _attn_probs(jnp.where(mask, qv, jnp.zeros_like(qv)), kw, b_ref[h], base - A_PAD)
            out = jnp.where(mask, _dot(p.astype(BF16), vw), out)
        o_ref[...] = out.astype(BF16)

    kv_rows = s + A_PAD
    return _pcall(
        body, stages, name="attn_fwd", out_shape=jax.ShapeDtypeStruct((s, aw), BF16), grid=(nblk, s // ATTN_Q),
        in_specs=[pl.BlockSpec((ATTN_Q, LANES), lambda b, i: (i, b)),
                  pl.BlockSpec((kv_rows, LANES), lambda b, i: (0, b)),
                  pl.BlockSpec((kv_rows, LANES), lambda b, i: (0, nblk + b)),
                  pl.BlockSpec((hpb, ATTN_Q, ATTN_W), lambda b, i: (b, 0, 0))],
        out_specs=pl.BlockSpec((ATTN_Q, LANES), lambda b, i: (i, b)),
        compiler_params=_cp("arbitrary", "arbitrary"),
    )(p1, kvp, kvp, band)


def _attn_bwd(p1, kvp, band, dya, stages=None):
    s = p1.shape[0]
    aw = A_HEADS * A_HEAD_DIM
    nblk = aw // LANES
    hpb = LANES // A_HEAD_DIM
    scale = A_HEAD_DIM ** -0.5

    def body(q_ref, k_ref, v_ref, b_ref, do_ref, dq_ref, dk_ref, dv_ref, db_ref):
        @pl.when(pl.program_id(1) == 0)
        def _():
            dk_ref[...] = jnp.zeros_like(dk_ref)
            dv_ref[...] = jnp.zeros_like(dv_ref)
            db_ref[...] = jnp.zeros_like(db_ref)

        base = pl.multiple_of(pl.program_id(1) * ATTN_Q, ATTN_Q)
        window = pl.ds(base, ATTN_W)
        kw = k_ref[window, :]
        vw = v_ref[window, :]
        qv = q_ref[...]
        dov = do_ref[...]
        dq = jnp.zeros((ATTN_Q, LANES), F32)
        dk = jnp.zeros((ATTN_W, LANES), F32)
        dv = jnp.zeros((ATTN_W, LANES), F32)
        for h, mask in enumerate(_head_masks()):
            qh = jnp.where(mask, qv, jnp.zeros_like(qv))
            doh = jnp.where(mask, dov, jnp.zeros_like(dov))
            p = _attn_probs(qh, kw, b_ref[h], base - A_PAD)
            dp = _dot(doh, vw, "nt")
            ds = p * (dp - jnp.sum(p * dp, axis=-1, keepdims=True))
            db_ref[h] += ds
            dsb = (ds * scale).astype(BF16)
            dq = jnp.where(mask, _dot(dsb, kw), dq)
            dk = dk + _dot(dsb, qh, "tn")
            dv = dv + _dot(p.astype(BF16), doh, "tn")
        dq_ref[...] = dq.astype(BF16)
        dk_ref[window, :] += dk
        dv_ref[window, :] += dv

    kv_rows = s + A_PAD
    q_spec = pl.BlockSpec((ATTN_Q, LANES), lambda b, i: (i, b))
    acc_spec = pl.BlockSpec((kv_rows, LANES), lambda b, i: (0, b))
    b_spec = pl.BlockSpec((hpb, ATTN_Q, ATTN_W), lambda b, i: (b, 0, 0))
    return _pcall(
        body, stages, name="attn_bwd",
        out_shape=(jax.ShapeDtypeStruct((s, aw), BF16), jax.ShapeDtypeStruct((kv_rows, aw), F32),
                   jax.ShapeDtypeStruct((kv_rows, aw), F32), jax.ShapeDtypeStruct((A_HEADS, ATTN_Q, ATTN_W), F32)),
        grid=(nblk, s // ATTN_Q),
        in_specs=[q_spec, acc_spec, pl.BlockSpec((kv_rows, LANES), lambda b, i: (0, nblk + b)), b_spec, q_spec],
        out_specs=(q_spec, acc_spec, acc_spec, b_spec), compiler_params=_cp("arbitrary", "arbitrary"),
    )(p1, kvp, kvp, band, dya)


REL_TILE = CHUNK * A_BAND // 8


def _rel_onehot():
    qi = jnp.arange(CHUNK)[:, None]
    ks = jnp.arange(A_BAND)[None, :]
    idx = (jnp.clip(ks - A_PAD - qi, -REL_CLIP, CHUNK - 1) + REL_CLIP).reshape(1, CHUNK * A_BAND)
    return (jnp.arange(REL_SIZE)[:, None] == idx).astype(F32)


def _gla_gate(lr, wa2, balpha):
    z = _dot(lr, wa2) + balpha
    la = (jnp.minimum(z, 0.0) - jnp.log(1.0 + jnp.exp(-jnp.abs(z)))) * (1.0 / GATE_TAU)
    row = lax.broadcasted_iota(jnp.int32, (CHUNK, CHUNK), 0)
    col = lax.broadcasted_iota(jnp.int32, (CHUNK, CHUNK), 1)
    cum = _dot((row >= col).astype(F32), la, precision=HIGHEST)
    return z, la, cum


def _gla_dims(p2):
    kd = p2.shape[1] // 6
    hk = kd // B_HEADS
    hv = 2 * hk
    return kd, hk, hv


GLA_CPS = 4


def _gla_fwd(p2, lrp, wa2p, balpha, gnorm, stages=None):
    s = p2.shape[0]
    kd, hk, hv = _gla_dims(p2)
    nc = s // CHUNK
    cps = GLA_CPS if nc % GLA_CPS == 0 else 1
    rows_per = cps * CHUNK
    qscale = hk ** -0.5

    def body(p_ref, lr_ref, wa_ref, ba_ref, gn_ref, yb_ref, st_ref, state):
        @pl.when(pl.program_id(0) == 0)
        def _():
            state[...] = jnp.zeros_like(state)

        gn = gn_ref[...]
        for sub in range(cps):
            rows = slice(sub * CHUNK, (sub + 1) * CHUNK)
            _, _, cum = _gla_gate(lr_ref[rows, :], wa_ref[...], ba_ref[...])
            last = cum[CHUNK - 1:CHUNK, :]
            e = jnp.exp(last - cum)
            dch = jnp.exp(last)
            for hh in range(B_HEADS):
                ks = slice(hh * hk, (hh + 1) * hk)
                q = p_ref[rows, hh * hk:(hh + 1) * hk].astype(F32)
                k = p_ref[rows, kd + hh * hk:kd + (hh + 1) * hk].astype(F32)
                v = p_ref[rows, 2 * kd + hh * hv:2 * kd + (hh + 1) * hv]
                rg = p_ref[rows, 4 * kd + hh * hv:4 * kd + (hh + 1) * hv].astype(F32)
                kdec = (k * e[:, ks]).astype(BF16)
                st = state[hh] * dch[:, ks] + _dot(v, kdec, "tn")
                state[hh] = st
                st_ref[sub, hh] = st
                o = _dot((q * qscale).astype(BF16), st.astype(BF16), "nt")
                rinv = lax.rsqrt(jnp.mean(o * o, axis=-1, keepdims=True) + RMS_EPS)
                yb_ref[rows, hh * hv:(hh + 1) * hv] = ((o * rinv * gn) * (rg * _sigmoid(rg))).astype(BF16)

    return _pcall(
        body, stages, name="gla_fwd",
        out_shape=(jax.ShapeDtypeStruct((s, 2 * kd), BF16), jax.ShapeDtypeStruct((nc, B_HEADS, hv, hk), F32)),
        grid=(nc // cps,),
        in_specs=[pl.BlockSpec((rows_per, 6 * kd), lambda i: (i, 0)), pl.BlockSpec((rows_per, LANES), lambda i: (i, 0)),
                  pl.BlockSpec((LANES, kd), lambda i: (0, 0)), pl.BlockSpec((1, kd), lambda i: (0, 0)),
                  pl.BlockSpec((1, hv), lambda i: (0, 0))],
        out_specs=(pl.BlockSpec((rows_per, 2 * kd), lambda i: (i, 0)),
                   pl.BlockSpec((cps, B_HEADS, hv, hk), lambda i: (i, 0, 0, 0))),
        scratch_shapes=[pltpu.VMEM((B_HEADS, hv, hk), F32)], compiler_params=_cp("arbitrary"),
    )(p2, lrp, wa2p, balpha, gnorm)


GLA_ROW_DBALPHA, GLA_ROW_DGNORM = 0, 1


def _gla_bwd(p2, lrp, wa2p, balpha, gnorm, states, dyb, stages=None):
    s = p2.shape[0]
    kd, hk, hv = _gla_dims(p2)
    nc = s // CHUNK
    cps = GLA_CPS if nc % GLA_CPS == 0 else 1
    rows_per = cps * CHUNK
    nblk = nc // cps
    qscale = hk ** -0.5

    def body(p_ref, lr_ref, wa_ref, ba_ref, gn_ref, st_ref, sp_ref, dy_ref, dp_ref, dz_ref, sm_ref, gcar):
        i = pl.program_id(0)

        @pl.when(i == 0)
        def _():
            gcar[...] = jnp.zeros_like(gcar)
            sm_ref[...] = jnp.zeros_like(sm_ref)

        block_has_prev = (i < nblk - 1).astype(F32)
        gn = gn_ref[...]
        row = lax.broadcasted_iota(jnp.int32, (CHUNK, CHUNK), 0)
        col = lax.broadcasted_iota(jnp.int32, (CHUNK, CHUNK), 1)
        tri_strict = (row > col).astype(F32)
        for sub in reversed(range(cps)):
            rows = slice(sub * CHUNK, (sub + 1) * CHUNK)
            z, _, cum = _gla_gate(lr_ref[rows, :], wa_ref[...], ba_ref[...])
            last = cum[CHUNK - 1:CHUNK, :]
            e = jnp.exp(last - cum)
            dch = jnp.exp(last)
            sgn = _sigmoid(-z) * (1.0 / GATE_TAU)
            for hh in range(B_HEADS):
                ks = slice(hh * hk, (hh + 1) * hk)
                q = p_ref[rows, hh * hk:(hh + 1) * hk].astype(F32)
                k = p_ref[rows, kd + hh * hk:kd + (hh + 1) * hk].astype(F32)
                v = p_ref[rows, 2 * kd + hh * hv:2 * kd + (hh + 1) * hv]
                rg = p_ref[rows, 4 * kd + hh * hv:4 * kd + (hh + 1) * hv].astype(F32)
                kdecf = k * e[:, ks]
                kdec = kdecf.astype(BF16)
                st16 = st_ref[sub, hh].astype(BF16)
                prev = st_ref[sub - 1, hh] if sub > 0 else sp_ref[hh] * block_has_prev
                qs = (q * qscale).astype(BF16)
                o = _dot(qs, st16, "nt")
                rinv = lax.rsqrt(jnp.mean(o * o, axis=-1, keepdims=True) + RMS_EPS)
                dy = dy_ref[rows, hh * hv:(hh + 1) * hv].astype(F32)
                sg = _sigmoid(rg)
                onorm = o * rinv
                drg = dy * (onorm * gn) * (sg * (1.0 + rg * (1.0 - sg)))
                dob = dy * (rg * sg)
                sm_ref[GLA_ROW_DGNORM:GLA_ROW_DGNORM + 1, 0:hv] += _colsum(dob * onorm)
                t = dob * gn
                do = rinv * (t - onorm * jnp.mean(t * onorm, axis=-1, keepdims=True))
                do16 = do.astype(BF16)
                dq = _dot(do16, st16) * qscale
                gt = _dot(do16, qs, "tn") + gcar[hh]
                gcar[hh] = gt * dch[:, ks]
                dd = _colsum(gt * prev)
                gt16 = gt.astype(BF16)
                dkdec = _dot(v, gt16)
                dv = _dot(kdec, gt16, "nt")
                dla = dd * dch[:, ks] + _dot(tri_strict, dkdec * kdecf, precision=HIGHEST)
                dzh = dla * sgn[:, ks]
                sm_ref[GLA_ROW_DBALPHA:GLA_ROW_DBALPHA + 1, hh * hk:(hh + 1) * hk] += _colsum(dzh)
                dz_ref[rows, hh * hk:(hh + 1) * hk] = dzh.astype(BF16)
                dp_ref[rows, hh * hk:(hh + 1) * hk] = dq.astype(BF16)
                dp_ref[rows, kd + hh * hk:kd + (hh + 1) * hk] = (dkdec * e[:, ks]).astype(BF16)
                dp_ref[rows, 2 * kd + hh * hv:2 * kd + (hh + 1) * hv] = dv.astype(BF16)
                dp_ref[rows, 4 * kd + hh * hv:4 * kd + (hh + 1) * hv] = drg.astype(BF16)

    rev = lambda i: (nblk - 1 - i, 0)
    return _pcall(
        body, stages, name="gla_bwd",
        out_shape=(jax.ShapeDtypeStruct((s, 6 * kd), BF16), jax.ShapeDtypeStruct((s, kd), BF16),
                   jax.ShapeDtypeStruct((SUBLANES, kd), F32)),
        grid=(nblk,),
        in_specs=[pl.BlockSpec((rows_per, 6 * kd), rev), pl.BlockSpec((rows_per, LANES), rev),
                  pl.BlockSpec((LANES, kd), lambda i: (0, 0)), pl.BlockSpec((1, kd), lambda i: (0, 0)),
                  pl.BlockSpec((1, hv), lambda i: (0, 0)),
                  pl.BlockSpec((cps, B_HEADS, hv, hk), lambda i: (nblk - 1 - i, 0, 0, 0)),
                  pl.BlockSpec((None, B_HEADS, hv, hk), lambda i: (jnp.maximum((nblk - 1 - i) * cps - 1, 0), 0, 0, 0)),
                  pl.BlockSpec((rows_per, 2 * kd), rev)],
        out_specs=(pl.BlockSpec((rows_per, 6 * kd), rev), pl.BlockSpec((rows_per, kd), rev),
                   pl.BlockSpec((SUBLANES, kd), lambda i: (0, 0))),
        scratch_shapes=[pltpu.VMEM((B_HEADS, hv, hk), F32)], compiler_params=_cp("arbitrary"),
    )(p2, lrp, wa2p, balpha, gnorm, states, states, dyb)


def _merge_fwd(ya, yb, wpa, wpb, g):
    s, ka = ya.shape
    kb = yb.shape[1]
    d = wpa.shape[1]
    tm, tn = _tile(s, (1024, 512)), _tile(d, (512,))

    def body(ya_ref, yb_ref, wa_ref, wb_ref, g_ref, m_ref, pab_ref):
        yav, ybv = ya_ref[...], yb_ref[...]
        for cols in _strips(tn):
            pa = _dot(yav, wa_ref[:, cols])
            pb = _dot(ybv, wb_ref[:, cols])
            m_ref[:, cols] = (_sigmoid(g_ref[0, :, cols].astype(F32)) * pa
                              + _sigmoid(g_ref[1, :, cols].astype(F32)) * pb).astype(BF16)
            pab_ref[0, :, cols] = pa.astype(BF16)
            pab_ref[1, :, cols] = pb.astype(BF16)

    st = pl.BlockSpec((2, tm, tn), lambda i, j: (0, i, j))
    return pl.pallas_call(
        body, name="merge_fwd",
        out_shape=(jax.ShapeDtypeStruct((s, d), BF16), jax.ShapeDtypeStruct((2, s, d), BF16)),
        grid=(s // tm, d // tn),
        in_specs=[pl.BlockSpec((tm, ka), lambda i, j: (i, 0)), pl.BlockSpec((tm, kb), lambda i, j: (i, 0)),
                  pl.BlockSpec((ka, tn), lambda i, j: (0, j)), pl.BlockSpec((kb, tn), lambda i, j: (0, j)), st],
        out_specs=(pl.BlockSpec((tm, tn), lambda i, j: (i, j)), st),
        compiler_params=_cp("parallel", "parallel"),
    )(ya, yb, wpa, wpb, g)


def _merge_bwd(dm, wmo, g, pab, stages=None):
    s, d = dm.shape
    tm, tn = _tile(s, (1024, 512)), _tile(d, (512,))

    def body(dm_ref, w_ref, g_ref, pab_ref, dpab_ref, dg_ref):
        dmv = dm_ref[...]
        for cols in _strips(tn):
            dmg = _dot(dmv, w_ref[cols, :], "nt")
            for j in range(2):
                sg = _sigmoid(g_ref[j, :, cols].astype(F32))
                dpab_ref[j, :, cols] = (dmg * sg).astype(BF16)
                dg_ref[j, :, cols] = (dmg * pab_ref[j, :, cols].astype(F32) * (sg * (1.0 - sg))).astype(BF16)

    st = pl.BlockSpec((2, tm, tn), lambda i, j: (0, i, j))
    return _pcall(
        body, stages, name="merge_bwd",
        out_shape=(jax.ShapeDtypeStruct((2, s, d), BF16), jax.ShapeDtypeStruct((2, s, d), BF16)),
        grid=(s // tm, d // tn),
        in_specs=[pl.BlockSpec((tm, d), lambda i, j: (i, 0)), pl.BlockSpec((tn, d), lambda i, j: (j, 0)), st, st],
        out_specs=(st, st), compiler_params=_cp("arbitrary", "arbitrary"),
    )(dm, wmo, g, pab)


def _virtual_rows(parts, lo, hi):
    out, off = [], 0
    for p in parts:
        a, b = max(lo, off), min(hi, off + p.shape[0])
        if a < b:
            out.append(p[a - off:b - off])
        off += p.shape[0]
    return out[0] if len(out) == 1 else jnp.concatenate(out, axis=0)


def _mix_in_row_groups(d):
    o1 = 3 * A_HEADS * A_HEAD_DIM
    o2 = o1 + 6 * (d // 4)
    o3 = o2 + GATE_RANK
    return (0, o1), (o1, o2), (o2, o3), (o3, o3 + 2 * d)


def _split_mix_in(stacked):
    d = stacked.shape[2]
    flat = stacked.reshape(-1, d)
    _, _, (lo, hi), (glo, ghi) = _mix_in_row_groups(d)
    return flat, jnp.pad(flat[lo:hi], ((0, LANES - GATE_RANK), (0, 0))), flat[glo:ghi]


MIX_TILE = 1024


def _mix_in_weights(plan):
    return plan.memo("mix_in_weights", lambda: _split_mix_in(plan.weight("w_mix_in")))


def _hosted_mm(plan):
    return lambda name, *a, **k: plan.host(name, lambda st: _mm(name, *a, stages=st, **k))


def _mix_forward(u2, plan, small):
    s, d = u2.shape
    wt, wt_lr, wt_g = _mix_in_weights(plan)
    bias, wa2p, balpha, gnorm = small
    mm = _hosted_mm(plan)
    aw = A_HEADS * A_HEAD_DIM
    tm, tn = _tile(s, (1024,)), MIX_TILE
    (_, na), (_, nab) = _mix_in_row_groups(d)[:2]
    assert na % tn == 0 and nab % tn == 0
    p1 = mm("mix_in_a", "nt", u2, wt, (s, na, d), tm=tm, tn=tn, tk=d, out_dtype=BF16)
    p2 = mm("mix_in_b", "nt", u2, wt, (s, nab - na, d), tm=tm, tn=tn, tk=d, out_dtype=BF16,
            b_spec=pl.BlockSpec((tn, d), lambda i, j, kk: (na // tn + j, 0)))
    lrp = mm("mix_in_lr", "nt", u2, wt_lr, (s, LANES, d), tm=tm, tn=LANES, tk=d, out_dtype=BF16)
    nbg = d // tn
    g = mm("mix_in_g", "nt", u2, wt_g, (s, 2 * d, d), tm=tm, tn=tn, tk=d, out_dtype=BF16, out_shape=(2, s, d),
           o_spec=pl.BlockSpec((None, tm, tn), lambda i, j, kk: (j // nbg, i, j % nbg)))
    kvp = jnp.pad(p1[:, aw:], ((A_PAD, 0), (0, 0)))
    ya = plan.host("attn_fwd", lambda st: _attn_fwd(p1, kvp, bias, st))
    yb, states = plan.host("gla_fwd", lambda st: _gla_fwd(p2, lrp, wa2p, balpha, gnorm, st))
    merged, pab = _merge_fwd(ya, yb, plan.weight("w_proj_a"), plan.weight("w_proj_b"), g)
    m = mm("mix_out", "nn", merged, plan.weight("w_mix_out"), (s, d, d), tm=tm, tn=tn, tk=d)
    return m, (p1, kvp, p2, lrp, states, ya, yb, g, pab, merged)


def _mix_backward(dm, u2, saved, plan, small):
    s, d = u2.shape
    wt, wt_lr, wt_g = _mix_in_weights(plan)
    wpa, wpb, wmo = plan.weight("w_proj_a"), plan.weight("w_proj_b"), plan.weight("w_mix_out")
    bias, wa2p, balpha, gnorm = small
    p1, kvp, p2, lrp, states, ya, yb, g, pab, merged = saved
    mm = _hosted_mm(plan)
    aw = A_HEADS * A_HEAD_DIM
    kd = d // 4
    t = MIX_TILE
    tm = _tile(s, (1024,))
    tks = _tile(s, LONG_K)

    plan.grad("w_mix_out", mm("mix_dw_out", "tn", merged, dm, (d, d, s), tm=t, tn=t, tk=tks))
    dpab, dg = plan.host("merge_bwd", lambda st: _merge_bwd(dm, wmo, g, pab, st))
    sel = lambda j: pl.BlockSpec((None, tm, d), lambda i, jj, kk: (j, i, 0))
    dya = mm("mix_dya", "nt", dpab, wpa, (s, aw, d), tm=tm, tn=t, tk=d, out_dtype=BF16, a_spec=sel(0))
    dyb = mm("mix_dyb", "nt", dpab, wpb, (s, 2 * kd, d), tm=tm, tn=t, tk=d, out_dtype=BF16, a_spec=sel(1))
    selk = lambda j: pl.BlockSpec((None, tks, t), lambda i, jj, kk: (j, kk, jj))
    plan.grad("w_proj_a", mm("mix_dwpa", "tn", ya, dpab, (aw, d, s), tm=t, tn=t, tk=tks, b_spec=selk(0)))
    plan.grad("w_proj_b", mm("mix_dwpb", "tn", yb, dpab, (2 * kd, d, s), tm=t, tn=t, tk=tks, b_spec=selk(1)))

    dq, dkp, dvp, dbias = plan.host("attn_bwd", lambda st: _attn_bwd(p1, kvp, bias, dya, st))
    dp1 = jnp.concatenate([dq, dkp[A_PAD:].astype(BF16), dvp[A_PAD:].astype(BF16)], axis=1)
    dp2, dz, gsm = plan.host("gla_bwd", lambda st: _gla_bwd(p2, lrp, wa2p, balpha, gnorm, states, dyb, st))
    dlrp = mm("gla_dlr", "nt", dz, wa2p, (s, LANES, kd), tm=tm, tn=LANES, tk=kd, out_dtype=BF16)
    dwa2p = mm("gla_dwa2", "tn", lrp, dz, (LANES, kd, s), tm=LANES, tn=kd, tk=tks)

    tka = 3 * aw
    assert 6 * kd == tka
    du = mm("mix_du_a", "nn", dp1, wt, (s, d, tka), tm=tm, tn=t, tk=tka)
    du = mm("mix_du_b", "nn", dp2, wt, (s, d, tka), tm=tm, tn=t, tk=tka, add=du,
            b_spec=pl.BlockSpec((tka, t), lambda i, j, kk: (1 + kk, j)))
    du = mm("mix_du_g", "nn", dg, wt_g, (s, d, 2 * d), tm=tm, tn=t, tk=d, add=du, thin=(dlrp, wt_lr),
            a_spec=pl.BlockSpec((None, tm, d), lambda i, j, kk: (kk, i, 0)))
    nkg = d // t
    dw1 = mm("mix_dw_a", "tn", dp1, u2, (3 * aw, d, s), tm=t, tn=t, tk=tks)
    dw2 = mm("mix_dw_b", "tn", dp2, u2, (6 * kd, d, s), tm=t, tn=t, tk=tks)
    dwlr = mm("mix_dw_lr", "tn", dlrp, u2, (LANES, d, s), tm=LANES, tn=t, tk=tks)
    dwg = mm("mix_dw_g", "tn", dg, u2, (2 * d, d, s), tm=t, tn=t, tk=tks,
             a_spec=pl.BlockSpec((None, tks, t), lambda i, j, kk: (i // nkg, kk, i % nkg)))
    pieces = [dw1, dw2, dwlr[:GATE_RANK], dwg]
    shard_rows = sum(p.shape[0] for p in pieces) // N_CHIPS
    plan.grad("w_mix_in", jnp.stack([_virtual_rows(pieces, j * shard_rows, (j + 1) * shard_rows)
                                     for j in range(N_CHIPS)]))
    return du, (dbias, dwa2p[:GATE_RANK], gsm)


def _device_step(x, target, mod, small, plan):
    s, d = x.shape
    row = lambda i: mod[i:i + 1]
    sh1, sc1, g1, sh2, sc2, g2, sh3, sc3, g3 = (row(i) for i in range(N_MOD))

    onehot = _rel_onehot()
    bias = _mm("rel_bias_expand", "nn", small["rel_bias"], onehot, (A_HEADS, CHUNK * A_BAND, REL_SIZE),
               tm=A_HEADS, tn=REL_TILE, tk=REL_SIZE, precision=HIGHEST).reshape(A_HEADS, CHUNK, A_BAND)
    bias = _band_bias(bias)
    wa2p = jnp.pad(small["w_alpha2"], ((0, LANES - GATE_RANK), (0, 0))).astype(BF16)
    mix_small = (bias, wa2p, small["b_alpha"], small["gla_norm_g"])

    u1, u1t = _modulate("mod1", x, sh1, sc1)
    f1, sv1 = _ffn_forward("ffn1", u1, plan)
    h1, u2 = _resid_ln_fwd("ln1_fwd", x, f1, g1, small["ln1_g"], small["ln1_b"], sh2, sc2, 0.5)
    m, svm = _mix_forward(u2, plan, mix_small)
    h2, u3, u3t = _resid_ln_fwd("ln2_fwd", h1, m, g2, small["ln2_g"], small["ln2_b"], sh3, sc3, 1.0, transposed=True)
    f2, sv2 = _ffn_forward("ffn2", u3, plan)

    dr3, df2, acc3 = _final_ln_loss_bwd("ln3_loss_bwd", h2, f2, target, g3, small["ln3_g"], small["ln3_b"], 0.5)
    du3 = _ffn_backward("ffn2", df2, u3t, sv2, plan, in_first=False)
    dr2, dmx, acc2 = _resid_ln_bwd("ln2_bwd", du3, dr3, h1, m, sc3, g2, small["ln2_g"], small["ln2_b"], 1.0)
    du2, (dbias, dwa2, gsm) = _mix_backward(dmx, u2, svm, plan, mix_small)
    dr1, df1, acc1 = _resid_ln_bwd("ln1_bwd", du2, dr2, x, f1, sc2, g1, small["ln1_g"], small["ln1_b"], 0.5)
    du1 = _ffn_backward("ffn1", df1, u1t, sv1, plan, in_first=True)
    grad_x, acc0 = _input_grad("input_grad", du1, dr1, x, sc1)

    drel = _hosted_mm(plan)("rel_bias_grad", "nt", _band_bias_grad(dbias).reshape(A_HEADS, CHUNK * A_BAND), onehot,
                            (A_HEADS, REL_SIZE, CHUNK * A_BAND), tm=A_HEADS, tn=REL_SIZE, tk=REL_TILE, precision=HIGHEST)
    loss = jnp.sum(acc3[ROW_LOSS])
    dmod = jnp.stack([acc0[ROW_DSH], acc0[ROW_DSC], acc1[ROW_DGATE], acc1[ROW_DSH], acc1[ROW_DSC], acc2[ROW_DGATE],
                      acc2[ROW_DSH], acc2[ROW_DSC], acc3[ROW_DGATE]])
    kd = d // 4
    small_grads = dict(ln1_g=acc1[ROW_DLN_G], ln1_b=acc1[ROW_DLN_B], ln2_g=acc2[ROW_DLN_G], ln2_b=acc2[ROW_DLN_B],
                       ln3_g=acc3[ROW_DLN_G], ln3_b=acc3[ROW_DLN_B], b_alpha=gsm[GLA_ROW_DBALPHA],
                       gla_norm_g=gsm[GLA_ROW_DGNORM, :kd // B_HEADS * 2], rel_bias=drel, w_alpha2=dwa2)
    return loss, grad_x, small_grads, dmod


HBM_SPEC = pl.BlockSpec(memory_space=pl.ANY)


def _mesh_pos():
    return lax.axis_index("x"), lax.axis_index("y"), lax.axis_index("c")


def _other_chips(x, y):
    return [(1 - x, y), (x, 1 - y), (1 - x, 1 - y)]


def _remote(src, dst, send_sem, recv_sem, to):
    return pltpu.make_async_remote_copy(src_ref=src, dst_ref=dst, send_sem=send_sem, recv_sem=recv_sem,
                                        device_id=to, device_id_type=MESH)


def _allgather_rows(name, v):
    m_per, n = v.shape

    def body(x_ref, out_ref, send_sems, recv_sems, local_sem):
        x, y, c = _mesh_pos()
        me, sibling = (x, y, c), (x, y, 1 - c)
        chips = _other_chips(x, y)

        def rows(px, py, pc):
            return out_ref.at[pl.ds((4 * px + 2 * py + pc) * m_per, m_per), :]

        def copy(k, block, to, src=None):
            return _remote(rows(*block) if src is None else src, rows(*block), send_sems.at[k], recv_sems.at[k], to)

        mine = pltpu.make_async_copy(x_ref, rows(*me), local_sem)
        mine.start()
        first = [copy(0, me, sibling, src=x_ref)]
        first += [copy(1 + j, me, (*chip, c), src=x_ref) for j, chip in enumerate(chips)]
        for cp in first:
            cp.start()
        passed = [copy(4 + j, (*chip, c), sibling) for j, chip in enumerate(chips)]
        for j, chip in enumerate(chips):
            copy(1 + j, (*chip, c), me).wait_recv()
            passed[j].start()
        copy(0, sibling, me).wait_recv()
        for j, chip in enumerate(chips):
            copy(4 + j, (*chip, 1 - c), me).wait_recv()
        for cp in first + passed:
            cp.wait_send()
        mine.wait()

    return pl.pallas_call(
        body, name=name, out_shape=jax.ShapeDtypeStruct((N_DEV * m_per, n), v.dtype),
        in_specs=[pl.BlockSpec(memory_space=pltpu.VMEM)], out_specs=pl.BlockSpec(memory_space=pltpu.VMEM),
        scratch_shapes=[pltpu.SemaphoreType.DMA((7,)), pltpu.SemaphoreType.DMA((7,)), pltpu.SemaphoreType.DMA],
    )(v)


def _allgather_weights(bufs):
    n = len(bufs)
    TO_X, TO_Y, PASS_TO_X, PASS_TO_Y, SIB_X, SIB_Y, SIB_D0, SIB_D1 = range(8)

    def body(*refs):
        ins, outs = refs[:n], refs[n:2 * n]
        send_sems, recv_sems = refs[2 * n:]
        x, y, c = _mesh_pos()
        sibling = (x, y, 1 - c)
        xn, yn, dg = _other_chips(x, y)
        j0, jx, jy, jd = (2 * p[0] + p[1] for p in ((x, y), xn, yn, dg))
        sends = []

        def rows(w, hc, quarter=None):
            hr = bufs[w].shape[1] // 2
            if quarter is None:
                return pl.ds(hc * hr, hr)
            return pl.ds(hc * hr + quarter * (hr // 2), hr // 2)

        def push(src, dst, w, k, to):
            cp = _remote(src, dst, send_sems.at[w, k], recv_sems.at[w, k], to)
            cp.start()
            sends.append(cp)

        def landed(piece, w, k):
            _remote(piece, piece, send_sems.at[w, k], recv_sems.at[w, k], sibling).wait_recv()

        for w in range(n):
            mine = rows(w, c)
            push(ins[w].at[j0, mine, :], outs[w].at[j0, mine, :], w, TO_X, (*xn, c))
            push(ins[w].at[j0, mine, :], outs[w].at[j0, mine, :], w, TO_Y, (*yn, c))
        for w in range(n):
            half_x = outs[w].at[jx, rows(w, c), :]
            landed(half_x, w, TO_X)
            quarter = outs[w].at[jx, rows(w, c, 1), :]
            push(quarter, quarter, w, PASS_TO_Y, (*yn, c))
            push(half_x, half_x, w, SIB_X, sibling)
            half_y = outs[w].at[jy, rows(w, c), :]
            landed(half_y, w, TO_Y)
            quarter = outs[w].at[jy, rows(w, c, 0), :]
            push(quarter, quarter, w, PASS_TO_X, (*xn, c))
            push(half_y, half_y, w, SIB_Y, sibling)
        for w in range(n):
            for q, arrives_on, on in ((0, PASS_TO_X, SIB_D0), (1, PASS_TO_Y, SIB_D1)):
                piece = outs[w].at[jd, rows(w, c, q), :]
                landed(piece, w, arrives_on)
                push(piece, piece, w, on, sibling)
        for w in range(n):
            landed(outs[w].at[jx, rows(w, 1 - c), :], w, SIB_X)
            landed(outs[w].at[jy, rows(w, 1 - c), :], w, SIB_Y)
            landed(outs[w].at[jd, rows(w, 1 - c, 0), :], w, SIB_D0)
            landed(outs[w].at[jd, rows(w, 1 - c, 1), :], w, SIB_D1)
        for cp in sends:
            cp.wait_send()

    return pl.pallas_call(
        body, name="allgather_weights", out_shape=[jax.ShapeDtypeStruct(b.shape, b.dtype) for b in bufs],
        in_specs=[HBM_SPEC] * n, out_specs=[HBM_SPEC] * n, input_output_aliases={w: w for w in range(n)},
        scratch_shapes=[pltpu.SemaphoreType.DMA((n, 8)), pltpu.SemaphoreType.DMA((n, 8))],
    )(*bufs)


def _half(ref, hc, col, *lead):
    rows, cols = ref.shape[-2:]
    if col:
        return ref.at[(*lead, slice(None), pl.ds(hc * (cols // 2), cols // 2))]
    return ref.at[(*lead, pl.ds(hc * (rows // 2), rows // 2), slice(None))]


def _half_shape(shape, col):
    return shape[:-2] + ((shape[-2], shape[-1] // 2) if col else (shape[-2] // 2, shape[-1]))


def _quarter(ref, hc, q, col, *lead):
    rows, cols = ref.shape[-2:]
    if col:
        return ref.at[(*lead, slice(None), pl.ds(hc * (cols // 2) + q * (cols // 4), cols // 4))]
    return ref.at[(*lead, pl.ds(hc * (rows // 2) + q * (rows // 4), rows // 4), slice(None))]


def _stage_gather_ici(bufs, cols):
    n = len(bufs)
    TO_X, TO_Y, PASS_TO_X, PASS_TO_Y = range(4)

    def places():
        x, y, c = _mesh_pos()
        xn, yn, dg = _other_chips(x, y)
        return c, (*xn, c), (*yn, c), [2 * p[0] + p[1] for p in ((x, y), xn, yn, dg)]

    def remote(src, dst, send, recv, w, k, to):
        return _remote(src, dst, send.at[4 * w + k], recv.at[4 * w + k], to)

    def own(ins, outs, send, recv):
        c, to_x, to_y, (j0, _, _, _) = places()
        for w in range(n):
            for k, to in ((TO_X, to_x), (TO_Y, to_y)):
                yield remote(_half(ins[w], c, cols[w], j0), _half(outs[w], c, cols[w], j0), send, recv, w, k, to)

    def relays(ins, outs, send, recv):
        c, to_x, to_y, (_, jx, jy, _) = places()
        for w in range(n):
            for j, k, q, pass_k, to in ((jx, TO_X, 1, PASS_TO_Y, to_y), (jy, TO_Y, 0, PASS_TO_X, to_x)):
                half = _half(outs[w], c, cols[w], j)
                piece = _quarter(outs[w], c, q, cols[w], j)
                yield remote(half, half, send, recv, w, k, to), remote(piece, piece, send, recv, w, pass_k, to)

    def passed(ins, outs, send, recv):
        c, to_x, _, (_, _, _, jd) = places()
        for w in range(n):
            for q, k in ((0, PASS_TO_X), (1, PASS_TO_Y)):
                piece = _quarter(outs[w], c, q, cols[w], jd)
                yield remote(piece, piece, send, recv, w, k, to_x)

    def start(*refs):
        for cp in own(*refs):
            cp.start()

    def relay(*refs):
        for arrived, onward in relays(*refs):
            arrived.wait_recv()
            onward.start()

    def finish(*refs):
        for cp in passed(*refs):
            cp.wait_recv()
        for cp in own(*refs):
            cp.wait_send()
        for _, onward in relays(*refs):
            onward.wait_send()

    outs = [jax.ShapeDtypeStruct(b.shape, b.dtype) for b in bufs]
    return _Stage(bufs, outs, 4 * n, start, finish, aliases={w: w for w in range(n)}, relay=relay)


def _stage_gather_d2d(partial, cols):
    n = len(partial)

    def copies(ins, outs, send, recv):
        x, y, c = _mesh_pos()
        for w in range(n):
            for r, chip in enumerate(_other_chips(x, y)):
                jr = 2 * chip[0] + chip[1]
                mine = _remote(_half(ins[w], c, cols[w], jr), _half(outs[w], c, cols[w], jr), send.at[3 * w + r],
                               recv.at[3 * w + r], (x, y, 1 - c))
                got = _half(outs[w], 1 - c, cols[w], jr)
                yield mine, _remote(got, got, send.at[3 * w + r], recv.at[3 * w + r], (x, y, 1 - c))

    def start(*refs):
        for mine, _ in copies(*refs):
            mine.start()

    def finish(*refs):
        pairs = list(copies(*refs))
        for _, theirs in pairs:
            theirs.wait_recv()
        for mine, _ in pairs:
            mine.wait_send()

    outs = [jax.ShapeDtypeStruct(p.shape, p.dtype) for p in partial]
    return _Stage(partial, outs, 3 * n, start, finish, aliases={w: w for w in range(n)})


def _stage_exchange_halves(grads, cols):
    n = len(grads)

    def copies(ins, outs, send, recv):
        x, y, c = _mesh_pos()
        for w in range(n):
            yield _remote(_half(ins[w], 1 - c, cols[w], slice(None)), outs[w], send.at[w], recv.at[w], (x, y, 1 - c))

    def start(*refs):
        for cp in copies(*refs):
            cp.start()

    def finish(*refs):
        cps = list(copies(*refs))
        for cp in cps:
            cp.wait_recv()
        for cp in cps:
            cp.wait_send()

    outs = [jax.ShapeDtypeStruct(_half_shape(g.shape, col), g.dtype) for g, col in zip(grads, cols)]
    return _Stage(grads, outs, n, start, finish)


def _stage_scatter(parts):
    n = len(parts)

    def copies(ins, outs, send, recv):
        x, y, c = _mesh_pos()
        for w in range(n):
            for r, chip in enumerate(_other_chips(x, y)):
                jr = 2 * chip[0] + chip[1]
                yield _remote(ins[w].at[jr], outs[w].at[r], send.at[3 * w + r], recv.at[3 * w + r], (*chip, c))

    def start(*refs):
        for cp in copies(*refs):
            cp.start()

    def finish(*refs):
        cps = list(copies(*refs))
        for cp in cps:
            cp.wait_recv()
        for cp in cps:
            cp.wait_send()

    outs = [jax.ShapeDtypeStruct((3,) + p.shape[1:], p.dtype) for p in parts]
    return _Stage(parts, outs, 3 * n, start, finish)


def _stage_share(fulls, cols):
    n = len(fulls)

    def copies(ins, outs, send, recv):
        x, y, c = _mesh_pos()
        for w in range(n):
            theirs = _half(outs[w], 1 - c, cols[w])
            yield (_remote(_half(ins[w], c, cols[w]), _half(outs[w], c, cols[w]), send.at[w], recv.at[w], (x, y, 1 - c)),
                   _remote(theirs, theirs, send.at[w], recv.at[w], (x, y, 1 - c)))

    def start(*refs):
        for mine, _ in copies(*refs):
            mine.start()

    def finish(*refs):
        pairs = list(copies(*refs))
        for _, theirs in pairs:
            theirs.wait_recv()
        for mine, _ in pairs:
            mine.wait_send()

    outs = [jax.ShapeDtypeStruct(h.shape, h.dtype) for h in fulls]
    return _Stage(fulls, outs, n, start, finish, aliases={w: w for w in range(n)})


def _run_stages(name, stages):
    return _pcall(None, stages, name=name, out_shape=[], in_specs=[], out_specs=[])()[1]


TILE_BYTES = 2 * 1024 * 1024
SUM_TILE_BYTES = 4 * 1024 * 1024


def _row_tile(rows, cols, itemsize=4, tile_bytes=TILE_BYTES):
    for t in range(min(rows, tile_bytes // (cols * itemsize)) // SUBLANES * SUBLANES, 0, -SUBLANES):
        if rows % t == 0:
            return t
    return rows


def _col_tile(rows, cols, itemsize=4, tile_bytes=TILE_BYTES):
    for t in (2048, 1024, 512, 256, 128):
        if cols % t == 0 and t * rows * itemsize <= tile_bytes:
            return t
    return cols


def _tiling(rows, cols, col, tile_bytes=TILE_BYTES):
    if col:
        tc = _col_tile(rows, cols, tile_bytes=tile_bytes)
        return (rows, tc), cols // tc
    tr = _row_tile(rows, cols, tile_bytes=tile_bytes)
    return (tr, cols), rows // tr


def _strip(col, i):
    return (0, i) if col else (i, 0)


def _pair_sum(name, g, recv, core, col):
    blk, nb = _tiling(*recv.shape[1:], col, tile_bytes=SUM_TILE_BYTES)

    def body(c_ref, g_ref, r_ref, o_ref):
        o_ref[...] = (g_ref[...] + r_ref[...]).astype(BF16)

    grid_spec = pltpu.PrefetchScalarGridSpec(
        num_scalar_prefetch=1, grid=(N_CHIPS, nb),
        in_specs=[pl.BlockSpec((None,) + blk, lambda j, i, cr: (j,) + _strip(col, cr[0] * nb + i)),
                  pl.BlockSpec((None,) + blk, lambda j, i, cr: (j,) + _strip(col, i))],
        out_specs=pl.BlockSpec((None,) + blk, lambda j, i, cr: (j,) + _strip(col, i)))
    return pl.pallas_call(body, name=name, out_shape=jax.ShapeDtypeStruct(recv.shape, BF16), grid_spec=grid_spec,
                          compiler_params=_cp("parallel", "parallel"))(core, g, recv)


def _quad_sum(name, own, landed, chip_core, col):
    rows, cols = landed.shape[1:]
    blk, nb = _tiling(rows, cols, col, tile_bytes=SUM_TILE_BYTES)
    full = (rows, 2 * cols) if col else (2 * rows, cols)

    def body(cc_ref, own_ref, l_ref, o_ref):
        o_ref[...] = ((own_ref[...].astype(F32) + l_ref[0].astype(F32)) + l_ref[1].astype(F32)) + l_ref[2].astype(F32)

    grid_spec = pltpu.PrefetchScalarGridSpec(
        num_scalar_prefetch=1, grid=(nb,),
        in_specs=[pl.BlockSpec((None,) + blk, lambda i, cc: (cc[0],) + _strip(col, i)),
                  pl.BlockSpec((3,) + blk, lambda i, cc: (0,) + _strip(col, i))],
        out_specs=pl.BlockSpec(blk, lambda i, cc: _strip(col, cc[1] * nb + i)))
    return pl.pallas_call(body, name=name, out_shape=jax.ShapeDtypeStruct(full, F32), grid_spec=grid_spec,
                          compiler_params=_cp("arbitrary"))(chip_core, own, landed)


def _device_sum(name, gathered):
    def body(g_ref, o_ref):
        total = g_ref[0]
        for k in range(1, N_DEV):
            total = total + g_ref[k]
        o_ref[...] = total

    return pl.pallas_call(body, name=name, out_shape=jax.ShapeDtypeStruct(gathered.shape[1:], F32))(gathered)


def _adamw(name, w, g, m, v):
    rows, cols = w.shape
    col = rows % SUBLANES != 0
    blk, nb = _tiling(rows, cols, col)
    bc1 = 1.0 - ADAM_B1 ** ADAM_STEP
    bc2 = 1.0 - ADAM_B2 ** ADAM_STEP

    def body(w_ref, g_ref, m_ref, v_ref, d_ref, mo_ref, vo_ref):
        gv = g_ref[...]
        mn = ADAM_B1 * m_ref[...] + (1.0 - ADAM_B1) * gv
        vn = ADAM_B2 * v_ref[...] + (1.0 - ADAM_B2) * (gv * gv)
        mo_ref[...] = mn
        vo_ref[...] = vn
        d_ref[...] = -ADAM_LR * ((mn / bc1) / (jnp.sqrt(vn / bc2) + ADAM_EPS) + ADAM_WD * w_ref[...])

    spec = pl.BlockSpec(blk, lambda i: _strip(col, i))
    return pl.pallas_call(
        body, name=name, out_shape=[jax.ShapeDtypeStruct((rows, cols), F32)] * 3, grid=(nb,),
        in_specs=[spec] * 4, out_specs=[spec] * 3, compiler_params=_cp("parallel"),
    )(w, g, m, v)


WEIGHTS = ["w_ada", "b_ada", "ffn1_w_in", "ffn1_w_out", "ln1_g", "ln1_b", "w_mix_in", "rel_bias", "w_alpha2",
           "b_alpha", "gla_norm_g", "w_proj_a", "w_proj_b", "w_mix_out", "ln2_g", "ln2_b", "ffn2_w_in", "ffn2_w_out",
           "ln3_g", "ln3_b"]
BIG = {"ffn1_w_in": True, "ffn1_w_out": False, "w_mix_in": False, "w_proj_a": True, "w_proj_b": True,
       "w_mix_out": False, "ffn2_w_in": True, "ffn2_w_out": False}
TRANSPOSED = ("w_mix_in",)
STACKED = ("ffn1_w_in", "ffn2_w_in", "w_mix_in")
GROUP_FFN1 = ("ffn1_w_in", "ffn1_w_out")
GROUP_PROJ = ("w_proj_a", "w_proj_b", "w_mix_out")
SMALL = ["ln1_g", "ln1_b", "ln2_g", "ln2_b", "ln3_g", "ln3_b", "b_alpha", "gla_norm_g", "rel_bias", "w_alpha2"]


def _pad_rows(vec, rows=SUBLANES):
    per = -(-vec.shape[0] // (rows * LANES)) * LANES
    return jnp.pad(vec, (0, rows * per - vec.shape[0])).reshape(rows, per)


def _silu(v):
    return v * _sigmoid(v)


class _MeshPlan:
    def __init__(self, shards, chip, core):
        self.shapes = {k: v.shape for k, v in shards.items()}
        self.slots = {k: lax.dynamic_update_slice(lax.empty((N_CHIPS,) + v.shape, v.dtype), v[None], (chip, 0, 0))
                      for k, v in shards.items()}
        self.core1 = core.astype(jnp.int32).reshape(1)
        self.chip_core = jnp.stack([chip, core]).astype(jnp.int32)
        self.partial, self.full, self.local, self.pair, self.half, self.final, self.memos = {}, {}, {}, {}, {}, {}, {}
        ici, d2d, x1, x2, x3 = self.gather_ici, self.gather_d2d, self.exchange, self.scatter, self.share
        mix_in, in1, out1, in2, out2 = ("w_mix_in",), ("ffn1_w_in",), ("ffn1_w_out",), ("ffn2_w_in",), ("ffn2_w_out",)
        self.schedule = {
            "ffn1_in_fwd": [ici(mix_in)], "ffn1_out_fwd": [d2d(mix_in), ici(out2)],
            "mix_in_g": [ici(GROUP_PROJ), d2d(out2)],
            "attn_fwd": [ici(in2), d2d(GROUP_PROJ)], "gla_fwd": [d2d(in2)],
            "ffn2_dw_in": [x1(out2)], "ffn2_du": [x2(out2), x1(in2)],
            "attn_bwd": [x2(in2), x3(out2)], "gla_bwd": [x3(in2), x1(GROUP_PROJ)],
            "mix_du_g": [x2(GROUP_PROJ)],
            "ffn1_out_bwd": [x1(mix_in), x3(GROUP_PROJ)], "ffn1_dw_in": [x2(mix_in)], "ffn1_dw_out": [x3(mix_in), x1(in1)],
            "ffn1_du": [x2(in1), x1(out1)], "rel_bias_grad": [x2(out1), x3(in1)],
        }

    def weight(self, k):
        return self.full[k]

    def grad(self, k, g):
        r, cc = self.shapes[k]
        if k not in STACKED:
            g = g.reshape(r, N_CHIPS, cc).transpose(1, 0, 2) if BIG[k] else g.reshape(N_CHIPS, r, cc)
        self.local[k] = g

    def memo(self, key, make):
        if key not in self.memos:
            self.memos[key] = make()
        return self.memos[key]

    def host(self, name, call):
        builders = self.schedule.get(name)
        if not builders:
            return call(None)
        built = [b() for b in builders]
        main, comm = call([st for st, _ in built])
        for (_, post), res in zip(built, comm):
            post(res)
        return main

    def run(self, name, builders):
        built = [b() for b in builders]
        for (_, post), res in zip(built, _run_stages(name, [st for st, _ in built])):
            post(res)

    def set_gathered(self, names, gathered):
        for k, g in zip(names, gathered):
            _, r, cc = g.shape
            if k not in STACKED:
                g = g.transpose(1, 0, 2).reshape(r, N_CHIPS * cc) if BIG[k] else g.reshape(N_CHIPS * r, cc)
            self.full[k] = g

    @staticmethod
    def cols(names):
        return [k in TRANSPOSED for k in names]

    def gather_ici(self, names):
        def post(res):
            self.partial.update(zip(names, res))
        return lambda: (_stage_gather_ici([self.slots[k] for k in names], self.cols(names)), post)

    def gather_d2d(self, names):
        return lambda: (_stage_gather_d2d([self.partial[k] for k in names], self.cols(names)),
                        lambda res: self.set_gathered(names, res))

    def exchange(self, names):
        def post(res):
            for k, r in zip(names, res):
                self.pair[k] = _pair_sum(f"pair_sum_{k}", self.local[k], r, self.core1, k in TRANSPOSED)
        return lambda: (_stage_exchange_halves([self.local[k] for k in names], self.cols(names)), post)

    def scatter(self, names):
        def post(res):
            for k, landed in zip(names, res):
                self.half[k] = _quad_sum(f"quad_sum_{k}", self.pair[k], landed, self.chip_core, k in TRANSPOSED)
        return lambda: (_stage_scatter([self.pair[k] for k in names]), post)

    def share(self, names):
        def post(res):
            self.final.update(zip(names, res))
        return lambda: (_stage_share([self.half[k] for k in names], self.cols(names)), post)


def _step(args):
    x_pos, y_pos, c_pos = _mesh_pos()
    chip = 2 * x_pos + y_pos
    dev = 4 * x_pos + 2 * y_pos + c_pos
    take = lambda name, k: args[name][0].T if k in TRANSPOSED else args[name][0]
    w = {k: take(k, k) for k in WEIGHTS}
    mom = {k: take("m_" + k, k) for k in WEIGHTS}
    vel = {k: take("v_" + k, k) for k in WEIGHTS}
    x = args["x"][0]
    target = args["loss_target"][0]
    s, d = x.shape
    kd = d // 4
    rel_sh = w["rel_bias"].shape[1]
    wa2_sh = w["w_alpha2"].shape[1]
    ada_sh = w["w_ada"].shape[1]

    n_rel, n_wa2 = A_HEADS * rel_sh, GATE_RANK * wa2_sh
    packed = _pad_rows(jnp.concatenate([args["c"].reshape(-1), w["rel_bias"].reshape(-1), w["w_alpha2"].reshape(-1)]))
    got = _allgather_rows("gather_small_inputs", packed).reshape(N_DEV, -1)
    c_all = got[:, :d]
    per_chip = got[0::2]
    rel_bias = per_chip[:, d:d + n_rel].reshape(N_CHIPS, A_HEADS, rel_sh).transpose(1, 0, 2).reshape(A_HEADS, -1)
    w_alpha2 = per_chip[:, d + n_rel:d + n_rel + n_wa2].reshape(N_CHIPS, GATE_RANK, wa2_sh).transpose(1, 0, 2)
    w_alpha2 = w_alpha2.reshape(GATE_RANK, -1)

    b_shard = lax.dynamic_slice(w["b_ada"], (chip * ada_sh,), (ada_sh,))
    mod_shard = _mm("ada_fwd", "nn", c_all, w["w_ada"], (N_DEV, ada_sh, d), tm=N_DEV, tn=_tile(ada_sh, (512, 128)),
                    tk=d, precision=HIGHEST, a_fn=_silu, add=jnp.broadcast_to(b_shard[None], (N_DEV, ada_sh)))
    mod_all = _allgather_rows("gather_mod", mod_shard).reshape(N_DEV, N_DEV, ada_sh)[0::2]
    mod_all = mod_all.transpose(1, 0, 2).reshape(N_DEV, N_MOD * d)
    mod = lax.dynamic_index_in_dim(mod_all, dev, 0, keepdims=False).reshape(N_MOD, d)

    names = list(BIG)
    plan = _MeshPlan({k: w[k].astype(BF16) for k in names}, chip, c_pos)
    plan.set_gathered(GROUP_FFN1, _allgather_weights([plan.slots[k] for k in GROUP_FFN1]))

    small = dict(rel_bias=rel_bias, w_alpha2=w_alpha2, b_alpha=w["b_alpha"][None], gla_norm_g=w["gla_norm_g"][None])
    for k in ("ln1_g", "ln1_b", "ln2_g", "ln2_b", "ln3_g", "ln3_b"):
        small[k] = w[k][None]
    loss_local, grad_x, small_grads, dmod = _device_step(x, target, mod, small, plan)
    loss = lax.psum(loss_local, ("x", "y", "c"))
    plan.run("grad_tail_share", [plan.share(GROUP_FFN1[1:])])

    flat = jnp.concatenate([small_grads[k].reshape(-1) for k in SMALL] + [dmod.reshape(-1)])
    n_small = flat.shape[0] - N_MOD * d
    packed = _pad_rows(flat)
    all_small = _allgather_rows("gather_small_grads", packed).reshape(N_DEV, SUBLANES, -1)
    summed = _device_sum("small_grad_sum", all_small).reshape(-1)
    dmod_all = all_small.reshape(N_DEV, -1)[:, n_small:n_small + N_MOD * d]
    dmod_shard = lax.dynamic_slice(dmod_all, (0, chip * ada_sh), (N_DEV, ada_sh))
    grads = {"b_ada": summed[n_small:n_small + N_MOD * d]}
    off = 0
    for k in SMALL:
        size = small_grads[k].size
        grads[k] = summed[off:off + size].reshape(small_grads[k].shape)
        off += size
    grads["rel_bias"] = lax.dynamic_slice(grads["rel_bias"], (0, chip * rel_sh), (A_HEADS, rel_sh))
    grads["w_alpha2"] = lax.dynamic_slice(grads["w_alpha2"], (0, chip * wa2_sh), (GATE_RANK, wa2_sh))
    grads["w_ada"] = _mm("ada_bwd", "nn", jnp.pad(c_all.T, ((0, 0), (0, LANES - N_DEV))),
                         jnp.pad(dmod_shard, ((0, LANES - N_DEV), (0, 0))), (d, ada_sh, LANES), tm=_tile(d, (1024,)),
                         tn=_tile(ada_sh, (512, 128)), tk=LANES, precision=HIGHEST, a_fn=_silu)

    grads.update(plan.final)

    delta, new_m, new_v = {}, {}, {}
    for k in ["w_ada"] + names:
        delta[k], new_m[k], new_v[k] = _adamw(f"adamw_{k}", w[k], grads[k], mom[k], vel[k])
    tiny = ["b_ada"] + SMALL
    pack = lambda src: _pad_rows(jnp.concatenate([src[k].reshape(-1) for k in tiny]), rows=1).reshape(-1, LANES)
    outs = _adamw("adamw_small", pack(w), pack(grads), pack(mom), pack(vel))
    off = 0
    for k in tiny:
        size = w[k].size
        for dst, src in zip((delta, new_m, new_v), outs):
            dst[k] = src.reshape(-1)[off:off + size].reshape(w[k].shape)
        off += size

    give = lambda src: [src[k].T[None] if k in TRANSPOSED else src[k][None] for k in WEIGHTS]
    return (loss, grad_x[None], *give(grads), *give(delta), *give(new_m), *give(new_v))


def kernel(x, c, w_ada, b_ada, ffn1_w_in, ffn1_w_out, ln1_g, ln1_b, w_mix_in, rel_bias, w_alpha2, b_alpha, gla_norm_g, w_proj_a, w_proj_b, w_mix_out, ln2_g, ln2_b, ffn2_w_in, ffn2_w_out, ln3_g, ln3_b, loss_target, m_w_ada, m_b_ada, m_ffn1_w_in, m_ffn1_w_out, m_ln1_g, m_ln1_b, m_w_mix_in, m_rel_bias, m_w_alpha2, m_b_alpha, m_gla_norm_g, m_w_proj_a, m_w_proj_b, m_w_mix_out, m_ln2_g, m_ln2_b, m_ffn2_w_in, m_ffn2_w_out, m_ln3_g, m_ln3_b, v_w_ada, v_b_ada, v_ffn1_w_in, v_ffn1_w_out, v_ln1_g, v_ln1_b, v_w_mix_in, v_rel_bias, v_w_alpha2, v_b_alpha, v_gla_norm_g, v_w_proj_a, v_w_proj_b, v_w_mix_out, v_ln2_g, v_ln2_b, v_ffn2_w_in, v_ffn2_w_out, v_ln3_g, v_ln3_b):
    return _step(dict(locals()))
```

```python
import functools

import jax
import jax.numpy as jnp
from jax import lax
from jax.experimental import pallas as pl
from jax.experimental.pallas import tpu as pltpu

F32 = jnp.float32
BF16 = jnp.bfloat16
MESH = pl.DeviceIdType.MESH
HIGHEST = lax.Precision.HIGHEST

VMEM_LIMIT_BYTES = 56 * 1024 * 1024
LANES = 128
SUBLANES = 8

CHUNK = 64
A_HEADS = 16
A_HEAD_DIM = 64
A_PAST_CHUNKS = 8
A_BAND = (A_PAST_CHUNKS + 1) * CHUNK
A_PAD = A_PAST_CHUNKS * CHUNK
REL_CLIP = 256
REL_SIZE = REL_CLIP + CHUNK
B_HEADS = 4
GATE_RANK = 16
GATE_TAU = 16.0
N_MOD = 9
DEPTH = 1
ALPHA = (2.0 * DEPTH) ** 0.25
LN_EPS = 1e-5
RMS_EPS = 1e-6
ADAM_LR = 0.001
ADAM_B1 = 0.9
ADAM_B2 = 0.999
ADAM_EPS = 1e-08
ADAM_WD = 0.01
ADAM_STEP = 10
NEG_BIG = -1e30

N_CHIPS = 4
N_DEV = 8


def _cp(*sem):
    return pltpu.CompilerParams(dimension_semantics=sem, vmem_limit_bytes=VMEM_LIMIT_BYTES)


class _Stage:
    def __init__(self, arrays, out_shapes, n_sems, start, finish, aliases=None, relay=None):
        self.arrays, self.out_shapes, self.n_sems = list(arrays), list(out_shapes), n_sems
        self.start, self.finish, self.relay, self.aliases = start, finish, relay, dict(aliases or {})


def _pcall(body, stages, *, name, out_shape, in_specs, out_specs, grid=(), scratch_shapes=(), compiler_params=None):
    single = not isinstance(out_shape, (list, tuple))
    outs = [out_shape] if single else list(out_shape)
    ospecs = [out_specs] if single else list(out_specs)
    in_specs, scratch_shapes = list(in_specs), list(scratch_shapes)
    n_in, n_out, n_sc = len(in_specs), len(outs), len(scratch_shapes)
    stages = list(stages or [])
    c_in = [a for st in stages for a in st.arrays]
    c_out = [o for st in stages for o in st.out_shapes]
    aliases = {}
    io, oo = n_in, n_out
    for st in stages:
        for a, b in st.aliases.items():
            aliases[io + a] = oo + b
        io += len(st.arrays)
        oo += len(st.out_shapes)

    def wrapped(*refs):
        ins = refs[:n_in]
        cins = refs[n_in:n_in + len(c_in)]
        base = n_in + len(c_in)
        mouts = refs[base:base + n_out]
        couts = refs[base + n_out:base + n_out + len(c_out)]
        base += n_out + len(c_out)
        scr = refs[base:base + n_sc]
        sems = refs[base + n_sc:]

        def each(phase):
            i = o = 0
            for k, st in enumerate(stages):
                fn = (st.start, st.relay, st.finish)[phase]
                if fn is not None:
                    fn(cins[i:i + len(st.arrays)], couts[o:o + len(st.out_shapes)], sems[2 * k], sems[2 * k + 1])
                i += len(st.arrays)
                o += len(st.out_shapes)

        if stages and grid:
            step = functools.reduce(lambda acc, a: acc * grid[a] + pl.program_id(a), range(len(grid)), 0)
            steps = functools.reduce(lambda a, b: a * b, grid)
            pl.when(step == 0)(lambda: each(0))
            if any(st.relay for st in stages):
                pl.when(step == (2 * steps) // 3)(lambda: each(1))
            if body is not None:
                body(*ins, *mouts, *scr)
            pl.when(step == steps - 1)(lambda: each(2))
        else:
            each(0)
            each(1)
            if body is not None:
                body(*ins, *mouts, *scr)
            each(2)

    sem_shapes = []
    for st in stages:
        sem_shapes += [pltpu.SemaphoreType.DMA((st.n_sems,)), pltpu.SemaphoreType.DMA((st.n_sems,))]
    kwargs = dict(grid=grid) if grid else {}
    if compiler_params is not None:
        kwargs["compiler_params"] = compiler_params

    def run(*operands):
        res = pl.pallas_call(
            wrapped, name=name, out_shape=outs + c_out, in_specs=in_specs + [HBM_SPEC] * len(c_in),
            out_specs=ospecs + [HBM_SPEC] * len(c_out), scratch_shapes=scratch_shapes + sem_shapes,
            input_output_aliases=aliases, **kwargs)(*operands, *c_in)
        main = res[0] if single else tuple(res[:n_out])
        if not stages:
            return main
        comm, o = [], n_out
        for st in stages:
            comm.append(list(res[o:o + len(st.out_shapes)]))
            o += len(st.out_shapes)
        return main, comm

    return run


LONG_K = (2048, 1024)


def _tile(n, prefs):
    for t in prefs:
        if t <= n and n % t == 0:
            return t
    return n


_DIMS = {"nn": (((1,), (0,)), ((), ())), "nt": (((1,), (1,)), ((), ())), "tn": (((0,), (0,)), ((), ()))}


def _dot(a, b, mode="nn", precision=None):
    return lax.dot_general(a, b, _DIMS[mode], precision=precision, preferred_element_type=F32)


def _sigmoid(x):
    return 0.5 * jnp.tanh(0.5 * x) + 0.5


EPILOGUE_STRIP = 256


def _strips(n, width=EPILOGUE_STRIP):
    width = width if n % width == 0 else n
    return [slice(j, j + width) for j in range(0, n, width)]


def _mm(name, mode, a, b, mnk, *, tm, tn, tk, out_dtype=F32, precision=None, a_spec=None, b_spec=None,
        out_shape=None, o_spec=None, add=None, a_fn=None, thin=None, stages=None):
    m, n, k = mnk
    assert m % tm == 0 and n % tn == 0 and k % tk == 0, (name, mnk, tm, tn, tk)
    nk = k // tk
    if a_spec is None:
        a_spec = {"nn": pl.BlockSpec((tm, tk), lambda i, j, kk: (i, kk)),
                  "nt": pl.BlockSpec((tm, tk), lambda i, j, kk: (i, kk)),
                  "tn": pl.BlockSpec((tk, tm), lambda i, j, kk: (kk, i))}[mode]
    if b_spec is None:
        b_spec = {"nn": pl.BlockSpec((tk, tn), lambda i, j, kk: (kk, j)),
                  "nt": pl.BlockSpec((tn, tk), lambda i, j, kk: (j, kk)),
                  "tn": pl.BlockSpec((tk, tn), lambda i, j, kk: (kk, j))}[mode]
    if o_spec is None:
        o_spec = pl.BlockSpec((tm, tn), lambda i, j, kk: (i, j))
    if out_shape is None:
        out_shape = (m, n)
    has_add = add is not None
    n_in = 2 + has_add + (2 if thin else 0)

    def body(*refs):
        a_ref, b_ref = refs[0], refs[1]
        add_ref = refs[2] if has_add else None
        o_ref = refs[n_in]
        av = a_ref[...]
        if a_fn is not None:
            av = a_fn(av)
        part = _dot(av, b_ref[...], mode, precision)

        def finish(total):
            if has_add:
                total = total + add_ref[...]
            if thin:
                total = total + _dot(refs[n_in - 2][...], refs[n_in - 1][...])
            o_ref[...] = total.astype(out_dtype)

        if nk == 1:
            finish(part)
        else:
            acc_ref = refs[-1]
            kk = pl.program_id(2)

            @pl.when(kk == 0)
            def _():
                acc_ref[...] = part

            @pl.when(kk > 0)
            def _():
                acc_ref[...] += part

            @pl.when(kk == nk - 1)
            def _():
                finish(acc_ref[...])

    in_specs = [a_spec, b_spec]
    operands = [a, b]
    if has_add:
        in_specs.append(pl.BlockSpec((tm, tn), lambda i, j, kk: (i, j)))
        operands.append(add)
    if thin:
        k2 = thin[0].shape[1]
        in_specs += [pl.BlockSpec((tm, k2), lambda i, j, kk: (i, 0)), pl.BlockSpec((k2, tn), lambda i, j, kk: (0, j))]
        operands += list(thin)
    return _pcall(
        body, stages, name=name, out_shape=jax.ShapeDtypeStruct(out_shape, out_dtype), grid=(m // tm, n // tn, nk),
        in_specs=in_specs, out_specs=o_spec,
        scratch_shapes=[pltpu.VMEM((tm, tn), F32)] if nk > 1 else [],
        compiler_params=_cp("arbitrary", "arbitrary", "arbitrary") if stages else _cp("parallel", "parallel", "arbitrary"),
    )(*operands)


def _row_spec(tr, d):
    return pl.BlockSpec((tr, d), lambda i: (i, 0))


def _vec_spec(d, rows=1):
    return pl.BlockSpec((rows, d), lambda i: (0, 0))


def _col_spec(d, tr):
    return pl.BlockSpec((d, tr), lambda i: (0, i))


def _modulate(name, x, sh, sc):
    s, d = x.shape
    tr = _tile(s, (512, 256))

    def body(x_ref, sh_ref, sc_ref, o_ref, ot_ref):
        u = x_ref[...] * (1.0 + sc_ref[...]) + sh_ref[...]
        o_ref[...] = u.astype(BF16)
        ot_ref[...] = u.T.astype(BF16)

    return pl.pallas_call(
        body, name=name, out_shape=(jax.ShapeDtypeStruct((s, d), BF16), jax.ShapeDtypeStruct((d, s), BF16)),
        grid=(s // tr,), in_specs=[_row_spec(tr, d), _vec_spec(d), _vec_spec(d)],
        out_specs=(_row_spec(tr, d), _col_spec(d, tr)), compiler_params=_cp("parallel"),
    )(x, sh, sc)


def _ln_stats(r):
    mu = jnp.mean(r, axis=-1, keepdims=True)
    xc = r - mu
    var = jnp.mean(xc * xc, axis=-1, keepdims=True)
    rstd = lax.rsqrt(var + LN_EPS)
    return xc * rstd, rstd


def _resid_ln_fwd(name, x, f, gate, ln_g, ln_b, sh_n, sc_n, coef, transposed=False):
    s, d = x.shape
    tr = _tile(s, (256,))

    def body(x_ref, f_ref, gate_ref, g_ref, b_ref, sh_ref, sc_ref, h_ref, u_ref, *ut_ref):
        r = ALPHA * x_ref[...] + (coef * gate_ref[...]) * f_ref[...]
        xhat, _ = _ln_stats(r)
        h = xhat * g_ref[...] + b_ref[...]
        h_ref[...] = h
        u = h * (1.0 + sc_ref[...]) + sh_ref[...]
        u_ref[...] = u.astype(BF16)
        if transposed:
            ut_ref[0][...] = u.T.astype(BF16)

    extra_shape = (jax.ShapeDtypeStruct((d, s), BF16),) if transposed else ()
    extra_spec = (_col_spec(d, tr),) if transposed else ()
    return pl.pallas_call(
        body, name=name,
        out_shape=(jax.ShapeDtypeStruct((s, d), F32), jax.ShapeDtypeStruct((s, d), BF16)) + extra_shape,
        grid=(s // tr,), in_specs=[_row_spec(tr, d), _row_spec(tr, d)] + [_vec_spec(d)] * 5,
        out_specs=(_row_spec(tr, d), _row_spec(tr, d)) + extra_spec, compiler_params=_cp("parallel"),
    )(x, f, gate, ln_g, ln_b, sh_n, sc_n)


ROW_DSC, ROW_DSH, ROW_DLN_G, ROW_DLN_B, ROW_DGATE, ROW_LOSS = 0, 1, 2, 3, 4, 5


def _ln_bwd_core(dy, xhat, rstd, ln_g):
    dxhat = dy * ln_g
    m1 = jnp.mean(dxhat, axis=-1, keepdims=True)
    m2 = jnp.mean(dxhat * xhat, axis=-1, keepdims=True)
    return rstd * (dxhat - m1 - xhat * m2)


def _colsum(v):
    return jnp.sum(v, axis=0, keepdims=True)


def _final_ln_loss_bwd(name, x, f, target, gate, ln_g, ln_b, coef):
    s, d = x.shape
    tr = _tile(s, (256,))
    inv_d = 1.0 / d

    def body(x_ref, f_ref, t_ref, gate_ref, g_ref, b_ref, dr_ref, df_ref, acc_ref):
        @pl.when(pl.program_id(0) == 0)
        def _():
            acc_ref[...] = jnp.zeros_like(acc_ref)

        fv = f_ref[...]
        r = ALPHA * x_ref[...] + (coef * gate_ref[...]) * fv
        xhat, rstd = _ln_stats(r)
        h = xhat * g_ref[...] + b_ref[...]
        err = h - t_ref[...]
        dy = err * inv_d
        dr = _ln_bwd_core(dy, xhat, rstd, g_ref[...])
        dr_ref[...] = dr
        df_ref[...] = ((coef * gate_ref[...]) * dr).astype(BF16)
        acc_ref[ROW_DLN_G:ROW_DLN_G + 1, :] += _colsum(dy * xhat)
        acc_ref[ROW_DLN_B:ROW_DLN_B + 1, :] += _colsum(dy)
        acc_ref[ROW_DGATE:ROW_DGATE + 1, :] += _colsum((coef * dr) * fv)
        acc_ref[ROW_LOSS:ROW_LOSS + 1, :] += _colsum(err * err) * (0.5 * inv_d)

    return pl.pallas_call(
        body, name=name,
        out_shape=(jax.ShapeDtypeStruct((s, d), F32), jax.ShapeDtypeStruct((s, d), BF16),
                   jax.ShapeDtypeStruct((SUBLANES, d), F32)),
        grid=(s // tr,), in_specs=[_row_spec(tr, d)] * 3 + [_vec_spec(d)] * 3,
        out_specs=(_row_spec(tr, d), _row_spec(tr, d), _vec_spec(d, SUBLANES)),
        compiler_params=_cp("arbitrary"),
    )(x, f, target, gate, ln_g, ln_b)


def _resid_ln_bwd(name, du_n, dr_n, x, f, sc_n, gate, ln_g, ln_b, coef):
    s, d = x.shape
    tr = _tile(s, (256,))

    def body(du_ref, drn_ref, x_ref, f_ref, sc_ref, gate_ref, g_ref, b_ref, dr_ref, df_ref, acc_ref):
        @pl.when(pl.program_id(0) == 0)
        def _():
            acc_ref[...] = jnp.zeros_like(acc_ref)

        fv = f_ref[...]
        du = du_ref[...]
        r = ALPHA * x_ref[...] + (coef * gate_ref[...]) * fv
        xhat, rstd = _ln_stats(r)
        h = xhat * g_ref[...] + b_ref[...]
        dy = du * (1.0 + sc_ref[...]) + ALPHA * drn_ref[...]
        dr = _ln_bwd_core(dy, xhat, rstd, g_ref[...])
        dr_ref[...] = dr
        df_ref[...] = ((coef * gate_ref[...]) * dr).astype(BF16)
        acc_ref[ROW_DSC:ROW_DSC + 1, :] += _colsum(du * h)
        acc_ref[ROW_DSH:ROW_DSH + 1, :] += _colsum(du)
        acc_ref[ROW_DLN_G:ROW_DLN_G + 1, :] += _colsum(dy * xhat)
        acc_ref[ROW_DLN_B:ROW_DLN_B + 1, :] += _colsum(dy)
        acc_ref[ROW_DGATE:ROW_DGATE + 1, :] += _colsum((coef * dr) * fv)

    return pl.pallas_call(
        body, name=name,
        out_shape=(jax.ShapeDtypeStruct((s, d), F32), jax.ShapeDtypeStruct((s, d), BF16),
                   jax.ShapeDtypeStruct((SUBLANES, d), F32)),
        grid=(s // tr,), in_specs=[_row_spec(tr, d)] * 4 + [_vec_spec(d)] * 4,
        out_specs=(_row_spec(tr, d), _row_spec(tr, d), _vec_spec(d, SUBLANES)),
        compiler_params=_cp("arbitrary"),
    )(du_n, dr_n, x, f, sc_n, gate, ln_g, ln_b)


def _input_grad(name, du, dr, x, sc):
    s, d = x.shape
    tr = _tile(s, (256,))

    def body(du_ref, dr_ref, x_ref, sc_ref, gx_ref, acc_ref):
        @pl.when(pl.program_id(0) == 0)
        def _():
            acc_ref[...] = jnp.zeros_like(acc_ref)

        du = du_ref[...]
        gx_ref[...] = du * (1.0 + sc_ref[...]) + ALPHA * dr_ref[...]
        acc_ref[ROW_DSC:ROW_DSC + 1, :] += _colsum(du * x_ref[...])
        acc_ref[ROW_DSH:ROW_DSH + 1, :] += _colsum(du)

    return pl.pallas_call(
        body, name=name,
        out_shape=(jax.ShapeDtypeStruct((s, d), F32), jax.ShapeDtypeStruct((SUBLANES, d), F32)),
        grid=(s // tr,), in_specs=[_row_spec(tr, d)] * 3 + [_vec_spec(d)],
        out_specs=(_row_spec(tr, d), _vec_spec(d, SUBLANES)), compiler_params=_cp("arbitrary"),
    )(du, dr, x, sc)


def _ffn_in_fwd(name, u, w_in, stages=None):
    s, d = u.shape
    cs = w_in.shape[2]
    f = 2 * cs
    tm, tn = _tile(s, (2048, 1024, 512)), _tile(cs, (256, 128))
    nb = f // tn
    nbs = cs // tn

    def body(u_ref, wa_ref, wb_ref, ab_ref, act_ref):
        for rows in _strips(tm, 512):
            uv = u_ref[rows, :]
            a = _dot(uv, wa_ref[...])
            b = _dot(uv, wb_ref[...])
            sg = _sigmoid(a)
            silu = a * sg
            ab_ref[0, rows, :] = (b * (sg + silu * (1.0 - sg))).astype(BF16)
            ab_ref[1, rows, :] = silu.astype(BF16)
            act_ref[rows, :] = (silu * b).astype(BF16)

    return _pcall(
        body, stages, name=name,
        out_shape=(jax.ShapeDtypeStruct((2, s, f), BF16), jax.ShapeDtypeStruct((s, f), BF16)),
        grid=(s // tm, nb),
        in_specs=[pl.BlockSpec((tm, d), lambda i, j: (i, 0)),
                  pl.BlockSpec((None, d, tn), lambda i, j: (j // nbs, 0, j % nbs)),
                  pl.BlockSpec((None, d, tn), lambda i, j: (2 + j // nbs, 0, j % nbs))],
        out_specs=(pl.BlockSpec((2, tm, tn), lambda i, j: (0, i, j)), pl.BlockSpec((tm, tn), lambda i, j: (i, j))),
        compiler_params=_cp("arbitrary", "arbitrary"),
    )(u, w_in, w_in)


def _ffn_out_bwd(name, df, w_out, ab, stages=None):
    s, d = df.shape
    f = w_out.shape[0]
    tm, tn = _tile(s, (1024, 512)), _tile(f, (512, 256, 128))

    def body(df_ref, w_ref, ab_ref, dab_ref):
        dfv = df_ref[...]
        for cols in _strips(tn):
            dact = _dot(dfv, w_ref[cols, :], "nt")
            dab_ref[0, :, cols] = (dact * ab_ref[0, :, cols].astype(F32)).astype(BF16)
            dab_ref[1, :, cols] = (dact * ab_ref[1, :, cols].astype(F32)).astype(BF16)

    return _pcall(
        body, stages, name=name, out_shape=jax.ShapeDtypeStruct((2, s, f), BF16), grid=(s // tm, f // tn),
        in_specs=[pl.BlockSpec((tm, d), lambda i, j: (i, 0)), pl.BlockSpec((tn, d), lambda i, j: (j, 0)),
                  pl.BlockSpec((2, tm, tn), lambda i, j: (0, i, j))],
        out_specs=pl.BlockSpec((2, tm, tn), lambda i, j: (0, i, j)),
        compiler_params=_cp("arbitrary", "arbitrary"),
    )(df, w_out, ab)


def _ffn_forward(tag, u, plan):
    w_in, w_out = plan.weight(f"{tag}_w_in"), plan.weight(f"{tag}_w_out")
    s, d = u.shape
    f = w_out.shape[0]
    ab, act = plan.host(f"{tag}_in_fwd", lambda st: _ffn_in_fwd(f"{tag}_in_fwd", u, w_in, st))
    out = plan.host(f"{tag}_out_fwd", lambda st: _mm(
        f"{tag}_out_fwd", "nn", act, w_out, (s, d, f), tm=_tile(s, (1024,)), tn=_tile(d, (1024,)),
        tk=_tile(f, (2816, 1408, 512, 128)), stages=st))
    return out, (ab, act)


def _ffn_backward(tag, df, ut, saved, plan, in_first):
    w_in, w_out = plan.weight(f"{tag}_w_in"), plan.weight(f"{tag}_w_out")
    ab, act = saved
    d, s = ut.shape
    f = w_out.shape[0]
    dab = plan.host(f"{tag}_out_bwd", lambda st: _ffn_out_bwd(f"{tag}_out_bwd", df, w_out, ab, st))
    cs = w_in.shape[2]
    tk = _tile(cs, (2816, 1408, 256, 128))
    nkh, nks = f // tk, cs // tk
    tmd = _tile(d, (1024,))
    tks = _tile(s, LONG_K)

    def dw_in():
        tw = _tile(cs, (256, 128))
        nwh, nws = f // tw, cs // tw
        plan.grad(f"{tag}_w_in", plan.host(f"{tag}_dw_in", lambda st: _mm(
            f"{tag}_dw_in", "nn", ut, dab, (d, 2 * f, s), tm=tmd, tn=tw, tk=s,
            b_spec=pl.BlockSpec((None, s, tw), lambda i, j, kk: (j // nwh, 0, j % nwh)), out_shape=(N_CHIPS, d, cs),
            o_spec=pl.BlockSpec((None, tmd, tw), lambda i, j, kk: (j // nws, i, j % nws)), stages=st)))

    def dw_out():
        plan.grad(f"{tag}_w_out", plan.host(f"{tag}_dw_out", lambda st: _mm(
            f"{tag}_dw_out", "tn", act, df, (f, d, s), tm=_tile(f, (1408, 512, 128)), tn=tmd, tk=tks, stages=st)))

    for step in ((dw_in, dw_out) if in_first else (dw_out, dw_in)):
        step()
    return plan.host(f"{tag}_du", lambda st: _mm(
        f"{tag}_du", "nt", dab, w_in, (s, d, 2 * f), tm=_tile(s, (1024,)), tn=tmd, tk=tk,
        a_spec=pl.BlockSpec((None, _tile(s, (1024,)), tk), lambda i, j, kk: (kk // nkh, i, kk % nkh)),
        b_spec=pl.BlockSpec((None, tmd, tk), lambda i, j, kk: (kk // nks, j, kk % nks)), stages=st))


ATTN_Q = 4 * CHUNK
ATTN_W = ATTN_Q + A_PAD


def _band_bias(bias):
    n = ATTN_Q // CHUNK
    rows = [jnp.pad(bias, ((0, 0), (0, 0), (i * CHUNK, (n - 1 - i) * CHUNK)), constant_values=NEG_BIG)
            for i in range(n)]
    return jnp.concatenate(rows, axis=1)


def _band_bias_grad(dband):
    n = ATTN_Q // CHUNK
    parts = [dband[:, i * CHUNK:(i + 1) * CHUNK, i * CHUNK:i * CHUNK + A_BAND] for i in range(n)]
    return functools.reduce(jnp.add, parts)


def _attn_probs(q, kw, bias, key0):
    sc = _dot(q, kw, "nt") * (A_HEAD_DIM ** -0.5) + bias
    ks = lax.broadcasted_iota(jnp.int32, sc.shape, 1)
    sc = jnp.where(key0 + ks >= 0, sc, NEG_BIG)
    p = jnp.exp(sc - jnp.max(sc, axis=-1, keepdims=True))
    return p * (1.0 / jnp.sum(p, axis=-1, keepdims=True))


def _head_masks():
    lane = lax.broadcasted_iota(jnp.int32, (1, LANES), 1)
    return [lane // A_HEAD_DIM == h for h in range(LANES // A_HEAD_DIM)]


def _attn_fwd(p1, kvp, band, stages=None):
    s = p1.shape[0]
    aw = A_HEADS * A_HEAD_DIM
    nblk = aw // LANES
    hpb = LANES // A_HEAD_DIM
    assert s % ATTN_Q == 0

    def body(q_ref, k_ref, v_ref, b_ref, o_ref):
        base = pl.multiple_of(pl.program_id(1) * ATTN_Q, ATTN_Q)
        qv = q_ref[...]
        kw = k_ref[pl.ds(base, ATTN_W), :]
        vw = v_ref[pl.ds(base, ATTN_W), :]
        out = jnp.zeros((ATTN_Q, LANES), F32)
        for h, mask in enumerate(_head_masks()):
            p = _attn_probs(jnp.where(mask, qv, jnp.zeros_like(qv)), kw, b_ref[h], base - A_PAD)
            out = jnp.where(mask, _dot(p.astype(BF16), vw), out)
        o_ref[...] = out.astype(BF16)

    kv_rows = s + A_PAD
    return _pcall(
        body, stages, name="attn_fwd", out_shape=jax.ShapeDtypeStruct((s, aw), BF16), grid=(nblk, s // ATTN_Q),
        in_specs=[pl.BlockSpec((ATTN_Q, LANES), lambda b, i: (i, b)),
                  pl.BlockSpec((kv_rows, LANES), lambda b, i: (0, b)),
                  pl.BlockSpec((kv_rows, LANES), lambda b, i: (0, nblk + b)),
                  pl.BlockSpec((hpb, ATTN_Q, ATTN_W), lambda b, i: (b, 0, 0))],
        out_specs=pl.BlockSpec((ATTN_Q, LANES), lambda b, i: (i, b)),
        compiler_params=_cp("arbitrary", "arbitrary"),
    )(p1, kvp, kvp, band)


def _attn_bwd(p1, kvp, band, dya, stages=None):
    s = p1.shape[0]
    aw = A_HEADS * A_HEAD_DIM
    nblk = aw // LANES
    hpb = LANES // A_HEAD_DIM
    scale = A_HEAD_DIM ** -0.5

    def body(q_ref, k_ref, v_ref, b_ref, do_ref, dq_ref, dk_ref, dv_ref, db_ref):
        @pl.when(pl.program_id(1) == 0)
        def _():
            dk_ref[...] = jnp.zeros_like(dk_ref)
            dv_ref[...] = jnp.zeros_like(dv_ref)
            db_ref[...] = jnp.zeros_like(db_ref)

        base = pl.multiple_of(pl.program_id(1) * ATTN_Q, ATTN_Q)
        window = pl.ds(base, ATTN_W)
        kw = k_ref[window, :]
        vw = v_ref[window, :]
        qv = q_ref[...]
        dov = do_ref[...]
        dq = jnp.zeros((ATTN_Q, LANES), F32)
        dk = jnp.zeros((ATTN_W, LANES), F32)
        dv = jnp.zeros((ATTN_W, LANES), F32)
        for h, mask in enumerate(_head_masks()):
            qh = jnp.where(mask, qv, jnp.zeros_like(qv))
            doh = jnp.where(mask, dov, jnp.zeros_like(dov))
            p = _attn_probs(qh, kw, b_ref[h], base - A_PAD)
            dp = _dot(doh, vw, "nt")
            ds = p * (dp - jnp.sum(p * dp, axis=-1, keepdims=True))
            db_ref[h] += ds
            dsb = (ds * scale).astype(BF16)
            dq = jnp.where(mask, _dot(dsb, kw), dq)
            dk = dk + _dot(dsb, qh, "tn")
            dv = dv + _dot(p.astype(BF16), doh, "tn")
        dq_ref[...] = dq.astype(BF16)
        dk_ref[window, :] += dk
        dv_ref[window, :] += dv

    kv_rows = s + A_PAD
    q_spec = pl.BlockSpec((ATTN_Q, LANES), lambda b, i: (i, b))
    acc_spec = pl.BlockSpec((kv_rows, LANES), lambda b, i: (0, b))
    b_spec = pl.BlockSpec((hpb, ATTN_Q, ATTN_W), lambda b, i: (b, 0, 0))
    return _pcall(
        body, stages, name="attn_bwd",
        out_shape=(jax.ShapeDtypeStruct((s, aw), BF16), jax.ShapeDtypeStruct((kv_rows, aw), F32),
                   jax.ShapeDtypeStruct((kv_rows, aw), F32), jax.ShapeDtypeStruct((A_HEADS, ATTN_Q, ATTN_W), F32)),
        grid=(nblk, s // ATTN_Q),
        in_specs=[q_spec, acc_spec, pl.BlockSpec((kv_rows, LANES), lambda b, i: (0, nblk + b)), b_spec, q_spec],
        out_specs=(q_spec, acc_spec, acc_spec, b_spec), compiler_params=_cp("arbitrary", "arbitrary"),
    )(p1, kvp, kvp, band, dya)


REL_TILE = CHUNK * A_BAND // 8


def _rel_onehot():
    qi = jnp.arange(CHUNK)[:, None]
    ks = jnp.arange(A_BAND)[None, :]
    idx = (jnp.clip(ks - A_PAD - qi, -REL_CLIP, CHUNK - 1) + REL_CLIP).reshape(1, CHUNK * A_BAND)
    return (jnp.arange(REL_SIZE)[:, None] == idx).astype(F32)


def _gla_gate(lr, wa2, balpha):
    z = _dot(lr, wa2) + balpha
    la = (jnp.minimum(z, 0.0) - jnp.log(1.0 + jnp.exp(-jnp.abs(z)))) * (1.0 / GATE_TAU)
    row = lax.broadcasted_iota(jnp.int32, (CHUNK, CHUNK), 0)
    col = lax.broadcasted_iota(jnp.int32, (CHUNK, CHUNK), 1)
    cum = _dot((row >= col).astype(F32), la, precision=HIGHEST)
    return z, la, cum


def _gla_dims(p2):
    kd = p2.shape[1] // 6
    hk = kd // B_HEADS
    hv = 2 * hk
    return kd, hk, hv


GLA_CPS = 8


def _gla_fwd(p2, lrp, wa2p, balpha, gnorm, stages=None):
    s = p2.shape[0]
    kd, hk, hv = _gla_dims(p2)
    nc = s // CHUNK
    cps = GLA_CPS if nc % GLA_CPS == 0 else 1
    rows_per = cps * CHUNK
    qscale = hk ** -0.5

    def body(p_ref, lr_ref, wa_ref, ba_ref, gn_ref, yb_ref, st_ref, state):
        @pl.when(pl.program_id(0) == 0)
        def _():
            state[...] = jnp.zeros_like(state)

        gn = gn_ref[...]
        for sub in range(cps):
            rows = slice(sub * CHUNK, (sub + 1) * CHUNK)
            _, _, cum = _gla_gate(lr_ref[rows, :], wa_ref[...], ba_ref[...])
            last = cum[CHUNK - 1:CHUNK, :]
            e = jnp.exp(last - cum)
            dch = jnp.exp(last)
            for hh in range(B_HEADS):
                ks = slice(hh * hk, (hh + 1) * hk)
                q = p_ref[rows, hh * hk:(hh + 1) * hk].astype(F32)
                k = p_ref[rows, kd + hh * hk:kd + (hh + 1) * hk].astype(F32)
                v = p_ref[rows, 2 * kd + hh * hv:2 * kd + (hh + 1) * hv]
                rg = p_ref[rows, 4 * kd + hh * hv:4 * kd + (hh + 1) * hv].astype(F32)
                kdec = (k * e[:, ks]).astype(BF16)
                st = state[hh] * dch[:, ks] + _dot(v, kdec, "tn")
                state[hh] = st
                st_ref[sub, hh] = st
                o = _dot((q * qscale).astype(BF16), st.astype(BF16), "nt")
                rinv = lax.rsqrt(jnp.mean(o * o, axis=-1, keepdims=True) + RMS_EPS)
                yb_ref[rows, hh * hv:(hh + 1) * hv] = ((o * rinv * gn) * (rg * _sigmoid(rg))).astype(BF16)

    return _pcall(
        body, stages, name="gla_fwd",
        out_shape=(jax.ShapeDtypeStruct((s, 2 * kd), BF16), jax.ShapeDtypeStruct((nc, B_HEADS, hv, hk), F32)),
        grid=(nc // cps,),
        in_specs=[pl.BlockSpec((rows_per, 6 * kd), lambda i: (i, 0)), pl.BlockSpec((rows_per, LANES), lambda i: (i, 0)),
                  pl.BlockSpec((LANES, kd), lambda i: (0, 0)), pl.BlockSpec((1, kd), lambda i: (0, 0)),
                  pl.BlockSpec((1, hv), lambda i: (0, 0))],
        out_specs=(pl.BlockSpec((rows_per, 2 * kd), lambda i: (i, 0)),
                   pl.BlockSpec((cps, B_HEADS, hv, hk), lambda i: (i, 0, 0, 0))),
        scratch_shapes=[pltpu.VMEM((B_HEADS, hv, hk), F32)], compiler_params=_cp("arbitrary"),
    )(p2, lrp, wa2p, balpha, gnorm)


GLA_ROW_DBALPHA, GLA_ROW_DGNORM = 0, 1


def _gla_bwd(p2, lrp, wa2p, balpha, gnorm, states, dyb, stages=None):
    s = p2.shape[0]
    kd, hk, hv = _gla_dims(p2)
    nc = s // CHUNK
    cps = GLA_CPS if nc % GLA_CPS == 0 else 1
    rows_per = cps * CHUNK
    nblk = nc // cps
    qscale = hk ** -0.5

    def body(p_ref, lr_ref, wa_ref, ba_ref, gn_ref, st_ref, sp_ref, dy_ref, dp_ref, dz_ref, sm_ref, gcar):
        i = pl.program_id(0)

        @pl.when(i == 0)
        def _():
            gcar[...] = jnp.zeros_like(gcar)
            sm_ref[...] = jnp.zeros_like(sm_ref)

        block_has_prev = (i < nblk - 1).astype(F32)
        gn = gn_ref[...]
        row = lax.broadcasted_iota(jnp.int32, (CHUNK, CHUNK), 0)
        col = lax.broadcasted_iota(jnp.int32, (CHUNK, CHUNK), 1)
        tri_strict = (row > col).astype(F32)
        for sub in reversed(range(cps)):
            rows = slice(sub * CHUNK, (sub + 1) * CHUNK)
            z, _, cum = _gla_gate(lr_ref[rows, :], wa_ref[...], ba_ref[...])
            last = cum[CHUNK - 1:CHUNK, :]
            e = jnp.exp(last - cum)
            dch = jnp.exp(last)
            sgn = _sigmoid(-z) * (1.0 / GATE_TAU)
            for hh in range(B_HEADS):
                ks = slice(hh * hk, (hh + 1) * hk)
                q = p_ref[rows, hh * hk:(hh + 1) * hk].astype(F32)
                k = p_ref[rows, kd + hh * hk:kd + (hh + 1) * hk].astype(F32)
                v = p_ref[rows, 2 * kd + hh * hv:2 * kd + (hh + 1) * hv]
                rg = p_ref[rows, 4 * kd + hh * hv:4 * kd + (hh + 1) * hv].astype(F32)
                kdecf = k * e[:, ks]
                kdec = kdecf.astype(BF16)
                st16 = st_ref[sub, hh].astype(BF16)
                prev = st_ref[sub - 1, hh] if sub > 0 else sp_ref[hh] * block_has_prev
                qs = (q * qscale).astype(BF16)
                o = _dot(qs, st16, "nt")
                rinv = lax.rsqrt(jnp.mean(o * o, axis=-1, keepdims=True) + RMS_EPS)
                dy = dy_ref[rows, hh * hv:(hh + 1) * hv].astype(F32)
                sg = _sigmoid(rg)
                onorm = o * rinv
                drg = dy * (onorm * gn) * (sg * (1.0 + rg * (1.0 - sg)))
                dob = dy * (rg * sg)
                sm_ref[GLA_ROW_DGNORM:GLA_ROW_DGNORM + 1, 0:hv] += _colsum(dob * onorm)
                t = dob * gn
                do = rinv * (t - onorm * jnp.mean(t * onorm, axis=-1, keepdims=True))
                do16 = do.astype(BF16)
                dq = _dot(do16, st16) * qscale
                gt = _dot(do16, qs, "tn") + gcar[hh]
                gcar[hh] = gt * dch[:, ks]
                dd = _colsum(gt * prev)
                gt16 = gt.astype(BF16)
                dkdec = _dot(v, gt16)
                dv = _dot(kdec, gt16, "nt")
                dla = dd * dch[:, ks] + _dot(tri_strict, dkdec * kdecf, precision=HIGHEST)
                dzh = dla * sgn[:, ks]
                sm_ref[GLA_ROW_DBALPHA:GLA_ROW_DBALPHA + 1, hh * hk:(hh + 1) * hk] += _colsum(dzh)
                dz_ref[rows, hh * hk:(hh + 1) * hk] = dzh.astype(BF16)
                dp_ref[rows, hh * hk:(hh + 1) * hk] = dq.astype(BF16)
                dp_ref[rows, kd + hh * hk:kd + (hh + 1) * hk] = (dkdec * e[:, ks]).astype(BF16)
                dp_ref[rows, 2 * kd + hh * hv:2 * kd + (hh + 1) * hv] = dv.astype(BF16)
                dp_ref[rows, 4 * kd + hh * hv:4 * kd + (hh + 1) * hv] = drg.astype(BF16)

    rev = lambda i: (nblk - 1 - i, 0)
    return _pcall(
        body, stages, name="gla_bwd",
        out_shape=(jax.ShapeDtypeStruct((s, 6 * kd), BF16), jax.ShapeDtypeStruct((s, kd), BF16),
                   jax.ShapeDtypeStruct((SUBLANES, kd), F32)),
        grid=(nblk,),
        in_specs=[pl.BlockSpec((rows_per, 6 * kd), rev), pl.BlockSpec((rows_per, LANES), rev),
                  pl.BlockSpec((LANES, kd), lambda i: (0, 0)), pl.BlockSpec((1, kd), lambda i: (0, 0)),
                  pl.BlockSpec((1, hv), lambda i: (0, 0)),
                  pl.BlockSpec((cps, B_HEADS, hv, hk), lambda i: (nblk - 1 - i, 0, 0, 0)),
                  pl.BlockSpec((None, B_HEADS, hv, hk), lambda i: (jnp.maximum((nblk - 1 - i) * cps - 1, 0), 0, 0, 0)),
                  pl.BlockSpec((rows_per, 2 * kd), rev)],
        out_specs=(pl.BlockSpec((rows_per, 6 * kd), rev), pl.BlockSpec((rows_per, kd), rev),
                   pl.BlockSpec((SUBLANES, kd), lambda i: (0, 0))),
        scratch_shapes=[pltpu.VMEM((B_HEADS, hv, hk), F32)], compiler_params=_cp("arbitrary"),
    )(p2, lrp, wa2p, balpha, gnorm, states, states, dyb)


def _merge_fwd(ya, yb, wpa, wpb, g):
    s, ka = ya.shape
    kb = yb.shape[1]
    d = wpa.shape[1]
    tm, tn = _tile(s, (1024, 512)), _tile(d, (512,))

    def body(ya_ref, yb_ref, wa_ref, wb_ref, g_ref, m_ref, pab_ref):
        yav, ybv = ya_ref[...], yb_ref[...]
        for cols in _strips(tn):
            pa = _dot(yav, wa_ref[:, cols])
            pb = _dot(ybv, wb_ref[:, cols])
            m_ref[:, cols] = (_sigmoid(g_ref[0, :, cols].astype(F32)) * pa
                              + _sigmoid(g_ref[1, :, cols].astype(F32)) * pb).astype(BF16)
            pab_ref[0, :, cols] = pa.astype(BF16)
            pab_ref[1, :, cols] = pb.astype(BF16)

    st = pl.BlockSpec((2, tm, tn), lambda i, j: (0, i, j))
    return pl.pallas_call(
        body, name="merge_fwd",
        out_shape=(jax.ShapeDtypeStruct((s, d), BF16), jax.ShapeDtypeStruct((2, s, d), BF16)),
        grid=(s // tm, d // tn),
        in_specs=[pl.BlockSpec((tm, ka), lambda i, j: (i, 0)), pl.BlockSpec((tm, kb), lambda i, j: (i, 0)),
                  pl.BlockSpec((ka, tn), lambda i, j: (0, j)), pl.BlockSpec((kb, tn), lambda i, j: (0, j)), st],
        out_specs=(pl.BlockSpec((tm, tn), lambda i, j: (i, j)), st),
        compiler_params=_cp("parallel", "parallel"),
    )(ya, yb, wpa, wpb, g)


def _merge_bwd(dm, wmo, g, pab, stages=None):
    s, d = dm.shape
    tm, tn = _tile(s, (1024, 512)), _tile(d, (512,))

    def body(dm_ref, w_ref, g_ref, pab_ref, dpab_ref, dg_ref):
        dmv = dm_ref[...]
        for cols in _strips(tn):
            dmg = _dot(dmv, w_ref[cols, :], "nt")
            for j in range(2):
                sg = _sigmoid(g_ref[j, :, cols].astype(F32))
                dpab_ref[j, :, cols] = (dmg * sg).astype(BF16)
                dg_ref[j, :, cols] = (dmg * pab_ref[j, :, cols].astype(F32) * (sg * (1.0 - sg))).astype(BF16)

    st = pl.BlockSpec((2, tm, tn), lambda i, j: (0, i, j))
    return _pcall(
        body, stages, name="merge_bwd",
        out_shape=(jax.ShapeDtypeStruct((2, s, d), BF16), jax.ShapeDtypeStruct((2, s, d), BF16)),
        grid=(s // tm, d // tn),
        in_specs=[pl.BlockSpec((tm, d), lambda i, j: (i, 0)), pl.BlockSpec((tn, d), lambda i, j: (j, 0)), st, st],
        out_specs=(st, st), compiler_params=_cp("arbitrary", "arbitrary"),
    )(dm, wmo, g, pab)


def _virtual_rows(parts, lo, hi):
    out, off = [], 0
    for p in parts:
        a, b = max(lo, off), min(hi, off + p.shape[0])
        if a < b:
            out.append(p[a - off:b - off])
        off += p.shape[0]
    return out[0] if len(out) == 1 else jnp.concatenate(out, axis=0)


def _mix_in_row_groups(d):
    o1 = 3 * A_HEADS * A_HEAD_DIM
    o2 = o1 + 6 * (d // 4)
    o3 = o2 + GATE_RANK
    return (0, o1), (o1, o2), (o2, o3), (o3, o3 + 2 * d)


def _split_mix_in(stacked):
    d = stacked.shape[2]
    flat = stacked.reshape(-1, d)
    _, _, (lo, hi), (glo, ghi) = _mix_in_row_groups(d)
    return flat, jnp.pad(flat[lo:hi], ((0, LANES - GATE_RANK), (0, 0))), flat[glo:ghi]


MIX_TILE = 1024


def _mix_in_weights(plan):
    return plan.memo("mix_in_weights", lambda: _split_mix_in(plan.weight("w_mix_in")))


def _hosted_mm(plan):
    return lambda name, *a, **k: plan.host(name, lambda st: _mm(name, *a, stages=st, **k))


def _mix_forward(u2, plan, small):
    s, d = u2.shape
    wt, wt_lr, wt_g = _mix_in_weights(plan)
    bias, wa2p, balpha, gnorm = small
    mm = _hosted_mm(plan)
    aw = A_HEADS * A_HEAD_DIM
    tm, tn = _tile(s, (2048, 1024)), MIX_TILE
    (_, na), (_, nab) = _mix_in_row_groups(d)[:2]
    assert na % tn == 0 and nab % tn == 0
    p1 = mm("mix_in_a", "nt", u2, wt, (s, na, d), tm=tm, tn=tn, tk=d, out_dtype=BF16)
    p2 = mm("mix_in_b", "nt", u2, wt, (s, nab - na, d), tm=tm, tn=tn, tk=d, out_dtype=BF16,
            b_spec=pl.BlockSpec((tn, d), lambda i, j, kk: (na // tn + j, 0)))
    lrp = mm("mix_in_lr", "nt", u2, wt_lr, (s, LANES, d), tm=tm, tn=LANES, tk=d, out_dtype=BF16)
    nbg = d // tn
    g = mm("mix_in_g", "nt", u2, wt_g, (s, 2 * d, d), tm=tm, tn=tn, tk=d, out_dtype=BF16, out_shape=(2, s, d),
           o_spec=pl.BlockSpec((None, tm, tn), lambda i, j, kk: (j // nbg, i, j % nbg)))
    kvp = jnp.pad(p1[:, aw:], ((A_PAD, 0), (0, 0)))
    ya = plan.host("attn_fwd", lambda st: _attn_fwd(p1, kvp, bias, st))
    yb, states = plan.host("gla_fwd", lambda st: _gla_fwd(p2, lrp, wa2p, balpha, gnorm, st))
    merged, pab = _merge_fwd(ya, yb, plan.weight("w_proj_a"), plan.weight("w_proj_b"), g)
    m = mm("mix_out", "nn", merged, plan.weight("w_mix_out"), (s, d, d), tm=tm, tn=tn, tk=d)
    return m, (p1, kvp, p2, lrp, states, ya, yb, g, pab, merged)


def _mix_backward(dm, u2, saved, plan, small):
    s, d = u2.shape
    wt, wt_lr, wt_g = _mix_in_weights(plan)
    wpa, wpb, wmo = plan.weight("w_proj_a"), plan.weight("w_proj_b"), plan.weight("w_mix_out")
    bias, wa2p, balpha, gnorm = small
    p1, kvp, p2, lrp, states, ya, yb, g, pab, merged = saved
    mm = _hosted_mm(plan)
    aw = A_HEADS * A_HEAD_DIM
    kd = d // 4
    t = MIX_TILE
    tm = _tile(s, (1024,))
    tks = _tile(s, LONG_K)

    plan.grad("w_mix_out", mm("mix_dw_out", "tn", merged, dm, (d, d, s), tm=t, tn=t, tk=tks))
    dpab, dg = plan.host("merge_bwd", lambda st: _merge_bwd(dm, wmo, g, pab, st))
    sel = lambda j: pl.BlockSpec((None, tm, d), lambda i, jj, kk: (j, i, 0))
    dya = mm("mix_dya", "nt", dpab, wpa, (s, aw, d), tm=tm, tn=t, tk=d, out_dtype=BF16, a_spec=sel(0))
    dyb = mm("mix_dyb", "nt", dpab, wpb, (s, 2 * kd, d), tm=tm, tn=t, tk=d, out_dtype=BF16, a_spec=sel(1))
    selk = lambda j: pl.BlockSpec((None, tks, t), lambda i, jj, kk: (j, kk, jj))
    plan.grad("w_proj_a", mm("mix_dwpa", "tn", ya, dpab, (aw, d, s), tm=t, tn=t, tk=tks, b_spec=selk(0)))
    plan.grad("w_proj_b", mm("mix_dwpb", "tn", yb, dpab, (2 * kd, d, s), tm=t, tn=t, tk=tks, b_spec=selk(1)))

    dq, dkp, dvp, dbias = plan.host("attn_bwd", lambda st: _attn_bwd(p1, kvp, bias, dya, st))
    dp1 = jnp.concatenate([dq, dkp[A_PAD:].astype(BF16), dvp[A_PAD:].astype(BF16)], axis=1)
    dp2, dz, gsm = plan.host("gla_bwd", lambda st: _gla_bwd(p2, lrp, wa2p, balpha, gnorm, states, dyb, st))
    dlrp = mm("gla_dlr", "nt", dz, wa2p, (s, LANES, kd), tm=tm, tn=LANES, tk=kd, out_dtype=BF16)
    dwa2p = mm("gla_dwa2", "tn", lrp, dz, (LANES, kd, s), tm=LANES, tn=kd, tk=tks)

    tka = 3 * aw
    assert 6 * kd == tka
    du = mm("mix_du_a", "nn", dp1, wt, (s, d, tka), tm=tm, tn=t, tk=tka)
    du = mm("mix_du_b", "nn", dp2, wt, (s, d, tka), tm=tm, tn=t, tk=tka, add=du,
            b_spec=pl.BlockSpec((tka, t), lambda i, j, kk: (1 + kk, j)))
    du = mm("mix_du_g", "nn", dg, wt_g, (s, d, 2 * d), tm=tm, tn=t, tk=d, add=du, thin=(dlrp, wt_lr),
            a_spec=pl.BlockSpec((None, tm, d), lambda i, j, kk: (kk, i, 0)))
    nkg = d // t
    dw1 = mm("mix_dw_a", "tn", dp1, u2, (3 * aw, d, s), tm=t, tn=t, tk=tks)
    dw2 = mm("mix_dw_b", "tn", dp2, u2, (6 * kd, d, s), tm=t, tn=t, tk=tks)
    dwlr = mm("mix_dw_lr", "tn", dlrp, u2, (LANES, d, s), tm=LANES, tn=t, tk=tks)
    dwg = mm("mix_dw_g", "tn", dg, u2, (2 * d, d, s), tm=t, tn=t, tk=tks,
             a_spec=pl.BlockSpec((None, tks, t), lambda i, j, kk: (i // nkg, kk, i % nkg)))
    pieces = [dw1, dw2, dwlr[:GATE_RANK], dwg]
    shard_rows = sum(p.shape[0] for p in pieces) // N_CHIPS
    plan.grad("w_mix_in", jnp.stack([_virtual_rows(pieces, j * shard_rows, (j + 1) * shard_rows)
                                     for j in range(N_CHIPS)]))
    return du, (dbias, dwa2p[:GATE_RANK], gsm)


def _device_step(x, target, mod, small, plan):
    s, d = x.shape
    row = lambda i: mod[i:i + 1]
    sh1, sc1, g1, sh2, sc2, g2, sh3, sc3, g3 = (row(i) for i in range(N_MOD))

    onehot = _rel_onehot()
    bias = _mm("rel_bias_expand", "nn", small["rel_bias"], onehot, (A_HEADS, CHUNK * A_BAND, REL_SIZE),
               tm=A_HEADS, tn=REL_TILE, tk=REL_SIZE, precision=HIGHEST).reshape(A_HEADS, CHUNK, A_BAND)
    bias = _band_bias(bias)
    wa2p = jnp.pad(small["w_alpha2"], ((0, LANES - GATE_RANK), (0, 0))).astype(BF16)
    mix_small = (bias, wa2p, small["b_alpha"], small["gla_norm_g"])

    u1, u1t = _modulate("mod1", x, sh1, sc1)
    f1, sv1 = _ffn_forward("ffn1", u1, plan)
    h1, u2 = _resid_ln_fwd("ln1_fwd", x, f1, g1, small["ln1_g"], small["ln1_b"], sh2, sc2, 0.5)
    m, svm = _mix_forward(u2, plan, mix_small)
    h2, u3, u3t = _resid_ln_fwd("ln2_fwd", h1, m, g2, small["ln2_g"], small["ln2_b"], sh3, sc3, 1.0, transposed=True)
    f2, sv2 = _ffn_forward("ffn2", u3, plan)

    dr3, df2, acc3 = _final_ln_loss_bwd("ln3_loss_bwd", h2, f2, target, g3, small["ln3_g"], small["ln3_b"], 0.5)
    du3 = _ffn_backward("ffn2", df2, u3t, sv2, plan, in_first=False)
    dr2, dmx, acc2 = _resid_ln_bwd("ln2_bwd", du3, dr3, h1, m, sc3, g2, small["ln2_g"], small["ln2_b"], 1.0)
    du2, (dbias, dwa2, gsm) = _mix_backward(dmx, u2, svm, plan, mix_small)
    dr1, df1, acc1 = _resid_ln_bwd("ln1_bwd", du2, dr2, x, f1, sc2, g1, small["ln1_g"], small["ln1_b"], 0.5)
    du1 = _ffn_backward("ffn1", df1, u1t, sv1, plan, in_first=True)
    grad_x, acc0 = _input_grad("input_grad", du1, dr1, x, sc1)

    drel = _hosted_mm(plan)("rel_bias_grad", "nt", _band_bias_grad(dbias).reshape(A_HEADS, CHUNK * A_BAND), onehot,
                            (A_HEADS, REL_SIZE, CHUNK * A_BAND), tm=A_HEADS, tn=REL_SIZE, tk=REL_TILE, precision=HIGHEST)
    loss = jnp.sum(acc3[ROW_LOSS])
    dmod = jnp.stack([acc0[ROW_DSH], acc0[ROW_DSC], acc1[ROW_DGATE], acc1[ROW_DSH], acc1[ROW_DSC], acc2[ROW_DGATE],
                      acc2[ROW_DSH], acc2[ROW_DSC], acc3[ROW_DGATE]])
    kd = d // 4
    small_grads = dict(ln1_g=acc1[ROW_DLN_G], ln1_b=acc1[ROW_DLN_B], ln2_g=acc2[ROW_DLN_G], ln2_b=acc2[ROW_DLN_B],
                       ln3_g=acc3[ROW_DLN_G], ln3_b=acc3[ROW_DLN_B], b_alpha=gsm[GLA_ROW_DBALPHA],
                       gla_norm_g=gsm[GLA_ROW_DGNORM, :kd // B_HEADS * 2], rel_bias=drel, w_alpha2=dwa2)
    return loss, grad_x, small_grads, dmod


HBM_SPEC = pl.BlockSpec(memory_space=pl.ANY)


def _mesh_pos():
    return lax.axis_index("x"), lax.axis_index("y"), lax.axis_index("c")


def _other_chips(x, y):
    return [(1 - x, y), (x, 1 - y), (1 - x, 1 - y)]


def _remote(src, dst, send_sem, recv_sem, to):
    return pltpu.make_async_remote_copy(src_ref=src, dst_ref=dst, send_sem=send_sem, recv_sem=recv_sem,
                                        device_id=to, device_id_type=MESH)


def _allgather_rows(name, v):
    m_per, n = v.shape

    def body(x_ref, out_ref, send_sems, recv_sems, local_sem):
        x, y, c = _mesh_pos()
        me, sibling = (x, y, c), (x, y, 1 - c)
        chips = _other_chips(x, y)

        def rows(px, py, pc):
            return out_ref.at[pl.ds((4 * px + 2 * py + pc) * m_per, m_per), :]

        def copy(k, block, to, src=None):
            return _remote(rows(*block) if src is None else src, rows(*block), send_sems.at[k], recv_sems.at[k], to)

        mine = pltpu.make_async_copy(x_ref, rows(*me), local_sem)
        mine.start()
        first = [copy(0, me, sibling, src=x_ref)]
        first += [copy(1 + j, me, (*chip, c), src=x_ref) for j, chip in enumerate(chips)]
        for cp in first:
            cp.start()
        passed = [copy(4 + j, (*chip, c), sibling) for j, chip in enumerate(chips)]
        for j, chip in enumerate(chips):
            copy(1 + j, (*chip, c), me).wait_recv()
            passed[j].start()
        copy(0, sibling, me).wait_recv()
        for j, chip in enumerate(chips):
            copy(4 + j, (*chip, 1 - c), me).wait_recv()
        for cp in first + passed:
            cp.wait_send()
        mine.wait()

    return pl.pallas_call(
        body, name=name, out_shape=jax.ShapeDtypeStruct((N_DEV * m_per, n), v.dtype),
        in_specs=[pl.BlockSpec(memory_space=pltpu.VMEM)], out_specs=pl.BlockSpec(memory_space=pltpu.VMEM),
        scratch_shapes=[pltpu.SemaphoreType.DMA((7,)), pltpu.SemaphoreType.DMA((7,)), pltpu.SemaphoreType.DMA],
    )(v)


def _allgather_weights(bufs):
    n = len(bufs)
    TO_X, TO_Y, PASS_TO_X, PASS_TO_Y, SIB_X, SIB_Y, SIB_D0, SIB_D1 = range(8)

    def body(*refs):
        ins, outs = refs[:n], refs[n:2 * n]
        send_sems, recv_sems = refs[2 * n:]
        x, y, c = _mesh_pos()
        sibling = (x, y, 1 - c)
        xn, yn, dg = _other_chips(x, y)
        j0, jx, jy, jd = (2 * p[0] + p[1] for p in ((x, y), xn, yn, dg))
        sends = []

        def rows(w, hc, quarter=None):
            hr = bufs[w].shape[1] // 2
            if quarter is None:
                return pl.ds(hc * hr, hr)
            return pl.ds(hc * hr + quarter * (hr // 2), hr // 2)

        def push(src, dst, w, k, to):
            cp = _remote(src, dst, send_sems.at[w, k], recv_sems.at[w, k], to)
            cp.start()
            sends.append(cp)

        def landed(piece, w, k):
            _remote(piece, piece, send_sems.at[w, k], recv_sems.at[w, k], sibling).wait_recv()

        for w in range(n):
            mine = rows(w, c)
            push(ins[w].at[j0, mine, :], outs[w].at[j0, mine, :], w, TO_X, (*xn, c))
            push(ins[w].at[j0, mine, :], outs[w].at[j0, mine, :], w, TO_Y, (*yn, c))
        for w in range(n):
            half_x = outs[w].at[jx, rows(w, c), :]
            landed(half_x, w, TO_X)
            quarter = outs[w].at[jx, rows(w, c, 1), :]
            push(quarter, quarter, w, PASS_TO_Y, (*yn, c))
            push(half_x, half_x, w, SIB_X, sibling)
            half_y = outs[w].at[jy, rows(w, c), :]
            landed(half_y, w, TO_Y)
            quarter = outs[w].at[jy, rows(w, c, 0), :]
            push(quarter, quarter, w, PASS_TO_X, (*xn, c))
            push(half_y, half_y, w, SIB_Y, sibling)
        for w in range(n):
            for q, arrives_on, on in ((0, PASS_TO_X, SIB_D0), (1, PASS_TO_Y, SIB_D1)):
                piece = outs[w].at[jd, rows(w, c, q), :]
                landed(piece, w, arrives_on)
                push(piece, piece, w, on, sibling)
        for w in range(n):
            landed(outs[w].at[jx, rows(w, 1 - c), :], w, SIB_X)
            landed(outs[w].at[jy, rows(w, 1 - c), :], w, SIB_Y)
            landed(outs[w].at[jd, rows(w, 1 - c, 0), :], w, SIB_D0)
            landed(outs[w].at[jd, rows(w, 1 - c, 1), :], w, SIB_D1)
        for cp in sends:
            cp.wait_send()

    return pl.pallas_call(
        body, name="allgather_weights", out_shape=[jax.ShapeDtypeStruct(b.shape, b.dtype) for b in bufs],
        in_specs=[HBM_SPEC] * n, out_specs=[HBM_SPEC] * n, input_output_aliases={w: w for w in range(n)},
        scratch_shapes=[pltpu.SemaphoreType.DMA((n, 8)), pltpu.SemaphoreType.DMA((n, 8))],
    )(*bufs)


def _half(ref, hc, col, *lead):
    rows, cols = ref.shape[-2:]
    if col:
        return ref.at[(*lead, slice(None), pl.ds(hc * (cols // 2), cols // 2))]
    return ref.at[(*lead, pl.ds(hc * (rows // 2), rows // 2), slice(None))]


def _half_shape(shape, col):
    return shape[:-2] + ((shape[-2], shape[-1] // 2) if col else (shape[-2] // 2, shape[-1]))


def _quarter(ref, hc, q, col, *lead):
    rows, cols = ref.shape[-2:]
    if col:
        return ref.at[(*lead, slice(None), pl.ds(hc * (cols // 2) + q * (cols // 4), cols // 4))]
    return ref.at[(*lead, pl.ds(hc * (rows // 2) + q * (rows // 4), rows // 4), slice(None))]


def _stage_gather_ici(bufs, cols):
    n = len(bufs)
    TO_X, TO_Y, PASS_TO_X, PASS_TO_Y = range(4)

    def places():
        x, y, c = _mesh_pos()
        xn, yn, dg = _other_chips(x, y)
        return c, (*xn, c), (*yn, c), [2 * p[0] + p[1] for p in ((x, y), xn, yn, dg)]

    def remote(src, dst, send, recv, w, k, to):
        return _remote(src, dst, send.at[4 * w + k], recv.at[4 * w + k], to)

    def own(ins, outs, send, recv):
        c, to_x, to_y, (j0, _, _, _) = places()
        for w in range(n):
            for k, to in ((TO_X, to_x), (TO_Y, to_y)):
                yield remote(_half(ins[w], c, cols[w], j0), _half(outs[w], c, cols[w], j0), send, recv, w, k, to)

    def relays(ins, outs, send, recv):
        c, to_x, to_y, (_, jx, jy, _) = places()
        for w in range(n):
            for j, k, q, pass_k, to in ((jx, TO_X, 1, PASS_TO_Y, to_y), (jy, TO_Y, 0, PASS_TO_X, to_x)):
                half = _half(outs[w], c, cols[w], j)
                piece = _quarter(outs[w], c, q, cols[w], j)
                yield remote(half, half, send, recv, w, k, to), remote(piece, piece, send, recv, w, pass_k, to)

    def passed(ins, outs, send, recv):
        c, to_x, _, (_, _, _, jd) = places()
        for w in range(n):
            for q, k in ((0, PASS_TO_X), (1, PASS_TO_Y)):
                piece = _quarter(outs[w], c, q, cols[w], jd)
                yield remote(piece, piece, send, recv, w, k, to_x)

    def start(*refs):
        for cp in own(*refs):
            cp.start()

    def relay(*refs):
        for arrived, onward in relays(*refs):
            arrived.wait_recv()
            onward.start()

    def finish(*refs):
        for cp in passed(*refs):
            cp.wait_recv()
        for cp in own(*refs):
            cp.wait_send()
        for _, onward in relays(*refs):
            onward.wait_send()

    outs = [jax.ShapeDtypeStruct(b.shape, b.dtype) for b in bufs]
    return _Stage(bufs, outs, 4 * n, start, finish, aliases={w: w for w in range(n)}, relay=relay)


def _stage_gather_d2d(partial, cols):
    n = len(partial)

    def copies(ins, outs, send, recv):
        x, y, c = _mesh_pos()
        for w in range(n):
            for r, chip in enumerate(_other_chips(x, y)):
                jr = 2 * chip[0] + chip[1]
                mine = _remote(_half(ins[w], c, cols[w], jr), _half(outs[w], c, cols[w], jr), send.at[3 * w + r],
                               recv.at[3 * w + r], (x, y, 1 - c))
                got = _half(outs[w], 1 - c, cols[w], jr)
                yield mine, _remote(got, got, send.at[3 * w + r], recv.at[3 * w + r], (x, y, 1 - c))

    def start(*refs):
        for mine, _ in copies(*refs):
            mine.start()

    def finish(*refs):
        pairs = list(copies(*refs))
        for _, theirs in pairs:
            theirs.wait_recv()
        for mine, _ in pairs:
            mine.wait_send()

    outs = [jax.ShapeDtypeStruct(p.shape, p.dtype) for p in partial]
    return _Stage(partial, outs, 3 * n, start, finish, aliases={w: w for w in range(n)})


def _stage_exchange_halves(grads, cols):
    n = len(grads)

    def copies(ins, outs, send, recv):
        x, y, c = _mesh_pos()
        for w in range(n):
            yield _remote(_half(ins[w], 1 - c, cols[w], slice(None)), outs[w], send.at[w], recv.at[w], (x, y, 1 - c))

    def start(*refs):
        for cp in copies(*refs):
            cp.start()

    def finish(*refs):
        cps = list(copies(*refs))
        for cp in cps:
            cp.wait_recv()
        for cp in cps:
            cp.wait_send()

    outs = [jax.ShapeDtypeStruct(_half_shape(g.shape, col), g.dtype) for g, col in zip(grads, cols)]
    return _Stage(grads, outs, n, start, finish)


def _stage_scatter(parts):
    n = len(parts)

    def copies(ins, outs, send, recv):
        x, y, c = _mesh_pos()
        for w in range(n):
            for r, chip in enumerate(_other_chips(x, y)):
                jr = 2 * chip[0] + chip[1]
                yield _remote(ins[w].at[jr], outs[w].at[r], send.at[3 * w + r], recv.at[3 * w + r], (*chip, c))

    def start(*refs):
        for cp in copies(*refs):
            cp.start()

    def finish(*refs):
        cps = list(copies(*refs))
        for cp in cps:
            cp.wait_recv()
        for cp in cps:
            cp.wait_send()

    outs = [jax.ShapeDtypeStruct((3,) + p.shape[1:], p.dtype) for p in parts]
    return _Stage(parts, outs, 3 * n, start, finish)


def _stage_share(fulls, cols):
    n = len(fulls)

    def copies(ins, outs, send, recv):
        x, y, c = _mesh_pos()
        for w in range(n):
            theirs = _half(outs[w], 1 - c, cols[w])
            yield (_remote(_half(ins[w], c, cols[w]), _half(outs[w], c, cols[w]), send.at[w], recv.at[w], (x, y, 1 - c)),
                   _remote(theirs, theirs, send.at[w], recv.at[w], (x, y, 1 - c)))

    def start(*refs):
        for mine, _ in copies(*refs):
            mine.start()

    def finish(*refs):
        pairs = list(copies(*refs))
        for _, theirs in pairs:
            theirs.wait_recv()
        for mine, _ in pairs:
            mine.wait_send()

    outs = [jax.ShapeDtypeStruct(h.shape, h.dtype) for h in fulls]
    return _Stage(fulls, outs, n, start, finish, aliases={w: w for w in range(n)})


def _run_stages(name, stages):
    return _pcall(None, stages, name=name, out_shape=[], in_specs=[], out_specs=[])()[1]


TILE_BYTES = 2 * 1024 * 1024
SUM_TILE_BYTES = 4 * 1024 * 1024


def _row_tile(rows, cols, itemsize=4, tile_bytes=TILE_BYTES):
    for t in range(min(rows, tile_bytes // (cols * itemsize)) // SUBLANES * SUBLANES, 0, -SUBLANES):
        if rows % t == 0:
            return t
    return rows


def _col_tile(rows, cols, itemsize=4, tile_bytes=TILE_BYTES):
    for t in (2048, 1024, 512, 256, 128):
        if cols % t == 0 and t * rows * itemsize <= tile_bytes:
            return t
    return cols


def _tiling(rows, cols, col, tile_bytes=TILE_BYTES):
    if col:
        tc = _col_tile(rows, cols, tile_bytes=tile_bytes)
        return (rows, tc), cols // tc
    tr = _row_tile(rows, cols, tile_bytes=tile_bytes)
    return (tr, cols), rows // tr


def _strip(col, i):
    return (0, i) if col else (i, 0)


def _pair_sum(name, g, recv, core, col):
    blk, nb = _tiling(*recv.shape[1:], col, tile_bytes=SUM_TILE_BYTES)

    def body(c_ref, g_ref, r_ref, o_ref):
        o_ref[...] = (g_ref[...] + r_ref[...]).astype(BF16)

    grid_spec = pltpu.PrefetchScalarGridSpec(
        num_scalar_prefetch=1, grid=(N_CHIPS, nb),
        in_specs=[pl.BlockSpec((None,) + blk, lambda j, i, cr: (j,) + _strip(col, cr[0] * nb + i)),
                  pl.BlockSpec((None,) + blk, lambda j, i, cr: (j,) + _strip(col, i))],
        out_specs=pl.BlockSpec((None,) + blk, lambda j, i, cr: (j,) + _strip(col, i)))
    return pl.pallas_call(body, name=name, out_shape=jax.ShapeDtypeStruct(recv.shape, BF16), grid_spec=grid_spec,
                          compiler_params=_cp("parallel", "parallel"))(core, g, recv)


def _quad_sum(name, own, landed, chip_core, col):
    rows, cols = landed.shape[1:]
    blk, nb = _tiling(rows, cols, col, tile_bytes=SUM_TILE_BYTES)
    full = (rows, 2 * cols) if col else (2 * rows, cols)

    def body(cc_ref, own_ref, l_ref, o_ref):
        o_ref[...] = ((own_ref[...].astype(F32) + l_ref[0].astype(F32)) + l_ref[1].astype(F32)) + l_ref[2].astype(F32)

    grid_spec = pltpu.PrefetchScalarGridSpec(
        num_scalar_prefetch=1, grid=(nb,),
        in_specs=[pl.BlockSpec((None,) + blk, lambda i, cc: (cc[0],) + _strip(col, i)),
                  pl.BlockSpec((3,) + blk, lambda i, cc: (0,) + _strip(col, i))],
        out_specs=pl.BlockSpec(blk, lambda i, cc: _strip(col, cc[1] * nb + i)))
    return pl.pallas_call(body, name=name, out_shape=jax.ShapeDtypeStruct(full, F32), grid_spec=grid_spec,
                          compiler_params=_cp("arbitrary"))(chip_core, own, landed)


def _device_sum(name, gathered):
    def body(g_ref, o_ref):
        total = g_ref[0]
        for k in range(1, N_DEV):
            total = total + g_ref[k]
        o_ref[...] = total

    return pl.pallas_call(body, name=name, out_shape=jax.ShapeDtypeStruct(gathered.shape[1:], F32))(gathered)


def _adamw(name, w, g, m, v):
    rows, cols = w.shape
    col = rows % SUBLANES != 0
    blk, nb = _tiling(rows, cols, col)
    bc1 = 1.0 - ADAM_B1 ** ADAM_STEP
    bc2 = 1.0 - ADAM_B2 ** ADAM_STEP

    def body(w_ref, g_ref, m_ref, v_ref, d_ref, mo_ref, vo_ref):
        gv = g_ref[...]
        mn = ADAM_B1 * m_ref[...] + (1.0 - ADAM_B1) * gv
        vn = ADAM_B2 * v_ref[...] + (1.0 - ADAM_B2) * (gv * gv)
        mo_ref[...] = mn
        vo_ref[...] = vn
        d_ref[...] = -ADAM_LR * ((mn / bc1) / (jnp.sqrt(vn / bc2) + ADAM_EPS) + ADAM_WD * w_ref[...])

    spec = pl.BlockSpec(blk, lambda i: _strip(col, i))
    return pl.pallas_call(
        body, name=name, out_shape=[jax.ShapeDtypeStruct((rows, cols), F32)] * 3, grid=(nb,),
        in_specs=[spec] * 4, out_specs=[spec] * 3, compiler_params=_cp("parallel"),
    )(w, g, m, v)


WEIGHTS = ["w_ada", "b_ada", "ffn1_w_in", "ffn1_w_out", "ln1_g", "ln1_b", "w_mix_in", "rel_bias", "w_alpha2",
           "b_alpha", "gla_norm_g", "w_proj_a", "w_proj_b", "w_mix_out", "ln2_g", "ln2_b", "ffn2_w_in", "ffn2_w_out",
           "ln3_g", "ln3_b"]
BIG = {"ffn1_w_in": True, "ffn1_w_out": False, "w_mix_in": False, "w_proj_a": True, "w_proj_b": True,
       "w_mix_out": False, "ffn2_w_in": True, "ffn2_w_out": False}
TRANSPOSED = ("w_mix_in",)
STACKED = ("ffn1_w_in", "ffn2_w_in", "w_mix_in")
GROUP_FFN1 = ("ffn1_w_in", "ffn1_w_out")
GROUP_PROJ = ("w_proj_a", "w_proj_b", "w_mix_out")
SMALL = ["ln1_g", "ln1_b", "ln2_g", "ln2_b", "ln3_g", "ln3_b", "b_alpha", "gla_norm_g", "rel_bias", "w_alpha2"]


def _pad_rows(vec, rows=SUBLANES):
    per = -(-vec.shape[0] // (rows * LANES)) * LANES
    return jnp.pad(vec, (0, rows * per - vec.shape[0])).reshape(rows, per)


def _silu(v):
    return v * _sigmoid(v)


class _MeshPlan:
    def __init__(self, shards, chip, core):
        self.shapes = {k: v.shape for k, v in shards.items()}
        self.slots = {k: lax.dynamic_update_slice(lax.empty((N_CHIPS,) + v.shape, v.dtype), v[None], (chip, 0, 0))
                      for k, v in shards.items()}
        self.core1 = core.astype(jnp.int32).reshape(1)
        self.chip_core = jnp.stack([chip, core]).astype(jnp.int32)
        self.partial, self.full, self.local, self.pair, self.half, self.final, self.memos = {}, {}, {}, {}, {}, {}, {}
        ici, d2d, x1, x2, x3 = self.gather_ici, self.gather_d2d, self.exchange, self.scatter, self.share
        mix_in, in1, out1, in2, out2 = ("w_mix_in",), ("ffn1_w_in",), ("ffn1_w_out",), ("ffn2_w_in",), ("ffn2_w_out",)
        self.schedule = {
            "ffn1_in_fwd": [ici(mix_in)], "ffn1_out_fwd": [d2d(mix_in), ici(out2)],
            "mix_in_g": [ici(GROUP_PROJ), d2d(out2)],
            "attn_fwd": [ici(in2), d2d(GROUP_PROJ)], "gla_fwd": [d2d(in2)],
            "ffn2_dw_in": [x1(out2)], "ffn2_du": [x2(out2), x1(in2)],
            "attn_bwd": [x2(in2), x3(out2)], "gla_bwd": [x3(in2), x1(GROUP_PROJ)],
            "mix_du_g": [x2(GROUP_PROJ)],
            "ffn1_out_bwd": [x1(mix_in), x3(GROUP_PROJ)], "ffn1_dw_in": [x2(mix_in)], "ffn1_dw_out": [x3(mix_in), x1(in1)],
            "ffn1_du": [x2(in1), x1(out1)], "rel_bias_grad": [x2(out1), x3(in1)],
        }

    def weight(self, k):
        return self.full[k]

    def grad(self, k, g):
        r, cc = self.shapes[k]
        if k not in STACKED:
            g = g.reshape(r, N_CHIPS, cc).transpose(1, 0, 2) if BIG[k] else g.reshape(N_CHIPS, r, cc)
        self.local[k] = g

    def memo(self, key, make):
        if key not in self.memos:
            self.memos[key] = make()
        return self.memos[key]

    def host(self, name, call):
        builders = self.schedule.get(name)
        if not builders:
            return call(None)
        built = [b() for b in builders]
        main, comm = call([st for st, _ in built])
        for (_, post), res in zip(built, comm):
            post(res)
        return main

    def run(self, name, builders):
        built = [b() for b in builders]
        for (_, post), res in zip(built, _run_stages(name, [st for st, _ in built])):
            post(res)

    def set_gathered(self, names, gathered):
        for k, g in zip(names, gathered):
            _, r, cc = g.shape
            if k not in STACKED:
                g = g.transpose(1, 0, 2).reshape(r, N_CHIPS * cc) if BIG[k] else g.reshape(N_CHIPS * r, cc)
            self.full[k] = g

    @staticmethod
    def cols(names):
        return [k in TRANSPOSED for k in names]

    def gather_ici(self, names):
        def post(res):
            self.partial.update(zip(names, res))
        return lambda: (_stage_gather_ici([self.slots[k] for k in names], self.cols(names)), post)

    def gather_d2d(self, names):
        return lambda: (_stage_gather_d2d([self.partial[k] for k in names], self.cols(names)),
                        lambda res: self.set_gathered(names, res))

    def exchange(self, names):
        def post(res):
            for k, r in zip(names, res):
                self.pair[k] = _pair_sum(f"pair_sum_{k}", self.local[k], r, self.core1, k in TRANSPOSED)
        return lambda: (_stage_exchange_halves([self.local[k] for k in names], self.cols(names)), post)

    def scatter(self, names):
        def post(res):
            for k, landed in zip(names, res):
                self.half[k] = _quad_sum(f"quad_sum_{k}", self.pair[k], landed, self.chip_core, k in TRANSPOSED)
        return lambda: (_stage_scatter([self.pair[k] for k in names]), post)

    def share(self, names):
        def post(res):
            self.final.update(zip(names, res))
        return lambda: (_stage_share([self.half[k] for k in names], self.cols(names)), post)


def _step(args):
    x_pos, y_pos, c_pos = _mesh_pos()
    chip = 2 * x_pos + y_pos
    dev = 4 * x_pos + 2 * y_pos + c_pos
    take = lambda name, k: args[name][0].T if k in TRANSPOSED else args[name][0]
    w = {k: take(k, k) for k in WEIGHTS}
    mom = {k: take("m_" + k, k) for k in WEIGHTS}
    vel = {k: take("v_" + k, k) for k in WEIGHTS}
    x = args["x"][0]
    target = args["loss_target"][0]
    s, d = x.shape
    kd = d // 4
    rel_sh = w["rel_bias"].shape[1]
    wa2_sh = w["w_alpha2"].shape[1]
    ada_sh = w["w_ada"].shape[1]

    n_rel, n_wa2 = A_HEADS * rel_sh, GATE_RANK * wa2_sh
    packed = _pad_rows(jnp.concatenate([args["c"].reshape(-1), w["rel_bias"].reshape(-1), w["w_alpha2"].reshape(-1)]))
    got = _allgather_rows("gather_small_inputs", packed).reshape(N_DEV, -1)
    c_all = got[:, :d]
    per_chip = got[0::2]
    rel_bias = per_chip[:, d:d + n_rel].reshape(N_CHIPS, A_HEADS, rel_sh).transpose(1, 0, 2).reshape(A_HEADS, -1)
    w_alpha2 = per_chip[:, d + n_rel:d + n_rel + n_wa2].reshape(N_CHIPS, GATE_RANK, wa2_sh).transpose(1, 0, 2)
    w_alpha2 = w_alpha2.reshape(GATE_RANK, -1)

    b_shard = lax.dynamic_slice(w["b_ada"], (chip * ada_sh,), (ada_sh,))
    mod_shard = _mm("ada_fwd", "nn", c_all, w["w_ada"], (N_DEV, ada_sh, d), tm=N_DEV, tn=_tile(ada_sh, (512, 128)),
                    tk=d, precision=HIGHEST, a_fn=_silu, add=jnp.broadcast_to(b_shard[None], (N_DEV, ada_sh)))
    mod_all = _allgather_rows("gather_mod", mod_shard).reshape(N_DEV, N_DEV, ada_sh)[0::2]
    mod_all = mod_all.transpose(1, 0, 2).reshape(N_DEV, N_MOD * d)
    mod = lax.dynamic_index_in_dim(mod_all, dev, 0, keepdims=False).reshape(N_MOD, d)

    names = list(BIG)
    plan = _MeshPlan({k: w[k].astype(BF16) for k in names}, chip, c_pos)
    plan.set_gathered(GROUP_FFN1, _allgather_weights([plan.slots[k] for k in GROUP_FFN1]))

    small = dict(rel_bias=rel_bias, w_alpha2=w_alpha2, b_alpha=w["b_alpha"][None], gla_norm_g=w["gla_norm_g"][None])
    for k in ("ln1_g", "ln1_b", "ln2_g", "ln2_b", "ln3_g", "ln3_b"):
        small[k] = w[k][None]
    loss_local, grad_x, small_grads, dmod = _device_step(x, target, mod, small, plan)
    loss = lax.psum(loss_local, ("x", "y", "c"))
    plan.run("grad_tail_share", [plan.share(GROUP_FFN1[1:])])

    flat = jnp.concatenate([small_grads[k].reshape(-1) for k in SMALL] + [dmod.reshape(-1)])
    n_small = flat.shape[0] - N_MOD * d
    packed = _pad_rows(flat)
    all_small = _allgather_rows("gather_small_grads", packed).reshape(N_DEV, SUBLANES, -1)
    summed = _device_sum("small_grad_sum", all_small).reshape(-1)
    dmod_all = all_small.reshape(N_DEV, -1)[:, n_small:n_small + N_MOD * d]
    dmod_shard = lax.dynamic_slice(dmod_all, (0, chip * ada_sh), (N_DEV, ada_sh))
    grads = {"b_ada": summed[n_small:n_small + N_MOD * d]}
    off = 0
    for k in SMALL:
        size = small_grads[k].size
        grads[k] = summed[off:off + size].reshape(small_grads[k].shape)
        off += size
    grads["rel_bias"] = lax.dynamic_slice(grads["rel_bias"], (0, chip * rel_sh), (A_HEADS, rel_sh))
    grads["w_alpha2"] = lax.dynamic_slice(grads["w_alpha2"], (0, chip * wa2_sh), (GATE_RANK, wa2_sh))
    grads["w_ada"] = _mm("ada_bwd", "nn", jnp.pad(c_all.T, ((0, 0), (0, LANES - N_DEV))),
                         jnp.pad(dmod_shard, ((0, LANES - N_DEV), (0, 0))), (d, ada_sh, LANES), tm=_tile(d, (1024,)),
                         tn=_tile(ada_sh, (512, 128)), tk=LANES, precision=HIGHEST, a_fn=_silu)

    grads.update(plan.final)

    delta, new_m, new_v = {}, {}, {}
    for k in ["w_ada"] + names:
        delta[k], new_m[k], new_v[k] = _adamw(f"adamw_{k}", w[k], grads[k], mom[k], vel[k])
    tiny = ["b_ada"] + SMALL
    pack = lambda src: _pad_rows(jnp.concatenate([src[k].reshape(-1) for k in tiny]), rows=1).reshape(-1, LANES)
    outs = _adamw("adamw_small", pack(w), pack(grads), pack(mom), pack(vel))
    off = 0
    for k in tiny:
        size = w[k].size
        for dst, src in zip((delta, new_m, new_v), outs):
            dst[k] = src.reshape(-1)[off:off + size].reshape(w[k].shape)
        off += size

    give = lambda src: [src[k].T[None] if k in TRANSPOSED else src[k][None] for k in WEIGHTS]
    return (loss, grad_x[None], *give(grads), *give(delta), *give(new_m), *give(new_v))


def kernel(x, c, w_ada, b_ada, ffn1_w_in, ffn1_w_out, ln1_g, ln1_b, w_mix_in, rel_bias, w_alpha2, b_alpha, gla_norm_g, w_proj_a, w_proj_b, w_mix_out, ln2_g, ln2_b, ffn2_w_in, ffn2_w_out, ln3_g, ln3_b, loss_target, m_w_ada, m_b_ada, m_ffn1_w_in, m_ffn1_w_out, m_ln1_g, m_ln1_b, m_w_mix_in, m_rel_bias, m_w_alpha2, m_b_alpha, m_gla_norm_g, m_w_proj_a, m_w_proj_b, m_w_mix_out, m_ln2_g, m_ln2_b, m_ffn2_w_in, m_ffn2_w_out, m_ln3_g, m_ln3_b, v_w_ada, v_b_ada, v_ffn1_w_in, v_ffn1_w_out, v_ln1_g, v_ln1_b, v_w_mix_in, v_rel_bias, v_w_alpha2, v_b_alpha, v_gla_norm_g, v_w_proj_a, v_w_proj_b, v_w_mix_out, v_ln2_g, v_ln2_b, v_ffn2_w_in, v_ffn2_w_out, v_ln3_g, v_ln3_b):
    return _step(dict(locals()))
```

```python
import functools

import jax
import jax.numpy as jnp
from jax import lax
from jax.experimental import pallas as pl
from jax.experimental.pallas import tpu as pltpu

F32 = jnp.float32
BF16 = jnp.bfloat16
MESH = pl.DeviceIdType.MESH
HIGHEST = lax.Precision.HIGHEST

VMEM_LIMIT_BYTES = 56 * 1024 * 1024
LANES = 128
SUBLANES = 8

CHUNK = 64
A_HEADS = 16
A_HEAD_DIM = 64
A_PAST_CHUNKS = 8
A_BAND = (A_PAST_CHUNKS + 1) * CHUNK
A_PAD = A_PAST_CHUNKS * CHUNK
REL_CLIP = 256
REL_SIZE = REL_CLIP + CHUNK
B_HEADS = 4
GATE_RANK = 16
GATE_TAU = 16.0
N_MOD = 9
DEPTH = 1
ALPHA = (2.0 * DEPTH) ** 0.25
LN_EPS = 1e-5
RMS_EPS = 1e-6
ADAM_LR = 0.001
ADAM_B1 = 0.9
ADAM_B2 = 0.999
ADAM_EPS = 1e-08
ADAM_WD = 0.01
ADAM_STEP = 10
NEG_BIG = -1e30

N_CHIPS = 4
N_DEV = 8


def _cp(*sem):
    return pltpu.CompilerParams(dimension_semantics=sem, vmem_limit_bytes=VMEM_LIMIT_BYTES)


class _Stage:
    def __init__(self, arrays, out_shapes, n_sems, start, finish, aliases=None, relay=None):
        self.arrays, self.out_shapes, self.n_sems = list(arrays), list(out_shapes), n_sems
        self.start, self.finish, self.relay, self.aliases = start, finish, relay, dict(aliases or {})


def _pcall(body, stages, *, name, out_shape, in_specs, out_specs, grid=(), scratch_shapes=(), compiler_params=None):
    single = not isinstance(out_shape, (list, tuple))
    outs = [out_shape] if single else list(out_shape)
    ospecs = [out_specs] if single else list(out_specs)
    in_specs, scratch_shapes = list(in_specs), list(scratch_shapes)
    n_in, n_out, n_sc = len(in_specs), len(outs), len(scratch_shapes)
    stages = list(stages or [])
    c_in = [a for st in stages for a in st.arrays]
    c_out = [o for st in stages for o in st.out_shapes]
    aliases = {}
    io, oo = n_in, n_out
    for st in stages:
        for a, b in st.aliases.items():
            aliases[io + a] = oo + b
        io += len(st.arrays)
        oo += len(st.out_shapes)

    def wrapped(*refs):
        ins = refs[:n_in]
        cins = refs[n_in:n_in + len(c_in)]
        base = n_in + len(c_in)
        mouts = refs[base:base + n_out]
        couts = refs[base + n_out:base + n_out + len(c_out)]
        base += n_out + len(c_out)
        scr = refs[base:base + n_sc]
        sems = refs[base + n_sc:]

        def each(phase):
            i = o = 0
            for k, st in enumerate(stages):
                fn = (st.start, st.relay, st.finish)[phase]
                if fn is not None:
                    fn(cins[i:i + len(st.arrays)], couts[o:o + len(st.out_shapes)], sems[2 * k], sems[2 * k + 1])
                i += len(st.arrays)
                o += len(st.out_shapes)

        if stages and grid:
            step = functools.reduce(lambda acc, a: acc * grid[a] + pl.program_id(a), range(len(grid)), 0)
            steps = functools.reduce(lambda a, b: a * b, grid)
            pl.when(step == 0)(lambda: each(0))
            if any(st.relay for st in stages):
                pl.when(step == (2 * steps) // 3)(lambda: each(1))
            if body is not None:
                body(*ins, *mouts, *scr)
            pl.when(step == steps - 1)(lambda: each(2))
        else:
            each(0)
            each(1)
            if body is not None:
                body(*ins, *mouts, *scr)
            each(2)

    sem_shapes = []
    for st in stages:
        sem_shapes += [pltpu.SemaphoreType.DMA((st.n_sems,)), pltpu.SemaphoreType.DMA((st.n_sems,))]
    kwargs = dict(grid=grid) if grid else {}
    if compiler_params is not None:
        kwargs["compiler_params"] = compiler_params

    def run(*operands):
        res = pl.pallas_call(
            wrapped, name=name, out_shape=outs + c_out, in_specs=in_specs + [HBM_SPEC] * len(c_in),
            out_specs=ospecs + [HBM_SPEC] * len(c_out), scratch_shapes=scratch_shapes + sem_shapes,
            input_output_aliases=aliases, **kwargs)(*operands, *c_in)
        main = res[0] if single else tuple(res[:n_out])
        if not stages:
            return main
        comm, o = [], n_out
        for st in stages:
            comm.append(list(res[o:o + len(st.out_shapes)]))
            o += len(st.out_shapes)
        return main, comm

    return run


LONG_K = (2048, 1024)


def _tile(n, prefs):
    for t in prefs:
        if t <= n and n % t == 0:
            return t
    return n


_DIMS = {"nn": (((1,), (0,)), ((), ())), "nt": (((1,), (1,)), ((), ())), "tn": (((0,), (0,)), ((), ()))}


def _dot(a, b, mode="nn", precision=None):
    return lax.dot_general(a, b, _DIMS[mode], precision=precision, preferred_element_type=F32)


def _sigmoid(x):
    return 0.5 * jnp.tanh(0.5 * x) + 0.5


EPILOGUE_STRIP = 256


def _strips(n, width=EPILOGUE_STRIP):
    width = width if n % width == 0 else n
    return [slice(j, j + width) for j in range(0, n, width)]


def _mm(name, mode, a, b, mnk, *, tm, tn, tk, out_dtype=F32, precision=None, a_spec=None, b_spec=None,
        out_shape=None, o_spec=None, add=None, a_fn=None, thin=None, stages=None):
    m, n, k = mnk
    assert m % tm == 0 and n % tn == 0 and k % tk == 0, (name, mnk, tm, tn, tk)
    nk = k // tk
    if a_spec is None:
        a_spec = {"nn": pl.BlockSpec((tm, tk), lambda i, j, kk: (i, kk)),
                  "nt": pl.BlockSpec((tm, tk), lambda i, j, kk: (i, kk)),
                  "tn": pl.BlockSpec((tk, tm), lambda i, j, kk: (kk, i))}[mode]
    if b_spec is None:
        b_spec = {"nn": pl.BlockSpec((tk, tn), lambda i, j, kk: (kk, j)),
                  "nt": pl.BlockSpec((tn, tk), lambda i, j, kk: (j, kk)),
                  "tn": pl.BlockSpec((tk, tn), lambda i, j, kk: (kk, j))}[mode]
    if o_spec is None:
        o_spec = pl.BlockSpec((tm, tn), lambda i, j, kk: (i, j))
    if out_shape is None:
        out_shape = (m, n)
    has_add = add is not None
    n_in = 2 + has_add + (2 if thin else 0)

    def body(*refs):
        a_ref, b_ref = refs[0], refs[1]
        add_ref = refs[2] if has_add else None
        o_ref = refs[n_in]
        av = a_ref[...]
        if a_fn is not None:
            av = a_fn(av)
        part = _dot(av, b_ref[...], mode, precision)

        def finish(total):
            if has_add:
                total = total + add_ref[...]
            if thin:
                total = total + _dot(refs[n_in - 2][...], refs[n_in - 1][...])
            o_ref[...] = total.astype(out_dtype)

        if nk == 1:
            finish(part)
        else:
            acc_ref = refs[-1]
            kk = pl.program_id(2)

            @pl.when(kk == 0)
            def _():
                acc_ref[...] = part

            @pl.when(kk > 0)
            def _():
                acc_ref[...] += part

            @pl.when(kk == nk - 1)
            def _():
                finish(acc_ref[...])

    in_specs = [a_spec, b_spec]
    operands = [a, b]
    if has_add:
        in_specs.append(pl.BlockSpec((tm, tn), lambda i, j, kk: (i, j)))
        operands.append(add)
    if thin:
        k2 = thin[0].shape[1]
        in_specs += [pl.BlockSpec((tm, k2), lambda i, j, kk: (i, 0)), pl.BlockSpec((k2, tn), lambda i, j, kk: (0, j))]
        operands += list(thin)
    return _pcall(
        body, stages, name=name, out_shape=jax.ShapeDtypeStruct(out_shape, out_dtype), grid=(m // tm, n // tn, nk),
        in_specs=in_specs, out_specs=o_spec,
        scratch_shapes=[pltpu.VMEM((tm, tn), F32)] if nk > 1 else [],
        compiler_params=_cp("arbitrary", "arbitrary", "arbitrary") if stages else _cp("parallel", "parallel", "arbitrary"),
    )(*operands)


def _row_spec(tr, d):
    return pl.BlockSpec((tr, d), lambda i: (i, 0))


def _vec_spec(d, rows=1):
    return pl.BlockSpec((rows, d), lambda i: (0, 0))


def _col_spec(d, tr):
    return pl.BlockSpec((d, tr), lambda i: (0, i))


def _modulate(name, x, sh, sc):
    s, d = x.shape
    tr = _tile(s, (512, 256))

    def body(x_ref, sh_ref, sc_ref, o_ref, ot_ref):
        u = x_ref[...] * (1.0 + sc_ref[...]) + sh_ref[...]
        o_ref[...] = u.astype(BF16)
        ot_ref[...] = u.T.astype(BF16)

    return pl.pallas_call(
        body, name=name, out_shape=(jax.ShapeDtypeStruct((s, d), BF16), jax.ShapeDtypeStruct((d, s), BF16)),
        grid=(s // tr,), in_specs=[_row_spec(tr, d), _vec_spec(d), _vec_spec(d)],
        out_specs=(_row_spec(tr, d), _col_spec(d, tr)), compiler_params=_cp("parallel"),
    )(x, sh, sc)


def _ln_stats(r):
    mu = jnp.mean(r, axis=-1, keepdims=True)
    xc = r - mu
    var = jnp.mean(xc * xc, axis=-1, keepdims=True)
    rstd = lax.rsqrt(var + LN_EPS)
    return xc * rstd, rstd


def _resid_ln_fwd(name, x, f, gate, ln_g, ln_b, sh_n, sc_n, coef, transposed=False):
    s, d = x.shape
    tr = _tile(s, (256,))

    def body(x_ref, f_ref, gate_ref, g_ref, b_ref, sh_ref, sc_ref, h_ref, u_ref, *ut_ref):
        r = ALPHA * x_ref[...] + (coef * gate_ref[...]) * f_ref[...]
        xhat, _ = _ln_stats(r)
        h = xhat * g_ref[...] + b_ref[...]
        h_ref[...] = h
        u = h * (1.0 + sc_ref[...]) + sh_ref[...]
        u_ref[...] = u.astype(BF16)
        if transposed:
            ut_ref[0][...] = u.T.astype(BF16)

    extra_shape = (jax.ShapeDtypeStruct((d, s), BF16),) if transposed else ()
    extra_spec = (_col_spec(d, tr),) if transposed else ()
    return pl.pallas_call(
        body, name=name,
        out_shape=(jax.ShapeDtypeStruct((s, d), F32), jax.ShapeDtypeStruct((s, d), BF16)) + extra_shape,
        grid=(s // tr,), in_specs=[_row_spec(tr, d), _row_spec(tr, d)] + [_vec_spec(d)] * 5,
        out_specs=(_row_spec(tr, d), _row_spec(tr, d)) + extra_spec, compiler_params=_cp("parallel"),
    )(x, f, gate, ln_g, ln_b, sh_n, sc_n)


ROW_DSC, ROW_DSH, ROW_DLN_G, ROW_DLN_B, ROW_DGATE, ROW_LOSS = 0, 1, 2, 3, 4, 5


def _ln_bwd_core(dy, xhat, rstd, ln_g):
    dxhat = dy * ln_g
    m1 = jnp.mean(dxhat, axis=-1, keepdims=True)
    m2 = jnp.mean(dxhat * xhat, axis=-1, keepdims=True)
    return rstd * (dxhat - m1 - xhat * m2)


def _colsum(v):
    return jnp.sum(v, axis=0, keepdims=True)


def _final_ln_loss_bwd(name, x, f, target, gate, ln_g, ln_b, coef):
    s, d = x.shape
    tr = _tile(s, (256,))
    inv_d = 1.0 / d

    def body(x_ref, f_ref, t_ref, gate_ref, g_ref, b_ref, dr_ref, df_ref, acc_ref):
        @pl.when(pl.program_id(0) == 0)
        def _():
            acc_ref[...] = jnp.zeros_like(acc_ref)

        fv = f_ref[...]
        r = ALPHA * x_ref[...] + (coef * gate_ref[...]) * fv
        xhat, rstd = _ln_stats(r)
        h = xhat * g_ref[...] + b_ref[...]
        err = h - t_ref[...]
        dy = err * inv_d
        dr = _ln_bwd_core(dy, xhat, rstd, g_ref[...])
        dr_ref[...] = dr
        df_ref[...] = ((coef * gate_ref[...]) * dr).astype(BF16)
        acc_ref[ROW_DLN_G:ROW_DLN_G + 1, :] += _colsum(dy * xhat)
        acc_ref[ROW_DLN_B:ROW_DLN_B + 1, :] += _colsum(dy)
        acc_ref[ROW_DGATE:ROW_DGATE + 1, :] += _colsum((coef * dr) * fv)
        acc_ref[ROW_LOSS:ROW_LOSS + 1, :] += _colsum(err * err) * (0.5 * inv_d)

    return pl.pallas_call(
        body, name=name,
        out_shape=(jax.ShapeDtypeStruct((s, d), F32), jax.ShapeDtypeStruct((s, d), BF16),
                   jax.ShapeDtypeStruct((SUBLANES, d), F32)),
        grid=(s // tr,), in_specs=[_row_spec(tr, d)] * 3 + [_vec_spec(d)] * 3,
        out_specs=(_row_spec(tr, d), _row_spec(tr, d), _vec_spec(d, SUBLANES)),
        compiler_params=_cp("arbitrary"),
    )(x, f, target, gate, ln_g, ln_b)


def _resid_ln_bwd(name, du_n, dr_n, x, f, sc_n, gate, ln_g, ln_b, coef):
    s, d = x.shape
    tr = _tile(s, (256,))

    def body(du_ref, drn_ref, x_ref, f_ref, sc_ref, gate_ref, g_ref, b_ref, dr_ref, df_ref, acc_ref):
        @pl.when(pl.program_id(0) == 0)
        def _():
            acc_ref[...] = jnp.zeros_like(acc_ref)

        fv = f_ref[...]
        du = du_ref[...]
        r = ALPHA * x_ref[...] + (coef * gate_ref[...]) * fv
        xhat, rstd = _ln_stats(r)
        h = xhat * g_ref[...] + b_ref[...]
        dy = du * (1.0 + sc_ref[...]) + ALPHA * drn_ref[...]
        dr = _ln_bwd_core(dy, xhat, rstd, g_ref[...])
        dr_ref[...] = dr
        df_ref[...] = ((coef * gate_ref[...]) * dr).astype(BF16)
        acc_ref[ROW_DSC:ROW_DSC + 1, :] += _colsum(du * h)
        acc_ref[ROW_DSH:ROW_DSH + 1, :] += _colsum(du)
        acc_ref[ROW_DLN_G:ROW_DLN_G + 1, :] += _colsum(dy * xhat)
        acc_ref[ROW_DLN_B:ROW_DLN_B + 1, :] += _colsum(dy)
        acc_ref[ROW_DGATE:ROW_DGATE + 1, :] += _colsum((coef * dr) * fv)

    return pl.pallas_call(
        body, name=name,
        out_shape=(jax.ShapeDtypeStruct((s, d), F32), jax.ShapeDtypeStruct((s, d), BF16),
                   jax.ShapeDtypeStruct((SUBLANES, d), F32)),
        grid=(s // tr,), in_specs=[_row_spec(tr, d)] * 4 + [_vec_spec(d)] * 4,
        out_specs=(_row_spec(tr, d), _row_spec(tr, d), _vec_spec(d, SUBLANES)),
        compiler_params=_cp("arbitrary"),
    )(du_n, dr_n, x, f, sc_n, gate, ln_g, ln_b)


def _input_grad(name, du, dr, x, sc):
    s, d = x.shape
    tr = _tile(s, (256,))

    def body(du_ref, dr_ref, x_ref, sc_ref, gx_ref, acc_ref):
        @pl.when(pl.program_id(0) == 0)
        def _():
            acc_ref[...] = jnp.zeros_like(acc_ref)

        du = du_ref[...]
        gx_ref[...] = du * (1.0 + sc_ref[...]) + ALPHA * dr_ref[...]
        acc_ref[ROW_DSC:ROW_DSC + 1, :] += _colsum(du * x_ref[...])
        acc_ref[ROW_DSH:ROW_DSH + 1, :] += _colsum(du)

    return pl.pallas_call(
        body, name=name,
        out_shape=(jax.ShapeDtypeStruct((s, d), F32), jax.ShapeDtypeStruct((SUBLANES, d), F32)),
        grid=(s // tr,), in_specs=[_row_spec(tr, d)] * 3 + [_vec_spec(d)],
        out_specs=(_row_spec(tr, d), _vec_spec(d, SUBLANES)), compiler_params=_cp("arbitrary"),
    )(du, dr, x, sc)


def _ffn_in_fwd(name, u, w_in, stages=None):
    s, d = u.shape
    cs = w_in.shape[2]
    f = 2 * cs
    tm, tn = _tile(s, (2048, 1024, 512)), _tile(cs, (256, 128))
    nb = f // tn
    nbs = cs // tn

    def body(u_ref, wa_ref, wb_ref, ab_ref, act_ref):
        for rows in _strips(tm, 512):
            uv = u_ref[rows, :]
            a = _dot(uv, wa_ref[...])
            b = _dot(uv, wb_ref[...])
            sg = _sigmoid(a)
            silu = a * sg
            ab_ref[0, rows, :] = (b * (sg + silu * (1.0 - sg))).astype(BF16)
            ab_ref[1, rows, :] = silu.astype(BF16)
            act_ref[rows, :] = (silu * b).astype(BF16)

    return _pcall(
        body, stages, name=name,
        out_shape=(jax.ShapeDtypeStruct((2, s, f), BF16), jax.ShapeDtypeStruct((s, f), BF16)),
        grid=(s // tm, nb),
        in_specs=[pl.BlockSpec((tm, d), lambda i, j: (i, 0)),
                  pl.BlockSpec((None, d, tn), lambda i, j: (j // nbs, 0, j % nbs)),
                  pl.BlockSpec((None, d, tn), lambda i, j: (2 + j // nbs, 0, j % nbs))],
        out_specs=(pl.BlockSpec((2, tm, tn), lambda i, j: (0, i, j)), pl.BlockSpec((tm, tn), lambda i, j: (i, j))),
        compiler_params=_cp("arbitrary", "arbitrary"),
    )(u, w_in, w_in)


def _ffn_out_bwd(name, df, w_out, ab, stages=None):
    s, d = df.shape
    f = w_out.shape[0]
    tm, tn = _tile(s, (1024, 512)), _tile(f, (512, 256, 128))

    def body(df_ref, w_ref, ab_ref, dab_ref):
        dfv = df_ref[...]
        for cols in _strips(tn):
            dact = _dot(dfv, w_ref[cols, :], "nt")
            dab_ref[0, :, cols] = (dact * ab_ref[0, :, cols].astype(F32)).astype(BF16)
            dab_ref[1, :, cols] = (dact * ab_ref[1, :, cols].astype(F32)).astype(BF16)

    return _pcall(
        body, stages, name=name, out_shape=jax.ShapeDtypeStruct((2, s, f), BF16), grid=(s // tm, f // tn),
        in_specs=[pl.BlockSpec((tm, d), lambda i, j: (i, 0)), pl.BlockSpec((tn, d), lambda i, j: (j, 0)),
                  pl.BlockSpec((2, tm, tn), lambda i, j: (0, i, j))],
        out_specs=pl.BlockSpec((2, tm, tn), lambda i, j: (0, i, j)),
        compiler_params=_cp("arbitrary", "arbitrary"),
    )(df, w_out, ab)


def _ffn_forward(tag, u, plan):
    w_in = plan.weight(f"{tag}_w_in")
    s, d = u.shape
    ab, act = plan.host(f"{tag}_in_fwd", lambda st: _ffn_in_fwd(f"{tag}_in_fwd", u, w_in, st))
    w_out = plan.weight(f"{tag}_w_out")
    f = w_out.shape[0]
    out = plan.host(f"{tag}_out_fwd", lambda st: _mm(
        f"{tag}_out_fwd", "nn", act, w_out, (s, d, f), tm=_tile(s, (1024,)), tn=_tile(d, (1024,)),
        tk=_tile(f, (2816, 1408, 512, 128)), stages=st))
    return out, (ab, act)


def _ffn_backward(tag, df, ut, saved, plan, in_first):
    w_in, w_out = plan.weight(f"{tag}_w_in"), plan.weight(f"{tag}_w_out")
    ab, act = saved
    d, s = ut.shape
    f = w_out.shape[0]
    dab = plan.host(f"{tag}_out_bwd", lambda st: _ffn_out_bwd(f"{tag}_out_bwd", df, w_out, ab, st))
    cs = w_in.shape[2]
    tk = _tile(cs, (2816, 1408, 256, 128))
    nkh, nks = f // tk, cs // tk
    tmd = _tile(d, (1024,))
    tks = _tile(s, LONG_K)

    def dw_in():
        tw = _tile(cs, (256, 128))
        nwh, nws = f // tw, cs // tw
        plan.grad(f"{tag}_w_in", plan.host(f"{tag}_dw_in", lambda st: _mm(
            f"{tag}_dw_in", "nn", ut, dab, (d, 2 * f, s), tm=tmd, tn=tw, tk=s,
            b_spec=pl.BlockSpec((None, s, tw), lambda i, j, kk: (j // nwh, 0, j % nwh)), out_shape=(N_CHIPS, d, cs),
            o_spec=pl.BlockSpec((None, tmd, tw), lambda i, j, kk: (j // nws, i, j % nws)), stages=st)))

    def dw_out():
        plan.grad(f"{tag}_w_out", plan.host(f"{tag}_dw_out", lambda st: _mm(
            f"{tag}_dw_out", "tn", act, df, (f, d, s), tm=_tile(f, (1408, 512, 128)), tn=tmd, tk=tks, stages=st)))

    for step in ((dw_in, dw_out) if in_first else (dw_out, dw_in)):
        step()
    return plan.host(f"{tag}_du", lambda st: _mm(
        f"{tag}_du", "nt", dab, w_in, (s, d, 2 * f), tm=_tile(s, (1024,)), tn=tmd, tk=tk,
        a_spec=pl.BlockSpec((None, _tile(s, (1024,)), tk), lambda i, j, kk: (kk // nkh, i, kk % nkh)),
        b_spec=pl.BlockSpec((None, tmd, tk), lambda i, j, kk: (kk // nks, j, kk % nks)), stages=st))


ATTN_Q = 4 * CHUNK
ATTN_W = ATTN_Q + A_PAD


def _band_bias(bias):
    n = ATTN_Q // CHUNK
    rows = [jnp.pad(bias, ((0, 0), (0, 0), (i * CHUNK, (n - 1 - i) * CHUNK)), constant_values=NEG_BIG)
            for i in range(n)]
    return jnp.concatenate(rows, axis=1)


def _band_bias_grad(dband):
    n = ATTN_Q // CHUNK
    parts = [dband[:, i * CHUNK:(i + 1) * CHUNK, i * CHUNK:i * CHUNK + A_BAND] for i in range(n)]
    return functools.reduce(jnp.add, parts)


def _attn_probs(q, kw, bias, key0):
    sc = _dot(q, kw, "nt") * (A_HEAD_DIM ** -0.5) + bias
    ks = lax.broadcasted_iota(jnp.int32, sc.shape, 1)
    sc = jnp.where(key0 + ks >= 0, sc, NEG_BIG)
    p = jnp.exp(sc - jnp.max(sc, axis=-1, keepdims=True))
    return p * (1.0 / jnp.sum(p, axis=-1, keepdims=True))


def _head_masks():
    lane = lax.broadcasted_iota(jnp.int32, (1, LANES), 1)
    return [lane // A_HEAD_DIM == h for h in range(LANES // A_HEAD_DIM)]


def _attn_fwd(p1, kvp, band, stages=None):
    s = p1.shape[0]
    aw = A_HEADS * A_HEAD_DIM
    nblk = aw // LANES
    hpb = LANES // A_HEAD_DIM
    assert s % ATTN_Q == 0

    def body(q_ref, k_ref, v_ref, b_ref, o_ref):
        base = pl.multiple_of(pl.program_id(1) * ATTN_Q, ATTN_Q)
        qv = q_ref[...]
        kw = k_ref[pl.ds(base, ATTN_W), :]
        vw = v_ref[pl.ds(base, ATTN_W), :]
        out = jnp.zeros((ATTN_Q, LANES), F32)
        for h, mask in enumerate(_head_masks()):
            p = _attn_probs(jnp.where(mask, qv, jnp.zeros_like(qv)), kw, b_ref[h], base - A_PAD)
            out = jnp.where(mask, _dot(p.astype(BF16), vw), out)
        o_ref[...] = out.astype(BF16)

    kv_rows = s + A_PAD
    return _pcall(
        body, stages, name="attn_fwd", out_shape=jax.ShapeDtypeStruct((s, aw), BF16), grid=(nblk, s // ATTN_Q),
        in_specs=[pl.BlockSpec((ATTN_Q, LANES), lambda b, i: (i, b)),
                  pl.BlockSpec((kv_rows, LANES), lambda b, i: (0, b)),
                  pl.BlockSpec((kv_rows, LANES), lambda b, i: (0, nblk + b)),
                  pl.BlockSpec((hpb, ATTN_Q, ATTN_W), lambda b, i: (b, 0, 0))],
        out_specs=pl.BlockSpec((ATTN_Q, LANES), lambda b, i: (i, b)),
        compiler_params=_cp("arbitrary", "arbitrary"),
    )(p1, kvp, kvp, band)


def _attn_bwd(p1, kvp, band, dya, stages=None):
    s = p1.shape[0]
    aw = A_HEADS * A_HEAD_DIM
    nblk = aw // LANES
    hpb = LANES // A_HEAD_DIM
    scale = A_HEAD_DIM ** -0.5

    def body(q_ref, k_ref, v_ref, b_ref, do_ref, dq_ref, dk_ref, dv_ref, db_ref):
        @pl.when(pl.program_id(1) == 0)
        def _():
            dk_ref[...] = jnp.zeros_like(dk_ref)
            dv_ref[...] = jnp.zeros_like(dv_ref)
            db_ref[...] = jnp.zeros_like(db_ref)

        base = pl.multiple_of(pl.program_id(1) * ATTN_Q, ATTN_Q)
        window = pl.ds(base, ATTN_W)
        kw = k_ref[window, :]
        vw = v_ref[window, :]
        qv = q_ref[...]
        dov = do_ref[...]
        dq = jnp.zeros((ATTN_Q, LANES), F32)
        dk = jnp.zeros((ATTN_W, LANES), F32)
        dv = jnp.zeros((ATTN_W, LANES), F32)
        for h, mask in enumerate(_head_masks()):
            qh = jnp.where(mask, qv, jnp.zeros_like(qv))
            doh = jnp.where(mask, dov, jnp.zeros_like(dov))
            p = _attn_probs(qh, kw, b_ref[h], base - A_PAD)
            dp = _dot(doh, vw, "nt")
            ds = p * (dp - jnp.sum(p * dp, axis=-1, keepdims=True))
            db_ref[h] += ds
            dsb = (ds * scale).astype(BF16)
            dq = jnp.where(mask, _dot(dsb, kw), dq)
            dk = dk + _dot(dsb, qh, "tn")
            dv = dv + _dot(p.astype(BF16), doh, "tn")
        dq_ref[...] = dq.astype(BF16)
        dk_ref[window, :] += dk
        dv_ref[window, :] += dv

    kv_rows = s + A_PAD
    q_spec = pl.BlockSpec((ATTN_Q, LANES), lambda b, i: (i, b))
    acc_spec = pl.BlockSpec((kv_rows, LANES), lambda b, i: (0, b))
    b_spec = pl.BlockSpec((hpb, ATTN_Q, ATTN_W), lambda b, i: (b, 0, 0))
    return _pcall(
        body, stages, name="attn_bwd",
        out_shape=(jax.ShapeDtypeStruct((s, aw), BF16), jax.ShapeDtypeStruct((kv_rows, aw), F32),
                   jax.ShapeDtypeStruct((kv_rows, aw), F32), jax.ShapeDtypeStruct((A_HEADS, ATTN_Q, ATTN_W), F32)),
        grid=(nblk, s // ATTN_Q),
        in_specs=[q_spec, acc_spec, pl.BlockSpec((kv_rows, LANES), lambda b, i: (0, nblk + b)), b_spec, q_spec],
        out_specs=(q_spec, acc_spec, acc_spec, b_spec), compiler_params=_cp("arbitrary", "arbitrary"),
    )(p1, kvp, kvp, band, dya)


REL_TILE = CHUNK * A_BAND // 8


def _rel_onehot():
    qi = jnp.arange(CHUNK)[:, None]
    ks = jnp.arange(A_BAND)[None, :]
    idx = (jnp.clip(ks - A_PAD - qi, -REL_CLIP, CHUNK - 1) + REL_CLIP).reshape(1, CHUNK * A_BAND)
    return (jnp.arange(REL_SIZE)[:, None] == idx).astype(F32)


def _gla_gate(lr, wa2, balpha):
    z = _dot(lr, wa2) + balpha
    la = (jnp.minimum(z, 0.0) - jnp.log(1.0 + jnp.exp(-jnp.abs(z)))) * (1.0 / GATE_TAU)
    row = lax.broadcasted_iota(jnp.int32, (CHUNK, CHUNK), 0)
    col = lax.broadcasted_iota(jnp.int32, (CHUNK, CHUNK), 1)
    cum = _dot((row >= col).astype(F32), la, precision=HIGHEST)
    return z, la, cum


def _gla_dims(p2):
    kd = p2.shape[1] // 6
    hk = kd // B_HEADS
    hv = 2 * hk
    return kd, hk, hv


GLA_CPS = 8


def _gla_fwd(p2, lrp, wa2p, balpha, gnorm, stages=None):
    s = p2.shape[0]
    kd, hk, hv = _gla_dims(p2)
    nc = s // CHUNK
    cps = GLA_CPS if nc % GLA_CPS == 0 else 1
    rows_per = cps * CHUNK
    qscale = hk ** -0.5

    def body(p_ref, lr_ref, wa_ref, ba_ref, gn_ref, yb_ref, st_ref, state):
        @pl.when(pl.program_id(0) == 0)
        def _():
            state[...] = jnp.zeros_like(state)

        gn = gn_ref[...]
        for sub in range(cps):
            rows = slice(sub * CHUNK, (sub + 1) * CHUNK)
            _, _, cum = _gla_gate(lr_ref[rows, :], wa_ref[...], ba_ref[...])
            last = cum[CHUNK - 1:CHUNK, :]
            e = jnp.exp(last - cum)
            dch = jnp.exp(last)
            for hh in range(B_HEADS):
                ks = slice(hh * hk, (hh + 1) * hk)
                q = p_ref[rows, hh * hk:(hh + 1) * hk].astype(F32)
                k = p_ref[rows, kd + hh * hk:kd + (hh + 1) * hk].astype(F32)
                v = p_ref[rows, 2 * kd + hh * hv:2 * kd + (hh + 1) * hv]
                rg = p_ref[rows, 4 * kd + hh * hv:4 * kd + (hh + 1) * hv].astype(F32)
                kdec = (k * e[:, ks]).astype(BF16)
                st = state[hh] * dch[:, ks] + _dot(v, kdec, "tn")
                state[hh] = st
                st_ref[sub, hh] = st
                o = _dot((q * qscale).astype(BF16), st.astype(BF16), "nt")
                rinv = lax.rsqrt(jnp.mean(o * o, axis=-1, keepdims=True) + RMS_EPS)
                yb_ref[rows, hh * hv:(hh + 1) * hv] = ((o * rinv * gn) * (rg * _sigmoid(rg))).astype(BF16)

    return _pcall(
        body, stages, name="gla_fwd",
        out_shape=(jax.ShapeDtypeStruct((s, 2 * kd), BF16), jax.ShapeDtypeStruct((nc, B_HEADS, hv, hk), F32)),
        grid=(nc // cps,),
        in_specs=[pl.BlockSpec((rows_per, 6 * kd), lambda i: (i, 0)), pl.BlockSpec((rows_per, LANES), lambda i: (i, 0)),
                  pl.BlockSpec((LANES, kd), lambda i: (0, 0)), pl.BlockSpec((1, kd), lambda i: (0, 0)),
                  pl.BlockSpec((1, hv), lambda i: (0, 0))],
        out_specs=(pl.BlockSpec((rows_per, 2 * kd), lambda i: (i, 0)),
                   pl.BlockSpec((cps, B_HEADS, hv, hk), lambda i: (i, 0, 0, 0))),
        scratch_shapes=[pltpu.VMEM((B_HEADS, hv, hk), F32)], compiler_params=_cp("arbitrary"),
    )(p2, lrp, wa2p, balpha, gnorm)


GLA_ROW_DBALPHA, GLA_ROW_DGNORM = 0, 1


def _gla_bwd(p2, lrp, wa2p, balpha, gnorm, states, dyb, stages=None):
    s = p2.shape[0]
    kd, hk, hv = _gla_dims(p2)
    nc = s // CHUNK
    cps = GLA_CPS if nc % GLA_CPS == 0 else 1
    rows_per = cps * CHUNK
    nblk = nc // cps
    qscale = hk ** -0.5

    def body(p_ref, lr_ref, wa_ref, ba_ref, gn_ref, st_ref, sp_ref, dy_ref, dp_ref, dz_ref, sm_ref, gcar):
        i = pl.program_id(0)

        @pl.when(i == 0)
        def _():
            gcar[...] = jnp.zeros_like(gcar)
            sm_ref[...] = jnp.zeros_like(sm_ref)

        block_has_prev = (i < nblk - 1).astype(F32)
        gn = gn_ref[...]
        row = lax.broadcasted_iota(jnp.int32, (CHUNK, CHUNK), 0)
        col = lax.broadcasted_iota(jnp.int32, (CHUNK, CHUNK), 1)
        tri_strict = (row > col).astype(F32)
        for sub in reversed(range(cps)):
            rows = slice(sub * CHUNK, (sub + 1) * CHUNK)
            z, _, cum = _gla_gate(lr_ref[rows, :], wa_ref[...], ba_ref[...])
            last = cum[CHUNK - 1:CHUNK, :]
            e = jnp.exp(last - cum)
            dch = jnp.exp(last)
            sgn = _sigmoid(-z) * (1.0 / GATE_TAU)
            for hh in range(B_HEADS):
                ks = slice(hh * hk, (hh + 1) * hk)
                q = p_ref[rows, hh * hk:(hh + 1) * hk].astype(F32)
                k = p_ref[rows, kd + hh * hk:kd + (hh + 1) * hk].astype(F32)
                v = p_ref[rows, 2 * kd + hh * hv:2 * kd + (hh + 1) * hv]
                rg = p_ref[rows, 4 * kd + hh * hv:4 * kd + (hh + 1) * hv].astype(F32)
                kdecf = k * e[:, ks]
                kdec = kdecf.astype(BF16)
                st16 = st_ref[sub, hh].astype(BF16)
                prev = st_ref[sub - 1, hh] if sub > 0 else sp_ref[hh] * block_has_prev
                qs = (q * qscale).astype(BF16)
                o = _dot(qs, st16, "nt")
                rinv = lax.rsqrt(jnp.mean(o * o, axis=-1, keepdims=True) + RMS_EPS)
                dy = dy_ref[rows, hh * hv:(hh + 1) * hv].astype(F32)
                sg = _sigmoid(rg)
                onorm = o * rinv
                drg = dy * (onorm * gn) * (sg * (1.0 + rg * (1.0 - sg)))
                dob = dy * (rg * sg)
                sm_ref[GLA_ROW_DGNORM:GLA_ROW_DGNORM + 1, 0:hv] += _colsum(dob * onorm)
                t = dob * gn
                do = rinv * (t - onorm * jnp.mean(t * onorm, axis=-1, keepdims=True))
                do16 = do.astype(BF16)
                dq = _dot(do16, st16) * qscale
                gt = _dot(do16, qs, "tn") + gcar[hh]
                gcar[hh] = gt * dch[:, ks]
                dd = _colsum(gt * prev)
                gt16 = gt.astype(BF16)
                dkdec = _dot(v, gt16)
                dv = _dot(kdec, gt16, "nt")
                dla = dd * dch[:, ks] + _dot(tri_strict, dkdec * kdecf, precision=HIGHEST)
                dzh = dla * sgn[:, ks]
                sm_ref[GLA_ROW_DBALPHA:GLA_ROW_DBALPHA + 1, hh * hk:(hh + 1) * hk] += _colsum(dzh)
                dz_ref[rows, hh * hk:(hh + 1) * hk] = dzh.astype(BF16)
                dp_ref[rows, hh * hk:(hh + 1) * hk] = dq.astype(BF16)
                dp_ref[rows, kd + hh * hk:kd + (hh + 1) * hk] = (dkdec * e[:, ks]).astype(BF16)
                dp_ref[rows, 2 * kd + hh * hv:2 * kd + (hh + 1) * hv] = dv.astype(BF16)
                dp_ref[rows, 4 * kd + hh * hv:4 * kd + (hh + 1) * hv] = drg.astype(BF16)

    rev = lambda i: (nblk - 1 - i, 0)
    return _pcall(
        body, stages, name="gla_bwd",
        out_shape=(jax.ShapeDtypeStruct((s, 6 * kd), BF16), jax.ShapeDtypeStruct((s, kd), BF16),
                   jax.ShapeDtypeStruct((SUBLANES, kd), F32)),
        grid=(nblk,),
        in_specs=[pl.BlockSpec((rows_per, 6 * kd), rev), pl.BlockSpec((rows_per, LANES), rev),
                  pl.BlockSpec((LANES, kd), lambda i: (0, 0)), pl.BlockSpec((1, kd), lambda i: (0, 0)),
                  pl.BlockSpec((1, hv), lambda i: (0, 0)),
                  pl.BlockSpec((cps, B_HEADS, hv, hk), lambda i: (nblk - 1 - i, 0, 0, 0)),
                  pl.BlockSpec((None, B_HEADS, hv, hk), lambda i: (jnp.maximum((nblk - 1 - i) * cps - 1, 0), 0, 0, 0)),
                  pl.BlockSpec((rows_per, 2 * kd), rev)],
        out_specs=(pl.BlockSpec((rows_per, 6 * kd), rev), pl.BlockSpec((rows_per, kd), rev),
                   pl.BlockSpec((SUBLANES, kd), lambda i: (0, 0))),
        scratch_shapes=[pltpu.VMEM((B_HEADS, hv, hk), F32)], compiler_params=_cp("arbitrary"),
    )(p2, lrp, wa2p, balpha, gnorm, states, states, dyb)


def _merge_fwd(ya, yb, wpa, wpb, g):
    s, ka = ya.shape
    kb = yb.shape[1]
    d = wpa.shape[1]
    tm, tn = _tile(s, (1024, 512)), _tile(d, (512,))

    def body(ya_ref, yb_ref, wa_ref, wb_ref, g_ref, m_ref, pab_ref):
        yav, ybv = ya_ref[...], yb_ref[...]
        for cols in _strips(tn):
            pa = _dot(yav, wa_ref[:, cols])
            pb = _dot(ybv, wb_ref[:, cols])
            m_ref[:, cols] = (_sigmoid(g_ref[0, :, cols].astype(F32)) * pa
                              + _sigmoid(g_ref[1, :, cols].astype(F32)) * pb).astype(BF16)
            pab_ref[0, :, cols] = pa.astype(BF16)
            pab_ref[1, :, cols] = pb.astype(BF16)

    st = pl.BlockSpec((2, tm, tn), lambda i, j: (0, i, j))
    return pl.pallas_call(
        body, name="merge_fwd",
        out_shape=(jax.ShapeDtypeStruct((s, d), BF16), jax.ShapeDtypeStruct((2, s, d), BF16)),
        grid=(s // tm, d // tn),
        in_specs=[pl.BlockSpec((tm, ka), lambda i, j: (i, 0)), pl.BlockSpec((tm, kb), lambda i, j: (i, 0)),
                  pl.BlockSpec((ka, tn), lambda i, j: (0, j)), pl.BlockSpec((kb, tn), lambda i, j: (0, j)), st],
        out_specs=(pl.BlockSpec((tm, tn), lambda i, j: (i, j)), st),
        compiler_params=_cp("parallel", "parallel"),
    )(ya, yb, wpa, wpb, g)


def _merge_bwd(dm, wmo, g, pab, stages=None):
    s, d = dm.shape
    tm, tn = _tile(s, (1024, 512)), _tile(d, (512,))

    def body(dm_ref, w_ref, g_ref, pab_ref, dpab_ref, dg_ref):
        dmv = dm_ref[...]
        for cols in _strips(tn):
            dmg = _dot(dmv, w_ref[cols, :], "nt")
            for j in range(2):
                sg = _sigmoid(g_ref[j, :, cols].astype(F32))
                dpab_ref[j, :, cols] = (dmg * sg).astype(BF16)
                dg_ref[j, :, cols] = (dmg * pab_ref[j, :, cols].astype(F32) * (sg * (1.0 - sg))).astype(BF16)

    st = pl.BlockSpec((2, tm, tn), lambda i, j: (0, i, j))
    return _pcall(
        body, stages, name="merge_bwd",
        out_shape=(jax.ShapeDtypeStruct((2, s, d), BF16), jax.ShapeDtypeStruct((2, s, d), BF16)),
        grid=(s // tm, d // tn),
        in_specs=[pl.BlockSpec((tm, d), lambda i, j: (i, 0)), pl.BlockSpec((tn, d), lambda i, j: (j, 0)), st, st],
        out_specs=(st, st), compiler_params=_cp("arbitrary", "arbitrary"),
    )(dm, wmo, g, pab)


def _virtual_rows(parts, lo, hi):
    out, off = [], 0
    for p in parts:
        a, b = max(lo, off), min(hi, off + p.shape[0])
        if a < b:
            out.append(p[a - off:b - off])
        off += p.shape[0]
    return out[0] if len(out) == 1 else jnp.concatenate(out, axis=0)


def _mix_in_row_groups(d):
    o1 = 3 * A_HEADS * A_HEAD_DIM
    o2 = o1 + 6 * (d // 4)
    o3 = o2 + GATE_RANK
    return (0, o1), (o1, o2), (o2, o3), (o3, o3 + 2 * d)


def _split_mix_in(stacked):
    d = stacked.shape[2]
    flat = stacked.reshape(-1, d)
    _, _, (lo, hi), (glo, ghi) = _mix_in_row_groups(d)
    return flat, jnp.pad(flat[lo:hi], ((0, LANES - GATE_RANK), (0, 0))), flat[glo:ghi]


MIX_TILE = 1024


def _mix_in_weights(plan):
    return plan.memo("mix_in_weights", lambda: _split_mix_in(plan.weight("w_mix_in")))


def _hosted_mm(plan):
    return lambda name, *a, **k: plan.host(name, lambda st: _mm(name, *a, stages=st, **k))


def _mix_forward(u2, plan, small):
    s, d = u2.shape
    wt, wt_lr, wt_g = _mix_in_weights(plan)
    bias, wa2p, balpha, gnorm = small
    mm = _hosted_mm(plan)
    aw = A_HEADS * A_HEAD_DIM
    tm, tn = _tile(s, (2048, 1024)), MIX_TILE
    (_, na), (_, nab) = _mix_in_row_groups(d)[:2]
    assert na % tn == 0 and nab % tn == 0
    p1 = mm("mix_in_a", "nt", u2, wt, (s, na, d), tm=tm, tn=tn, tk=d, out_dtype=BF16)
    p2 = mm("mix_in_b", "nt", u2, wt, (s, nab - na, d), tm=tm, tn=tn, tk=d, out_dtype=BF16,
            b_spec=pl.BlockSpec((tn, d), lambda i, j, kk: (na // tn + j, 0)))
    lrp = mm("mix_in_lr", "nt", u2, wt_lr, (s, LANES, d), tm=tm, tn=LANES, tk=d, out_dtype=BF16)
    nbg = d // tn
    g = mm("mix_in_g", "nt", u2, wt_g, (s, 2 * d, d), tm=tm, tn=tn, tk=d, out_dtype=BF16, out_shape=(2, s, d),
           o_spec=pl.BlockSpec((None, tm, tn), lambda i, j, kk: (j // nbg, i, j % nbg)))
    kvp = jnp.pad(p1[:, aw:], ((A_PAD, 0), (0, 0)))
    ya = plan.host("attn_fwd", lambda st: _attn_fwd(p1, kvp, bias, st))
    yb, states = plan.host("gla_fwd", lambda st: _gla_fwd(p2, lrp, wa2p, balpha, gnorm, st))
    merged, pab = _merge_fwd(ya, yb, plan.weight("w_proj_a"), plan.weight("w_proj_b"), g)
    m = mm("mix_out", "nn", merged, plan.weight("w_mix_out"), (s, d, d), tm=tm, tn=tn, tk=d)
    return m, (p1, kvp, p2, lrp, states, ya, yb, g, pab, merged)


def _mix_backward(dm, u2, saved, plan, small):
    s, d = u2.shape
    wt, wt_lr, wt_g = _mix_in_weights(plan)
    wpa, wpb, wmo = plan.weight("w_proj_a"), plan.weight("w_proj_b"), plan.weight("w_mix_out")
    bias, wa2p, balpha, gnorm = small
    p1, kvp, p2, lrp, states, ya, yb, g, pab, merged = saved
    mm = _hosted_mm(plan)
    aw = A_HEADS * A_HEAD_DIM
    kd = d // 4
    t = MIX_TILE
    tm = _tile(s, (1024,))
    tks = _tile(s, LONG_K)

    plan.grad("w_mix_out", mm("mix_dw_out", "tn", merged, dm, (d, d, s), tm=t, tn=t, tk=tks))
    dpab, dg = plan.host("merge_bwd", lambda st: _merge_bwd(dm, wmo, g, pab, st))
    sel = lambda j: pl.BlockSpec((None, tm, d), lambda i, jj, kk: (j, i, 0))
    dya = mm("mix_dya", "nt", dpab, wpa, (s, aw, d), tm=tm, tn=t, tk=d, out_dtype=BF16, a_spec=sel(0))
    dyb = mm("mix_dyb", "nt", dpab, wpb, (s, 2 * kd, d), tm=tm, tn=t, tk=d, out_dtype=BF16, a_spec=sel(1))
    selk = lambda j: pl.BlockSpec((None, tks, t), lambda i, jj, kk: (j, kk, jj))
    plan.grad("w_proj_a", mm("mix_dwpa", "tn", ya, dpab, (aw, d, s), tm=t, tn=t, tk=tks, b_spec=selk(0)))
    plan.grad("w_proj_b", mm("mix_dwpb", "tn", yb, dpab, (2 * kd, d, s), tm=t, tn=t, tk=tks, b_spec=selk(1)))

    dq, dkp, dvp, dbias = plan.host("attn_bwd", lambda st: _attn_bwd(p1, kvp, bias, dya, st))
    dp1 = jnp.concatenate([dq, dkp[A_PAD:].astype(BF16), dvp[A_PAD:].astype(BF16)], axis=1)
    dp2, dz, gsm = plan.host("gla_bwd", lambda st: _gla_bwd(p2, lrp, wa2p, balpha, gnorm, states, dyb, st))
    dlrp = mm("gla_dlr", "nt", dz, wa2p, (s, LANES, kd), tm=tm, tn=LANES, tk=kd, out_dtype=BF16)
    dwa2p = mm("gla_dwa2", "tn", lrp, dz, (LANES, kd, s), tm=LANES, tn=kd, tk=tks)

    tka = 3 * aw
    assert 6 * kd == tka
    du = mm("mix_du_a", "nn", dp1, wt, (s, d, tka), tm=tm, tn=t, tk=tka)
    du = mm("mix_du_b", "nn", dp2, wt, (s, d, tka), tm=tm, tn=t, tk=tka, add=du,
            b_spec=pl.BlockSpec((tka, t), lambda i, j, kk: (1 + kk, j)))
    du = mm("mix_du_g", "nn", dg, wt_g, (s, d, 2 * d), tm=tm, tn=t, tk=d, add=du, thin=(dlrp, wt_lr),
            a_spec=pl.BlockSpec((None, tm, d), lambda i, j, kk: (kk, i, 0)))
    nkg = d // t
    dw1 = mm("mix_dw_a", "tn", dp1, u2, (3 * aw, d, s), tm=t, tn=t, tk=tks)
    dw2 = mm("mix_dw_b", "tn", dp2, u2, (6 * kd, d, s), tm=t, tn=t, tk=tks)
    dwlr = mm("mix_dw_lr", "tn", dlrp, u2, (LANES, d, s), tm=LANES, tn=t, tk=tks)
    dwg = mm("mix_dw_g", "tn", dg, u2, (2 * d, d, s), tm=t, tn=t, tk=tks,
             a_spec=pl.BlockSpec((None, tks, t), lambda i, j, kk: (i // nkg, kk, i % nkg)))
    pieces = [dw1, dw2, dwlr[:GATE_RANK], dwg]
    shard_rows = sum(p.shape[0] for p in pieces) // N_CHIPS
    plan.grad("w_mix_in", jnp.stack([_virtual_rows(pieces, j * shard_rows, (j + 1) * shard_rows)
                                     for j in range(N_CHIPS)]))
    return du, (dbias, dwa2p[:GATE_RANK], gsm)


def _device_step(x, target, mod, small, plan):
    s, d = x.shape
    row = lambda i: mod[i:i + 1]
    sh1, sc1, g1, sh2, sc2, g2, sh3, sc3, g3 = (row(i) for i in range(N_MOD))

    onehot = _rel_onehot()
    bias = _mm("rel_bias_expand", "nn", small["rel_bias"], onehot, (A_HEADS, CHUNK * A_BAND, REL_SIZE),
               tm=A_HEADS, tn=REL_TILE, tk=REL_SIZE, precision=HIGHEST).reshape(A_HEADS, CHUNK, A_BAND)
    bias = _band_bias(bias)
    wa2p = jnp.pad(small["w_alpha2"], ((0, LANES - GATE_RANK), (0, 0))).astype(BF16)
    mix_small = (bias, wa2p, small["b_alpha"], small["gla_norm_g"])

    u1, u1t = _modulate("mod1", x, sh1, sc1)
    f1, sv1 = _ffn_forward("ffn1", u1, plan)
    h1, u2 = _resid_ln_fwd("ln1_fwd", x, f1, g1, small["ln1_g"], small["ln1_b"], sh2, sc2, 0.5)
    m, svm = _mix_forward(u2, plan, mix_small)
    h2, u3, u3t = _resid_ln_fwd("ln2_fwd", h1, m, g2, small["ln2_g"], small["ln2_b"], sh3, sc3, 1.0, transposed=True)
    f2, sv2 = _ffn_forward("ffn2", u3, plan)

    dr3, df2, acc3 = _final_ln_loss_bwd("ln3_loss_bwd", h2, f2, target, g3, small["ln3_g"], small["ln3_b"], 0.5)
    du3 = _ffn_backward("ffn2", df2, u3t, sv2, plan, in_first=False)
    dr2, dmx, acc2 = _resid_ln_bwd("ln2_bwd", du3, dr3, h1, m, sc3, g2, small["ln2_g"], small["ln2_b"], 1.0)
    du2, (dbias, dwa2, gsm) = _mix_backward(dmx, u2, svm, plan, mix_small)
    dr1, df1, acc1 = _resid_ln_bwd("ln1_bwd", du2, dr2, x, f1, sc2, g1, small["ln1_g"], small["ln1_b"], 0.5)
    du1 = _ffn_backward("ffn1", df1, u1t, sv1, plan, in_first=True)
    grad_x, acc0 = _input_grad("input_grad", du1, dr1, x, sc1)

    drel = _hosted_mm(plan)("rel_bias_grad", "nt", _band_bias_grad(dbias).reshape(A_HEADS, CHUNK * A_BAND), onehot,
                            (A_HEADS, REL_SIZE, CHUNK * A_BAND), tm=A_HEADS, tn=REL_SIZE, tk=REL_TILE, precision=HIGHEST)
    loss = jnp.sum(acc3[ROW_LOSS])
    dmod = jnp.stack([acc0[ROW_DSH], acc0[ROW_DSC], acc1[ROW_DGATE], acc1[ROW_DSH], acc1[ROW_DSC], acc2[ROW_DGATE],
                      acc2[ROW_DSH], acc2[ROW_DSC], acc3[ROW_DGATE]])
    kd = d // 4
    small_grads = dict(ln1_g=acc1[ROW_DLN_G], ln1_b=acc1[ROW_DLN_B], ln2_g=acc2[ROW_DLN_G], ln2_b=acc2[ROW_DLN_B],
                       ln3_g=acc3[ROW_DLN_G], ln3_b=acc3[ROW_DLN_B], b_alpha=gsm[GLA_ROW_DBALPHA],
                       gla_norm_g=gsm[GLA_ROW_DGNORM, :kd // B_HEADS * 2], rel_bias=drel, w_alpha2=dwa2)
    return loss, grad_x, small_grads, dmod


HBM_SPEC = pl.BlockSpec(memory_space=pl.ANY)


def _mesh_pos():
    return lax.axis_index("x"), lax.axis_index("y"), lax.axis_index("c")


def _other_chips(x, y):
    return [(1 - x, y), (x, 1 - y), (1 - x, 1 - y)]


def _remote(src, dst, send_sem, recv_sem, to):
    return pltpu.make_async_remote_copy(src_ref=src, dst_ref=dst, send_sem=send_sem, recv_sem=recv_sem,
                                        device_id=to, device_id_type=MESH)


def _allgather_rows(name, v):
    m_per, n = v.shape

    def body(x_ref, out_ref, send_sems, recv_sems, local_sem):
        x, y, c = _mesh_pos()
        me, sibling = (x, y, c), (x, y, 1 - c)
        chips = _other_chips(x, y)

        def rows(px, py, pc):
            return out_ref.at[pl.ds((4 * px + 2 * py + pc) * m_per, m_per), :]

        def copy(k, block, to, src=None):
            return _remote(rows(*block) if src is None else src, rows(*block), send_sems.at[k], recv_sems.at[k], to)

        mine = pltpu.make_async_copy(x_ref, rows(*me), local_sem)
        mine.start()
        first = [copy(0, me, sibling, src=x_ref)]
        first += [copy(1 + j, me, (*chip, c), src=x_ref) for j, chip in enumerate(chips)]
        for cp in first:
            cp.start()
        passed = [copy(4 + j, (*chip, c), sibling) for j, chip in enumerate(chips)]
        for j, chip in enumerate(chips):
            copy(1 + j, (*chip, c), me).wait_recv()
            passed[j].start()
        copy(0, sibling, me).wait_recv()
        for j, chip in enumerate(chips):
            copy(4 + j, (*chip, 1 - c), me).wait_recv()
        for cp in first + passed:
            cp.wait_send()
        mine.wait()

    return pl.pallas_call(
        body, name=name, out_shape=jax.ShapeDtypeStruct((N_DEV * m_per, n), v.dtype),
        in_specs=[pl.BlockSpec(memory_space=pltpu.VMEM)], out_specs=pl.BlockSpec(memory_space=pltpu.VMEM),
        scratch_shapes=[pltpu.SemaphoreType.DMA((7,)), pltpu.SemaphoreType.DMA((7,)), pltpu.SemaphoreType.DMA],
    )(v)


def _allgather_weights(bufs):
    n = len(bufs)
    TO_X, TO_Y, PASS_TO_X, PASS_TO_Y, SIB_X, SIB_Y, SIB_D0, SIB_D1 = range(8)

    def body(*refs):
        ins, outs = refs[:n], refs[n:2 * n]
        send_sems, recv_sems = refs[2 * n:]
        x, y, c = _mesh_pos()
        sibling = (x, y, 1 - c)
        xn, yn, dg = _other_chips(x, y)
        j0, jx, jy, jd = (2 * p[0] + p[1] for p in ((x, y), xn, yn, dg))
        sends = []

        def rows(w, hc, quarter=None):
            hr = bufs[w].shape[1] // 2
            if quarter is None:
                return pl.ds(hc * hr, hr)
            return pl.ds(hc * hr + quarter * (hr // 2), hr // 2)

        def push(src, dst, w, k, to):
            cp = _remote(src, dst, send_sems.at[w, k], recv_sems.at[w, k], to)
            cp.start()
            sends.append(cp)

        def landed(piece, w, k):
            _remote(piece, piece, send_sems.at[w, k], recv_sems.at[w, k], sibling).wait_recv()

        for w in range(n):
            mine = rows(w, c)
            push(ins[w].at[j0, mine, :], outs[w].at[j0, mine, :], w, TO_X, (*xn, c))
            push(ins[w].at[j0, mine, :], outs[w].at[j0, mine, :], w, TO_Y, (*yn, c))
        for w in range(n):
            half_x = outs[w].at[jx, rows(w, c), :]
            landed(half_x, w, TO_X)
            quarter = outs[w].at[jx, rows(w, c, 1), :]
            push(quarter, quarter, w, PASS_TO_Y, (*yn, c))
            push(half_x, half_x, w, SIB_X, sibling)
            half_y = outs[w].at[jy, rows(w, c), :]
            landed(half_y, w, TO_Y)
            quarter = outs[w].at[jy, rows(w, c, 0), :]
            push(quarter, quarter, w, PASS_TO_X, (*xn, c))
            push(half_y, half_y, w, SIB_Y, sibling)
        for w in range(n):
            for q, arrives_on, on in ((0, PASS_TO_X, SIB_D0), (1, PASS_TO_Y, SIB_D1)):
                piece = outs[w].at[jd, rows(w, c, q), :]
                landed(piece, w, arrives_on)
                push(piece, piece, w, on, sibling)
        for w in range(n):
            landed(outs[w].at[jx, rows(w, 1 - c), :], w, SIB_X)
            landed(outs[w].at[jy, rows(w, 1 - c), :], w, SIB_Y)
            landed(outs[w].at[jd, rows(w, 1 - c, 0), :], w, SIB_D0)
            landed(outs[w].at[jd, rows(w, 1 - c, 1), :], w, SIB_D1)
        for cp in sends:
            cp.wait_send()

    return pl.pallas_call(
        body, name="allgather_weights", out_shape=[jax.ShapeDtypeStruct(b.shape, b.dtype) for b in bufs],
        in_specs=[HBM_SPEC] * n, out_specs=[HBM_SPEC] * n, input_output_aliases={w: w for w in range(n)},
        scratch_shapes=[pltpu.SemaphoreType.DMA((n, 8)), pltpu.SemaphoreType.DMA((n, 8))],
    )(*bufs)


def _half(ref, hc, col, *lead):
    rows, cols = ref.shape[-2:]
    if col:
        return ref.at[(*lead, slice(None), pl.ds(hc * (cols // 2), cols // 2))]
    return ref.at[(*lead, pl.ds(hc * (rows // 2), rows // 2), slice(None))]


def _half_shape(shape, col):
    return shape[:-2] + ((shape[-2], shape[-1] // 2) if col else (shape[-2] // 2, shape[-1]))


def _quarter(ref, hc, q, col, *lead):
    rows, cols = ref.shape[-2:]
    if col:
        return ref.at[(*lead, slice(None), pl.ds(hc * (cols // 2) + q * (cols // 4), cols // 4))]
    return ref.at[(*lead, pl.ds(hc * (rows // 2) + q * (rows // 4), rows // 4), slice(None))]


def _stage_gather_ici(bufs, cols):
    n = len(bufs)
    TO_X, TO_Y, PASS_TO_X, PASS_TO_Y = range(4)

    def places():
        x, y, c = _mesh_pos()
        xn, yn, dg = _other_chips(x, y)
        return c, (*xn, c), (*yn, c), [2 * p[0] + p[1] for p in ((x, y), xn, yn, dg)]

    def remote(src, dst, send, recv, w, k, to):
        return _remote(src, dst, send.at[4 * w + k], recv.at[4 * w + k], to)

    def own(ins, outs, send, recv):
        c, to_x, to_y, (j0, _, _, _) = places()
        for w in range(n):
            for k, to in ((TO_X, to_x), (TO_Y, to_y)):
                yield remote(_half(ins[w], c, cols[w], j0), _half(outs[w], c, cols[w], j0), send, recv, w, k, to)

    def relays(ins, outs, send, recv):
        c, to_x, to_y, (_, jx, jy, _) = places()
        for w in range(n):
            for j, k, q, pass_k, to in ((jx, TO_X, 1, PASS_TO_Y, to_y), (jy, TO_Y, 0, PASS_TO_X, to_x)):
                half = _half(outs[w], c, cols[w], j)
                piece = _quarter(outs[w], c, q, cols[w], j)
                yield remote(half, half, send, recv, w, k, to), remote(piece, piece, send, recv, w, pass_k, to)

    def passed(ins, outs, send, recv):
        c, to_x, _, (_, _, _, jd) = places()
        for w in range(n):
            for q, k in ((0, PASS_TO_X), (1, PASS_TO_Y)):
                piece = _quarter(outs[w], c, q, cols[w], jd)
                yield remote(piece, piece, send, recv, w, k, to_x)

    def start(*refs):
        for cp in own(*refs):
            cp.start()

    def relay(*refs):
        for arrived, onward in relays(*refs):
            arrived.wait_recv()
            onward.start()

    def finish(*refs):
        for cp in passed(*refs):
            cp.wait_recv()
        for cp in own(*refs):
            cp.wait_send()
        for _, onward in relays(*refs):
            onward.wait_send()

    outs = [jax.ShapeDtypeStruct(b.shape, b.dtype) for b in bufs]
    return _Stage(bufs, outs, 4 * n, start, finish, aliases={w: w for w in range(n)}, relay=relay)


def _stage_gather_whole(bufs):
    n = len(bufs)
    TO_X, TO_Y, PASS_TO_X, PASS_TO_Y, SIB_X, SIB_Y, SIB_D0, SIB_D1 = range(8)

    def places():
        x, y, c = _mesh_pos()
        xn, yn, dg = _other_chips(x, y)
        return c, (x, y, 1 - c), (*xn, c), (*yn, c), [2 * p[0] + p[1] for p in ((x, y), xn, yn, dg)]

    def cp(src, dst, send, recv, w, k, to):
        return _remote(src, dst, send.at[8 * w + k], recv.at[8 * w + k], to)

    def start(ins, outs, send, recv):
        c, _, to_x, to_y, (j0, _, _, _) = places()
        for w in range(n):
            for k, to in ((TO_X, to_x), (TO_Y, to_y)):
                cp(_half(ins[w], c, False, j0), _half(outs[w], c, False, j0), send, recv, w, k, to).start()

    def relay(ins, outs, send, recv):
        c, sib, to_x, to_y, (_, jx, jy, _) = places()
        for w in range(n):
            for j, k, q, pass_k, to, sib_k in ((jx, TO_X, 1, PASS_TO_Y, to_y, SIB_X), (jy, TO_Y, 0, PASS_TO_X, to_x, SIB_Y)):
                half = _half(outs[w], c, False, j)
                cp(half, half, send, recv, w, k, to).wait_recv()
                piece = _quarter(outs[w], c, q, False, j)
                cp(piece, piece, send, recv, w, pass_k, to).start()
                cp(half, half, send, recv, w, sib_k, sib).start()

    def finish(ins, outs, send, recv):
        c, sib, to_x, to_y, (j0, jx, jy, jd) = places()
        for w in range(n):
            for q, k_in, k_out in ((0, PASS_TO_X, SIB_D0), (1, PASS_TO_Y, SIB_D1)):
                piece = _quarter(outs[w], c, q, False, jd)
                cp(piece, piece, send, recv, w, k_in, sib).wait_recv()
                cp(piece, piece, send, recv, w, k_out, sib).start()
        for w in range(n):
            for j, k in ((jx, SIB_X), (jy, SIB_Y)):
                got = _half(outs[w], 1 - c, False, j)
                cp(got, got, send, recv, w, k, sib).wait_recv()
            for q, k in ((0, SIB_D0), (1, SIB_D1)):
                got = _quarter(outs[w], 1 - c, q, False, jd)
                cp(got, got, send, recv, w, k, sib).wait_recv()
        for w in range(n):
            half = _half(outs[w], c, False, j0)
            quarter = _quarter(outs[w], c, 0, False, j0)
            for k, ref in ((TO_X, half), (TO_Y, half), (SIB_X, half), (SIB_Y, half), (PASS_TO_X, quarter),
                           (PASS_TO_Y, quarter), (SIB_D0, quarter), (SIB_D1, quarter)):
                cp(ref, ref, send, recv, w, k, sib).wait_send()

    outs = [jax.ShapeDtypeStruct(b.shape, b.dtype) for b in bufs]
    return _Stage(bufs, outs, 8 * n, start, finish, aliases={w: w for w in range(n)}, relay=relay)


def _stage_gather_d2d(partial, cols):
    n = len(partial)

    def copies(ins, outs, send, recv):
        x, y, c = _mesh_pos()
        for w in range(n):
            for r, chip in enumerate(_other_chips(x, y)):
                jr = 2 * chip[0] + chip[1]
                mine = _remote(_half(ins[w], c, cols[w], jr), _half(outs[w], c, cols[w], jr), send.at[3 * w + r],
                               recv.at[3 * w + r], (x, y, 1 - c))
                got = _half(outs[w], 1 - c, cols[w], jr)
                yield mine, _remote(got, got, send.at[3 * w + r], recv.at[3 * w + r], (x, y, 1 - c))

    def start(*refs):
        for mine, _ in copies(*refs):
            mine.start()

    def finish(*refs):
        pairs = list(copies(*refs))
        for _, theirs in pairs:
            theirs.wait_recv()
        for mine, _ in pairs:
            mine.wait_send()

    outs = [jax.ShapeDtypeStruct(p.shape, p.dtype) for p in partial]
    return _Stage(partial, outs, 3 * n, start, finish, aliases={w: w for w in range(n)})


def _stage_exchange_halves(grads, cols):
    n = len(grads)

    def copies(ins, outs, send, recv):
        x, y, c = _mesh_pos()
        for w in range(n):
            yield _remote(_half(ins[w], 1 - c, cols[w], slice(None)), outs[w], send.at[w], recv.at[w], (x, y, 1 - c))

    def start(*refs):
        for cp in copies(*refs):
            cp.start()

    def finish(*refs):
        cps = list(copies(*refs))
        for cp in cps:
            cp.wait_recv()
        for cp in cps:
            cp.wait_send()

    outs = [jax.ShapeDtypeStruct(_half_shape(g.shape, col), g.dtype) for g, col in zip(grads, cols)]
    return _Stage(grads, outs, n, start, finish)


def _stage_scatter(parts):
    n = len(parts)

    def copies(ins, outs, send, recv):
        x, y, c = _mesh_pos()
        for w in range(n):
            for r, chip in enumerate(_other_chips(x, y)):
                jr = 2 * chip[0] + chip[1]
                yield _remote(ins[w].at[jr], outs[w].at[r], send.at[3 * w + r], recv.at[3 * w + r], (*chip, c))

    def start(*refs):
        for cp in copies(*refs):
            cp.start()

    def finish(*refs):
        cps = list(copies(*refs))
        for cp in cps:
            cp.wait_recv()
        for cp in cps:
            cp.wait_send()

    outs = [jax.ShapeDtypeStruct((3,) + p.shape[1:], p.dtype) for p in parts]
    return _Stage(parts, outs, 3 * n, start, finish)


def _stage_share(fulls, cols):
    n = len(fulls)

    def copies(ins, outs, send, recv):
        x, y, c = _mesh_pos()
        for w in range(n):
            theirs = _half(outs[w], 1 - c, cols[w])
            yield (_remote(_half(ins[w], c, cols[w]), _half(outs[w], c, cols[w]), send.at[w], recv.at[w], (x, y, 1 - c)),
                   _remote(theirs, theirs, send.at[w], recv.at[w], (x, y, 1 - c)))

    def start(*refs):
        for mine, _ in copies(*refs):
            mine.start()

    def finish(*refs):
        pairs = list(copies(*refs))
        for _, theirs in pairs:
            theirs.wait_recv()
        for mine, _ in pairs:
            mine.wait_send()

    outs = [jax.ShapeDtypeStruct(h.shape, h.dtype) for h in fulls]
    return _Stage(fulls, outs, n, start, finish, aliases={w: w for w in range(n)})


def _run_stages(name, stages):
    return _pcall(None, stages, name=name, out_shape=[], in_specs=[], out_specs=[])()[1]


TILE_BYTES = 2 * 1024 * 1024
SUM_TILE_BYTES = 4 * 1024 * 1024


def _row_tile(rows, cols, itemsize=4, tile_bytes=TILE_BYTES):
    for t in range(min(rows, tile_bytes // (cols * itemsize)) // SUBLANES * SUBLANES, 0, -SUBLANES):
        if rows % t == 0:
            return t
    return rows


def _col_tile(rows, cols, itemsize=4, tile_bytes=TILE_BYTES):
    for t in (2048, 1024, 512, 256, 128):
        if cols % t == 0 and t * rows * itemsize <= tile_bytes:
            return t
    return cols


def _tiling(rows, cols, col, tile_bytes=TILE_BYTES):
    if col:
        tc = _col_tile(rows, cols, tile_bytes=tile_bytes)
        return (rows, tc), cols // tc
    tr = _row_tile(rows, cols, tile_bytes=tile_bytes)
    return (tr, cols), rows // tr


def _strip(col, i):
    return (0, i) if col else (i, 0)


def _pair_sum(name, g, recv, core, col):
    blk, nb = _tiling(*recv.shape[1:], col, tile_bytes=SUM_TILE_BYTES)

    def body(c_ref, g_ref, r_ref, o_ref):
        o_ref[...] = (g_ref[...] + r_ref[...]).astype(BF16)

    grid_spec = pltpu.PrefetchScalarGridSpec(
        num_scalar_prefetch=1, grid=(N_CHIPS, nb),
        in_specs=[pl.BlockSpec((None,) + blk, lambda j, i, cr: (j,) + _strip(col, cr[0] * nb + i)),
                  pl.BlockSpec((None,) + blk, lambda j, i, cr: (j,) + _strip(col, i))],
        out_specs=pl.BlockSpec((None,) + blk, lambda j, i, cr: (j,) + _strip(col, i)))
    return pl.pallas_call(body, name=name, out_shape=jax.ShapeDtypeStruct(recv.shape, BF16), grid_spec=grid_spec,
                          compiler_params=_cp("parallel", "parallel"))(core, g, recv)


def _quad_sum(name, own, landed, chip_core, col):
    rows, cols = landed.shape[1:]
    blk, nb = _tiling(rows, cols, col, tile_bytes=SUM_TILE_BYTES)
    full = (rows, 2 * cols) if col else (2 * rows, cols)

    def body(cc_ref, own_ref, l_ref, o_ref):
        o_ref[...] = ((own_ref[...].astype(F32) + l_ref[0].astype(F32)) + l_ref[1].astype(F32)) + l_ref[2].astype(F32)

    grid_spec = pltpu.PrefetchScalarGridSpec(
        num_scalar_prefetch=1, grid=(nb,),
        in_specs=[pl.BlockSpec((None,) + blk, lambda i, cc: (cc[0],) + _strip(col, i)),
                  pl.BlockSpec((3,) + blk, lambda i, cc: (0,) + _strip(col, i))],
        out_specs=pl.BlockSpec(blk, lambda i, cc: _strip(col, cc[1] * nb + i)))
    return pl.pallas_call(body, name=name, out_shape=jax.ShapeDtypeStruct(full, F32), grid_spec=grid_spec,
                          compiler_params=_cp("arbitrary"))(chip_core, own, landed)


def _device_sum(name, gathered):
    def body(g_ref, o_ref):
        total = g_ref[0]
        for k in range(1, N_DEV):
            total = total + g_ref[k]
        o_ref[...] = total

    return pl.pallas_call(body, name=name, out_shape=jax.ShapeDtypeStruct(gathered.shape[1:], F32))(gathered)


def _adamw(name, w, g, m, v):
    rows, cols = w.shape
    col = rows % SUBLANES != 0
    blk, nb = _tiling(rows, cols, col)
    bc1 = 1.0 - ADAM_B1 ** ADAM_STEP
    bc2 = 1.0 - ADAM_B2 ** ADAM_STEP

    def body(w_ref, g_ref, m_ref, v_ref, d_ref, mo_ref, vo_ref):
        gv = g_ref[...]
        mn = ADAM_B1 * m_ref[...] + (1.0 - ADAM_B1) * gv
        vn = ADAM_B2 * v_ref[...] + (1.0 - ADAM_B2) * (gv * gv)
        mo_ref[...] = mn
        vo_ref[...] = vn
        d_ref[...] = -ADAM_LR * ((mn / bc1) / (jnp.sqrt(vn / bc2) + ADAM_EPS) + ADAM_WD * w_ref[...])

    spec = pl.BlockSpec(blk, lambda i: _strip(col, i))
    return pl.pallas_call(
        body, name=name, out_shape=[jax.ShapeDtypeStruct((rows, cols), F32)] * 3, grid=(nb,),
        in_specs=[spec] * 4, out_specs=[spec] * 3, compiler_params=_cp("parallel"),
    )(w, g, m, v)


WEIGHTS = ["w_ada", "b_ada", "ffn1_w_in", "ffn1_w_out", "ln1_g", "ln1_b", "w_mix_in", "rel_bias", "w_alpha2",
           "b_alpha", "gla_norm_g", "w_proj_a", "w_proj_b", "w_mix_out", "ln2_g", "ln2_b", "ffn2_w_in", "ffn2_w_out",
           "ln3_g", "ln3_b"]
BIG = {"ffn1_w_in": True, "ffn1_w_out": False, "w_mix_in": False, "w_proj_a": True, "w_proj_b": True,
       "w_mix_out": False, "ffn2_w_in": True, "ffn2_w_out": False}
TRANSPOSED = ("w_mix_in",)
STACKED = ("ffn1_w_in", "ffn2_w_in", "w_mix_in")
GROUP_FFN1 = ("ffn1_w_in", "ffn1_w_out")
GROUP_PROJ = ("w_proj_a", "w_proj_b", "w_mix_out")
SMALL = ["ln1_g", "ln1_b", "ln2_g", "ln2_b", "ln3_g", "ln3_b", "b_alpha", "gla_norm_g", "rel_bias", "w_alpha2"]


def _pad_rows(vec, rows=SUBLANES):
    per = -(-vec.shape[0] // (rows * LANES)) * LANES
    return jnp.pad(vec, (0, rows * per - vec.shape[0])).reshape(rows, per)


def _silu(v):
    return v * _sigmoid(v)


class _MeshPlan:
    def __init__(self, shards, chip, core):
        self.shapes = {k: v.shape for k, v in shards.items()}
        self.slots = {k: lax.dynamic_update_slice(lax.empty((N_CHIPS,) + v.shape, v.dtype), v[None], (chip, 0, 0))
                      for k, v in shards.items()}
        self.core1 = core.astype(jnp.int32).reshape(1)
        self.chip_core = jnp.stack([chip, core]).astype(jnp.int32)
        self.partial, self.full, self.local, self.pair, self.half, self.final, self.memos = {}, {}, {}, {}, {}, {}, {}
        ici, d2d, x1, x2, x3 = self.gather_ici, self.gather_d2d, self.exchange, self.scatter, self.share
        mix_in, in1, out1, in2, out2 = ("w_mix_in",), ("ffn1_w_in",), ("ffn1_w_out",), ("ffn2_w_in",), ("ffn2_w_out",)
        self.schedule = {
            "ffn1_in_fwd": [self.gather_whole(out1), ici(mix_in)], "ffn1_out_fwd": [d2d(mix_in), ici(out2)],
            "mix_in_g": [ici(GROUP_PROJ), d2d(out2)],
            "attn_fwd": [ici(in2), d2d(GROUP_PROJ)], "gla_fwd": [d2d(in2)],
            "ffn2_dw_in": [x1(out2)], "ffn2_du": [x2(out2), x1(in2)],
            "attn_bwd": [x2(in2), x3(out2)], "gla_bwd": [x3(in2), x1(GROUP_PROJ)],
            "mix_du_g": [x2(GROUP_PROJ)],
            "ffn1_out_bwd": [x1(mix_in), x3(GROUP_PROJ)], "ffn1_dw_in": [x2(mix_in)], "ffn1_dw_out": [x3(mix_in), x1(in1)],
            "ffn1_du": [x2(in1), x1(out1)], "rel_bias_grad": [x2(out1), x3(in1)],
        }

    def weight(self, k):
        return self.full[k]

    def grad(self, k, g):
        r, cc = self.shapes[k]
        if k not in STACKED:
            g = g.reshape(r, N_CHIPS, cc).transpose(1, 0, 2) if BIG[k] else g.reshape(N_CHIPS, r, cc)
        self.local[k] = g

    def memo(self, key, make):
        if key not in self.memos:
            self.memos[key] = make()
        return self.memos[key]

    def host(self, name, call):
        builders = self.schedule.get(name)
        if not builders:
            return call(None)
        built = [b() for b in builders]
        main, comm = call([st for st, _ in built])
        for (_, post), res in zip(built, comm):
            post(res)
        return main

    def run(self, name, builders):
        built = [b() for b in builders]
        for (_, post), res in zip(built, _run_stages(name, [st for st, _ in built])):
            post(res)

    def set_gathered(self, names, gathered):
        for k, g in zip(names, gathered):
            _, r, cc = g.shape
            if k not in STACKED:
                g = g.transpose(1, 0, 2).reshape(r, N_CHIPS * cc) if BIG[k] else g.reshape(N_CHIPS * r, cc)
            self.full[k] = g

    @staticmethod
    def cols(names):
        return [k in TRANSPOSED for k in names]

    def gather_ici(self, names):
        def post(res):
            self.partial.update(zip(names, res))
        return lambda: (_stage_gather_ici([self.slots[k] for k in names], self.cols(names)), post)

    def gather_whole(self, names):
        return lambda: (_stage_gather_whole([self.slots[k] for k in names]), lambda res: self.set_gathered(names, res))

    def gather_d2d(self, names):
        return lambda: (_stage_gather_d2d([self.partial[k] for k in names], self.cols(names)),
                        lambda res: self.set_gathered(names, res))

    def exchange(self, names):
        def post(res):
            for k, r in zip(names, res):
                self.pair[k] = _pair_sum(f"pair_sum_{k}", self.local[k], r, self.core1, k in TRANSPOSED)
        return lambda: (_stage_exchange_halves([self.local[k] for k in names], self.cols(names)), post)

    def scatter(self, names):
        def post(res):
            for k, landed in zip(names, res):
                self.half[k] = _quad_sum(f"quad_sum_{k}", self.pair[k], landed, self.chip_core, k in TRANSPOSED)
        return lambda: (_stage_scatter([self.pair[k] for k in names]), post)

    def share(self, names):
        def post(res):
            self.final.update(zip(names, res))
        return lambda: (_stage_share([self.half[k] for k in names], self.cols(names)), post)


def _step(args):
    x_pos, y_pos, c_pos = _mesh_pos()
    chip = 2 * x_pos + y_pos
    dev = 4 * x_pos + 2 * y_pos + c_pos
    take = lambda name, k: args[name][0].T if k in TRANSPOSED else args[name][0]
    w = {k: take(k, k) for k in WEIGHTS}
    mom = {k: take("m_" + k, k) for k in WEIGHTS}
    vel = {k: take("v_" + k, k) for k in WEIGHTS}
    x = args["x"][0]
    target = args["loss_target"][0]
    s, d = x.shape
    kd = d // 4
    rel_sh = w["rel_bias"].shape[1]
    wa2_sh = w["w_alpha2"].shape[1]
    ada_sh = w["w_ada"].shape[1]

    n_rel, n_wa2 = A_HEADS * rel_sh, GATE_RANK * wa2_sh
    packed = _pad_rows(jnp.concatenate([args["c"].reshape(-1), w["rel_bias"].reshape(-1), w["w_alpha2"].reshape(-1)]))
    got = _allgather_rows("gather_small_inputs", packed).reshape(N_DEV, -1)
    c_all = got[:, :d]
    per_chip = got[0::2]
    rel_bias = per_chip[:, d:d + n_rel].reshape(N_CHIPS, A_HEADS, rel_sh).transpose(1, 0, 2).reshape(A_HEADS, -1)
    w_alpha2 = per_chip[:, d + n_rel:d + n_rel + n_wa2].reshape(N_CHIPS, GATE_RANK, wa2_sh).transpose(1, 0, 2)
    w_alpha2 = w_alpha2.reshape(GATE_RANK, -1)

    b_shard = lax.dynamic_slice(w["b_ada"], (chip * ada_sh,), (ada_sh,))
    mod_shard = _mm("ada_fwd", "nn", c_all, w["w_ada"], (N_DEV, ada_sh, d), tm=N_DEV, tn=_tile(ada_sh, (512, 128)),
                    tk=d, precision=HIGHEST, a_fn=_silu, add=jnp.broadcast_to(b_shard[None], (N_DEV, ada_sh)))
    mod_all = _allgather_rows("gather_mod", mod_shard).reshape(N_DEV, N_DEV, ada_sh)[0::2]
    mod_all = mod_all.transpose(1, 0, 2).reshape(N_DEV, N_MOD * d)
    mod = lax.dynamic_index_in_dim(mod_all, dev, 0, keepdims=False).reshape(N_MOD, d)

    names = list(BIG)
    plan = _MeshPlan({k: w[k].astype(BF16) for k in names}, chip, c_pos)
    plan.set_gathered(GROUP_FFN1[:1], _allgather_weights([plan.slots[k] for k in GROUP_FFN1[:1]]))

    small = dict(rel_bias=rel_bias, w_alpha2=w_alpha2, b_alpha=w["b_alpha"][None], gla_norm_g=w["gla_norm_g"][None])
    for k in ("ln1_g", "ln1_b", "ln2_g", "ln2_b", "ln3_g", "ln3_b"):
        small[k] = w[k][None]
    loss_local, grad_x, small_grads, dmod = _device_step(x, target, mod, small, plan)
    loss = lax.psum(loss_local, ("x", "y", "c"))
    plan.run("grad_tail_share", [plan.share(GROUP_FFN1[1:])])

    flat = jnp.concatenate([small_grads[k].reshape(-1) for k in SMALL] + [dmod.reshape(-1)])
    n_small = flat.shape[0] - N_MOD * d
    packed = _pad_rows(flat)
    all_small = _allgather_rows("gather_small_grads", packed).reshape(N_DEV, SUBLANES, -1)
    summed = _device_sum("small_grad_sum", all_small).reshape(-1)
    dmod_all = all_small.reshape(N_DEV, -1)[:, n_small:n_small + N_MOD * d]
    dmod_shard = lax.dynamic_slice(dmod_all, (0, chip * ada_sh), (N_DEV, ada_sh))
    grads = {"b_ada": summed[n_small:n_small + N_MOD * d]}
    off = 0
    for k in SMALL:
        size = small_grads[k].size
        grads[k] = summed[off:off + size].reshape(small_grads[k].shape)
        off += size
    grads["rel_bias"] = lax.dynamic_slice(grads["rel_bias"], (0, chip * rel_sh), (A_HEADS, rel_sh))
    grads["w_alpha2"] = lax.dynamic_slice(grads["w_alpha2"], (0, chip * wa2_sh), (GATE_RANK, wa2_sh))
    grads["w_ada"] = _mm("ada_bwd", "nn", jnp.pad(c_all.T, ((0, 0), (0, LANES - N_DEV))),
                         jnp.pad(dmod_shard, ((0, LANES - N_DEV), (0, 0))), (d, ada_sh, LANES), tm=_tile(d, (1024,)),
                         tn=_tile(ada_sh, (512, 128)), tk=LANES, precision=HIGHEST, a_fn=_silu)

    grads.update(plan.final)

    delta, new_m, new_v = {}, {}, {}
    for k in ["w_ada"] + names:
        delta[k], new_m[k], new_v[k] = _adamw(f"adamw_{k}", w[k], grads[k], mom[k], vel[k])
    tiny = ["b_ada"] + SMALL
    pack = lambda src: _pad_rows(jnp.concatenate([src[k].reshape(-1) for k in tiny]), rows=1).reshape(-1, LANES)
    outs = _adamw("adamw_small", pack(w), pack(grads), pack(mom), pack(vel))
    off = 0
    for k in tiny:
        size = w[k].size
        for dst, src in zip((delta, new_m, new_v), outs):
            dst[k] = src.reshape(-1)[off:off + size].reshape(w[k].shape)
        off += size

    give = lambda src: [src[k].T[None] if k in TRANSPOSED else src[k][None] for k in WEIGHTS]
    return (loss, grad_x[None], *give(grads), *give(delta), *give(new_m), *give(new_v))


def kernel(x, c, w_ada, b_ada, ffn1_w_in, ffn1_w_out, ln1_g, ln1_b, w_mix_in, rel_bias, w_alpha2, b_alpha, gla_norm_g, w_proj_a, w_proj_b, w_mix_out, ln2_g, ln2_b, ffn2_w_in, ffn2_w_out, ln3_g, ln3_b, loss_target, m_w_ada, m_b_ada, m_ffn1_w_in, m_ffn1_w_out, m_ln1_g, m_ln1_b, m_w_mix_in, m_rel_bias, m_w_alpha2, m_b_alpha, m_gla_norm_g, m_w_proj_a, m_w_proj_b, m_w_mix_out, m_ln2_g, m_ln2_b, m_ffn2_w_in, m_ffn2_w_out, m_ln3_g, m_ln3_b, v_w_ada, v_b_ada, v_ffn1_w_in, v_ffn1_w_out, v_ln1_g, v_ln1_b, v_w_mix_in, v_rel_bias, v_w_alpha2, v_b_alpha, v_gla_norm_g, v_w_proj_a, v_w_proj_b, v_w_mix_out, v_ln2_g, v_ln2_b, v_ffn2_w_in, v_ffn2_w_out, v_ln3_g, v_ln3_b):
    return _step(dict(locals()))
```
